```python
import jax, jax.numpy as jnp
from jax import lax
import numpy as np

D_MODEL = 2048
BATCH = 8
SEQ = 4096
DEPTH = 1

N_META = 16
D_LRU = D_MODEL // 2
N_LRU_HEADS = 16
LRU_BLOCK = D_LRU // N_LRU_HEADS
LRU_CONV_WIDTH = 4
LRU_C = 8.0
D_SCONV = D_MODEL - D_LRU
N_SCONV_GROUPS = 16
SCONV_BLOCK = D_SCONV // N_SCONV_GROUPS
SCONV_WIDTH = 3
D_FF = ((8 * D_MODEL // 3 + 127) // 128) * 128
IN_COLS = 2 * D_LRU + 3 * D_SCONV
EPS = 1e-6

kernel_name = "hymba_lru_shortconv_macaron"


def rmsnorm(x, g):
    xf = x.astype(jnp.float32)
    y = xf * lax.rsqrt(jnp.mean(xf * xf, axis=-1, keepdims=True) + EPS)
    return (y * g.astype(jnp.float32)).astype(x.dtype)


def group_rmsnorm(x, g, n_groups):
    b, t, c = x.shape
    xf = x.astype(jnp.float32).reshape(b, t, n_groups, c // n_groups)
    y = xf * lax.rsqrt(jnp.mean(xf * xf, axis=-1, keepdims=True) + EPS)
    return (y.reshape(b, t, c) * g.astype(jnp.float32)).astype(x.dtype)


def causal_depthwise_conv(x, w):
    k = w.shape[0]
    return lax.conv_general_dilated(
        x, w[:, None, :].astype(x.dtype), window_strides=(1,),
        padding=[(k - 1, 0)], dimension_numbers=("NWC", "WIO", "NWC"),
        feature_group_count=x.shape[-1])


def swiglu(x, w_gate, w_up, w_down):
    return (jax.nn.silu(x @ w_gate) * (x @ w_up)) @ w_down


def rg_lru(x, w_a, b_a, w_x, b_x, lam):
    bn, t, c = x.shape
    xh = x.reshape(bn, t, N_LRU_HEADS, LRU_BLOCK)
    gate_a = jax.nn.sigmoid(jnp.einsum("bthi,hij->bthj", xh, w_a).reshape(bn, t, c) + b_a)
    gate_x = jax.nn.sigmoid(jnp.einsum("bthi,hij->bthj", xh, w_x).reshape(bn, t, c) + b_x)
    log_a = -LRU_C * gate_a.astype(jnp.float32) * jax.nn.softplus(-lam.astype(jnp.float32))
    a = jnp.exp(log_a)
    mult = jnp.sqrt(-jnp.expm1(2.0 * log_a))
    u = mult * (gate_x * x).astype(jnp.float32)

    def combine(left, right):
        a_l, b_l = left
        a_r, b_r = right
        return a_r * a_l, a_r * b_l + b_r

    _, h = lax.associative_scan(combine, (a, u), axis=1)
    return h.astype(x.dtype)


def _fwd_setup_inputs(seed: int = 0) -> dict:
    key = jax.random.key(seed)
    ks = iter(jax.random.split(key, 40))
    f32 = jnp.float32
    L = DEPTH

    def nrm(shape, fan_in):
        return jax.random.normal(next(ks), shape, f32) * (fan_in ** -0.5)

    def gain(shape):
        return 1.0 + 0.02 * jax.random.normal(next(ks), shape, f32)

    def bias(shape):
        return 0.01 * jax.random.normal(next(ks), shape, f32)

    x = jax.random.normal(next(ks), (BATCH, SEQ, D_MODEL), f32)
    meta_tokens = jax.random.normal(next(ks), (N_META, D_MODEL), f32)

    a_c = jax.random.uniform(next(ks), (L, D_LRU), f32, 0.9, 0.999)
    s = a_c ** (1.0 / LRU_C)
    lru_lambda = jnp.log(s) - jnp.log1p(-s)

    return {
        "x": x,
        "meta_tokens": meta_tokens,
        "ffn1_pre_g": gain((L, D_MODEL)),
        "ffn1_w_gate": nrm((L, D_MODEL, D_FF), D_MODEL),
        "ffn1_w_up": nrm((L, D_MODEL, D_FF), D_MODEL),
        "ffn1_w_down": nrm((L, D_FF, D_MODEL), D_FF),
        "ffn1_post_g": gain((L, D_MODEL)),
        "mix_pre_g": gain((L, D_MODEL)),
        "w_in": nrm((L, D_MODEL, IN_COLS), D_MODEL),
        "lru_conv_w": nrm((L, LRU_CONV_WIDTH, D_LRU), LRU_CONV_WIDTH),
        "lru_conv_b": bias((L, D_LRU)),
        "lru_w_a": nrm((L, N_LRU_HEADS, LRU_BLOCK, LRU_BLOCK), LRU_BLOCK),
        "lru_b_a": bias((L, D_LRU)),
        "lru_w_x": nrm((L, N_LRU_HEADS, LRU_BLOCK, LRU_BLOCK), LRU_BLOCK),
        "lru_b_x": bias((L, D_LRU)),
        "lru_lambda": lru_lambda,
        "sconv_w": nrm((L, SCONV_WIDTH, D_SCONV), SCONV_WIDTH),
        "lru_out_g": gain((L, D_LRU)),
        "sconv_out_g": gain((L, D_SCONV)),
        "w_out": nrm((L, D_MODEL, D_MODEL), D_MODEL),
        "mix_post_g": gain((L, D_MODEL)),
        "ffn2_pre_g": gain((L, D_MODEL)),
        "ffn2_w_gate": nrm((L, D_MODEL, D_FF), D_MODEL),
        "ffn2_w_up": nrm((L, D_MODEL, D_FF), D_MODEL),
        "ffn2_w_down": nrm((L, D_FF, D_MODEL), D_FF),
        "ffn2_post_g": gain((L, D_MODEL)),
    }


def _fwd_reference(x, meta_tokens, ffn1_pre_g, ffn1_w_gate, ffn1_w_up, ffn1_w_down, ffn1_post_g,
              mix_pre_g, w_in, lru_conv_w, lru_conv_b, lru_w_a, lru_b_a, lru_w_x, lru_b_x,
              lru_lambda, sconv_w, lru_out_g, sconv_out_g, w_out, mix_post_g,
              ffn2_pre_g, ffn2_w_gate, ffn2_w_up, ffn2_w_down, ffn2_post_g):
    bn = x.shape[0]
    meta = jnp.broadcast_to(meta_tokens.astype(x.dtype)[None], (bn, N_META, x.shape[-1]))
    h = jnp.concatenate([meta, x], axis=1)
    splits = [D_LRU, 2 * D_LRU, 2 * D_LRU + D_SCONV, 2 * D_LRU + 2 * D_SCONV]

    for l in range(DEPTH):
        f = swiglu(rmsnorm(h, ffn1_pre_g[l]), ffn1_w_gate[l], ffn1_w_up[l], ffn1_w_down[l])
        h = h + 0.5 * rmsnorm(f, ffn1_post_g[l])

        u = rmsnorm(h, mix_pre_g[l])
        z = u @ w_in[l]
        y_lru, x_lru, b_sc, c_sc, v_sc = jnp.split(z, splits, axis=-1)

        x_lru = causal_depthwise_conv(x_lru, lru_conv_w[l]) + lru_conv_b[l]
        lru_out = rg_lru(x_lru, lru_w_a[l], lru_b_a[l], lru_w_x[l], lru_b_x[l], lru_lambda[l])
        lru_out = lru_out * jax.nn.gelu(y_lru, approximate=True)

        sc_out = b_sc * causal_depthwise_conv(c_sc * v_sc, sconv_w[l])

        mixed = jnp.concatenate([
            group_rmsnorm(lru_out, lru_out_g[l], N_LRU_HEADS),
            group_rmsnorm(sc_out, sconv_out_g[l], N_SCONV_GROUPS)], axis=-1)
        h = h + rmsnorm(mixed @ w_out[l], mix_post_g[l])

        f = swiglu(rmsnorm(h, ffn2_pre_g[l]), ffn2_w_gate[l], ffn2_w_up[l], ffn2_w_down[l])
        h = h + 0.5 * rmsnorm(f, ffn2_post_g[l])

    return h[:, N_META:]


import jax as _jax
import jax.numpy as _jnp

TWIN_FORMAT = 'train_step'
FWD_PARAMS = ['x', 'meta_tokens', 'ffn1_pre_g', 'ffn1_w_gate', 'ffn1_w_up', 'ffn1_w_down', 'ffn1_post_g', 'mix_pre_g', 'w_in', 'lru_conv_w', 'lru_conv_b', 'lru_w_a', 'lru_b_a', 'lru_w_x', 'lru_b_x', 'lru_lambda', 'sconv_w', 'lru_out_g', 'sconv_out_g', 'w_out', 'mix_post_g', 'ffn2_pre_g', 'ffn2_w_gate', 'ffn2_w_up', 'ffn2_w_down', 'ffn2_post_g']
TWIN_WEIGHTS = ['meta_tokens', 'ffn1_pre_g', 'ffn1_w_gate', 'ffn1_w_up', 'ffn1_w_down', 'ffn1_post_g', 'mix_pre_g', 'w_in', 'lru_conv_w', 'lru_conv_b', 'lru_w_a', 'lru_b_a', 'lru_w_x', 'lru_b_x', 'lru_lambda', 'sconv_w', 'lru_out_g', 'sconv_out_g', 'w_out', 'mix_post_g', 'ffn2_pre_g', 'ffn2_w_gate', 'ffn2_w_up', 'ffn2_w_down', 'ffn2_post_g']
TWIN_DIFF_INPUT = 'x'
TWIN_INPUTS = ['x', 'meta_tokens', 'ffn1_pre_g', 'ffn1_w_gate', 'ffn1_w_up', 'ffn1_w_down', 'ffn1_post_g', 'mix_pre_g', 'w_in', 'lru_conv_w', 'lru_conv_b', 'lru_w_a', 'lru_b_a', 'lru_w_x', 'lru_b_x', 'lru_lambda', 'sconv_w', 'lru_out_g', 'sconv_out_g', 'w_out', 'mix_post_g', 'ffn2_pre_g', 'ffn2_w_gate', 'ffn2_w_up', 'ffn2_w_down', 'ffn2_post_g', 'loss_target', 'm_meta_tokens', 'm_ffn1_pre_g', 'm_ffn1_w_gate', 'm_ffn1_w_up', 'm_ffn1_w_down', 'm_ffn1_post_g', 'm_mix_pre_g', 'm_w_in', 'm_lru_conv_w', 'm_lru_conv_b', 'm_lru_w_a', 'm_lru_b_a', 'm_lru_w_x', 'm_lru_b_x', 'm_lru_lambda', 'm_sconv_w', 'm_lru_out_g', 'm_sconv_out_g', 'm_w_out', 'm_mix_post_g', 'm_ffn2_pre_g', 'm_ffn2_w_gate', 'm_ffn2_w_up', 'm_ffn2_w_down', 'm_ffn2_post_g', 'v_meta_tokens', 'v_ffn1_pre_g', 'v_ffn1_w_gate', 'v_ffn1_w_up', 'v_ffn1_w_down', 'v_ffn1_post_g', 'v_mix_pre_g', 'v_w_in', 'v_lru_conv_w', 'v_lru_conv_b', 'v_lru_w_a', 'v_lru_b_a', 'v_lru_w_x', 'v_lru_b_x', 'v_lru_lambda', 'v_sconv_w', 'v_lru_out_g', 'v_sconv_out_g', 'v_w_out', 'v_mix_post_g', 'v_ffn2_pre_g', 'v_ffn2_w_gate', 'v_ffn2_w_up', 'v_ffn2_w_down', 'v_ffn2_post_g']
TWIN_OUTPUTS = ['loss', 'grad_x', 'grad_meta_tokens', 'grad_ffn1_pre_g', 'grad_ffn1_w_gate', 'grad_ffn1_w_up', 'grad_ffn1_w_down', 'grad_ffn1_post_g', 'grad_mix_pre_g', 'grad_w_in', 'grad_lru_conv_w', 'grad_lru_conv_b', 'grad_lru_w_a', 'grad_lru_b_a', 'grad_lru_w_x', 'grad_lru_b_x', 'grad_lru_lambda', 'grad_sconv_w', 'grad_lru_out_g', 'grad_sconv_out_g', 'grad_w_out', 'grad_mix_post_g', 'grad_ffn2_pre_g', 'grad_ffn2_w_gate', 'grad_ffn2_w_up', 'grad_ffn2_w_down', 'grad_ffn2_post_g', 'delta_meta_tokens', 'delta_ffn1_pre_g', 'delta_ffn1_w_gate', 'delta_ffn1_w_up', 'delta_ffn1_w_down', 'delta_ffn1_post_g', 'delta_mix_pre_g', 'delta_w_in', 'delta_lru_conv_w', 'delta_lru_conv_b', 'delta_lru_w_a', 'delta_lru_b_a', 'delta_lru_w_x', 'delta_lru_b_x', 'delta_lru_lambda', 'delta_sconv_w', 'delta_lru_out_g', 'delta_sconv_out_g', 'delta_w_out', 'delta_mix_post_g', 'delta_ffn2_pre_g', 'delta_ffn2_w_gate', 'delta_ffn2_w_up', 'delta_ffn2_w_down', 'delta_ffn2_post_g', 'new_m_meta_tokens', 'new_m_ffn1_pre_g', 'new_m_ffn1_w_gate', 'new_m_ffn1_w_up', 'new_m_ffn1_w_down', 'new_m_ffn1_post_g', 'new_m_mix_pre_g', 'new_m_w_in', 'new_m_lru_conv_w', 'new_m_lru_conv_b', 'new_m_lru_w_a', 'new_m_lru_b_a', 'new_m_lru_w_x', 'new_m_lru_b_x', 'new_m_lru_lambda', 'new_m_sconv_w', 'new_m_lru_out_g', 'new_m_sconv_out_g', 'new_m_w_out', 'new_m_mix_post_g', 'new_m_ffn2_pre_g', 'new_m_ffn2_w_gate', 'new_m_ffn2_w_up', 'new_m_ffn2_w_down', 'new_m_ffn2_post_g', 'new_v_meta_tokens', 'new_v_ffn1_pre_g', 'new_v_ffn1_w_gate', 'new_v_ffn1_w_up', 'new_v_ffn1_w_down', 'new_v_ffn1_post_g', 'new_v_mix_pre_g', 'new_v_w_in', 'new_v_lru_conv_w', 'new_v_lru_conv_b', 'new_v_lru_w_a', 'new_v_lru_b_a', 'new_v_lru_w_x', 'new_v_lru_b_x', 'new_v_lru_lambda', 'new_v_sconv_w', 'new_v_lru_out_g', 'new_v_sconv_out_g', 'new_v_w_out', 'new_v_mix_post_g', 'new_v_ffn2_pre_g', 'new_v_ffn2_w_gate', 'new_v_ffn2_w_up', 'new_v_ffn2_w_down', 'new_v_ffn2_post_g']
TWIN_LEAF_KINDS = {'loss': 'loss', 'grad_x': 'grad_x', 'grad_meta_tokens': 'grad_w', 'grad_ffn1_pre_g': 'grad_w', 'grad_ffn1_w_gate': 'grad_w', 'grad_ffn1_w_up': 'grad_w', 'grad_ffn1_w_down': 'grad_w', 'grad_ffn1_post_g': 'grad_w', 'grad_mix_pre_g': 'grad_w', 'grad_w_in': 'grad_w', 'grad_lru_conv_w': 'grad_w', 'grad_lru_conv_b': 'grad_w', 'grad_lru_w_a': 'grad_w', 'grad_lru_b_a': 'grad_w', 'grad_lru_w_x': 'grad_w', 'grad_lru_b_x': 'grad_w', 'grad_lru_lambda': 'grad_w', 'grad_sconv_w': 'grad_w', 'grad_lru_out_g': 'grad_w', 'grad_sconv_out_g': 'grad_w', 'grad_w_out': 'grad_w', 'grad_mix_post_g': 'grad_w', 'grad_ffn2_pre_g': 'grad_w', 'grad_ffn2_w_gate': 'grad_w', 'grad_ffn2_w_up': 'grad_w', 'grad_ffn2_w_down': 'grad_w', 'grad_ffn2_post_g': 'grad_w', 'delta_meta_tokens': 'delta_w', 'delta_ffn1_pre_g': 'delta_w', 'delta_ffn1_w_gate': 'delta_w', 'delta_ffn1_w_up': 'delta_w', 'delta_ffn1_w_down': 'delta_w', 'delta_ffn1_post_g': 'delta_w', 'delta_mix_pre_g': 'delta_w', 'delta_w_in': 'delta_w', 'delta_lru_conv_w': 'delta_w', 'delta_lru_conv_b': 'delta_w', 'delta_lru_w_a': 'delta_w', 'delta_lru_b_a': 'delta_w', 'delta_lru_w_x': 'delta_w', 'delta_lru_b_x': 'delta_w', 'delta_lru_lambda': 'delta_w', 'delta_sconv_w': 'delta_w', 'delta_lru_out_g': 'delta_w', 'delta_sconv_out_g': 'delta_w', 'delta_w_out': 'delta_w', 'delta_mix_post_g': 'delta_w', 'delta_ffn2_pre_g': 'delta_w', 'delta_ffn2_w_gate': 'delta_w', 'delta_ffn2_w_up': 'delta_w', 'delta_ffn2_w_down': 'delta_w', 'delta_ffn2_post_g': 'delta_w', 'new_m_meta_tokens': 'new_m', 'new_m_ffn1_pre_g': 'new_m', 'new_m_ffn1_w_gate': 'new_m', 'new_m_ffn1_w_up': 'new_m', 'new_m_ffn1_w_down': 'new_m', 'new_m_ffn1_post_g': 'new_m', 'new_m_mix_pre_g': 'new_m', 'new_m_w_in': 'new_m', 'new_m_lru_conv_w': 'new_m', 'new_m_lru_conv_b': 'new_m', 'new_m_lru_w_a': 'new_m', 'new_m_lru_b_a': 'new_m', 'new_m_lru_w_x': 'new_m', 'new_m_lru_b_x': 'new_m', 'new_m_lru_lambda': 'new_m', 'new_m_sconv_w': 'new_m', 'new_m_lru_out_g': 'new_m', 'new_m_sconv_out_g': 'new_m', 'new_m_w_out': 'new_m', 'new_m_mix_post_g': 'new_m', 'new_m_ffn2_pre_g': 'new_m', 'new_m_ffn2_w_gate': 'new_m', 'new_m_ffn2_w_up': 'new_m', 'new_m_ffn2_w_down': 'new_m', 'new_m_ffn2_post_g': 'new_m', 'new_v_meta_tokens': 'new_v', 'new_v_ffn1_pre_g': 'new_v', 'new_v_ffn1_w_gate': 'new_v', 'new_v_ffn1_w_up': 'new_v', 'new_v_ffn1_w_down': 'new_v', 'new_v_ffn1_post_g': 'new_v', 'new_v_mix_pre_g': 'new_v', 'new_v_w_in': 'new_v', 'new_v_lru_conv_w': 'new_v', 'new_v_lru_conv_b': 'new_v', 'new_v_lru_w_a': 'new_v', 'new_v_lru_b_a': 'new_v', 'new_v_lru_w_x': 'new_v', 'new_v_lru_b_x': 'new_v', 'new_v_lru_lambda': 'new_v', 'new_v_sconv_w': 'new_v', 'new_v_lru_out_g': 'new_v', 'new_v_sconv_out_g': 'new_v', 'new_v_w_out': 'new_v', 'new_v_mix_post_g': 'new_v', 'new_v_ffn2_pre_g': 'new_v', 'new_v_ffn2_w_gate': 'new_v', 'new_v_ffn2_w_up': 'new_v', 'new_v_ffn2_w_down': 'new_v', 'new_v_ffn2_post_g': 'new_v'}


def _forward(args):
    return _fwd_reference(*[args[k] for k in FWD_PARAMS])


def _output_shape():
    def fwd():
        inp = _fwd_setup_inputs(0)
        return _fwd_reference(*[inp[k] for k in FWD_PARAMS])
    out = _jax.eval_shape(fwd)
    return out.shape, out.dtype

N_MICROBATCH = 1
ADAM_LR = 0.001
ADAM_B1 = 0.9
ADAM_B2 = 0.999
ADAM_EPS = 1e-08
ADAM_WD = 0.01
ADAM_STEP = 10
PER_EXAMPLE_BATCH_AXIS = {'x': 0, 'loss_target': 0}
SHARED_INPUTS = []
_WEIGHT_DTYPES = {'meta_tokens': _jnp.float32, 'ffn1_pre_g': _jnp.float32, 'ffn1_w_gate': _jnp.float32, 'ffn1_w_up': _jnp.float32, 'ffn1_w_down': _jnp.float32, 'ffn1_post_g': _jnp.float32, 'mix_pre_g': _jnp.float32, 'w_in': _jnp.float32, 'lru_conv_w': _jnp.float32, 'lru_conv_b': _jnp.float32, 'lru_w_a': _jnp.float32, 'lru_b_a': _jnp.float32, 'lru_w_x': _jnp.float32, 'lru_b_x': _jnp.float32, 'lru_lambda': _jnp.float32, 'sconv_w': _jnp.float32, 'lru_out_g': _jnp.float32, 'sconv_out_g': _jnp.float32, 'w_out': _jnp.float32, 'mix_post_g': _jnp.float32, 'ffn2_pre_g': _jnp.float32, 'ffn2_w_gate': _jnp.float32, 'ffn2_w_up': _jnp.float32, 'ffn2_w_down': _jnp.float32, 'ffn2_post_g': _jnp.float32}
MOMENT_SCALE = {'meta_tokens': 1.026312e-02, 'ffn1_pre_g': 2.227085e-01, 'ffn1_w_gate': 9.424258e-02, 'ffn1_w_up': 9.519822e-02, 'ffn1_w_down': 1.567369e-01, 'ffn1_post_g': 3.982038e+00, 'mix_pre_g': 2.684642e-01, 'w_in': 1.754366e-01, 'lru_conv_w': 2.120942e-01, 'lru_conv_b': 3.689455e+00, 'lru_w_a': 1.027551e-01, 'lru_b_a': 6.131692e-02, 'lru_w_x': 1.911208e-01, 'lru_b_x': 5.606227e-02, 'lru_lambda': 9.956103e-02, 'sconv_w': 1.739369e-01, 'lru_out_g': 2.349918e-01, 'sconv_out_g': 1.890253e-01, 'w_out': 2.005242e-01, 'mix_post_g': 1.604039e+01, 'ffn2_pre_g': 1.565797e-01, 'ffn2_w_gate': 5.478775e-02, 'ffn2_w_up': 7.540545e-02, 'ffn2_w_down': 1.241020e-01, 'ffn2_post_g': 3.984181e+00}


def _to_microbatches(a, axis):
    t = _jnp.moveaxis(a, axis, 0)
    t = t.reshape((N_MICROBATCH, t.shape[0] // N_MICROBATCH) + t.shape[1:])
    return _jnp.moveaxis(t, 1, axis + 1)


def setup_inputs(seed: int = 0) -> dict:
    inp = _fwd_setup_inputs(seed)
    key = _jax.random.fold_in(_jax.random.key(seed), 7919)
    shape, _ = _output_shape()
    out = dict(inp)
    out["loss_target"] = _jax.random.normal(_jax.random.fold_in(key, 0), shape, _jnp.float32)
    for i, name in enumerate(TWIN_WEIGHTS):
        w = inp[name].astype(_jnp.float32)
        if MOMENT_SCALE is None:
            s = _jnp.sqrt(_jnp.mean(_jnp.square(w)) + 1e-30)
        else:
            s = MOMENT_SCALE[name]
        km, kv = _jax.random.split(_jax.random.fold_in(key, i + 1))
        out[name] = w
        out["m_" + name] = s * _jax.random.normal(km, w.shape, _jnp.float32)
        out["v_" + name] = (s * s) * _jax.random.uniform(kv, w.shape, _jnp.float32, 0.5, 1.5)
    if N_MICROBATCH > 1:
        for name, axis in PER_EXAMPLE_BATCH_AXIS.items():
            out[name] = _to_microbatches(out[name], axis)
    return {'x': out['x'], 'meta_tokens': out['meta_tokens'], 'ffn1_pre_g': out['ffn1_pre_g'], 'ffn1_w_gate': out['ffn1_w_gate'], 'ffn1_w_up': out['ffn1_w_up'], 'ffn1_w_down': out['ffn1_w_down'], 'ffn1_post_g': out['ffn1_post_g'], 'mix_pre_g': out['mix_pre_g'], 'w_in': out['w_in'], 'lru_conv_w': out['lru_conv_w'], 'lru_conv_b': out['lru_conv_b'], 'lru_w_a': out['lru_w_a'], 'lru_b_a': out['lru_b_a'], 'lru_w_x': out['lru_w_x'], 'lru_b_x': out['lru_b_x'], 'lru_lambda': out['lru_lambda'], 'sconv_w': out['sconv_w'], 'lru_out_g': out['lru_out_g'], 'sconv_out_g': out['sconv_out_g'], 'w_out': out['w_out'], 'mix_post_g': out['mix_post_g'], 'ffn2_pre_g': out['ffn2_pre_g'], 'ffn2_w_gate': out['ffn2_w_gate'], 'ffn2_w_up': out['ffn2_w_up'], 'ffn2_w_down': out['ffn2_w_down'], 'ffn2_post_g': out['ffn2_post_g'], 'loss_target': out['loss_target'], 'm_meta_tokens': out['m_meta_tokens'], 'm_ffn1_pre_g': out['m_ffn1_pre_g'], 'm_ffn1_w_gate': out['m_ffn1_w_gate'], 'm_ffn1_w_up': out['m_ffn1_w_up'], 'm_ffn1_w_down': out['m_ffn1_w_down'], 'm_ffn1_post_g': out['m_ffn1_post_g'], 'm_mix_pre_g': out['m_mix_pre_g'], 'm_w_in': out['m_w_in'], 'm_lru_conv_w': out['m_lru_conv_w'], 'm_lru_conv_b': out['m_lru_conv_b'], 'm_lru_w_a': out['m_lru_w_a'], 'm_lru_b_a': out['m_lru_b_a'], 'm_lru_w_x': out['m_lru_w_x'], 'm_lru_b_x': out['m_lru_b_x'], 'm_lru_lambda': out['m_lru_lambda'], 'm_sconv_w': out['m_sconv_w'], 'm_lru_out_g': out['m_lru_out_g'], 'm_sconv_out_g': out['m_sconv_out_g'], 'm_w_out': out['m_w_out'], 'm_mix_post_g': out['m_mix_post_g'], 'm_ffn2_pre_g': out['m_ffn2_pre_g'], 'm_ffn2_w_gate': out['m_ffn2_w_gate'], 'm_ffn2_w_up': out['m_ffn2_w_up'], 'm_ffn2_w_down': out['m_ffn2_w_down'], 'm_ffn2_post_g': out['m_ffn2_post_g'], 'v_meta_tokens': out['v_meta_tokens'], 'v_ffn1_pre_g': out['v_ffn1_pre_g'], 'v_ffn1_w_gate': out['v_ffn1_w_gate'], 'v_ffn1_w_up': out['v_ffn1_w_up'], 'v_ffn1_w_down': out['v_ffn1_w_down'], 'v_ffn1_post_g': out['v_ffn1_post_g'], 'v_mix_pre_g': out['v_mix_pre_g'], 'v_w_in': out['v_w_in'], 'v_lru_conv_w': out['v_lru_conv_w'], 'v_lru_conv_b': out['v_lru_conv_b'], 'v_lru_w_a': out['v_lru_w_a'], 'v_lru_b_a': out['v_lru_b_a'], 'v_lru_w_x': out['v_lru_w_x'], 'v_lru_b_x': out['v_lru_b_x'], 'v_lru_lambda': out['v_lru_lambda'], 'v_sconv_w': out['v_sconv_w'], 'v_lru_out_g': out['v_lru_out_g'], 'v_sconv_out_g': out['v_sconv_out_g'], 'v_w_out': out['v_w_out'], 'v_mix_post_g': out['v_mix_post_g'], 'v_ffn2_pre_g': out['v_ffn2_pre_g'], 'v_ffn2_w_gate': out['v_ffn2_w_gate'], 'v_ffn2_w_up': out['v_ffn2_w_up'], 'v_ffn2_w_down': out['v_ffn2_w_down'], 'v_ffn2_post_g': out['v_ffn2_post_g']}


def _loss(weights, diff, rest, loss_target):
    with _jax.named_scope("forward"):
        args = {**rest, TWIN_DIFF_INPUT: diff, **{k: w.astype(_WEIGHT_DTYPES[k]) for k, w in weights.items()}}
        y = _forward(args)
    with _jax.named_scope("loss_head"):
        err = _jnp.square(y.astype(_jnp.float32) - loss_target)
        return 0.5 * _jnp.sum(_jnp.mean(err, axis=-1)) if err.ndim else 0.5 * err


def _adamw(w, g, m, v):
    m = ADAM_B1 * m + (1.0 - ADAM_B1) * g
    v = ADAM_B2 * v + (1.0 - ADAM_B2) * _jnp.square(g)
    m_hat = m / (1.0 - ADAM_B1 ** ADAM_STEP)
    v_hat = v / (1.0 - ADAM_B2 ** ADAM_STEP)
    delta = -ADAM_LR * (m_hat / (_jnp.sqrt(v_hat) + ADAM_EPS) + ADAM_WD * w)
    return delta, m, v


def reference(x, meta_tokens, ffn1_pre_g, ffn1_w_gate, ffn1_w_up, ffn1_w_down, ffn1_post_g, mix_pre_g, w_in, lru_conv_w, lru_conv_b, lru_w_a, lru_b_a, lru_w_x, lru_b_x, lru_lambda, sconv_w, lru_out_g, sconv_out_g, w_out, mix_post_g, ffn2_pre_g, ffn2_w_gate, ffn2_w_up, ffn2_w_down, ffn2_post_g, loss_target, m_meta_tokens, m_ffn1_pre_g, m_ffn1_w_gate, m_ffn1_w_up, m_ffn1_w_down, m_ffn1_post_g, m_mix_pre_g, m_w_in, m_lru_conv_w, m_lru_conv_b, m_lru_w_a, m_lru_b_a, m_lru_w_x, m_lru_b_x, m_lru_lambda, m_sconv_w, m_lru_out_g, m_sconv_out_g, m_w_out, m_mix_post_g, m_ffn2_pre_g, m_ffn2_w_gate, m_ffn2_w_up, m_ffn2_w_down, m_ffn2_post_g, v_meta_tokens, v_ffn1_pre_g, v_ffn1_w_gate, v_ffn1_w_up, v_ffn1_w_down, v_ffn1_post_g, v_mix_pre_g, v_w_in, v_lru_conv_w, v_lru_conv_b, v_lru_w_a, v_lru_b_a, v_lru_w_x, v_lru_b_x, v_lru_lambda, v_sconv_w, v_lru_out_g, v_sconv_out_g, v_w_out, v_mix_post_g, v_ffn2_pre_g, v_ffn2_w_gate, v_ffn2_w_up, v_ffn2_w_down, v_ffn2_post_g):
    given = dict(x=x, meta_tokens=meta_tokens, ffn1_pre_g=ffn1_pre_g, ffn1_w_gate=ffn1_w_gate, ffn1_w_up=ffn1_w_up, ffn1_w_down=ffn1_w_down, ffn1_post_g=ffn1_post_g, mix_pre_g=mix_pre_g, w_in=w_in, lru_conv_w=lru_conv_w, lru_conv_b=lru_conv_b, lru_w_a=lru_w_a, lru_b_a=lru_b_a, lru_w_x=lru_w_x, lru_b_x=lru_b_x, lru_lambda=lru_lambda, sconv_w=sconv_w, lru_out_g=lru_out_g, sconv_out_g=sconv_out_g, w_out=w_out, mix_post_g=mix_post_g, ffn2_pre_g=ffn2_pre_g, ffn2_w_gate=ffn2_w_gate, ffn2_w_up=ffn2_w_up, ffn2_w_down=ffn2_w_down, ffn2_post_g=ffn2_post_g, loss_target=loss_target, m_meta_tokens=m_meta_tokens, m_ffn1_pre_g=m_ffn1_pre_g, m_ffn1_w_gate=m_ffn1_w_gate, m_ffn1_w_up=m_ffn1_w_up, m_ffn1_w_down=m_ffn1_w_down, m_ffn1_post_g=m_ffn1_post_g, m_mix_pre_g=m_mix_pre_g, m_w_in=m_w_in, m_lru_conv_w=m_lru_conv_w, m_lru_conv_b=m_lru_conv_b, m_lru_w_a=m_lru_w_a, m_lru_b_a=m_lru_b_a, m_lru_w_x=m_lru_w_x, m_lru_b_x=m_lru_b_x, m_lru_lambda=m_lru_lambda, m_sconv_w=m_sconv_w, m_lru_out_g=m_lru_out_g, m_sconv_out_g=m_sconv_out_g, m_w_out=m_w_out, m_mix_post_g=m_mix_post_g, m_ffn2_pre_g=m_ffn2_pre_g, m_ffn2_w_gate=m_ffn2_w_gate, m_ffn2_w_up=m_ffn2_w_up, m_ffn2_w_down=m_ffn2_w_down, m_ffn2_post_g=m_ffn2_post_g, v_meta_tokens=v_meta_tokens, v_ffn1_pre_g=v_ffn1_pre_g, v_ffn1_w_gate=v_ffn1_w_gate, v_ffn1_w_up=v_ffn1_w_up, v_ffn1_w_down=v_ffn1_w_down, v_ffn1_post_g=v_ffn1_post_g, v_mix_pre_g=v_mix_pre_g, v_w_in=v_w_in, v_lru_conv_w=v_lru_conv_w, v_lru_conv_b=v_lru_conv_b, v_lru_w_a=v_lru_w_a, v_lru_b_a=v_lru_b_a, v_lru_w_x=v_lru_w_x, v_lru_b_x=v_lru_b_x, v_lru_lambda=v_lru_lambda, v_sconv_w=v_sconv_w, v_lru_out_g=v_lru_out_g, v_sconv_out_g=v_sconv_out_g, v_w_out=v_w_out, v_mix_post_g=v_mix_post_g, v_ffn2_pre_g=v_ffn2_pre_g, v_ffn2_w_gate=v_ffn2_w_gate, v_ffn2_w_up=v_ffn2_w_up, v_ffn2_w_down=v_ffn2_w_down, v_ffn2_post_g=v_ffn2_post_g)
    weights = {n: given[n] for n in TWIN_WEIGHTS}
    shared = {n: given[n] for n in SHARED_INPUTS}
    per_example = {n: given[n] for n in ['x']}
    grad_fn = _jax.value_and_grad(_loss, argnums=(0, 1))

    def one_microbatch(ex, loss_target):
        ex = dict(ex)
        diff = ex.pop(TWIN_DIFF_INPUT)
        return grad_fn(weights, diff, {**shared, **ex}, loss_target)

    if N_MICROBATCH == 1:
        loss, (grad_w, grad_x) = one_microbatch(per_example, given["loss_target"])
    else:
        def body(carry, xs):
            loss_sum, grad_sum = carry
            l_k, (gw_k, gx_k) = one_microbatch(xs[0], xs[1])
            with _jax.named_scope("update"):
                return (loss_sum + l_k, _jax.tree.map(_jnp.add, grad_sum, gw_k)), gx_k

        init = (_jnp.zeros((), _jnp.float32), _jax.tree.map(_jnp.zeros_like, weights))
        (loss, grad_w), grad_x = _jax.lax.scan(body, init, (per_example, given["loss_target"]))
    with _jax.named_scope("update"):
        delta_w, new_m, new_v = {}, {}, {}
        for n in TWIN_WEIGHTS:
            delta_w[n], new_m[n], new_v[n] = _adamw(weights[n], grad_w[n], given["m_" + n], given["v_" + n])
    return (loss, grad_x, *[grad_w[n] for n in TWIN_WEIGHTS], *[delta_w[n] for n in TWIN_WEIGHTS],
            *[new_m[n] for n in TWIN_WEIGHTS], *[new_v[n] for n in TWIN_WEIGHTS])
```

```python
import functools
import math

import jax
import jax.numpy as jnp
from jax import lax
from jax.experimental import pallas as pl
from jax.experimental.pallas import tpu as pltpu

F32 = jnp.float32
BF = jnp.bfloat16
MESH = pl.DeviceIdType.MESH

EPS = 1e-6
N_META = 16
N_HEADS = 16
HEAD = 64
LRU_C = 8.0
LANE = 128
N_CHIP = 4
ROW_ALIGN = 384
MM_TILES = 8
EW_TILES = 12
MIX_CHUNKS = 24
VMEM_LIMIT = 56 << 20

ADAM_LR = 0.001
ADAM_B1 = 0.9
ADAM_B2 = 0.999
ADAM_EPS = 1e-08
ADAM_WD = 0.01
ADAM_STEP = 10


def _round_up(a, b):
    return (a + b - 1) // b * b


def _params(sem=None):
    if sem is None:
        return pltpu.CompilerParams(vmem_limit_bytes=VMEM_LIMIT)
    return pltpu.CompilerParams(dimension_semantics=sem, vmem_limit_bytes=VMEM_LIMIT)


def _sigmoid(x):
    return 1.0 / (1.0 + jnp.exp(-x))


def _dot(a, b):
    return jnp.dot(a, b, preferred_element_type=F32)


def _dot_nt(a, b):
    return lax.dot_general(a, b, (((1,), (1,)), ((), ())), preferred_element_type=F32)


def _dot_tn(a, b):
    return lax.dot_general(a, b, (((0,), (0,)), ((), ())), preferred_element_type=F32)


def _rms(x, g):
    r = lax.rsqrt(jnp.mean(x * x, axis=-1, keepdims=True) + EPS)
    return x * r * g


def _rms_bwd(x, g, dy):
    r = lax.rsqrt(jnp.mean(x * x, axis=-1, keepdims=True) + EPS)
    xh = x * r
    q = dy * g
    dx = r * (q - xh * jnp.mean(q * xh, axis=-1, keepdims=True))
    return dx, dy * xh


def _ffn_up(n, wg, wu, name):
    tp, d = n.shape
    fp = wg.shape[2]
    tm = tp // MM_TILES

    def body(n_ref, wg_ref, wu_ref, a_ref, b_ref, s_ref):
        nn = n_ref[...]
        a = _dot(nn, wg_ref[...])
        b = _dot(nn, wu_ref[...])
        a_ref[...] = a.astype(BF)
        b_ref[...] = b.astype(BF)
        s_ref[...] = (a * _sigmoid(a) * b).astype(BF)

    out = jax.ShapeDtypeStruct((N_CHIP, tp, fp), BF)
    wspec = pl.BlockSpec((None, d, fp), lambda k, i: (k, 0, 0))
    ospec = pl.BlockSpec((None, tm, fp), lambda k, i: (k, i, 0))
    return pl.pallas_call(
        body, name=name, grid=(N_CHIP, MM_TILES),
        in_specs=[pl.BlockSpec((tm, d), lambda k, i: (i, 0)), wspec, wspec],
        out_specs=[ospec, ospec, ospec], out_shape=[out, out, out],
        compiler_params=_params(("arbitrary", "arbitrary")),
    )(n, wg, wu)


def _ffn_bwd_act(df, wd, a, b, name):
    tp, d = df.shape
    fp = wd.shape[1]
    tm = tp // MM_TILES

    def body(df_ref, wd_ref, a_ref, b_ref, da_ref, db_ref):
        ds = _dot_nt(df_ref[...], wd_ref[...])
        av = a_ref[...].astype(F32)
        bv = b_ref[...].astype(F32)
        sg = _sigmoid(av)
        da_ref[...] = (ds * bv * sg * (1.0 + av * (1.0 - sg))).astype(BF)
        db_ref[...] = (ds * av * sg).astype(BF)

    out = jax.ShapeDtypeStruct((N_CHIP, tp, fp), BF)
    aspec = pl.BlockSpec((None, tm, fp), lambda k, i: (k, i, 0))
    return pl.pallas_call(
        body, name=name, grid=(N_CHIP, MM_TILES),
        in_specs=[pl.BlockSpec((tm, d), lambda k, i: (i, 0)),
                  pl.BlockSpec((None, fp, d), lambda k, i: (k, 0, 0)), aspec, aspec],
        out_specs=[aspec, aspec], out_shape=[out, out],
        compiler_params=_params(("arbitrary", "arbitrary")),
    )(df, wd, a, b)


def _col_matmul(lhs, w, name, trans_b, out_dtype):
    tp, kd = lhs.shape
    nk = w.shape[0]
    nc = w.shape[1] if trans_b else w.shape[2]
    tm = tp // MM_TILES

    def body(l_ref, w_ref, o_ref):
        if trans_b:
            o_ref[...] = _dot_nt(l_ref[...], w_ref[...]).astype(out_dtype)
        else:
            o_ref[...] = _dot(l_ref[...], w_ref[...]).astype(out_dtype)

    return pl.pallas_call(
        body, name=name, grid=(nk, MM_TILES),
        in_specs=[pl.BlockSpec((tm, kd), lambda k, i: (i, 0)),
                  pl.BlockSpec((None,) + tuple(w.shape[1:]), lambda k, i: (k, 0, 0))],
        out_specs=pl.BlockSpec((tm, nc), lambda k, i: (i, k)),
        out_shape=jax.ShapeDtypeStruct((tp, nk * nc), out_dtype),
        compiler_params=_params(("arbitrary", "arbitrary")),
    )(lhs, w)


def _row_matmul(pairs, name, trans_b, d_out):
    l0 = pairs[0][0]
    tp = l0.shape[1] if l0.ndim == 3 else l0.shape[0]
    nk = pairs[0][1].shape[0]
    tm = tp // MM_TILES
    npair = len(pairs)

    def body(*refs):
        o_ref = refs[2 * npair]
        k = pl.program_id(1)
        part = None
        for q in range(npair):
            l = refs[2 * q][...]
            w = refs[2 * q + 1][...]
            t = _dot_nt(l, w) if trans_b else _dot(l, w)
            part = t if part is None else part + t

        @pl.when(k == 0)
        def _():
            o_ref[...] = part

        @pl.when(k > 0)
        def _():
            o_ref[...] += part

    in_specs, args = [], []
    for lhs, w in pairs:
        if lhs.ndim == 3:
            in_specs.append(pl.BlockSpec((None, tm, lhs.shape[2]), lambda i, k: (k, i, 0)))
        else:
            in_specs.append(pl.BlockSpec((tm, lhs.shape[1] // nk), lambda i, k: (i, k)))
        in_specs.append(pl.BlockSpec((None,) + tuple(w.shape[1:]), lambda i, k: (k, 0, 0)))
        args += [lhs, w]
    return pl.pallas_call(
        body, name=name, grid=(MM_TILES, nk), in_specs=in_specs,
        out_specs=pl.BlockSpec((tm, d_out), lambda i, k: (i, 0)),
        out_shape=jax.ShapeDtypeStruct((tp, d_out), F32),
        compiler_params=_params(("arbitrary", "arbitrary")),
    )(*args)


def _wgrad_call(x, y, name, x_width=None, y_width=None):
    tp = x.shape[1] if x.ndim == 3 else x.shape[0]
    tm = tp // MM_TILES

    def spec(a, width):
        if a.ndim == 3:
            return pl.BlockSpec((None, tm, a.shape[2]), lambda k, i: (k, i, 0)), a.shape[2]
        if width is None:
            return pl.BlockSpec((tm, a.shape[1]), lambda k, i: (i, 0)), a.shape[1]
        return pl.BlockSpec((tm, width), lambda k, i: (i, k)), width

    xs, p = spec(x, x_width)
    ys, q = spec(y, y_width)

    def body(x_ref, y_ref, o_ref, acc):
        i = pl.program_id(1)
        t = _dot_tn(x_ref[...], y_ref[...])

        @pl.when(i == 0)
        def _():
            acc[...] = t

        @pl.when(i > 0)
        def _():
            acc[...] += t

        @pl.when(i == MM_TILES - 1)
        def _():
            o_ref[...] = acc[...].astype(BF)

    return pl.pallas_call(
        body, name=name, grid=(N_CHIP, MM_TILES), in_specs=[xs, ys],
        out_specs=pl.BlockSpec((None, p, q), lambda k, i: (k, 0, 0)),
        out_shape=jax.ShapeDtypeStruct((N_CHIP, p, q), BF),
        scratch_shapes=[pltpu.VMEM((p, q), F32)],
        compiler_params=_params(("arbitrary", "arbitrary")),
    )(x, y)


def _row_call(body, name, tp, d, row_ins, vec_ins, row_out_dtypes, n_acc, acc_shape=None):
    te = tp // EW_TILES
    rspec = pl.BlockSpec((te, d), lambda i: (i, 0))
    vspec = pl.BlockSpec((1, d), lambda i: (0, 0))
    acc_shape = acc_shape or (1, d)
    aspec = pl.BlockSpec(acc_shape, lambda i: (0, 0))
    return pl.pallas_call(
        body, name=name, grid=(EW_TILES,),
        in_specs=[rspec] * len(row_ins) + [vspec] * len(vec_ins),
        out_specs=[rspec] * len(row_out_dtypes) + [aspec] * n_acc,
        out_shape=[jax.ShapeDtypeStruct((tp, d), dt) for dt in row_out_dtypes]
        + [jax.ShapeDtypeStruct(acc_shape, F32)] * n_acc,
        compiler_params=_params(("arbitrary",)),
    )(*row_ins, *vec_ins)


def _norm0(h, g):
    tp, d = h.shape

    def body(h_ref, g_ref, n_ref):
        n_ref[...] = _rms(h_ref[...], g_ref[...]).astype(BF)

    return _row_call(body, "norm0", tp, d, [h], [g], [BF], 0)[0]


def _post_fwd(f, h, g_post, g_next, scale, name):
    tp, d = h.shape

    def body(f_ref, h_ref, gp_ref, gn_ref, hn_ref, n_ref):
        hn = h_ref[...] + scale * _rms(f_ref[...], gp_ref[...])
        hn_ref[...] = hn
        n_ref[...] = _rms(hn, gn_ref[...]).astype(BF)

    return _row_call(body, name, tp, d, [f, h], [g_post, g_next], [F32, BF], 0)


def _loss_bwd(f, h, tgt, g_post, t_real):
    tp, d = h.shape
    te = tp // EW_TILES

    def body(f_ref, h_ref, t_ref, gp_ref, dh_ref, df_ref, dg_ref, loss_ref):
        i = pl.program_id(0)

        @pl.when(i == 0)
        def _():
            dg_ref[...] = jnp.zeros_like(dg_ref)
            loss_ref[...] = jnp.zeros_like(loss_ref)

        f = f_ref[...]
        gp = gp_ref[...]
        h3 = h_ref[...] + 0.5 * _rms(f, gp)
        rows = i * te + lax.broadcasted_iota(jnp.int32, (te, 1), 0)
        real = (rows >= N_META) & (rows < t_real)
        e = jnp.where(real, h3 - t_ref[...], 0.0)
        loss_ref[...] += 0.5 * jnp.sum(jnp.sum(e * e, axis=1, keepdims=True), axis=0, keepdims=True) / d
        dh = e / d
        dh_ref[...] = dh
        dfv, dgr = _rms_bwd(f, gp, 0.5 * dh)
        df_ref[...] = dfv.astype(BF)
        dg_ref[...] += jnp.sum(dgr, axis=0, keepdims=True)

    rspec = pl.BlockSpec((te, d), lambda i: (i, 0))
    vspec = pl.BlockSpec((1, d), lambda i: (0, 0))
    return pl.pallas_call(
        body, name="loss_bwd", grid=(EW_TILES,),
        in_specs=[rspec, rspec, rspec, vspec],
        out_specs=[rspec, rspec, vspec, pl.BlockSpec((1, 1), lambda i: (0, 0))],
        out_shape=[jax.ShapeDtypeStruct((tp, d), F32), jax.ShapeDtypeStruct((tp, d), BF),
                   jax.ShapeDtypeStruct((1, d), F32), jax.ShapeDtypeStruct((1, 1), F32)],
        compiler_params=_params(("arbitrary",)),
    )(f, h, tgt, g_post)


def _pre_bwd(dn, h, dh_out, g_pre, name, chain=None):
    tp, d = h.shape

    def body(*refs):
        if chain is None:
            dn_ref, h_ref, dho_ref, g_ref, dh_ref, dg_ref = refs
        else:
            dn_ref, h_ref, dho_ref, p_ref, g_ref, gp_ref, dh_ref, dp_ref, dg_ref, dgp_ref = refs
        i = pl.program_id(0)

        @pl.when(i == 0)
        def _():
            dg_ref[...] = jnp.zeros_like(dg_ref)
            if chain is not None:
                dgp_ref[...] = jnp.zeros_like(dgp_ref)

        dx, dgr = _rms_bwd(h_ref[...], g_ref[...], dn_ref[...])
        dh = dho_ref[...] + dx
        dh_ref[...] = dh
        dg_ref[...] += jnp.sum(dgr, axis=0, keepdims=True)
        if chain is not None:
            dp, dgpr = _rms_bwd(p_ref[...], gp_ref[...], chain[2] * dh)
            dp_ref[...] = dp.astype(BF)
            dgp_ref[...] += jnp.sum(dgpr, axis=0, keepdims=True)

    if chain is None:
        return _row_call(body, name, tp, d, [dn, h, dh_out], [g_pre], [F32], 1)
    return _row_call(body, name, tp, d, [dn, h, dh_out, chain[0]], [g_pre, chain[1]], [F32, BF], 2)


def _gelu(y):
    c = math.sqrt(2.0 / math.pi)
    return 0.5 * y * (1.0 + jnp.tanh(c * (y + 0.044715 * y * y * y)))


def _gelu_grad(y):
    c = math.sqrt(2.0 / math.pi)
    t = jnp.tanh(c * (y + 0.044715 * y * y * y))
    return 0.5 * (1.0 + t) + 0.5 * y * (1.0 - t * t) * c * (1.0 + 3.0 * 0.044715 * y * y)


def _neg_expm1(x):
    p = 1.0 + x * (1.0 / 9.0)
    for n in (8.0, 7.0, 6.0, 5.0, 4.0, 3.0, 2.0):
        p = 1.0 + x * (1.0 / n) * p
    return -jnp.where(x > -0.35, x * p, jnp.exp(x) - 1.0)


def _softplus(x):
    e = jnp.exp(-jnp.abs(x))
    w = 1.0 + e
    l1p = jnp.where(w == 1.0, e, jnp.log(w) * (e / jnp.where(w == 1.0, 1.0, w - 1.0)))
    return jnp.maximum(x, 0.0) + l1p


def _group_mean(v, gm):
    hi = v.astype(BF)
    lo = (v - hi.astype(F32)).astype(BF)
    return _dot(hi, gm) + _dot(lo, gm)


def _shift_dn(win, s, r):
    if s == 0:
        return win[8:8 + r]
    return pltpu.roll(win, s, 0)[8:8 + r]


def _shift_up(win, s, r):
    if s == 0:
        return win[0:r]
    return pltpu.roll(win, r + 8 - s, 0)[0:r]


def _window_dn(ref, t0, r, first):
    if first:
        return jnp.concatenate([jnp.zeros((8, ref.shape[1]), F32), ref[0:r, :]], axis=0)
    return ref[pl.ds(t0 - 8, r + 8), :]


def _tile_scan(a, u, reverse):
    r = a.shape[0]
    rid = lax.broadcasted_iota(jnp.int32, a.shape, 0) & 7
    for dlt in (1, 2, 4):
        sh = (r - dlt) if reverse else dlt
        a_s = pltpu.roll(a, sh, 0)
        u_s = pltpu.roll(u, sh, 0)
        keep = (rid + dlt <= 7) if reverse else (rid >= dlt)
        u = jnp.where(keep, u + a * u_s, u)
        a = jnp.where(keep, a * a_s, a)
    return a, u


def _lru_gates(xc, wa, ba, wx, bx, sp):
    xb = xc.astype(BF)
    ga = _sigmoid(_dot(xb, wa) + ba)
    gx = _sigmoid(_dot(xb, wx) + bx)
    la = -LRU_C * ga * sp
    return ga, gx, la


def _conv4(win, w4, cb, r):
    return (cb + w4[3:4] * _shift_dn(win, 0, r) + w4[2:3] * _shift_dn(win, 1, r)
            + w4[1:2] * _shift_dn(win, 2, r) + w4[0:1] * _shift_dn(win, 3, r))


def _lru_fwd(z, w4, cb, wa2, ba, wx2, bx, lam, g_out, gm):
    tp = z.shape[0]
    dl = cb.shape[1]
    nb = dl // LANE
    r = tp // MIX_CHUNKS
    c = LANE

    def body(y_ref, x_ref, w4_ref, cb_ref, wa_ref, ba_ref, wx_ref, bx_ref, lam_ref, go_ref, gm_ref, m_ref, hs_ref):
        w4v = w4_ref[...]
        cbv = cb_ref[...]
        wa = wa_ref[...]
        wx = wx_ref[...]
        bav = ba_ref[...]
        bxv = bx_ref[...]
        gov = go_ref[...]
        gmv = gm_ref[...]
        sp = _softplus(-lam_ref[...])

        def chunk(t0, hprev, first):
            win = _window_dn(x_ref, t0, r, first)
            xc = _conv4(win, w4v, cbv, r)
            ga, gx, la = _lru_gates(xc, wa, bav, wx, bxv, sp)
            a = jnp.exp(la)
            u = jnp.sqrt(_neg_expm1(2.0 * la)) * gx * xc
            ac, uc = _tile_scan(a, u, False)
            for j in range(r // 8):
                hj = uc[8 * j:8 * j + 8] + ac[8 * j:8 * j + 8] * hprev
                hs_ref[pl.ds(t0 + 8 * j, 8), :] = hj
                hprev = jnp.broadcast_to(hj[7:8], (8, c))
            h = hs_ref[pl.ds(t0, r), :]
            lo = h * _gelu(y_ref[pl.ds(t0, r), :])
            rs = lax.rsqrt(_group_mean(lo * lo, gmv) + EPS)
            m_ref[pl.ds(t0, r), :] = (lo * rs * gov).astype(BF)
            return hprev

        hp = chunk(0, jnp.zeros((8, c), F32), True)

        def loop(ci, hp):
            return chunk(pl.multiple_of(ci * r, 16), hp, False)

        lax.fori_loop(1, MIX_CHUNKS, loop, hp)

    col = lambda off: pl.BlockSpec((tp, c), lambda j: (0, off + j))
    vec = pl.BlockSpec((1, c), lambda j: (0, j))
    return pl.pallas_call(
        body, name="lru_fwd", grid=(nb,),
        in_specs=[col(0), col(nb), pl.BlockSpec((8, c), lambda j: (0, j)), vec,
                  pl.BlockSpec((None, c, c), lambda j: (j, 0, 0)), vec,
                  pl.BlockSpec((None, c, c), lambda j: (j, 0, 0)), vec, vec, vec,
                  pl.BlockSpec((c, c), lambda j: (0, 0))],
        out_specs=[col(0), col(0)],
        out_shape=[jax.ShapeDtypeStruct((tp, dl), BF), jax.ShapeDtypeStruct((tp, dl), F32)],
        compiler_params=_params(("arbitrary",)),
    )(z, z, w4, cb, wa2, ba, wx2, bx, lam, g_out, gm)


def _lru_bwd(z, hs, dmix, w4, cb, wa2, ba, wx2, bx, lam, g_out, gm):
    tp = z.shape[0]
    dl = cb.shape[1]
    nb = dl // LANE
    r = tp // MIX_CHUNKS
    c = LANE

    def body(y_ref, x_ref, hs_ref, dm_ref, w4_ref, cb_ref, wa_ref, ba_ref, wx_ref, bx_ref, lam_ref, go_ref, gm_ref,
             dy_ref, dx_ref, small_ref, dwa_ref, dwx_ref, xc_buf, ga_buf, gx_buf, a_buf, dh_buf, dxc_buf):
        w4v = w4_ref[...]
        cbv = cb_ref[...]
        wa = wa_ref[...]
        wx = wx_ref[...]
        bav = ba_ref[...]
        bxv = bx_ref[...]
        gov = go_ref[...]
        gmv = gm_ref[...]
        lamv = lam_ref[...]
        sp = _softplus(-lamv)
        small_ref[...] = jnp.zeros_like(small_ref)
        dwa_ref[...] = jnp.zeros_like(dwa_ref)
        dwx_ref[...] = jnp.zeros_like(dwx_ref)
        a_buf[pl.ds(tp, 8), :] = jnp.zeros((8, c), F32)
        dxc_buf[pl.ds(tp, 8), :] = jnp.zeros((8, c), F32)

        def fwd_chunk(t0, first):
            win = _window_dn(x_ref, t0, r, first)
            xc = _conv4(win, w4v, cbv, r)
            ga, gx, la = _lru_gates(xc, wa, bav, wx, bxv, sp)
            xc_buf[pl.ds(t0, r), :] = xc
            ga_buf[pl.ds(t0, r), :] = ga
            gx_buf[pl.ds(t0, r), :] = gx
            a_buf[pl.ds(t0, r), :] = jnp.exp(la)
            h = hs_ref[pl.ds(t0, r), :]
            yv = y_ref[pl.ds(t0, r), :]
            ge = _gelu(yv)
            lo = h * ge
            rs = lax.rsqrt(_group_mean(lo * lo, gmv) + EPS)
            xh = lo * rs
            dm = dm_ref[pl.ds(t0, r), :]
            q = dm * gov
            dlo = rs * (q - xh * _group_mean(q * xh, gmv))
            small_ref[8:9, :] += jnp.sum(dm * xh, axis=0, keepdims=True)
            dh_buf[pl.ds(t0, r), :] = dlo * ge
            dy_ref[pl.ds(t0, r), :] = (dlo * h * _gelu_grad(yv)).astype(BF)

        fwd_chunk(0, True)

        def floop(ci, carry):
            fwd_chunk(pl.multiple_of(ci * r, 16), False)
            return carry

        lax.fori_loop(1, MIX_CHUNKS, floop, 0)

        def bwd_chunk(t0, vnext, first):
            ap = _shift_up(a_buf[pl.ds(t0, r + 8), :], 1, r)
            ac, uc = _tile_scan(ap, dh_buf[pl.ds(t0, r), :], True)
            for j in reversed(range(r // 8)):
                vj = uc[8 * j:8 * j + 8] + ac[8 * j:8 * j + 8] * vnext
                dh_buf[pl.ds(t0 + 8 * j, 8), :] = vj
                vnext = jnp.broadcast_to(vj[0:1], (8, c))
            v = dh_buf[pl.ds(t0, r), :]
            hprev = _shift_dn(_window_dn(hs_ref, t0, r, first), 1, r)
            xc = xc_buf[pl.ds(t0, r), :]
            ga = ga_buf[pl.ds(t0, r), :]
            gx = gx_buf[pl.ds(t0, r), :]
            a = a_buf[pl.ds(t0, r), :]
            em = _neg_expm1(-2.0 * LRU_C * ga * sp)
            mult = jnp.sqrt(em)
            dla = v * hprev * a - (v * gx * xc) * ((1.0 - em) / mult)
            dgx = v * mult * xc
            dxc = v * mult * gx
            dga = dla * (-LRU_C) * sp
            small_ref[7:8, :] += jnp.sum(dla * (-LRU_C) * ga, axis=0, keepdims=True)
            dpa = dga * ga * (1.0 - ga)
            dpx = dgx * gx * (1.0 - gx)
            small_ref[5:6, :] += jnp.sum(dpa, axis=0, keepdims=True)
            small_ref[6:7, :] += jnp.sum(dpx, axis=0, keepdims=True)
            dpab = dpa.astype(BF)
            dpxb = dpx.astype(BF)
            xb = xc.astype(BF)
            dxc = dxc + _dot_nt(dpab, wa) + _dot_nt(dpxb, wx)
            dwa_ref[...] += _dot_tn(xb, dpab)
            dwx_ref[...] += _dot_tn(xb, dpxb)
            dxc_buf[pl.ds(t0, r), :] = dxc
            small_ref[4:5, :] += jnp.sum(dxc, axis=0, keepdims=True)
            dwin = dxc_buf[pl.ds(t0, r + 8), :]
            dx_ref[pl.ds(t0, r), :] = (w4v[3:4] * dxc + w4v[2:3] * _shift_up(dwin, 1, r)
                                       + w4v[1:2] * _shift_up(dwin, 2, r) + w4v[0:1] * _shift_up(dwin, 3, r)).astype(BF)
            xwin = _window_dn(x_ref, t0, r, first)
            for k in range(4):
                small_ref[k:k + 1, :] += jnp.sum(dxc * _shift_dn(xwin, 3 - k, r), axis=0, keepdims=True)
            return vnext

        def bloop(it, vnext):
            ci = MIX_CHUNKS - 1 - it
            return bwd_chunk(pl.multiple_of(ci * r, 16), vnext, False)

        vn = lax.fori_loop(0, MIX_CHUNKS - 1, bloop, jnp.zeros((8, c), F32))
        bwd_chunk(0, vn, True)
        small_ref[7:8, :] = small_ref[7:8, :] * (-_sigmoid(-lamv))

    col = lambda off: pl.BlockSpec((tp, c), lambda j: (0, off + j))
    vec = pl.BlockSpec((1, c), lambda j: (0, j))
    mat = pl.BlockSpec((None, c, c), lambda j: (j, 0, 0))
    buf = pltpu.VMEM((tp, c), F32)
    bufp = pltpu.VMEM((tp + 8, c), F32)
    return pl.pallas_call(
        body, name="lru_bwd", grid=(nb,),
        in_specs=[col(0), col(nb), col(0), col(0), pl.BlockSpec((8, c), lambda j: (0, j)), vec, mat, vec, mat, vec,
                  vec, vec, pl.BlockSpec((c, c), lambda j: (0, 0))],
        out_specs=[col(0), col(0), pl.BlockSpec((16, c), lambda j: (0, j)), mat, mat],
        out_shape=[jax.ShapeDtypeStruct((tp, dl), BF), jax.ShapeDtypeStruct((tp, dl), BF),
                   jax.ShapeDtypeStruct((16, dl), F32), jax.ShapeDtypeStruct((nb, c, c), F32),
                   jax.ShapeDtypeStruct((nb, c, c), F32)],
        scratch_shapes=[buf, buf, buf, bufp, buf, bufp],
        compiler_params=_params(("arbitrary",)),
    )(z, z, hs, dmix, w4, cb, wa2, ba, wx2, bx, lam, g_out, gm)


def _sc_conv(cvwin, w3, r):
    return w3[2:3] * _shift_dn(cvwin, 0, r) + w3[1:2] * _shift_dn(cvwin, 1, r) + w3[0:1] * _shift_dn(cvwin, 2, r)


def _sc_fwd(z, w3, g_out, gm, dl):
    tp = z.shape[0]
    nb = dl // LANE
    r = tp // MIX_CHUNKS
    c = LANE

    def body(b_ref, c_ref, v_ref, w3_ref, go_ref, gm_ref, m_ref):
        w3v = w3_ref[...]
        gov = go_ref[...]
        gmv = gm_ref[...]

        def chunk(t0, first):
            cvwin = _window_dn(c_ref, t0, r, first) * _window_dn(v_ref, t0, r, first)
            so = b_ref[pl.ds(t0, r), :] * _sc_conv(cvwin, w3v, r)
            rs = lax.rsqrt(_group_mean(so * so, gmv) + EPS)
            m_ref[pl.ds(t0, r), :] = (so * rs * gov).astype(BF)

        chunk(0, True)

        def loop(ci, carry):
            chunk(pl.multiple_of(ci * r, 16), False)
            return carry

        lax.fori_loop(1, MIX_CHUNKS, loop, 0)

    col = lambda off: pl.BlockSpec((tp, c), lambda j: (0, off + j))
    return pl.pallas_call(
        body, name="sconv_fwd", grid=(nb,),
        in_specs=[col(2 * nb), col(3 * nb), col(4 * nb), pl.BlockSpec((8, c), lambda j: (0, j)),
                  pl.BlockSpec((1, c), lambda j: (0, j)), pl.BlockSpec((c, c), lambda j: (0, 0))],
        out_specs=col(0), out_shape=jax.ShapeDtypeStruct((tp, dl), BF),
        compiler_params=_params(("arbitrary",)),
    )(z, z, z, w3, g_out, gm)


def _sc_bwd(z, dmix, w3, g_out, gm, dl):
    tp = z.shape[0]
    nb = dl // LANE
    r = tp // MIX_CHUNKS
    c = LANE

    def body(b_ref, c_ref, v_ref, dm_ref, w3_ref, go_ref, gm_ref, db_ref, dc_ref, dv_ref, small_ref, dsc_buf):
        w3v = w3_ref[...]
        gov = go_ref[...]
        gmv = gm_ref[...]
        small_ref[...] = jnp.zeros_like(small_ref)
        dsc_buf[pl.ds(tp, 8), :] = jnp.zeros((8, c), F32)

        def chunk1(t0, first):
            cvwin = _window_dn(c_ref, t0, r, first) * _window_dn(v_ref, t0, r, first)
            sc = _sc_conv(cvwin, w3v, r)
            bv = b_ref[pl.ds(t0, r), :]
            so = bv * sc
            rs = lax.rsqrt(_group_mean(so * so, gmv) + EPS)
            xh = so * rs
            dm = dm_ref[pl.ds(t0, r), :]
            q = dm * gov
            dso = rs * (q - xh * _group_mean(q * xh, gmv))
            small_ref[3:4, :] += jnp.sum(dm * xh, axis=0, keepdims=True)
            db_ref[pl.ds(t0, r), :] = (dso * sc).astype(BF)
            dsc = dso * bv
            dsc_buf[pl.ds(t0, r), :] = dsc
            for k in range(3):
                small_ref[k:k + 1, :] += jnp.sum(dsc * _shift_dn(cvwin, 2 - k, r), axis=0, keepdims=True)

        chunk1(0, True)

        def loop1(ci, carry):
            chunk1(pl.multiple_of(ci * r, 16), False)
            return carry

        lax.fori_loop(1, MIX_CHUNKS, loop1, 0)

        def loop2(ci, carry):
            t0 = pl.multiple_of(ci * r, 16)
            dwin = dsc_buf[pl.ds(t0, r + 8), :]
            dcv = w3v[2:3] * _shift_up(dwin, 0, r) + w3v[1:2] * _shift_up(dwin, 1, r) + w3v[0:1] * _shift_up(dwin, 2, r)
            dc_ref[pl.ds(t0, r), :] = (dcv * v_ref[pl.ds(t0, r), :]).astype(BF)
            dv_ref[pl.ds(t0, r), :] = (dcv * c_ref[pl.ds(t0, r), :]).astype(BF)
            return carry

        lax.fori_loop(0, MIX_CHUNKS, loop2, 0)

    col = lambda off: pl.BlockSpec((tp, c), lambda j: (0, off + j))
    out = jax.ShapeDtypeStruct((tp, dl), BF)
    return pl.pallas_call(
        body, name="sconv_bwd", grid=(nb,),
        in_specs=[col(2 * nb), col(3 * nb), col(4 * nb), col(nb), pl.BlockSpec((8, c), lambda j: (0, j)),
                  pl.BlockSpec((1, c), lambda j: (0, j)), pl.BlockSpec((c, c), lambda j: (0, 0))],
        out_specs=[col(0), col(0), col(0), pl.BlockSpec((8, c), lambda j: (0, j))],
        out_shape=[out, out, out, jax.ShapeDtypeStruct((8, dl), F32)],
        scratch_shapes=[pltpu.VMEM((tp + 8, c), F32)],
        compiler_params=_params(("arbitrary",)),
    )(z, z, z, dmix, w3, g_out, gm)


def _cast_pad(w, rows_p, cols_p, name):
    r, c = w.shape

    def body(w_ref, o_ref):
        if (rows_p, cols_p) != (r, c):
            o_ref[...] = jnp.zeros_like(o_ref)
        o_ref[0:r, 0:c] = w_ref[...].astype(BF)

    return pl.pallas_call(
        body, name=name, out_shape=jax.ShapeDtypeStruct((rows_p, cols_p), BF),
        in_specs=[pl.BlockSpec((r, c), lambda: (0, 0))], out_specs=pl.BlockSpec((rows_p, cols_p), lambda: (0, 0)),
        compiler_params=_params(),
    )(w)


def _adamw_math(w, g, m, v):
    m2 = ADAM_B1 * m + (1.0 - ADAM_B1) * g
    v2 = ADAM_B2 * v + (1.0 - ADAM_B2) * (g * g)
    m_hat = m2 / (1.0 - ADAM_B1 ** ADAM_STEP)
    v_hat = v2 / (1.0 - ADAM_B2 ** ADAM_STEP)
    delta = -ADAM_LR * (m_hat / (jnp.sqrt(v_hat) + ADAM_EPS) + ADAM_WD * w)
    return delta, m2, v2


def _adamw(w, g, m, v, name, row_tiles, col_tiles):
    r, c = w.shape
    tr = r // row_tiles
    tc = c // col_tiles
    gc = g.shape[1] if col_tiles == 1 else tc

    def body(w_ref, g_ref, m_ref, v_ref, go_ref, d_ref, mo_ref, vo_ref):
        gv = g_ref[...][:, 0:tc]
        delta, m2, v2 = _adamw_math(w_ref[...], gv, m_ref[...], v_ref[...])
        go_ref[...] = gv
        d_ref[...] = delta
        mo_ref[...] = m2
        vo_ref[...] = v2

    spec = pl.BlockSpec((tr, tc), lambda i, j: (i, j))
    out = jax.ShapeDtypeStruct((r, c), F32)
    return pl.pallas_call(
        body, name=name, grid=(row_tiles, col_tiles),
        in_specs=[spec, pl.BlockSpec((tr, gc), lambda i, j: (i, j)), spec, spec],
        out_specs=[spec] * 4, out_shape=[out] * 4,
        compiler_params=_params(("arbitrary", "arbitrary")),
    )(w, g, m, v)


def _adamw_small(w, g, m, v):
    def body(w_ref, g_ref, m_ref, v_ref, d_ref, mo_ref, vo_ref):
        delta, m2, v2 = _adamw_math(w_ref[...], g_ref[...], m_ref[...], v_ref[...])
        d_ref[...] = delta
        mo_ref[...] = m2
        vo_ref[...] = v2

    out = jax.ShapeDtypeStruct(w.shape, F32)
    spec = pl.BlockSpec(w.shape, lambda: (0, 0))
    return pl.pallas_call(body, name="adamw_small", in_specs=[spec] * 4, out_specs=[spec] * 3, out_shape=[out] * 3,
                          compiler_params=_params())(w, g, m, v)


def _place():
    x, y, c = lax.axis_index("x"), lax.axis_index("y"), lax.axis_index("c")
    chips = [(1 - x, y), (x, 1 - y), (1 - x, 1 - y)]
    return x, y, c, chips


ANY = pl.BlockSpec(memory_space=pl.ANY)


def _all_gather(shards):
    n = len(shards)

    def body(*refs):
        ins, outs = refs[:n], refs[n:2 * n]
        lsem, s_ici, r_ici, s_d2d, r_d2d = refs[2 * n:]
        x, y, c, chips = _place()
        me = 2 * x + y

        def rows(w, chip, core):
            half = shards[w].shape[0] // 2
            return outs[w].at[chip, pl.ds(core * half, half)]

        local = [pltpu.make_async_copy(ins[w], outs[w].at[me], lsem.at[w]) for w in range(n)]
        for cp in local:
            cp.start()

        def ici(w, j, px, py):
            half = shards[w].shape[0] // 2
            return pltpu.make_async_remote_copy(
                src_ref=ins[w].at[pl.ds(c * half, half)], dst_ref=rows(w, me, c),
                send_sem=s_ici.at[w, j], recv_sem=r_ici.at[w, j], device_id=(px, py, c), device_id_type=MESH)

        def d2d(w, j, chip, core):
            return pltpu.make_async_remote_copy(
                src_ref=rows(w, chip, core), dst_ref=rows(w, chip, core),
                send_sem=s_d2d.at[w, j], recv_sem=r_d2d.at[w, j], device_id=(x, y, 1 - c), device_id_type=MESH)

        sends = [ici(w, j, px, py) for w in range(n) for j, (px, py) in enumerate(chips)]
        for cp in sends:
            cp.start()
        passed = []
        for w in range(n):
            for j, (px, py) in enumerate(chips):
                chip = 2 * px + py
                half = shards[w].shape[0] // 2
                pltpu.make_async_remote_copy(
                    src_ref=rows(w, chip, c), dst_ref=rows(w, chip, c), send_sem=s_ici.at[w, j],
                    recv_sem=r_ici.at[w, j], device_id=(px, py, c), device_id_type=MESH).wait_recv()
                fw = d2d(w, j, chip, c)
                fw.start()
                passed.append(fw)
        for w in range(n):
            for j, (px, py) in enumerate(chips):
                d2d(w, j, 2 * px + py, 1 - c).wait_recv()
        for cp in sends + passed:
            cp.wait_send()
        for cp in local:
            cp.wait()

    return pl.pallas_call(
        body, name="all_gather_weights",
        out_shape=[jax.ShapeDtypeStruct((N_CHIP,) + s.shape, s.dtype) for s in shards],
        in_specs=[ANY] * n, out_specs=[ANY] * n,
        scratch_shapes=[pltpu.SemaphoreType.DMA((n,)), pltpu.SemaphoreType.DMA((n, 3)), pltpu.SemaphoreType.DMA((n, 3)),
                        pltpu.SemaphoreType.DMA((n, 3)), pltpu.SemaphoreType.DMA((n, 3))],
    )(*shards)


def _pair_exchange(grads):
    n = len(grads)

    def body(*refs):
        ins, outs = refs[:n], refs[n:2 * n]
        ssem, rsem = refs[2 * n:]
        x, y, c, _ = _place()
        cps = []
        for w in range(n):
            half = grads[w].shape[1] // 2
            cps.append(pltpu.make_async_remote_copy(
                src_ref=ins[w].at[:, pl.ds((1 - c) * half, half)], dst_ref=outs[w],
                send_sem=ssem.at[w], recv_sem=rsem.at[w], device_id=(x, y, 1 - c), device_id_type=MESH))
        for cp in cps:
            cp.start()
        for cp in cps:
            cp.wait()

    return pl.pallas_call(
        body, name="grad_pair_exchange",
        out_shape=[jax.ShapeDtypeStruct((N_CHIP, g.shape[1] // 2, g.shape[2]), BF) for g in grads],
        in_specs=[ANY] * n, out_specs=[ANY] * n,
        scratch_shapes=[pltpu.SemaphoreType.DMA((n,)), pltpu.SemaphoreType.DMA((n,))],
    )(*grads)


def _pair_sum(g, sib, core, name):
    _, r, cdim = g.shape
    half = r // 2

    def body(core_ref, g_ref, s_ref, o_ref):
        o_ref[...] = (g_ref[...].astype(F32) + s_ref[...].astype(F32)).astype(BF)

    return pl.pallas_call(
        body, name=name,
        grid_spec=pltpu.PrefetchScalarGridSpec(
            num_scalar_prefetch=1, grid=(N_CHIP,),
            in_specs=[pl.BlockSpec((None, half, cdim), lambda k, core: (k, core[0], 0)),
                      pl.BlockSpec((None, half, cdim), lambda k, core: (k, 0, 0))],
            out_specs=pl.BlockSpec((None, half, cdim), lambda k, core: (k, 0, 0))),
        out_shape=jax.ShapeDtypeStruct((N_CHIP, half, cdim), BF),
        compiler_params=_params(("arbitrary",)),
    )(core, g, sib)


def _chip_exchange(psums):
    n = len(psums)

    def body(*refs):
        ins, outs = refs[:n], refs[n:2 * n]
        ssem, rsem = refs[2 * n:]
        x, y, c, chips = _place()
        cps = []
        for w in range(n):
            for j, (px, py) in enumerate(chips):
                cps.append(pltpu.make_async_remote_copy(
                    src_ref=ins[w].at[2 * px + py], dst_ref=outs[w].at[j],
                    send_sem=ssem.at[w, j], recv_sem=rsem.at[w, j], device_id=(px, py, c), device_id_type=MESH))
        for cp in cps:
            cp.start()
        for cp in cps:
            cp.wait()

    return pl.pallas_call(
        body, name="grad_chip_exchange",
        out_shape=[jax.ShapeDtypeStruct((3,) + p.shape[1:], BF) for p in psums],
        in_specs=[ANY] * n, out_specs=[ANY] * n,
        scratch_shapes=[pltpu.SemaphoreType.DMA((n, 3)), pltpu.SemaphoreType.DMA((n, 3))],
    )(*psums)


def _final_sum(g, sib, recv, sel, name):
    _, r, cdim = g.shape
    half = r // 2
    nt = 4
    th = half // nt

    def body(sel_ref, g_ref, s_ref, r_ref, o_ref):
        acc = g_ref[...].astype(F32) + s_ref[...].astype(F32)
        for j in range(3):
            acc = acc + r_ref[j].astype(F32)
        o_ref[...] = acc

    return pl.pallas_call(
        body, name=name,
        grid_spec=pltpu.PrefetchScalarGridSpec(
            num_scalar_prefetch=1, grid=(nt,),
            in_specs=[pl.BlockSpec((None, th, cdim), lambda i, sel: (sel[0], sel[1] * nt + i, 0)),
                      pl.BlockSpec((None, th, cdim), lambda i, sel: (sel[0], i, 0)),
                      pl.BlockSpec((3, th, cdim), lambda i, sel: (0, i, 0))],
            out_specs=pl.BlockSpec((th, cdim), lambda i, sel: (i, 0))),
        out_shape=jax.ShapeDtypeStruct((half, cdim), F32),
        compiler_params=_params(("arbitrary",)),
    )(sel, g, sib, recv)


def _join_halves(halves):
    n = len(halves)

    def body(*refs):
        ins, outs = refs[:n], refs[n:2 * n]
        lsem, ssem, rsem = refs[2 * n:]
        x, y, c, _ = _place()
        loc, rem = [], []
        for w in range(n):
            half = halves[w].shape[0]
            mine = outs[w].at[pl.ds(c * half, half)]
            loc.append(pltpu.make_async_copy(ins[w], mine, lsem.at[w]))
            rem.append(pltpu.make_async_remote_copy(
                src_ref=ins[w], dst_ref=mine, send_sem=ssem.at[w], recv_sem=rsem.at[w],
                device_id=(x, y, 1 - c), device_id_type=MESH))
        for cp in loc + rem:
            cp.start()
        for w in range(n):
            half = halves[w].shape[0]
            theirs = outs[w].at[pl.ds((1 - c) * half, half)]
            pltpu.make_async_remote_copy(
                src_ref=ins[w], dst_ref=theirs, send_sem=ssem.at[w], recv_sem=rsem.at[w],
                device_id=(x, y, 1 - c), device_id_type=MESH).wait_recv()
        for cp in rem:
            cp.wait_send()
        for cp in loc:
            cp.wait()

    return pl.pallas_call(
        body, name="grad_join_halves",
        out_shape=[jax.ShapeDtypeStruct((2 * h.shape[0], h.shape[1]), F32) for h in halves],
        in_specs=[ANY] * n, out_specs=[ANY] * n,
        scratch_shapes=[pltpu.SemaphoreType.DMA((n,)), pltpu.SemaphoreType.DMA((n,)), pltpu.SemaphoreType.DMA((n,))],
    )(*halves)


def _small_all_reduce(buf, name):
    rows, d = buf.shape

    def body(in_ref, out_ref, sib, all4, ssem, rsem, psem, qsem):
        x, y, c, chips = _place()
        me = 2 * x + y
        to_sib = pltpu.make_async_remote_copy(src_ref=in_ref, dst_ref=sib, send_sem=ssem, recv_sem=rsem,
                                              device_id=(x, y, 1 - c), device_id_type=MESH)
        to_sib.start()
        to_sib.wait()
        all4[me] = in_ref[...] + sib[...]
        cps = [pltpu.make_async_remote_copy(src_ref=all4.at[me], dst_ref=all4.at[me], send_sem=psem.at[j],
                                            recv_sem=qsem.at[j], device_id=(px, py, c), device_id_type=MESH)
               for j, (px, py) in enumerate(chips)]
        for cp in cps:
            cp.start()
        for j, (px, py) in enumerate(chips):
            chip = 2 * px + py
            pltpu.make_async_remote_copy(src_ref=all4.at[chip], dst_ref=all4.at[chip], send_sem=psem.at[j],
                                         recv_sem=qsem.at[j], device_id=(px, py, c), device_id_type=MESH).wait_recv()
        for cp in cps:
            cp.wait_send()
        out_ref[...] = (all4[0] + all4[1]) + (all4[2] + all4[3])

    vm = pl.BlockSpec(memory_space=pltpu.VMEM)
    return pl.pallas_call(
        body, name=name, out_shape=jax.ShapeDtypeStruct((rows, d), F32),
        in_specs=[vm], out_specs=vm,
        scratch_shapes=[pltpu.VMEM((rows, d), F32), pltpu.VMEM((N_CHIP, rows, d), F32),
                        pltpu.SemaphoreType.DMA, pltpu.SemaphoreType.DMA,
                        pltpu.SemaphoreType.DMA((3,)), pltpu.SemaphoreType.DMA((3,))],
        compiler_params=_params(),
    )(buf)


def _pair_blocks(w):
    w4 = w.reshape(N_HEADS // 2, 2, HEAD, HEAD)
    eye = jnp.eye(2, dtype=w.dtype)
    return jnp.einsum("pirc,ij->pirjc", w4, eye).reshape(N_HEADS // 2, LANE, LANE)


def _unpair_blocks(w2):
    w5 = w2.reshape(N_HEADS // 2, 2, HEAD, 2, HEAD)
    return jnp.stack([w5[:, 0, :, 0, :], w5[:, 1, :, 1, :]], axis=1).reshape(N_HEADS, HEAD, HEAD)


def kernel(x, meta_tokens, ffn1_pre_g, ffn1_w_gate, ffn1_w_up, ffn1_w_down, ffn1_post_g, mix_pre_g, w_in, lru_conv_w, lru_conv_b, lru_w_a, lru_b_a, lru_w_x, lru_b_x, lru_lambda, sconv_w, lru_out_g, sconv_out_g, w_out, mix_post_g, ffn2_pre_g, ffn2_w_gate, ffn2_w_up, ffn2_w_down, ffn2_post_g, loss_target, m_meta_tokens, m_ffn1_pre_g, m_ffn1_w_gate, m_ffn1_w_up, m_ffn1_w_down, m_ffn1_post_g, m_mix_pre_g, m_w_in, m_lru_conv_w, m_lru_conv_b, m_lru_w_a, m_lru_b_a, m_lru_w_x, m_lru_b_x, m_lru_lambda, m_sconv_w, m_lru_out_g, m_sconv_out_g, m_w_out, m_mix_post_g, m_ffn2_pre_g, m_ffn2_w_gate, m_ffn2_w_up, m_ffn2_w_down, m_ffn2_post_g, v_meta_tokens, v_ffn1_pre_g, v_ffn1_w_gate, v_ffn1_w_up, v_ffn1_w_down, v_ffn1_post_g, v_mix_pre_g, v_w_in, v_lru_conv_w, v_lru_conv_b, v_lru_w_a, v_lru_b_a, v_lru_w_x, v_lru_b_x, v_lru_lambda, v_sconv_w, v_lru_out_g, v_sconv_out_g, v_w_out, v_mix_post_g, v_ffn2_pre_g, v_ffn2_w_gate, v_ffn2_w_up, v_ffn2_w_down, v_ffn2_post_g):
    seq, d = x.shape[1], x.shape[2]
    t_real = N_META + seq
    tp = _round_up(t_real, ROW_ALIGN)
    f4 = ffn1_w_gate.shape[2]
    f4p = _round_up(f4, LANE)
    dl = lru_conv_b.shape[1]
    cin = w_in.shape[2]
    xi, yi, ci = lax.axis_index("x"), lax.axis_index("y"), lax.axis_index("c")
    chip = 2 * xi + yi
    zero = jnp.zeros((), jnp.int32)

    big = {
        "ffn1_w_gate": (ffn1_w_gate[0], d, f4p), "ffn1_w_up": (ffn1_w_up[0], d, f4p),
        "ffn1_w_down": (ffn1_w_down[0], f4p, d), "w_in": (w_in[0], d, cin), "w_out": (w_out[0], w_out.shape[1], d),
        "ffn2_w_gate": (ffn2_w_gate[0], d, f4p), "ffn2_w_up": (ffn2_w_up[0], d, f4p),
        "ffn2_w_down": (ffn2_w_down[0], f4p, d),
    }
    names = list(big)
    shards = [_cast_pad(big[k][0], big[k][1], big[k][2], "cast_" + k) for k in names]
    full = dict(zip(names, _all_gather(shards)))

    gm = jnp.kron(jnp.eye(2, dtype=F32), jnp.full((HEAD, HEAD), 1.0 / HEAD, F32)).astype(BF)
    wa2 = _pair_blocks(lru_w_a[0])
    wx2 = _pair_blocks(lru_w_x[0])

    dlq = dl // N_CHIP
    dq = d // N_CHIP
    R_GAIN, R_LOSS, R_META, R_LRU, R_SC, R_WA = 0, 6, 8, 24, 40, 48
    n_wrows = (N_HEADS // 2) * LANE * LANE // d
    R_WX = R_WA + n_wrows
    R_END = R_WX + n_wrows

    def pack(gains, meta, lru16, sc8, wa_, wx_, loss=None):
        rows = [jnp.concatenate(gains, axis=0)]
        lossrow = jnp.zeros((2, d), F32)
        if loss is not None:
            lossrow = lossrow.at[0, 0].set(loss)
        rows.append(lossrow)
        rows.append(meta)
        rows.append(jnp.concatenate([lru16, jnp.zeros((16, d - dl), F32)], axis=1))
        rows.append(jnp.concatenate([sc8, jnp.zeros((8, d - dl), F32)], axis=1))
        rows.append(wa_.reshape(n_wrows, d))
        rows.append(wx_.reshape(n_wrows, d))
        return jnp.concatenate(rows, axis=0)

    def place_cols(blk, width, total):
        return lax.dynamic_update_slice(jnp.zeros((blk.shape[0], total), F32), blk, (zero, chip * width))

    def pack_params(meta_, g1pre, g1post, gmpre, gmpost, g2pre, g2post, cw, cbias, wa_, ba_, wx_, bx_, lam_, sw, lgo, sgo):
        lru16 = jnp.concatenate([place_cols(cw[0], dlq, dl), cbias, ba_, bx_, lam_, lgo, jnp.zeros((7, dl), F32)], axis=0)
        sc8 = jnp.concatenate([place_cols(sw[0], dlq, dl), sgo, jnp.zeros((4, dl), F32)], axis=0)
        return pack([g1pre, g1post, gmpre, gmpost, g2pre, g2post], place_cols(meta_, dq, d), lru16, sc8,
                    _pair_blocks(wa_[0]), _pair_blocks(wx_[0]))

    p_w = pack_params(meta_tokens, ffn1_pre_g, ffn1_post_g, mix_pre_g, mix_post_g, ffn2_pre_g, ffn2_post_g, lru_conv_w,
                      lru_conv_b, lru_w_a, lru_b_a, lru_w_x, lru_b_x, lru_lambda, sconv_w, lru_out_g, sconv_out_g)
    p_m = pack_params(m_meta_tokens, m_ffn1_pre_g, m_ffn1_post_g, m_mix_pre_g, m_mix_post_g, m_ffn2_pre_g, m_ffn2_post_g,
                      m_lru_conv_w, m_lru_conv_b, m_lru_w_a, m_lru_b_a, m_lru_w_x, m_lru_b_x, m_lru_lambda, m_sconv_w,
                      m_lru_out_g, m_sconv_out_g)
    p_v = pack_params(v_meta_tokens, v_ffn1_pre_g, v_ffn1_post_g, v_mix_pre_g, v_mix_post_g, v_ffn2_pre_g, v_ffn2_post_g,
                      v_lru_conv_w, v_lru_conv_b, v_lru_w_a, v_lru_b_a, v_lru_w_x, v_lru_b_x, v_lru_lambda, v_sconv_w,
                      v_lru_out_g, v_sconv_out_g)

    gathered = _small_all_reduce(jnp.where(ci == 0, p_w, 0.0)[R_META:R_WA], "small_weight_gather")
    meta_full = gathered[0:N_META]
    w4_full = gathered[R_LRU - R_META:R_LRU - R_META + 4, 0:dl]
    w3_full = gathered[R_SC - R_META:R_SC - R_META + 3, 0:dl]
    w4p = jnp.concatenate([w4_full, jnp.zeros((4, dl), F32)], axis=0)
    w3p = jnp.concatenate([w3_full, jnp.zeros((5, dl), F32)], axis=0)

    h0 = jnp.concatenate([meta_full, x[0], jnp.zeros((tp - t_real, d), F32)], axis=0)
    tgt = jnp.concatenate([jnp.zeros((N_META, d), F32), loss_target[0], jnp.zeros((tp - t_real, d), F32)], axis=0)

    n1 = _norm0(h0, ffn1_pre_g)
    a1, b1, s1 = _ffn_up(n1, full["ffn1_w_gate"], full["ffn1_w_up"], "ffn1_up")
    f1 = _row_matmul([(s1, full["ffn1_w_down"])], "ffn1_down", False, d)
    h1, u = _post_fwd(f1, h0, ffn1_post_g, mix_pre_g, 0.5, "ffn1_post")
    z = _col_matmul(u, full["w_in"], "in_proj", False, F32)
    m_lru, hs = _lru_fwd(z, w4p, lru_conv_b, wa2.astype(BF), lru_b_a, wx2.astype(BF), lru_b_x, lru_lambda, lru_out_g, gm)
    m_sc = _sc_fwd(z, w3p, sconv_out_g, gm, dl)
    mixed = jnp.concatenate([m_lru, m_sc], axis=1)
    p = _row_matmul([(mixed, full["w_out"])], "out_proj", False, d)
    h2, n2 = _post_fwd(p, h1, mix_post_g, ffn2_pre_g, 1.0, "mix_post")
    a2, b2, s2 = _ffn_up(n2, full["ffn2_w_gate"], full["ffn2_w_up"], "ffn2_up")
    f2 = _row_matmul([(s2, full["ffn2_w_down"])], "ffn2_down", False, d)
    dh3, df2, dg_ffn2_post, loss_part = _loss_bwd(f2, h2, tgt, ffn2_post_g, t_real)

    da2, db2 = _ffn_bwd_act(df2, full["ffn2_w_down"], a2, b2, "ffn2_bwd_act")
    g_ffn2_down = _wgrad_call(s2, df2, "ffn2_down_wgrad")
    g_ffn2_gate = _wgrad_call(n2, da2, "ffn2_gate_wgrad")
    g_ffn2_up = _wgrad_call(n2, db2, "ffn2_up_wgrad")
    dn2 = _row_matmul([(da2, full["ffn2_w_gate"]), (db2, full["ffn2_w_up"])], "ffn2_bwd_up", True, d)
    dh2, dp, dg_ffn2_pre, dg_mix_post = _pre_bwd(dn2, h2, dh3, ffn2_pre_g, "ffn2_pre_bwd", (p, mix_post_g, 1.0))
    dmixed = _col_matmul(dp, full["w_out"], "out_proj_bwd", True, F32)
    g_w_out = _wgrad_call(mixed, dp, "w_out_wgrad", x_width=mixed.shape[1] // N_CHIP)
    dzy, dzx, lru_small, dwa2, dwx2 = _lru_bwd(z, hs, dmixed, w4p, lru_conv_b, wa2.astype(BF), lru_b_a, wx2.astype(BF),
                                               lru_b_x, lru_lambda, lru_out_g, gm)
    dzb, dzc, dzv, sc_small = _sc_bwd(z, dmixed, w3p, sconv_out_g, gm, dl)
    dz = jnp.concatenate([dzy, dzx, dzb, dzc, dzv], axis=1)
    g_w_in = _wgrad_call(u, dz, "w_in_wgrad", y_width=cin)
    du = _row_matmul([(dz, full["w_in"])], "in_proj_bwd", True, d)
    dh1, df1, dg_mix_pre, dg_ffn1_post = _pre_bwd(du, h1, dh2, mix_pre_g, "mix_pre_bwd", (f1, ffn1_post_g, 0.5))
    da1, db1 = _ffn_bwd_act(df1, full["ffn1_w_down"], a1, b1, "ffn1_bwd_act")
    g_ffn1_down = _wgrad_call(s1, df1, "ffn1_down_wgrad")
    g_ffn1_gate = _wgrad_call(n1, da1, "ffn1_gate_wgrad")
    g_ffn1_up = _wgrad_call(n1, db1, "ffn1_up_wgrad")
    dn1 = _row_matmul([(da1, full["ffn1_w_gate"]), (db1, full["ffn1_w_up"])], "ffn1_bwd_up", True, d)
    dh0, dg_ffn1_pre = _pre_bwd(dn1, h0, dh1, ffn1_pre_g, "ffn1_pre_bwd")

    grad_x = dh0[N_META:t_real][None]

    p_g_local = pack([dg_ffn1_pre, dg_ffn1_post, dg_mix_pre, dg_mix_post, dg_ffn2_pre, dg_ffn2_post], dh0[0:N_META],
                     lru_small, sc_small, dwa2, dwx2, loss=loss_part[0, 0])
    p_g = _small_all_reduce(p_g_local, "small_grad_all_reduce")
    loss = p_g[R_LOSS, 0]
    p_delta, p_newm, p_newv = _adamw_small(p_w, p_g, p_m, p_v)

    def unpack(buf):
        out = {}
        for i, k in enumerate(["ffn1_pre_g", "ffn1_post_g", "mix_pre_g", "mix_post_g", "ffn2_pre_g", "ffn2_post_g"]):
            out[k] = buf[R_GAIN + i:R_GAIN + i + 1]
        out["meta_tokens"] = lax.dynamic_slice(buf[R_META:R_META + N_META], (zero, chip * dq), (N_META, dq))
        lru = buf[R_LRU:R_LRU + 16, 0:dl]
        out["lru_conv_w"] = lax.dynamic_slice(lru[0:4], (zero, chip * dlq), (4, dlq))[None]
        out["lru_conv_b"] = lru[4:5]
        out["lru_b_a"] = lru[5:6]
        out["lru_b_x"] = lru[6:7]
        out["lru_lambda"] = lru[7:8]
        out["lru_out_g"] = lru[8:9]
        sc = buf[R_SC:R_SC + 8, 0:dl]
        out["sconv_w"] = lax.dynamic_slice(sc[0:3], (zero, chip * dlq), (3, dlq))[None]
        out["sconv_out_g"] = sc[3:4]
        out["lru_w_a"] = _unpair_blocks(buf[R_WA:R_WX].reshape(N_HEADS // 2, LANE, LANE))[None]
        out["lru_w_x"] = _unpair_blocks(buf[R_WX:R_END].reshape(N_HEADS // 2, LANE, LANE))[None]
        return out

    s_grad, s_delta, s_newm, s_newv = unpack(p_g), unpack(p_delta), unpack(p_newm), unpack(p_newv)

    g_big = {"ffn1_w_gate": g_ffn1_gate, "ffn1_w_up": g_ffn1_up, "ffn1_w_down": g_ffn1_down, "w_in": g_w_in,
             "w_out": g_w_out, "ffn2_w_gate": g_ffn2_gate, "ffn2_w_up": g_ffn2_up, "ffn2_w_down": g_ffn2_down}
    glist = [g_big[k] for k in names]
    sibs = _pair_exchange(glist)
    core = jnp.reshape(ci, (1,)).astype(jnp.int32)
    psums = [_pair_sum(g, s, core, "pair_sum_" + k) for g, s, k in zip(glist, sibs, names)]
    recvs = _chip_exchange(psums)
    sel = jnp.stack([chip, ci]).astype(jnp.int32)
    halves = [_final_sum(g, s, r_, sel, "final_sum_" + k) for g, s, r_, k in zip(glist, sibs, recvs, names)]
    gfull = dict(zip(names, _join_halves(halves)))

    w_big = {"ffn1_w_gate": ffn1_w_gate, "ffn1_w_up": ffn1_w_up, "ffn1_w_down": ffn1_w_down, "w_in": w_in, "w_out": w_out,
             "ffn2_w_gate": ffn2_w_gate, "ffn2_w_up": ffn2_w_up, "ffn2_w_down": ffn2_w_down}
    m_big = {"ffn1_w_gate": m_ffn1_w_gate, "ffn1_w_up": m_ffn1_w_up, "ffn1_w_down": m_ffn1_w_down, "w_in": m_w_in,
             "w_out": m_w_out, "ffn2_w_gate": m_ffn2_w_gate, "ffn2_w_up": m_ffn2_w_up, "ffn2_w_down": m_ffn2_w_down}
    v_big = {"ffn1_w_gate": v_ffn1_w_gate, "ffn1_w_up": v_ffn1_w_up, "ffn1_w_down": v_ffn1_w_down, "w_in": v_w_in,
             "w_out": v_w_out, "ffn2_w_gate": v_ffn2_w_gate, "ffn2_w_up": v_ffn2_w_up, "ffn2_w_down": v_ffn2_w_down}
    b_grad, b_delta, b_newm, b_newv = {}, {}, {}, {}
    for k in names:
        wide_rows = w_big[k].shape[1] % 64 == 0
        g_, d_, m_, v_ = _adamw(w_big[k][0], gfull[k], m_big[k][0], v_big[k][0], "adamw_" + k,
                                8 if wide_rows else 4, 1 if wide_rows else 2)
        b_grad[k], b_delta[k], b_newm[k], b_newv[k] = g_[None], d_[None], m_[None], v_[None]

    order = ["meta_tokens", "ffn1_pre_g", "ffn1_w_gate", "ffn1_w_up", "ffn1_w_down", "ffn1_post_g", "mix_pre_g", "w_in",
             "lru_conv_w", "lru_conv_b", "lru_w_a", "lru_b_a", "lru_w_x", "lru_b_x", "lru_lambda", "sconv_w", "lru_out_g",
             "sconv_out_g", "w_out", "mix_post_g", "ffn2_pre_g", "ffn2_w_gate", "ffn2_w_up", "ffn2_w_down", "ffn2_post_g"]

    def pick(small, bigd):
        return [bigd[k] if k in bigd else small[k] for k in order]

    return (loss, grad_x, *pick(s_grad, b_grad), *pick(s_delta, b_delta), *pick(s_newm, b_newm), *pick(s_newv, b_newv))
```

```python
import functools
import math

import jax
import jax.numpy as jnp
from jax import lax
from jax.experimental import pallas as pl
from jax.experimental.pallas import tpu as pltpu

F32 = jnp.float32
BF = jnp.bfloat16
MESH = pl.DeviceIdType.MESH

EPS = 1e-6
N_META = 16
N_HEADS = 16
HEAD = 64
LRU_C = 8.0
LANE = 128
N_CHIP = 4
ROW_ALIGN = 384
MM_TILES = 8
EW_TILES = 12
MIX_CHUNKS = 24
VMEM_LIMIT = 56 << 20

ADAM_LR = 0.001
ADAM_B1 = 0.9
ADAM_B2 = 0.999
ADAM_EPS = 1e-08
ADAM_WD = 0.01
ADAM_STEP = 10


def _round_up(a, b):
    return (a + b - 1) // b * b


def _params(sem=None):
    if sem is None:
        return pltpu.CompilerParams(vmem_limit_bytes=VMEM_LIMIT)
    return pltpu.CompilerParams(dimension_semantics=sem, vmem_limit_bytes=VMEM_LIMIT)


def _sigmoid(x):
    return 1.0 / (1.0 + jnp.exp(-x))


def _dot(a, b):
    return jnp.dot(a, b, preferred_element_type=F32)


def _dot_nt(a, b):
    return lax.dot_general(a, b, (((1,), (1,)), ((), ())), preferred_element_type=F32)


def _dot_tn(a, b):
    return lax.dot_general(a, b, (((0,), (0,)), ((), ())), preferred_element_type=F32)


def _rms(x, g):
    r = lax.rsqrt(jnp.mean(x * x, axis=-1, keepdims=True) + EPS)
    return x * r * g


def _rms_bwd(x, g, dy):
    r = lax.rsqrt(jnp.mean(x * x, axis=-1, keepdims=True) + EPS)
    xh = x * r
    q = dy * g
    dx = r * (q - xh * jnp.mean(q * xh, axis=-1, keepdims=True))
    return dx, dy * xh


def _ffn_up(n, wg, wu, name):
    tp, d = n.shape
    fp = wg.shape[2]
    tm = tp // MM_TILES

    def body(n_ref, wg_ref, wu_ref, a_ref, b_ref, s_ref):
        nn = n_ref[...]
        a = _dot(nn, wg_ref[...])
        b = _dot(nn, wu_ref[...])
        a_ref[...] = a.astype(BF)
        b_ref[...] = b.astype(BF)
        s_ref[...] = (a * _sigmoid(a) * b).astype(BF)

    out = jax.ShapeDtypeStruct((N_CHIP, tp, fp), BF)
    wspec = pl.BlockSpec((None, d, fp), lambda k, i: (k, 0, 0))
    ospec = pl.BlockSpec((None, tm, fp), lambda k, i: (k, i, 0))
    return pl.pallas_call(
        body, name=name, grid=(N_CHIP, MM_TILES),
        in_specs=[pl.BlockSpec((tm, d), lambda k, i: (i, 0)), wspec, wspec],
        out_specs=[ospec, ospec, ospec], out_shape=[out, out, out],
        compiler_params=_params(("arbitrary", "arbitrary")),
    )(n, wg, wu)


def _ffn_bwd_act(df, wd, a, b, name):
    tp, d = df.shape
    fp = wd.shape[1]
    tm = tp // MM_TILES

    def body(df_ref, wd_ref, a_ref, b_ref, da_ref, db_ref):
        ds = _dot_nt(df_ref[...], wd_ref[...])
        av = a_ref[...].astype(F32)
        bv = b_ref[...].astype(F32)
        sg = _sigmoid(av)
        da_ref[...] = (ds * bv * sg * (1.0 + av * (1.0 - sg))).astype(BF)
        db_ref[...] = (ds * av * sg).astype(BF)

    out = jax.ShapeDtypeStruct((N_CHIP, tp, fp), BF)
    aspec = pl.BlockSpec((None, tm, fp), lambda k, i: (k, i, 0))
    return pl.pallas_call(
        body, name=name, grid=(N_CHIP, MM_TILES),
        in_specs=[pl.BlockSpec((tm, d), lambda k, i: (i, 0)),
                  pl.BlockSpec((None, fp, d), lambda k, i: (k, 0, 0)), aspec, aspec],
        out_specs=[aspec, aspec], out_shape=[out, out],
        compiler_params=_params(("arbitrary", "arbitrary")),
    )(df, wd, a, b)


def _col_matmul(lhs, w, name, trans_b, out_dtype):
    tp, kd = lhs.shape
    nk = w.shape[0]
    nc = w.shape[1] if trans_b else w.shape[2]
    tm = tp // MM_TILES

    def body(l_ref, w_ref, o_ref):
        if trans_b:
            o_ref[...] = _dot_nt(l_ref[...], w_ref[...]).astype(out_dtype)
        else:
            o_ref[...] = _dot(l_ref[...], w_ref[...]).astype(out_dtype)

    return pl.pallas_call(
        body, name=name, grid=(nk, MM_TILES),
        in_specs=[pl.BlockSpec((tm, kd), lambda k, i: (i, 0)),
                  pl.BlockSpec((None,) + tuple(w.shape[1:]), lambda k, i: (k, 0, 0))],
        out_specs=pl.BlockSpec((tm, nc), lambda k, i: (i, k)),
        out_shape=jax.ShapeDtypeStruct((tp, nk * nc), out_dtype),
        compiler_params=_params(("arbitrary", "arbitrary")),
    )(lhs, w)


def _row_matmul(pairs, name, trans_b, d_out):
    l0 = pairs[0][0]
    tp = l0.shape[1] if l0.ndim == 3 else l0.shape[0]
    nk = pairs[0][1].shape[0]
    tm = tp // MM_TILES
    npair = len(pairs)

    def body(*refs):
        o_ref = refs[2 * npair]
        k = pl.program_id(1)
        part = None
        for q in range(npair):
            l = refs[2 * q][...]
            w = refs[2 * q + 1][...]
            t = _dot_nt(l, w) if trans_b else _dot(l, w)
            part = t if part is None else part + t

        @pl.when(k == 0)
        def _():
            o_ref[...] = part

        @pl.when(k > 0)
        def _():
            o_ref[...] += part

    in_specs, args = [], []
    for lhs, w in pairs:
        if lhs.ndim == 3:
            in_specs.append(pl.BlockSpec((None, tm, lhs.shape[2]), lambda i, k: (k, i, 0)))
        else:
            in_specs.append(pl.BlockSpec((tm, lhs.shape[1] // nk), lambda i, k: (i, k)))
        in_specs.append(pl.BlockSpec((None,) + tuple(w.shape[1:]), lambda i, k: (k, 0, 0)))
        args += [lhs, w]
    return pl.pallas_call(
        body, name=name, grid=(MM_TILES, nk), in_specs=in_specs,
        out_specs=pl.BlockSpec((tm, d_out), lambda i, k: (i, 0)),
        out_shape=jax.ShapeDtypeStruct((tp, d_out), F32),
        compiler_params=_params(("arbitrary", "arbitrary")),
    )(*args)


def _wgrad_call(x, y, name, x_width=None, y_width=None):
    tp = x.shape[1] if x.ndim == 3 else x.shape[0]
    tm = tp // MM_TILES

    def spec(a, width):
        if a.ndim == 3:
            return pl.BlockSpec((None, tm, a.shape[2]), lambda k, i: (k, i, 0)), a.shape[2]
        if width is None:
            return pl.BlockSpec((tm, a.shape[1]), lambda k, i: (i, 0)), a.shape[1]
        return pl.BlockSpec((tm, width), lambda k, i: (i, k)), width

    xs, p = spec(x, x_width)
    ys, q = spec(y, y_width)

    def body(x_ref, y_ref, o_ref, acc):
        i = pl.program_id(1)
        t = _dot_tn(x_ref[...], y_ref[...])

        @pl.when(i == 0)
        def _():
            acc[...] = t

        @pl.when(i > 0)
        def _():
            acc[...] += t

        @pl.when(i == MM_TILES - 1)
        def _():
            o_ref[...] = acc[...].astype(BF)

    return pl.pallas_call(
        body, name=name, grid=(N_CHIP, MM_TILES), in_specs=[xs, ys],
        out_specs=pl.BlockSpec((None, p, q), lambda k, i: (k, 0, 0)),
        out_shape=jax.ShapeDtypeStruct((N_CHIP, p, q), BF),
        scratch_shapes=[pltpu.VMEM((p, q), F32)],
        compiler_params=_params(("arbitrary", "arbitrary")),
    )(x, y)


def _row_call(body, name, tp, d, row_ins, vec_ins, row_out_dtypes, n_acc, acc_shape=None):
    te = tp // EW_TILES
    rspec = pl.BlockSpec((te, d), lambda i: (i, 0))
    vspec = pl.BlockSpec((1, d), lambda i: (0, 0))
    acc_shape = acc_shape or (1, d)
    aspec = pl.BlockSpec(acc_shape, lambda i: (0, 0))
    return pl.pallas_call(
        body, name=name, grid=(EW_TILES,),
        in_specs=[rspec] * len(row_ins) + [vspec] * len(vec_ins),
        out_specs=[rspec] * len(row_out_dtypes) + [aspec] * n_acc,
        out_shape=[jax.ShapeDtypeStruct((tp, d), dt) for dt in row_out_dtypes]
        + [jax.ShapeDtypeStruct(acc_shape, F32)] * n_acc,
        compiler_params=_params(("arbitrary",)),
    )(*row_ins, *vec_ins)


def _norm0(h, g):
    tp, d = h.shape

    def body(h_ref, g_ref, n_ref):
        n_ref[...] = _rms(h_ref[...], g_ref[...]).astype(BF)

    return _row_call(body, "norm0", tp, d, [h], [g], [BF], 0)[0]


def _post_fwd(f, h, g_post, g_next, scale, name):
    tp, d = h.shape

    def body(f_ref, h_ref, gp_ref, gn_ref, hn_ref, n_ref):
        hn = h_ref[...] + scale * _rms(f_ref[...], gp_ref[...])
        hn_ref[...] = hn
        n_ref[...] = _rms(hn, gn_ref[...]).astype(BF)

    return _row_call(body, name, tp, d, [f, h], [g_post, g_next], [F32, BF], 0)


def _loss_bwd(f, h, tgt, g_post, t_real):
    tp, d = h.shape
    te = tp // EW_TILES

    def body(f_ref, h_ref, t_ref, gp_ref, dh_ref, df_ref, dg_ref, loss_ref):
        i = pl.program_id(0)

        @pl.when(i == 0)
        def _():
            dg_ref[...] = jnp.zeros_like(dg_ref)
            loss_ref[...] = jnp.zeros_like(loss_ref)

        f = f_ref[...]
        gp = gp_ref[...]
        h3 = h_ref[...] + 0.5 * _rms(f, gp)
        rows = i * te + lax.broadcasted_iota(jnp.int32, (te, 1), 0)
        real = (rows >= N_META) & (rows < t_real)
        e = jnp.where(real, h3 - t_ref[...], 0.0)
        loss_ref[...] += 0.5 * jnp.sum(jnp.sum(e * e, axis=1, keepdims=True), axis=0, keepdims=True) / d
        dh = e / d
        dh_ref[...] = dh
        dfv, dgr = _rms_bwd(f, gp, 0.5 * dh)
        df_ref[...] = dfv.astype(BF)
        dg_ref[...] += jnp.sum(dgr, axis=0, keepdims=True)

    rspec = pl.BlockSpec((te, d), lambda i: (i, 0))
    vspec = pl.BlockSpec((1, d), lambda i: (0, 0))
    return pl.pallas_call(
        body, name="loss_bwd", grid=(EW_TILES,),
        in_specs=[rspec, rspec, rspec, vspec],
        out_specs=[rspec, rspec, vspec, pl.BlockSpec((1, 1), lambda i: (0, 0))],
        out_shape=[jax.ShapeDtypeStruct((tp, d), F32), jax.ShapeDtypeStruct((tp, d), BF),
                   jax.ShapeDtypeStruct((1, d), F32), jax.ShapeDtypeStruct((1, 1), F32)],
        compiler_params=_params(("arbitrary",)),
    )(f, h, tgt, g_post)


def _pre_bwd(dn, h, dh_out, g_pre, name, chain=None):
    tp, d = h.shape

    def body(*refs):
        if chain is None:
            dn_ref, h_ref, dho_ref, g_ref, dh_ref, dg_ref = refs
        else:
            dn_ref, h_ref, dho_ref, p_ref, g_ref, gp_ref, dh_ref, dp_ref, dg_ref, dgp_ref = refs
        i = pl.program_id(0)

        @pl.when(i == 0)
        def _():
            dg_ref[...] = jnp.zeros_like(dg_ref)
            if chain is not None:
                dgp_ref[...] = jnp.zeros_like(dgp_ref)

        dx, dgr = _rms_bwd(h_ref[...], g_ref[...], dn_ref[...])
        dh = dho_ref[...] + dx
        dh_ref[...] = dh
        dg_ref[...] += jnp.sum(dgr, axis=0, keepdims=True)
        if chain is not None:
            dp, dgpr = _rms_bwd(p_ref[...], gp_ref[...], chain[2] * dh)
            dp_ref[...] = dp.astype(BF)
            dgp_ref[...] += jnp.sum(dgpr, axis=0, keepdims=True)

    if chain is None:
        return _row_call(body, name, tp, d, [dn, h, dh_out], [g_pre], [F32], 1)
    return _row_call(body, name, tp, d, [dn, h, dh_out, chain[0]], [g_pre, chain[1]], [F32, BF], 2)


def _gelu(y):
    c = math.sqrt(2.0 / math.pi)
    return 0.5 * y * (1.0 + jnp.tanh(c * (y + 0.044715 * y * y * y)))


def _gelu_grad(y):
    c = math.sqrt(2.0 / math.pi)
    t = jnp.tanh(c * (y + 0.044715 * y * y * y))
    return 0.5 * (1.0 + t) + 0.5 * y * (1.0 - t * t) * c * (1.0 + 3.0 * 0.044715 * y * y)


def _neg_expm1(x):
    p = 1.0 + x * (1.0 / 9.0)
    for n in (8.0, 7.0, 6.0, 5.0, 4.0, 3.0, 2.0):
        p = 1.0 + x * (1.0 / n) * p
    return -jnp.where(x > -0.35, x * p, jnp.exp(x) - 1.0)


def _softplus(x):
    e = jnp.exp(-jnp.abs(x))
    w = 1.0 + e
    l1p = jnp.where(w == 1.0, e, jnp.log(w) * (e / jnp.where(w == 1.0, 1.0, w - 1.0)))
    return jnp.maximum(x, 0.0) + l1p


def _group_mean(v, gm):
    hi = v.astype(BF)
    lo = (v - hi.astype(F32)).astype(BF)
    return _dot(hi, gm) + _dot(lo, gm)


def _shift_dn(win, s, r):
    if s == 0:
        return win[8:8 + r]
    return pltpu.roll(win, s, 0)[8:8 + r]


def _shift_up(win, s, r):
    if s == 0:
        return win[0:r]
    return pltpu.roll(win, r + 8 - s, 0)[0:r]


def _window_dn(ref, t0, r, first):
    if first:
        return jnp.concatenate([jnp.zeros((8, ref.shape[1]), F32), ref[0:r, :]], axis=0)
    return ref[pl.ds(t0 - 8, r + 8), :]


def _tile_scan(a, u, reverse):
    r = a.shape[0]
    rid = lax.broadcasted_iota(jnp.int32, a.shape, 0) & 7
    for dlt in (1, 2, 4):
        sh = (r - dlt) if reverse else dlt
        a_s = pltpu.roll(a, sh, 0)
        u_s = pltpu.roll(u, sh, 0)
        keep = (rid + dlt <= 7) if reverse else (rid >= dlt)
        u = jnp.where(keep, u + a * u_s, u)
        a = jnp.where(keep, a * a_s, a)
    return a, u


def _lru_gates(xc, wa, ba, wx, bx, sp):
    xb = xc.astype(BF)
    ga = _sigmoid(_dot(xb, wa) + ba)
    gx = _sigmoid(_dot(xb, wx) + bx)
    la = -LRU_C * ga * sp
    return ga, gx, la


def _conv4(win, w4, cb, r):
    return (cb + w4[3:4] * _shift_dn(win, 0, r) + w4[2:3] * _shift_dn(win, 1, r)
            + w4[1:2] * _shift_dn(win, 2, r) + w4[0:1] * _shift_dn(win, 3, r))


def _lru_fwd(z, w4, cb, wa2, ba, wx2, bx, lam, g_out, gm):
    tp = z.shape[0]
    dl = cb.shape[1]
    nb = dl // LANE
    r = tp // MIX_CHUNKS
    c = LANE

    def body(y_ref, x_ref, w4_ref, cb_ref, wa_ref, ba_ref, wx_ref, bx_ref, lam_ref, go_ref, gm_ref, m_ref, hs_ref):
        w4v = w4_ref[...]
        cbv = cb_ref[...]
        wa = wa_ref[...]
        wx = wx_ref[...]
        bav = ba_ref[...]
        bxv = bx_ref[...]
        gov = go_ref[...]
        gmv = gm_ref[...]
        sp = _softplus(-lam_ref[...])

        def chunk(t0, hprev, first):
            win = _window_dn(x_ref, t0, r, first)
            xc = _conv4(win, w4v, cbv, r)
            ga, gx, la = _lru_gates(xc, wa, bav, wx, bxv, sp)
            a = jnp.exp(la)
            u = jnp.sqrt(_neg_expm1(2.0 * la)) * gx * xc
            ac, uc = _tile_scan(a, u, False)
            for j in range(r // 8):
                hj = uc[8 * j:8 * j + 8] + ac[8 * j:8 * j + 8] * hprev
                hs_ref[pl.ds(t0 + 8 * j, 8), :] = hj
                hprev = jnp.broadcast_to(hj[7:8], (8, c))
            h = hs_ref[pl.ds(t0, r), :]
            lo = h * _gelu(y_ref[pl.ds(t0, r), :])
            rs = lax.rsqrt(_group_mean(lo * lo, gmv) + EPS)
            m_ref[pl.ds(t0, r), :] = (lo * rs * gov).astype(BF)
            return hprev

        hp = chunk(0, jnp.zeros((8, c), F32), True)

        def loop(ci, hp):
            return chunk(pl.multiple_of(ci * r, 16), hp, False)

        lax.fori_loop(1, MIX_CHUNKS, loop, hp)

    col = lambda off: pl.BlockSpec((tp, c), lambda j: (0, off + j))
    vec = pl.BlockSpec((1, c), lambda j: (0, j))
    return pl.pallas_call(
        body, name="lru_fwd", grid=(nb,),
        in_specs=[col(0), col(nb), pl.BlockSpec((8, c), lambda j: (0, j)), vec,
                  pl.BlockSpec((None, c, c), lambda j: (j, 0, 0)), vec,
                  pl.BlockSpec((None, c, c), lambda j: (j, 0, 0)), vec, vec, vec,
                  pl.BlockSpec((c, c), lambda j: (0, 0))],
        out_specs=[col(0), col(0)],
        out_shape=[jax.ShapeDtypeStruct((tp, dl), BF), jax.ShapeDtypeStruct((tp, dl), F32)],
        compiler_params=_params(("arbitrary",)),
    )(z, z, w4, cb, wa2, ba, wx2, bx, lam, g_out, gm)


def _lru_bwd(z, hs, dmix, w4, cb, wa2, ba, wx2, bx, lam, g_out, gm):
    tp = z.shape[0]
    dl = cb.shape[1]
    nb = dl // LANE
    r = tp // MIX_CHUNKS
    c = LANE

    def body(y_ref, x_ref, hs_ref, dm_ref, w4_ref, cb_ref, wa_ref, ba_ref, wx_ref, bx_ref, lam_ref, go_ref, gm_ref,
             dy_ref, dx_ref, small_ref, dwa_ref, dwx_ref, xc_buf, ga_buf, gx_buf, a_buf, dh_buf, dxc_buf):
        w4v = w4_ref[...]
        cbv = cb_ref[...]
        wa = wa_ref[...]
        wx = wx_ref[...]
        bav = ba_ref[...]
        bxv = bx_ref[...]
        gov = go_ref[...]
        gmv = gm_ref[...]
        lamv = lam_ref[...]
        sp = _softplus(-lamv)
        small_ref[...] = jnp.zeros_like(small_ref)
        dwa_ref[...] = jnp.zeros_like(dwa_ref)
        dwx_ref[...] = jnp.zeros_like(dwx_ref)
        a_buf[pl.ds(tp, 8), :] = jnp.zeros((8, c), F32)
        dxc_buf[pl.ds(tp, 8), :] = jnp.zeros((8, c), F32)

        def fwd_chunk(t0, first):
            win = _window_dn(x_ref, t0, r, first)
            xc = _conv4(win, w4v, cbv, r)
            ga, gx, la = _lru_gates(xc, wa, bav, wx, bxv, sp)
            xc_buf[pl.ds(t0, r), :] = xc
            ga_buf[pl.ds(t0, r), :] = ga
            gx_buf[pl.ds(t0, r), :] = gx
            a_buf[pl.ds(t0, r), :] = jnp.exp(la)
            h = hs_ref[pl.ds(t0, r), :]
            yv = y_ref[pl.ds(t0, r), :]
            ge = _gelu(yv)
            lo = h * ge
            rs = lax.rsqrt(_group_mean(lo * lo, gmv) + EPS)
            xh = lo * rs
            dm = dm_ref[pl.ds(t0, r), :]
            q = dm * gov
            dlo = rs * (q - xh * _group_mean(q * xh, gmv))
            small_ref[8:9, :] += jnp.sum(dm * xh, axis=0, keepdims=True)
            dh_buf[pl.ds(t0, r), :] = dlo * ge
            dy_ref[pl.ds(t0, r), :] = (dlo * h * _gelu_grad(yv)).astype(BF)

        fwd_chunk(0, True)

        def floop(ci, carry):
            fwd_chunk(pl.multiple_of(ci * r, 16), False)
            return carry

        lax.fori_loop(1, MIX_CHUNKS, floop, 0)

        def bwd_chunk(t0, vnext, first):
            ap = _shift_up(a_buf[pl.ds(t0, r + 8), :], 1, r)
            ac, uc = _tile_scan(ap, dh_buf[pl.ds(t0, r), :], True)
            for j in reversed(range(r // 8)):
                vj = uc[8 * j:8 * j + 8] + ac[8 * j:8 * j + 8] * vnext
                dh_buf[pl.ds(t0 + 8 * j, 8), :] = vj
                vnext = jnp.broadcast_to(vj[0:1], (8, c))
            v = dh_buf[pl.ds(t0, r), :]
            hprev = _shift_dn(_window_dn(hs_ref, t0, r, first), 1, r)
            xc = xc_buf[pl.ds(t0, r), :]
            ga = ga_buf[pl.ds(t0, r), :]
            gx = gx_buf[pl.ds(t0, r), :]
            a = a_buf[pl.ds(t0, r), :]
            em = _neg_expm1(-2.0 * LRU_C * ga * sp)
            mult = jnp.sqrt(em)
            dla = v * hprev * a - (v * gx * xc) * ((1.0 - em) / mult)
            dgx = v * mult * xc
            dxc = v * mult * gx
            dga = dla * (-LRU_C) * sp
            small_ref[7:8, :] += jnp.sum(dla * (-LRU_C) * ga, axis=0, keepdims=True)
            dpa = dga * ga * (1.0 - ga)
            dpx = dgx * gx * (1.0 - gx)
            small_ref[5:6, :] += jnp.sum(dpa, axis=0, keepdims=True)
            small_ref[6:7, :] += jnp.sum(dpx, axis=0, keepdims=True)
            dpab = dpa.astype(BF)
            dpxb = dpx.astype(BF)
            xb = xc.astype(BF)
            dxc = dxc + _dot_nt(dpab, wa) + _dot_nt(dpxb, wx)
            dwa_ref[...] += _dot_tn(xb, dpab)
            dwx_ref[...] += _dot_tn(xb, dpxb)
            dxc_buf[pl.ds(t0, r), :] = dxc
            small_ref[4:5, :] += jnp.sum(dxc, axis=0, keepdims=True)
            dwin = dxc_buf[pl.ds(t0, r + 8), :]
            dx_ref[pl.ds(t0, r), :] = (w4v[3:4] * dxc + w4v[2:3] * _shift_up(dwin, 1, r)
                                       + w4v[1:2] * _shift_up(dwin, 2, r) + w4v[0:1] * _shift_up(dwin, 3, r)).astype(BF)
            xwin = _window_dn(x_ref, t0, r, first)
            for k in range(4):
                small_ref[k:k + 1, :] += jnp.sum(dxc * _shift_dn(xwin, 3 - k, r), axis=0, keepdims=True)
            return vnext

        def bloop(it, vnext):
            ci = MIX_CHUNKS - 1 - it
            return bwd_chunk(pl.multiple_of(ci * r, 16), vnext, False)

        vn = lax.fori_loop(0, MIX_CHUNKS - 1, bloop, jnp.zeros((8, c), F32))
        bwd_chunk(0, vn, True)
        small_ref[7:8, :] = small_ref[7:8, :] * (-_sigmoid(-lamv))

    col = lambda off: pl.BlockSpec((tp, c), lambda j: (0, off + j))
    vec = pl.BlockSpec((1, c), lambda j: (0, j))
    mat = pl.BlockSpec((None, c, c), lambda j: (j, 0, 0))
    buf = pltpu.VMEM((tp, c), F32)
    bufp = pltpu.VMEM((tp + 8, c), F32)
    return pl.pallas_call(
        body, name="lru_bwd", grid=(nb,),
        in_specs=[col(0), col(nb), col(0), col(0), pl.BlockSpec((8, c), lambda j: (0, j)), vec, mat, vec, mat, vec,
                  vec, vec, pl.BlockSpec((c, c), lambda j: (0, 0))],
        out_specs=[col(0), col(0), pl.BlockSpec((16, c), lambda j: (0, j)), mat, mat],
        out_shape=[jax.ShapeDtypeStruct((tp, dl), BF), jax.ShapeDtypeStruct((tp, dl), BF),
                   jax.ShapeDtypeStruct((16, dl), F32), jax.ShapeDtypeStruct((nb, c, c), F32),
                   jax.ShapeDtypeStruct((nb, c, c), F32)],
        scratch_shapes=[buf, buf, buf, bufp, buf, bufp],
        compiler_params=_params(("arbitrary",)),
    )(z, z, hs, dmix, w4, cb, wa2, ba, wx2, bx, lam, g_out, gm)


def _sc_conv(cvwin, w3, r):
    return w3[2:3] * _shift_dn(cvwin, 0, r) + w3[1:2] * _shift_dn(cvwin, 1, r) + w3[0:1] * _shift_dn(cvwin, 2, r)


def _sc_fwd(z, w3, g_out, gm, dl):
    tp = z.shape[0]
    nb = dl // LANE
    r = tp // MIX_CHUNKS
    c = LANE

    def body(b_ref, c_ref, v_ref, w3_ref, go_ref, gm_ref, m_ref):
        w3v = w3_ref[...]
        gov = go_ref[...]
        gmv = gm_ref[...]

        def chunk(t0, first):
            cvwin = _window_dn(c_ref, t0, r, first) * _window_dn(v_ref, t0, r, first)
            so = b_ref[pl.ds(t0, r), :] * _sc_conv(cvwin, w3v, r)
            rs = lax.rsqrt(_group_mean(so * so, gmv) + EPS)
            m_ref[pl.ds(t0, r), :] = (so * rs * gov).astype(BF)

        chunk(0, True)

        def loop(ci, carry):
            chunk(pl.multiple_of(ci * r, 16), False)
            return carry

        lax.fori_loop(1, MIX_CHUNKS, loop, 0)

    col = lambda off: pl.BlockSpec((tp, c), lambda j: (0, off + j))
    return pl.pallas_call(
        body, name="sconv_fwd", grid=(nb,),
        in_specs=[col(2 * nb), col(3 * nb), col(4 * nb), pl.BlockSpec((8, c), lambda j: (0, j)),
                  pl.BlockSpec((1, c), lambda j: (0, j)), pl.BlockSpec((c, c), lambda j: (0, 0))],
        out_specs=col(0), out_shape=jax.ShapeDtypeStruct((tp, dl), BF),
        compiler_params=_params(("arbitrary",)),
    )(z, z, z, w3, g_out, gm)


def _sc_bwd(z, dmix, w3, g_out, gm, dl):
    tp = z.shape[0]
    nb = dl // LANE
    r = tp // MIX_CHUNKS
    c = LANE

    def body(b_ref, c_ref, v_ref, dm_ref, w3_ref, go_ref, gm_ref, db_ref, dc_ref, dv_ref, small_ref, dsc_buf):
        w3v = w3_ref[...]
        gov = go_ref[...]
        gmv = gm_ref[...]
        small_ref[...] = jnp.zeros_like(small_ref)
        dsc_buf[pl.ds(tp, 8), :] = jnp.zeros((8, c), F32)

        def chunk1(t0, first):
            cvwin = _window_dn(c_ref, t0, r, first) * _window_dn(v_ref, t0, r, first)
            sc = _sc_conv(cvwin, w3v, r)
            bv = b_ref[pl.ds(t0, r), :]
            so = bv * sc
            rs = lax.rsqrt(_group_mean(so * so, gmv) + EPS)
            xh = so * rs
            dm = dm_ref[pl.ds(t0, r), :]
            q = dm * gov
            dso = rs * (q - xh * _group_mean(q * xh, gmv))
            small_ref[3:4, :] += jnp.sum(dm * xh, axis=0, keepdims=True)
            db_ref[pl.ds(t0, r), :] = (dso * sc).astype(BF)
            dsc = dso * bv
            dsc_buf[pl.ds(t0, r), :] = dsc
            for k in range(3):
                small_ref[k:k + 1, :] += jnp.sum(dsc * _shift_dn(cvwin, 2 - k, r), axis=0, keepdims=True)

        chunk1(0, True)

        def loop1(ci, carry):
            chunk1(pl.multiple_of(ci * r, 16), False)
            return carry

        lax.fori_loop(1, MIX_CHUNKS, loop1, 0)

        def loop2(ci, carry):
            t0 = pl.multiple_of(ci * r, 16)
            dwin = dsc_buf[pl.ds(t0, r + 8), :]
            dcv = w3v[2:3] * _shift_up(dwin, 0, r) + w3v[1:2] * _shift_up(dwin, 1, r) + w3v[0:1] * _shift_up(dwin, 2, r)
            dc_ref[pl.ds(t0, r), :] = (dcv * v_ref[pl.ds(t0, r), :]).astype(BF)
            dv_ref[pl.ds(t0, r), :] = (dcv * c_ref[pl.ds(t0, r), :]).astype(BF)
            return carry

        lax.fori_loop(0, MIX_CHUNKS, loop2, 0)

    col = lambda off: pl.BlockSpec((tp, c), lambda j: (0, off + j))
    out = jax.ShapeDtypeStruct((tp, dl), BF)
    return pl.pallas_call(
        body, name="sconv_bwd", grid=(nb,),
        in_specs=[col(2 * nb), col(3 * nb), col(4 * nb), col(nb), pl.BlockSpec((8, c), lambda j: (0, j)),
                  pl.BlockSpec((1, c), lambda j: (0, j)), pl.BlockSpec((c, c), lambda j: (0, 0))],
        out_specs=[col(0), col(0), col(0), pl.BlockSpec((8, c), lambda j: (0, j))],
        out_shape=[out, out, out, jax.ShapeDtypeStruct((8, dl), F32)],
        scratch_shapes=[pltpu.VMEM((tp + 8, c), F32)],
        compiler_params=_params(("arbitrary",)),
    )(z, z, z, dmix, w3, g_out, gm)


def _cast_pad(w, rows_p, cols_p, chip, name):
    r, c = w.shape

    def body(chip_ref, w_ref, o_ref):
        if (rows_p, cols_p) != (r, c):
            o_ref[...] = jnp.zeros_like(o_ref)
        o_ref[0:r, 0:c] = w_ref[...].astype(BF)

    return pl.pallas_call(
        body, name=name, out_shape=jax.ShapeDtypeStruct((N_CHIP, rows_p, cols_p), BF),
        grid_spec=pltpu.PrefetchScalarGridSpec(
            num_scalar_prefetch=1, grid=(1,),
            in_specs=[pl.BlockSpec((r, c), lambda i, chip: (0, 0))],
            out_specs=pl.BlockSpec((None, rows_p, cols_p), lambda i, chip: (chip[0], 0, 0))),
        compiler_params=_params(("arbitrary",)),
    )(chip, w)


def _adamw_math(w, g, m, v):
    m2 = ADAM_B1 * m + (1.0 - ADAM_B1) * g
    v2 = ADAM_B2 * v + (1.0 - ADAM_B2) * (g * g)
    m_hat = m2 / (1.0 - ADAM_B1 ** ADAM_STEP)
    v_hat = v2 / (1.0 - ADAM_B2 ** ADAM_STEP)
    delta = -ADAM_LR * (m_hat / (jnp.sqrt(v_hat) + ADAM_EPS) + ADAM_WD * w)
    return delta, m2, v2


def _adamw(w, g, m, v, name, row_tiles, col_tiles):
    r, c = w.shape
    tr = r // row_tiles
    tc = c // col_tiles
    gc = g.shape[1] if col_tiles == 1 else tc

    def body(w_ref, g_ref, m_ref, v_ref, go_ref, d_ref, mo_ref, vo_ref):
        gv = g_ref[...][:, 0:tc]
        delta, m2, v2 = _adamw_math(w_ref[...], gv, m_ref[...], v_ref[...])
        go_ref[...] = gv
        d_ref[...] = delta
        mo_ref[...] = m2
        vo_ref[...] = v2

    spec = pl.BlockSpec((tr, tc), lambda i, j: (i, j))
    out = jax.ShapeDtypeStruct((r, c), F32)
    return pl.pallas_call(
        body, name=name, grid=(row_tiles, col_tiles),
        in_specs=[spec, pl.BlockSpec((tr, gc), lambda i, j: (i, j)), spec, spec],
        out_specs=[spec] * 4, out_shape=[out] * 4,
        compiler_params=_params(("arbitrary", "arbitrary")),
    )(w, g, m, v)


def _adamw_small(w, g, m, v):
    def body(w_ref, g_ref, m_ref, v_ref, d_ref, mo_ref, vo_ref):
        delta, m2, v2 = _adamw_math(w_ref[...], g_ref[...], m_ref[...], v_ref[...])
        d_ref[...] = delta
        mo_ref[...] = m2
        vo_ref[...] = v2

    out = jax.ShapeDtypeStruct(w.shape, F32)
    spec = pl.BlockSpec(w.shape, lambda: (0, 0))
    return pl.pallas_call(body, name="adamw_small", in_specs=[spec] * 4, out_specs=[spec] * 3, out_shape=[out] * 3,
                          compiler_params=_params())(w, g, m, v)


def _place():
    x, y, c = lax.axis_index("x"), lax.axis_index("y"), lax.axis_index("c")
    chips = [(1 - x, y), (x, 1 - y), (1 - x, 1 - y)]
    return x, y, c, chips


ANY = pl.BlockSpec(memory_space=pl.ANY)


def _all_gather(bufs):
    n = len(bufs)

    def body(*refs):
        outs = refs[n:2 * n]
        s_ici, r_ici, s_d2d, r_d2d = refs[2 * n:]
        x, y, c, chips = _place()
        me = 2 * x + y

        def rows(w, chip, core):
            half = bufs[w].shape[1] // 2
            return outs[w].at[chip, pl.ds(core * half, half)]

        def ici(w, j, px, py):
            return pltpu.make_async_remote_copy(
                src_ref=rows(w, me, c), dst_ref=rows(w, me, c),
                send_sem=s_ici.at[w, j], recv_sem=r_ici.at[w, j], device_id=(px, py, c), device_id_type=MESH)

        def d2d(w, j, chip, core):
            return pltpu.make_async_remote_copy(
                src_ref=rows(w, chip, core), dst_ref=rows(w, chip, core),
                send_sem=s_d2d.at[w, j], recv_sem=r_d2d.at[w, j], device_id=(x, y, 1 - c), device_id_type=MESH)

        sends = [ici(w, j, px, py) for w in range(n) for j, (px, py) in enumerate(chips)]
        for cp in sends:
            cp.start()
        passed = []
        for w in range(n):
            for j, (px, py) in enumerate(chips):
                chip = 2 * px + py
                pltpu.make_async_remote_copy(
                    src_ref=rows(w, chip, c), dst_ref=rows(w, chip, c), send_sem=s_ici.at[w, j],
                    recv_sem=r_ici.at[w, j], device_id=(px, py, c), device_id_type=MESH).wait_recv()
                fw = d2d(w, j, chip, c)
                fw.start()
                passed.append(fw)
        for w in range(n):
            for j, (px, py) in enumerate(chips):
                d2d(w, j, 2 * px + py, 1 - c).wait_recv()
        for cp in sends + passed:
            cp.wait_send()

    return pl.pallas_call(
        body, name="all_gather_weights",
        out_shape=[jax.ShapeDtypeStruct(b.shape, b.dtype) for b in bufs],
        in_specs=[ANY] * n, out_specs=[ANY] * n, input_output_aliases={w: w for w in range(n)},
        scratch_shapes=[pltpu.SemaphoreType.DMA((n, 3)), pltpu.SemaphoreType.DMA((n, 3)),
                        pltpu.SemaphoreType.DMA((n, 3)), pltpu.SemaphoreType.DMA((n, 3))],
    )(*bufs)


def _pair_exchange(grads):
    n = len(grads)

    def body(*refs):
        ins, outs = refs[:n], refs[n:2 * n]
        ssem, rsem = refs[2 * n:]
        x, y, c, _ = _place()
        cps = []
        for w in range(n):
            half = grads[w].shape[1] // 2
            cps.append(pltpu.make_async_remote_copy(
                src_ref=ins[w].at[:, pl.ds((1 - c) * half, half)], dst_ref=outs[w],
                send_sem=ssem.at[w], recv_sem=rsem.at[w], device_id=(x, y, 1 - c), device_id_type=MESH))
        for cp in cps:
            cp.start()
        for cp in cps:
            cp.wait()

    return pl.pallas_call(
        body, name="grad_pair_exchange",
        out_shape=[jax.ShapeDtypeStruct((N_CHIP, g.shape[1] // 2, g.shape[2]), BF) for g in grads],
        in_specs=[ANY] * n, out_specs=[ANY] * n,
        scratch_shapes=[pltpu.SemaphoreType.DMA((n,)), pltpu.SemaphoreType.DMA((n,))],
    )(*grads)


def _pair_sum(g, sib, core, name):
    _, r, cdim = g.shape
    half = r // 2

    def body(core_ref, g_ref, s_ref, o_ref):
        o_ref[...] = (g_ref[...].astype(F32) + s_ref[...].astype(F32)).astype(BF)

    return pl.pallas_call(
        body, name=name,
        grid_spec=pltpu.PrefetchScalarGridSpec(
            num_scalar_prefetch=1, grid=(N_CHIP,),
            in_specs=[pl.BlockSpec((None, half, cdim), lambda k, core: (k, core[0], 0)),
                      pl.BlockSpec((None, half, cdim), lambda k, core: (k, 0, 0))],
            out_specs=pl.BlockSpec((None, half, cdim), lambda k, core: (k, 0, 0))),
        out_shape=jax.ShapeDtypeStruct((N_CHIP, half, cdim), BF),
        compiler_params=_params(("arbitrary",)),
    )(core, g, sib)


def _chip_exchange(psums):
    n = len(psums)

    def body(*refs):
        ins, outs = refs[:n], refs[n:2 * n]
        ssem, rsem = refs[2 * n:]
        x, y, c, chips = _place()
        cps = []
        for w in range(n):
            for j, (px, py) in enumerate(chips):
                cps.append(pltpu.make_async_remote_copy(
                    src_ref=ins[w].at[2 * px + py], dst_ref=outs[w].at[j],
                    send_sem=ssem.at[w, j], recv_sem=rsem.at[w, j], device_id=(px, py, c), device_id_type=MESH))
        for cp in cps:
            cp.start()
        for cp in cps:
            cp.wait()

    return pl.pallas_call(
        body, name="grad_chip_exchange",
        out_shape=[jax.ShapeDtypeStruct((3,) + p.shape[1:], BF) for p in psums],
        in_specs=[ANY] * n, out_specs=[ANY] * n,
        scratch_shapes=[pltpu.SemaphoreType.DMA((n, 3)), pltpu.SemaphoreType.DMA((n, 3))],
    )(*psums)


def _final_sum(g, sib, recv, sel, name):
    _, r, cdim = g.shape
    half = r // 2
    nt = 4
    th = half // nt

    def body(sel_ref, g_ref, s_ref, r_ref, o_ref):
        acc = g_ref[...].astype(F32) + s_ref[...].astype(F32)
        for j in range(3):
            acc = acc + r_ref[j].astype(F32)
        o_ref[...] = acc

    return pl.pallas_call(
        body, name=name,
        grid_spec=pltpu.PrefetchScalarGridSpec(
            num_scalar_prefetch=1, grid=(nt,),
            in_specs=[pl.BlockSpec((None, th, cdim), lambda i, sel: (sel[0], sel[1] * nt + i, 0)),
                      pl.BlockSpec((None, th, cdim), lambda i, sel: (sel[0], i, 0)),
                      pl.BlockSpec((3, th, cdim), lambda i, sel: (0, i, 0))],
            out_specs=pl.BlockSpec((th, cdim), lambda i, sel: (sel[1] * nt + i, 0))),
        out_shape=jax.ShapeDtypeStruct((r, cdim), F32),
        compiler_params=_params(("arbitrary",)),
    )(sel, g, sib, recv)


def _join_halves(bufs):
    n = len(bufs)

    def body(*refs):
        outs = refs[n:2 * n]
        ssem, rsem = refs[2 * n:]
        x, y, c, _ = _place()
        cps = []
        for w in range(n):
            half = bufs[w].shape[0] // 2
            mine = outs[w].at[pl.ds(c * half, half)]
            cps.append(pltpu.make_async_remote_copy(
                src_ref=mine, dst_ref=mine, send_sem=ssem.at[w], recv_sem=rsem.at[w],
                device_id=(x, y, 1 - c), device_id_type=MESH))
        for cp in cps:
            cp.start()
        for w in range(n):
            half = bufs[w].shape[0] // 2
            theirs = outs[w].at[pl.ds((1 - c) * half, half)]
            pltpu.make_async_remote_copy(
                src_ref=theirs, dst_ref=theirs, send_sem=ssem.at[w], recv_sem=rsem.at[w],
                device_id=(x, y, 1 - c), device_id_type=MESH).wait_recv()
        for cp in cps:
            cp.wait_send()

    return pl.pallas_call(
        body, name="grad_join_halves",
        out_shape=[jax.ShapeDtypeStruct(b.shape, F32) for b in bufs],
        in_specs=[ANY] * n, out_specs=[ANY] * n, input_output_aliases={w: w for w in range(n)},
        scratch_shapes=[pltpu.SemaphoreType.DMA((n,)), pltpu.SemaphoreType.DMA((n,))],
    )(*bufs)


def _small_all_reduce(buf, name):
    rows, d = buf.shape

    def body(in_ref, out_ref, sib, all4, ssem, rsem, psem, qsem):
        x, y, c, chips = _place()
        me = 2 * x + y
        to_sib = pltpu.make_async_remote_copy(src_ref=in_ref, dst_ref=sib, send_sem=ssem, recv_sem=rsem,
                                              device_id=(x, y, 1 - c), device_id_type=MESH)
        to_sib.start()
        to_sib.wait()
        all4[me] = in_ref[...] + sib[...]
        cps = [pltpu.make_async_remote_copy(src_ref=all4.at[me], dst_ref=all4.at[me], send_sem=psem.at[j],
                                            recv_sem=qsem.at[j], device_id=(px, py, c), device_id_type=MESH)
               for j, (px, py) in enumerate(chips)]
        for cp in cps:
            cp.start()
        for j, (px, py) in enumerate(chips):
            chip = 2 * px + py
            pltpu.make_async_remote_copy(src_ref=all4.at[chip], dst_ref=all4.at[chip], send_sem=psem.at[j],
                                         recv_sem=qsem.at[j], device_id=(px, py, c), device_id_type=MESH).wait_recv()
        for cp in cps:
            cp.wait_send()
        out_ref[...] = (all4[0] + all4[1]) + (all4[2] + all4[3])

    vm = pl.BlockSpec(memory_space=pltpu.VMEM)
    return pl.pallas_call(
        body, name=name, out_shape=jax.ShapeDtypeStruct((rows, d), F32),
        in_specs=[vm], out_specs=vm,
        scratch_shapes=[pltpu.VMEM((rows, d), F32), pltpu.VMEM((N_CHIP, rows, d), F32),
                        pltpu.SemaphoreType.DMA, pltpu.SemaphoreType.DMA,
                        pltpu.SemaphoreType.DMA((3,)), pltpu.SemaphoreType.DMA((3,))],
        compiler_params=_params(),
    )(buf)


def _pair_blocks(w):
    w4 = w.reshape(N_HEADS // 2, 2, HEAD, HEAD)
    eye = jnp.eye(2, dtype=w.dtype)
    return jnp.einsum("pirc,ij->pirjc", w4, eye).reshape(N_HEADS // 2, LANE, LANE)


def _unpair_blocks(w2):
    w5 = w2.reshape(N_HEADS // 2, 2, HEAD, 2, HEAD)
    return jnp.stack([w5[:, 0, :, 0, :], w5[:, 1, :, 1, :]], axis=1).reshape(N_HEADS, HEAD, HEAD)


def kernel(x, meta_tokens, ffn1_pre_g, ffn1_w_gate, ffn1_w_up, ffn1_w_down, ffn1_post_g, mix_pre_g, w_in, lru_conv_w, lru_conv_b, lru_w_a, lru_b_a, lru_w_x, lru_b_x, lru_lambda, sconv_w, lru_out_g, sconv_out_g, w_out, mix_post_g, ffn2_pre_g, ffn2_w_gate, ffn2_w_up, ffn2_w_down, ffn2_post_g, loss_target, m_meta_tokens, m_ffn1_pre_g, m_ffn1_w_gate, m_ffn1_w_up, m_ffn1_w_down, m_ffn1_post_g, m_mix_pre_g, m_w_in, m_lru_conv_w, m_lru_conv_b, m_lru_w_a, m_lru_b_a, m_lru_w_x, m_lru_b_x, m_lru_lambda, m_sconv_w, m_lru_out_g, m_sconv_out_g, m_w_out, m_mix_post_g, m_ffn2_pre_g, m_ffn2_w_gate, m_ffn2_w_up, m_ffn2_w_down, m_ffn2_post_g, v_meta_tokens, v_ffn1_pre_g, v_ffn1_w_gate, v_ffn1_w_up, v_ffn1_w_down, v_ffn1_post_g, v_mix_pre_g, v_w_in, v_lru_conv_w, v_lru_conv_b, v_lru_w_a, v_lru_b_a, v_lru_w_x, v_lru_b_x, v_lru_lambda, v_sconv_w, v_lru_out_g, v_sconv_out_g, v_w_out, v_mix_post_g, v_ffn2_pre_g, v_ffn2_w_gate, v_ffn2_w_up, v_ffn2_w_down, v_ffn2_post_g):
    seq, d = x.shape[1], x.shape[2]
    t_real = N_META + seq
    tp = _round_up(t_real, ROW_ALIGN)
    f4 = ffn1_w_gate.shape[2]
    f4p = _round_up(f4, LANE)
    dl = lru_conv_b.shape[1]
    cin = w_in.shape[2]
    xi, yi, ci = lax.axis_index("x"), lax.axis_index("y"), lax.axis_index("c")
    chip = 2 * xi + yi
    zero = jnp.zeros((), jnp.int32)

    big = {
        "ffn1_w_gate": (ffn1_w_gate[0], d, f4p), "ffn1_w_up": (ffn1_w_up[0], d, f4p),
        "ffn1_w_down": (ffn1_w_down[0], f4p, d), "w_in": (w_in[0], d, cin), "w_out": (w_out[0], w_out.shape[1], d),
        "ffn2_w_gate": (ffn2_w_gate[0], d, f4p), "ffn2_w_up": (ffn2_w_up[0], d, f4p),
        "ffn2_w_down": (ffn2_w_down[0], f4p, d),
    }
    names = list(big)
    chip1 = jnp.reshape(chip, (1,)).astype(jnp.int32)
    shards = [_cast_pad(big[k][0], big[k][1], big[k][2], chip1, "cast_" + k) for k in names]
    full = dict(zip(names, _all_gather(shards)))

    gm = jnp.kron(jnp.eye(2, dtype=F32), jnp.full((HEAD, HEAD), 1.0 / HEAD, F32)).astype(BF)
    wa2 = _pair_blocks(lru_w_a[0])
    wx2 = _pair_blocks(lru_w_x[0])

    dlq = dl // N_CHIP
    dq = d // N_CHIP
    R_GAIN, R_LOSS, R_META, R_LRU, R_SC, R_WA = 0, 6, 8, 24, 40, 48
    n_wrows = (N_HEADS // 2) * LANE * LANE // d
    R_WX = R_WA + n_wrows
    R_END = R_WX + n_wrows

    def pack(gains, meta, lru16, sc8, wa_, wx_, loss=None):
        rows = [jnp.concatenate(gains, axis=0)]
        lossrow = jnp.zeros((2, d), F32)
        if loss is not None:
            lossrow = lossrow.at[0, 0].set(loss)
        rows.append(lossrow)
        rows.append(meta)
        rows.append(jnp.concatenate([lru16, jnp.zeros((16, d - dl), F32)], axis=1))
        rows.append(jnp.concatenate([sc8, jnp.zeros((8, d - dl), F32)], axis=1))
        rows.append(wa_.reshape(n_wrows, d))
        rows.append(wx_.reshape(n_wrows, d))
        return jnp.concatenate(rows, axis=0)

    def place_cols(blk, width, total):
        return lax.dynamic_update_slice(jnp.zeros((blk.shape[0], total), F32), blk, (zero, chip * width))

    def pack_params(meta_, g1pre, g1post, gmpre, gmpost, g2pre, g2post, cw, cbias, wa_, ba_, wx_, bx_, lam_, sw, lgo, sgo):
        lru16 = jnp.concatenate([place_cols(cw[0], dlq, dl), cbias, ba_, bx_, lam_, lgo, jnp.zeros((7, dl), F32)], axis=0)
        sc8 = jnp.concatenate([place_cols(sw[0], dlq, dl), sgo, jnp.zeros((4, dl), F32)], axis=0)
        return pack([g1pre, g1post, gmpre, gmpost, g2pre, g2post], place_cols(meta_, dq, d), lru16, sc8,
                    _pair_blocks(wa_[0]), _pair_blocks(wx_[0]))

    p_w = pack_params(meta_tokens, ffn1_pre_g, ffn1_post_g, mix_pre_g, mix_post_g, ffn2_pre_g, ffn2_post_g, lru_conv_w,
                      lru_conv_b, lru_w_a, lru_b_a, lru_w_x, lru_b_x, lru_lambda, sconv_w, lru_out_g, sconv_out_g)
    p_m = pack_params(m_meta_tokens, m_ffn1_pre_g, m_ffn1_post_g, m_mix_pre_g, m_mix_post_g, m_ffn2_pre_g, m_ffn2_post_g,
                      m_lru_conv_w, m_lru_conv_b, m_lru_w_a, m_lru_b_a, m_lru_w_x, m_lru_b_x, m_lru_lambda, m_sconv_w,
                      m_lru_out_g, m_sconv_out_g)
    p_v = pack_params(v_meta_tokens, v_ffn1_pre_g, v_ffn1_post_g, v_mix_pre_g, v_mix_post_g, v_ffn2_pre_g, v_ffn2_post_g,
                      v_lru_conv_w, v_lru_conv_b, v_lru_w_a, v_lru_b_a, v_lru_w_x, v_lru_b_x, v_lru_lambda, v_sconv_w,
                      v_lru_out_g, v_sconv_out_g)

    gathered = _small_all_reduce(jnp.where(ci == 0, p_w, 0.0)[R_META:R_WA], "small_weight_gather")
    meta_full = gathered[0:N_META]
    w4_full = gathered[R_LRU - R_META:R_LRU - R_META + 4, 0:dl]
    w3_full = gathered[R_SC - R_META:R_SC - R_META + 3, 0:dl]
    w4p = jnp.concatenate([w4_full, jnp.zeros((4, dl), F32)], axis=0)
    w3p = jnp.concatenate([w3_full, jnp.zeros((5, dl), F32)], axis=0)

    h0 = jnp.concatenate([meta_full, x[0], jnp.zeros((tp - t_real, d), F32)], axis=0)
    tgt = jnp.concatenate([jnp.zeros((N_META, d), F32), loss_target[0], jnp.zeros((tp - t_real, d), F32)], axis=0)

    n1 = _norm0(h0, ffn1_pre_g)
    a1, b1, s1 = _ffn_up(n1, full["ffn1_w_gate"], full["ffn1_w_up"], "ffn1_up")
    f1 = _row_matmul([(s1, full["ffn1_w_down"])], "ffn1_down", False, d)
    h1, u = _post_fwd(f1, h0, ffn1_post_g, mix_pre_g, 0.5, "ffn1_post")
    z = _col_matmul(u, full["w_in"], "in_proj", False, F32)
    m_lru, hs = _lru_fwd(z, w4p, lru_conv_b, wa2.astype(BF), lru_b_a, wx2.astype(BF), lru_b_x, lru_lambda, lru_out_g, gm)
    m_sc = _sc_fwd(z, w3p, sconv_out_g, gm, dl)
    mixed = jnp.concatenate([m_lru, m_sc], axis=1)
    p = _row_matmul([(mixed, full["w_out"])], "out_proj", False, d)
    h2, n2 = _post_fwd(p, h1, mix_post_g, ffn2_pre_g, 1.0, "mix_post")
    a2, b2, s2 = _ffn_up(n2, full["ffn2_w_gate"], full["ffn2_w_up"], "ffn2_up")
    f2 = _row_matmul([(s2, full["ffn2_w_down"])], "ffn2_down", False, d)
    dh3, df2, dg_ffn2_post, loss_part = _loss_bwd(f2, h2, tgt, ffn2_post_g, t_real)

    da2, db2 = _ffn_bwd_act(df2, full["ffn2_w_down"], a2, b2, "ffn2_bwd_act")
    g_ffn2_down = _wgrad_call(s2, df2, "ffn2_down_wgrad")
    g_ffn2_gate = _wgrad_call(n2, da2, "ffn2_gate_wgrad")
    g_ffn2_up = _wgrad_call(n2, db2, "ffn2_up_wgrad")
    dn2 = _row_matmul([(da2, full["ffn2_w_gate"]), (db2, full["ffn2_w_up"])], "ffn2_bwd_up", True, d)
    dh2, dp, dg_ffn2_pre, dg_mix_post = _pre_bwd(dn2, h2, dh3, ffn2_pre_g, "ffn2_pre_bwd", (p, mix_post_g, 1.0))
    dmixed = _col_matmul(dp, full["w_out"], "out_proj_bwd", True, F32)
    g_w_out = _wgrad_call(mixed, dp, "w_out_wgrad", x_width=mixed.shape[1] // N_CHIP)
    dzy, dzx, lru_small, dwa2, dwx2 = _lru_bwd(z, hs, dmixed, w4p, lru_conv_b, wa2.astype(BF), lru_b_a, wx2.astype(BF),
                                               lru_b_x, lru_lambda, lru_out_g, gm)
    dzb, dzc, dzv, sc_small = _sc_bwd(z, dmixed, w3p, sconv_out_g, gm, dl)
    dz = jnp.concatenate([dzy, dzx, dzb, dzc, dzv], axis=1)
    g_w_in = _wgrad_call(u, dz, "w_in_wgrad", y_width=cin)
    du = _row_matmul([(dz, full["w_in"])], "in_proj_bwd", True, d)
    dh1, df1, dg_mix_pre, dg_ffn1_post = _pre_bwd(du, h1, dh2, mix_pre_g, "mix_pre_bwd", (f1, ffn1_post_g, 0.5))
    da1, db1 = _ffn_bwd_act(df1, full["ffn1_w_down"], a1, b1, "ffn1_bwd_act")
    g_ffn1_down = _wgrad_call(s1, df1, "ffn1_down_wgrad")
    g_ffn1_gate = _wgrad_call(n1, da1, "ffn1_gate_wgrad")
    g_ffn1_up = _wgrad_call(n1, db1, "ffn1_up_wgrad")
    dn1 = _row_matmul([(da1, full["ffn1_w_gate"]), (db1, full["ffn1_w_up"])], "ffn1_bwd_up", True, d)
    dh0, dg_ffn1_pre = _pre_bwd(dn1, h0, dh1, ffn1_pre_g, "ffn1_pre_bwd")

    grad_x = dh0[N_META:t_real][None]

    p_g_local = pack([dg_ffn1_pre, dg_ffn1_post, dg_mix_pre, dg_mix_post, dg_ffn2_pre, dg_ffn2_post], dh0[0:N_META],
                     lru_small, sc_small, dwa2, dwx2, loss=loss_part[0, 0])
    p_g = _small_all_reduce(p_g_local, "small_grad_all_reduce")
    loss = p_g[R_LOSS, 0]
    p_delta, p_newm, p_newv = _adamw_small(p_w, p_g, p_m, p_v)

    def unpack(buf):
        out = {}
        for i, k in enumerate(["ffn1_pre_g", "ffn1_post_g", "mix_pre_g", "mix_post_g", "ffn2_pre_g", "ffn2_post_g"]):
            out[k] = buf[R_GAIN + i:R_GAIN + i + 1]
        out["meta_tokens"] = lax.dynamic_slice(buf[R_META:R_META + N_META], (zero, chip * dq), (N_META, dq))
        lru = buf[R_LRU:R_LRU + 16, 0:dl]
        out["lru_conv_w"] = lax.dynamic_slice(lru[0:4], (zero, chip * dlq), (4, dlq))[None]
        out["lru_conv_b"] = lru[4:5]
        out["lru_b_a"] = lru[5:6]
        out["lru_b_x"] = lru[6:7]
        out["lru_lambda"] = lru[7:8]
        out["lru_out_g"] = lru[8:9]
        sc = buf[R_SC:R_SC + 8, 0:dl]
        out["sconv_w"] = lax.dynamic_slice(sc[0:3], (zero, chip * dlq), (3, dlq))[None]
        out["sconv_out_g"] = sc[3:4]
        out["lru_w_a"] = _unpair_blocks(buf[R_WA:R_WX].reshape(N_HEADS // 2, LANE, LANE))[None]
        out["lru_w_x"] = _unpair_blocks(buf[R_WX:R_END].reshape(N_HEADS // 2, LANE, LANE))[None]
        return out

    s_grad, s_delta, s_newm, s_newv = unpack(p_g), unpack(p_delta), unpack(p_newm), unpack(p_newv)

    g_big = {"ffn1_w_gate": g_ffn1_gate, "ffn1_w_up": g_ffn1_up, "ffn1_w_down": g_ffn1_down, "w_in": g_w_in,
             "w_out": g_w_out, "ffn2_w_gate": g_ffn2_gate, "ffn2_w_up": g_ffn2_up, "ffn2_w_down": g_ffn2_down}
    glist = [g_big[k] for k in names]
    sibs = _pair_exchange(glist)
    core = jnp.reshape(ci, (1,)).astype(jnp.int32)
    psums = [_pair_sum(g, s, core, "pair_sum_" + k) for g, s, k in zip(glist, sibs, names)]
    recvs = _chip_exchange(psums)
    sel = jnp.stack([chip, ci]).astype(jnp.int32)
    halves = [_final_sum(g, s, r_, sel, "final_sum_" + k) for g, s, r_, k in zip(glist, sibs, recvs, names)]
    gfull = dict(zip(names, _join_halves(halves)))

    w_big = {"ffn1_w_gate": ffn1_w_gate, "ffn1_w_up": ffn1_w_up, "ffn1_w_down": ffn1_w_down, "w_in": w_in, "w_out": w_out,
             "ffn2_w_gate": ffn2_w_gate, "ffn2_w_up": ffn2_w_up, "ffn2_w_down": ffn2_w_down}
    m_big = {"ffn1_w_gate": m_ffn1_w_gate, "ffn1_w_up": m_ffn1_w_up, "ffn1_w_down": m_ffn1_w_down, "w_in": m_w_in,
             "w_out": m_w_out, "ffn2_w_gate": m_ffn2_w_gate, "ffn2_w_up": m_ffn2_w_up, "ffn2_w_down": m_ffn2_w_down}
    v_big = {"ffn1_w_gate": v_ffn1_w_gate, "ffn1_w_up": v_ffn1_w_up, "ffn1_w_down": v_ffn1_w_down, "w_in": v_w_in,
             "w_out": v_w_out, "ffn2_w_gate": v_ffn2_w_gate, "ffn2_w_up": v_ffn2_w_up, "ffn2_w_down": v_ffn2_w_down}
    b_grad, b_delta, b_newm, b_newv = {}, {}, {}, {}
    for k in names:
        wide_rows = w_big[k].shape[1] % 64 == 0
        g_, d_, m_, v_ = _adamw(w_big[k][0], gfull[k], m_big[k][0], v_big[k][0], "adamw_" + k,
                                8 if wide_rows else 4, 1 if wide_rows else 2)
        b_grad[k], b_delta[k], b_newm[k], b_newv[k] = g_[None], d_[None], m_[None], v_[None]

    order = ["meta_tokens", "ffn1_pre_g", "ffn1_w_gate", "ffn1_w_up", "ffn1_w_down", "ffn1_post_g", "mix_pre_g", "w_in",
             "lru_conv_w", "lru_conv_b", "lru_w_a", "lru_b_a", "lru_w_x", "lru_b_x", "lru_lambda", "sconv_w", "lru_out_g",
             "sconv_out_g", "w_out", "mix_post_g", "ffn2_pre_g", "ffn2_w_gate", "ffn2_w_up", "ffn2_w_down", "ffn2_post_g"]

    def pick(small, bigd):
        return [bigd[k] if k in bigd else small[k] for k in order]

    return (loss, grad_x, *pick(s_grad, b_grad), *pick(s_delta, b_delta), *pick(s_newm, b_newm), *pick(s_newv, b_newv))
```

```python
import functools
import math

import jax
import jax.numpy as jnp
from jax import lax
from jax.experimental import pallas as pl
from jax.experimental.pallas import tpu as pltpu

F32 = jnp.float32
BF = jnp.bfloat16
MESH = pl.DeviceIdType.MESH

EPS = 1e-6
N_META = 16
N_HEADS = 16
HEAD = 64
LRU_C = 8.0
LANE = 128
N_CHIP = 4
ROW_ALIGN = 384
MM_TILES = 8
EW_TILES = 12
MIX_CHUNKS = 24
WGRAD_TILE_X = 256
WGRAD_TILE_Y = 512
VMEM_LIMIT = 56 << 20

ADAM_LR = 0.001
ADAM_B1 = 0.9
ADAM_B2 = 0.999
ADAM_EPS = 1e-08
ADAM_WD = 0.01
ADAM_STEP = 10


def _round_up(a, b):
    return (a + b - 1) // b * b


def _params(sem=None):
    if sem is None:
        return pltpu.CompilerParams(vmem_limit_bytes=VMEM_LIMIT)
    return pltpu.CompilerParams(dimension_semantics=sem, vmem_limit_bytes=VMEM_LIMIT)


def _sigmoid(x):
    return 1.0 / (1.0 + jnp.exp(-x))


def _dot(a, b):
    return jnp.dot(a, b, preferred_element_type=F32)


def _dot_nt(a, b):
    return lax.dot_general(a, b, (((1,), (1,)), ((), ())), preferred_element_type=F32)


def _dot_tn(a, b):
    return lax.dot_general(a, b, (((0,), (0,)), ((), ())), preferred_element_type=F32)


def _rms(x, g):
    r = lax.rsqrt(jnp.mean(x * x, axis=-1, keepdims=True) + EPS)
    return x * r * g


def _rms_bwd(x, g, dy):
    r = lax.rsqrt(jnp.mean(x * x, axis=-1, keepdims=True) + EPS)
    xh = x * r
    q = dy * g
    dx = r * (q - xh * jnp.mean(q * xh, axis=-1, keepdims=True))
    return dx, dy * xh


def _ffn_up(n, wg, wu, name):
    tp, d = n.shape
    fp = wg.shape[1]
    tm = tp // MM_TILES

    def body(n_ref, wg_ref, wu_ref, a_ref, b_ref, s_ref):
        nn = n_ref[...]
        a = _dot_nt(nn, wg_ref[...])
        b = _dot_nt(nn, wu_ref[...])
        a_ref[...] = a.astype(BF)
        b_ref[...] = b.astype(BF)
        s_ref[...] = (a * _sigmoid(a) * b).astype(BF)

    out = jax.ShapeDtypeStruct((N_CHIP, tp, fp), BF)
    wspec = pl.BlockSpec((None, fp, d), lambda k, i: (k, 0, 0))
    ospec = pl.BlockSpec((None, tm, fp), lambda k, i: (k, i, 0))
    return pl.pallas_call(
        body, name=name, grid=(N_CHIP, MM_TILES),
        in_specs=[pl.BlockSpec((tm, d), lambda k, i: (i, 0)), wspec, wspec],
        out_specs=[ospec, ospec, ospec], out_shape=[out, out, out],
        compiler_params=_params(("arbitrary", "arbitrary")),
    )(n, wg, wu)


def _ffn_bwd_act(df, wd, a, b, name):
    tp, d = df.shape
    fp = wd.shape[1]
    tm = tp // MM_TILES

    def body(df_ref, wd_ref, a_ref, b_ref, da_ref, db_ref):
        ds = _dot_nt(df_ref[...], wd_ref[...])
        av = a_ref[...].astype(F32)
        bv = b_ref[...].astype(F32)
        sg = _sigmoid(av)
        da_ref[...] = (ds * bv * sg * (1.0 + av * (1.0 - sg))).astype(BF)
        db_ref[...] = (ds * av * sg).astype(BF)

    out = jax.ShapeDtypeStruct((N_CHIP, tp, fp), BF)
    aspec = pl.BlockSpec((None, tm, fp), lambda k, i: (k, i, 0))
    return pl.pallas_call(
        body, name=name, grid=(N_CHIP, MM_TILES),
        in_specs=[pl.BlockSpec((tm, d), lambda k, i: (i, 0)),
                  pl.BlockSpec((None, fp, d), lambda k, i: (k, 0, 0)), aspec, aspec],
        out_specs=[aspec, aspec], out_shape=[out, out],
        compiler_params=_params(("arbitrary", "arbitrary")),
    )(df, wd, a, b)


def _col_matmul(lhs, w, name, trans_b, out_dtype):
    tp, kd = lhs.shape
    nk = w.shape[0]
    nc = w.shape[1] if trans_b else w.shape[2]
    tm = tp // MM_TILES

    def body(l_ref, w_ref, o_ref):
        if trans_b:
            o_ref[...] = _dot_nt(l_ref[...], w_ref[...]).astype(out_dtype)
        else:
            o_ref[...] = _dot(l_ref[...], w_ref[...]).astype(out_dtype)

    return pl.pallas_call(
        body, name=name, grid=(nk, MM_TILES),
        in_specs=[pl.BlockSpec((tm, kd), lambda k, i: (i, 0)),
                  pl.BlockSpec((None,) + tuple(w.shape[1:]), lambda k, i: (k, 0, 0))],
        out_specs=pl.BlockSpec((tm, nc), lambda k, i: (i, k)),
        out_shape=jax.ShapeDtypeStruct((tp, nk * nc), out_dtype),
        compiler_params=_params(("arbitrary", "arbitrary")),
    )(lhs, w)


def _row_matmul(pairs, name, trans_b, d_out):
    l0 = pairs[0][0]
    tp = l0.shape[1] if l0.ndim == 3 else l0.shape[0]
    nk = pairs[0][1].shape[0]
    tm = tp // MM_TILES
    npair = len(pairs)

    def body(*refs):
        o_ref = refs[2 * npair]
        k = pl.program_id(1)
        part = None
        for q in range(npair):
            l = refs[2 * q][...]
            w = refs[2 * q + 1][...]
            t = _dot_nt(l, w) if trans_b else _dot(l, w)
            part = t if part is None else part + t

        @pl.when(k == 0)
        def _():
            o_ref[...] = part

        @pl.when(k > 0)
        def _():
            o_ref[...] += part

    in_specs, args = [], []
    for lhs, w in pairs:
        if lhs.ndim == 3:
            in_specs.append(pl.BlockSpec((None, tm, lhs.shape[2]), lambda i, k: (k, i, 0)))
        else:
            in_specs.append(pl.BlockSpec((tm, lhs.shape[1] // nk), lambda i, k: (i, k)))
        in_specs.append(pl.BlockSpec((None,) + tuple(w.shape[1:]), lambda i, k: (k, 0, 0)))
        args += [lhs, w]
    return pl.pallas_call(
        body, name=name, grid=(MM_TILES, nk), in_specs=in_specs,
        out_specs=pl.BlockSpec((tm, d_out), lambda i, k: (i, 0)),
        out_shape=jax.ShapeDtypeStruct((tp, d_out), F32),
        compiler_params=_params(("arbitrary", "arbitrary")),
    )(*args)


def _wgrad_call(x, y, name, x_width=None, y_width=None, tile_x=None, tile_y=None):
    tp = x.shape[1] if x.ndim == 3 else x.shape[0]

    def spec(a, width, tile):
        cols = a.shape[2] if a.ndim == 3 else (a.shape[1] if width is None else width)
        tc = cols if tile is None else tile
        per = cols // tc
        if a.ndim == 3:
            return pl.BlockSpec((None, tp, tc), lambda k, t: (k, 0, t if tile else 0)), cols, per
        if width is None:
            return pl.BlockSpec((tp, tc), lambda k, t: (0, t if tile else 0)), cols, per
        return pl.BlockSpec((tp, tc), lambda k, t: (0, k * per + (t if tile else 0))), cols, per

    xs, p, nx = spec(x, x_width, tile_x)
    ys, q, ny = spec(y, y_width, tile_y)
    nt = nx * ny
    if tile_x:
        ospec = pl.BlockSpec((None, tile_x, q), lambda k, t: (k, t, 0))
    else:
        ospec = pl.BlockSpec((None, p, tile_y), lambda k, t: (k, 0, t))

    def body(x_ref, y_ref, o_ref):
        o_ref[...] = _dot_tn(x_ref[...], y_ref[...]).astype(BF)

    return pl.pallas_call(
        body, name=name, grid=(N_CHIP, nt), in_specs=[xs, ys], out_specs=ospec,
        out_shape=jax.ShapeDtypeStruct((N_CHIP, p, q), BF),
        compiler_params=_params(("arbitrary", "arbitrary")),
    )(x, y)


def _row_call(body, name, tp, d, row_ins, vec_ins, row_out_dtypes, n_acc, acc_shape=None):
    te = tp // EW_TILES
    rspec = pl.BlockSpec((te, d), lambda i: (i, 0))
    vspec = pl.BlockSpec((1, d), lambda i: (0, 0))
    acc_shape = acc_shape or (1, d)
    aspec = pl.BlockSpec(acc_shape, lambda i: (0, 0))
    return pl.pallas_call(
        body, name=name, grid=(EW_TILES,),
        in_specs=[rspec] * len(row_ins) + [vspec] * len(vec_ins),
        out_specs=[rspec] * len(row_out_dtypes) + [aspec] * n_acc,
        out_shape=[jax.ShapeDtypeStruct((tp, d), dt) for dt in row_out_dtypes]
        + [jax.ShapeDtypeStruct(acc_shape, F32)] * n_acc,
        compiler_params=_params(("arbitrary",)),
    )(*row_ins, *vec_ins)


def _norm0(h, g):
    tp, d = h.shape

    def body(h_ref, g_ref, n_ref):
        n_ref[...] = _rms(h_ref[...], g_ref[...]).astype(BF)

    return _row_call(body, "norm0", tp, d, [h], [g], [BF], 0)[0]


def _post_fwd(f, h, g_post, g_next, scale, name):
    tp, d = h.shape

    def body(f_ref, h_ref, gp_ref, gn_ref, hn_ref, n_ref):
        hn = h_ref[...] + scale * _rms(f_ref[...], gp_ref[...])
        hn_ref[...] = hn
        n_ref[...] = _rms(hn, gn_ref[...]).astype(BF)

    return _row_call(body, name, tp, d, [f, h], [g_post, g_next], [F32, BF], 0)


def _loss_bwd(f, h, tgt, g_post, t_real):
    tp, d = h.shape
    te = tp // EW_TILES

    def body(f_ref, h_ref, t_ref, gp_ref, dh_ref, df_ref, dg_ref, loss_ref):
        i = pl.program_id(0)

        @pl.when(i == 0)
        def _():
            dg_ref[...] = jnp.zeros_like(dg_ref)
            loss_ref[...] = jnp.zeros_like(loss_ref)

        f = f_ref[...]
        gp = gp_ref[...]
        h3 = h_ref[...] + 0.5 * _rms(f, gp)
        rows = i * te + lax.broadcasted_iota(jnp.int32, (te, 1), 0)
        real = (rows >= N_META) & (rows < t_real)
        e = jnp.where(real, h3 - t_ref[...], 0.0)
        loss_ref[...] += 0.5 * jnp.sum(jnp.sum(e * e, axis=1, keepdims=True), axis=0, keepdims=True) / d
        dh = e / d
        dh_ref[...] = dh
        dfv, dgr = _rms_bwd(f, gp, 0.5 * dh)
        df_ref[...] = dfv.astype(BF)
        dg_ref[...] += jnp.sum(dgr, axis=0, keepdims=True)

    rspec = pl.BlockSpec((te, d), lambda i: (i, 0))
    vspec = pl.BlockSpec((1, d), lambda i: (0, 0))
    return pl.pallas_call(
        body, name="loss_bwd", grid=(EW_TILES,),
        in_specs=[rspec, rspec, rspec, vspec],
        out_specs=[rspec, rspec, vspec, pl.BlockSpec((1, 1), lambda i: (0, 0))],
        out_shape=[jax.ShapeDtypeStruct((tp, d), F32), jax.ShapeDtypeStruct((tp, d), BF),
                   jax.ShapeDtypeStruct((1, d), F32), jax.ShapeDtypeStruct((1, 1), F32)],
        compiler_params=_params(("arbitrary",)),
    )(f, h, tgt, g_post)


def _pre_bwd(dn, h, dh_out, g_pre, name, chain=None):
    tp, d = h.shape

    def body(*refs):
        if chain is None:
            dn_ref, h_ref, dho_ref, g_ref, dh_ref, dg_ref = refs
        else:
            dn_ref, h_ref, dho_ref, p_ref, g_ref, gp_ref, dh_ref, dp_ref, dg_ref, dgp_ref = refs
        i = pl.program_id(0)

        @pl.when(i == 0)
        def _():
            dg_ref[...] = jnp.zeros_like(dg_ref)
            if chain is not None:
                dgp_ref[...] = jnp.zeros_like(dgp_ref)

        dx, dgr = _rms_bwd(h_ref[...], g_ref[...], dn_ref[...])
        dh = dho_ref[...] + dx
        dh_ref[...] = dh
        dg_ref[...] += jnp.sum(dgr, axis=0, keepdims=True)
        if chain is not None:
            dp, dgpr = _rms_bwd(p_ref[...], gp_ref[...], chain[2] * dh)
            dp_ref[...] = dp.astype(BF)
            dgp_ref[...] += jnp.sum(dgpr, axis=0, keepdims=True)

    if chain is None:
        return _row_call(body, name, tp, d, [dn, h, dh_out], [g_pre], [F32], 1)
    return _row_call(body, name, tp, d, [dn, h, dh_out, chain[0]], [g_pre, chain[1]], [F32, BF], 2)


def _gelu(y):
    c = math.sqrt(2.0 / math.pi)
    return 0.5 * y * (1.0 + jnp.tanh(c * (y + 0.044715 * y * y * y)))


def _gelu_grad(y):
    c = math.sqrt(2.0 / math.pi)
    t = jnp.tanh(c * (y + 0.044715 * y * y * y))
    return 0.5 * (1.0 + t) + 0.5 * y * (1.0 - t * t) * c * (1.0 + 3.0 * 0.044715 * y * y)


def _neg_expm1(x):
    p = 1.0 + x * (1.0 / 9.0)
    for n in (8.0, 7.0, 6.0, 5.0, 4.0, 3.0, 2.0):
        p = 1.0 + x * (1.0 / n) * p
    return -jnp.where(x > -0.35, x * p, jnp.exp(x) - 1.0)


def _softplus(x):
    e = jnp.exp(-jnp.abs(x))
    w = 1.0 + e
    l1p = jnp.where(w == 1.0, e, jnp.log(w) * (e / jnp.where(w == 1.0, 1.0, w - 1.0)))
    return jnp.maximum(x, 0.0) + l1p


def _group_mean(v, gm):
    hi = v.astype(BF)
    lo = (v - hi.astype(F32)).astype(BF)
    return _dot(hi, gm) + _dot(lo, gm)


def _shift_dn(win, s, r):
    if s == 0:
        return win[8:8 + r]
    return pltpu.roll(win, s, 0)[8:8 + r]


def _shift_up(win, s, r):
    if s == 0:
        return win[0:r]
    return pltpu.roll(win, r + 8 - s, 0)[0:r]


def _window_dn(ref, t0, r, first):
    if first:
        return jnp.concatenate([jnp.zeros((8, ref.shape[1]), F32), ref[0:r, :]], axis=0)
    return ref[pl.ds(t0 - 8, r + 8), :]


def _tile_scan(a, u, reverse):
    r = a.shape[0]
    rid = lax.broadcasted_iota(jnp.int32, a.shape, 0) & 7
    for dlt in (1, 2, 4):
        sh = (r - dlt) if reverse else dlt
        a_s = pltpu.roll(a, sh, 0)
        u_s = pltpu.roll(u, sh, 0)
        keep = (rid + dlt <= 7) if reverse else (rid >= dlt)
        u = jnp.where(keep, u + a * u_s, u)
        a = jnp.where(keep, a * a_s, a)
    return a, u


def _lru_gates(xc, wa, ba, wx, bx, sp):
    xb = xc.astype(BF)
    ga = _sigmoid(_dot(xb, wa) + ba)
    gx = _sigmoid(_dot(xb, wx) + bx)
    la = -LRU_C * ga * sp
    return ga, gx, la


def _conv4(win, w4, cb, r):
    return (cb + w4[3:4] * _shift_dn(win, 0, r) + w4[2:3] * _shift_dn(win, 1, r)
            + w4[1:2] * _shift_dn(win, 2, r) + w4[0:1] * _shift_dn(win, 3, r))


def _lru_fwd(z, w4, cb, wa2, ba, wx2, bx, lam, g_out, gm):
    tp = z.shape[0]
    dl = cb.shape[1]
    nb = dl // LANE
    r = tp // MIX_CHUNKS
    c = LANE

    def body(y_ref, x_ref, w4_ref, cb_ref, wa_ref, ba_ref, wx_ref, bx_ref, lam_ref, go_ref, gm_ref, m_ref, hs_ref):
        w4v = w4_ref[...]
        cbv = cb_ref[...]
        wa = wa_ref[...]
        wx = wx_ref[...]
        bav = ba_ref[...]
        bxv = bx_ref[...]
        gov = go_ref[...]
        gmv = gm_ref[...]
        sp = _softplus(-lam_ref[...])

        def chunk(t0, hprev, first):
            win = _window_dn(x_ref, t0, r, first)
            xc = _conv4(win, w4v, cbv, r)
            ga, gx, la = _lru_gates(xc, wa, bav, wx, bxv, sp)
            a = jnp.exp(la)
            u = jnp.sqrt(_neg_expm1(2.0 * la)) * gx * xc
            ac, uc = _tile_scan(a, u, False)
            for j in range(r // 8):
                hj = uc[8 * j:8 * j + 8] + ac[8 * j:8 * j + 8] * hprev
                hs_ref[pl.ds(t0 + 8 * j, 8), :] = hj
                hprev = jnp.broadcast_to(hj[7:8], (8, c))
            h = hs_ref[pl.ds(t0, r), :]
            lo = h * _gelu(y_ref[pl.ds(t0, r), :])
            rs = lax.rsqrt(_group_mean(lo * lo, gmv) + EPS)
            m_ref[pl.ds(t0, r), :] = (lo * rs * gov).astype(BF)
            return hprev

        hp = chunk(0, jnp.zeros((8, c), F32), True)

        def loop(ci, hp):
            return chunk(pl.multiple_of(ci * r, 16), hp, False)

        lax.fori_loop(1, MIX_CHUNKS, loop, hp)

    col = lambda off: pl.BlockSpec((tp, c), lambda j: (0, off + j))
    vec = pl.BlockSpec((1, c), lambda j: (0, j))
    return pl.pallas_call(
        body, name="lru_fwd", grid=(nb,),
        in_specs=[col(0), col(nb), pl.BlockSpec((8, c), lambda j: (0, j)), vec,
                  pl.BlockSpec((None, c, c), lambda j: (j, 0, 0)), vec,
                  pl.BlockSpec((None, c, c), lambda j: (j, 0, 0)), vec, vec, vec,
                  pl.BlockSpec((c, c), lambda j: (0, 0))],
        out_specs=[col(0), col(0)],
        out_shape=[jax.ShapeDtypeStruct((tp, dl), BF), jax.ShapeDtypeStruct((tp, dl), F32)],
        compiler_params=_params(("arbitrary",)),
    )(z, z, w4, cb, wa2, ba, wx2, bx, lam, g_out, gm)


def _lru_bwd(z, hs, dmix, w4, cb, wa2, ba, wx2, bx, lam, g_out, gm):
    tp = z.shape[0]
    dl = cb.shape[1]
    nb = dl // LANE
    r = tp // MIX_CHUNKS
    c = LANE

    def body(y_ref, x_ref, hs_ref, dm_ref, w4_ref, cb_ref, wa_ref, ba_ref, wx_ref, bx_ref, lam_ref, go_ref, gm_ref,
             dy_ref, dx_ref, small_ref, dwa_ref, dwx_ref, xc_buf, ga_buf, gx_buf, a_buf, dh_buf, dxc_buf):
        w4v = w4_ref[...]
        cbv = cb_ref[...]
        wa = wa_ref[...]
        wx = wx_ref[...]
        bav = ba_ref[...]
        bxv = bx_ref[...]
        gov = go_ref[...]
        gmv = gm_ref[...]
        lamv = lam_ref[...]
        sp = _softplus(-lamv)
        small_ref[...] = jnp.zeros_like(small_ref)
        dwa_ref[...] = jnp.zeros_like(dwa_ref)
        dwx_ref[...] = jnp.zeros_like(dwx_ref)
        a_buf[pl.ds(tp, 8), :] = jnp.zeros((8, c), F32)
        dxc_buf[pl.ds(tp, 8), :] = jnp.zeros((8, c), F32)

        def fwd_chunk(t0, first):
            win = _window_dn(x_ref, t0, r, first)
            xc = _conv4(win, w4v, cbv, r)
            ga, gx, la = _lru_gates(xc, wa, bav, wx, bxv, sp)
            xc_buf[pl.ds(t0, r), :] = xc
            ga_buf[pl.ds(t0, r), :] = ga
            gx_buf[pl.ds(t0, r), :] = gx
            a_buf[pl.ds(t0, r), :] = jnp.exp(la)
            h = hs_ref[pl.ds(t0, r), :]
            yv = y_ref[pl.ds(t0, r), :]
            ge = _gelu(yv)
            lo = h * ge
            rs = lax.rsqrt(_group_mean(lo * lo, gmv) + EPS)
            xh = lo * rs
            dm = dm_ref[pl.ds(t0, r), :]
            q = dm * gov
            dlo = rs * (q - xh * _group_mean(q * xh, gmv))
            small_ref[8:9, :] += jnp.sum(dm * xh, axis=0, keepdims=True)
            dh_buf[pl.ds(t0, r), :] = dlo * ge
            dy_ref[pl.ds(t0, r), :] = (dlo * h * _gelu_grad(yv)).astype(BF)

        fwd_chunk(0, True)

        def floop(ci, carry):
            fwd_chunk(pl.multiple_of(ci * r, 16), False)
            return carry

        lax.fori_loop(1, MIX_CHUNKS, floop, 0)

        def bwd_chunk(t0, vnext, first):
            ap = _shift_up(a_buf[pl.ds(t0, r + 8), :], 1, r)
            ac, uc = _tile_scan(ap, dh_buf[pl.ds(t0, r), :], True)
            for j in reversed(range(r // 8)):
                vj = uc[8 * j:8 * j + 8] + ac[8 * j:8 * j + 8] * vnext
                dh_buf[pl.ds(t0 + 8 * j, 8), :] = vj
                vnext = jnp.broadcast_to(vj[0:1], (8, c))
            v = dh_buf[pl.ds(t0, r), :]
            hprev = _shift_dn(_window_dn(hs_ref, t0, r, first), 1, r)
            xc = xc_buf[pl.ds(t0, r), :]
            ga = ga_buf[pl.ds(t0, r), :]
            gx = gx_buf[pl.ds(t0, r), :]
            a = a_buf[pl.ds(t0, r), :]
            em = _neg_expm1(-2.0 * LRU_C * ga * sp)
            mult = jnp.sqrt(em)
            dla = v * hprev * a - (v * gx * xc) * ((1.0 - em) / mult)
            dgx = v * mult * xc
            dxc = v * mult * gx
            dga = dla * (-LRU_C) * sp
            small_ref[7:8, :] += jnp.sum(dla * (-LRU_C) * ga, axis=0, keepdims=True)
            dpa = dga * ga * (1.0 - ga)
            dpx = dgx * gx * (1.0 - gx)
            small_ref[5:6, :] += jnp.sum(dpa, axis=0, keepdims=True)
            small_ref[6:7, :] += jnp.sum(dpx, axis=0, keepdims=True)
            dpab = dpa.astype(BF)
            dpxb = dpx.astype(BF)
            xb = xc.astype(BF)
            dxc = dxc + _dot_nt(dpab, wa) + _dot_nt(dpxb, wx)
            dwa_ref[...] += _dot_tn(xb, dpab)
            dwx_ref[...] += _dot_tn(xb, dpxb)
            dxc_buf[pl.ds(t0, r), :] = dxc
            small_ref[4:5, :] += jnp.sum(dxc, axis=0, keepdims=True)
            dwin = dxc_buf[pl.ds(t0, r + 8), :]
            dx_ref[pl.ds(t0, r), :] = (w4v[3:4] * dxc + w4v[2:3] * _shift_up(dwin, 1, r)
                                       + w4v[1:2] * _shift_up(dwin, 2, r) + w4v[0:1] * _shift_up(dwin, 3, r)).astype(BF)
            xwin = _window_dn(x_ref, t0, r, first)
            for k in range(4):
                small_ref[k:k + 1, :] += jnp.sum(dxc * _shift_dn(xwin, 3 - k, r), axis=0, keepdims=True)
            return vnext

        def bloop(it, vnext):
            ci = MIX_CHUNKS - 1 - it
            return bwd_chunk(pl.multiple_of(ci * r, 16), vnext, False)

        vn = lax.fori_loop(0, MIX_CHUNKS - 1, bloop, jnp.zeros((8, c), F32))
        bwd_chunk(0, vn, True)
        small_ref[7:8, :] = small_ref[7:8, :] * (-_sigmoid(-lamv))

    col = lambda off: pl.BlockSpec((tp, c), lambda j: (0, off + j))
    vec = pl.BlockSpec((1, c), lambda j: (0, j))
    mat = pl.BlockSpec((None, c, c), lambda j: (j, 0, 0))
    buf = pltpu.VMEM((tp, c), F32)
    bufp = pltpu.VMEM((tp + 8, c), F32)
    return pl.pallas_call(
        body, name="lru_bwd", grid=(nb,),
        in_specs=[col(0), col(nb), col(0), col(0), pl.BlockSpec((8, c), lambda j: (0, j)), vec, mat, vec, mat, vec,
                  vec, vec, pl.BlockSpec((c, c), lambda j: (0, 0))],
        out_specs=[col(0), col(0), pl.BlockSpec((16, c), lambda j: (0, j)), mat, mat],
        out_shape=[jax.ShapeDtypeStruct((tp, dl), BF), jax.ShapeDtypeStruct((tp, dl), BF),
                   jax.ShapeDtypeStruct((16, dl), F32), jax.ShapeDtypeStruct((nb, c, c), F32),
                   jax.ShapeDtypeStruct((nb, c, c), F32)],
        scratch_shapes=[buf, buf, buf, bufp, buf, bufp],
        compiler_params=_params(("arbitrary",)),
    )(z, z, hs, dmix, w4, cb, wa2, ba, wx2, bx, lam, g_out, gm)


def _sc_conv(cvwin, w3, r):
    return w3[2:3] * _shift_dn(cvwin, 0, r) + w3[1:2] * _shift_dn(cvwin, 1, r) + w3[0:1] * _shift_dn(cvwin, 2, r)


def _sc_fwd(z, w3, g_out, gm, dl):
    tp = z.shape[0]
    nb = dl // LANE
    r = tp // MIX_CHUNKS
    c = LANE

    def body(b_ref, c_ref, v_ref, w3_ref, go_ref, gm_ref, m_ref):
        w3v = w3_ref[...]
        gov = go_ref[...]
        gmv = gm_ref[...]

        def chunk(t0, first):
            cvwin = _window_dn(c_ref, t0, r, first) * _window_dn(v_ref, t0, r, first)
            so = b_ref[pl.ds(t0, r), :] * _sc_conv(cvwin, w3v, r)
            rs = lax.rsqrt(_group_mean(so * so, gmv) + EPS)
            m_ref[pl.ds(t0, r), :] = (so * rs * gov).astype(BF)

        chunk(0, True)

        def loop(ci, carry):
            chunk(pl.multiple_of(ci * r, 16), False)
            return carry

        lax.fori_loop(1, MIX_CHUNKS, loop, 0)

    col = lambda off: pl.BlockSpec((tp, c), lambda j: (0, off + j))
    return pl.pallas_call(
        body, name="sconv_fwd", grid=(nb,),
        in_specs=[col(2 * nb), col(3 * nb), col(4 * nb), pl.BlockSpec((8, c), lambda j: (0, j)),
                  pl.BlockSpec((1, c), lambda j: (0, j)), pl.BlockSpec((c, c), lambda j: (0, 0))],
        out_specs=col(0), out_shape=jax.ShapeDtypeStruct((tp, dl), BF),
        compiler_params=_params(("arbitrary",)),
    )(z, z, z, w3, g_out, gm)


def _sc_bwd(z, dmix, w3, g_out, gm, dl):
    tp = z.shape[0]
    nb = dl // LANE
    r = tp // MIX_CHUNKS
    c = LANE

    def body(b_ref, c_ref, v_ref, dm_ref, w3_ref, go_ref, gm_ref, db_ref, dc_ref, dv_ref, small_ref, dsc_buf):
        w3v = w3_ref[...]
        gov = go_ref[...]
        gmv = gm_ref[...]
        small_ref[...] = jnp.zeros_like(small_ref)
        dsc_buf[pl.ds(tp, 8), :] = jnp.zeros((8, c), F32)

        def chunk1(t0, first):
            cvwin = _window_dn(c_ref, t0, r, first) * _window_dn(v_ref, t0, r, first)
            sc = _sc_conv(cvwin, w3v, r)
            bv = b_ref[pl.ds(t0, r), :]
            so = bv * sc
            rs = lax.rsqrt(_group_mean(so * so, gmv) + EPS)
            xh = so * rs
            dm = dm_ref[pl.ds(t0, r), :]
            q = dm * gov
            dso = rs * (q - xh * _group_mean(q * xh, gmv))
            small_ref[3:4, :] += jnp.sum(dm * xh, axis=0, keepdims=True)
            db_ref[pl.ds(t0, r), :] = (dso * sc).astype(BF)
            dsc = dso * bv
            dsc_buf[pl.ds(t0, r), :] = dsc
            for k in range(3):
                small_ref[k:k + 1, :] += jnp.sum(dsc * _shift_dn(cvwin, 2 - k, r), axis=0, keepdims=True)

        chunk1(0, True)

        def loop1(ci, carry):
            chunk1(pl.multiple_of(ci * r, 16), False)
            return carry

        lax.fori_loop(1, MIX_CHUNKS, loop1, 0)

        def loop2(ci, carry):
            t0 = pl.multiple_of(ci * r, 16)
            dwin = dsc_buf[pl.ds(t0, r + 8), :]
            dcv = w3v[2:3] * _shift_up(dwin, 0, r) + w3v[1:2] * _shift_up(dwin, 1, r) + w3v[0:1] * _shift_up(dwin, 2, r)
            dc_ref[pl.ds(t0, r), :] = (dcv * v_ref[pl.ds(t0, r), :]).astype(BF)
            dv_ref[pl.ds(t0, r), :] = (dcv * c_ref[pl.ds(t0, r), :]).astype(BF)
            return carry

        lax.fori_loop(0, MIX_CHUNKS, loop2, 0)

    col = lambda off: pl.BlockSpec((tp, c), lambda j: (0, off + j))
    out = jax.ShapeDtypeStruct((tp, dl), BF)
    return pl.pallas_call(
        body, name="sconv_bwd", grid=(nb,),
        in_specs=[col(2 * nb), col(3 * nb), col(4 * nb), col(nb), pl.BlockSpec((8, c), lambda j: (0, j)),
                  pl.BlockSpec((1, c), lambda j: (0, j)), pl.BlockSpec((c, c), lambda j: (0, 0))],
        out_specs=[col(0), col(0), col(0), pl.BlockSpec((8, c), lambda j: (0, j))],
        out_shape=[out, out, out, jax.ShapeDtypeStruct((8, dl), F32)],
        scratch_shapes=[pltpu.VMEM((tp + 8, c), F32)],
        compiler_params=_params(("arbitrary",)),
    )(z, z, z, dmix, w3, g_out, gm)


def _cast_pad(w, rows_p, cols_p, chip, name):
    r, c = w.shape

    def body(chip_ref, w_ref, o_ref):
        if (rows_p, cols_p) != (r, c):
            o_ref[...] = jnp.zeros_like(o_ref)
        o_ref[0:r, 0:c] = w_ref[...].astype(BF)

    return pl.pallas_call(
        body, name=name, out_shape=jax.ShapeDtypeStruct((N_CHIP, rows_p, cols_p), BF),
        grid_spec=pltpu.PrefetchScalarGridSpec(
            num_scalar_prefetch=1, grid=(1,),
            in_specs=[pl.BlockSpec((r, c), lambda i, chip: (0, 0))],
            out_specs=pl.BlockSpec((None, rows_p, cols_p), lambda i, chip: (chip[0], 0, 0))),
        compiler_params=_params(("arbitrary",)),
    )(chip, w)


def _adamw_math(w, g, m, v):
    m2 = ADAM_B1 * m + (1.0 - ADAM_B1) * g
    v2 = ADAM_B2 * v + (1.0 - ADAM_B2) * (g * g)
    m_hat = m2 / (1.0 - ADAM_B1 ** ADAM_STEP)
    v_hat = v2 / (1.0 - ADAM_B2 ** ADAM_STEP)
    delta = -ADAM_LR * (m_hat / (jnp.sqrt(v_hat) + ADAM_EPS) + ADAM_WD * w)
    return delta, m2, v2


def _adamw(w, g, m, v, name, row_tiles, col_tiles):
    r, c = w.shape
    tr = r // row_tiles
    tc = c // col_tiles
    gc = g.shape[1] if col_tiles == 1 else tc

    def body(w_ref, g_ref, m_ref, v_ref, go_ref, d_ref, mo_ref, vo_ref):
        gv = g_ref[...][:, 0:tc]
        delta, m2, v2 = _adamw_math(w_ref[...], gv, m_ref[...], v_ref[...])
        go_ref[...] = gv
        d_ref[...] = delta
        mo_ref[...] = m2
        vo_ref[...] = v2

    spec = pl.BlockSpec((tr, tc), lambda i, j: (i, j))
    out = jax.ShapeDtypeStruct((r, c), F32)
    return pl.pallas_call(
        body, name=name, grid=(row_tiles, col_tiles),
        in_specs=[spec, pl.BlockSpec((tr, gc), lambda i, j: (i, j)), spec, spec],
        out_specs=[spec] * 4, out_shape=[out] * 4,
        compiler_params=_params(("arbitrary", "arbitrary")),
    )(w, g, m, v)


def _adamw_small(w, g, m, v):
    def body(w_ref, g_ref, m_ref, v_ref, d_ref, mo_ref, vo_ref):
        delta, m2, v2 = _adamw_math(w_ref[...], g_ref[...], m_ref[...], v_ref[...])
        d_ref[...] = delta
        mo_ref[...] = m2
        vo_ref[...] = v2

    out = jax.ShapeDtypeStruct(w.shape, F32)
    spec = pl.BlockSpec(w.shape, lambda: (0, 0))
    return pl.pallas_call(body, name="adamw_small", in_specs=[spec] * 4, out_specs=[spec] * 3, out_shape=[out] * 3,
                          compiler_params=_params())(w, g, m, v)


def _place():
    x, y, c = lax.axis_index("x"), lax.axis_index("y"), lax.axis_index("c")
    chips = [(1 - x, y), (x, 1 - y), (1 - x, 1 - y)]
    return x, y, c, chips


ANY = pl.BlockSpec(memory_space=pl.ANY)


def _all_gather(bufs):
    n = len(bufs)

    def body(*refs):
        outs = refs[n:2 * n]
        s_ici, r_ici, s_d2d, r_d2d = refs[2 * n:]
        x, y, c, chips = _place()
        me = 2 * x + y

        def rows(w, chip, core):
            half = bufs[w].shape[1] // 2
            return outs[w].at[chip, pl.ds(core * half, half)]

        def ici(w, j, px, py):
            return pltpu.make_async_remote_copy(
                src_ref=rows(w, me, c), dst_ref=rows(w, me, c),
                send_sem=s_ici.at[w, j], recv_sem=r_ici.at[w, j], device_id=(px, py, c), device_id_type=MESH)

        def d2d(w, j, chip, core):
            return pltpu.make_async_remote_copy(
                src_ref=rows(w, chip, core), dst_ref=rows(w, chip, core),
                send_sem=s_d2d.at[w, j], recv_sem=r_d2d.at[w, j], device_id=(x, y, 1 - c), device_id_type=MESH)

        sends = [ici(w, j, px, py) for w in range(n) for j, (px, py) in enumerate(chips)]
        for cp in sends:
            cp.start()
        passed = []
        for w in range(n):
            for j, (px, py) in enumerate(chips):
                chip = 2 * px + py
                pltpu.make_async_remote_copy(
                    src_ref=rows(w, chip, c), dst_ref=rows(w, chip, c), send_sem=s_ici.at[w, j],
                    recv_sem=r_ici.at[w, j], device_id=(px, py, c), device_id_type=MESH).wait_recv()
                fw = d2d(w, j, chip, c)
                fw.start()
                passed.append(fw)
        for w in range(n):
            for j, (px, py) in enumerate(chips):
                d2d(w, j, 2 * px + py, 1 - c).wait_recv()
        for cp in sends + passed:
            cp.wait_send()

    return pl.pallas_call(
        body, name="all_gather_weights",
        out_shape=[jax.ShapeDtypeStruct(b.shape, b.dtype) for b in bufs],
        in_specs=[ANY] * n, out_specs=[ANY] * n, input_output_aliases={w: w for w in range(n)},
        scratch_shapes=[pltpu.SemaphoreType.DMA((n, 3)), pltpu.SemaphoreType.DMA((n, 3)),
                        pltpu.SemaphoreType.DMA((n, 3)), pltpu.SemaphoreType.DMA((n, 3))],
    )(*bufs)


def _pair_exchange(grads):
    n = len(grads)

    def body(*refs):
        ins, outs = refs[:n], refs[n:2 * n]
        ssem, rsem = refs[2 * n:]
        x, y, c, _ = _place()
        cps = []
        for w in range(n):
            half = grads[w].shape[1] // 2
            cps.append(pltpu.make_async_remote_copy(
                src_ref=ins[w].at[:, pl.ds((1 - c) * half, half)], dst_ref=outs[w],
                send_sem=ssem.at[w], recv_sem=rsem.at[w], device_id=(x, y, 1 - c), device_id_type=MESH))
        for cp in cps:
            cp.start()
        for cp in cps:
            cp.wait()

    return pl.pallas_call(
        body, name="grad_pair_exchange",
        out_shape=[jax.ShapeDtypeStruct((N_CHIP, g.shape[1] // 2, g.shape[2]), BF) for g in grads],
        in_specs=[ANY] * n, out_specs=[ANY] * n,
        scratch_shapes=[pltpu.SemaphoreType.DMA((n,)), pltpu.SemaphoreType.DMA((n,))],
    )(*grads)


def _pair_sum(g, sib, core, name):
    _, r, cdim = g.shape
    half = r // 2

    def body(core_ref, g_ref, s_ref, o_ref):
        o_ref[...] = (g_ref[...].astype(F32) + s_ref[...].astype(F32)).astype(BF)

    return pl.pallas_call(
        body, name=name,
        grid_spec=pltpu.PrefetchScalarGridSpec(
            num_scalar_prefetch=1, grid=(N_CHIP,),
            in_specs=[pl.BlockSpec((None, half, cdim), lambda k, core: (k, core[0], 0)),
                      pl.BlockSpec((None, half, cdim), lambda k, core: (k, 0, 0))],
            out_specs=pl.BlockSpec((None, half, cdim), lambda k, core: (k, 0, 0))),
        out_shape=jax.ShapeDtypeStruct((N_CHIP, half, cdim), BF),
        compiler_params=_params(("arbitrary",)),
    )(core, g, sib)


def _chip_exchange(psums):
    n = len(psums)

    def body(*refs):
        ins, outs = refs[:n], refs[n:2 * n]
        ssem, rsem = refs[2 * n:]
        x, y, c, chips = _place()
        cps = []
        for w in range(n):
            for j, (px, py) in enumerate(chips):
                cps.append(pltpu.make_async_remote_copy(
                    src_ref=ins[w].at[2 * px + py], dst_ref=outs[w].at[j],
                    send_sem=ssem.at[w, j], recv_sem=rsem.at[w, j], device_id=(px, py, c), device_id_type=MESH))
        for cp in cps:
            cp.start()
        for cp in cps:
            cp.wait()

    return pl.pallas_call(
        body, name="grad_chip_exchange",
        out_shape=[jax.ShapeDtypeStruct((3,) + p.shape[1:], BF) for p in psums],
        in_specs=[ANY] * n, out_specs=[ANY] * n,
        scratch_shapes=[pltpu.SemaphoreType.DMA((n, 3)), pltpu.SemaphoreType.DMA((n, 3))],
    )(*psums)


def _final_sum(g, sib, recv, sel, name):
    _, r, cdim = g.shape
    half = r // 2
    nt = 4
    th = half // nt

    def body(sel_ref, g_ref, s_ref, r_ref, o_ref):
        acc = g_ref[...].astype(F32) + s_ref[...].astype(F32)
        for j in range(3):
            acc = acc + r_ref[j].astype(F32)
        o_ref[...] = acc

    return pl.pallas_call(
        body, name=name,
        grid_spec=pltpu.PrefetchScalarGridSpec(
            num_scalar_prefetch=1, grid=(nt,),
            in_specs=[pl.BlockSpec((None, th, cdim), lambda i, sel: (sel[0], sel[1] * nt + i, 0)),
                      pl.BlockSpec((None, th, cdim), lambda i, sel: (sel[0], i, 0)),
                      pl.BlockSpec((3, th, cdim), lambda i, sel: (0, i, 0))],
            out_specs=pl.BlockSpec((th, cdim), lambda i, sel: (sel[1] * nt + i, 0))),
        out_shape=jax.ShapeDtypeStruct((r, cdim), F32),
        compiler_params=_params(("arbitrary",)),
    )(sel, g, sib, recv)


def _join_halves(bufs):
    n = len(bufs)

    def body(*refs):
        outs = refs[n:2 * n]
        ssem, rsem = refs[2 * n:]
        x, y, c, _ = _place()
        cps = []
        for w in range(n):
            half = bufs[w].shape[0] // 2
            mine = outs[w].at[pl.ds(c * half, half)]
            cps.append(pltpu.make_async_remote_copy(
                src_ref=mine, dst_ref=mine, send_sem=ssem.at[w], recv_sem=rsem.at[w],
                device_id=(x, y, 1 - c), device_id_type=MESH))
        for cp in cps:
            cp.start()
        for w in range(n):
            half = bufs[w].shape[0] // 2
            theirs = outs[w].at[pl.ds((1 - c) * half, half)]
            pltpu.make_async_remote_copy(
                src_ref=theirs, dst_ref=theirs, send_sem=ssem.at[w], recv_sem=rsem.at[w],
                device_id=(x, y, 1 - c), device_id_type=MESH).wait_recv()
        for cp in cps:
            cp.wait_send()

    return pl.pallas_call(
        body, name="grad_join_halves",
        out_shape=[jax.ShapeDtypeStruct(b.shape, F32) for b in bufs],
        in_specs=[ANY] * n, out_specs=[ANY] * n, input_output_aliases={w: w for w in range(n)},
        scratch_shapes=[pltpu.SemaphoreType.DMA((n,)), pltpu.SemaphoreType.DMA((n,))],
    )(*bufs)


def _small_all_reduce(buf, name):
    rows, d = buf.shape

    def body(in_ref, out_ref, sib, all4, ssem, rsem, psem, qsem):
        x, y, c, chips = _place()
        me = 2 * x + y
        to_sib = pltpu.make_async_remote_copy(src_ref=in_ref, dst_ref=sib, send_sem=ssem, recv_sem=rsem,
                                              device_id=(x, y, 1 - c), device_id_type=MESH)
        to_sib.start()
        to_sib.wait()
        all4[me] = in_ref[...] + sib[...]
        cps = [pltpu.make_async_remote_copy(src_ref=all4.at[me], dst_ref=all4.at[me], send_sem=psem.at[j],
                                            recv_sem=qsem.at[j], device_id=(px, py, c), device_id_type=MESH)
               for j, (px, py) in enumerate(chips)]
        for cp in cps:
            cp.start()
        for j, (px, py) in enumerate(chips):
            chip = 2 * px + py
            pltpu.make_async_remote_copy(src_ref=all4.at[chip], dst_ref=all4.at[chip], send_sem=psem.at[j],
                                         recv_sem=qsem.at[j], device_id=(px, py, c), device_id_type=MESH).wait_recv()
        for cp in cps:
            cp.wait_send()
        out_ref[...] = (all4[0] + all4[1]) + (all4[2] + all4[3])

    vm = pl.BlockSpec(memory_space=pltpu.VMEM)
    return pl.pallas_call(
        body, name=name, out_shape=jax.ShapeDtypeStruct((rows, d), F32),
        in_specs=[vm], out_specs=vm,
        scratch_shapes=[pltpu.VMEM((rows, d), F32), pltpu.VMEM((N_CHIP, rows, d), F32),
                        pltpu.SemaphoreType.DMA, pltpu.SemaphoreType.DMA,
                        pltpu.SemaphoreType.DMA((3,)), pltpu.SemaphoreType.DMA((3,))],
        compiler_params=_params(),
    )(buf)


def _pair_blocks(w):
    w4 = w.reshape(N_HEADS // 2, 2, HEAD, HEAD)
    eye = jnp.eye(2, dtype=w.dtype)
    return jnp.einsum("pirc,ij->pirjc", w4, eye).reshape(N_HEADS // 2, LANE, LANE)


def _unpair_blocks(w2):
    w5 = w2.reshape(N_HEADS // 2, 2, HEAD, 2, HEAD)
    return jnp.stack([w5[:, 0, :, 0, :], w5[:, 1, :, 1, :]], axis=1).reshape(N_HEADS, HEAD, HEAD)


def kernel(x, meta_tokens, ffn1_pre_g, ffn1_w_gate, ffn1_w_up, ffn1_w_down, ffn1_post_g, mix_pre_g, w_in, lru_conv_w, lru_conv_b, lru_w_a, lru_b_a, lru_w_x, lru_b_x, lru_lambda, sconv_w, lru_out_g, sconv_out_g, w_out, mix_post_g, ffn2_pre_g, ffn2_w_gate, ffn2_w_up, ffn2_w_down, ffn2_post_g, loss_target, m_meta_tokens, m_ffn1_pre_g, m_ffn1_w_gate, m_ffn1_w_up, m_ffn1_w_down, m_ffn1_post_g, m_mix_pre_g, m_w_in, m_lru_conv_w, m_lru_conv_b, m_lru_w_a, m_lru_b_a, m_lru_w_x, m_lru_b_x, m_lru_lambda, m_sconv_w, m_lru_out_g, m_sconv_out_g, m_w_out, m_mix_post_g, m_ffn2_pre_g, m_ffn2_w_gate, m_ffn2_w_up, m_ffn2_w_down, m_ffn2_post_g, v_meta_tokens, v_ffn1_pre_g, v_ffn1_w_gate, v_ffn1_w_up, v_ffn1_w_down, v_ffn1_post_g, v_mix_pre_g, v_w_in, v_lru_conv_w, v_lru_conv_b, v_lru_w_a, v_lru_b_a, v_lru_w_x, v_lru_b_x, v_lru_lambda, v_sconv_w, v_lru_out_g, v_sconv_out_g, v_w_out, v_mix_post_g, v_ffn2_pre_g, v_ffn2_w_gate, v_ffn2_w_up, v_ffn2_w_down, v_ffn2_post_g):
    seq, d = x.shape[1], x.shape[2]
    t_real = N_META + seq
    tp = _round_up(t_real, ROW_ALIGN)
    f4 = ffn1_w_gate.shape[2]
    f4p = _round_up(f4, LANE)
    dl = lru_conv_b.shape[1]
    cin = w_in.shape[2]
    xi, yi, ci = lax.axis_index("x"), lax.axis_index("y"), lax.axis_index("c")
    chip = 2 * xi + yi
    zero = jnp.zeros((), jnp.int32)

    transposed = ("ffn1_w_gate", "ffn1_w_up", "ffn2_w_gate", "ffn2_w_up")

    def view(k, a):
        return a[0].T if k in transposed else a[0]

    def unview(k, a):
        return (a.T if k in transposed else a)[None]

    big = {
        "ffn1_w_gate": (view("ffn1_w_gate", ffn1_w_gate), f4p, d), "ffn1_w_up": (view("ffn1_w_up", ffn1_w_up), f4p, d),
        "ffn1_w_down": (ffn1_w_down[0], f4p, d), "w_in": (w_in[0], d, cin), "w_out": (w_out[0], w_out.shape[1], d),
        "ffn2_w_gate": (view("ffn2_w_gate", ffn2_w_gate), f4p, d), "ffn2_w_up": (view("ffn2_w_up", ffn2_w_up), f4p, d),
        "ffn2_w_down": (ffn2_w_down[0], f4p, d),
    }
    names = list(big)
    chip1 = jnp.reshape(chip, (1,)).astype(jnp.int32)
    shards = [_cast_pad(big[k][0], big[k][1], big[k][2], chip1, "cast_" + k) for k in names]
    full = dict(zip(names, _all_gather(shards)))

    gm = jnp.kron(jnp.eye(2, dtype=F32), jnp.full((HEAD, HEAD), 1.0 / HEAD, F32)).astype(BF)
    wa2 = _pair_blocks(lru_w_a[0])
    wx2 = _pair_blocks(lru_w_x[0])

    dlq = dl // N_CHIP
    dq = d // N_CHIP
    R_GAIN, R_LOSS, R_META, R_LRU, R_SC, R_WA = 0, 6, 8, 24, 40, 48
    n_wrows = (N_HEADS // 2) * LANE * LANE // d
    R_WX = R_WA + n_wrows
    R_END = R_WX + n_wrows

    def pack(gains, meta, lru16, sc8, wa_, wx_, loss=None):
        rows = [jnp.concatenate(gains, axis=0)]
        lossrow = jnp.zeros((2, d), F32)
        if loss is not None:
            lossrow = lossrow.at[0, 0].set(loss)
        rows.append(lossrow)
        rows.append(meta)
        rows.append(jnp.concatenate([lru16, jnp.zeros((16, d - dl), F32)], axis=1))
        rows.append(jnp.concatenate([sc8, jnp.zeros((8, d - dl), F32)], axis=1))
        rows.append(wa_.reshape(n_wrows, d))
        rows.append(wx_.reshape(n_wrows, d))
        return jnp.concatenate(rows, axis=0)

    def place_cols(blk, width, total):
        return lax.dynamic_update_slice(jnp.zeros((blk.shape[0], total), F32), blk, (zero, chip * width))

    def pack_params(meta_, g1pre, g1post, gmpre, gmpost, g2pre, g2post, cw, cbias, wa_, ba_, wx_, bx_, lam_, sw, lgo, sgo):
        lru16 = jnp.concatenate([place_cols(cw[0], dlq, dl), cbias, ba_, bx_, lam_, lgo, jnp.zeros((7, dl), F32)], axis=0)
        sc8 = jnp.concatenate([place_cols(sw[0], dlq, dl), sgo, jnp.zeros((4, dl), F32)], axis=0)
        return pack([g1pre, g1post, gmpre, gmpost, g2pre, g2post], place_cols(meta_, dq, d), lru16, sc8,
                    _pair_blocks(wa_[0]), _pair_blocks(wx_[0]))

    p_w = pack_params(meta_tokens, ffn1_pre_g, ffn1_post_g, mix_pre_g, mix_post_g, ffn2_pre_g, ffn2_post_g, lru_conv_w,
                      lru_conv_b, lru_w_a, lru_b_a, lru_w_x, lru_b_x, lru_lambda, sconv_w, lru_out_g, sconv_out_g)
    p_m = pack_params(m_meta_tokens, m_ffn1_pre_g, m_ffn1_post_g, m_mix_pre_g, m_mix_post_g, m_ffn2_pre_g, m_ffn2_post_g,
                      m_lru_conv_w, m_lru_conv_b, m_lru_w_a, m_lru_b_a, m_lru_w_x, m_lru_b_x, m_lru_lambda, m_sconv_w,
                      m_lru_out_g, m_sconv_out_g)
    p_v = pack_params(v_meta_tokens, v_ffn1_pre_g, v_ffn1_post_g, v_mix_pre_g, v_mix_post_g, v_ffn2_pre_g, v_ffn2_post_g,
                      v_lru_conv_w, v_lru_conv_b, v_lru_w_a, v_lru_b_a, v_lru_w_x, v_lru_b_x, v_lru_lambda, v_sconv_w,
                      v_lru_out_g, v_sconv_out_g)

    gathered = _small_all_reduce(jnp.where(ci == 0, p_w, 0.0)[R_META:R_WA], "small_weight_gather")
    meta_full = gathered[0:N_META]
    w4_full = gathered[R_LRU - R_META:R_LRU - R_META + 4, 0:dl]
    w3_full = gathered[R_SC - R_META:R_SC - R_META + 3, 0:dl]
    w4p = jnp.concatenate([w4_full, jnp.zeros((4, dl), F32)], axis=0)
    w3p = jnp.concatenate([w3_full, jnp.zeros((5, dl), F32)], axis=0)

    h0 = jnp.concatenate([meta_full, x[0], jnp.zeros((tp - t_real, d), F32)], axis=0)
    tgt = jnp.concatenate([jnp.zeros((N_META, d), F32), loss_target[0], jnp.zeros((tp - t_real, d), F32)], axis=0)

    n1 = _norm0(h0, ffn1_pre_g)
    a1, b1, s1 = _ffn_up(n1, full["ffn1_w_gate"], full["ffn1_w_up"], "ffn1_up")
    f1 = _row_matmul([(s1, full["ffn1_w_down"])], "ffn1_down", False, d)
    h1, u = _post_fwd(f1, h0, ffn1_post_g, mix_pre_g, 0.5, "ffn1_post")
    z = _col_matmul(u, full["w_in"], "in_proj", False, F32)
    m_lru, hs = _lru_fwd(z, w4p, lru_conv_b, wa2.astype(BF), lru_b_a, wx2.astype(BF), lru_b_x, lru_lambda, lru_out_g, gm)
    m_sc = _sc_fwd(z, w3p, sconv_out_g, gm, dl)
    mixed = jnp.concatenate([m_lru, m_sc], axis=1)
    p = _row_matmul([(mixed, full["w_out"])], "out_proj", False, d)
    h2, n2 = _post_fwd(p, h1, mix_post_g, ffn2_pre_g, 1.0, "mix_post")
    a2, b2, s2 = _ffn_up(n2, full["ffn2_w_gate"], full["ffn2_w_up"], "ffn2_up")
    f2 = _row_matmul([(s2, full["ffn2_w_down"])], "ffn2_down", False, d)
    dh3, df2, dg_ffn2_post, loss_part = _loss_bwd(f2, h2, tgt, ffn2_post_g, t_real)

    da2, db2 = _ffn_bwd_act(df2, full["ffn2_w_down"], a2, b2, "ffn2_bwd_act")
    g_ffn2_down = _wgrad_call(s2, df2, "ffn2_down_wgrad", tile_y=WGRAD_TILE_Y)
    g_ffn2_gate = _wgrad_call(da2, n2, "ffn2_gate_wgrad", tile_y=WGRAD_TILE_Y)
    g_ffn2_up = _wgrad_call(db2, n2, "ffn2_up_wgrad", tile_y=WGRAD_TILE_Y)
    dn2 = _row_matmul([(da2, full["ffn2_w_gate"]), (db2, full["ffn2_w_up"])], "ffn2_bwd_up", False, d)
    dh2, dp, dg_ffn2_pre, dg_mix_post = _pre_bwd(dn2, h2, dh3, ffn2_pre_g, "ffn2_pre_bwd", (p, mix_post_g, 1.0))
    dmixed = _col_matmul(dp, full["w_out"], "out_proj_bwd", True, F32)
    g_w_out = _wgrad_call(mixed, dp, "w_out_wgrad", x_width=mixed.shape[1] // N_CHIP, tile_y=WGRAD_TILE_Y)
    dzy, dzx, lru_small, dwa2, dwx2 = _lru_bwd(z, hs, dmixed, w4p, lru_conv_b, wa2.astype(BF), lru_b_a, wx2.astype(BF),
                                               lru_b_x, lru_lambda, lru_out_g, gm)
    dzb, dzc, dzv, sc_small = _sc_bwd(z, dmixed, w3p, sconv_out_g, gm, dl)
    dz = jnp.concatenate([dzy, dzx, dzb, dzc, dzv], axis=1)
    g_w_in = _wgrad_call(u, dz, "w_in_wgrad", y_width=cin, tile_x=WGRAD_TILE_X)
    du = _row_matmul([(dz, full["w_in"])], "in_proj_bwd", True, d)
    dh1, df1, dg_mix_pre, dg_ffn1_post = _pre_bwd(du, h1, dh2, mix_pre_g, "mix_pre_bwd", (f1, ffn1_post_g, 0.5))
    da1, db1 = _ffn_bwd_act(df1, full["ffn1_w_down"], a1, b1, "ffn1_bwd_act")
    g_ffn1_down = _wgrad_call(s1, df1, "ffn1_down_wgrad", tile_y=WGRAD_TILE_Y)
    g_ffn1_gate = _wgrad_call(da1, n1, "ffn1_gate_wgrad", tile_y=WGRAD_TILE_Y)
    g_ffn1_up = _wgrad_call(db1, n1, "ffn1_up_wgrad", tile_y=WGRAD_TILE_Y)
    dn1 = _row_matmul([(da1, full["ffn1_w_gate"]), (db1, full["ffn1_w_up"])], "ffn1_bwd_up", False, d)
    dh0, dg_ffn1_pre = _pre_bwd(dn1, h0, dh1, ffn1_pre_g, "ffn1_pre_bwd")

    grad_x = dh0[N_META:t_real][None]

    p_g_local = pack([dg_ffn1_pre, dg_ffn1_post, dg_mix_pre, dg_mix_post, dg_ffn2_pre, dg_ffn2_post], dh0[0:N_META],
                     lru_small, sc_small, dwa2, dwx2, loss=loss_part[0, 0])
    p_g = _small_all_reduce(p_g_local, "small_grad_all_reduce")
    loss = p_g[R_LOSS, 0]
    p_delta, p_newm, p_newv = _adamw_small(p_w, p_g, p_m, p_v)

    def unpack(buf):
        out = {}
        for i, k in enumerate(["ffn1_pre_g", "ffn1_post_g", "mix_pre_g", "mix_post_g", "ffn2_pre_g", "ffn2_post_g"]):
            out[k] = buf[R_GAIN + i:R_GAIN + i + 1]
        out["meta_tokens"] = lax.dynamic_slice(buf[R_META:R_META + N_META], (zero, chip * dq), (N_META, dq))
        lru = buf[R_LRU:R_LRU + 16, 0:dl]
        out["lru_conv_w"] = lax.dynamic_slice(lru[0:4], (zero, chip * dlq), (4, dlq))[None]
        out["lru_conv_b"] = lru[4:5]
        out["lru_b_a"] = lru[5:6]
        out["lru_b_x"] = lru[6:7]
        out["lru_lambda"] = lru[7:8]
        out["lru_out_g"] = lru[8:9]
        sc = buf[R_SC:R_SC + 8, 0:dl]
        out["sconv_w"] = lax.dynamic_slice(sc[0:3], (zero, chip * dlq), (3, dlq))[None]
        out["sconv_out_g"] = sc[3:4]
        out["lru_w_a"] = _unpair_blocks(buf[R_WA:R_WX].reshape(N_HEADS // 2, LANE, LANE))[None]
        out["lru_w_x"] = _unpair_blocks(buf[R_WX:R_END].reshape(N_HEADS // 2, LANE, LANE))[None]
        return out

    s_grad, s_delta, s_newm, s_newv = unpack(p_g), unpack(p_delta), unpack(p_newm), unpack(p_newv)

    g_big = {"ffn1_w_gate": g_ffn1_gate, "ffn1_w_up": g_ffn1_up, "ffn1_w_down": g_ffn1_down, "w_in": g_w_in,
             "w_out": g_w_out, "ffn2_w_gate": g_ffn2_gate, "ffn2_w_up": g_ffn2_up, "ffn2_w_down": g_ffn2_down}
    glist = [g_big[k] for k in names]
    sibs = _pair_exchange(glist)
    core = jnp.reshape(ci, (1,)).astype(jnp.int32)
    psums = [_pair_sum(g, s, core, "pair_sum_" + k) for g, s, k in zip(glist, sibs, names)]
    recvs = _chip_exchange(psums)
    sel = jnp.stack([chip, ci]).astype(jnp.int32)
    halves = [_final_sum(g, s, r_, sel, "final_sum_" + k) for g, s, r_, k in zip(glist, sibs, recvs, names)]
    gfull = dict(zip(names, _join_halves(halves)))

    w_big = {"ffn1_w_gate": ffn1_w_gate, "ffn1_w_up": ffn1_w_up, "ffn1_w_down": ffn1_w_down, "w_in": w_in, "w_out": w_out,
             "ffn2_w_gate": ffn2_w_gate, "ffn2_w_up": ffn2_w_up, "ffn2_w_down": ffn2_w_down}
    m_big = {"ffn1_w_gate": m_ffn1_w_gate, "ffn1_w_up": m_ffn1_w_up, "ffn1_w_down": m_ffn1_w_down, "w_in": m_w_in,
             "w_out": m_w_out, "ffn2_w_gate": m_ffn2_w_gate, "ffn2_w_up": m_ffn2_w_up, "ffn2_w_down": m_ffn2_w_down}
    v_big = {"ffn1_w_gate": v_ffn1_w_gate, "ffn1_w_up": v_ffn1_w_up, "ffn1_w_down": v_ffn1_w_down, "w_in": v_w_in,
             "w_out": v_w_out, "ffn2_w_gate": v_ffn2_w_gate, "ffn2_w_up": v_ffn2_w_up, "ffn2_w_down": v_ffn2_w_down}
    b_grad, b_delta, b_newm, b_newv = {}, {}, {}, {}
    for k in names:
        wv, mv, vv = view(k, w_big[k]), view(k, m_big[k]), view(k, v_big[k])
        wide_rows = wv.shape[0] % 64 == 0
        g_, d_, m_, v_ = _adamw(wv, gfull[k], mv, vv, "adamw_" + k, 8 if wide_rows else 4, 1 if wide_rows else 2)
        b_grad[k], b_delta[k], b_newm[k], b_newv[k] = unview(k, g_), unview(k, d_), unview(k, m_), unview(k, v_)

    order = ["meta_tokens", "ffn1_pre_g", "ffn1_w_gate", "ffn1_w_up", "ffn1_w_down", "ffn1_post_g", "mix_pre_g", "w_in",
             "lru_conv_w", "lru_conv_b", "lru_w_a", "lru_b_a", "lru_w_x", "lru_b_x", "lru_lambda", "sconv_w", "lru_out_g",
             "sconv_out_g", "w_out", "mix_post_g", "ffn2_pre_g", "ffn2_w_gate", "ffn2_w_up", "ffn2_w_down", "ffn2_post_g"]

    def pick(small, bigd):
        return [bigd[k] if k in bigd else small[k] for k in order]

    return (loss, grad_x, *pick(s_grad, b_grad), *pick(s_delta, b_delta), *pick(s_newm, b_newm), *pick(s_newv, b_newv))
```

```python
import functools
import math

import jax
import jax.numpy as jnp
from jax import lax
from jax.experimental import pallas as pl
from jax.experimental.pallas import tpu as pltpu

F32 = jnp.float32
BF = jnp.bfloat16
MESH = pl.DeviceIdType.MESH

EPS = 1e-6
N_META = 16
N_HEADS = 16
HEAD = 64
LRU_C = 8.0
LANE = 128
N_CHIP = 4
ROW_ALIGN = 384
MM_TILES = 8
EW_TILES = 12
MIX_CHUNKS = 24
WGRAD_TILE_X = 256
WGRAD_TILE_Y = 512
VMEM_LIMIT = 56 << 20

ADAM_LR = 0.001
ADAM_B1 = 0.9
ADAM_B2 = 0.999
ADAM_EPS = 1e-08
ADAM_WD = 0.01
ADAM_STEP = 10


def _round_up(a, b):
    return (a + b - 1) // b * b


def _params(sem=None):
    if sem is None:
        return pltpu.CompilerParams(vmem_limit_bytes=VMEM_LIMIT)
    return pltpu.CompilerParams(dimension_semantics=sem, vmem_limit_bytes=VMEM_LIMIT)


def _sigmoid(x):
    return 1.0 / (1.0 + jnp.exp(-x))


def _dot(a, b):
    return jnp.dot(a, b, preferred_element_type=F32)


def _dot_nt(a, b):
    return lax.dot_general(a, b, (((1,), (1,)), ((), ())), preferred_element_type=F32)


def _dot_tn(a, b):
    return lax.dot_general(a, b, (((0,), (0,)), ((), ())), preferred_element_type=F32)


def _rms(x, g):
    r = lax.rsqrt(jnp.mean(x * x, axis=-1, keepdims=True) + EPS)
    return x * r * g


def _rms_bwd(x, g, dy):
    r = lax.rsqrt(jnp.mean(x * x, axis=-1, keepdims=True) + EPS)
    xh = x * r
    q = dy * g
    dx = r * (q - xh * jnp.mean(q * xh, axis=-1, keepdims=True))
    return dx, dy * xh


class _Side:
    def __init__(self, ins, outs, alias, sems, start, finish):
        self.ins, self.outs, self.alias, self.sems, self.start, self.finish = ins, outs, alias, sems, start, finish


def _grid_call(body, name, grid, in_specs, out_specs, out_shape, args, side=None):
    sem = ("arbitrary",) * len(grid)
    if side is None:
        res = pl.pallas_call(body, name=name, grid=grid, in_specs=in_specs, out_specs=out_specs, out_shape=out_shape,
                             compiler_params=_params(sem))(*args)
        return res, []
    nin, nout, sin, sout = len(in_specs), len(out_specs), len(side.ins), len(side.outs)

    def full(*refs):
        base_in, side_in = refs[:nin], refs[nin:nin + sin]
        base_out = refs[nin + sin:nin + sin + nout]
        side_out = refs[nin + sin + nout:nin + sin + nout + sout]
        sems = refs[nin + sin + nout + sout:]
        first = (pl.program_id(0) == 0) & (pl.program_id(1) == 0)
        last = (pl.program_id(0) == grid[0] - 1) & (pl.program_id(1) == grid[1] - 1)

        @pl.when(first)
        def _():
            side.start(side_in, side_out, sems)

        body(*base_in, *base_out)

        @pl.when(last)
        def _():
            side.finish(side_in, side_out, sems)

    any_spec = pl.BlockSpec(memory_space=pl.ANY)
    res = pl.pallas_call(
        full, name=name, grid=grid, in_specs=list(in_specs) + [any_spec] * sin,
        out_specs=list(out_specs) + [any_spec] * sout, out_shape=list(out_shape) + list(side.outs),
        scratch_shapes=list(side.sems), input_output_aliases={nin + i: nout + o for i, o in side.alias.items()},
        compiler_params=_params(sem))(*args, *side.ins)
    return res[:nout], res[nout:]


def _ffn_up(n, wg, wu, name, side=None):
    tp, d = n.shape
    fp = wg.shape[1]
    tm = tp // MM_TILES

    def body(n_ref, wg_ref, wu_ref, a_ref, b_ref, s_ref):
        nn = n_ref[...]
        a = _dot_nt(nn, wg_ref[...])
        b = _dot_nt(nn, wu_ref[...])
        a_ref[...] = a.astype(BF)
        b_ref[...] = b.astype(BF)
        s_ref[...] = (a * _sigmoid(a) * b).astype(BF)

    out = jax.ShapeDtypeStruct((N_CHIP, tp, fp), BF)
    wspec = pl.BlockSpec((None, fp, d), lambda k, i: (k, 0, 0))
    ospec = pl.BlockSpec((None, tm, fp), lambda k, i: (k, i, 0))
    return _grid_call(body, name, (N_CHIP, MM_TILES), [pl.BlockSpec((tm, d), lambda k, i: (i, 0)), wspec, wspec],
                      [ospec, ospec, ospec], [out, out, out], (n, wg, wu), side)


def _ffn_bwd_act(df, wd, a, b, name):
    tp, d = df.shape
    fp = wd.shape[1]
    tm = tp // MM_TILES

    def body(df_ref, wd_ref, a_ref, b_ref, da_ref, db_ref):
        ds = _dot_nt(df_ref[...], wd_ref[...])
        av = a_ref[...].astype(F32)
        bv = b_ref[...].astype(F32)
        sg = _sigmoid(av)
        da_ref[...] = (ds * bv * sg * (1.0 + av * (1.0 - sg))).astype(BF)
        db_ref[...] = (ds * av * sg).astype(BF)

    out = jax.ShapeDtypeStruct((N_CHIP, tp, fp), BF)
    aspec = pl.BlockSpec((None, tm, fp), lambda k, i: (k, i, 0))
    return pl.pallas_call(
        body, name=name, grid=(N_CHIP, MM_TILES),
        in_specs=[pl.BlockSpec((tm, d), lambda k, i: (i, 0)),
                  pl.BlockSpec((None, fp, d), lambda k, i: (k, 0, 0)), aspec, aspec],
        out_specs=[aspec, aspec], out_shape=[out, out],
        compiler_params=_params(("arbitrary", "arbitrary")),
    )(df, wd, a, b)


def _col_matmul(lhs, w, name, trans_b, out_dtype, side=None):
    tp, kd = lhs.shape
    nk = w.shape[0]
    nc = w.shape[1] if trans_b else w.shape[2]
    tm = tp // MM_TILES

    def body(l_ref, w_ref, o_ref):
        if trans_b:
            o_ref[...] = _dot_nt(l_ref[...], w_ref[...]).astype(out_dtype)
        else:
            o_ref[...] = _dot(l_ref[...], w_ref[...]).astype(out_dtype)

    res, extra = _grid_call(
        body, name, (nk, MM_TILES),
        [pl.BlockSpec((tm, kd), lambda k, i: (i, 0)), pl.BlockSpec((None,) + tuple(w.shape[1:]), lambda k, i: (k, 0, 0))],
        [pl.BlockSpec((tm, nc), lambda k, i: (i, k))], [jax.ShapeDtypeStruct((tp, nk * nc), out_dtype)], (lhs, w), side)
    return res[0], extra


def _row_matmul(pairs, name, trans_b, d_out, side=None):
    l0 = pairs[0][0]
    tp = l0.shape[1] if l0.ndim == 3 else l0.shape[0]
    nk = pairs[0][1].shape[0]
    tm = tp // MM_TILES
    npair = len(pairs)

    def body(*refs):
        o_ref = refs[2 * npair]
        k = pl.program_id(1)
        part = None
        for q in range(npair):
            l = refs[2 * q][...]
            w = refs[2 * q + 1][...]
            t = _dot_nt(l, w) if trans_b else _dot(l, w)
            part = t if part is None else part + t

        @pl.when(k == 0)
        def _():
            o_ref[...] = part

        @pl.when(k > 0)
        def _():
            o_ref[...] += part

    in_specs, args = [], []
    for lhs, w in pairs:
        if lhs.ndim == 3:
            in_specs.append(pl.BlockSpec((None, tm, lhs.shape[2]), lambda i, k: (k, i, 0)))
        else:
            in_specs.append(pl.BlockSpec((tm, lhs.shape[1] // nk), lambda i, k: (i, k)))
        in_specs.append(pl.BlockSpec((None,) + tuple(w.shape[1:]), lambda i, k: (k, 0, 0)))
        args += [lhs, w]
    res, extra = _grid_call(body, name, (MM_TILES, nk), in_specs, [pl.BlockSpec((tm, d_out), lambda i, k: (i, 0))],
                            [jax.ShapeDtypeStruct((tp, d_out), F32)], args, side)
    return res[0], extra


def _wgrad_call(x, y, name, x_width=None, y_width=None, tile_x=None, tile_y=None, side=None):
    tp = x.shape[1] if x.ndim == 3 else x.shape[0]

    def spec(a, width, tile):
        cols = a.shape[2] if a.ndim == 3 else (a.shape[1] if width is None else width)
        tc = cols if tile is None else tile
        per = cols // tc
        if a.ndim == 3:
            return pl.BlockSpec((None, tp, tc), lambda k, t: (k, 0, t if tile else 0)), cols, per
        if width is None:
            return pl.BlockSpec((tp, tc), lambda k, t: (0, t if tile else 0)), cols, per
        return pl.BlockSpec((tp, tc), lambda k, t: (0, k * per + (t if tile else 0))), cols, per

    xs, p, nx = spec(x, x_width, tile_x)
    ys, q, ny = spec(y, y_width, tile_y)
    nt = nx * ny
    if tile_x:
        ospec = pl.BlockSpec((None, tile_x, q), lambda k, t: (k, t, 0))
    else:
        ospec = pl.BlockSpec((None, p, tile_y), lambda k, t: (k, 0, t))

    def body(x_ref, y_ref, o_ref):
        o_ref[...] = _dot_tn(x_ref[...], y_ref[...]).astype(BF)

    res, extra = _grid_call(body, name, (N_CHIP, nt), [xs, ys], [ospec], [jax.ShapeDtypeStruct((N_CHIP, p, q), BF)],
                            (x, y), side)
    return res[0], extra


def _row_call(body, name, tp, d, row_ins, vec_ins, row_out_dtypes, n_acc, acc_shape=None):
    te = tp // EW_TILES
    rspec = pl.BlockSpec((te, d), lambda i: (i, 0))
    vspec = pl.BlockSpec((1, d), lambda i: (0, 0))
    acc_shape = acc_shape or (1, d)
    aspec = pl.BlockSpec(acc_shape, lambda i: (0, 0))
    return pl.pallas_call(
        body, name=name, grid=(EW_TILES,),
        in_specs=[rspec] * len(row_ins) + [vspec] * len(vec_ins),
        out_specs=[rspec] * len(row_out_dtypes) + [aspec] * n_acc,
        out_shape=[jax.ShapeDtypeStruct((tp, d), dt) for dt in row_out_dtypes]
        + [jax.ShapeDtypeStruct(acc_shape, F32)] * n_acc,
        compiler_params=_params(("arbitrary",)),
    )(*row_ins, *vec_ins)


def _norm0(h, g):
    tp, d = h.shape

    def body(h_ref, g_ref, n_ref):
        n_ref[...] = _rms(h_ref[...], g_ref[...]).astype(BF)

    return _row_call(body, "norm0", tp, d, [h], [g], [BF], 0)[0]


def _post_fwd(f, h, g_post, g_next, scale, name):
    tp, d = h.shape

    def body(f_ref, h_ref, gp_ref, gn_ref, hn_ref, n_ref):
        hn = h_ref[...] + scale * _rms(f_ref[...], gp_ref[...])
        hn_ref[...] = hn
        n_ref[...] = _rms(hn, gn_ref[...]).astype(BF)

    return _row_call(body, name, tp, d, [f, h], [g_post, g_next], [F32, BF], 0)


def _loss_bwd(f, h, tgt, g_post, t_real):
    tp, d = h.shape
    te = tp // EW_TILES

    def body(f_ref, h_ref, t_ref, gp_ref, dh_ref, df_ref, dg_ref, loss_ref):
        i = pl.program_id(0)

        @pl.when(i == 0)
        def _():
            dg_ref[...] = jnp.zeros_like(dg_ref)
            loss_ref[...] = jnp.zeros_like(loss_ref)

        f = f_ref[...]
        gp = gp_ref[...]
        h3 = h_ref[...] + 0.5 * _rms(f, gp)
        rows = i * te + lax.broadcasted_iota(jnp.int32, (te, 1), 0)
        real = (rows >= N_META) & (rows < t_real)
        e = jnp.where(real, h3 - t_ref[...], 0.0)
        loss_ref[...] += 0.5 * jnp.sum(jnp.sum(e * e, axis=1, keepdims=True), axis=0, keepdims=True) / d
        dh = e / d
        dh_ref[...] = dh
        dfv, dgr = _rms_bwd(f, gp, 0.5 * dh)
        df_ref[...] = dfv.astype(BF)
        dg_ref[...] += jnp.sum(dgr, axis=0, keepdims=True)

    rspec = pl.BlockSpec((te, d), lambda i: (i, 0))
    vspec = pl.BlockSpec((1, d), lambda i: (0, 0))
    return pl.pallas_call(
        body, name="loss_bwd", grid=(EW_TILES,),
        in_specs=[rspec, rspec, rspec, vspec],
        out_specs=[rspec, rspec, vspec, pl.BlockSpec((1, 1), lambda i: (0, 0))],
        out_shape=[jax.ShapeDtypeStruct((tp, d), F32), jax.ShapeDtypeStruct((tp, d), BF),
                   jax.ShapeDtypeStruct((1, d), F32), jax.ShapeDtypeStruct((1, 1), F32)],
        compiler_params=_params(("arbitrary",)),
    )(f, h, tgt, g_post)


def _pre_bwd(dn, h, dh_out, g_pre, name, chain=None):
    tp, d = h.shape

    def body(*refs):
        if chain is None:
            dn_ref, h_ref, dho_ref, g_ref, dh_ref, dg_ref = refs
        else:
            dn_ref, h_ref, dho_ref, p_ref, g_ref, gp_ref, dh_ref, dp_ref, dg_ref, dgp_ref = refs
        i = pl.program_id(0)

        @pl.when(i == 0)
        def _():
            dg_ref[...] = jnp.zeros_like(dg_ref)
            if chain is not None:
                dgp_ref[...] = jnp.zeros_like(dgp_ref)

        dx, dgr = _rms_bwd(h_ref[...], g_ref[...], dn_ref[...])
        dh = dho_ref[...] + dx
        dh_ref[...] = dh
        dg_ref[...] += jnp.sum(dgr, axis=0, keepdims=True)
        if chain is not None:
            dp, dgpr = _rms_bwd(p_ref[...], gp_ref[...], chain[2] * dh)
            dp_ref[...] = dp.astype(BF)
            dgp_ref[...] += jnp.sum(dgpr, axis=0, keepdims=True)

    if chain is None:
        return _row_call(body, name, tp, d, [dn, h, dh_out], [g_pre], [F32], 1)
    return _row_call(body, name, tp, d, [dn, h, dh_out, chain[0]], [g_pre, chain[1]], [F32, BF], 2)


def _gelu(y):
    c = math.sqrt(2.0 / math.pi)
    return 0.5 * y * (1.0 + jnp.tanh(c * (y + 0.044715 * y * y * y)))


def _gelu_grad(y):
    c = math.sqrt(2.0 / math.pi)
    t = jnp.tanh(c * (y + 0.044715 * y * y * y))
    return 0.5 * (1.0 + t) + 0.5 * y * (1.0 - t * t) * c * (1.0 + 3.0 * 0.044715 * y * y)


def _neg_expm1(x):
    p = 1.0 + x * (1.0 / 9.0)
    for n in (8.0, 7.0, 6.0, 5.0, 4.0, 3.0, 2.0):
        p = 1.0 + x * (1.0 / n) * p
    return -jnp.where(x > -0.35, x * p, jnp.exp(x) - 1.0)


def _softplus(x):
    e = jnp.exp(-jnp.abs(x))
    w = 1.0 + e
    l1p = jnp.where(w == 1.0, e, jnp.log(w) * (e / jnp.where(w == 1.0, 1.0, w - 1.0)))
    return jnp.maximum(x, 0.0) + l1p


def _group_mean(v, gm):
    hi = v.astype(BF)
    lo = (v - hi.astype(F32)).astype(BF)
    return _dot(hi, gm) + _dot(lo, gm)


def _shift_dn(win, s, r):
    if s == 0:
        return win[8:8 + r]
    return pltpu.roll(win, s, 0)[8:8 + r]


def _shift_up(win, s, r):
    if s == 0:
        return win[0:r]
    return pltpu.roll(win, r + 8 - s, 0)[0:r]


def _window_dn(ref, t0, r, first):
    if first:
        return jnp.concatenate([jnp.zeros((8, ref.shape[1]), F32), ref[0:r, :]], axis=0)
    return ref[pl.ds(t0 - 8, r + 8), :]


def _tile_scan(a, u, reverse):
    r = a.shape[0]
    rid = lax.broadcasted_iota(jnp.int32, a.shape, 0) & 7
    for dlt in (1, 2, 4):
        sh = (r - dlt) if reverse else dlt
        a_s = pltpu.roll(a, sh, 0)
        u_s = pltpu.roll(u, sh, 0)
        keep = (rid + dlt <= 7) if reverse else (rid >= dlt)
        u = jnp.where(keep, u + a * u_s, u)
        a = jnp.where(keep, a * a_s, a)
    return a, u


def _lru_gates(xc, wa, ba, wx, bx, sp):
    xb = xc.astype(BF)
    ga = _sigmoid(_dot(xb, wa) + ba)
    gx = _sigmoid(_dot(xb, wx) + bx)
    la = -LRU_C * ga * sp
    return ga, gx, la


def _conv4(win, w4, cb, r):
    return (cb + w4[3:4] * _shift_dn(win, 0, r) + w4[2:3] * _shift_dn(win, 1, r)
            + w4[1:2] * _shift_dn(win, 2, r) + w4[0:1] * _shift_dn(win, 3, r))


def _lru_fwd(z, w4, cb, wa2, ba, wx2, bx, lam, g_out, gm):
    tp = z.shape[0]
    dl = cb.shape[1]
    nb = dl // LANE
    r = tp // MIX_CHUNKS
    c = LANE

    def body(y_ref, x_ref, w4_ref, cb_ref, wa_ref, ba_ref, wx_ref, bx_ref, lam_ref, go_ref, gm_ref, m_ref, hs_ref):
        w4v = w4_ref[...]
        cbv = cb_ref[...]
        wa = wa_ref[...]
        wx = wx_ref[...]
        bav = ba_ref[...]
        bxv = bx_ref[...]
        gov = go_ref[...]
        gmv = gm_ref[...]
        sp = _softplus(-lam_ref[...])

        def chunk(t0, hprev, first):
            win = _window_dn(x_ref, t0, r, first)
            xc = _conv4(win, w4v, cbv, r)
            ga, gx, la = _lru_gates(xc, wa, bav, wx, bxv, sp)
            a = jnp.exp(la)
            u = jnp.sqrt(_neg_expm1(2.0 * la)) * gx * xc
            ac, uc = _tile_scan(a, u, False)
            for j in range(r // 8):
                hj = uc[8 * j:8 * j + 8] + ac[8 * j:8 * j + 8] * hprev
                hs_ref[pl.ds(t0 + 8 * j, 8), :] = hj
                hprev = jnp.broadcast_to(hj[7:8], (8, c))
            h = hs_ref[pl.ds(t0, r), :]
            lo = h * _gelu(y_ref[pl.ds(t0, r), :])
            rs = lax.rsqrt(_group_mean(lo * lo, gmv) + EPS)
            m_ref[pl.ds(t0, r), :] = (lo * rs * gov).astype(BF)
            return hprev

        hp = chunk(0, jnp.zeros((8, c), F32), True)

        def loop(ci, hp):
            return chunk(pl.multiple_of(ci * r, 16), hp, False)

        lax.fori_loop(1, MIX_CHUNKS, loop, hp)

    col = lambda off: pl.BlockSpec((tp, c), lambda j: (0, off + j))
    vec = pl.BlockSpec((1, c), lambda j: (0, j))
    return pl.pallas_call(
        body, name="lru_fwd", grid=(nb,),
        in_specs=[col(0), col(nb), pl.BlockSpec((8, c), lambda j: (0, j)), vec,
                  pl.BlockSpec((None, c, c), lambda j: (j, 0, 0)), vec,
                  pl.BlockSpec((None, c, c), lambda j: (j, 0, 0)), vec, vec, vec,
                  pl.BlockSpec((c, c), lambda j: (0, 0))],
        out_specs=[col(0), col(0)],
        out_shape=[jax.ShapeDtypeStruct((tp, dl), BF), jax.ShapeDtypeStruct((tp, dl), F32)],
        compiler_params=_params(("arbitrary",)),
    )(z, z, w4, cb, wa2, ba, wx2, bx, lam, g_out, gm)


def _lru_bwd(z, hs, dmix, w4, cb, wa2, ba, wx2, bx, lam, g_out, gm):
    tp = z.shape[0]
    dl = cb.shape[1]
    nb = dl // LANE
    r = tp // MIX_CHUNKS
    c = LANE

    def body(y_ref, x_ref, hs_ref, dm_ref, w4_ref, cb_ref, wa_ref, ba_ref, wx_ref, bx_ref, lam_ref, go_ref, gm_ref,
             dy_ref, dx_ref, small_ref, dwa_ref, dwx_ref, xc_buf, ga_buf, gx_buf, a_buf, dh_buf, dxc_buf):
        w4v = w4_ref[...]
        cbv = cb_ref[...]
        wa = wa_ref[...]
        wx = wx_ref[...]
        bav = ba_ref[...]
        bxv = bx_ref[...]
        gov = go_ref[...]
        gmv = gm_ref[...]
        lamv = lam_ref[...]
        sp = _softplus(-lamv)
        small_ref[...] = jnp.zeros_like(small_ref)
        dwa_ref[...] = jnp.zeros_like(dwa_ref)
        dwx_ref[...] = jnp.zeros_like(dwx_ref)
        a_buf[pl.ds(tp, 8), :] = jnp.zeros((8, c), F32)
        dxc_buf[pl.ds(tp, 8), :] = jnp.zeros((8, c), F32)

        def fwd_chunk(t0, first):
            win = _window_dn(x_ref, t0, r, first)
            xc = _conv4(win, w4v, cbv, r)
            ga, gx, la = _lru_gates(xc, wa, bav, wx, bxv, sp)
            xc_buf[pl.ds(t0, r), :] = xc
            ga_buf[pl.ds(t0, r), :] = ga
            gx_buf[pl.ds(t0, r), :] = gx
            a_buf[pl.ds(t0, r), :] = jnp.exp(la)
            h = hs_ref[pl.ds(t0, r), :]
            yv = y_ref[pl.ds(t0, r), :]
            ge = _gelu(yv)
            lo = h * ge
            rs = lax.rsqrt(_group_mean(lo * lo, gmv) + EPS)
            xh = lo * rs
            dm = dm_ref[pl.ds(t0, r), :]
            q = dm * gov
            dlo = rs * (q - xh * _group_mean(q * xh, gmv))
            small_ref[8:9, :] += jnp.sum(dm * xh, axis=0, keepdims=True)
            dh_buf[pl.ds(t0, r), :] = dlo * ge
            dy_ref[pl.ds(t0, r), :] = (dlo * h * _gelu_grad(yv)).astype(BF)

        fwd_chunk(0, True)

        def floop(ci, carry):
            fwd_chunk(pl.multiple_of(ci * r, 16), False)
            return carry

        lax.fori_loop(1, MIX_CHUNKS, floop, 0)

        def bwd_chunk(t0, vnext, first):
            ap = _shift_up(a_buf[pl.ds(t0, r + 8), :], 1, r)
            ac, uc = _tile_scan(ap, dh_buf[pl.ds(t0, r), :], True)
            for j in reversed(range(r // 8)):
                vj = uc[8 * j:8 * j + 8] + ac[8 * j:8 * j + 8] * vnext
                dh_buf[pl.ds(t0 + 8 * j, 8), :] = vj
                vnext = jnp.broadcast_to(vj[0:1], (8, c))
            v = dh_buf[pl.ds(t0, r), :]
            hprev = _shift_dn(_window_dn(hs_ref, t0, r, first), 1, r)
            xc = xc_buf[pl.ds(t0, r), :]
            ga = ga_buf[pl.ds(t0, r), :]
            gx = gx_buf[pl.ds(t0, r), :]
            a = a_buf[pl.ds(t0, r), :]
            em = _neg_expm1(-2.0 * LRU_C * ga * sp)
            mult = jnp.sqrt(em)
            dla = v * hprev * a - (v * gx * xc) * ((1.0 - em) / mult)
            dgx = v * mult * xc
            dxc = v * mult * gx
            dga = dla * (-LRU_C) * sp
            small_ref[7:8, :] += jnp.sum(dla * (-LRU_C) * ga, axis=0, keepdims=True)
            dpa = dga * ga * (1.0 - ga)
            dpx = dgx * gx * (1.0 - gx)
            small_ref[5:6, :] += jnp.sum(dpa, axis=0, keepdims=True)
            small_ref[6:7, :] += jnp.sum(dpx, axis=0, keepdims=True)
            dpab = dpa.astype(BF)
            dpxb = dpx.astype(BF)
            xb = xc.astype(BF)
            dxc = dxc + _dot_nt(dpab, wa) + _dot_nt(dpxb, wx)
            dwa_ref[...] += _dot_tn(xb, dpab)
            dwx_ref[...] += _dot_tn(xb, dpxb)
            dxc_buf[pl.ds(t0, r), :] = dxc
            small_ref[4:5, :] += jnp.sum(dxc, axis=0, keepdims=True)
            dwin = dxc_buf[pl.ds(t0, r + 8), :]
            dx_ref[pl.ds(t0, r), :] = (w4v[3:4] * dxc + w4v[2:3] * _shift_up(dwin, 1, r)
                                       + w4v[1:2] * _shift_up(dwin, 2, r) + w4v[0:1] * _shift_up(dwin, 3, r)).astype(BF)
            xwin = _window_dn(x_ref, t0, r, first)
            for k in range(4):
                small_ref[k:k + 1, :] += jnp.sum(dxc * _shift_dn(xwin, 3 - k, r), axis=0, keepdims=True)
            return vnext

        def bloop(it, vnext):
            ci = MIX_CHUNKS - 1 - it
            return bwd_chunk(pl.multiple_of(ci * r, 16), vnext, False)

        vn = lax.fori_loop(0, MIX_CHUNKS - 1, bloop, jnp.zeros((8, c), F32))
        bwd_chunk(0, vn, True)
        small_ref[7:8, :] = small_ref[7:8, :] * (-_sigmoid(-lamv))

    col = lambda off: pl.BlockSpec((tp, c), lambda j: (0, off + j))
    vec = pl.BlockSpec((1, c), lambda j: (0, j))
    mat = pl.BlockSpec((None, c, c), lambda j: (j, 0, 0))
    buf = pltpu.VMEM((tp, c), F32)
    bufp = pltpu.VMEM((tp + 8, c), F32)
    return pl.pallas_call(
        body, name="lru_bwd", grid=(nb,),
        in_specs=[col(0), col(nb), col(0), col(0), pl.BlockSpec((8, c), lambda j: (0, j)), vec, mat, vec, mat, vec,
                  vec, vec, pl.BlockSpec((c, c), lambda j: (0, 0))],
        out_specs=[col(0), col(0), pl.BlockSpec((16, c), lambda j: (0, j)), mat, mat],
        out_shape=[jax.ShapeDtypeStruct((tp, dl), BF), jax.ShapeDtypeStruct((tp, dl), BF),
                   jax.ShapeDtypeStruct((16, dl), F32), jax.ShapeDtypeStruct((nb, c, c), F32),
                   jax.ShapeDtypeStruct((nb, c, c), F32)],
        scratch_shapes=[buf, buf, buf, bufp, buf, bufp],
        compiler_params=_params(("arbitrary",)),
    )(z, z, hs, dmix, w4, cb, wa2, ba, wx2, bx, lam, g_out, gm)


def _sc_conv(cvwin, w3, r):
    return w3[2:3] * _shift_dn(cvwin, 0, r) + w3[1:2] * _shift_dn(cvwin, 1, r) + w3[0:1] * _shift_dn(cvwin, 2, r)


def _sc_fwd(z, w3, g_out, gm, dl):
    tp = z.shape[0]
    nb = dl // LANE
    r = tp // MIX_CHUNKS
    c = LANE

    def body(b_ref, c_ref, v_ref, w3_ref, go_ref, gm_ref, m_ref):
        w3v = w3_ref[...]
        gov = go_ref[...]
        gmv = gm_ref[...]

        def chunk(t0, first):
            cvwin = _window_dn(c_ref, t0, r, first) * _window_dn(v_ref, t0, r, first)
            so = b_ref[pl.ds(t0, r), :] * _sc_conv(cvwin, w3v, r)
            rs = lax.rsqrt(_group_mean(so * so, gmv) + EPS)
            m_ref[pl.ds(t0, r), :] = (so * rs * gov).astype(BF)

        chunk(0, True)

        def loop(ci, carry):
            chunk(pl.multiple_of(ci * r, 16), False)
            return carry

        lax.fori_loop(1, MIX_CHUNKS, loop, 0)

    col = lambda off: pl.BlockSpec((tp, c), lambda j: (0, off + j))
    return pl.pallas_call(
        body, name="sconv_fwd", grid=(nb,),
        in_specs=[col(2 * nb), col(3 * nb), col(4 * nb), pl.BlockSpec((8, c), lambda j: (0, j)),
                  pl.BlockSpec((1, c), lambda j: (0, j)), pl.BlockSpec((c, c), lambda j: (0, 0))],
        out_specs=col(0), out_shape=jax.ShapeDtypeStruct((tp, dl), BF),
        compiler_params=_params(("arbitrary",)),
    )(z, z, z, w3, g_out, gm)


def _sc_bwd(z, dmix, w3, g_out, gm, dl):
    tp = z.shape[0]
    nb = dl // LANE
    r = tp // MIX_CHUNKS
    c = LANE

    def body(b_ref, c_ref, v_ref, dm_ref, w3_ref, go_ref, gm_ref, db_ref, dc_ref, dv_ref, small_ref, dsc_buf):
        w3v = w3_ref[...]
        gov = go_ref[...]
        gmv = gm_ref[...]
        small_ref[...] = jnp.zeros_like(small_ref)
        dsc_buf[pl.ds(tp, 8), :] = jnp.zeros((8, c), F32)

        def chunk1(t0, first):
            cvwin = _window_dn(c_ref, t0, r, first) * _window_dn(v_ref, t0, r, first)
            sc = _sc_conv(cvwin, w3v, r)
            bv = b_ref[pl.ds(t0, r), :]
            so = bv * sc
            rs = lax.rsqrt(_group_mean(so * so, gmv) + EPS)
            xh = so * rs
            dm = dm_ref[pl.ds(t0, r), :]
            q = dm * gov
            dso = rs * (q - xh * _group_mean(q * xh, gmv))
            small_ref[3:4, :] += jnp.sum(dm * xh, axis=0, keepdims=True)
            db_ref[pl.ds(t0, r), :] = (dso * sc).astype(BF)
            dsc = dso * bv
            dsc_buf[pl.ds(t0, r), :] = dsc
            for k in range(3):
                small_ref[k:k + 1, :] += jnp.sum(dsc * _shift_dn(cvwin, 2 - k, r), axis=0, keepdims=True)

        chunk1(0, True)

        def loop1(ci, carry):
            chunk1(pl.multiple_of(ci * r, 16), False)
            return carry

        lax.fori_loop(1, MIX_CHUNKS, loop1, 0)

        def loop2(ci, carry):
            t0 = pl.multiple_of(ci * r, 16)
            dwin = dsc_buf[pl.ds(t0, r + 8), :]
            dcv = w3v[2:3] * _shift_up(dwin, 0, r) + w3v[1:2] * _shift_up(dwin, 1, r) + w3v[0:1] * _shift_up(dwin, 2, r)
            dc_ref[pl.ds(t0, r), :] = (dcv * v_ref[pl.ds(t0, r), :]).astype(BF)
            dv_ref[pl.ds(t0, r), :] = (dcv * c_ref[pl.ds(t0, r), :]).astype(BF)
            return carry

        lax.fori_loop(0, MIX_CHUNKS, loop2, 0)

    col = lambda off: pl.BlockSpec((tp, c), lambda j: (0, off + j))
    out = jax.ShapeDtypeStruct((tp, dl), BF)
    return pl.pallas_call(
        body, name="sconv_bwd", grid=(nb,),
        in_specs=[col(2 * nb), col(3 * nb), col(4 * nb), col(nb), pl.BlockSpec((8, c), lambda j: (0, j)),
                  pl.BlockSpec((1, c), lambda j: (0, j)), pl.BlockSpec((c, c), lambda j: (0, 0))],
        out_specs=[col(0), col(0), col(0), pl.BlockSpec((8, c), lambda j: (0, j))],
        out_shape=[out, out, out, jax.ShapeDtypeStruct((8, dl), F32)],
        scratch_shapes=[pltpu.VMEM((tp + 8, c), F32)],
        compiler_params=_params(("arbitrary",)),
    )(z, z, z, dmix, w3, g_out, gm)


def _cast_pad(w, rows_p, cols_p, chip, name):
    r, c = w.shape

    def body(chip_ref, w_ref, o_ref):
        if (rows_p, cols_p) != (r, c):
            o_ref[...] = jnp.zeros_like(o_ref)
        o_ref[0:r, 0:c] = w_ref[...].astype(BF)

    return pl.pallas_call(
        body, name=name, out_shape=jax.ShapeDtypeStruct((N_CHIP, rows_p, cols_p), BF),
        grid_spec=pltpu.PrefetchScalarGridSpec(
            num_scalar_prefetch=1, grid=(1,),
            in_specs=[pl.BlockSpec((r, c), lambda i, chip: (0, 0))],
            out_specs=pl.BlockSpec((None, rows_p, cols_p), lambda i, chip: (chip[0], 0, 0))),
        compiler_params=_params(("arbitrary",)),
    )(chip, w)


def _adamw_math(w, g, m, v):
    m2 = ADAM_B1 * m + (1.0 - ADAM_B1) * g
    v2 = ADAM_B2 * v + (1.0 - ADAM_B2) * (g * g)
    m_hat = m2 / (1.0 - ADAM_B1 ** ADAM_STEP)
    v_hat = v2 / (1.0 - ADAM_B2 ** ADAM_STEP)
    delta = -ADAM_LR * (m_hat / (jnp.sqrt(v_hat) + ADAM_EPS) + ADAM_WD * w)
    return delta, m2, v2


def _adamw(w, g, m, v, name, row_tiles, col_tiles):
    r, c = w.shape
    tr = r // row_tiles
    tc = c // col_tiles
    gc = g.shape[1] if col_tiles == 1 else tc

    def body(w_ref, g_ref, m_ref, v_ref, go_ref, d_ref, mo_ref, vo_ref):
        gv = g_ref[...][:, 0:tc]
        delta, m2, v2 = _adamw_math(w_ref[...], gv, m_ref[...], v_ref[...])
        go_ref[...] = gv
        d_ref[...] = delta
        mo_ref[...] = m2
        vo_ref[...] = v2

    spec = pl.BlockSpec((tr, tc), lambda i, j: (i, j))
    out = jax.ShapeDtypeStruct((r, c), F32)
    return pl.pallas_call(
        body, name=name, grid=(row_tiles, col_tiles),
        in_specs=[spec, pl.BlockSpec((tr, gc), lambda i, j: (i, j)), spec, spec],
        out_specs=[spec] * 4, out_shape=[out] * 4,
        compiler_params=_params(("arbitrary", "arbitrary")),
    )(w, g, m, v)


def _adamw_small(w, g, m, v):
    def body(w_ref, g_ref, m_ref, v_ref, d_ref, mo_ref, vo_ref):
        delta, m2, v2 = _adamw_math(w_ref[...], g_ref[...], m_ref[...], v_ref[...])
        d_ref[...] = delta
        mo_ref[...] = m2
        vo_ref[...] = v2

    out = jax.ShapeDtypeStruct(w.shape, F32)
    spec = pl.BlockSpec(w.shape, lambda: (0, 0))
    return pl.pallas_call(body, name="adamw_small", in_specs=[spec] * 4, out_specs=[spec] * 3, out_shape=[out] * 3,
                          compiler_params=_params())(w, g, m, v)


def _place():
    x, y, c = lax.axis_index("x"), lax.axis_index("y"), lax.axis_index("c")
    chips = [(1 - x, y), (x, 1 - y), (1 - x, 1 - y)]
    return x, y, c, chips


ANY = pl.BlockSpec(memory_space=pl.ANY)


def _gather_side(bufs):
    n = len(bufs)

    def copies(outs, sems):
        s_ici, r_ici, s_d2d, r_d2d = sems
        x, y, c, chips = _place()
        me = 2 * x + y

        def rows(w, chip, core):
            half = bufs[w].shape[1] // 2
            return outs[w].at[chip, pl.ds(core * half, half)]

        def ici(w, j, chip):
            px, py = chips[j]
            return pltpu.make_async_remote_copy(
                src_ref=rows(w, chip, c), dst_ref=rows(w, chip, c),
                send_sem=s_ici.at[w, j], recv_sem=r_ici.at[w, j], device_id=(px, py, c), device_id_type=MESH)

        def d2d(w, j, core):
            px, py = chips[j]
            return pltpu.make_async_remote_copy(
                src_ref=rows(w, 2 * px + py, core), dst_ref=rows(w, 2 * px + py, core),
                send_sem=s_d2d.at[w, j], recv_sem=r_d2d.at[w, j], device_id=(x, y, 1 - c), device_id_type=MESH)

        pairs = [(w, j) for w in range(n) for j in range(3)]
        return me, c, chips, ici, d2d, pairs

    def start(ins, outs, sems):
        me, c, chips, ici, d2d, pairs = copies(outs, sems)
        for w, j in pairs:
            ici(w, j, me).start()

    def finish(ins, outs, sems):
        me, c, chips, ici, d2d, pairs = copies(outs, sems)
        for w, j in pairs:
            ici(w, j, 2 * chips[j][0] + chips[j][1]).wait_recv()
            d2d(w, j, c).start()
        for w, j in pairs:
            d2d(w, j, 1 - c).wait_recv()
        for w, j in pairs:
            ici(w, j, me).wait_send()
            d2d(w, j, c).wait_send()

    dma = pltpu.SemaphoreType.DMA((n, 3))
    return _Side(list(bufs), [jax.ShapeDtypeStruct(b.shape, b.dtype) for b in bufs], {w: w for w in range(n)},
                 [dma, dma, dma, dma], start, finish)


def _run_side(side, name):
    sin, sout = len(side.ins), len(side.outs)

    def body(*refs):
        ins, outs, sems = refs[:sin], refs[sin:sin + sout], refs[sin + sout:]
        side.start(ins, outs, sems)
        side.finish(ins, outs, sems)

    return pl.pallas_call(
        body, name=name, out_shape=list(side.outs), in_specs=[ANY] * sin, out_specs=[ANY] * sout,
        scratch_shapes=list(side.sems), input_output_aliases=dict(side.alias))(*side.ins)


def _pair_exchange(grads, name):
    n = len(grads)

    def body(*refs):
        ins, outs = refs[:n], refs[n:2 * n]
        ssem, rsem = refs[2 * n:]
        x, y, c, _ = _place()
        cps = []
        for w in range(n):
            half = grads[w].shape[1] // 2
            cps.append(pltpu.make_async_remote_copy(
                src_ref=ins[w].at[:, pl.ds((1 - c) * half, half)], dst_ref=outs[w],
                send_sem=ssem.at[w], recv_sem=rsem.at[w], device_id=(x, y, 1 - c), device_id_type=MESH))
        for cp in cps:
            cp.start()
        for cp in cps:
            cp.wait()

    return pl.pallas_call(
        body, name=name,
        out_shape=[jax.ShapeDtypeStruct((N_CHIP, g.shape[1] // 2, g.shape[2]), BF) for g in grads],
        in_specs=[ANY] * n, out_specs=[ANY] * n,
        scratch_shapes=[pltpu.SemaphoreType.DMA((n,)), pltpu.SemaphoreType.DMA((n,))],
    )(*grads)


def _pair_sum(g, sib, core, name):
    _, r, cdim = g.shape
    half = r // 2

    def body(core_ref, g_ref, s_ref, o_ref):
        o_ref[...] = (g_ref[...].astype(F32) + s_ref[...].astype(F32)).astype(BF)

    return pl.pallas_call(
        body, name=name,
        grid_spec=pltpu.PrefetchScalarGridSpec(
            num_scalar_prefetch=1, grid=(N_CHIP,),
            in_specs=[pl.BlockSpec((None, half, cdim), lambda k, core: (k, core[0], 0)),
                      pl.BlockSpec((None, half, cdim), lambda k, core: (k, 0, 0))],
            out_specs=pl.BlockSpec((None, half, cdim), lambda k, core: (k, 0, 0))),
        out_shape=jax.ShapeDtypeStruct((N_CHIP, half, cdim), BF),
        compiler_params=_params(("arbitrary",)),
    )(core, g, sib)


def _chip_exchange_side(psums):
    n = len(psums)

    def copies(ins, outs, sems):
        ssem, rsem = sems
        x, y, c, chips = _place()
        return [pltpu.make_async_remote_copy(
            src_ref=ins[w].at[2 * px + py], dst_ref=outs[w].at[j],
            send_sem=ssem.at[w, j], recv_sem=rsem.at[w, j], device_id=(px, py, c), device_id_type=MESH)
            for w in range(n) for j, (px, py) in enumerate(chips)]

    def start(ins, outs, sems):
        for cp in copies(ins, outs, sems):
            cp.start()

    def finish(ins, outs, sems):
        for cp in copies(ins, outs, sems):
            cp.wait()

    dma = pltpu.SemaphoreType.DMA((n, 3))
    return _Side(list(psums), [jax.ShapeDtypeStruct((3,) + p.shape[1:], BF) for p in psums], {}, [dma, dma],
                 start, finish)


def _final_sum(g, sib, recv, sel, name):
    _, r, cdim = g.shape
    half = r // 2
    nt = 4
    th = half // nt

    def body(sel_ref, g_ref, s_ref, r_ref, o_ref):
        acc = g_ref[...].astype(F32) + s_ref[...].astype(F32)
        for j in range(3):
            acc = acc + r_ref[j].astype(F32)
        o_ref[...] = acc

    return pl.pallas_call(
        body, name=name,
        grid_spec=pltpu.PrefetchScalarGridSpec(
            num_scalar_prefetch=1, grid=(nt,),
            in_specs=[pl.BlockSpec((None, th, cdim), lambda i, sel: (sel[0], sel[1] * nt + i, 0)),
                      pl.BlockSpec((None, th, cdim), lambda i, sel: (sel[0], i, 0)),
                      pl.BlockSpec((3, th, cdim), lambda i, sel: (0, i, 0))],
            out_specs=pl.BlockSpec((th, cdim), lambda i, sel: (sel[1] * nt + i, 0))),
        out_shape=jax.ShapeDtypeStruct((r, cdim), F32),
        compiler_params=_params(("arbitrary",)),
    )(sel, g, sib, recv)


def _join_halves(bufs):
    n = len(bufs)

    def body(*refs):
        outs = refs[n:2 * n]
        ssem, rsem = refs[2 * n:]
        x, y, c, _ = _place()
        cps = []
        for w in range(n):
            half = bufs[w].shape[0] // 2
            mine = outs[w].at[pl.ds(c * half, half)]
            cps.append(pltpu.make_async_remote_copy(
                src_ref=mine, dst_ref=mine, send_sem=ssem.at[w], recv_sem=rsem.at[w],
                device_id=(x, y, 1 - c), device_id_type=MESH))
        for cp in cps:
            cp.start()
        for w in range(n):
            half = bufs[w].shape[0] // 2
            theirs = outs[w].at[pl.ds((1 - c) * half, half)]
            pltpu.make_async_remote_copy(
                src_ref=theirs, dst_ref=theirs, send_sem=ssem.at[w], recv_sem=rsem.at[w],
                device_id=(x, y, 1 - c), device_id_type=MESH).wait_recv()
        for cp in cps:
            cp.wait_send()

    return pl.pallas_call(
        body, name="grad_join_halves",
        out_shape=[jax.ShapeDtypeStruct(b.shape, F32) for b in bufs],
        in_specs=[ANY] * n, out_specs=[ANY] * n, input_output_aliases={w: w for w in range(n)},
        scratch_shapes=[pltpu.SemaphoreType.DMA((n,)), pltpu.SemaphoreType.DMA((n,))],
    )(*bufs)


def _small_all_reduce(buf, name):
    rows, d = buf.shape

    def body(in_ref, out_ref, sib, all4, ssem, rsem, psem, qsem):
        x, y, c, chips = _place()
        me = 2 * x + y
        to_sib = pltpu.make_async_remote_copy(src_ref=in_ref, dst_ref=sib, send_sem=ssem, recv_sem=rsem,
                                              device_id=(x, y, 1 - c), device_id_type=MESH)
        to_sib.start()
        to_sib.wait()
        all4[me] = in_ref[...] + sib[...]
        cps = [pltpu.make_async_remote_copy(src_ref=all4.at[me], dst_ref=all4.at[me], send_sem=psem.at[j],
                                            recv_sem=qsem.at[j], device_id=(px, py, c), device_id_type=MESH)
               for j, (px, py) in enumerate(chips)]
        for cp in cps:
            cp.start()
        for j, (px, py) in enumerate(chips):
            chip = 2 * px + py
            pltpu.make_async_remote_copy(src_ref=all4.at[chip], dst_ref=all4.at[chip], send_sem=psem.at[j],
                                         recv_sem=qsem.at[j], device_id=(px, py, c), device_id_type=MESH).wait_recv()
        for cp in cps:
            cp.wait_send()
        out_ref[...] = (all4[0] + all4[1]) + (all4[2] + all4[3])

    vm = pl.BlockSpec(memory_space=pltpu.VMEM)
    return pl.pallas_call(
        body, name=name, out_shape=jax.ShapeDtypeStruct((rows, d), F32),
        in_specs=[vm], out_specs=vm,
        scratch_shapes=[pltpu.VMEM((rows, d), F32), pltpu.VMEM((N_CHIP, rows, d), F32),
                        pltpu.SemaphoreType.DMA, pltpu.SemaphoreType.DMA,
                        pltpu.SemaphoreType.DMA((3,)), pltpu.SemaphoreType.DMA((3,))],
        compiler_params=_params(),
    )(buf)


def _pair_blocks(w):
    w4 = w.reshape(N_HEADS // 2, 2, HEAD, HEAD)
    eye = jnp.eye(2, dtype=w.dtype)
    return jnp.einsum("pirc,ij->pirjc", w4, eye).reshape(N_HEADS // 2, LANE, LANE)


def _unpair_blocks(w2):
    w5 = w2.reshape(N_HEADS // 2, 2, HEAD, 2, HEAD)
    return jnp.stack([w5[:, 0, :, 0, :], w5[:, 1, :, 1, :]], axis=1).reshape(N_HEADS, HEAD, HEAD)


def kernel(x, meta_tokens, ffn1_pre_g, ffn1_w_gate, ffn1_w_up, ffn1_w_down, ffn1_post_g, mix_pre_g, w_in, lru_conv_w, lru_conv_b, lru_w_a, lru_b_a, lru_w_x, lru_b_x, lru_lambda, sconv_w, lru_out_g, sconv_out_g, w_out, mix_post_g, ffn2_pre_g, ffn2_w_gate, ffn2_w_up, ffn2_w_down, ffn2_post_g, loss_target, m_meta_tokens, m_ffn1_pre_g, m_ffn1_w_gate, m_ffn1_w_up, m_ffn1_w_down, m_ffn1_post_g, m_mix_pre_g, m_w_in, m_lru_conv_w, m_lru_conv_b, m_lru_w_a, m_lru_b_a, m_lru_w_x, m_lru_b_x, m_lru_lambda, m_sconv_w, m_lru_out_g, m_sconv_out_g, m_w_out, m_mix_post_g, m_ffn2_pre_g, m_ffn2_w_gate, m_ffn2_w_up, m_ffn2_w_down, m_ffn2_post_g, v_meta_tokens, v_ffn1_pre_g, v_ffn1_w_gate, v_ffn1_w_up, v_ffn1_w_down, v_ffn1_post_g, v_mix_pre_g, v_w_in, v_lru_conv_w, v_lru_conv_b, v_lru_w_a, v_lru_b_a, v_lru_w_x, v_lru_b_x, v_lru_lambda, v_sconv_w, v_lru_out_g, v_sconv_out_g, v_w_out, v_mix_post_g, v_ffn2_pre_g, v_ffn2_w_gate, v_ffn2_w_up, v_ffn2_w_down, v_ffn2_post_g):
    seq, d = x.shape[1], x.shape[2]
    t_real = N_META + seq
    tp = _round_up(t_real, ROW_ALIGN)
    f4 = ffn1_w_gate.shape[2]
    f4p = _round_up(f4, LANE)
    dl = lru_conv_b.shape[1]
    cin = w_in.shape[2]
    xi, yi, ci = lax.axis_index("x"), lax.axis_index("y"), lax.axis_index("c")
    chip = 2 * xi + yi
    zero = jnp.zeros((), jnp.int32)

    transposed = ("ffn1_w_gate", "ffn1_w_up", "ffn2_w_gate", "ffn2_w_up")

    def view(k, a):
        return a[0].T if k in transposed else a[0]

    def unview(k, a):
        return (a.T if k in transposed else a)[None]

    big = {
        "ffn1_w_gate": (view("ffn1_w_gate", ffn1_w_gate), f4p, d), "ffn1_w_up": (view("ffn1_w_up", ffn1_w_up), f4p, d),
        "ffn1_w_down": (ffn1_w_down[0], f4p, d), "w_in": (w_in[0], d, cin), "w_out": (w_out[0], w_out.shape[1], d),
        "ffn2_w_gate": (view("ffn2_w_gate", ffn2_w_gate), f4p, d), "ffn2_w_up": (view("ffn2_w_up", ffn2_w_up), f4p, d),
        "ffn2_w_down": (ffn2_w_down[0], f4p, d),
    }
    names = list(big)
    chip1 = jnp.reshape(chip, (1,)).astype(jnp.int32)
    shard = {k: _cast_pad(big[k][0], big[k][1], big[k][2], chip1, "cast_" + k) for k in names}
    full = dict(zip(("ffn1_w_gate", "ffn1_w_up"),
                    _run_side(_gather_side([shard["ffn1_w_gate"], shard["ffn1_w_up"]]), "gather_ffn1_in")))

    gm = jnp.kron(jnp.eye(2, dtype=F32), jnp.full((HEAD, HEAD), 1.0 / HEAD, F32)).astype(BF)
    wa2 = _pair_blocks(lru_w_a[0])
    wx2 = _pair_blocks(lru_w_x[0])

    dlq = dl // N_CHIP
    dq = d // N_CHIP
    R_GAIN, R_LOSS, R_META, R_LRU, R_SC, R_WA = 0, 6, 8, 24, 40, 48
    n_wrows = (N_HEADS // 2) * LANE * LANE // d
    R_WX = R_WA + n_wrows
    R_END = R_WX + n_wrows

    def pack(gains, meta, lru16, sc8, wa_, wx_, loss=None):
        rows = [jnp.concatenate(gains, axis=0)]
        lossrow = jnp.zeros((2, d), F32)
        if loss is not None:
            lossrow = lossrow.at[0, 0].set(loss)
        rows.append(lossrow)
        rows.append(meta)
        rows.append(jnp.concatenate([lru16, jnp.zeros((16, d - dl), F32)], axis=1))
        rows.append(jnp.concatenate([sc8, jnp.zeros((8, d - dl), F32)], axis=1))
        rows.append(wa_.reshape(n_wrows, d))
        rows.append(wx_.reshape(n_wrows, d))
        return jnp.concatenate(rows, axis=0)

    def place_cols(blk, width, total):
        return lax.dynamic_update_slice(jnp.zeros((blk.shape[0], total), F32), blk, (zero, chip * width))

    def pack_params(meta_, g1pre, g1post, gmpre, gmpost, g2pre, g2post, cw, cbias, wa_, ba_, wx_, bx_, lam_, sw, lgo, sgo):
        lru16 = jnp.concatenate([place_cols(cw[0], dlq, dl), cbias, ba_, bx_, lam_, lgo, jnp.zeros((7, dl), F32)], axis=0)
        sc8 = jnp.concatenate([place_cols(sw[0], dlq, dl), sgo, jnp.zeros((4, dl), F32)], axis=0)
        return pack([g1pre, g1post, gmpre, gmpost, g2pre, g2post], place_cols(meta_, dq, d), lru16, sc8,
                    _pair_blocks(wa_[0]), _pair_blocks(wx_[0]))

    p_w = pack_params(meta_tokens, ffn1_pre_g, ffn1_post_g, mix_pre_g, mix_post_g, ffn2_pre_g, ffn2_post_g, lru_conv_w,
                      lru_conv_b, lru_w_a, lru_b_a, lru_w_x, lru_b_x, lru_lambda, sconv_w, lru_out_g, sconv_out_g)
    p_m = pack_params(m_meta_tokens, m_ffn1_pre_g, m_ffn1_post_g, m_mix_pre_g, m_mix_post_g, m_ffn2_pre_g, m_ffn2_post_g,
                      m_lru_conv_w, m_lru_conv_b, m_lru_w_a, m_lru_b_a, m_lru_w_x, m_lru_b_x, m_lru_lambda, m_sconv_w,
                      m_lru_out_g, m_sconv_out_g)
    p_v = pack_params(v_meta_tokens, v_ffn1_pre_g, v_ffn1_post_g, v_mix_pre_g, v_mix_post_g, v_ffn2_pre_g, v_ffn2_post_g,
                      v_lru_conv_w, v_lru_conv_b, v_lru_w_a, v_lru_b_a, v_lru_w_x, v_lru_b_x, v_lru_lambda, v_sconv_w,
                      v_lru_out_g, v_sconv_out_g)

    gathered = _small_all_reduce(jnp.where(ci == 0, p_w, 0.0)[R_META:R_WA], "small_weight_gather")
    meta_full = gathered[0:N_META]
    w4_full = gathered[R_LRU - R_META:R_LRU - R_META + 4, 0:dl]
    w3_full = gathered[R_SC - R_META:R_SC - R_META + 3, 0:dl]
    w4p = jnp.concatenate([w4_full, jnp.zeros((4, dl), F32)], axis=0)
    w3p = jnp.concatenate([w3_full, jnp.zeros((5, dl), F32)], axis=0)

    h0 = jnp.concatenate([meta_full, x[0], jnp.zeros((tp - t_real, d), F32)], axis=0)
    tgt = jnp.concatenate([jnp.zeros((N_META, d), F32), loss_target[0], jnp.zeros((tp - t_real, d), F32)], axis=0)

    n1 = _norm0(h0, ffn1_pre_g)
    (a1, b1, s1), got = _ffn_up(n1, full["ffn1_w_gate"], full["ffn1_w_up"], "ffn1_up",
                                _gather_side([shard["ffn1_w_down"], shard["w_in"]]))
    full["ffn1_w_down"], full["w_in"] = got
    f1, got = _row_matmul([(s1, full["ffn1_w_down"])], "ffn1_down", False, d,
                          _gather_side([shard["w_out"], shard["ffn2_w_gate"]]))
    full["w_out"], full["ffn2_w_gate"] = got
    h1, u = _post_fwd(f1, h0, ffn1_post_g, mix_pre_g, 0.5, "ffn1_post")
    z, got = _col_matmul(u, full["w_in"], "in_proj", False, F32, _gather_side([shard["ffn2_w_up"]]))
    full["ffn2_w_up"] = got[0]
    m_lru, hs = _lru_fwd(z, w4p, lru_conv_b, wa2.astype(BF), lru_b_a, wx2.astype(BF), lru_b_x, lru_lambda, lru_out_g, gm)
    m_sc = _sc_fwd(z, w3p, sconv_out_g, gm, dl)
    mixed = jnp.concatenate([m_lru, m_sc], axis=1)
    p, got = _row_matmul([(mixed, full["w_out"])], "out_proj", False, d, _gather_side([shard["ffn2_w_down"]]))
    full["ffn2_w_down"] = got[0]
    h2, n2 = _post_fwd(p, h1, mix_post_g, ffn2_pre_g, 1.0, "mix_post")
    (a2, b2, s2), _ = _ffn_up(n2, full["ffn2_w_gate"], full["ffn2_w_up"], "ffn2_up")
    f2, _ = _row_matmul([(s2, full["ffn2_w_down"])], "ffn2_down", False, d)
    dh3, df2, dg_ffn2_post, loss_part = _loss_bwd(f2, h2, tgt, ffn2_post_g, t_real)

    core = jnp.reshape(ci, (1,)).astype(jnp.int32)
    sel = jnp.stack([chip, ci]).astype(jnp.int32)
    reduced = {}

    def pair_reduce(k, g):
        sib = _pair_exchange([g], "pair_exchange_" + k)[0]
        reduced[k] = [g, sib, None]
        return _chip_exchange_side([_pair_sum(g, sib, core, "pair_sum_" + k)])

    da2, db2 = _ffn_bwd_act(df2, full["ffn2_w_down"], a2, b2, "ffn2_bwd_act")
    g, _ = _wgrad_call(s2, df2, "ffn2_down_wgrad", tile_y=WGRAD_TILE_Y)
    side = pair_reduce("ffn2_w_down", g)
    g, got = _wgrad_call(da2, n2, "ffn2_gate_wgrad", tile_y=WGRAD_TILE_Y, side=side)
    reduced["ffn2_w_down"][2] = got[0]
    side = pair_reduce("ffn2_w_gate", g)
    g, got = _wgrad_call(db2, n2, "ffn2_up_wgrad", tile_y=WGRAD_TILE_Y, side=side)
    reduced["ffn2_w_gate"][2] = got[0]
    side = pair_reduce("ffn2_w_up", g)
    dn2, got = _row_matmul([(da2, full["ffn2_w_gate"]), (db2, full["ffn2_w_up"])], "ffn2_bwd_up", False, d, side)
    reduced["ffn2_w_up"][2] = got[0]
    dh2, dp, dg_ffn2_pre, dg_mix_post = _pre_bwd(dn2, h2, dh3, ffn2_pre_g, "ffn2_pre_bwd", (p, mix_post_g, 1.0))
    dmixed, _ = _col_matmul(dp, full["w_out"], "out_proj_bwd", True, F32)
    g, _ = _wgrad_call(mixed, dp, "w_out_wgrad", x_width=mixed.shape[1] // N_CHIP, tile_y=WGRAD_TILE_Y)
    side = pair_reduce("w_out", g)
    dzy, dzx, lru_small, dwa2, dwx2 = _lru_bwd(z, hs, dmixed, w4p, lru_conv_b, wa2.astype(BF), lru_b_a, wx2.astype(BF),
                                               lru_b_x, lru_lambda, lru_out_g, gm)
    dzb, dzc, dzv, sc_small = _sc_bwd(z, dmixed, w3p, sconv_out_g, gm, dl)
    dz = jnp.concatenate([dzy, dzx, dzb, dzc, dzv], axis=1)
    g, got = _wgrad_call(u, dz, "w_in_wgrad", y_width=cin, tile_x=WGRAD_TILE_X, side=side)
    reduced["w_out"][2] = got[0]
    side = pair_reduce("w_in", g)
    du, got = _row_matmul([(dz, full["w_in"])], "in_proj_bwd", True, d, side)
    reduced["w_in"][2] = got[0]
    dh1, df1, dg_mix_pre, dg_ffn1_post = _pre_bwd(du, h1, dh2, mix_pre_g, "mix_pre_bwd", (f1, ffn1_post_g, 0.5))
    da1, db1 = _ffn_bwd_act(df1, full["ffn1_w_down"], a1, b1, "ffn1_bwd_act")
    g, _ = _wgrad_call(s1, df1, "ffn1_down_wgrad", tile_y=WGRAD_TILE_Y)
    side = pair_reduce("ffn1_w_down", g)
    g, got = _wgrad_call(da1, n1, "ffn1_gate_wgrad", tile_y=WGRAD_TILE_Y, side=side)
    reduced["ffn1_w_down"][2] = got[0]
    side = pair_reduce("ffn1_w_gate", g)
    g, got = _wgrad_call(db1, n1, "ffn1_up_wgrad", tile_y=WGRAD_TILE_Y, side=side)
    reduced["ffn1_w_gate"][2] = got[0]
    side = pair_reduce("ffn1_w_up", g)
    dn1, got = _row_matmul([(da1, full["ffn1_w_gate"]), (db1, full["ffn1_w_up"])], "ffn1_bwd_up", False, d, side)
    reduced["ffn1_w_up"][2] = got[0]
    dh0, dg_ffn1_pre = _pre_bwd(dn1, h0, dh1, ffn1_pre_g, "ffn1_pre_bwd")

    grad_x = dh0[N_META:t_real][None]

    p_g_local = pack([dg_ffn1_pre, dg_ffn1_post, dg_mix_pre, dg_mix_post, dg_ffn2_pre, dg_ffn2_post], dh0[0:N_META],
                     lru_small, sc_small, dwa2, dwx2, loss=loss_part[0, 0])
    p_g = _small_all_reduce(p_g_local, "small_grad_all_reduce")
    loss = p_g[R_LOSS, 0]
    p_delta, p_newm, p_newv = _adamw_small(p_w, p_g, p_m, p_v)

    def unpack(buf):
        out = {}
        for i, k in enumerate(["ffn1_pre_g", "ffn1_post_g", "mix_pre_g", "mix_post_g", "ffn2_pre_g", "ffn2_post_g"]):
            out[k] = buf[R_GAIN + i:R_GAIN + i + 1]
        out["meta_tokens"] = lax.dynamic_slice(buf[R_META:R_META + N_META], (zero, chip * dq), (N_META, dq))
        lru = buf[R_LRU:R_LRU + 16, 0:dl]
        out["lru_conv_w"] = lax.dynamic_slice(lru[0:4], (zero, chip * dlq), (4, dlq))[None]
        out["lru_conv_b"] = lru[4:5]
        out["lru_b_a"] = lru[5:6]
        out["lru_b_x"] = lru[6:7]
        out["lru_lambda"] = lru[7:8]
        out["lru_out_g"] = lru[8:9]
        sc = buf[R_SC:R_SC + 8, 0:dl]
        out["sconv_w"] = lax.dynamic_slice(sc[0:3], (zero, chip * dlq), (3, dlq))[None]
        out["sconv_out_g"] = sc[3:4]
        out["lru_w_a"] = _unpair_blocks(buf[R_WA:R_WX].reshape(N_HEADS // 2, LANE, LANE))[None]
        out["lru_w_x"] = _unpair_blocks(buf[R_WX:R_END].reshape(N_HEADS // 2, LANE, LANE))[None]
        return out

    s_grad, s_delta, s_newm, s_newv = unpack(p_g), unpack(p_delta), unpack(p_newm), unpack(p_newv)

    halves = [_final_sum(*reduced[k], sel, "final_sum_" + k) for k in names]
    gfull = dict(zip(names, _join_halves(halves)))

    w_big = {"ffn1_w_gate": ffn1_w_gate, "ffn1_w_up": ffn1_w_up, "ffn1_w_down": ffn1_w_down, "w_in": w_in, "w_out": w_out,
             "ffn2_w_gate": ffn2_w_gate, "ffn2_w_up": ffn2_w_up, "ffn2_w_down": ffn2_w_down}
    m_big = {"ffn1_w_gate": m_ffn1_w_gate, "ffn1_w_up": m_ffn1_w_up, "ffn1_w_down": m_ffn1_w_down, "w_in": m_w_in,
             "w_out": m_w_out, "ffn2_w_gate": m_ffn2_w_gate, "ffn2_w_up": m_ffn2_w_up, "ffn2_w_down": m_ffn2_w_down}
    v_big = {"ffn1_w_gate": v_ffn1_w_gate, "ffn1_w_up": v_ffn1_w_up, "ffn1_w_down": v_ffn1_w_down, "w_in": v_w_in,
             "w_out": v_w_out, "ffn2_w_gate": v_ffn2_w_gate, "ffn2_w_up": v_ffn2_w_up, "ffn2_w_down": v_ffn2_w_down}
    b_grad, b_delta, b_newm, b_newv = {}, {}, {}, {}
    for k in names:
        wv, mv, vv = view(k, w_big[k]), view(k, m_big[k]), view(k, v_big[k])
        wide_rows = wv.shape[0] % 64 == 0
        g_, d_, m_, v_ = _adamw(wv, gfull[k], mv, vv, "adamw_" + k, 8 if wide_rows else 4, 1 if wide_rows else 2)
        b_grad[k], b_delta[k], b_newm[k], b_newv[k] = unview(k, g_), unview(k, d_), unview(k, m_), unview(k, v_)

    order = ["meta_tokens", "ffn1_pre_g", "ffn1_w_gate", "ffn1_w_up", "ffn1_w_down", "ffn1_post_g", "mix_pre_g", "w_in",
             "lru_conv_w", "lru_conv_b", "lru_w_a", "lru_b_a", "lru_w_x", "lru_b_x", "lru_lambda", "sconv_w", "lru_out_g",
             "sconv_out_g", "w_out", "mix_post_g", "ffn2_pre_g", "ffn2_w_gate", "ffn2_w_up", "ffn2_w_down", "ffn2_post_g"]

    def pick(small, bigd):
        return [bigd[k] if k in bigd else small[k] for k in order]

    return (loss, grad_x, *pick(s_grad, b_grad), *pick(s_delta, b_delta), *pick(s_newm, b_newm), *pick(s_newv, b_newv))
```

```python
import functools
import math

import jax
import jax.numpy as jnp
from jax import lax
from jax.experimental import pallas as pl
from jax.experimental.pallas import tpu as pltpu

F32 = jnp.float32
BF = jnp.bfloat16
MESH = pl.DeviceIdType.MESH

EPS = 1e-6
N_META = 16
N_HEADS = 16
HEAD = 64
LRU_C = 8.0
LANE = 128
MXU_COLS = 256
N_CHIP = 4
ROW_ALIGN = 384
MM_TILES = 8
EW_TILES = 12
MIX_CHUNKS = 24
WGRAD_TILE_X = 256
WGRAD_TILE_Y = 512
VMEM_LIMIT = 56 << 20

ADAM_LR = 0.001
ADAM_B1 = 0.9
ADAM_B2 = 0.999
ADAM_EPS = 1e-08
ADAM_WD = 0.01
ADAM_STEP = 10


def _round_up(a, b):
    return (a + b - 1) // b * b


def _params(sem=None):
    if sem is None:
        return pltpu.CompilerParams(vmem_limit_bytes=VMEM_LIMIT)
    return pltpu.CompilerParams(dimension_semantics=sem, vmem_limit_bytes=VMEM_LIMIT)


def _sigmoid(x):
    return 1.0 / (1.0 + jnp.exp(-x))


def _dot(a, b):
    return jnp.dot(a, b, preferred_element_type=F32)


def _dot_nt(a, b):
    return lax.dot_general(a, b, (((1,), (1,)), ((), ())), preferred_element_type=F32)


def _dot_tn(a, b):
    return lax.dot_general(a, b, (((0,), (0,)), ((), ())), preferred_element_type=F32)


def _rms(x, g):
    r = lax.rsqrt(jnp.mean(x * x, axis=-1, keepdims=True) + EPS)
    return x * r * g


def _rms_bwd(x, g, dy):
    r = lax.rsqrt(jnp.mean(x * x, axis=-1, keepdims=True) + EPS)
    xh = x * r
    q = dy * g
    dx = r * (q - xh * jnp.mean(q * xh, axis=-1, keepdims=True))
    return dx, dy * xh


class _Side:
    def __init__(self, ins, outs, alias, sems, start, finish):
        self.ins, self.outs, self.alias, self.sems, self.start, self.finish = ins, outs, alias, sems, start, finish


def _merge_sides(sides):
    sides = [s for s in sides if s is not None]
    if len(sides) <= 1:
        return sides[0] if sides else None
    ins, outs, sems, alias, spans = [], [], [], {}, []
    for s in sides:
        for i, o in s.alias.items():
            alias[len(ins) + i] = len(outs) + o
        spans.append((len(ins), len(ins) + len(s.ins), len(outs), len(outs) + len(s.outs), len(sems),
                      len(sems) + len(s.sems)))
        ins += list(s.ins)
        outs += list(s.outs)
        sems += list(s.sems)

    def run(which):
        def go(in_refs, out_refs, sem_refs):
            for s, (a, b, c, d, e, f) in zip(sides, spans):
                getattr(s, which)(in_refs[a:b], out_refs[c:d], sem_refs[e:f])
        return go

    return _Side(ins, outs, alias, sems, run("start"), run("finish"))


def _grid_call(body, name, grid, in_specs, out_specs, out_shape, args, side=None):
    sem = ("arbitrary",) * len(grid)
    if side is None:
        res = pl.pallas_call(body, name=name, grid=grid, in_specs=in_specs, out_specs=out_specs, out_shape=out_shape,
                             compiler_params=_params(sem))(*args)
        return res, []
    nin, nout, sin, sout = len(in_specs), len(out_specs), len(side.ins), len(side.outs)

    def full(*refs):
        base_in, side_in = refs[:nin], refs[nin:nin + sin]
        base_out = refs[nin + sin:nin + sin + nout]
        side_out = refs[nin + sin + nout:nin + sin + nout + sout]
        sems = refs[nin + sin + nout + sout:]
        first = pl.program_id(0) == 0
        last = pl.program_id(0) == grid[0] - 1
        for ax in range(1, len(grid)):
            first = first & (pl.program_id(ax) == 0)
            last = last & (pl.program_id(ax) == grid[ax] - 1)

        @pl.when(first)
        def _():
            side.start(side_in, side_out, sems)

        body(*base_in, *base_out)

        @pl.when(last)
        def _():
            side.finish(side_in, side_out, sems)

    any_spec = pl.BlockSpec(memory_space=pl.ANY)
    res = pl.pallas_call(
        full, name=name, grid=grid, in_specs=list(in_specs) + [any_spec] * sin,
        out_specs=list(out_specs) + [any_spec] * sout, out_shape=list(out_shape) + list(side.outs),
        scratch_shapes=list(side.sems), input_output_aliases={nin + i: nout + o for i, o in side.alias.items()},
        compiler_params=_params(sem))(*args, *side.ins)
    return res[:nout], res[nout:]


def _ffn_up(n, wg, wu, name, side=None):
    tp, d = n.shape
    fp = wg.shape[1]
    tm = tp // MM_TILES

    def body(n_ref, wg_ref, wu_ref, a_ref, b_ref, s_ref):
        nn = n_ref[...]
        for c0 in range(0, fp, MXU_COLS):
            cs = slice(c0, min(c0 + MXU_COLS, fp))
            a = _dot_nt(nn, wg_ref[cs, :])
            b = _dot_nt(nn, wu_ref[cs, :])
            a_ref[:, cs] = a.astype(BF)
            b_ref[:, cs] = b.astype(BF)
            s_ref[:, cs] = (a * _sigmoid(a) * b).astype(BF)

    out = jax.ShapeDtypeStruct((N_CHIP, tp, fp), BF)
    wspec = pl.BlockSpec((None, fp, d), lambda k, i: (k, 0, 0))
    ospec = pl.BlockSpec((None, tm, fp), lambda k, i: (k, i, 0))
    return _grid_call(body, name, (N_CHIP, MM_TILES), [pl.BlockSpec((tm, d), lambda k, i: (i, 0)), wspec, wspec],
                      [ospec, ospec, ospec], [out, out, out], (n, wg, wu), side)


def _ffn_bwd_act(df, wd, a, b, name, side=None):
    tp, d = df.shape
    fp = wd.shape[1]
    tm = tp // MM_TILES

    def body(df_ref, wd_ref, a_ref, b_ref, da_ref, db_ref):
        dfv = df_ref[...]
        for c0 in range(0, fp, MXU_COLS):
            cs = slice(c0, min(c0 + MXU_COLS, fp))
            ds = _dot_nt(dfv, wd_ref[cs, :])
            av = a_ref[:, cs].astype(F32)
            bv = b_ref[:, cs].astype(F32)
            sg = _sigmoid(av)
            da_ref[:, cs] = (ds * bv * sg * (1.0 + av * (1.0 - sg))).astype(BF)
            db_ref[:, cs] = (ds * av * sg).astype(BF)

    out = jax.ShapeDtypeStruct((N_CHIP, tp, fp), BF)
    aspec = pl.BlockSpec((None, tm, fp), lambda k, i: (k, i, 0))
    return _grid_call(
        body, name, (N_CHIP, MM_TILES),
        [pl.BlockSpec((tm, d), lambda k, i: (i, 0)), pl.BlockSpec((None, fp, d), lambda k, i: (k, 0, 0)), aspec, aspec],
        [aspec, aspec], [out, out], (df, wd, a, b), side)


def _col_matmul(lhs, w, name, trans_b, out_dtype, side=None):
    tp, kd = lhs.shape
    nk = w.shape[0]
    nc = w.shape[1] if trans_b else w.shape[2]
    tm = tp // MM_TILES

    def body(l_ref, w_ref, o_ref):
        if trans_b:
            o_ref[...] = _dot_nt(l_ref[...], w_ref[...]).astype(out_dtype)
        else:
            o_ref[...] = _dot(l_ref[...], w_ref[...]).astype(out_dtype)

    res, extra = _grid_call(
        body, name, (nk, MM_TILES),
        [pl.BlockSpec((tm, kd), lambda k, i: (i, 0)), pl.BlockSpec((None,) + tuple(w.shape[1:]), lambda k, i: (k, 0, 0))],
        [pl.BlockSpec((tm, nc), lambda k, i: (i, k))], [jax.ShapeDtypeStruct((tp, nk * nc), out_dtype)], (lhs, w), side)
    return res[0], extra


def _row_matmul(pairs, name, trans_b, d_out, side=None):
    l0 = pairs[0][0]
    tp = l0.shape[1] if l0.ndim == 3 else l0.shape[0]
    nk = pairs[0][1].shape[0]
    tm = tp // MM_TILES
    npair = len(pairs)

    def body(*refs):
        o_ref = refs[2 * npair]
        k = pl.program_id(1)
        part = None
        for q in range(npair):
            l = refs[2 * q][...]
            w = refs[2 * q + 1][...]
            t = _dot_nt(l, w) if trans_b else _dot(l, w)
            part = t if part is None else part + t

        @pl.when(k == 0)
        def _():
            o_ref[...] = part

        @pl.when(k > 0)
        def _():
            o_ref[...] += part

    in_specs, args = [], []
    for lhs, w in pairs:
        if lhs.ndim == 3:
            in_specs.append(pl.BlockSpec((None, tm, lhs.shape[2]), lambda i, k: (k, i, 0)))
        else:
            in_specs.append(pl.BlockSpec((tm, lhs.shape[1] // nk), lambda i, k: (i, k)))
        in_specs.append(pl.BlockSpec((None,) + tuple(w.shape[1:]), lambda i, k: (k, 0, 0)))
        args += [lhs, w]
    res, extra = _grid_call(body, name, (MM_TILES, nk), in_specs, [pl.BlockSpec((tm, d_out), lambda i, k: (i, 0))],
                            [jax.ShapeDtypeStruct((tp, d_out), F32)], args, side)
    return res[0], extra


def _wgrad_call(x, y, name, x_width=None, y_width=None, tile_x=None, tile_y=None, side=None):
    tp = x.shape[1] if x.ndim == 3 else x.shape[0]

    def spec(a, width, tile):
        cols = a.shape[2] if a.ndim == 3 else (a.shape[1] if width is None else width)
        tc = cols if tile is None else tile
        per = cols // tc
        if a.ndim == 3:
            return pl.BlockSpec((None, tp, tc), lambda k, t: (k, 0, t if tile else 0)), cols, per
        if width is None:
            return pl.BlockSpec((tp, tc), lambda k, t: (0, t if tile else 0)), cols, per
        return pl.BlockSpec((tp, tc), lambda k, t: (0, k * per + (t if tile else 0))), cols, per

    xs, p, nx = spec(x, x_width, tile_x)
    ys, q, ny = spec(y, y_width, tile_y)
    nt = nx * ny
    if tile_x:
        ospec = pl.BlockSpec((None, tile_x, q), lambda k, t: (k, t, 0))
    else:
        ospec = pl.BlockSpec((None, p, tile_y), lambda k, t: (k, 0, t))

    def body(x_ref, y_ref, o_ref):
        o_ref[...] = _dot_tn(x_ref[...], y_ref[...]).astype(BF)

    res, extra = _grid_call(body, name, (N_CHIP, nt), [xs, ys], [ospec], [jax.ShapeDtypeStruct((N_CHIP, p, q), BF)],
                            (x, y), side)
    return res[0], extra


def _row_call(body, name, tp, d, row_ins, vec_ins, row_out_dtypes, n_acc, acc_shape=None):
    te = tp // EW_TILES
    rspec = pl.BlockSpec((te, d), lambda i: (i, 0))
    vspec = pl.BlockSpec((1, d), lambda i: (0, 0))
    acc_shape = acc_shape or (1, d)
    aspec = pl.BlockSpec(acc_shape, lambda i: (0, 0))
    return pl.pallas_call(
        body, name=name, grid=(EW_TILES,),
        in_specs=[rspec] * len(row_ins) + [vspec] * len(vec_ins),
        out_specs=[rspec] * len(row_out_dtypes) + [aspec] * n_acc,
        out_shape=[jax.ShapeDtypeStruct((tp, d), dt) for dt in row_out_dtypes]
        + [jax.ShapeDtypeStruct(acc_shape, F32)] * n_acc,
        compiler_params=_params(("arbitrary",)),
    )(*row_ins, *vec_ins)


def _norm0(h, g):
    tp, d = h.shape

    def body(h_ref, g_ref, n_ref):
        n_ref[...] = _rms(h_ref[...], g_ref[...]).astype(BF)

    return _row_call(body, "norm0", tp, d, [h], [g], [BF], 0)[0]


def _post_fwd(f, h, g_post, g_next, scale, name):
    tp, d = h.shape

    def body(f_ref, h_ref, gp_ref, gn_ref, hn_ref, n_ref):
        hn = h_ref[...] + scale * _rms(f_ref[...], gp_ref[...])
        hn_ref[...] = hn
        n_ref[...] = _rms(hn, gn_ref[...]).astype(BF)

    return _row_call(body, name, tp, d, [f, h], [g_post, g_next], [F32, BF], 0)


def _loss_bwd(f, h, tgt, g_post, t_real):
    tp, d = h.shape
    te = tp // EW_TILES

    def body(f_ref, h_ref, t_ref, gp_ref, dh_ref, df_ref, dg_ref, loss_ref):
        i = pl.program_id(0)

        @pl.when(i == 0)
        def _():
            dg_ref[...] = jnp.zeros_like(dg_ref)
            loss_ref[...] = jnp.zeros_like(loss_ref)

        f = f_ref[...]
        gp = gp_ref[...]
        h3 = h_ref[...] + 0.5 * _rms(f, gp)
        rows = i * te + lax.broadcasted_iota(jnp.int32, (te, 1), 0)
        real = (rows >= N_META) & (rows < t_real)
        e = jnp.where(real, h3 - t_ref[...], 0.0)
        loss_ref[...] += 0.5 * jnp.sum(jnp.sum(e * e, axis=1, keepdims=True), axis=0, keepdims=True) / d
        dh = e / d
        dh_ref[...] = dh
        dfv, dgr = _rms_bwd(f, gp, 0.5 * dh)
        df_ref[...] = dfv.astype(BF)
        dg_ref[...] += jnp.sum(dgr, axis=0, keepdims=True)

    rspec = pl.BlockSpec((te, d), lambda i: (i, 0))
    vspec = pl.BlockSpec((1, d), lambda i: (0, 0))
    return pl.pallas_call(
        body, name="loss_bwd", grid=(EW_TILES,),
        in_specs=[rspec, rspec, rspec, vspec],
        out_specs=[rspec, rspec, vspec, pl.BlockSpec((1, 1), lambda i: (0, 0))],
        out_shape=[jax.ShapeDtypeStruct((tp, d), F32), jax.ShapeDtypeStruct((tp, d), BF),
                   jax.ShapeDtypeStruct((1, d), F32), jax.ShapeDtypeStruct((1, 1), F32)],
        compiler_params=_params(("arbitrary",)),
    )(f, h, tgt, g_post)


def _pre_bwd(dn, h, dh_out, g_pre, name, chain=None):
    tp, d = h.shape

    def body(*refs):
        if chain is None:
            dn_ref, h_ref, dho_ref, g_ref, dh_ref, dg_ref = refs
        else:
            dn_ref, h_ref, dho_ref, p_ref, g_ref, gp_ref, dh_ref, dp_ref, dg_ref, dgp_ref = refs
        i = pl.program_id(0)

        @pl.when(i == 0)
        def _():
            dg_ref[...] = jnp.zeros_like(dg_ref)
            if chain is not None:
                dgp_ref[...] = jnp.zeros_like(dgp_ref)

        dx, dgr = _rms_bwd(h_ref[...], g_ref[...], dn_ref[...])
        dh = dho_ref[...] + dx
        dh_ref[...] = dh
        dg_ref[...] += jnp.sum(dgr, axis=0, keepdims=True)
        if chain is not None:
            dp, dgpr = _rms_bwd(p_ref[...], gp_ref[...], chain[2] * dh)
            dp_ref[...] = dp.astype(BF)
            dgp_ref[...] += jnp.sum(dgpr, axis=0, keepdims=True)

    if chain is None:
        return _row_call(body, name, tp, d, [dn, h, dh_out], [g_pre], [F32], 1)
    return _row_call(body, name, tp, d, [dn, h, dh_out, chain[0]], [g_pre, chain[1]], [F32, BF], 2)


def _gelu(y):
    c = math.sqrt(2.0 / math.pi)
    return 0.5 * y * (1.0 + jnp.tanh(c * (y + 0.044715 * y * y * y)))


def _gelu_grad(y):
    c = math.sqrt(2.0 / math.pi)
    t = jnp.tanh(c * (y + 0.044715 * y * y * y))
    return 0.5 * (1.0 + t) + 0.5 * y * (1.0 - t * t) * c * (1.0 + 3.0 * 0.044715 * y * y)


def _neg_expm1(x):
    p = 1.0 + x * (1.0 / 9.0)
    for n in (8.0, 7.0, 6.0, 5.0, 4.0, 3.0, 2.0):
        p = 1.0 + x * (1.0 / n) * p
    return -jnp.where(x > -0.35, x * p, jnp.exp(x) - 1.0)


def _softplus(x):
    e = jnp.exp(-jnp.abs(x))
    w = 1.0 + e
    l1p = jnp.where(w == 1.0, e, jnp.log(w) * (e / jnp.where(w == 1.0, 1.0, w - 1.0)))
    return jnp.maximum(x, 0.0) + l1p


def _group_mean(v, gm):
    hi = v.astype(BF)
    lo = (v - hi.astype(F32)).astype(BF)
    return _dot(hi, gm) + _dot(lo, gm)


def _shift_dn(win, s, r):
    if s == 0:
        return win[8:8 + r]
    return pltpu.roll(win, s, 0)[8:8 + r]


def _shift_up(win, s, r):
    if s == 0:
        return win[0:r]
    return pltpu.roll(win, r + 8 - s, 0)[0:r]


def _window_dn(ref, t0, r, first):
    if first:
        return jnp.concatenate([jnp.zeros((8, ref.shape[1]), F32), ref[0:r, :]], axis=0)
    return ref[pl.ds(t0 - 8, r + 8), :]


def _tile_scan(a, u, reverse):
    r = a.shape[0]
    rid = lax.broadcasted_iota(jnp.int32, a.shape, 0) & 7
    for dlt in (1, 2, 4):
        sh = (r - dlt) if reverse else dlt
        a_s = pltpu.roll(a, sh, 0)
        u_s = pltpu.roll(u, sh, 0)
        keep = (rid + dlt <= 7) if reverse else (rid >= dlt)
        u = jnp.where(keep, u + a * u_s, u)
        a = jnp.where(keep, a * a_s, a)
    return a, u


def _lru_gates(xc, wa, ba, wx, bx, sp):
    xb = xc.astype(BF)
    ga = _sigmoid(_dot(xb, wa) + ba)
    gx = _sigmoid(_dot(xb, wx) + bx)
    la = -LRU_C * ga * sp
    return ga, gx, la


def _conv4(win, w4, cb, r):
    return (cb + w4[3:4] * _shift_dn(win, 0, r) + w4[2:3] * _shift_dn(win, 1, r)
            + w4[1:2] * _shift_dn(win, 2, r) + w4[0:1] * _shift_dn(win, 3, r))


def _lru_fwd(z, w4, cb, wa2, ba, wx2, bx, lam, g_out, gm, side=None):
    tp = z.shape[0]
    dl = cb.shape[1]
    nb = dl // LANE
    r = tp // MIX_CHUNKS
    c = LANE

    def body(y_ref, x_ref, w4_ref, cb_ref, wa_ref, ba_ref, wx_ref, bx_ref, lam_ref, go_ref, gm_ref, m_ref, hs_ref):
        w4v = w4_ref[...]
        cbv = cb_ref[...]
        wa = wa_ref[...]
        wx = wx_ref[...]
        bav = ba_ref[...]
        bxv = bx_ref[...]
        gov = go_ref[...]
        gmv = gm_ref[...]
        sp = _softplus(-lam_ref[...])

        def chunk(t0, hprev, first):
            win = _window_dn(x_ref, t0, r, first)
            xc = _conv4(win, w4v, cbv, r)
            ga, gx, la = _lru_gates(xc, wa, bav, wx, bxv, sp)
            a = jnp.exp(la)
            u = jnp.sqrt(_neg_expm1(2.0 * la)) * gx * xc
            ac, uc = _tile_scan(a, u, False)
            for j in range(r // 8):
                hj = uc[8 * j:8 * j + 8] + ac[8 * j:8 * j + 8] * hprev
                hs_ref[pl.ds(t0 + 8 * j, 8), :] = hj
                hprev = jnp.broadcast_to(hj[7:8], (8, c))
            h = hs_ref[pl.ds(t0, r), :]
            lo = h * _gelu(y_ref[pl.ds(t0, r), :])
            rs = lax.rsqrt(_group_mean(lo * lo, gmv) + EPS)
            m_ref[pl.ds(t0, r), :] = (lo * rs * gov).astype(BF)
            return hprev

        hp = chunk(0, jnp.zeros((8, c), F32), True)

        def loop(ci, hp):
            return chunk(pl.multiple_of(ci * r, 16), hp, False)

        lax.fori_loop(1, MIX_CHUNKS, loop, hp)

    col = lambda off: pl.BlockSpec((tp, c), lambda j: (0, off + j))
    vec = pl.BlockSpec((1, c), lambda j: (0, j))
    return _grid_call(
        body, "lru_fwd", (nb,),
        [col(0), col(nb), pl.BlockSpec((8, c), lambda j: (0, j)), vec, pl.BlockSpec((None, c, c), lambda j: (j, 0, 0)),
         vec, pl.BlockSpec((None, c, c), lambda j: (j, 0, 0)), vec, vec, vec, pl.BlockSpec((c, c), lambda j: (0, 0))],
        [col(0), col(0)], [jax.ShapeDtypeStruct((tp, dl), BF), jax.ShapeDtypeStruct((tp, dl), F32)],
        (z, z, w4, cb, wa2, ba, wx2, bx, lam, g_out, gm), side)


def _lru_bwd(z, hs, dmix, w4, cb, wa2, ba, wx2, bx, lam, g_out, gm):
    tp = z.shape[0]
    dl = cb.shape[1]
    nb = dl // LANE
    r = tp // MIX_CHUNKS
    c = LANE

    def body(y_ref, x_ref, hs_ref, dm_ref, w4_ref, cb_ref, wa_ref, ba_ref, wx_ref, bx_ref, lam_ref, go_ref, gm_ref,
             dy_ref, dx_ref, small_ref, dwa_ref, dwx_ref, xc_buf, ga_buf, gx_buf, a_buf, dh_buf, dxc_buf):
        w4v = w4_ref[...]
        cbv = cb_ref[...]
        wa = wa_ref[...]
        wx = wx_ref[...]
        bav = ba_ref[...]
        bxv = bx_ref[...]
        gov = go_ref[...]
        gmv = gm_ref[...]
        lamv = lam_ref[...]
        sp = _softplus(-lamv)
        small_ref[...] = jnp.zeros_like(small_ref)
        dwa_ref[...] = jnp.zeros_like(dwa_ref)
        dwx_ref[...] = jnp.zeros_like(dwx_ref)
        a_buf[pl.ds(tp, 8), :] = jnp.zeros((8, c), F32)
        dxc_buf[pl.ds(tp, 8), :] = jnp.zeros((8, c), F32)

        def fwd_chunk(t0, first):
            win = _window_dn(x_ref, t0, r, first)
            xc = _conv4(win, w4v, cbv, r)
            ga, gx, la = _lru_gates(xc, wa, bav, wx, bxv, sp)
            xc_buf[pl.ds(t0, r), :] = xc
            ga_buf[pl.ds(t0, r), :] = ga
            gx_buf[pl.ds(t0, r), :] = gx
            a_buf[pl.ds(t0, r), :] = jnp.exp(la)
            h = hs_ref[pl.ds(t0, r), :]
            yv = y_ref[pl.ds(t0, r), :]
            ge = _gelu(yv)
            lo = h * ge
            rs = lax.rsqrt(_group_mean(lo * lo, gmv) + EPS)
            xh = lo * rs
            dm = dm_ref[pl.ds(t0, r), :]
            q = dm * gov
            dlo = rs * (q - xh * _group_mean(q * xh, gmv))
            small_ref[8:9, :] += jnp.sum(dm * xh, axis=0, keepdims=True)
            dh_buf[pl.ds(t0, r), :] = dlo * ge
            dy_ref[pl.ds(t0, r), :] = (dlo * h * _gelu_grad(yv)).astype(BF)

        fwd_chunk(0, True)

        def floop(ci, carry):
            fwd_chunk(pl.multiple_of(ci * r, 16), False)
            return carry

        lax.fori_loop(1, MIX_CHUNKS, floop, 0)

        def bwd_chunk(t0, vnext, first):
            ap = _shift_up(a_buf[pl.ds(t0, r + 8), :], 1, r)
            ac, uc = _tile_scan(ap, dh_buf[pl.ds(t0, r), :], True)
            for j in reversed(range(r // 8)):
                vj = uc[8 * j:8 * j + 8] + ac[8 * j:8 * j + 8] * vnext
                dh_buf[pl.ds(t0 + 8 * j, 8), :] = vj
                vnext = jnp.broadcast_to(vj[0:1], (8, c))
            v = dh_buf[pl.ds(t0, r), :]
            hprev = _shift_dn(_window_dn(hs_ref, t0, r, first), 1, r)
            xc = xc_buf[pl.ds(t0, r), :]
            ga = ga_buf[pl.ds(t0, r), :]
            gx = gx_buf[pl.ds(t0, r), :]
            a = a_buf[pl.ds(t0, r), :]
            em = _neg_expm1(-2.0 * LRU_C * ga * sp)
            mult = jnp.sqrt(em)
            dla = v * hprev * a - (v * gx * xc) * ((1.0 - em) / mult)
            dgx = v * mult * xc
            dxc = v * mult * gx
            dga = dla * (-LRU_C) * sp
            small_ref[7:8, :] += jnp.sum(dla * (-LRU_C) * ga, axis=0, keepdims=True)
            dpa = dga * ga * (1.0 - ga)
            dpx = dgx * gx * (1.0 - gx)
            small_ref[5:6, :] += jnp.sum(dpa, axis=0, keepdims=True)
            small_ref[6:7, :] += jnp.sum(dpx, axis=0, keepdims=True)
            dpab = dpa.astype(BF)
            dpxb = dpx.astype(BF)
            xb = xc.astype(BF)
            dxc = dxc + _dot_nt(dpab, wa) + _dot_nt(dpxb, wx)
            dwa_ref[...] += _dot_tn(xb, dpab)
            dwx_ref[...] += _dot_tn(xb, dpxb)
            dxc_buf[pl.ds(t0, r), :] = dxc
            small_ref[4:5, :] += jnp.sum(dxc, axis=0, keepdims=True)
            dwin = dxc_buf[pl.ds(t0, r + 8), :]
            dx_ref[pl.ds(t0, r), :] = (w4v[3:4] * dxc + w4v[2:3] * _shift_up(dwin, 1, r)
                                       + w4v[1:2] * _shift_up(dwin, 2, r) + w4v[0:1] * _shift_up(dwin, 3, r)).astype(BF)
            xwin = _window_dn(x_ref, t0, r, first)
            for k in range(4):
                small_ref[k:k + 1, :] += jnp.sum(dxc * _shift_dn(xwin, 3 - k, r), axis=0, keepdims=True)
            return vnext

        def bloop(it, vnext):
            ci = MIX_CHUNKS - 1 - it
            return bwd_chunk(pl.multiple_of(ci * r, 16), vnext, False)

        vn = lax.fori_loop(0, MIX_CHUNKS - 1, bloop, jnp.zeros((8, c), F32))
        bwd_chunk(0, vn, True)
        small_ref[7:8, :] = small_ref[7:8, :] * (-_sigmoid(-lamv))

    col = lambda off: pl.BlockSpec((tp, c), lambda j: (0, off + j))
    vec = pl.BlockSpec((1, c), lambda j: (0, j))
    mat = pl.BlockSpec((None, c, c), lambda j: (j, 0, 0))
    buf = pltpu.VMEM((tp, c), F32)
    bufp = pltpu.VMEM((tp + 8, c), F32)
    return pl.pallas_call(
        body, name="lru_bwd", grid=(nb,),
        in_specs=[col(0), col(nb), col(0), col(0), pl.BlockSpec((8, c), lambda j: (0, j)), vec, mat, vec, mat, vec,
                  vec, vec, pl.BlockSpec((c, c), lambda j: (0, 0))],
        out_specs=[col(0), col(0), pl.BlockSpec((16, c), lambda j: (0, j)), mat, mat],
        out_shape=[jax.ShapeDtypeStruct((tp, dl), BF), jax.ShapeDtypeStruct((tp, dl), BF),
                   jax.ShapeDtypeStruct((16, dl), F32), jax.ShapeDtypeStruct((nb, c, c), F32),
                   jax.ShapeDtypeStruct((nb, c, c), F32)],
        scratch_shapes=[buf, buf, buf, bufp, buf, bufp],
        compiler_params=_params(("arbitrary",)),
    )(z, z, hs, dmix, w4, cb, wa2, ba, wx2, bx, lam, g_out, gm)


def _sc_conv(cvwin, w3, r):
    return w3[2:3] * _shift_dn(cvwin, 0, r) + w3[1:2] * _shift_dn(cvwin, 1, r) + w3[0:1] * _shift_dn(cvwin, 2, r)


def _sc_fwd(z, w3, g_out, gm, dl):
    tp = z.shape[0]
    nb = dl // LANE
    r = tp // MIX_CHUNKS
    c = LANE

    def body(b_ref, c_ref, v_ref, w3_ref, go_ref, gm_ref, m_ref):
        w3v = w3_ref[...]
        gov = go_ref[...]
        gmv = gm_ref[...]

        def chunk(t0, first):
            cvwin = _window_dn(c_ref, t0, r, first) * _window_dn(v_ref, t0, r, first)
            so = b_ref[pl.ds(t0, r), :] * _sc_conv(cvwin, w3v, r)
            rs = lax.rsqrt(_group_mean(so * so, gmv) + EPS)
            m_ref[pl.ds(t0, r), :] = (so * rs * gov).astype(BF)

        chunk(0, True)

        def loop(ci, carry):
            chunk(pl.multiple_of(ci * r, 16), False)
            return carry

        lax.fori_loop(1, MIX_CHUNKS, loop, 0)

    col = lambda off: pl.BlockSpec((tp, c), lambda j: (0, off + j))
    return pl.pallas_call(
        body, name="sconv_fwd", grid=(nb,),
        in_specs=[col(2 * nb), col(3 * nb), col(4 * nb), pl.BlockSpec((8, c), lambda j: (0, j)),
                  pl.BlockSpec((1, c), lambda j: (0, j)), pl.BlockSpec((c, c), lambda j: (0, 0))],
        out_specs=col(0), out_shape=jax.ShapeDtypeStruct((tp, dl), BF),
        compiler_params=_params(("arbitrary",)),
    )(z, z, z, w3, g_out, gm)


def _sc_bwd(z, dmix, w3, g_out, gm, dl):
    tp = z.shape[0]
    nb = dl // LANE
    r = tp // MIX_CHUNKS
    c = LANE

    def body(b_ref, c_ref, v_ref, dm_ref, w3_ref, go_ref, gm_ref, db_ref, dc_ref, dv_ref, small_ref, dsc_buf):
        w3v = w3_ref[...]
        gov = go_ref[...]
        gmv = gm_ref[...]
        small_ref[...] = jnp.zeros_like(small_ref)
        dsc_buf[pl.ds(tp, 8), :] = jnp.zeros((8, c), F32)

        def chunk1(t0, first):
            cvwin = _window_dn(c_ref, t0, r, first) * _window_dn(v_ref, t0, r, first)
            sc = _sc_conv(cvwin, w3v, r)
            bv = b_ref[pl.ds(t0, r), :]
            so = bv * sc
            rs = lax.rsqrt(_group_mean(so * so, gmv) + EPS)
            xh = so * rs
            dm = dm_ref[pl.ds(t0, r), :]
            q = dm * gov
            dso = rs * (q - xh * _group_mean(q * xh, gmv))
            small_ref[3:4, :] += jnp.sum(dm * xh, axis=0, keepdims=True)
            db_ref[pl.ds(t0, r), :] = (dso * sc).astype(BF)
            dsc = dso * bv
            dsc_buf[pl.ds(t0, r), :] = dsc
            for k in range(3):
                small_ref[k:k + 1, :] += jnp.sum(dsc * _shift_dn(cvwin, 2 - k, r), axis=0, keepdims=True)

        chunk1(0, True)

        def loop1(ci, carry):
            chunk1(pl.multiple_of(ci * r, 16), False)
            return carry

        lax.fori_loop(1, MIX_CHUNKS, loop1, 0)

        def loop2(ci, carry):
            t0 = pl.multiple_of(ci * r, 16)
            dwin = dsc_buf[pl.ds(t0, r + 8), :]
            dcv = w3v[2:3] * _shift_up(dwin, 0, r) + w3v[1:2] * _shift_up(dwin, 1, r) + w3v[0:1] * _shift_up(dwin, 2, r)
            dc_ref[pl.ds(t0, r), :] = (dcv * v_ref[pl.ds(t0, r), :]).astype(BF)
            dv_ref[pl.ds(t0, r), :] = (dcv * c_ref[pl.ds(t0, r), :]).astype(BF)
            return carry

        lax.fori_loop(0, MIX_CHUNKS, loop2, 0)

    col = lambda off: pl.BlockSpec((tp, c), lambda j: (0, off + j))
    out = jax.ShapeDtypeStruct((tp, dl), BF)
    return pl.pallas_call(
        body, name="sconv_bwd", grid=(nb,),
        in_specs=[col(2 * nb), col(3 * nb), col(4 * nb), col(nb), pl.BlockSpec((8, c), lambda j: (0, j)),
                  pl.BlockSpec((1, c), lambda j: (0, j)), pl.BlockSpec((c, c), lambda j: (0, 0))],
        out_specs=[col(0), col(0), col(0), pl.BlockSpec((8, c), lambda j: (0, j))],
        out_shape=[out, out, out, jax.ShapeDtypeStruct((8, dl), F32)],
        scratch_shapes=[pltpu.VMEM((tp + 8, c), F32)],
        compiler_params=_params(("arbitrary",)),
    )(z, z, z, dmix, w3, g_out, gm)


def _cast_pad(w, rows_p, cols_p, chip, name):
    r, c = w.shape

    def body(chip_ref, w_ref, o_ref):
        if (rows_p, cols_p) != (r, c):
            o_ref[...] = jnp.zeros_like(o_ref)
        o_ref[0:r, 0:c] = w_ref[...].astype(BF)

    return pl.pallas_call(
        body, name=name, out_shape=jax.ShapeDtypeStruct((N_CHIP, rows_p, cols_p), BF),
        grid_spec=pltpu.PrefetchScalarGridSpec(
            num_scalar_prefetch=1, grid=(1,),
            in_specs=[pl.BlockSpec((r, c), lambda i, chip: (0, 0))],
            out_specs=pl.BlockSpec((None, rows_p, cols_p), lambda i, chip: (chip[0], 0, 0))),
        compiler_params=_params(("arbitrary",)),
    )(chip, w)


def _adamw_math(w, g, m, v):
    m2 = ADAM_B1 * m + (1.0 - ADAM_B1) * g
    v2 = ADAM_B2 * v + (1.0 - ADAM_B2) * (g * g)
    m_hat = m2 / (1.0 - ADAM_B1 ** ADAM_STEP)
    v_hat = v2 / (1.0 - ADAM_B2 ** ADAM_STEP)
    delta = -ADAM_LR * (m_hat / (jnp.sqrt(v_hat) + ADAM_EPS) + ADAM_WD * w)
    return delta, m2, v2


def _adamw(w, g, m, v, name, row_tiles, col_tiles, side=None):
    r, c = w.shape
    tr = r // row_tiles
    tc = c // col_tiles
    gc = g.shape[1] if col_tiles == 1 else tc

    def body(w_ref, g_ref, m_ref, v_ref, go_ref, d_ref, mo_ref, vo_ref):
        gv = g_ref[...][:, 0:tc]
        delta, m2, v2 = _adamw_math(w_ref[...], gv, m_ref[...], v_ref[...])
        go_ref[...] = gv
        d_ref[...] = delta
        mo_ref[...] = m2
        vo_ref[...] = v2

    spec = pl.BlockSpec((tr, tc), lambda i, j: (i, j))
    out = jax.ShapeDtypeStruct((r, c), F32)
    return _grid_call(body, name, (row_tiles, col_tiles), [spec, pl.BlockSpec((tr, gc), lambda i, j: (i, j)), spec, spec],
                      [spec] * 4, [out] * 4, (w, g, m, v), side)


def _adamw_small(w, g4, m, v):
    def body(w_ref, g_ref, m_ref, v_ref, go_ref, d_ref, mo_ref, vo_ref):
        g = (g_ref[0] + g_ref[1]) + (g_ref[2] + g_ref[3])
        delta, m2, v2 = _adamw_math(w_ref[...], g, m_ref[...], v_ref[...])
        go_ref[...] = g
        d_ref[...] = delta
        mo_ref[...] = m2
        vo_ref[...] = v2

    out = jax.ShapeDtypeStruct(w.shape, F32)
    spec = pl.BlockSpec(w.shape, lambda: (0, 0))
    return pl.pallas_call(body, name="adamw_small", in_specs=[spec, pl.BlockSpec(g4.shape, lambda: (0, 0, 0)), spec, spec],
                          out_specs=[spec] * 4, out_shape=[out] * 4, compiler_params=_params())(w, g4, m, v)


def _place():
    x, y, c = lax.axis_index("x"), lax.axis_index("y"), lax.axis_index("c")
    chips = [(1 - x, y), (x, 1 - y), (1 - x, 1 - y)]
    return x, y, c, chips


ANY = pl.BlockSpec(memory_space=pl.ANY)


def _gather_side(bufs):
    n = len(bufs)

    def copies(outs, sems):
        s_ici, r_ici, s_d2d, r_d2d = sems
        x, y, c, chips = _place()
        me = 2 * x + y

        def rows(w, chip, core):
            half = bufs[w].shape[1] // 2
            return outs[w].at[chip, pl.ds(core * half, half)]

        def ici(w, j, chip):
            px, py = chips[j]
            return pltpu.make_async_remote_copy(
                src_ref=rows(w, chip, c), dst_ref=rows(w, chip, c),
                send_sem=s_ici.at[w, j], recv_sem=r_ici.at[w, j], device_id=(px, py, c), device_id_type=MESH)

        def d2d(w, j, core):
            px, py = chips[j]
            return pltpu.make_async_remote_copy(
                src_ref=rows(w, 2 * px + py, core), dst_ref=rows(w, 2 * px + py, core),
                send_sem=s_d2d.at[w, j], recv_sem=r_d2d.at[w, j], device_id=(x, y, 1 - c), device_id_type=MESH)

        pairs = [(w, j) for w in range(n) for j in range(3)]
        return me, c, chips, ici, d2d, pairs

    def start(ins, outs, sems):
        me, c, chips, ici, d2d, pairs = copies(outs, sems)
        for w, j in pairs:
            ici(w, j, me).start()

    def finish(ins, outs, sems):
        me, c, chips, ici, d2d, pairs = copies(outs, sems)
        for w, j in pairs:
            ici(w, j, 2 * chips[j][0] + chips[j][1]).wait_recv()
            d2d(w, j, c).start()
        for w, j in pairs:
            d2d(w, j, 1 - c).wait_recv()
        for w, j in pairs:
            ici(w, j, me).wait_send()
            d2d(w, j, c).wait_send()

    dma = pltpu.SemaphoreType.DMA((n, 3))
    return _Side(list(bufs), [jax.ShapeDtypeStruct(b.shape, b.dtype) for b in bufs], {w: w for w in range(n)},
                 [dma, dma, dma, dma], start, finish)


def _run_side(side, name):
    sin, sout = len(side.ins), len(side.outs)

    def body(*refs):
        ins, outs, sems = refs[:sin], refs[sin:sin + sout], refs[sin + sout:]
        side.start(ins, outs, sems)
        side.finish(ins, outs, sems)

    return pl.pallas_call(
        body, name=name, out_shape=list(side.outs), in_specs=[ANY] * sin, out_specs=[ANY] * sout,
        scratch_shapes=list(side.sems), input_output_aliases=dict(side.alias))(*side.ins)


def _pair_exchange_side(grads):
    n = len(grads)

    def copies(ins, outs, sems):
        ssem, rsem = sems
        x, y, c, _ = _place()
        cps = []
        for w in range(n):
            half = grads[w].shape[1] // 2
            cps.append(pltpu.make_async_remote_copy(
                src_ref=ins[w].at[:, pl.ds((1 - c) * half, half)], dst_ref=outs[w],
                send_sem=ssem.at[w], recv_sem=rsem.at[w], device_id=(x, y, 1 - c), device_id_type=MESH))
        return cps

    def start(ins, outs, sems):
        for cp in copies(ins, outs, sems):
            cp.start()

    def finish(ins, outs, sems):
        for cp in copies(ins, outs, sems):
            cp.wait()

    dma = pltpu.SemaphoreType.DMA((n,))
    return _Side(list(grads), [jax.ShapeDtypeStruct((N_CHIP, g.shape[1] // 2, g.shape[2]), BF) for g in grads], {},
                 [dma, dma], start, finish)


def _sibling_copy_side(buf):
    def copy(ins, outs, sems):
        x, y, c, _ = _place()
        return pltpu.make_async_remote_copy(src_ref=ins[0], dst_ref=outs[0], send_sem=sems[0], recv_sem=sems[1],
                                            device_id=(x, y, 1 - c), device_id_type=MESH)

    return _Side([buf], [jax.ShapeDtypeStruct(buf.shape, buf.dtype)], {}, [pltpu.SemaphoreType.DMA, pltpu.SemaphoreType.DMA],
                 lambda i, o, s: copy(i, o, s).start(), lambda i, o, s: copy(i, o, s).wait())


def _slot_exchange_side(buf4):
    def copies(outs, sems, sending):
        ssem, rsem = sems
        x, y, c, chips = _place()
        me = 2 * x + y
        return [pltpu.make_async_remote_copy(
            src_ref=outs[0].at[me if sending else 2 * px + py], dst_ref=outs[0].at[me if sending else 2 * px + py],
            send_sem=ssem.at[j], recv_sem=rsem.at[j], device_id=(px, py, c), device_id_type=MESH)
            for j, (px, py) in enumerate(chips)]

    def start(ins, outs, sems):
        for cp in copies(outs, sems, True):
            cp.start()

    def finish(ins, outs, sems):
        for cp in copies(outs, sems, False):
            cp.wait_recv()
        for cp in copies(outs, sems, True):
            cp.wait_send()

    dma = pltpu.SemaphoreType.DMA((3,))
    return _Side([buf4], [jax.ShapeDtypeStruct(buf4.shape, buf4.dtype)], {0: 0}, [dma, dma], start, finish)


def _pair_sum(g, sib, core, name):
    _, r, cdim = g.shape
    half = r // 2

    def body(core_ref, g_ref, s_ref, o_ref):
        o_ref[...] = (g_ref[...].astype(F32) + s_ref[...].astype(F32)).astype(BF)

    return pl.pallas_call(
        body, name=name,
        grid_spec=pltpu.PrefetchScalarGridSpec(
            num_scalar_prefetch=1, grid=(N_CHIP,),
            in_specs=[pl.BlockSpec((None, half, cdim), lambda k, core: (k, core[0], 0)),
                      pl.BlockSpec((None, half, cdim), lambda k, core: (k, 0, 0))],
            out_specs=pl.BlockSpec((None, half, cdim), lambda k, core: (k, 0, 0))),
        out_shape=jax.ShapeDtypeStruct((N_CHIP, half, cdim), BF),
        compiler_params=_params(("arbitrary",)),
    )(core, g, sib)


def _chip_exchange_side(psums):
    n = len(psums)

    def copies(ins, outs, sems):
        ssem, rsem = sems
        x, y, c, chips = _place()
        return [pltpu.make_async_remote_copy(
            src_ref=ins[w].at[2 * px + py], dst_ref=outs[w].at[j],
            send_sem=ssem.at[w, j], recv_sem=rsem.at[w, j], device_id=(px, py, c), device_id_type=MESH)
            for w in range(n) for j, (px, py) in enumerate(chips)]

    def start(ins, outs, sems):
        for cp in copies(ins, outs, sems):
            cp.start()

    def finish(ins, outs, sems):
        for cp in copies(ins, outs, sems):
            cp.wait()

    dma = pltpu.SemaphoreType.DMA((n, 3))
    return _Side(list(psums), [jax.ShapeDtypeStruct((3,) + p.shape[1:], BF) for p in psums], {}, [dma, dma],
                 start, finish)


def _final_sum(g, sib, recv, sel, name):
    _, r, cdim = g.shape
    half = r // 2
    nt = 4
    th = half // nt

    def body(sel_ref, g_ref, s_ref, r_ref, o_ref):
        acc = g_ref[...].astype(F32) + s_ref[...].astype(F32)
        for j in range(3):
            acc = acc + r_ref[j].astype(F32)
        o_ref[...] = acc

    return pl.pallas_call(
        body, name=name,
        grid_spec=pltpu.PrefetchScalarGridSpec(
            num_scalar_prefetch=1, grid=(nt,),
            in_specs=[pl.BlockSpec((None, th, cdim), lambda i, sel: (sel[0], sel[1] * nt + i, 0)),
                      pl.BlockSpec((None, th, cdim), lambda i, sel: (sel[0], i, 0)),
                      pl.BlockSpec((3, th, cdim), lambda i, sel: (0, i, 0))],
            out_specs=pl.BlockSpec((th, cdim), lambda i, sel: (sel[1] * nt + i, 0))),
        out_shape=jax.ShapeDtypeStruct((r, cdim), F32),
        compiler_params=_params(("arbitrary",)),
    )(sel, g, sib, recv)


def _join_side(bufs):
    n = len(bufs)

    def copies(outs, sems, core_of):
        ssem, rsem = sems
        x, y, c, _ = _place()
        cps = []
        for w in range(n):
            half = bufs[w].shape[0] // 2
            rows = outs[w].at[pl.ds(core_of(c) * half, half)]
            cps.append(pltpu.make_async_remote_copy(
                src_ref=rows, dst_ref=rows, send_sem=ssem.at[w], recv_sem=rsem.at[w],
                device_id=(x, y, 1 - c), device_id_type=MESH))
        return cps

    def start(ins, outs, sems):
        for cp in copies(outs, sems, lambda c: c):
            cp.start()

    def finish(ins, outs, sems):
        for cp in copies(outs, sems, lambda c: 1 - c):
            cp.wait_recv()
        for cp in copies(outs, sems, lambda c: c):
            cp.wait_send()

    dma = pltpu.SemaphoreType.DMA((n,))
    return _Side(list(bufs), [jax.ShapeDtypeStruct(b.shape, F32) for b in bufs], {w: w for w in range(n)}, [dma, dma],
                 start, finish)


def _small_pair_sum(buf, sib, chip):
    rows, d = buf.shape

    def body(chip_ref, a_ref, b_ref, o_ref):
        o_ref[...] = a_ref[...] + b_ref[...]

    return pl.pallas_call(
        body, name="small_pair_sum", out_shape=jax.ShapeDtypeStruct((N_CHIP, rows, d), F32),
        grid_spec=pltpu.PrefetchScalarGridSpec(
            num_scalar_prefetch=1, grid=(1,),
            in_specs=[pl.BlockSpec((rows, d), lambda i, chip: (0, 0))] * 2,
            out_specs=pl.BlockSpec((None, rows, d), lambda i, chip: (chip[0], 0, 0))),
        compiler_params=_params(("arbitrary",)),
    )(chip, buf, sib)


def _small_all_reduce(buf, name):
    rows, d = buf.shape

    def body(in_ref, out_ref, sib, all4, ssem, rsem, psem, qsem):
        x, y, c, chips = _place()
        me = 2 * x + y
        to_sib = pltpu.make_async_remote_copy(src_ref=in_ref, dst_ref=sib, send_sem=ssem, recv_sem=rsem,
                                              device_id=(x, y, 1 - c), device_id_type=MESH)
        to_sib.start()
        to_sib.wait()
        all4[me] = in_ref[...] + sib[...]
        cps = [pltpu.make_async_remote_copy(src_ref=all4.at[me], dst_ref=all4.at[me], send_sem=psem.at[j],
                                            recv_sem=qsem.at[j], device_id=(px, py, c), device_id_type=MESH)
               for j, (px, py) in enumerate(chips)]
        for cp in cps:
            cp.start()
        for j, (px, py) in enumerate(chips):
            chip = 2 * px + py
            pltpu.make_async_remote_copy(src_ref=all4.at[chip], dst_ref=all4.at[chip], send_sem=psem.at[j],
                                         recv_sem=qsem.at[j], device_id=(px, py, c), device_id_type=MESH).wait_recv()
        for cp in cps:
            cp.wait_send()
        out_ref[...] = (all4[0] + all4[1]) + (all4[2] + all4[3])

    vm = pl.BlockSpec(memory_space=pltpu.VMEM)
    return pl.pallas_call(
        body, name=name, out_shape=jax.ShapeDtypeStruct((rows, d), F32),
        in_specs=[vm], out_specs=vm,
        scratch_shapes=[pltpu.VMEM((rows, d), F32), pltpu.VMEM((N_CHIP, rows, d), F32),
                        pltpu.SemaphoreType.DMA, pltpu.SemaphoreType.DMA,
                        pltpu.SemaphoreType.DMA((3,)), pltpu.SemaphoreType.DMA((3,))],
        compiler_params=_params(),
    )(buf)


def _pair_blocks(w):
    w4 = w.reshape(N_HEADS // 2, 2, HEAD, HEAD)
    eye = jnp.eye(2, dtype=w.dtype)
    return jnp.einsum("pirc,ij->pirjc", w4, eye).reshape(N_HEADS // 2, LANE, LANE)


def _unpair_blocks(w2):
    w5 = w2.reshape(N_HEADS // 2, 2, HEAD, 2, HEAD)
    return jnp.stack([w5[:, 0, :, 0, :], w5[:, 1, :, 1, :]], axis=1).reshape(N_HEADS, HEAD, HEAD)


def kernel(x, meta_tokens, ffn1_pre_g, ffn1_w_gate, ffn1_w_up, ffn1_w_down, ffn1_post_g, mix_pre_g, w_in, lru_conv_w, lru_conv_b, lru_w_a, lru_b_a, lru_w_x, lru_b_x, lru_lambda, sconv_w, lru_out_g, sconv_out_g, w_out, mix_post_g, ffn2_pre_g, ffn2_w_gate, ffn2_w_up, ffn2_w_down, ffn2_post_g, loss_target, m_meta_tokens, m_ffn1_pre_g, m_ffn1_w_gate, m_ffn1_w_up, m_ffn1_w_down, m_ffn1_post_g, m_mix_pre_g, m_w_in, m_lru_conv_w, m_lru_conv_b, m_lru_w_a, m_lru_b_a, m_lru_w_x, m_lru_b_x, m_lru_lambda, m_sconv_w, m_lru_out_g, m_sconv_out_g, m_w_out, m_mix_post_g, m_ffn2_pre_g, m_ffn2_w_gate, m_ffn2_w_up, m_ffn2_w_down, m_ffn2_post_g, v_meta_tokens, v_ffn1_pre_g, v_ffn1_w_gate, v_ffn1_w_up, v_ffn1_w_down, v_ffn1_post_g, v_mix_pre_g, v_w_in, v_lru_conv_w, v_lru_conv_b, v_lru_w_a, v_lru_b_a, v_lru_w_x, v_lru_b_x, v_lru_lambda, v_sconv_w, v_lru_out_g, v_sconv_out_g, v_w_out, v_mix_post_g, v_ffn2_pre_g, v_ffn2_w_gate, v_ffn2_w_up, v_ffn2_w_down, v_ffn2_post_g):
    seq, d = x.shape[1], x.shape[2]
    t_real = N_META + seq
    tp = _round_up(t_real, ROW_ALIGN)
    f4 = ffn1_w_gate.shape[2]
    f4p = _round_up(f4, LANE)
    dl = lru_conv_b.shape[1]
    cin = w_in.shape[2]
    xi, yi, ci = lax.axis_index("x"), lax.axis_index("y"), lax.axis_index("c")
    chip = 2 * xi + yi
    zero = jnp.zeros((), jnp.int32)

    transposed = ("ffn1_w_gate", "ffn1_w_up", "ffn2_w_gate", "ffn2_w_up")

    def view(k, a):
        return a[0].T if k in transposed else a[0]

    def unview(k, a):
        return (a.T if k in transposed else a)[None]

    big = {
        "ffn1_w_gate": (view("ffn1_w_gate", ffn1_w_gate), f4p, d), "ffn1_w_up": (view("ffn1_w_up", ffn1_w_up), f4p, d),
        "ffn1_w_down": (ffn1_w_down[0], f4p, d), "w_in": (w_in[0], d, cin), "w_out": (w_out[0], w_out.shape[1], d),
        "ffn2_w_gate": (view("ffn2_w_gate", ffn2_w_gate), f4p, d), "ffn2_w_up": (view("ffn2_w_up", ffn2_w_up), f4p, d),
        "ffn2_w_down": (ffn2_w_down[0], f4p, d),
    }
    names = list(big)
    chip1 = jnp.reshape(chip, (1,)).astype(jnp.int32)
    shard = {k: _cast_pad(big[k][0], big[k][1], big[k][2], chip1, "cast_" + k) for k in names}
    full = dict(zip(("ffn1_w_gate", "ffn1_w_up"),
                    _run_side(_gather_side([shard["ffn1_w_gate"], shard["ffn1_w_up"]]), "gather_ffn1_in")))

    gm = jnp.kron(jnp.eye(2, dtype=F32), jnp.full((HEAD, HEAD), 1.0 / HEAD, F32)).astype(BF)
    wa2 = _pair_blocks(lru_w_a[0])
    wx2 = _pair_blocks(lru_w_x[0])

    dlq = dl // N_CHIP
    dq = d // N_CHIP
    R_GAIN, R_LOSS, R_META, R_LRU, R_SC, R_WA = 0, 6, 8, 24, 40, 48
    n_wrows = (N_HEADS // 2) * LANE * LANE // d
    R_WX = R_WA + n_wrows
    R_END = R_WX + n_wrows

    def pack(gains, meta, lru16, sc8, wa_, wx_, loss=None):
        rows = [jnp.concatenate(gains, axis=0)]
        lossrow = jnp.zeros((2, d), F32)
        if loss is not None:
            lossrow = lossrow.at[0, 0].set(loss)
        rows.append(lossrow)
        rows.append(meta)
        rows.append(jnp.concatenate([lru16, jnp.zeros((16, d - dl), F32)], axis=1))
        rows.append(jnp.concatenate([sc8, jnp.zeros((8, d - dl), F32)], axis=1))
        rows.append(wa_.reshape(n_wrows, d))
        rows.append(wx_.reshape(n_wrows, d))
        return jnp.concatenate(rows, axis=0)

    def place_cols(blk, width, total):
        return lax.dynamic_update_slice(jnp.zeros((blk.shape[0], total), F32), blk, (zero, chip * width))

    def pack_params(meta_, g1pre, g1post, gmpre, gmpost, g2pre, g2post, cw, cbias, wa_, ba_, wx_, bx_, lam_, sw, lgo, sgo):
        lru16 = jnp.concatenate([place_cols(cw[0], dlq, dl), cbias, ba_, bx_, lam_, lgo, jnp.zeros((7, dl), F32)], axis=0)
        sc8 = jnp.concatenate([place_cols(sw[0], dlq, dl), sgo, jnp.zeros((4, dl), F32)], axis=0)
        return pack([g1pre, g1post, gmpre, gmpost, g2pre, g2post], place_cols(meta_, dq, d), lru16, sc8,
                    _pair_blocks(wa_[0]), _pair_blocks(wx_[0]))

    p_w = pack_params(meta_tokens, ffn1_pre_g, ffn1_post_g, mix_pre_g, mix_post_g, ffn2_pre_g, ffn2_post_g, lru_conv_w,
                      lru_conv_b, lru_w_a, lru_b_a, lru_w_x, lru_b_x, lru_lambda, sconv_w, lru_out_g, sconv_out_g)
    p_m = pack_params(m_meta_tokens, m_ffn1_pre_g, m_ffn1_post_g, m_mix_pre_g, m_mix_post_g, m_ffn2_pre_g, m_ffn2_post_g,
                      m_lru_conv_w, m_lru_conv_b, m_lru_w_a, m_lru_b_a, m_lru_w_x, m_lru_b_x, m_lru_lambda, m_sconv_w,
                      m_lru_out_g, m_sconv_out_g)
    p_v = pack_params(v_meta_tokens, v_ffn1_pre_g, v_ffn1_post_g, v_mix_pre_g, v_mix_post_g, v_ffn2_pre_g, v_ffn2_post_g,
                      v_lru_conv_w, v_lru_conv_b, v_lru_w_a, v_lru_b_a, v_lru_w_x, v_lru_b_x, v_lru_lambda, v_sconv_w,
                      v_lru_out_g, v_sconv_out_g)

    gathered = _small_all_reduce(jnp.where(ci == 0, p_w, 0.0)[R_META:R_WA], "small_weight_gather")
    meta_full = gathered[0:N_META]
    w4_full = gathered[R_LRU - R_META:R_LRU - R_META + 4, 0:dl]
    w3_full = gathered[R_SC - R_META:R_SC - R_META + 3, 0:dl]
    w4p = jnp.concatenate([w4_full, jnp.zeros((4, dl), F32)], axis=0)
    w3p = jnp.concatenate([w3_full, jnp.zeros((5, dl), F32)], axis=0)

    h0 = jnp.concatenate([meta_full, x[0], jnp.zeros((tp - t_real, d), F32)], axis=0)
    tgt = jnp.concatenate([jnp.zeros((N_META, d), F32), loss_target[0], jnp.zeros((tp - t_real, d), F32)], axis=0)

    n1 = _norm0(h0, ffn1_pre_g)
    (a1, b1, s1), got = _ffn_up(n1, full["ffn1_w_gate"], full["ffn1_w_up"], "ffn1_up",
                                _gather_side([shard["ffn1_w_down"], shard["w_in"]]))
    full["ffn1_w_down"], full["w_in"] = got
    f1, got = _row_matmul([(s1, full["ffn1_w_down"])], "ffn1_down", False, d, _gather_side([shard["w_out"]]))
    full["w_out"] = got[0]
    h1, u = _post_fwd(f1, h0, ffn1_post_g, mix_pre_g, 0.5, "ffn1_post")
    z, got = _col_matmul(u, full["w_in"], "in_proj", False, F32, _gather_side([shard["ffn2_w_gate"]]))
    full["ffn2_w_gate"] = got[0]
    (m_lru, hs), got = _lru_fwd(z, w4p, lru_conv_b, wa2.astype(BF), lru_b_a, wx2.astype(BF), lru_b_x, lru_lambda,
                                lru_out_g, gm, _gather_side([shard["ffn2_w_up"]]))
    full["ffn2_w_up"] = got[0]
    m_sc = _sc_fwd(z, w3p, sconv_out_g, gm, dl)
    mixed = jnp.concatenate([m_lru, m_sc], axis=1)
    p, _ = _row_matmul([(mixed, full["w_out"])], "out_proj", False, d)
    h2, n2 = _post_fwd(p, h1, mix_post_g, ffn2_pre_g, 1.0, "mix_post")
    (a2, b2, s2), got = _ffn_up(n2, full["ffn2_w_gate"], full["ffn2_w_up"], "ffn2_up",
                                _gather_side([shard["ffn2_w_down"]]))
    full["ffn2_w_down"] = got[0]
    f2, _ = _row_matmul([(s2, full["ffn2_w_down"])], "ffn2_down", False, d)
    dh3, df2, dg_ffn2_post, loss_part = _loss_bwd(f2, h2, tgt, ffn2_post_g, t_real)

    core = jnp.reshape(ci, (1,)).astype(jnp.int32)
    sel = jnp.stack([chip, ci]).astype(jnp.int32)
    red = {}

    def pair_side(k):
        return _pair_exchange_side([red[k][0]])

    def chip_side(k):
        return _chip_exchange_side([_pair_sum(red[k][0], red[k][1], core, "pair_sum_" + k)])

    (da2, db2), _ = _ffn_bwd_act(df2, full["ffn2_w_down"], a2, b2, "ffn2_bwd_act")
    g, _ = _wgrad_call(s2, df2, "ffn2_down_wgrad", tile_y=WGRAD_TILE_Y)
    red["ffn2_w_down"] = [g, None, None]
    g, got = _wgrad_call(da2, n2, "ffn2_gate_wgrad", tile_y=WGRAD_TILE_Y, side=pair_side("ffn2_w_down"))
    red["ffn2_w_down"][1] = got[0]
    red["ffn2_w_gate"] = [g, None, None]
    g, got = _wgrad_call(db2, n2, "ffn2_up_wgrad", tile_y=WGRAD_TILE_Y,
                         side=_merge_sides([pair_side("ffn2_w_gate"), chip_side("ffn2_w_down")]))
    red["ffn2_w_gate"][1], red["ffn2_w_down"][2] = got
    red["ffn2_w_up"] = [g, None, None]
    red["ffn2_w_up"][1] = _run_side(pair_side("ffn2_w_up"), "pair_exchange_ffn2_w_up")[0]
    dn2, got = _row_matmul([(da2, full["ffn2_w_gate"]), (db2, full["ffn2_w_up"])], "ffn2_bwd_up", False, d,
                           _merge_sides([chip_side("ffn2_w_gate"), chip_side("ffn2_w_up")]))
    red["ffn2_w_gate"][2], red["ffn2_w_up"][2] = got
    dh2, dp, dg_ffn2_pre, dg_mix_post = _pre_bwd(dn2, h2, dh3, ffn2_pre_g, "ffn2_pre_bwd", (p, mix_post_g, 1.0))
    dmixed, _ = _col_matmul(dp, full["w_out"], "out_proj_bwd", True, F32)
    g, _ = _wgrad_call(mixed, dp, "w_out_wgrad", x_width=mixed.shape[1] // N_CHIP, tile_y=WGRAD_TILE_Y)
    red["w_out"] = [g, None, None]
    dzy, dzx, lru_small, dwa2, dwx2 = _lru_bwd(z, hs, dmixed, w4p, lru_conv_b, wa2.astype(BF), lru_b_a, wx2.astype(BF),
                                               lru_b_x, lru_lambda, lru_out_g, gm)
    dzb, dzc, dzv, sc_small = _sc_bwd(z, dmixed, w3p, sconv_out_g, gm, dl)
    dz = jnp.concatenate([dzy, dzx, dzb, dzc, dzv], axis=1)
    g, got = _wgrad_call(u, dz, "w_in_wgrad", y_width=cin, tile_x=WGRAD_TILE_X, side=pair_side("w_out"))
    red["w_out"][1] = got[0]
    red["w_in"] = [g, None, None]
    du, got = _row_matmul([(dz, full["w_in"])], "in_proj_bwd", True, d,
                          _merge_sides([pair_side("w_in"), chip_side("w_out")]))
    red["w_in"][1], red["w_out"][2] = got
    dh1, df1, dg_mix_pre, dg_ffn1_post = _pre_bwd(du, h1, dh2, mix_pre_g, "mix_pre_bwd", (f1, ffn1_post_g, 0.5))
    (da1, db1), got = _ffn_bwd_act(df1, full["ffn1_w_down"], a1, b1, "ffn1_bwd_act", chip_side("w_in"))
    red["w_in"][2] = got[0]
    early = ["ffn2_w_down", "ffn2_w_gate", "ffn2_w_up", "w_out", "w_in"]
    late = ["ffn1_w_down", "ffn1_w_gate", "ffn1_w_up"]
    g, got = _wgrad_call(s1, df1, "ffn1_down_wgrad", tile_y=WGRAD_TILE_Y,
                         side=_join_side([_final_sum(*red[k], sel, "final_sum_" + k) for k in early]))
    gfull = dict(zip(early, got))
    red["ffn1_w_down"] = [g, None, None]
    g, got = _wgrad_call(da1, n1, "ffn1_gate_wgrad", tile_y=WGRAD_TILE_Y, side=pair_side("ffn1_w_down"))
    red["ffn1_w_down"][1] = got[0]
    red["ffn1_w_gate"] = [g, None, None]
    g, got = _wgrad_call(db1, n1, "ffn1_up_wgrad", tile_y=WGRAD_TILE_Y,
                         side=_merge_sides([pair_side("ffn1_w_gate"), chip_side("ffn1_w_down")]))
    red["ffn1_w_gate"][1], red["ffn1_w_down"][2] = got
    red["ffn1_w_up"] = [g, None, None]
    red["ffn1_w_up"][1] = _run_side(pair_side("ffn1_w_up"), "pair_exchange_ffn1_w_up")[0]
    dn1, got = _row_matmul([(da1, full["ffn1_w_gate"]), (db1, full["ffn1_w_up"])], "ffn1_bwd_up", False, d,
                           _merge_sides([chip_side("ffn1_w_gate"), chip_side("ffn1_w_up")]))
    red["ffn1_w_gate"][2], red["ffn1_w_up"][2] = got
    dh0, dg_ffn1_pre = _pre_bwd(dn1, h0, dh1, ffn1_pre_g, "ffn1_pre_bwd")

    grad_x = dh0[N_META:t_real][None]

    w_big = {"ffn1_w_gate": ffn1_w_gate, "ffn1_w_up": ffn1_w_up, "ffn1_w_down": ffn1_w_down, "w_in": w_in, "w_out": w_out,
             "ffn2_w_gate": ffn2_w_gate, "ffn2_w_up": ffn2_w_up, "ffn2_w_down": ffn2_w_down}
    m_big = {"ffn1_w_gate": m_ffn1_w_gate, "ffn1_w_up": m_ffn1_w_up, "ffn1_w_down": m_ffn1_w_down, "w_in": m_w_in,
             "w_out": m_w_out, "ffn2_w_gate": m_ffn2_w_gate, "ffn2_w_up": m_ffn2_w_up, "ffn2_w_down": m_ffn2_w_down}
    v_big = {"ffn1_w_gate": v_ffn1_w_gate, "ffn1_w_up": v_ffn1_w_up, "ffn1_w_down": v_ffn1_w_down, "w_in": v_w_in,
             "w_out": v_w_out, "ffn2_w_gate": v_ffn2_w_gate, "ffn2_w_up": v_ffn2_w_up, "ffn2_w_down": v_ffn2_w_down}
    b_grad, b_delta, b_newm, b_newv = {}, {}, {}, {}

    def big_adamw(k, side=None):
        wv, mv, vv = view(k, w_big[k]), view(k, m_big[k]), view(k, v_big[k])
        wide_rows = wv.shape[0] % 64 == 0
        (g_, d_, m_, v_), got = _adamw(wv, gfull[k], mv, vv, "adamw_" + k, 8 if wide_rows else 4, 1 if wide_rows else 2,
                                       side)
        b_grad[k], b_delta[k], b_newm[k], b_newv[k] = unview(k, g_), unview(k, d_), unview(k, m_), unview(k, v_)
        return got

    p_g_local = pack([dg_ffn1_pre, dg_ffn1_post, dg_mix_pre, dg_mix_post, dg_ffn2_pre, dg_ffn2_post], dh0[0:N_META],
                     lru_small, sc_small, dwa2, dwx2, loss=loss_part[0, 0])
    got = big_adamw("ffn2_w_down", _merge_sides([_join_side([_final_sum(*red[k], sel, "final_sum_" + k) for k in late]),
                                                 _sibling_copy_side(p_g_local)]))
    gfull.update(zip(late, got[:3]))
    p_g4 = _small_pair_sum(p_g_local, got[3], chip1)
    p_g4 = big_adamw("ffn2_w_gate", _slot_exchange_side(p_g4))[0]
    p_g, p_delta, p_newm, p_newv = _adamw_small(p_w, p_g4, p_m, p_v)
    loss = p_g[R_LOSS, 0]
    for k in names:
        if k not in b_grad:
            big_adamw(k)

    def unpack(buf):
        out = {}
        for i, k in enumerate(["ffn1_pre_g", "ffn1_post_g", "mix_pre_g", "mix_post_g", "ffn2_pre_g", "ffn2_post_g"]):
            out[k] = buf[R_GAIN + i:R_GAIN + i + 1]
        out["meta_tokens"] = lax.dynamic_slice(buf[R_META:R_META + N_META], (zero, chip * dq), (N_META, dq))
        lru = buf[R_LRU:R_LRU + 16, 0:dl]
        out["lru_conv_w"] = lax.dynamic_slice(lru[0:4], (zero, chip * dlq), (4, dlq))[None]
        out["lru_conv_b"] = lru[4:5]
        out["lru_b_a"] = lru[5:6]
        out["lru_b_x"] = lru[6:7]
        out["lru_lambda"] = lru[7:8]
        out["lru_out_g"] = lru[8:9]
        sc = buf[R_SC:R_SC + 8, 0:dl]
        out["sconv_w"] = lax.dynamic_slice(sc[0:3], (zero, chip * dlq), (3, dlq))[None]
        out["sconv_out_g"] = sc[3:4]
        out["lru_w_a"] = _unpair_blocks(buf[R_WA:R_WX].reshape(N_HEADS // 2, LANE, LANE))[None]
        out["lru_w_x"] = _unpair_blocks(buf[R_WX:R_END].reshape(N_HEADS // 2, LANE, LANE))[None]
        return out

    s_grad, s_delta, s_newm, s_newv = unpack(p_g), unpack(p_delta), unpack(p_newm), unpack(p_newv)

    order = ["meta_tokens", "ffn1_pre_g", "ffn1_w_gate", "ffn1_w_up", "ffn1_w_down", "ffn1_post_g", "mix_pre_g", "w_in",
             "lru_conv_w", "lru_conv_b", "lru_w_a", "lru_b_a", "lru_w_x", "lru_b_x", "lru_lambda", "sconv_w", "lru_out_g",
             "sconv_out_g", "w_out", "mix_post_g", "ffn2_pre_g", "ffn2_w_gate", "ffn2_w_up", "ffn2_w_down", "ffn2_post_g"]

    def pick(small, bigd):
        return [bigd[k] if k in bigd else small[k] for k in order]

    return (loss, grad_x, *pick(s_grad, b_grad), *pick(s_delta, b_delta), *pick(s_newm, b_newm), *pick(s_newv, b_newv))
```

```python
import functools
import math

import jax
import jax.numpy as jnp
from jax import lax
from jax.experimental import pallas as pl
from jax.experimental.pallas import tpu as pltpu

F32 = jnp.float32
BF = jnp.bfloat16
MESH = pl.DeviceIdType.MESH

EPS = 1e-6
N_META = 16
N_HEADS = 16
HEAD = 64
LRU_C = 8.0
LANE = 128
MXU_COLS = 256
N_CHIP = 4
ROW_ALIGN = 384
MM_TILES = 8
MM_TILES_BIG = 4
EW_TILES = 12
MIX_CHUNKS = 24
WGRAD_TILE_X = 256
WGRAD_TILE_Y = 512
VMEM_LIMIT = 56 << 20

ADAM_LR = 0.001
ADAM_B1 = 0.9
ADAM_B2 = 0.999
ADAM_EPS = 1e-08
ADAM_WD = 0.01
ADAM_STEP = 10


def _round_up(a, b):
    return (a + b - 1) // b * b


def _params(sem=None):
    if sem is None:
        return pltpu.CompilerParams(vmem_limit_bytes=VMEM_LIMIT)
    return pltpu.CompilerParams(dimension_semantics=sem, vmem_limit_bytes=VMEM_LIMIT)


def _sigmoid(x):
    return 1.0 / (1.0 + jnp.exp(-x))


def _dot(a, b):
    return jnp.dot(a, b, preferred_element_type=F32)


def _dot_nt(a, b):
    return lax.dot_general(a, b, (((1,), (1,)), ((), ())), preferred_element_type=F32)


def _dot_tn(a, b):
    return lax.dot_general(a, b, (((0,), (0,)), ((), ())), preferred_element_type=F32)


def _rms(x, g):
    r = lax.rsqrt(jnp.mean(x * x, axis=-1, keepdims=True) + EPS)
    return x * r * g


def _rms_bwd(x, g, dy):
    r = lax.rsqrt(jnp.mean(x * x, axis=-1, keepdims=True) + EPS)
    xh = x * r
    q = dy * g
    dx = r * (q - xh * jnp.mean(q * xh, axis=-1, keepdims=True))
    return dx, dy * xh


class _Side:
    def __init__(self, ins, outs, alias, sems, start, finish):
        self.ins, self.outs, self.alias, self.sems, self.start, self.finish = ins, outs, alias, sems, start, finish


def _merge_sides(sides):
    sides = [s for s in sides if s is not None]
    if len(sides) <= 1:
        return sides[0] if sides else None
    ins, outs, sems, alias, spans = [], [], [], {}, []
    for s in sides:
        for i, o in s.alias.items():
            alias[len(ins) + i] = len(outs) + o
        spans.append((len(ins), len(ins) + len(s.ins), len(outs), len(outs) + len(s.outs), len(sems),
                      len(sems) + len(s.sems)))
        ins += list(s.ins)
        outs += list(s.outs)
        sems += list(s.sems)

    def run(which):
        def go(in_refs, out_refs, sem_refs):
            for s, (a, b, c, d, e, f) in zip(sides, spans):
                getattr(s, which)(in_refs[a:b], out_refs[c:d], sem_refs[e:f])
        return go

    return _Side(ins, outs, alias, sems, run("start"), run("finish"))


def _grid_call(body, name, grid, in_specs, out_specs, out_shape, args, side=None):
    sem = ("arbitrary",) * len(grid)
    if side is None:
        res = pl.pallas_call(body, name=name, grid=grid, in_specs=in_specs, out_specs=out_specs, out_shape=out_shape,
                             compiler_params=_params(sem))(*args)
        return res, []
    nin, nout, sin, sout = len(in_specs), len(out_specs), len(side.ins), len(side.outs)

    def full(*refs):
        base_in, side_in = refs[:nin], refs[nin:nin + sin]
        base_out = refs[nin + sin:nin + sin + nout]
        side_out = refs[nin + sin + nout:nin + sin + nout + sout]
        sems = refs[nin + sin + nout + sout:]
        first = pl.program_id(0) == 0
        last = pl.program_id(0) == grid[0] - 1
        for ax in range(1, len(grid)):
            first = first & (pl.program_id(ax) == 0)
            last = last & (pl.program_id(ax) == grid[ax] - 1)

        @pl.when(first)
        def _():
            side.start(side_in, side_out, sems)

        body(*base_in, *base_out)

        @pl.when(last)
        def _():
            side.finish(side_in, side_out, sems)

    any_spec = pl.BlockSpec(memory_space=pl.ANY)
    res = pl.pallas_call(
        full, name=name, grid=grid, in_specs=list(in_specs) + [any_spec] * sin,
        out_specs=list(out_specs) + [any_spec] * sout, out_shape=list(out_shape) + list(side.outs),
        scratch_shapes=list(side.sems), input_output_aliases={nin + i: nout + o for i, o in side.alias.items()},
        compiler_params=_params(sem))(*args, *side.ins)
    return res[:nout], res[nout:]


def _ffn_up(n, wg, wu, name, side=None, tiles=MM_TILES_BIG):
    tp, d = n.shape
    fp = wg.shape[1]
    tm = tp // tiles

    def body(n_ref, wg_ref, wu_ref, a_ref, b_ref, s_ref):
        nn = n_ref[...]
        for c0 in range(0, fp, MXU_COLS):
            cs = slice(c0, min(c0 + MXU_COLS, fp))
            a = _dot_nt(nn, wg_ref[cs, :])
            b = _dot_nt(nn, wu_ref[cs, :])
            a_ref[:, cs] = a.astype(BF)
            b_ref[:, cs] = b.astype(BF)
            s_ref[:, cs] = (a * _sigmoid(a) * b).astype(BF)

    out = jax.ShapeDtypeStruct((N_CHIP, tp, fp), BF)
    wspec = pl.BlockSpec((None, fp, d), lambda k, i: (k, 0, 0), pipeline_mode=pl.Buffered(1))
    ospec = pl.BlockSpec((None, tm, fp), lambda k, i: (k, i, 0))
    return _grid_call(body, name, (N_CHIP, tiles), [pl.BlockSpec((tm, d), lambda k, i: (i, 0)), wspec, wspec],
                      [ospec, ospec, ospec], [out, out, out], (n, wg, wu), side)


def _ffn_bwd_act(df, wd, a, b, name, side=None, tiles=MM_TILES_BIG):
    tp, d = df.shape
    fp = wd.shape[1]
    tm = tp // tiles

    def body(df_ref, wd_ref, a_ref, b_ref, da_ref, db_ref):
        dfv = df_ref[...]
        for c0 in range(0, fp, MXU_COLS):
            cs = slice(c0, min(c0 + MXU_COLS, fp))
            ds = _dot_nt(dfv, wd_ref[cs, :])
            av = a_ref[:, cs].astype(F32)
            bv = b_ref[:, cs].astype(F32)
            sg = _sigmoid(av)
            da_ref[:, cs] = (ds * bv * sg * (1.0 + av * (1.0 - sg))).astype(BF)
            db_ref[:, cs] = (ds * av * sg).astype(BF)

    out = jax.ShapeDtypeStruct((N_CHIP, tp, fp), BF)
    aspec = pl.BlockSpec((None, tm, fp), lambda k, i: (k, i, 0))
    return _grid_call(
        body, name, (N_CHIP, tiles),
        [pl.BlockSpec((tm, d), lambda k, i: (i, 0)),
         pl.BlockSpec((None, fp, d), lambda k, i: (k, 0, 0), pipeline_mode=pl.Buffered(1)), aspec, aspec],
        [aspec, aspec], [out, out], (df, wd, a, b), side)


def _col_matmul(lhs, w, name, trans_b, out_dtype, side=None, tiles=MM_TILES_BIG):
    tp, kd = lhs.shape
    nk = w.shape[0]
    nc = w.shape[1] if trans_b else w.shape[2]
    tm = tp // tiles

    def body(l_ref, w_ref, o_ref):
        if trans_b:
            o_ref[...] = _dot_nt(l_ref[...], w_ref[...]).astype(out_dtype)
        else:
            o_ref[...] = _dot(l_ref[...], w_ref[...]).astype(out_dtype)

    res, extra = _grid_call(
        body, name, (nk, tiles),
        [pl.BlockSpec((tm, kd), lambda k, i: (i, 0)),
         pl.BlockSpec((None,) + tuple(w.shape[1:]), lambda k, i: (k, 0, 0), pipeline_mode=pl.Buffered(1))],
        [pl.BlockSpec((tm, nc), lambda k, i: (i, k))], [jax.ShapeDtypeStruct((tp, nk * nc), out_dtype)], (lhs, w), side)
    return res[0], extra


def _row_matmul(pairs, name, trans_b, d_out, side=None, tiles=MM_TILES_BIG):
    l0 = pairs[0][0]
    tp = l0.shape[1] if l0.ndim == 3 else l0.shape[0]
    nk = pairs[0][1].shape[0]
    tm = tp // tiles
    npair = len(pairs)

    def body(*refs):
        o_ref = refs[2 * npair]
        k = pl.program_id(1)
        part = None
        for q in range(npair):
            l = refs[2 * q][...]
            w = refs[2 * q + 1][...]
            t = _dot_nt(l, w) if trans_b else _dot(l, w)
            part = t if part is None else part + t

        @pl.when(k == 0)
        def _():
            o_ref[...] = part

        @pl.when(k > 0)
        def _():
            o_ref[...] += part

    in_specs, args = [], []
    for lhs, w in pairs:
        if lhs.ndim == 3:
            in_specs.append(pl.BlockSpec((None, tm, lhs.shape[2]), lambda i, k: (k, i, 0)))
        else:
            in_specs.append(pl.BlockSpec((tm, lhs.shape[1] // nk), lambda i, k: (i, k)))
        in_specs.append(pl.BlockSpec((None,) + tuple(w.shape[1:]), lambda i, k: (k, 0, 0)))
        args += [lhs, w]
    res, extra = _grid_call(body, name, (tiles, nk), in_specs, [pl.BlockSpec((tm, d_out), lambda i, k: (i, 0))],
                            [jax.ShapeDtypeStruct((tp, d_out), F32)], args, side)
    return res[0], extra


def _wgrad_call(x, y, name, x_width=None, y_width=None, tile_x=None, tile_y=None, side=None):
    tp = x.shape[1] if x.ndim == 3 else x.shape[0]

    def spec(a, width, tile):
        cols = a.shape[2] if a.ndim == 3 else (a.shape[1] if width is None else width)
        tc = cols if tile is None else tile
        per = cols // tc
        if a.ndim == 3:
            return pl.BlockSpec((None, tp, tc), lambda k, t: (k, 0, t if tile else 0)), cols, per
        if width is None:
            return pl.BlockSpec((tp, tc), lambda k, t: (0, t if tile else 0)), cols, per
        return pl.BlockSpec((tp, tc), lambda k, t: (0, k * per + (t if tile else 0))), cols, per

    xs, p, nx = spec(x, x_width, tile_x)
    ys, q, ny = spec(y, y_width, tile_y)
    nt = nx * ny
    if tile_x:
        ospec = pl.BlockSpec((None, tile_x, q), lambda k, t: (k, t, 0))
    else:
        ospec = pl.BlockSpec((None, p, tile_y), lambda k, t: (k, 0, t))

    def body(x_ref, y_ref, o_ref):
        o_ref[...] = _dot_tn(x_ref[...], y_ref[...]).astype(BF)

    res, extra = _grid_call(body, name, (N_CHIP, nt), [xs, ys], [ospec], [jax.ShapeDtypeStruct((N_CHIP, p, q), BF)],
                            (x, y), side)
    return res[0], extra


def _row_call(body, name, tp, d, row_ins, vec_ins, row_out_dtypes, n_acc, side=None):
    te = tp // EW_TILES
    rspec = pl.BlockSpec((te, d), lambda i: (i, 0))
    vspec = pl.BlockSpec((1, d), lambda i: (0, 0))
    res, extra = _grid_call(
        body, name, (EW_TILES,), [rspec] * len(row_ins) + [vspec] * len(vec_ins),
        [rspec] * len(row_out_dtypes) + [vspec] * n_acc,
        [jax.ShapeDtypeStruct((tp, d), dt) for dt in row_out_dtypes] + [jax.ShapeDtypeStruct((1, d), F32)] * n_acc,
        (*row_ins, *vec_ins), side)
    return res if side is None else (res, extra)


def _norm0(h, g):
    tp, d = h.shape

    def body(h_ref, g_ref, n_ref):
        n_ref[...] = _rms(h_ref[...], g_ref[...]).astype(BF)

    return _row_call(body, "norm0", tp, d, [h], [g], [BF], 0)[0]


def _post_fwd(f, h, g_post, g_next, scale, name):
    tp, d = h.shape

    def body(f_ref, h_ref, gp_ref, gn_ref, hn_ref, n_ref):
        hn = h_ref[...] + scale * _rms(f_ref[...], gp_ref[...])
        hn_ref[...] = hn
        n_ref[...] = _rms(hn, gn_ref[...]).astype(BF)

    return _row_call(body, name, tp, d, [f, h], [g_post, g_next], [F32, BF], 0)


def _loss_bwd(f, h, tgt, g_post, t_real):
    tp, d = h.shape
    te = tp // EW_TILES

    def body(f_ref, h_ref, t_ref, gp_ref, dh_ref, df_ref, dg_ref, loss_ref):
        i = pl.program_id(0)

        @pl.when(i == 0)
        def _():
            dg_ref[...] = jnp.zeros_like(dg_ref)
            loss_ref[...] = jnp.zeros_like(loss_ref)

        f = f_ref[...]
        gp = gp_ref[...]
        h3 = h_ref[...] + 0.5 * _rms(f, gp)
        rows = i * te + lax.broadcasted_iota(jnp.int32, (te, 1), 0)
        real = (rows >= N_META) & (rows < t_real)
        e = jnp.where(real, h3 - t_ref[...], 0.0)
        loss_ref[...] += 0.5 * jnp.sum(jnp.sum(e * e, axis=1, keepdims=True), axis=0, keepdims=True) / d
        dh = e / d
        dh_ref[...] = dh
        dfv, dgr = _rms_bwd(f, gp, 0.5 * dh)
        df_ref[...] = dfv.astype(BF)
        dg_ref[...] += jnp.sum(dgr, axis=0, keepdims=True)

    rspec = pl.BlockSpec((te, d), lambda i: (i, 0))
    vspec = pl.BlockSpec((1, d), lambda i: (0, 0))
    return pl.pallas_call(
        body, name="loss_bwd", grid=(EW_TILES,),
        in_specs=[rspec, rspec, rspec, vspec],
        out_specs=[rspec, rspec, vspec, pl.BlockSpec((1, 1), lambda i: (0, 0))],
        out_shape=[jax.ShapeDtypeStruct((tp, d), F32), jax.ShapeDtypeStruct((tp, d), BF),
                   jax.ShapeDtypeStruct((1, d), F32), jax.ShapeDtypeStruct((1, 1), F32)],
        compiler_params=_params(("arbitrary",)),
    )(f, h, tgt, g_post)


def _pre_bwd(dn, h, dh_out, g_pre, name, chain=None, side=None):
    tp, d = h.shape

    def body(*refs):
        if chain is None:
            dn_ref, h_ref, dho_ref, g_ref, dh_ref, dg_ref = refs
        else:
            dn_ref, h_ref, dho_ref, p_ref, g_ref, gp_ref, dh_ref, dp_ref, dg_ref, dgp_ref = refs
        i = pl.program_id(0)

        @pl.when(i == 0)
        def _():
            dg_ref[...] = jnp.zeros_like(dg_ref)
            if chain is not None:
                dgp_ref[...] = jnp.zeros_like(dgp_ref)

        dx, dgr = _rms_bwd(h_ref[...], g_ref[...], dn_ref[...])
        dh = dho_ref[...] + dx
        dh_ref[...] = dh
        dg_ref[...] += jnp.sum(dgr, axis=0, keepdims=True)
        if chain is not None:
            dp, dgpr = _rms_bwd(p_ref[...], gp_ref[...], chain[2] * dh)
            dp_ref[...] = dp.astype(BF)
            dgp_ref[...] += jnp.sum(dgpr, axis=0, keepdims=True)

    if chain is None:
        return _row_call(body, name, tp, d, [dn, h, dh_out], [g_pre], [F32], 1, side)
    return _row_call(body, name, tp, d, [dn, h, dh_out, chain[0]], [g_pre, chain[1]], [F32, BF], 2, side)


def _gelu(y):
    c = math.sqrt(2.0 / math.pi)
    return 0.5 * y * (1.0 + jnp.tanh(c * (y + 0.044715 * y * y * y)))


def _gelu_grad(y):
    c = math.sqrt(2.0 / math.pi)
    t = jnp.tanh(c * (y + 0.044715 * y * y * y))
    return 0.5 * (1.0 + t) + 0.5 * y * (1.0 - t * t) * c * (1.0 + 3.0 * 0.044715 * y * y)


def _neg_expm1(x):
    p = 1.0 + x * (1.0 / 9.0)
    for n in (8.0, 7.0, 6.0, 5.0, 4.0, 3.0, 2.0):
        p = 1.0 + x * (1.0 / n) * p
    return -jnp.where(x > -0.35, x * p, jnp.exp(x) - 1.0)


def _softplus(x):
    e = jnp.exp(-jnp.abs(x))
    w = 1.0 + e
    l1p = jnp.where(w == 1.0, e, jnp.log(w) * (e / jnp.where(w == 1.0, 1.0, w - 1.0)))
    return jnp.maximum(x, 0.0) + l1p


def _group_mean(v, gm):
    hi = v.astype(BF)
    lo = (v - hi.astype(F32)).astype(BF)
    return _dot(hi, gm) + _dot(lo, gm)


def _shift_dn(win, s, r):
    if s == 0:
        return win[8:8 + r]
    return pltpu.roll(win, s, 0)[8:8 + r]


def _shift_up(win, s, r):
    if s == 0:
        return win[0:r]
    return pltpu.roll(win, r + 8 - s, 0)[0:r]


def _window_dn(ref, t0, r, first):
    if first:
        return jnp.concatenate([jnp.zeros((8, ref.shape[1]), F32), ref[0:r, :]], axis=0)
    return ref[pl.ds(t0 - 8, r + 8), :]


def _tile_scan(a, u, reverse):
    r = a.shape[0]
    rid = lax.broadcasted_iota(jnp.int32, a.shape, 0) & 7
    for dlt in (1, 2, 4):
        sh = (r - dlt) if reverse else dlt
        a_s = pltpu.roll(a, sh, 0)
        u_s = pltpu.roll(u, sh, 0)
        keep = (rid + dlt <= 7) if reverse else (rid >= dlt)
        u = jnp.where(keep, u + a * u_s, u)
        a = jnp.where(keep, a * a_s, a)
    return a, u


def _lru_gates(xc, wa, ba, wx, bx, sp):
    xb = xc.astype(BF)
    ga = _sigmoid(_dot(xb, wa) + ba)
    gx = _sigmoid(_dot(xb, wx) + bx)
    la = -LRU_C * ga * sp
    return ga, gx, la


def _conv4(win, w4, cb, r):
    return (cb + w4[3:4] * _shift_dn(win, 0, r) + w4[2:3] * _shift_dn(win, 1, r)
            + w4[1:2] * _shift_dn(win, 2, r) + w4[0:1] * _shift_dn(win, 3, r))


def _lru_fwd(z, w4, cb, wa2, ba, wx2, bx, lam, g_out, gm, side=None):
    tp = z.shape[0]
    dl = cb.shape[1]
    nb = dl // LANE
    r = tp // MIX_CHUNKS
    c = LANE

    def body(y_ref, x_ref, w4_ref, cb_ref, wa_ref, ba_ref, wx_ref, bx_ref, lam_ref, go_ref, gm_ref, m_ref, hs_ref):
        w4v = w4_ref[...]
        cbv = cb_ref[...]
        wa = wa_ref[...]
        wx = wx_ref[...]
        bav = ba_ref[...]
        bxv = bx_ref[...]
        gov = go_ref[...]
        gmv = gm_ref[...]
        sp = _softplus(-lam_ref[...])

        def chunk(t0, hprev, first):
            win = _window_dn(x_ref, t0, r, first)
            xc = _conv4(win, w4v, cbv, r)
            ga, gx, la = _lru_gates(xc, wa, bav, wx, bxv, sp)
            a = jnp.exp(la)
            u = jnp.sqrt(_neg_expm1(2.0 * la)) * gx * xc
            ac, uc = _tile_scan(a, u, False)
            for j in range(r // 8):
                hj = uc[8 * j:8 * j + 8] + ac[8 * j:8 * j + 8] * hprev
                hs_ref[pl.ds(t0 + 8 * j, 8), :] = hj
                hprev = jnp.broadcast_to(hj[7:8], (8, c))
            h = hs_ref[pl.ds(t0, r), :]
            lo = h * _gelu(y_ref[pl.ds(t0, r), :])
            rs = lax.rsqrt(_group_mean(lo * lo, gmv) + EPS)
            m_ref[pl.ds(t0, r), :] = (lo * rs * gov).astype(BF)
            return hprev

        hp = chunk(0, jnp.zeros((8, c), F32), True)

        def loop(ci, hp):
            return chunk(pl.multiple_of(ci * r, 16), hp, False)

        lax.fori_loop(1, MIX_CHUNKS, loop, hp)

    col = lambda off: pl.BlockSpec((tp, c), lambda j: (0, off + j))
    vec = pl.BlockSpec((1, c), lambda j: (0, j))
    return _grid_call(
        body, "lru_fwd", (nb,),
        [col(0), col(nb), pl.BlockSpec((8, c), lambda j: (0, j)), vec, pl.BlockSpec((None, c, c), lambda j: (j, 0, 0)),
         vec, pl.BlockSpec((None, c, c), lambda j: (j, 0, 0)), vec, vec, vec, pl.BlockSpec((c, c), lambda j: (0, 0))],
        [col(0), col(0)], [jax.ShapeDtypeStruct((tp, dl), BF), jax.ShapeDtypeStruct((tp, dl), F32)],
        (z, z, w4, cb, wa2, ba, wx2, bx, lam, g_out, gm), side)


def _lru_bwd(z, hs, dmix, w4, cb, wa2, ba, wx2, bx, lam, g_out, gm):
    tp = z.shape[0]
    dl = cb.shape[1]
    nb = dl // LANE
    r = tp // MIX_CHUNKS
    c = LANE

    def body(y_ref, x_ref, hs_ref, dm_ref, w4_ref, cb_ref, wa_ref, ba_ref, wx_ref, bx_ref, lam_ref, go_ref, gm_ref,
             dy_ref, dx_ref, small_ref, dwa_ref, dwx_ref, xc_buf, ga_buf, gx_buf, a_buf, dh_buf, dxc_buf):
        w4v = w4_ref[...]
        cbv = cb_ref[...]
        wa = wa_ref[...]
        wx = wx_ref[...]
        bav = ba_ref[...]
        bxv = bx_ref[...]
        gov = go_ref[...]
        gmv = gm_ref[...]
        lamv = lam_ref[...]
        sp = _softplus(-lamv)
        small_ref[...] = jnp.zeros_like(small_ref)
        dwa_ref[...] = jnp.zeros_like(dwa_ref)
        dwx_ref[...] = jnp.zeros_like(dwx_ref)
        a_buf[pl.ds(tp, 8), :] = jnp.zeros((8, c), F32)
        dxc_buf[pl.ds(tp, 8), :] = jnp.zeros((8, c), F32)

        def fwd_chunk(t0, first):
            win = _window_dn(x_ref, t0, r, first)
            xc = _conv4(win, w4v, cbv, r)
            ga, gx, la = _lru_gates(xc, wa, bav, wx, bxv, sp)
            xc_buf[pl.ds(t0, r), :] = xc
            ga_buf[pl.ds(t0, r), :] = ga
            gx_buf[pl.ds(t0, r), :] = gx
            a_buf[pl.ds(t0, r), :] = jnp.exp(la)
            h = hs_ref[pl.ds(t0, r), :]
            yv = y_ref[pl.ds(t0, r), :]
            ge = _gelu(yv)
            lo = h * ge
            rs = lax.rsqrt(_group_mean(lo * lo, gmv) + EPS)
            xh = lo * rs
            dm = dm_ref[pl.ds(t0, r), :]
            q = dm * gov
            dlo = rs * (q - xh * _group_mean(q * xh, gmv))
            small_ref[8:9, :] += jnp.sum(dm * xh, axis=0, keepdims=True)
            dh_buf[pl.ds(t0, r), :] = dlo * ge
            dy_ref[pl.ds(t0, r), :] = (dlo * h * _gelu_grad(yv)).astype(BF)

        fwd_chunk(0, True)

        def floop(ci, carry):
            fwd_chunk(pl.multiple_of(ci * r, 16), False)
            return carry

        lax.fori_loop(1, MIX_CHUNKS, floop, 0)

        def bwd_chunk(t0, vnext, first):
            ap = _shift_up(a_buf[pl.ds(t0, r + 8), :], 1, r)
            ac, uc = _tile_scan(ap, dh_buf[pl.ds(t0, r), :], True)
            for j in reversed(range(r // 8)):
                vj = uc[8 * j:8 * j + 8] + ac[8 * j:8 * j + 8] * vnext
                dh_buf[pl.ds(t0 + 8 * j, 8), :] = vj
                vnext = jnp.broadcast_to(vj[0:1], (8, c))
            v = dh_buf[pl.ds(t0, r), :]
            hprev = _shift_dn(_window_dn(hs_ref, t0, r, first), 1, r)
            xc = xc_buf[pl.ds(t0, r), :]
            ga = ga_buf[pl.ds(t0, r), :]
            gx = gx_buf[pl.ds(t0, r), :]
            a = a_buf[pl.ds(t0, r), :]
            em = _neg_expm1(-2.0 * LRU_C * ga * sp)
            mult = jnp.sqrt(em)
            dla = v * hprev * a - (v * gx * xc) * ((1.0 - em) / mult)
            dgx = v * mult * xc
            dxc = v * mult * gx
            dga = dla * (-LRU_C) * sp
            small_ref[7:8, :] += jnp.sum(dla * (-LRU_C) * ga, axis=0, keepdims=True)
            dpa = dga * ga * (1.0 - ga)
            dpx = dgx * gx * (1.0 - gx)
            small_ref[5:6, :] += jnp.sum(dpa, axis=0, keepdims=True)
            small_ref[6:7, :] += jnp.sum(dpx, axis=0, keepdims=True)
            dpab = dpa.astype(BF)
            dpxb = dpx.astype(BF)
            xb = xc.astype(BF)
            dxc = dxc + _dot_nt(dpab, wa) + _dot_nt(dpxb, wx)
            dwa_ref[...] += _dot_tn(xb, dpab)
            dwx_ref[...] += _dot_tn(xb, dpxb)
            dxc_buf[pl.ds(t0, r), :] = dxc
            small_ref[4:5, :] += jnp.sum(dxc, axis=0, keepdims=True)
            dwin = dxc_buf[pl.ds(t0, r + 8), :]
            dx_ref[pl.ds(t0, r), :] = (w4v[3:4] * dxc + w4v[2:3] * _shift_up(dwin, 1, r)
                                       + w4v[1:2] * _shift_up(dwin, 2, r) + w4v[0:1] * _shift_up(dwin, 3, r)).astype(BF)
            xwin = _window_dn(x_ref, t0, r, first)
            for k in range(4):
                small_ref[k:k + 1, :] += jnp.sum(dxc * _shift_dn(xwin, 3 - k, r), axis=0, keepdims=True)
            return vnext

        def bloop(it, vnext):
            ci = MIX_CHUNKS - 1 - it
            return bwd_chunk(pl.multiple_of(ci * r, 16), vnext, False)

        vn = lax.fori_loop(0, MIX_CHUNKS - 1, bloop, jnp.zeros((8, c), F32))
        bwd_chunk(0, vn, True)
        small_ref[7:8, :] = small_ref[7:8, :] * (-_sigmoid(-lamv))

    col = lambda off: pl.BlockSpec((tp, c), lambda j: (0, off + j))
    vec = pl.BlockSpec((1, c), lambda j: (0, j))
    mat = pl.BlockSpec((None, c, c), lambda j: (j, 0, 0))
    buf = pltpu.VMEM((tp, c), F32)
    bufp = pltpu.VMEM((tp + 8, c), F32)
    return pl.pallas_call(
        body, name="lru_bwd", grid=(nb,),
        in_specs=[col(0), col(nb), col(0), col(0), pl.BlockSpec((8, c), lambda j: (0, j)), vec, mat, vec, mat, vec,
                  vec, vec, pl.BlockSpec((c, c), lambda j: (0, 0))],
        out_specs=[col(0), col(0), pl.BlockSpec((16, c), lambda j: (0, j)), mat, mat],
        out_shape=[jax.ShapeDtypeStruct((tp, dl), BF), jax.ShapeDtypeStruct((tp, dl), BF),
                   jax.ShapeDtypeStruct((16, dl), F32), jax.ShapeDtypeStruct((nb, c, c), F32),
                   jax.ShapeDtypeStruct((nb, c, c), F32)],
        scratch_shapes=[buf, buf, buf, bufp, buf, bufp],
        compiler_params=_params(("arbitrary",)),
    )(z, z, hs, dmix, w4, cb, wa2, ba, wx2, bx, lam, g_out, gm)


def _sc_conv(cvwin, w3, r):
    return w3[2:3] * _shift_dn(cvwin, 0, r) + w3[1:2] * _shift_dn(cvwin, 1, r) + w3[0:1] * _shift_dn(cvwin, 2, r)


def _sc_fwd(z, w3, g_out, gm, dl):
    tp = z.shape[0]
    nb = dl // LANE
    r = tp // MIX_CHUNKS
    c = LANE

    def body(b_ref, c_ref, v_ref, w3_ref, go_ref, gm_ref, m_ref):
        w3v = w3_ref[...]
        gov = go_ref[...]
        gmv = gm_ref[...]

        def chunk(t0, first):
            cvwin = _window_dn(c_ref, t0, r, first) * _window_dn(v_ref, t0, r, first)
            so = b_ref[pl.ds(t0, r), :] * _sc_conv(cvwin, w3v, r)
            rs = lax.rsqrt(_group_mean(so * so, gmv) + EPS)
            m_ref[pl.ds(t0, r), :] = (so * rs * gov).astype(BF)

        chunk(0, True)

        def loop(ci, carry):
            chunk(pl.multiple_of(ci * r, 16), False)
            return carry

        lax.fori_loop(1, MIX_CHUNKS, loop, 0)

    col = lambda off: pl.BlockSpec((tp, c), lambda j: (0, off + j))
    return pl.pallas_call(
        body, name="sconv_fwd", grid=(nb,),
        in_specs=[col(2 * nb), col(3 * nb), col(4 * nb), pl.BlockSpec((8, c), lambda j: (0, j)),
                  pl.BlockSpec((1, c), lambda j: (0, j)), pl.BlockSpec((c, c), lambda j: (0, 0))],
        out_specs=col(0), out_shape=jax.ShapeDtypeStruct((tp, dl), BF),
        compiler_params=_params(("arbitrary",)),
    )(z, z, z, w3, g_out, gm)


def _sc_bwd(z, dmix, w3, g_out, gm, dl):
    tp = z.shape[0]
    nb = dl // LANE
    r = tp // MIX_CHUNKS
    c = LANE

    def body(b_ref, c_ref, v_ref, dm_ref, w3_ref, go_ref, gm_ref, db_ref, dc_ref, dv_ref, small_ref, dsc_buf):
        w3v = w3_ref[...]
        gov = go_ref[...]
        gmv = gm_ref[...]
        small_ref[...] = jnp.zeros_like(small_ref)
        dsc_buf[pl.ds(tp, 8), :] = jnp.zeros((8, c), F32)

        def chunk1(t0, first):
            cvwin = _window_dn(c_ref, t0, r, first) * _window_dn(v_ref, t0, r, first)
            sc = _sc_conv(cvwin, w3v, r)
            bv = b_ref[pl.ds(t0, r), :]
            so = bv * sc
            rs = lax.rsqrt(_group_mean(so * so, gmv) + EPS)
            xh = so * rs
            dm = dm_ref[pl.ds(t0, r), :]
            q = dm * gov
            dso = rs * (q - xh * _group_mean(q * xh, gmv))
            small_ref[3:4, :] += jnp.sum(dm * xh, axis=0, keepdims=True)
            db_ref[pl.ds(t0, r), :] = (dso * sc).astype(BF)
            dsc = dso * bv
            dsc_buf[pl.ds(t0, r), :] = dsc
            for k in range(3):
                small_ref[k:k + 1, :] += jnp.sum(dsc * _shift_dn(cvwin, 2 - k, r), axis=0, keepdims=True)

        chunk1(0, True)

        def loop1(ci, carry):
            chunk1(pl.multiple_of(ci * r, 16), False)
            return carry

        lax.fori_loop(1, MIX_CHUNKS, loop1, 0)

        def loop2(ci, carry):
            t0 = pl.multiple_of(ci * r, 16)
            dwin = dsc_buf[pl.ds(t0, r + 8), :]
            dcv = w3v[2:3] * _shift_up(dwin, 0, r) + w3v[1:2] * _shift_up(dwin, 1, r) + w3v[0:1] * _shift_up(dwin, 2, r)
            dc_ref[pl.ds(t0, r), :] = (dcv * v_ref[pl.ds(t0, r), :]).astype(BF)
            dv_ref[pl.ds(t0, r), :] = (dcv * c_ref[pl.ds(t0, r), :]).astype(BF)
            return carry

        lax.fori_loop(0, MIX_CHUNKS, loop2, 0)

    col = lambda off: pl.BlockSpec((tp, c), lambda j: (0, off + j))
    out = jax.ShapeDtypeStruct((tp, dl), BF)
    return pl.pallas_call(
        body, name="sconv_bwd", grid=(nb,),
        in_specs=[col(2 * nb), col(3 * nb), col(4 * nb), col(nb), pl.BlockSpec((8, c), lambda j: (0, j)),
                  pl.BlockSpec((1, c), lambda j: (0, j)), pl.BlockSpec((c, c), lambda j: (0, 0))],
        out_specs=[col(0), col(0), col(0), pl.BlockSpec((8, c), lambda j: (0, j))],
        out_shape=[out, out, out, jax.ShapeDtypeStruct((8, dl), F32)],
        scratch_shapes=[pltpu.VMEM((tp + 8, c), F32)],
        compiler_params=_params(("arbitrary",)),
    )(z, z, z, dmix, w3, g_out, gm)


def _cast_pad(w, rows_p, cols_p, chip, name):
    r, c = w.shape

    def body(chip_ref, w_ref, o_ref):
        if (rows_p, cols_p) != (r, c):
            o_ref[...] = jnp.zeros_like(o_ref)
        o_ref[0:r, 0:c] = w_ref[...].astype(BF)

    return pl.pallas_call(
        body, name=name, out_shape=jax.ShapeDtypeStruct((N_CHIP, rows_p, cols_p), BF),
        grid_spec=pltpu.PrefetchScalarGridSpec(
            num_scalar_prefetch=1, grid=(1,),
            in_specs=[pl.BlockSpec((r, c), lambda i, chip: (0, 0))],
            out_specs=pl.BlockSpec((None, rows_p, cols_p), lambda i, chip: (chip[0], 0, 0))),
        compiler_params=_params(("arbitrary",)),
    )(chip, w)


def _adamw_math(w, g, m, v):
    m2 = ADAM_B1 * m + (1.0 - ADAM_B1) * g
    v2 = ADAM_B2 * v + (1.0 - ADAM_B2) * (g * g)
    m_hat = m2 / (1.0 - ADAM_B1 ** ADAM_STEP)
    v_hat = v2 / (1.0 - ADAM_B2 ** ADAM_STEP)
    delta = -ADAM_LR * (m_hat / (jnp.sqrt(v_hat) + ADAM_EPS) + ADAM_WD * w)
    return delta, m2, v2


def _adamw(w, g, m, v, name, row_tiles, col_tiles, side=None):
    r, c = w.shape
    tr = r // row_tiles
    tc = c // col_tiles
    gc = g.shape[1] if col_tiles == 1 else tc

    def body(w_ref, g_ref, m_ref, v_ref, go_ref, d_ref, mo_ref, vo_ref):
        gv = g_ref[...][:, 0:tc]
        delta, m2, v2 = _adamw_math(w_ref[...], gv, m_ref[...], v_ref[...])
        go_ref[...] = gv
        d_ref[...] = delta
        mo_ref[...] = m2
        vo_ref[...] = v2

    spec = pl.BlockSpec((tr, tc), lambda i, j: (i, j))
    out = jax.ShapeDtypeStruct((r, c), F32)
    return _grid_call(body, name, (row_tiles, col_tiles), [spec, pl.BlockSpec((tr, gc), lambda i, j: (i, j)), spec, spec],
                      [spec] * 4, [out] * 4, (w, g, m, v), side)


def _adamw_small(w, g_top, g4, m, v):
    def body(w_ref, gt_ref, g_ref, m_ref, v_ref, go_ref, d_ref, mo_ref, vo_ref):
        g = jnp.concatenate([gt_ref[...], (g_ref[0] + g_ref[1]) + (g_ref[2] + g_ref[3])], axis=0)
        delta, m2, v2 = _adamw_math(w_ref[...], g, m_ref[...], v_ref[...])
        go_ref[...] = g
        d_ref[...] = delta
        mo_ref[...] = m2
        vo_ref[...] = v2

    out = jax.ShapeDtypeStruct(w.shape, F32)
    spec = pl.BlockSpec(w.shape, lambda: (0, 0))
    return pl.pallas_call(
        body, name="adamw_small",
        in_specs=[spec, pl.BlockSpec(g_top.shape, lambda: (0, 0)), pl.BlockSpec(g4.shape, lambda: (0, 0, 0)), spec, spec],
        out_specs=[spec] * 4, out_shape=[out] * 4, compiler_params=_params())(w, g_top, g4, m, v)


def _place():
    x, y, c = lax.axis_index("x"), lax.axis_index("y"), lax.axis_index("c")
    chips = [(1 - x, y), (x, 1 - y), (1 - x, 1 - y)]
    return x, y, c, chips


ANY = pl.BlockSpec(memory_space=pl.ANY)


def _gather_side(bufs):
    n = len(bufs)

    def copies(outs, sems):
        s_ici, r_ici, s_d2d, r_d2d = sems
        x, y, c, chips = _place()
        me = 2 * x + y

        def rows(w, chip, core):
            half = bufs[w].shape[1] // 2
            return outs[w].at[chip, pl.ds(core * half, half)]

        def ici(w, j, chip):
            px, py = chips[j]
            return pltpu.make_async_remote_copy(
                src_ref=rows(w, chip, c), dst_ref=rows(w, chip, c),
                send_sem=s_ici.at[w, j], recv_sem=r_ici.at[w, j], device_id=(px, py, c), device_id_type=MESH)

        def d2d(w, j, core):
            px, py = chips[j]
            return pltpu.make_async_remote_copy(
                src_ref=rows(w, 2 * px + py, core), dst_ref=rows(w, 2 * px + py, core),
                send_sem=s_d2d.at[w, j], recv_sem=r_d2d.at[w, j], device_id=(x, y, 1 - c), device_id_type=MESH)

        pairs = [(w, j) for w in range(n) for j in range(3)]
        return me, c, chips, ici, d2d, pairs

    def start(ins, outs, sems):
        me, c, chips, ici, d2d, pairs = copies(outs, sems)
        for w, j in pairs:
            ici(w, j, me).start()

    def finish(ins, outs, sems):
        me, c, chips, ici, d2d, pairs = copies(outs, sems)
        for w, j in pairs:
            ici(w, j, 2 * chips[j][0] + chips[j][1]).wait_recv()
            d2d(w, j, c).start()
        for w, j in pairs:
            d2d(w, j, 1 - c).wait_recv()
        for w, j in pairs:
            ici(w, j, me).wait_send()
            d2d(w, j, c).wait_send()

    dma = pltpu.SemaphoreType.DMA((n, 3))
    return _Side(list(bufs), [jax.ShapeDtypeStruct(b.shape, b.dtype) for b in bufs], {w: w for w in range(n)},
                 [dma, dma, dma, dma], start, finish)


def _run_side(side, name):
    sin, sout = len(side.ins), len(side.outs)

    def body(*refs):
        ins, outs, sems = refs[:sin], refs[sin:sin + sout], refs[sin + sout:]
        side.start(ins, outs, sems)
        side.finish(ins, outs, sems)

    return pl.pallas_call(
        body, name=name, out_shape=list(side.outs), in_specs=[ANY] * sin, out_specs=[ANY] * sout,
        scratch_shapes=list(side.sems), input_output_aliases=dict(side.alias))(*side.ins)


def _pair_exchange_side(grads):
    n = len(grads)

    def copies(ins, outs, sems):
        ssem, rsem = sems
        x, y, c, _ = _place()
        cps = []
        for w in range(n):
            half = grads[w].shape[1] // 2
            cps.append(pltpu.make_async_remote_copy(
                src_ref=ins[w].at[:, pl.ds((1 - c) * half, half)], dst_ref=outs[w],
                send_sem=ssem.at[w], recv_sem=rsem.at[w], device_id=(x, y, 1 - c), device_id_type=MESH))
        return cps

    def start(ins, outs, sems):
        for cp in copies(ins, outs, sems):
            cp.start()

    def finish(ins, outs, sems):
        for cp in copies(ins, outs, sems):
            cp.wait()

    dma = pltpu.SemaphoreType.DMA((n,))
    return _Side(list(grads), [jax.ShapeDtypeStruct((N_CHIP, g.shape[1] // 2, g.shape[2]), BF) for g in grads], {},
                 [dma, dma], start, finish)


def _sibling_copy_side(buf):
    def copy(ins, outs, sems):
        x, y, c, _ = _place()
        return pltpu.make_async_remote_copy(src_ref=ins[0], dst_ref=outs[0], send_sem=sems[0], recv_sem=sems[1],
                                            device_id=(x, y, 1 - c), device_id_type=MESH)

    return _Side([buf], [jax.ShapeDtypeStruct(buf.shape, buf.dtype)], {}, [pltpu.SemaphoreType.DMA, pltpu.SemaphoreType.DMA],
                 lambda i, o, s: copy(i, o, s).start(), lambda i, o, s: copy(i, o, s).wait())


def _slot_exchange_side(buf4):
    def copies(outs, sems, sending):
        ssem, rsem = sems
        x, y, c, chips = _place()
        me = 2 * x + y
        return [pltpu.make_async_remote_copy(
            src_ref=outs[0].at[me if sending else 2 * px + py], dst_ref=outs[0].at[me if sending else 2 * px + py],
            send_sem=ssem.at[j], recv_sem=rsem.at[j], device_id=(px, py, c), device_id_type=MESH)
            for j, (px, py) in enumerate(chips)]

    def start(ins, outs, sems):
        for cp in copies(outs, sems, True):
            cp.start()

    def finish(ins, outs, sems):
        for cp in copies(outs, sems, False):
            cp.wait_recv()
        for cp in copies(outs, sems, True):
            cp.wait_send()

    dma = pltpu.SemaphoreType.DMA((3,))
    return _Side([buf4], [jax.ShapeDtypeStruct(buf4.shape, buf4.dtype)], {0: 0}, [dma, dma], start, finish)


def _pair_sum(g, sib, core, name):
    _, r, cdim = g.shape
    half = r // 2

    def body(core_ref, g_ref, s_ref, o_ref):
        o_ref[...] = (g_ref[...].astype(F32) + s_ref[...].astype(F32)).astype(BF)

    return pl.pallas_call(
        body, name=name,
        grid_spec=pltpu.PrefetchScalarGridSpec(
            num_scalar_prefetch=1, grid=(N_CHIP,),
            in_specs=[pl.BlockSpec((None, half, cdim), lambda k, core: (k, core[0], 0)),
                      pl.BlockSpec((None, half, cdim), lambda k, core: (k, 0, 0))],
            out_specs=pl.BlockSpec((None, half, cdim), lambda k, core: (k, 0, 0))),
        out_shape=jax.ShapeDtypeStruct((N_CHIP, half, cdim), BF),
        compiler_params=_params(("arbitrary",)),
    )(core, g, sib)


def _chip_exchange_side(psums):
    n = len(psums)

    def copies(ins, outs, sems):
        ssem, rsem = sems
        x, y, c, chips = _place()
        return [pltpu.make_async_remote_copy(
            src_ref=ins[w].at[2 * px + py], dst_ref=outs[w].at[j],
            send_sem=ssem.at[w, j], recv_sem=rsem.at[w, j], device_id=(px, py, c), device_id_type=MESH)
            for w in range(n) for j, (px, py) in enumerate(chips)]

    def start(ins, outs, sems):
        for cp in copies(ins, outs, sems):
            cp.start()

    def finish(ins, outs, sems):
        for cp in copies(ins, outs, sems):
            cp.wait()

    dma = pltpu.SemaphoreType.DMA((n, 3))
    return _Side(list(psums), [jax.ShapeDtypeStruct((3,) + p.shape[1:], BF) for p in psums], {}, [dma, dma],
                 start, finish)


def _final_sum(g, sib, recv, sel, name):
    _, r, cdim = g.shape
    half = r // 2
    nt = 4
    th = half // nt

    def body(sel_ref, g_ref, s_ref, r_ref, o_ref):
        acc = g_ref[...].astype(F32) + s_ref[...].astype(F32)
        for j in range(3):
            acc = acc + r_ref[j].astype(F32)
        o_ref[...] = acc

    return pl.pallas_call(
        body, name=name,
        grid_spec=pltpu.PrefetchScalarGridSpec(
            num_scalar_prefetch=1, grid=(nt,),
            in_specs=[pl.BlockSpec((None, th, cdim), lambda i, sel: (sel[0], sel[1] * nt + i, 0)),
                      pl.BlockSpec((None, th, cdim), lambda i, sel: (sel[0], i, 0)),
                      pl.BlockSpec((3, th, cdim), lambda i, sel: (0, i, 0))],
            out_specs=pl.BlockSpec((th, cdim), lambda i, sel: (sel[1] * nt + i, 0))),
        out_shape=jax.ShapeDtypeStruct((r, cdim), F32),
        compiler_params=_params(("arbitrary",)),
    )(sel, g, sib, recv)


def _join_side(bufs):
    n = len(bufs)

    def copies(outs, sems, core_of):
        ssem, rsem = sems
        x, y, c, _ = _place()
        cps = []
        for w in range(n):
            half = bufs[w].shape[0] // 2
            rows = outs[w].at[pl.ds(core_of(c) * half, half)]
            cps.append(pltpu.make_async_remote_copy(
                src_ref=rows, dst_ref=rows, send_sem=ssem.at[w], recv_sem=rsem.at[w],
                device_id=(x, y, 1 - c), device_id_type=MESH))
        return cps

    def start(ins, outs, sems):
        for cp in copies(outs, sems, lambda c: c):
            cp.start()

    def finish(ins, outs, sems):
        for cp in copies(outs, sems, lambda c: 1 - c):
            cp.wait_recv()
        for cp in copies(outs, sems, lambda c: c):
            cp.wait_send()

    dma = pltpu.SemaphoreType.DMA((n,))
    return _Side(list(bufs), [jax.ShapeDtypeStruct(b.shape, F32) for b in bufs], {w: w for w in range(n)}, [dma, dma],
                 start, finish)


def _small_pair_sum(buf, sib, chip):
    rows, d = buf.shape

    def body(chip_ref, a_ref, b_ref, o_ref):
        o_ref[...] = a_ref[...] + b_ref[...]

    return pl.pallas_call(
        body, name="small_pair_sum", out_shape=jax.ShapeDtypeStruct((N_CHIP, rows, d), F32),
        grid_spec=pltpu.PrefetchScalarGridSpec(
            num_scalar_prefetch=1, grid=(1,),
            in_specs=[pl.BlockSpec((rows, d), lambda i, chip: (0, 0))] * 2,
            out_specs=pl.BlockSpec((None, rows, d), lambda i, chip: (chip[0], 0, 0))),
        compiler_params=_params(("arbitrary",)),
    )(chip, buf, sib)


def _small_all_reduce(buf, name):
    rows, d = buf.shape

    def body(in_ref, out_ref, sib, all4, ssem, rsem, psem, qsem):
        x, y, c, chips = _place()
        me = 2 * x + y
        to_sib = pltpu.make_async_remote_copy(src_ref=in_ref, dst_ref=sib, send_sem=ssem, recv_sem=rsem,
                                              device_id=(x, y, 1 - c), device_id_type=MESH)
        to_sib.start()
        to_sib.wait()
        all4[me] = in_ref[...] + sib[...]
        cps = [pltpu.make_async_remote_copy(src_ref=all4.at[me], dst_ref=all4.at[me], send_sem=psem.at[j],
                                            recv_sem=qsem.at[j], device_id=(px, py, c), device_id_type=MESH)
               for j, (px, py) in enumerate(chips)]
        for cp in cps:
            cp.start()
        for j, (px, py) in enumerate(chips):
            chip = 2 * px + py
            pltpu.make_async_remote_copy(src_ref=all4.at[chip], dst_ref=all4.at[chip], send_sem=psem.at[j],
                                         recv_sem=qsem.at[j], device_id=(px, py, c), device_id_type=MESH).wait_recv()
        for cp in cps:
            cp.wait_send()
        out_ref[...] = (all4[0] + all4[1]) + (all4[2] + all4[3])

    vm = pl.BlockSpec(memory_space=pltpu.VMEM)
    return pl.pallas_call(
        body, name=name, out_shape=jax.ShapeDtypeStruct((rows, d), F32),
        in_specs=[vm], out_specs=vm,
        scratch_shapes=[pltpu.VMEM((rows, d), F32), pltpu.VMEM((N_CHIP, rows, d), F32),
                        pltpu.SemaphoreType.DMA, pltpu.SemaphoreType.DMA,
                        pltpu.SemaphoreType.DMA((3,)), pltpu.SemaphoreType.DMA((3,))],
        compiler_params=_params(),
    )(buf)


def _pair_blocks(w):
    w4 = w.reshape(N_HEADS // 2, 2, HEAD, HEAD)
    eye = jnp.eye(2, dtype=w.dtype)
    return jnp.einsum("pirc,ij->pirjc", w4, eye).reshape(N_HEADS // 2, LANE, LANE)


def _unpair_blocks(w2):
    w5 = w2.reshape(N_HEADS // 2, 2, HEAD, 2, HEAD)
    return jnp.stack([w5[:, 0, :, 0, :], w5[:, 1, :, 1, :]], axis=1).reshape(N_HEADS, HEAD, HEAD)


def kernel(x, meta_tokens, ffn1_pre_g, ffn1_w_gate, ffn1_w_up, ffn1_w_down, ffn1_post_g, mix_pre_g, w_in, lru_conv_w, lru_conv_b, lru_w_a, lru_b_a, lru_w_x, lru_b_x, lru_lambda, sconv_w, lru_out_g, sconv_out_g, w_out, mix_post_g, ffn2_pre_g, ffn2_w_gate, ffn2_w_up, ffn2_w_down, ffn2_post_g, loss_target, m_meta_tokens, m_ffn1_pre_g, m_ffn1_w_gate, m_ffn1_w_up, m_ffn1_w_down, m_ffn1_post_g, m_mix_pre_g, m_w_in, m_lru_conv_w, m_lru_conv_b, m_lru_w_a, m_lru_b_a, m_lru_w_x, m_lru_b_x, m_lru_lambda, m_sconv_w, m_lru_out_g, m_sconv_out_g, m_w_out, m_mix_post_g, m_ffn2_pre_g, m_ffn2_w_gate, m_ffn2_w_up, m_ffn2_w_down, m_ffn2_post_g, v_meta_tokens, v_ffn1_pre_g, v_ffn1_w_gate, v_ffn1_w_up, v_ffn1_w_down, v_ffn1_post_g, v_mix_pre_g, v_w_in, v_lru_conv_w, v_lru_conv_b, v_lru_w_a, v_lru_b_a, v_lru_w_x, v_lru_b_x, v_lru_lambda, v_sconv_w, v_lru_out_g, v_sconv_out_g, v_w_out, v_mix_post_g, v_ffn2_pre_g, v_ffn2_w_gate, v_ffn2_w_up, v_ffn2_w_down, v_ffn2_post_g):
    seq, d = x.shape[1], x.shape[2]
    t_real = N_META + seq
    tp = _round_up(t_real, ROW_ALIGN)
    f4 = ffn1_w_gate.shape[2]
    f4p = _round_up(f4, LANE)
    dl = lru_conv_b.shape[1]
    cin = w_in.shape[2]
    xi, yi, ci = lax.axis_index("x"), lax.axis_index("y"), lax.axis_index("c")
    chip = 2 * xi + yi
    zero = jnp.zeros((), jnp.int32)

    transposed = ("ffn1_w_gate", "ffn1_w_up", "ffn2_w_gate", "ffn2_w_up")

    def view(k, a):
        return a[0].T if k in transposed else a[0]

    def unview(k, a):
        return (a.T if k in transposed else a)[None]

    big = {
        "ffn1_w_gate": (view("ffn1_w_gate", ffn1_w_gate), f4p, d), "ffn1_w_up": (view("ffn1_w_up", ffn1_w_up), f4p, d),
        "ffn1_w_down": (ffn1_w_down[0], f4p, d), "w_in": (w_in[0], d, cin), "w_out": (w_out[0], w_out.shape[1], d),
        "ffn2_w_gate": (view("ffn2_w_gate", ffn2_w_gate), f4p, d), "ffn2_w_up": (view("ffn2_w_up", ffn2_w_up), f4p, d),
        "ffn2_w_down": (ffn2_w_down[0], f4p, d),
    }
    names = list(big)
    chip1 = jnp.reshape(chip, (1,)).astype(jnp.int32)
    shard = {k: _cast_pad(big[k][0], big[k][1], big[k][2], chip1, "cast_" + k) for k in names}
    full = dict(zip(("ffn1_w_gate", "ffn1_w_up"),
                    _run_side(_gather_side([shard["ffn1_w_gate"], shard["ffn1_w_up"]]), "gather_ffn1_in")))

    gm = jnp.kron(jnp.eye(2, dtype=F32), jnp.full((HEAD, HEAD), 1.0 / HEAD, F32)).astype(BF)
    wa2 = _pair_blocks(lru_w_a[0])
    wx2 = _pair_blocks(lru_w_x[0])

    dlq = dl // N_CHIP
    dq = d // N_CHIP
    R_GAIN, R_LOSS, R_META, R_LRU, R_SC, R_WA = 0, 6, 8, 24, 40, 48
    n_wrows = (N_HEADS // 2) * LANE * LANE // d
    R_WX = R_WA + n_wrows
    R_END = R_WX + n_wrows

    def pack_top(gains, meta, loss=None):
        lossrow = jnp.zeros((2, d), F32)
        if loss is not None:
            lossrow = lossrow.at[0, 0].set(loss)
        return jnp.concatenate([jnp.concatenate(gains, axis=0), lossrow, meta], axis=0)

    def pack_rest(lru16, sc8, wa_, wx_):
        return jnp.concatenate([jnp.concatenate([lru16, jnp.zeros((16, d - dl), F32)], axis=1),
                                jnp.concatenate([sc8, jnp.zeros((8, d - dl), F32)], axis=1),
                                wa_.reshape(n_wrows, d), wx_.reshape(n_wrows, d)], axis=0)

    def pack(gains, meta, lru16, sc8, wa_, wx_):
        return jnp.concatenate([pack_top(gains, meta), pack_rest(lru16, sc8, wa_, wx_)], axis=0)

    def place_cols(blk, width, total):
        return lax.dynamic_update_slice(jnp.zeros((blk.shape[0], total), F32), blk, (zero, chip * width))

    def pack_params(meta_, g1pre, g1post, gmpre, gmpost, g2pre, g2post, cw, cbias, wa_, ba_, wx_, bx_, lam_, sw, lgo, sgo):
        lru16 = jnp.concatenate([place_cols(cw[0], dlq, dl), cbias, ba_, bx_, lam_, lgo, jnp.zeros((7, dl), F32)], axis=0)
        sc8 = jnp.concatenate([place_cols(sw[0], dlq, dl), sgo, jnp.zeros((4, dl), F32)], axis=0)
        return pack([g1pre, g1post, gmpre, gmpost, g2pre, g2post], place_cols(meta_, dq, d), lru16, sc8,
                    _pair_blocks(wa_[0]), _pair_blocks(wx_[0]))

    p_w = pack_params(meta_tokens, ffn1_pre_g, ffn1_post_g, mix_pre_g, mix_post_g, ffn2_pre_g, ffn2_post_g, lru_conv_w,
                      lru_conv_b, lru_w_a, lru_b_a, lru_w_x, lru_b_x, lru_lambda, sconv_w, lru_out_g, sconv_out_g)
    p_m = pack_params(m_meta_tokens, m_ffn1_pre_g, m_ffn1_post_g, m_mix_pre_g, m_mix_post_g, m_ffn2_pre_g, m_ffn2_post_g,
                      m_lru_conv_w, m_lru_conv_b, m_lru_w_a, m_lru_b_a, m_lru_w_x, m_lru_b_x, m_lru_lambda, m_sconv_w,
                      m_lru_out_g, m_sconv_out_g)
    p_v = pack_params(v_meta_tokens, v_ffn1_pre_g, v_ffn1_post_g, v_mix_pre_g, v_mix_post_g, v_ffn2_pre_g, v_ffn2_post_g,
                      v_lru_conv_w, v_lru_conv_b, v_lru_w_a, v_lru_b_a, v_lru_w_x, v_lru_b_x, v_lru_lambda, v_sconv_w,
                      v_lru_out_g, v_sconv_out_g)

    gathered = _small_all_reduce(jnp.where(ci == 0, p_w, 0.0)[R_META:R_WA], "small_weight_gather")
    meta_full = gathered[0:N_META]
    w4_full = gathered[R_LRU - R_META:R_LRU - R_META + 4, 0:dl]
    w3_full = gathered[R_SC - R_META:R_SC - R_META + 3, 0:dl]
    w4p = jnp.concatenate([w4_full, jnp.zeros((4, dl), F32)], axis=0)
    w3p = jnp.concatenate([w3_full, jnp.zeros((5, dl), F32)], axis=0)

    h0 = jnp.concatenate([meta_full, x[0], jnp.zeros((tp - t_real, d), F32)], axis=0)
    tgt = jnp.concatenate([jnp.zeros((N_META, d), F32), loss_target[0], jnp.zeros((tp - t_real, d), F32)], axis=0)

    n1 = _norm0(h0, ffn1_pre_g)
    (a1, b1, s1), got = _ffn_up(n1, full["ffn1_w_gate"], full["ffn1_w_up"], "ffn1_up",
                                _gather_side([shard["ffn1_w_down"], shard["w_in"]]))
    full["ffn1_w_down"], full["w_in"] = got
    f1, got = _row_matmul([(s1, full["ffn1_w_down"])], "ffn1_down", False, d, _gather_side([shard["w_out"]]))
    full["w_out"] = got[0]
    h1, u = _post_fwd(f1, h0, ffn1_post_g, mix_pre_g, 0.5, "ffn1_post")
    z, got = _col_matmul(u, full["w_in"], "in_proj", False, F32, _gather_side([shard["ffn2_w_gate"]]))
    full["ffn2_w_gate"] = got[0]
    (m_lru, hs), got = _lru_fwd(z, w4p, lru_conv_b, wa2.astype(BF), lru_b_a, wx2.astype(BF), lru_b_x, lru_lambda,
                                lru_out_g, gm, _gather_side([shard["ffn2_w_up"]]))
    full["ffn2_w_up"] = got[0]
    m_sc = _sc_fwd(z, w3p, sconv_out_g, gm, dl)
    mixed = jnp.concatenate([m_lru, m_sc], axis=1)
    p, _ = _row_matmul([(mixed, full["w_out"])], "out_proj", False, d)
    h2, n2 = _post_fwd(p, h1, mix_post_g, ffn2_pre_g, 1.0, "mix_post")
    (a2, b2, s2), got = _ffn_up(n2, full["ffn2_w_gate"], full["ffn2_w_up"], "ffn2_up",
                                _gather_side([shard["ffn2_w_down"]]))
    full["ffn2_w_down"] = got[0]
    f2, _ = _row_matmul([(s2, full["ffn2_w_down"])], "ffn2_down", False, d)
    dh3, df2, dg_ffn2_post, loss_part = _loss_bwd(f2, h2, tgt, ffn2_post_g, t_real)

    core = jnp.reshape(ci, (1,)).astype(jnp.int32)
    sel = jnp.stack([chip, ci]).astype(jnp.int32)
    red = {}

    def pair_side(k):
        return _pair_exchange_side([red[k][0]])

    def chip_side(k):
        return _chip_exchange_side([_pair_sum(red[k][0], red[k][1], core, "pair_sum_" + k)])

    (da2, db2), _ = _ffn_bwd_act(df2, full["ffn2_w_down"], a2, b2, "ffn2_bwd_act")
    g, _ = _wgrad_call(s2, df2, "ffn2_down_wgrad", tile_y=WGRAD_TILE_Y)
    red["ffn2_w_down"] = [g, None, None]
    g, got = _wgrad_call(da2, n2, "ffn2_gate_wgrad", tile_y=WGRAD_TILE_Y, side=pair_side("ffn2_w_down"))
    red["ffn2_w_down"][1] = got[0]
    red["ffn2_w_gate"] = [g, None, None]
    g, got = _wgrad_call(db2, n2, "ffn2_up_wgrad", tile_y=WGRAD_TILE_Y,
                         side=_merge_sides([pair_side("ffn2_w_gate"), chip_side("ffn2_w_down")]))
    red["ffn2_w_gate"][1], red["ffn2_w_down"][2] = got
    red["ffn2_w_up"] = [g, None, None]
    red["ffn2_w_up"][1] = _run_side(pair_side("ffn2_w_up"), "pair_exchange_ffn2_w_up")[0]
    dn2, got = _row_matmul([(da2, full["ffn2_w_gate"]), (db2, full["ffn2_w_up"])], "ffn2_bwd_up", False, d,
                           _merge_sides([chip_side("ffn2_w_gate"), chip_side("ffn2_w_up")]), tiles=MM_TILES)
    red["ffn2_w_gate"][2], red["ffn2_w_up"][2] = got
    dh2, dp, dg_ffn2_pre, dg_mix_post = _pre_bwd(dn2, h2, dh3, ffn2_pre_g, "ffn2_pre_bwd", (p, mix_post_g, 1.0))
    dmixed, _ = _col_matmul(dp, full["w_out"], "out_proj_bwd", True, F32)
    g, _ = _wgrad_call(mixed, dp, "w_out_wgrad", x_width=mixed.shape[1] // N_CHIP, tile_y=WGRAD_TILE_Y)
    red["w_out"] = [g, None, None]
    dzy, dzx, lru_small, dwa2, dwx2 = _lru_bwd(z, hs, dmixed, w4p, lru_conv_b, wa2.astype(BF), lru_b_a, wx2.astype(BF),
                                               lru_b_x, lru_lambda, lru_out_g, gm)
    dzb, dzc, dzv, sc_small = _sc_bwd(z, dmixed, w3p, sconv_out_g, gm, dl)
    dz = jnp.concatenate([dzy, dzx, dzb, dzc, dzv], axis=1)
    p_rest = pack_rest(lru_small, sc_small, dwa2, dwx2)
    g, got = _wgrad_call(u, dz, "w_in_wgrad", y_width=cin, tile_x=WGRAD_TILE_X,
                         side=_merge_sides([pair_side("w_out"), _sibling_copy_side(p_rest)]))
    red["w_out"][1] = got[0]
    p_rest4 = _small_pair_sum(p_rest, got[1], chip1)
    red["w_in"] = [g, None, None]
    du, got = _row_matmul([(dz, full["w_in"])], "in_proj_bwd", True, d,
                          _merge_sides([pair_side("w_in"), chip_side("w_out")]))
    red["w_in"][1], red["w_out"][2] = got
    dh1, df1, dg_mix_pre, dg_ffn1_post = _pre_bwd(du, h1, dh2, mix_pre_g, "mix_pre_bwd", (f1, ffn1_post_g, 0.5))
    (da1, db1), got = _ffn_bwd_act(df1, full["ffn1_w_down"], a1, b1, "ffn1_bwd_act",
                                   _merge_sides([chip_side("w_in"), _slot_exchange_side(p_rest4)]))
    red["w_in"][2], p_rest4 = got
    early = ["ffn2_w_down", "ffn2_w_gate", "ffn2_w_up", "w_out", "w_in"]
    late = ["ffn1_w_down", "ffn1_w_gate", "ffn1_w_up"]
    g, got = _wgrad_call(s1, df1, "ffn1_down_wgrad", tile_y=WGRAD_TILE_Y,
                         side=_join_side([_final_sum(*red[k], sel, "final_sum_" + k) for k in early]))
    gfull = dict(zip(early, got))
    red["ffn1_w_down"] = [g, None, None]
    g, got = _wgrad_call(da1, n1, "ffn1_gate_wgrad", tile_y=WGRAD_TILE_Y, side=pair_side("ffn1_w_down"))
    red["ffn1_w_down"][1] = got[0]
    red["ffn1_w_gate"] = [g, None, None]
    g, got = _wgrad_call(db1, n1, "ffn1_up_wgrad", tile_y=WGRAD_TILE_Y,
                         side=_merge_sides([pair_side("ffn1_w_gate"), chip_side("ffn1_w_down")]))
    red["ffn1_w_gate"][1], red["ffn1_w_down"][2] = got
    red["ffn1_w_up"] = [g, None, None]
    red["ffn1_w_up"][1] = _run_side(pair_side("ffn1_w_up"), "pair_exchange_ffn1_w_up")[0]
    dn1, got = _row_matmul([(da1, full["ffn1_w_gate"]), (db1, full["ffn1_w_up"])], "ffn1_bwd_up", False, d,
                           _merge_sides([chip_side("ffn1_w_gate"), chip_side("ffn1_w_up")]), tiles=MM_TILES)
    red["ffn1_w_gate"][2], red["ffn1_w_up"][2] = got
    (dh0, dg_ffn1_pre), got = _pre_bwd(dn1, h0, dh1, ffn1_pre_g, "ffn1_pre_bwd",
                                       side=_join_side([_final_sum(*red[k], sel, "final_sum_" + k) for k in late]))
    gfull.update(zip(late, got))

    grad_x = dh0[N_META:t_real][None]

    w_big = {"ffn1_w_gate": ffn1_w_gate, "ffn1_w_up": ffn1_w_up, "ffn1_w_down": ffn1_w_down, "w_in": w_in, "w_out": w_out,
             "ffn2_w_gate": ffn2_w_gate, "ffn2_w_up": ffn2_w_up, "ffn2_w_down": ffn2_w_down}
    m_big = {"ffn1_w_gate": m_ffn1_w_gate, "ffn1_w_up": m_ffn1_w_up, "ffn1_w_down": m_ffn1_w_down, "w_in": m_w_in,
             "w_out": m_w_out, "ffn2_w_gate": m_ffn2_w_gate, "ffn2_w_up": m_ffn2_w_up, "ffn2_w_down": m_ffn2_w_down}
    v_big = {"ffn1_w_gate": v_ffn1_w_gate, "ffn1_w_up": v_ffn1_w_up, "ffn1_w_down": v_ffn1_w_down, "w_in": v_w_in,
             "w_out": v_w_out, "ffn2_w_gate": v_ffn2_w_gate, "ffn2_w_up": v_ffn2_w_up, "ffn2_w_down": v_ffn2_w_down}
    b_grad, b_delta, b_newm, b_newv = {}, {}, {}, {}

    def big_adamw(k, side=None):
        wv, mv, vv = view(k, w_big[k]), view(k, m_big[k]), view(k, v_big[k])
        wide_rows = wv.shape[0] % 64 == 0
        (g_, d_, m_, v_), got = _adamw(wv, gfull[k], mv, vv, "adamw_" + k, 8 if wide_rows else 4, 1 if wide_rows else 2,
                                       side)
        b_grad[k], b_delta[k], b_newm[k], b_newv[k] = unview(k, g_), unview(k, d_), unview(k, m_), unview(k, v_)
        return got

    p_top = _small_all_reduce(
        pack_top([dg_ffn1_pre, dg_ffn1_post, dg_mix_pre, dg_mix_post, dg_ffn2_pre, dg_ffn2_post], dh0[0:N_META],
                 loss=loss_part[0, 0]), "small_grad_all_reduce")
    p_g, p_delta, p_newm, p_newv = _adamw_small(p_w, p_top, p_rest4, p_m, p_v)
    loss = p_g[R_LOSS, 0]
    for k in names:
        big_adamw(k)

    def unpack(buf):
        out = {}
        for i, k in enumerate(["ffn1_pre_g", "ffn1_post_g", "mix_pre_g", "mix_post_g", "ffn2_pre_g", "ffn2_post_g"]):
            out[k] = buf[R_GAIN + i:R_GAIN + i + 1]
        out["meta_tokens"] = lax.dynamic_slice(buf[R_META:R_META + N_META], (zero, chip * dq), (N_META, dq))
        lru = buf[R_LRU:R_LRU + 16, 0:dl]
        out["lru_conv_w"] = lax.dynamic_slice(lru[0:4], (zero, chip * dlq), (4, dlq))[None]
        out["lru_conv_b"] = lru[4:5]
        out["lru_b_a"] = lru[5:6]
        out["lru_b_x"] = lru[6:7]
        out["lru_lambda"] = lru[7:8]
        out["lru_out_g"] = lru[8:9]
        sc = buf[R_SC:R_SC + 8, 0:dl]
        out["sconv_w"] = lax.dynamic_slice(sc[0:3], (zero, chip * dlq), (3, dlq))[None]
        out["sconv_out_g"] = sc[3:4]
        out["lru_w_a"] = _unpair_blocks(buf[R_WA:R_WX].reshape(N_HEADS // 2, LANE, LANE))[None]
        out["lru_w_x"] = _unpair_blocks(buf[R_WX:R_END].reshape(N_HEADS // 2, LANE, LANE))[None]
        return out

    s_grad, s_delta, s_newm, s_newv = unpack(p_g), unpack(p_delta), unpack(p_newm), unpack(p_newv)

    order = ["meta_tokens", "ffn1_pre_g", "ffn1_w_gate", "ffn1_w_up", "ffn1_w_down", "ffn1_post_g", "mix_pre_g", "w_in",
             "lru_conv_w", "lru_conv_b", "lru_w_a", "lru_b_a", "lru_w_x", "lru_b_x", "lru_lambda", "sconv_w", "lru_out_g",
             "sconv_out_g", "w_out", "mix_post_g", "ffn2_pre_g", "ffn2_w_gate", "ffn2_w_up", "ffn2_w_down", "ffn2_post_g"]

    def pick(small, bigd):
        return [bigd[k] if k in bigd else small[k] for k in order]

    return (loss, grad_x, *pick(s_grad, b_grad), *pick(s_delta, b_delta), *pick(s_newm, b_newm), *pick(s_newv, b_newv))
```

```python
import functools
import math

import jax
import jax.numpy as jnp
from jax import lax
from jax.experimental import pallas as pl
from jax.experimental.pallas import tpu as pltpu

F32 = jnp.float32
BF = jnp.bfloat16
MESH = pl.DeviceIdType.MESH

EPS = 1e-6
N_META = 16
N_HEADS = 16
HEAD = 64
LRU_C = 8.0
LANE = 128
MXU_COLS = 256
N_CHIP = 4
ROW_ALIGN = 384
MM_TILES = 8
MM_TILES_BIG = 4
EW_TILES = 12
MIX_CHUNKS = 24
WGRAD_TILE_X = 256
WGRAD_TILE_Y = 512
VMEM_LIMIT = 56 << 20

ADAM_LR = 0.001
ADAM_B1 = 0.9
ADAM_B2 = 0.999
ADAM_EPS = 1e-08
ADAM_WD = 0.01
ADAM_STEP = 10


def _round_up(a, b):
    return (a + b - 1) // b * b


def _params(sem=None):
    if sem is None:
        return pltpu.CompilerParams(vmem_limit_bytes=VMEM_LIMIT)
    return pltpu.CompilerParams(dimension_semantics=sem, vmem_limit_bytes=VMEM_LIMIT)


def _sigmoid(x):
    return 0.5 * jnp.tanh(0.5 * x) + 0.5


def _dot(a, b):
    return jnp.dot(a, b, preferred_element_type=F32)


def _dot_nt(a, b):
    return lax.dot_general(a, b, (((1,), (1,)), ((), ())), preferred_element_type=F32)


def _dot_tn(a, b):
    return lax.dot_general(a, b, (((0,), (0,)), ((), ())), preferred_element_type=F32)


def _rms(x, g):
    r = lax.rsqrt(jnp.mean(x * x, axis=-1, keepdims=True) + EPS)
    return x * r * g


def _rms_bwd(x, g, dy):
    r = lax.rsqrt(jnp.mean(x * x, axis=-1, keepdims=True) + EPS)
    xh = x * r
    q = dy * g
    dx = r * (q - xh * jnp.mean(q * xh, axis=-1, keepdims=True))
    return dx, dy * xh


class _Side:
    def __init__(self, ins, outs, alias, sems, start, finish):
        self.ins, self.outs, self.alias, self.sems, self.start, self.finish = ins, outs, alias, sems, start, finish


def _merge_sides(sides):
    sides = [s for s in sides if s is not None]
    if len(sides) <= 1:
        return sides[0] if sides else None
    ins, outs, sems, alias, spans = [], [], [], {}, []
    for s in sides:
        for i, o in s.alias.items():
            alias[len(ins) + i] = len(outs) + o
        spans.append((len(ins), len(ins) + len(s.ins), len(outs), len(outs) + len(s.outs), len(sems),
                      len(sems) + len(s.sems)))
        ins += list(s.ins)
        outs += list(s.outs)
        sems += list(s.sems)

    def run(which):
        def go(in_refs, out_refs, sem_refs):
            for s, (a, b, c, d, e, f) in zip(sides, spans):
                getattr(s, which)(in_refs[a:b], out_refs[c:d], sem_refs[e:f])
        return go

    return _Side(ins, outs, alias, sems, run("start"), run("finish"))


def _grid_call(body, name, grid, in_specs, out_specs, out_shape, args, side=None):
    sem = ("arbitrary",) * len(grid)
    if side is None:
        res = pl.pallas_call(body, name=name, grid=grid, in_specs=in_specs, out_specs=out_specs, out_shape=out_shape,
                             compiler_params=_params(sem))(*args)
        return res, []
    nin, nout, sin, sout = len(in_specs), len(out_specs), len(side.ins), len(side.outs)

    def full(*refs):
        base_in, side_in = refs[:nin], refs[nin:nin + sin]
        base_out = refs[nin + sin:nin + sin + nout]
        side_out = refs[nin + sin + nout:nin + sin + nout + sout]
        sems = refs[nin + sin + nout + sout:]
        first = pl.program_id(0) == 0
        last = pl.program_id(0) == grid[0] - 1
        for ax in range(1, len(grid)):
            first = first & (pl.program_id(ax) == 0)
            last = last & (pl.program_id(ax) == grid[ax] - 1)

        @pl.when(first)
        def _():
            side.start(side_in, side_out, sems)

        body(*base_in, *base_out)

        @pl.when(last)
        def _():
            side.finish(side_in, side_out, sems)

    any_spec = pl.BlockSpec(memory_space=pl.ANY)
    res = pl.pallas_call(
        full, name=name, grid=grid, in_specs=list(in_specs) + [any_spec] * sin,
        out_specs=list(out_specs) + [any_spec] * sout, out_shape=list(out_shape) + list(side.outs),
        scratch_shapes=list(side.sems), input_output_aliases={nin + i: nout + o for i, o in side.alias.items()},
        compiler_params=_params(sem))(*args, *side.ins)
    return res[:nout], res[nout:]


def _ffn_up(n, wg, wu, name, side=None, tiles=MM_TILES):
    tp, d = n.shape
    fp = wg.shape[1]
    tm = tp // tiles

    def body(n_ref, wg_ref, wu_ref, a_ref, b_ref, s_ref):
        nn = n_ref[...]
        for c0 in range(0, fp, MXU_COLS):
            cs = slice(c0, min(c0 + MXU_COLS, fp))
            a = _dot_nt(nn, wg_ref[cs, :])
            b = _dot_nt(nn, wu_ref[cs, :])
            a_ref[:, cs] = a.astype(BF)
            b_ref[:, cs] = b.astype(BF)
            s_ref[:, cs] = (a * _sigmoid(a) * b).astype(BF)

    out = jax.ShapeDtypeStruct((N_CHIP, tp, fp), BF)
    wspec = pl.BlockSpec((None, fp, d), lambda k, i: (k, 0, 0))
    ospec = pl.BlockSpec((None, tm, fp), lambda k, i: (k, i, 0))
    return _grid_call(body, name, (N_CHIP, tiles), [pl.BlockSpec((tm, d), lambda k, i: (i, 0)), wspec, wspec],
                      [ospec, ospec, ospec], [out, out, out], (n, wg, wu), side)


def _ffn_bwd_act(df, wd, a, b, name, side=None, tiles=MM_TILES):
    tp, d = df.shape
    fp = wd.shape[1]
    tm = tp // tiles

    def body(df_ref, wd_ref, a_ref, b_ref, da_ref, db_ref):
        dfv = df_ref[...]
        for c0 in range(0, fp, MXU_COLS):
            cs = slice(c0, min(c0 + MXU_COLS, fp))
            ds = _dot_nt(dfv, wd_ref[cs, :])
            av = a_ref[:, cs].astype(F32)
            bv = b_ref[:, cs].astype(F32)
            sg = _sigmoid(av)
            da_ref[:, cs] = (ds * bv * sg * (1.0 + av * (1.0 - sg))).astype(BF)
            db_ref[:, cs] = (ds * av * sg).astype(BF)

    out = jax.ShapeDtypeStruct((N_CHIP, tp, fp), BF)
    aspec = pl.BlockSpec((None, tm, fp), lambda k, i: (k, i, 0))
    return _grid_call(
        body, name, (N_CHIP, tiles),
        [pl.BlockSpec((tm, d), lambda k, i: (i, 0)), pl.BlockSpec((None, fp, d), lambda k, i: (k, 0, 0)), aspec, aspec],
        [aspec, aspec], [out, out], (df, wd, a, b), side)


def _col_matmul(lhs, w, name, trans_b, out_dtype, side=None, tiles=MM_TILES_BIG):
    tp, kd = lhs.shape
    nk = w.shape[0]
    nc = w.shape[1] if trans_b else w.shape[2]
    tm = tp // tiles

    def body(l_ref, w_ref, o_ref):
        if trans_b:
            o_ref[...] = _dot_nt(l_ref[...], w_ref[...]).astype(out_dtype)
        else:
            o_ref[...] = _dot(l_ref[...], w_ref[...]).astype(out_dtype)

    res, extra = _grid_call(
        body, name, (nk, tiles),
        [pl.BlockSpec((tm, kd), lambda k, i: (i, 0)),
         pl.BlockSpec((None,) + tuple(w.shape[1:]), lambda k, i: (k, 0, 0), pipeline_mode=pl.Buffered(1))],
        [pl.BlockSpec((tm, nc), lambda k, i: (i, k))], [jax.ShapeDtypeStruct((tp, nk * nc), out_dtype)], (lhs, w), side)
    return res[0], extra


def _row_matmul(pairs, name, trans_b, d_out, side=None, tiles=MM_TILES_BIG):
    l0 = pairs[0][0]
    tp = l0.shape[1] if l0.ndim == 3 else l0.shape[0]
    nk = pairs[0][1].shape[0]
    tm = tp // tiles
    npair = len(pairs)

    def body(*refs):
        o_ref = refs[2 * npair]
        k = pl.program_id(1)
        part = None
        for q in range(npair):
            l = refs[2 * q][...]
            w = refs[2 * q + 1][...]
            t = _dot_nt(l, w) if trans_b else _dot(l, w)
            part = t if part is None else part + t

        @pl.when(k == 0)
        def _():
            o_ref[...] = part

        @pl.when(k > 0)
        def _():
            o_ref[...] += part

    in_specs, args = [], []
    for lhs, w in pairs:
        if lhs.ndim == 3:
            in_specs.append(pl.BlockSpec((None, tm, lhs.shape[2]), lambda i, k: (k, i, 0)))
        else:
            in_specs.append(pl.BlockSpec((tm, lhs.shape[1] // nk), lambda i, k: (i, k)))
        in_specs.append(pl.BlockSpec((None,) + tuple(w.shape[1:]), lambda i, k: (k, 0, 0)))
        args += [lhs, w]
    res, extra = _grid_call(body, name, (tiles, nk), in_specs, [pl.BlockSpec((tm, d_out), lambda i, k: (i, 0))],
                            [jax.ShapeDtypeStruct((tp, d_out), F32)], args, side)
    return res[0], extra


def _wgrad_call(x, y, name, x_width=None, y_width=None, tile_x=None, tile_y=None, side=None):
    tp = x.shape[1] if x.ndim == 3 else x.shape[0]

    def spec(a, width, tile):
        cols = a.shape[2] if a.ndim == 3 else (a.shape[1] if width is None else width)
        tc = cols if tile is None else tile
        per = cols // tc
        if a.ndim == 3:
            return pl.BlockSpec((None, tp, tc), lambda k, t: (k, 0, t if tile else 0)), cols, per
        if width is None:
            return pl.BlockSpec((tp, tc), lambda k, t: (0, t if tile else 0)), cols, per
        return pl.BlockSpec((tp, tc), lambda k, t: (0, k * per + (t if tile else 0))), cols, per

    xs, p, nx = spec(x, x_width, tile_x)
    ys, q, ny = spec(y, y_width, tile_y)
    nt = nx * ny
    if tile_x:
        ospec = pl.BlockSpec((None, tile_x, q), lambda k, t: (k, t, 0))
    else:
        ospec = pl.BlockSpec((None, p, tile_y), lambda k, t: (k, 0, t))

    def body(x_ref, y_ref, o_ref):
        o_ref[...] = _dot_tn(x_ref[...], y_ref[...]).astype(BF)

    res, extra = _grid_call(body, name, (N_CHIP, nt), [xs, ys], [ospec], [jax.ShapeDtypeStruct((N_CHIP, p, q), BF)],
                            (x, y), side)
    return res[0], extra


def _row_call(body, name, tp, d, row_ins, vec_ins, row_out_dtypes, n_acc, side=None):
    te = tp // EW_TILES
    rspec = pl.BlockSpec((te, d), lambda i: (i, 0))
    vspec = pl.BlockSpec((1, d), lambda i: (0, 0))
    res, extra = _grid_call(
        body, name, (EW_TILES,), [rspec] * len(row_ins) + [vspec] * len(vec_ins),
        [rspec] * len(row_out_dtypes) + [vspec] * n_acc,
        [jax.ShapeDtypeStruct((tp, d), dt) for dt in row_out_dtypes] + [jax.ShapeDtypeStruct((1, d), F32)] * n_acc,
        (*row_ins, *vec_ins), side)
    return res if side is None else (res, extra)


def _norm0(h, g):
    tp, d = h.shape

    def body(h_ref, g_ref, n_ref):
        n_ref[...] = _rms(h_ref[...], g_ref[...]).astype(BF)

    return _row_call(body, "norm0", tp, d, [h], [g], [BF], 0)[0]


def _post_fwd(f, h, g_post, g_next, scale, name):
    tp, d = h.shape

    def body(f_ref, h_ref, gp_ref, gn_ref, hn_ref, n_ref):
        hn = h_ref[...] + scale * _rms(f_ref[...], gp_ref[...])
        hn_ref[...] = hn
        n_ref[...] = _rms(hn, gn_ref[...]).astype(BF)

    return _row_call(body, name, tp, d, [f, h], [g_post, g_next], [F32, BF], 0)


def _loss_bwd(f, h, tgt, g_post, t_real):
    tp, d = h.shape
    te = tp // EW_TILES

    def body(f_ref, h_ref, t_ref, gp_ref, dh_ref, df_ref, dg_ref, loss_ref):
        i = pl.program_id(0)

        @pl.when(i == 0)
        def _():
            dg_ref[...] = jnp.zeros_like(dg_ref)
            loss_ref[...] = jnp.zeros_like(loss_ref)

        f = f_ref[...]
        gp = gp_ref[...]
        h3 = h_ref[...] + 0.5 * _rms(f, gp)
        rows = i * te + lax.broadcasted_iota(jnp.int32, (te, 1), 0)
        real = (rows >= N_META) & (rows < t_real)
        e = jnp.where(real, h3 - t_ref[...], 0.0)
        loss_ref[...] += 0.5 * jnp.sum(jnp.sum(e * e, axis=1, keepdims=True), axis=0, keepdims=True) / d
        dh = e / d
        dh_ref[...] = dh
        dfv, dgr = _rms_bwd(f, gp, 0.5 * dh)
        df_ref[...] = dfv.astype(BF)
        dg_ref[...] += jnp.sum(dgr, axis=0, keepdims=True)

    rspec = pl.BlockSpec((te, d), lambda i: (i, 0))
    vspec = pl.BlockSpec((1, d), lambda i: (0, 0))
    return pl.pallas_call(
        body, name="loss_bwd", grid=(EW_TILES,),
        in_specs=[rspec, rspec, rspec, vspec],
        out_specs=[rspec, rspec, vspec, pl.BlockSpec((1, 1), lambda i: (0, 0))],
        out_shape=[jax.ShapeDtypeStruct((tp, d), F32), jax.ShapeDtypeStruct((tp, d), BF),
                   jax.ShapeDtypeStruct((1, d), F32), jax.ShapeDtypeStruct((1, 1), F32)],
        compiler_params=_params(("arbitrary",)),
    )(f, h, tgt, g_post)


def _pre_bwd(dn, h, dh_out, g_pre, name, chain=None, side=None):
    tp, d = h.shape

    def body(*refs):
        if chain is None:
            dn_ref, h_ref, dho_ref, g_ref, dh_ref, dg_ref = refs
        else:
            dn_ref, h_ref, dho_ref, p_ref, g_ref, gp_ref, dh_ref, dp_ref, dg_ref, dgp_ref = refs
        i = pl.program_id(0)

        @pl.when(i == 0)
        def _():
            dg_ref[...] = jnp.zeros_like(dg_ref)
            if chain is not None:
                dgp_ref[...] = jnp.zeros_like(dgp_ref)

        dx, dgr = _rms_bwd(h_ref[...], g_ref[...], dn_ref[...])
        dh = dho_ref[...] + dx
        dh_ref[...] = dh
        dg_ref[...] += jnp.sum(dgr, axis=0, keepdims=True)
        if chain is not None:
            dp, dgpr = _rms_bwd(p_ref[...], gp_ref[...], chain[2] * dh)
            dp_ref[...] = dp.astype(BF)
            dgp_ref[...] += jnp.sum(dgpr, axis=0, keepdims=True)

    if chain is None:
        return _row_call(body, name, tp, d, [dn, h, dh_out], [g_pre], [F32], 1, side)
    return _row_call(body, name, tp, d, [dn, h, dh_out, chain[0]], [g_pre, chain[1]], [F32, BF], 2, side)


def _gelu(y):
    c = math.sqrt(2.0 / math.pi)
    return 0.5 * y * (1.0 + jnp.tanh(c * (y + 0.044715 * y * y * y)))


def _gelu_grad(y):
    c = math.sqrt(2.0 / math.pi)
    t = jnp.tanh(c * (y + 0.044715 * y * y * y))
    return 0.5 * (1.0 + t) + 0.5 * y * (1.0 - t * t) * c * (1.0 + 3.0 * 0.044715 * y * y)


def _neg_expm1(x):
    p = 1.0 + x * (1.0 / 9.0)
    for n in (8.0, 7.0, 6.0, 5.0, 4.0, 3.0, 2.0):
        p = 1.0 + x * (1.0 / n) * p
    return -jnp.where(x > -0.35, x * p, jnp.exp(x) - 1.0)


def _softplus(x):
    e = jnp.exp(-jnp.abs(x))
    w = 1.0 + e
    l1p = jnp.where(w == 1.0, e, jnp.log(w) * (e / jnp.where(w == 1.0, 1.0, w - 1.0)))
    return jnp.maximum(x, 0.0) + l1p


def _group_mean(v, gm):
    hi = v.astype(BF)
    lo = (v - hi.astype(F32)).astype(BF)
    return _dot(hi, gm) + _dot(lo, gm)


def _shift_dn(win, s, r):
    if s == 0:
        return win[8:8 + r]
    return pltpu.roll(win, s, 0)[8:8 + r]


def _shift_up(win, s, r):
    if s == 0:
        return win[0:r]
    return pltpu.roll(win, r + 8 - s, 0)[0:r]


def _window_dn(ref, t0, r, first):
    if first:
        return jnp.concatenate([jnp.zeros((8, ref.shape[1]), F32), ref[0:r, :]], axis=0)
    return ref[pl.ds(t0 - 8, r + 8), :]


def _tile_scan(a, u, reverse):
    r = a.shape[0]
    rid = lax.broadcasted_iota(jnp.int32, a.shape, 0) & 7
    for dlt in (1, 2, 4):
        sh = (r - dlt) if reverse else dlt
        a_s = pltpu.roll(a, sh, 0)
        u_s = pltpu.roll(u, sh, 0)
        keep = (rid + dlt <= 7) if reverse else (rid >= dlt)
        u = jnp.where(keep, u + a * u_s, u)
        a = jnp.where(keep, a * a_s, a)
    return a, u


def _lru_gates(xc, wa, ba, wx, bx, sp):
    xb = xc.astype(BF)
    ga = _sigmoid(_dot(xb, wa) + ba)
    gx = _sigmoid(_dot(xb, wx) + bx)
    la = -LRU_C * ga * sp
    return ga, gx, la


def _conv4(win, w4, cb, r):
    return (cb + w4[3:4] * _shift_dn(win, 0, r) + w4[2:3] * _shift_dn(win, 1, r)
            + w4[1:2] * _shift_dn(win, 2, r) + w4[0:1] * _shift_dn(win, 3, r))


def _lru_fwd(z, w4, cb, wa2, ba, wx2, bx, lam, g_out, gm, side=None):
    tp = z.shape[0]
    dl = cb.shape[1]
    nb = dl // LANE
    r = tp // MIX_CHUNKS
    c = LANE

    def body(y_ref, x_ref, w4_ref, cb_ref, wa_ref, ba_ref, wx_ref, bx_ref, lam_ref, go_ref, gm_ref, m_ref, hs_ref):
        w4v = w4_ref[...]
        cbv = cb_ref[...]
        wa = wa_ref[...]
        wx = wx_ref[...]
        bav = ba_ref[...]
        bxv = bx_ref[...]
        gov = go_ref[...]
        gmv = gm_ref[...]
        sp = _softplus(-lam_ref[...])

        def chunk(t0, hprev, first):
            win = _window_dn(x_ref, t0, r, first)
            xc = _conv4(win, w4v, cbv, r)
            ga, gx, la = _lru_gates(xc, wa, bav, wx, bxv, sp)
            a = jnp.exp(la)
            u = jnp.sqrt(_neg_expm1(2.0 * la)) * gx * xc
            ac, uc = _tile_scan(a, u, False)
            for j in range(r // 8):
                hj = uc[8 * j:8 * j + 8] + ac[8 * j:8 * j + 8] * hprev
                hs_ref[pl.ds(t0 + 8 * j, 8), :] = hj
                hprev = jnp.broadcast_to(hj[7:8], (8, c))
            h = hs_ref[pl.ds(t0, r), :]
            lo = h * _gelu(y_ref[pl.ds(t0, r), :])
            rs = lax.rsqrt(_group_mean(lo * lo, gmv) + EPS)
            m_ref[pl.ds(t0, r), :] = (lo * rs * gov).astype(BF)
            return hprev

        hp = chunk(0, jnp.zeros((8, c), F32), True)

        def loop(ci, hp):
            return chunk(pl.multiple_of(ci * r, 16), hp, False)

        lax.fori_loop(1, MIX_CHUNKS, loop, hp)

    col = lambda off: pl.BlockSpec((tp, c), lambda j: (0, off + j))
    vec = pl.BlockSpec((1, c), lambda j: (0, j))
    return _grid_call(
        body, "lru_fwd", (nb,),
        [col(0), col(nb), pl.BlockSpec((8, c), lambda j: (0, j)), vec, pl.BlockSpec((None, c, c), lambda j: (j, 0, 0)),
         vec, pl.BlockSpec((None, c, c), lambda j: (j, 0, 0)), vec, vec, vec, pl.BlockSpec((c, c), lambda j: (0, 0))],
        [col(0), col(0)], [jax.ShapeDtypeStruct((tp, dl), BF), jax.ShapeDtypeStruct((tp, dl), F32)],
        (z, z, w4, cb, wa2, ba, wx2, bx, lam, g_out, gm), side)


def _lru_bwd(z, hs, dmix, w4, cb, wa2, ba, wx2, bx, lam, g_out, gm):
    tp = z.shape[0]
    dl = cb.shape[1]
    nb = dl // LANE
    r = tp // MIX_CHUNKS
    c = LANE

    def body(y_ref, x_ref, hs_ref, dm_ref, w4_ref, cb_ref, wa_ref, ba_ref, wx_ref, bx_ref, lam_ref, go_ref, gm_ref,
             dy_ref, dx_ref, small_ref, dwa_ref, dwx_ref, xc_buf, ga_buf, gx_buf, a_buf, dh_buf, dxc_buf):
        w4v = w4_ref[...]
        cbv = cb_ref[...]
        wa = wa_ref[...]
        wx = wx_ref[...]
        bav = ba_ref[...]
        bxv = bx_ref[...]
        gov = go_ref[...]
        gmv = gm_ref[...]
        lamv = lam_ref[...]
        sp = _softplus(-lamv)
        small_ref[...] = jnp.zeros_like(small_ref)
        dwa_ref[...] = jnp.zeros_like(dwa_ref)
        dwx_ref[...] = jnp.zeros_like(dwx_ref)
        a_buf[pl.ds(tp, 8), :] = jnp.zeros((8, c), F32)
        dxc_buf[pl.ds(tp, 8), :] = jnp.zeros((8, c), F32)

        def fwd_chunk(t0, first):
            win = _window_dn(x_ref, t0, r, first)
            xc = _conv4(win, w4v, cbv, r)
            ga, gx, la = _lru_gates(xc, wa, bav, wx, bxv, sp)
            xc_buf[pl.ds(t0, r), :] = xc
            ga_buf[pl.ds(t0, r), :] = ga
            gx_buf[pl.ds(t0, r), :] = gx
            a_buf[pl.ds(t0, r), :] = jnp.exp(la)
            h = hs_ref[pl.ds(t0, r), :]
            yv = y_ref[pl.ds(t0, r), :]
            ge = _gelu(yv)
            lo = h * ge
            rs = lax.rsqrt(_group_mean(lo * lo, gmv) + EPS)
            xh = lo * rs
            dm = dm_ref[pl.ds(t0, r), :]
            q = dm * gov
            dlo = rs * (q - xh * _group_mean(q * xh, gmv))
            small_ref[8:9, :] += jnp.sum(dm * xh, axis=0, keepdims=True)
            dh_buf[pl.ds(t0, r), :] = dlo * ge
            dy_ref[pl.ds(t0, r), :] = (dlo * h * _gelu_grad(yv)).astype(BF)

        fwd_chunk(0, True)

        def floop(ci, carry):
            fwd_chunk(pl.multiple_of(ci * r, 16), False)
            return carry

        lax.fori_loop(1, MIX_CHUNKS, floop, 0)

        def bwd_chunk(t0, vnext, first):
            ap = _shift_up(a_buf[pl.ds(t0, r + 8), :], 1, r)
            ac, uc = _tile_scan(ap, dh_buf[pl.ds(t0, r), :], True)
            for j in reversed(range(r // 8)):
                vj = uc[8 * j:8 * j + 8] + ac[8 * j:8 * j + 8] * vnext
                dh_buf[pl.ds(t0 + 8 * j, 8), :] = vj
                vnext = jnp.broadcast_to(vj[0:1], (8, c))
            v = dh_buf[pl.ds(t0, r), :]
            hprev = _shift_dn(_window_dn(hs_ref, t0, r, first), 1, r)
            xc = xc_buf[pl.ds(t0, r), :]
            ga = ga_buf[pl.ds(t0, r), :]
            gx = gx_buf[pl.ds(t0, r), :]
            a = a_buf[pl.ds(t0, r), :]
            em = _neg_expm1(-2.0 * LRU_C * ga * sp)
            mult = jnp.sqrt(em)
            dla = v * hprev * a - (v * gx * xc) * ((1.0 - em) / mult)
            dgx = v * mult * xc
            dxc = v * mult * gx
            dga = dla * (-LRU_C) * sp
            small_ref[7:8, :] += jnp.sum(dla * (-LRU_C) * ga, axis=0, keepdims=True)
            dpa = dga * ga * (1.0 - ga)
            dpx = dgx * gx * (1.0 - gx)
            small_ref[5:6, :] += jnp.sum(dpa, axis=0, keepdims=True)
            small_ref[6:7, :] += jnp.sum(dpx, axis=0, keepdims=True)
            dpab = dpa.astype(BF)
            dpxb = dpx.astype(BF)
            xb = xc.astype(BF)
            dxc = dxc + _dot_nt(dpab, wa) + _dot_nt(dpxb, wx)
            dwa_ref[...] += _dot_tn(xb, dpab)
            dwx_ref[...] += _dot_tn(xb, dpxb)
            dxc_buf[pl.ds(t0, r), :] = dxc
            small_ref[4:5, :] += jnp.sum(dxc, axis=0, keepdims=True)
            dwin = dxc_buf[pl.ds(t0, r + 8), :]
            dx_ref[pl.ds(t0, r), :] = (w4v[3:4] * dxc + w4v[2:3] * _shift_up(dwin, 1, r)
                                       + w4v[1:2] * _shift_up(dwin, 2, r) + w4v[0:1] * _shift_up(dwin, 3, r)).astype(BF)
            xwin = _window_dn(x_ref, t0, r, first)
            for k in range(4):
                small_ref[k:k + 1, :] += jnp.sum(dxc * _shift_dn(xwin, 3 - k, r), axis=0, keepdims=True)
            return vnext

        def bloop(it, vnext):
            ci = MIX_CHUNKS - 1 - it
            return bwd_chunk(pl.multiple_of(ci * r, 16), vnext, False)

        vn = lax.fori_loop(0, MIX_CHUNKS - 1, bloop, jnp.zeros((8, c), F32))
        bwd_chunk(0, vn, True)
        small_ref[7:8, :] = small_ref[7:8, :] * (-_sigmoid(-lamv))

    col = lambda off: pl.BlockSpec((tp, c), lambda j: (0, off + j))
    vec = pl.BlockSpec((1, c), lambda j: (0, j))
    mat = pl.BlockSpec((None, c, c), lambda j: (j, 0, 0))
    buf = pltpu.VMEM((tp, c), F32)
    bufp = pltpu.VMEM((tp + 8, c), F32)
    return pl.pallas_call(
        body, name="lru_bwd", grid=(nb,),
        in_specs=[col(0), col(nb), col(0), col(0), pl.BlockSpec((8, c), lambda j: (0, j)), vec, mat, vec, mat, vec,
                  vec, vec, pl.BlockSpec((c, c), lambda j: (0, 0))],
        out_specs=[col(0), col(0), pl.BlockSpec((16, c), lambda j: (0, j)), mat, mat],
        out_shape=[jax.ShapeDtypeStruct((tp, dl), BF), jax.ShapeDtypeStruct((tp, dl), BF),
                   jax.ShapeDtypeStruct((16, dl), F32), jax.ShapeDtypeStruct((nb, c, c), F32),
                   jax.ShapeDtypeStruct((nb, c, c), F32)],
        scratch_shapes=[buf, buf, buf, bufp, buf, bufp],
        compiler_params=_params(("arbitrary",)),
    )(z, z, hs, dmix, w4, cb, wa2, ba, wx2, bx, lam, g_out, gm)


def _sc_conv(cvwin, w3, r):
    return w3[2:3] * _shift_dn(cvwin, 0, r) + w3[1:2] * _shift_dn(cvwin, 1, r) + w3[0:1] * _shift_dn(cvwin, 2, r)


def _sc_fwd(z, w3, g_out, gm, dl):
    tp = z.shape[0]
    nb = dl // LANE
    r = tp // MIX_CHUNKS
    c = LANE

    def body(b_ref, c_ref, v_ref, w3_ref, go_ref, gm_ref, m_ref):
        w3v = w3_ref[...]
        gov = go_ref[...]
        gmv = gm_ref[...]

        def chunk(t0, first):
            cvwin = _window_dn(c_ref, t0, r, first) * _window_dn(v_ref, t0, r, first)
            so = b_ref[pl.ds(t0, r), :] * _sc_conv(cvwin, w3v, r)
            rs = lax.rsqrt(_group_mean(so * so, gmv) + EPS)
            m_ref[pl.ds(t0, r), :] = (so * rs * gov).astype(BF)

        chunk(0, True)

        def loop(ci, carry):
            chunk(pl.multiple_of(ci * r, 16), False)
            return carry

        lax.fori_loop(1, MIX_CHUNKS, loop, 0)

    col = lambda off: pl.BlockSpec((tp, c), lambda j: (0, off + j))
    return pl.pallas_call(
        body, name="sconv_fwd", grid=(nb,),
        in_specs=[col(2 * nb), col(3 * nb), col(4 * nb), pl.BlockSpec((8, c), lambda j: (0, j)),
                  pl.BlockSpec((1, c), lambda j: (0, j)), pl.BlockSpec((c, c), lambda j: (0, 0))],
        out_specs=col(0), out_shape=jax.ShapeDtypeStruct((tp, dl), BF),
        compiler_params=_params(("arbitrary",)),
    )(z, z, z, w3, g_out, gm)


def _sc_bwd(z, dmix, w3, g_out, gm, dl):
    tp = z.shape[0]
    nb = dl // LANE
    r = tp // MIX_CHUNKS
    c = LANE

    def body(b_ref, c_ref, v_ref, dm_ref, w3_ref, go_ref, gm_ref, db_ref, dc_ref, dv_ref, small_ref, dsc_buf):
        w3v = w3_ref[...]
        gov = go_ref[...]
        gmv = gm_ref[...]
        small_ref[...] = jnp.zeros_like(small_ref)
        dsc_buf[pl.ds(tp, 8), :] = jnp.zeros((8, c), F32)

        def chunk1(t0, first):
            cvwin = _window_dn(c_ref, t0, r, first) * _window_dn(v_ref, t0, r, first)
            sc = _sc_conv(cvwin, w3v, r)
            bv = b_ref[pl.ds(t0, r), :]
            so = bv * sc
            rs = lax.rsqrt(_group_mean(so * so, gmv) + EPS)
            xh = so * rs
            dm = dm_ref[pl.ds(t0, r), :]
            q = dm * gov
            dso = rs * (q - xh * _group_mean(q * xh, gmv))
            small_ref[3:4, :] += jnp.sum(dm * xh, axis=0, keepdims=True)
            db_ref[pl.ds(t0, r), :] = (dso * sc).astype(BF)
            dsc = dso * bv
            dsc_buf[pl.ds(t0, r), :] = dsc
            for k in range(3):
                small_ref[k:k + 1, :] += jnp.sum(dsc * _shift_dn(cvwin, 2 - k, r), axis=0, keepdims=True)

        chunk1(0, True)

        def loop1(ci, carry):
            chunk1(pl.multiple_of(ci * r, 16), False)
            return carry

        lax.fori_loop(1, MIX_CHUNKS, loop1, 0)

        def loop2(ci, carry):
            t0 = pl.multiple_of(ci * r, 16)
            dwin = dsc_buf[pl.ds(t0, r + 8), :]
            dcv = w3v[2:3] * _shift_up(dwin, 0, r) + w3v[1:2] * _shift_up(dwin, 1, r) + w3v[0:1] * _shift_up(dwin, 2, r)
            dc_ref[pl.ds(t0, r), :] = (dcv * v_ref[pl.ds(t0, r), :]).astype(BF)
            dv_ref[pl.ds(t0, r), :] = (dcv * c_ref[pl.ds(t0, r), :]).astype(BF)
            return carry

        lax.fori_loop(0, MIX_CHUNKS, loop2, 0)

    col = lambda off: pl.BlockSpec((tp, c), lambda j: (0, off + j))
    out = jax.ShapeDtypeStruct((tp, dl), BF)
    return pl.pallas_call(
        body, name="sconv_bwd", grid=(nb,),
        in_specs=[col(2 * nb), col(3 * nb), col(4 * nb), col(nb), pl.BlockSpec((8, c), lambda j: (0, j)),
                  pl.BlockSpec((1, c), lambda j: (0, j)), pl.BlockSpec((c, c), lambda j: (0, 0))],
        out_specs=[col(0), col(0), col(0), pl.BlockSpec((8, c), lambda j: (0, j))],
        out_shape=[out, out, out, jax.ShapeDtypeStruct((8, dl), F32)],
        scratch_shapes=[pltpu.VMEM((tp + 8, c), F32)],
        compiler_params=_params(("arbitrary",)),
    )(z, z, z, dmix, w3, g_out, gm)


def _cast_pad(w, rows_p, cols_p, chip, name):
    r, c = w.shape

    def body(chip_ref, w_ref, o_ref):
        if (rows_p, cols_p) != (r, c):
            o_ref[...] = jnp.zeros_like(o_ref)
        o_ref[0:r, 0:c] = w_ref[...].astype(BF)

    return pl.pallas_call(
        body, name=name, out_shape=jax.ShapeDtypeStruct((N_CHIP, rows_p, cols_p), BF),
        grid_spec=pltpu.PrefetchScalarGridSpec(
            num_scalar_prefetch=1, grid=(1,),
            in_specs=[pl.BlockSpec((r, c), lambda i, chip: (0, 0))],
            out_specs=pl.BlockSpec((None, rows_p, cols_p), lambda i, chip: (chip[0], 0, 0))),
        compiler_params=_params(("arbitrary",)),
    )(chip, w)


def _adamw_math(w, g, m, v):
    m2 = ADAM_B1 * m + (1.0 - ADAM_B1) * g
    v2 = ADAM_B2 * v + (1.0 - ADAM_B2) * (g * g)
    m_hat = m2 / (1.0 - ADAM_B1 ** ADAM_STEP)
    v_hat = v2 / (1.0 - ADAM_B2 ** ADAM_STEP)
    delta = -ADAM_LR * (m_hat / (jnp.sqrt(v_hat) + ADAM_EPS) + ADAM_WD * w)
    return delta, m2, v2


def _adamw(w, g, m, v, name, row_tiles, col_tiles, side=None):
    r, c = w.shape
    tr = r // row_tiles
    tc = c // col_tiles
    gc = g.shape[1] if col_tiles == 1 else tc

    def body(w_ref, g_ref, m_ref, v_ref, go_ref, d_ref, mo_ref, vo_ref):
        gv = g_ref[...][:, 0:tc]
        delta, m2, v2 = _adamw_math(w_ref[...], gv, m_ref[...], v_ref[...])
        go_ref[...] = gv
        d_ref[...] = delta
        mo_ref[...] = m2
        vo_ref[...] = v2

    spec = pl.BlockSpec((tr, tc), lambda i, j: (i, j))
    out = jax.ShapeDtypeStruct((r, c), F32)
    return _grid_call(body, name, (row_tiles, col_tiles), [spec, pl.BlockSpec((tr, gc), lambda i, j: (i, j)), spec, spec],
                      [spec] * 4, [out] * 4, (w, g, m, v), side)


def _adamw_small(w, g_top, g4, m, v):
    def body(w_ref, gt_ref, g_ref, m_ref, v_ref, go_ref, d_ref, mo_ref, vo_ref):
        g = jnp.concatenate([gt_ref[...], (g_ref[0] + g_ref[1]) + (g_ref[2] + g_ref[3])], axis=0)
        delta, m2, v2 = _adamw_math(w_ref[...], g, m_ref[...], v_ref[...])
        go_ref[...] = g
        d_ref[...] = delta
        mo_ref[...] = m2
        vo_ref[...] = v2

    out = jax.ShapeDtypeStruct(w.shape, F32)
    spec = pl.BlockSpec(w.shape, lambda: (0, 0))
    return pl.pallas_call(
        body, name="adamw_small",
        in_specs=[spec, pl.BlockSpec(g_top.shape, lambda: (0, 0)), pl.BlockSpec(g4.shape, lambda: (0, 0, 0)), spec, spec],
        out_specs=[spec] * 4, out_shape=[out] * 4, compiler_params=_params())(w, g_top, g4, m, v)


def _place():
    x, y, c = lax.axis_index("x"), lax.axis_index("y"), lax.axis_index("c")
    chips = [(1 - x, y), (x, 1 - y), (1 - x, 1 - y)]
    return x, y, c, chips


ANY = pl.BlockSpec(memory_space=pl.ANY)


def _gather_side(bufs):
    n = len(bufs)

    def copies(outs, sems):
        s_ici, r_ici, s_d2d, r_d2d = sems
        x, y, c, chips = _place()
        me = 2 * x + y

        def rows(w, chip, core):
            half = bufs[w].shape[1] // 2
            return outs[w].at[chip, pl.ds(core * half, half)]

        def ici(w, j, chip):
            px, py = chips[j]
            return pltpu.make_async_remote_copy(
                src_ref=rows(w, chip, c), dst_ref=rows(w, chip, c),
                send_sem=s_ici.at[w, j], recv_sem=r_ici.at[w, j], device_id=(px, py, c), device_id_type=MESH)

        def d2d(w, j, core):
            px, py = chips[j]
            return pltpu.make_async_remote_copy(
                src_ref=rows(w, 2 * px + py, core), dst_ref=rows(w, 2 * px + py, core),
                send_sem=s_d2d.at[w, j], recv_sem=r_d2d.at[w, j], device_id=(x, y, 1 - c), device_id_type=MESH)

        pairs = [(w, j) for w in range(n) for j in range(3)]
        return me, c, chips, ici, d2d, pairs

    def start(ins, outs, sems):
        me, c, chips, ici, d2d, pairs = copies(outs, sems)
        for w, j in pairs:
            ici(w, j, me).start()

    def finish(ins, outs, sems):
        me, c, chips, ici, d2d, pairs = copies(outs, sems)
        for w, j in pairs:
            ici(w, j, 2 * chips[j][0] + chips[j][1]).wait_recv()
            d2d(w, j, c).start()
        for w, j in pairs:
            d2d(w, j, 1 - c).wait_recv()
        for w, j in pairs:
            ici(w, j, me).wait_send()
            d2d(w, j, c).wait_send()

    dma = pltpu.SemaphoreType.DMA((n, 3))
    return _Side(list(bufs), [jax.ShapeDtypeStruct(b.shape, b.dtype) for b in bufs], {w: w for w in range(n)},
                 [dma, dma, dma, dma], start, finish)


def _run_side(side, name):
    sin, sout = len(side.ins), len(side.outs)

    def body(*refs):
        ins, outs, sems = refs[:sin], refs[sin:sin + sout], refs[sin + sout:]
        side.start(ins, outs, sems)
        side.finish(ins, outs, sems)

    return pl.pallas_call(
        body, name=name, out_shape=list(side.outs), in_specs=[ANY] * sin, out_specs=[ANY] * sout,
        scratch_shapes=list(side.sems), input_output_aliases=dict(side.alias))(*side.ins)


def _pair_exchange_side(grads):
    n = len(grads)

    def copies(ins, outs, sems):
        ssem, rsem = sems
        x, y, c, _ = _place()
        cps = []
        for w in range(n):
            half = grads[w].shape[1] // 2
            cps.append(pltpu.make_async_remote_copy(
                src_ref=ins[w].at[:, pl.ds((1 - c) * half, half)], dst_ref=outs[w],
                send_sem=ssem.at[w], recv_sem=rsem.at[w], device_id=(x, y, 1 - c), device_id_type=MESH))
        return cps

    def start(ins, outs, sems):
        for cp in copies(ins, outs, sems):
            cp.start()

    def finish(ins, outs, sems):
        for cp in copies(ins, outs, sems):
            cp.wait()

    dma = pltpu.SemaphoreType.DMA((n,))
    return _Side(list(grads), [jax.ShapeDtypeStruct((N_CHIP, g.shape[1] // 2, g.shape[2]), BF) for g in grads], {},
                 [dma, dma], start, finish)


def _sibling_copy_side(buf):
    def copy(ins, outs, sems):
        x, y, c, _ = _place()
        return pltpu.make_async_remote_copy(src_ref=ins[0], dst_ref=outs[0], send_sem=sems[0], recv_sem=sems[1],
                                            device_id=(x, y, 1 - c), device_id_type=MESH)

    return _Side([buf], [jax.ShapeDtypeStruct(buf.shape, buf.dtype)], {}, [pltpu.SemaphoreType.DMA, pltpu.SemaphoreType.DMA],
                 lambda i, o, s: copy(i, o, s).start(), lambda i, o, s: copy(i, o, s).wait())


def _slot_exchange_side(buf4):
    def copies(outs, sems, sending):
        ssem, rsem = sems
        x, y, c, chips = _place()
        me = 2 * x + y
        return [pltpu.make_async_remote_copy(
            src_ref=outs[0].at[me if sending else 2 * px + py], dst_ref=outs[0].at[me if sending else 2 * px + py],
            send_sem=ssem.at[j], recv_sem=rsem.at[j], device_id=(px, py, c), device_id_type=MESH)
            for j, (px, py) in enumerate(chips)]

    def start(ins, outs, sems):
        for cp in copies(outs, sems, True):
            cp.start()

    def finish(ins, outs, sems):
        for cp in copies(outs, sems, False):
            cp.wait_recv()
        for cp in copies(outs, sems, True):
            cp.wait_send()

    dma = pltpu.SemaphoreType.DMA((3,))
    return _Side([buf4], [jax.ShapeDtypeStruct(buf4.shape, buf4.dtype)], {0: 0}, [dma, dma], start, finish)


def _pair_sum(g, sib, core, name):
    _, r, cdim = g.shape
    half = r // 2

    def body(core_ref, g_ref, s_ref, o_ref):
        o_ref[...] = (g_ref[...].astype(F32) + s_ref[...].astype(F32)).astype(BF)

    return pl.pallas_call(
        body, name=name,
        grid_spec=pltpu.PrefetchScalarGridSpec(
            num_scalar_prefetch=1, grid=(N_CHIP,),
            in_specs=[pl.BlockSpec((None, half, cdim), lambda k, core: (k, core[0], 0)),
                      pl.BlockSpec((None, half, cdim), lambda k, core: (k, 0, 0))],
            out_specs=pl.BlockSpec((None, half, cdim), lambda k, core: (k, 0, 0))),
        out_shape=jax.ShapeDtypeStruct((N_CHIP, half, cdim), BF),
        compiler_params=_params(("arbitrary",)),
    )(core, g, sib)


def _chip_exchange_side(psums):
    n = len(psums)

    def copies(ins, outs, sems):
        ssem, rsem = sems
        x, y, c, chips = _place()
        return [pltpu.make_async_remote_copy(
            src_ref=ins[w].at[2 * px + py], dst_ref=outs[w].at[j],
            send_sem=ssem.at[w, j], recv_sem=rsem.at[w, j], device_id=(px, py, c), device_id_type=MESH)
            for w in range(n) for j, (px, py) in enumerate(chips)]

    def start(ins, outs, sems):
        for cp in copies(ins, outs, sems):
            cp.start()

    def finish(ins, outs, sems):
        for cp in copies(ins, outs, sems):
            cp.wait()

    dma = pltpu.SemaphoreType.DMA((n, 3))
    return _Side(list(psums), [jax.ShapeDtypeStruct((3,) + p.shape[1:], BF) for p in psums], {}, [dma, dma],
                 start, finish)


def _final_sum(g, sib, recv, sel, name):
    _, r, cdim = g.shape
    half = r // 2
    nt = 4
    th = half // nt

    def body(sel_ref, g_ref, s_ref, r_ref, o_ref):
        acc = g_ref[...].astype(F32) + s_ref[...].astype(F32)
        for j in range(3):
            acc = acc + r_ref[j].astype(F32)
        o_ref[...] = acc

    return pl.pallas_call(
        body, name=name,
        grid_spec=pltpu.PrefetchScalarGridSpec(
            num_scalar_prefetch=1, grid=(nt,),
            in_specs=[pl.BlockSpec((None, th, cdim), lambda i, sel: (sel[0], sel[1] * nt + i, 0)),
                      pl.BlockSpec((None, th, cdim), lambda i, sel: (sel[0], i, 0)),
                      pl.BlockSpec((3, th, cdim), lambda i, sel: (0, i, 0))],
            out_specs=pl.BlockSpec((th, cdim), lambda i, sel: (sel[1] * nt + i, 0))),
        out_shape=jax.ShapeDtypeStruct((r, cdim), F32),
        compiler_params=_params(("arbitrary",)),
    )(sel, g, sib, recv)


def _join_side(bufs):
    n = len(bufs)

    def copies(outs, sems, core_of):
        ssem, rsem = sems
        x, y, c, _ = _place()
        cps = []
        for w in range(n):
            half = bufs[w].shape[0] // 2
            rows = outs[w].at[pl.ds(core_of(c) * half, half)]
            cps.append(pltpu.make_async_remote_copy(
                src_ref=rows, dst_ref=rows, send_sem=ssem.at[w], recv_sem=rsem.at[w],
                device_id=(x, y, 1 - c), device_id_type=MESH))
        return cps

    def start(ins, outs, sems):
        for cp in copies(outs, sems, lambda c: c):
            cp.start()

    def finish(ins, outs, sems):
        for cp in copies(outs, sems, lambda c: 1 - c):
            cp.wait_recv()
        for cp in copies(outs, sems, lambda c: c):
            cp.wait_send()

    dma = pltpu.SemaphoreType.DMA((n,))
    return _Side(list(bufs), [jax.ShapeDtypeStruct(b.shape, F32) for b in bufs], {w: w for w in range(n)}, [dma, dma],
                 start, finish)


def _small_pair_sum(buf, sib, chip):
    rows, d = buf.shape

    def body(chip_ref, a_ref, b_ref, o_ref):
        o_ref[...] = a_ref[...] + b_ref[...]

    return pl.pallas_call(
        body, name="small_pair_sum", out_shape=jax.ShapeDtypeStruct((N_CHIP, rows, d), F32),
        grid_spec=pltpu.PrefetchScalarGridSpec(
            num_scalar_prefetch=1, grid=(1,),
            in_specs=[pl.BlockSpec((rows, d), lambda i, chip: (0, 0))] * 2,
            out_specs=pl.BlockSpec((None, rows, d), lambda i, chip: (chip[0], 0, 0))),
        compiler_params=_params(("arbitrary",)),
    )(chip, buf, sib)


def _small_all_reduce(buf, name):
    rows, d = buf.shape

    def body(in_ref, out_ref, sib, all4, ssem, rsem, psem, qsem):
        x, y, c, chips = _place()
        me = 2 * x + y
        to_sib = pltpu.make_async_remote_copy(src_ref=in_ref, dst_ref=sib, send_sem=ssem, recv_sem=rsem,
                                              device_id=(x, y, 1 - c), device_id_type=MESH)
        to_sib.start()
        to_sib.wait()
        all4[me] = in_ref[...] + sib[...]
        cps = [pltpu.make_async_remote_copy(src_ref=all4.at[me], dst_ref=all4.at[me], send_sem=psem.at[j],
                                            recv_sem=qsem.at[j], device_id=(px, py, c), device_id_type=MESH)
               for j, (px, py) in enumerate(chips)]
        for cp in cps:
            cp.start()
        for j, (px, py) in enumerate(chips):
            chip = 2 * px + py
            pltpu.make_async_remote_copy(src_ref=all4.at[chip], dst_ref=all4.at[chip], send_sem=psem.at[j],
                                         recv_sem=qsem.at[j], device_id=(px, py, c), device_id_type=MESH).wait_recv()
        for cp in cps:
            cp.wait_send()
        out_ref[...] = (all4[0] + all4[1]) + (all4[2] + all4[3])

    vm = pl.BlockSpec(memory_space=pltpu.VMEM)
    return pl.pallas_call(
        body, name=name, out_shape=jax.ShapeDtypeStruct((rows, d), F32),
        in_specs=[vm], out_specs=vm,
        scratch_shapes=[pltpu.VMEM((rows, d), F32), pltpu.VMEM((N_CHIP, rows, d), F32),
                        pltpu.SemaphoreType.DMA, pltpu.SemaphoreType.DMA,
                        pltpu.SemaphoreType.DMA((3,)), pltpu.SemaphoreType.DMA((3,))],
        compiler_params=_params(),
    )(buf)


def _pair_blocks(w):
    w4 = w.reshape(N_HEADS // 2, 2, HEAD, HEAD)
    eye = jnp.eye(2, dtype=w.dtype)
    return jnp.einsum("pirc,ij->pirjc", w4, eye).reshape(N_HEADS // 2, LANE, LANE)


def _unpair_blocks(w2):
    w5 = w2.reshape(N_HEADS // 2, 2, HEAD, 2, HEAD)
    return jnp.stack([w5[:, 0, :, 0, :], w5[:, 1, :, 1, :]], axis=1).reshape(N_HEADS, HEAD, HEAD)


def kernel(x, meta_tokens, ffn1_pre_g, ffn1_w_gate, ffn1_w_up, ffn1_w_down, ffn1_post_g, mix_pre_g, w_in, lru_conv_w, lru_conv_b, lru_w_a, lru_b_a, lru_w_x, lru_b_x, lru_lambda, sconv_w, lru_out_g, sconv_out_g, w_out, mix_post_g, ffn2_pre_g, ffn2_w_gate, ffn2_w_up, ffn2_w_down, ffn2_post_g, loss_target, m_meta_tokens, m_ffn1_pre_g, m_ffn1_w_gate, m_ffn1_w_up, m_ffn1_w_down, m_ffn1_post_g, m_mix_pre_g, m_w_in, m_lru_conv_w, m_lru_conv_b, m_lru_w_a, m_lru_b_a, m_lru_w_x, m_lru_b_x, m_lru_lambda, m_sconv_w, m_lru_out_g, m_sconv_out_g, m_w_out, m_mix_post_g, m_ffn2_pre_g, m_ffn2_w_gate, m_ffn2_w_up, m_ffn2_w_down, m_ffn2_post_g, v_meta_tokens, v_ffn1_pre_g, v_ffn1_w_gate, v_ffn1_w_up, v_ffn1_w_down, v_ffn1_post_g, v_mix_pre_g, v_w_in, v_lru_conv_w, v_lru_conv_b, v_lru_w_a, v_lru_b_a, v_lru_w_x, v_lru_b_x, v_lru_lambda, v_sconv_w, v_lru_out_g, v_sconv_out_g, v_w_out, v_mix_post_g, v_ffn2_pre_g, v_ffn2_w_gate, v_ffn2_w_up, v_ffn2_w_down, v_ffn2_post_g):
    seq, d = x.shape[1], x.shape[2]
    t_real = N_META + seq
    tp = _round_up(t_real, ROW_ALIGN)
    f4 = ffn1_w_gate.shape[2]
    f4p = _round_up(f4, LANE)
    dl = lru_conv_b.shape[1]
    cin = w_in.shape[2]
    xi, yi, ci = lax.axis_index("x"), lax.axis_index("y"), lax.axis_index("c")
    chip = 2 * xi + yi
    zero = jnp.zeros((), jnp.int32)

    transposed = ("ffn1_w_gate", "ffn1_w_up", "ffn2_w_gate", "ffn2_w_up")

    def view(k, a):
        return a[0].T if k in transposed else a[0]

    def unview(k, a):
        return (a.T if k in transposed else a)[None]

    big = {
        "ffn1_w_gate": (view("ffn1_w_gate", ffn1_w_gate), f4p, d), "ffn1_w_up": (view("ffn1_w_up", ffn1_w_up), f4p, d),
        "ffn1_w_down": (ffn1_w_down[0], f4p, d), "w_in": (w_in[0], d, cin), "w_out": (w_out[0], w_out.shape[1], d),
        "ffn2_w_gate": (view("ffn2_w_gate", ffn2_w_gate), f4p, d), "ffn2_w_up": (view("ffn2_w_up", ffn2_w_up), f4p, d),
        "ffn2_w_down": (ffn2_w_down[0], f4p, d),
    }
    names = list(big)
    chip1 = jnp.reshape(chip, (1,)).astype(jnp.int32)
    shard = {k: _cast_pad(big[k][0], big[k][1], big[k][2], chip1, "cast_" + k) for k in names}
    full = dict(zip(("ffn1_w_gate", "ffn1_w_up"),
                    _run_side(_gather_side([shard["ffn1_w_gate"], shard["ffn1_w_up"]]), "gather_ffn1_in")))

    gm = jnp.kron(jnp.eye(2, dtype=F32), jnp.full((HEAD, HEAD), 1.0 / HEAD, F32)).astype(BF)
    wa2 = _pair_blocks(lru_w_a[0])
    wx2 = _pair_blocks(lru_w_x[0])

    dlq = dl // N_CHIP
    dq = d // N_CHIP
    R_GAIN, R_LOSS, R_META, R_LRU, R_SC, R_WA = 0, 6, 8, 24, 40, 48
    n_wrows = (N_HEADS // 2) * LANE * LANE // d
    R_WX = R_WA + n_wrows
    R_END = R_WX + n_wrows

    def pack_top(gains, meta, loss=None):
        lossrow = jnp.zeros((2, d), F32)
        if loss is not None:
            lossrow = lossrow.at[0, 0].set(loss)
        return jnp.concatenate([jnp.concatenate(gains, axis=0), lossrow, meta], axis=0)

    def pack_rest(lru16, sc8, wa_, wx_):
        return jnp.concatenate([jnp.concatenate([lru16, jnp.zeros((16, d - dl), F32)], axis=1),
                                jnp.concatenate([sc8, jnp.zeros((8, d - dl), F32)], axis=1),
                                wa_.reshape(n_wrows, d), wx_.reshape(n_wrows, d)], axis=0)

    def pack(gains, meta, lru16, sc8, wa_, wx_):
        return jnp.concatenate([pack_top(gains, meta), pack_rest(lru16, sc8, wa_, wx_)], axis=0)

    def place_cols(blk, width, total):
        return lax.dynamic_update_slice(jnp.zeros((blk.shape[0], total), F32), blk, (zero, chip * width))

    def pack_params(meta_, g1pre, g1post, gmpre, gmpost, g2pre, g2post, cw, cbias, wa_, ba_, wx_, bx_, lam_, sw, lgo, sgo):
        lru16 = jnp.concatenate([place_cols(cw[0], dlq, dl), cbias, ba_, bx_, lam_, lgo, jnp.zeros((7, dl), F32)], axis=0)
        sc8 = jnp.concatenate([place_cols(sw[0], dlq, dl), sgo, jnp.zeros((4, dl), F32)], axis=0)
        return pack([g1pre, g1post, gmpre, gmpost, g2pre, g2post], place_cols(meta_, dq, d), lru16, sc8,
                    _pair_blocks(wa_[0]), _pair_blocks(wx_[0]))

    p_w = pack_params(meta_tokens, ffn1_pre_g, ffn1_post_g, mix_pre_g, mix_post_g, ffn2_pre_g, ffn2_post_g, lru_conv_w,
                      lru_conv_b, lru_w_a, lru_b_a, lru_w_x, lru_b_x, lru_lambda, sconv_w, lru_out_g, sconv_out_g)
    p_m = pack_params(m_meta_tokens, m_ffn1_pre_g, m_ffn1_post_g, m_mix_pre_g, m_mix_post_g, m_ffn2_pre_g, m_ffn2_post_g,
                      m_lru_conv_w, m_lru_conv_b, m_lru_w_a, m_lru_b_a, m_lru_w_x, m_lru_b_x, m_lru_lambda, m_sconv_w,
                      m_lru_out_g, m_sconv_out_g)
    p_v = pack_params(v_meta_tokens, v_ffn1_pre_g, v_ffn1_post_g, v_mix_pre_g, v_mix_post_g, v_ffn2_pre_g, v_ffn2_post_g,
                      v_lru_conv_w, v_lru_conv_b, v_lru_w_a, v_lru_b_a, v_lru_w_x, v_lru_b_x, v_lru_lambda, v_sconv_w,
                      v_lru_out_g, v_sconv_out_g)

    gathered = _small_all_reduce(jnp.where(ci == 0, p_w, 0.0)[R_META:R_WA], "small_weight_gather")
    meta_full = gathered[0:N_META]
    w4_full = gathered[R_LRU - R_META:R_LRU - R_META + 4, 0:dl]
    w3_full = gathered[R_SC - R_META:R_SC - R_META + 3, 0:dl]
    w4p = jnp.concatenate([w4_full, jnp.zeros((4, dl), F32)], axis=0)
    w3p = jnp.concatenate([w3_full, jnp.zeros((5, dl), F32)], axis=0)

    h0 = jnp.concatenate([meta_full, x[0], jnp.zeros((tp - t_real, d), F32)], axis=0)
    tgt = jnp.concatenate([jnp.zeros((N_META, d), F32), loss_target[0], jnp.zeros((tp - t_real, d), F32)], axis=0)

    n1 = _norm0(h0, ffn1_pre_g)
    (a1, b1, s1), got = _ffn_up(n1, full["ffn1_w_gate"], full["ffn1_w_up"], "ffn1_up",
                                _gather_side([shard["ffn1_w_down"], shard["w_in"]]))
    full["ffn1_w_down"], full["w_in"] = got
    f1, got = _row_matmul([(s1, full["ffn1_w_down"])], "ffn1_down", False, d, _gather_side([shard["w_out"]]))
    full["w_out"] = got[0]
    h1, u = _post_fwd(f1, h0, ffn1_post_g, mix_pre_g, 0.5, "ffn1_post")
    z, got = _col_matmul(u, full["w_in"], "in_proj", False, F32, _gather_side([shard["ffn2_w_gate"]]))
    full["ffn2_w_gate"] = got[0]
    (m_lru, hs), got = _lru_fwd(z, w4p, lru_conv_b, wa2.astype(BF), lru_b_a, wx2.astype(BF), lru_b_x, lru_lambda,
                                lru_out_g, gm, _gather_side([shard["ffn2_w_up"]]))
    full["ffn2_w_up"] = got[0]
    m_sc = _sc_fwd(z, w3p, sconv_out_g, gm, dl)
    mixed = jnp.concatenate([m_lru, m_sc], axis=1)
    p, _ = _row_matmul([(mixed, full["w_out"])], "out_proj", False, d)
    h2, n2 = _post_fwd(p, h1, mix_post_g, ffn2_pre_g, 1.0, "mix_post")
    (a2, b2, s2), got = _ffn_up(n2, full["ffn2_w_gate"], full["ffn2_w_up"], "ffn2_up",
                                _gather_side([shard["ffn2_w_down"]]))
    full["ffn2_w_down"] = got[0]
    f2, _ = _row_matmul([(s2, full["ffn2_w_down"])], "ffn2_down", False, d)
    dh3, df2, dg_ffn2_post, loss_part = _loss_bwd(f2, h2, tgt, ffn2_post_g, t_real)

    core = jnp.reshape(ci, (1,)).astype(jnp.int32)
    sel = jnp.stack([chip, ci]).astype(jnp.int32)
    red = {}

    def pair_side(k):
        return _pair_exchange_side([red[k][0]])

    def chip_side(k):
        return _chip_exchange_side([_pair_sum(red[k][0], red[k][1], core, "pair_sum_" + k)])

    (da2, db2), _ = _ffn_bwd_act(df2, full["ffn2_w_down"], a2, b2, "ffn2_bwd_act")
    g, _ = _wgrad_call(s2, df2, "ffn2_down_wgrad", tile_y=WGRAD_TILE_Y)
    red["ffn2_w_down"] = [g, None, None]
    g, got = _wgrad_call(da2, n2, "ffn2_gate_wgrad", tile_y=WGRAD_TILE_Y, side=pair_side("ffn2_w_down"))
    red["ffn2_w_down"][1] = got[0]
    red["ffn2_w_gate"] = [g, None, None]
    g, got = _wgrad_call(db2, n2, "ffn2_up_wgrad", tile_y=WGRAD_TILE_Y,
                         side=_merge_sides([pair_side("ffn2_w_gate"), chip_side("ffn2_w_down")]))
    red["ffn2_w_gate"][1], red["ffn2_w_down"][2] = got
    red["ffn2_w_up"] = [g, None, None]
    red["ffn2_w_up"][1] = _run_side(pair_side("ffn2_w_up"), "pair_exchange_ffn2_w_up")[0]
    dn2, got = _row_matmul([(da2, full["ffn2_w_gate"]), (db2, full["ffn2_w_up"])], "ffn2_bwd_up", False, d,
                           _merge_sides([chip_side("ffn2_w_gate"), chip_side("ffn2_w_up")]), tiles=MM_TILES)
    red["ffn2_w_gate"][2], red["ffn2_w_up"][2] = got
    dh2, dp, dg_ffn2_pre, dg_mix_post = _pre_bwd(dn2, h2, dh3, ffn2_pre_g, "ffn2_pre_bwd", (p, mix_post_g, 1.0))
    dmixed, _ = _col_matmul(dp, full["w_out"], "out_proj_bwd", True, F32)
    g, _ = _wgrad_call(mixed, dp, "w_out_wgrad", x_width=mixed.shape[1] // N_CHIP, tile_y=WGRAD_TILE_Y)
    red["w_out"] = [g, None, None]
    dzy, dzx, lru_small, dwa2, dwx2 = _lru_bwd(z, hs, dmixed, w4p, lru_conv_b, wa2.astype(BF), lru_b_a, wx2.astype(BF),
                                               lru_b_x, lru_lambda, lru_out_g, gm)
    dzb, dzc, dzv, sc_small = _sc_bwd(z, dmixed, w3p, sconv_out_g, gm, dl)
    dz = jnp.concatenate([dzy, dzx, dzb, dzc, dzv], axis=1)
    p_rest = pack_rest(lru_small, sc_small, dwa2, dwx2)
    g, got = _wgrad_call(u, dz, "w_in_wgrad", y_width=cin, tile_x=WGRAD_TILE_X,
                         side=_merge_sides([pair_side("w_out"), _sibling_copy_side(p_rest)]))
    red["w_out"][1] = got[0]
    p_rest4 = _small_pair_sum(p_rest, got[1], chip1)
    red["w_in"] = [g, None, None]
    du, got = _row_matmul([(dz, full["w_in"])], "in_proj_bwd", True, d,
                          _merge_sides([pair_side("w_in"), chip_side("w_out")]))
    red["w_in"][1], red["w_out"][2] = got
    dh1, df1, dg_mix_pre, dg_ffn1_post = _pre_bwd(du, h1, dh2, mix_pre_g, "mix_pre_bwd", (f1, ffn1_post_g, 0.5))
    (da1, db1), got = _ffn_bwd_act(df1, full["ffn1_w_down"], a1, b1, "ffn1_bwd_act",
                                   _merge_sides([chip_side("w_in"), _slot_exchange_side(p_rest4)]))
    red["w_in"][2], p_rest4 = got
    early = ["ffn2_w_down", "ffn2_w_gate", "ffn2_w_up", "w_out", "w_in"]
    late = ["ffn1_w_down", "ffn1_w_gate", "ffn1_w_up"]
    g, got = _wgrad_call(s1, df1, "ffn1_down_wgrad", tile_y=WGRAD_TILE_Y,
                         side=_join_side([_final_sum(*red[k], sel, "final_sum_" + k) for k in early]))
    gfull = dict(zip(early, got))
    red["ffn1_w_down"] = [g, None, None]
    g, got = _wgrad_call(da1, n1, "ffn1_gate_wgrad", tile_y=WGRAD_TILE_Y, side=pair_side("ffn1_w_down"))
    red["ffn1_w_down"][1] = got[0]
    red["ffn1_w_gate"] = [g, None, None]
    g, got = _wgrad_call(db1, n1, "ffn1_up_wgrad", tile_y=WGRAD_TILE_Y,
                         side=_merge_sides([pair_side("ffn1_w_gate"), chip_side("ffn1_w_down")]))
    red["ffn1_w_gate"][1], red["ffn1_w_down"][2] = got
    red["ffn1_w_up"] = [g, None, None]
    red["ffn1_w_up"][1] = _run_side(pair_side("ffn1_w_up"), "pair_exchange_ffn1_w_up")[0]
    dn1, got = _row_matmul([(da1, full["ffn1_w_gate"]), (db1, full["ffn1_w_up"])], "ffn1_bwd_up", False, d,
                           _merge_sides([chip_side("ffn1_w_gate"), chip_side("ffn1_w_up")]), tiles=MM_TILES)
    red["ffn1_w_gate"][2], red["ffn1_w_up"][2] = got
    (dh0, dg_ffn1_pre), got = _pre_bwd(dn1, h0, dh1, ffn1_pre_g, "ffn1_pre_bwd",
                                       side=_join_side([_final_sum(*red[k], sel, "final_sum_" + k) for k in late]))
    gfull.update(zip(late, got))

    grad_x = dh0[N_META:t_real][None]

    w_big = {"ffn1_w_gate": ffn1_w_gate, "ffn1_w_up": ffn1_w_up, "ffn1_w_down": ffn1_w_down, "w_in": w_in, "w_out": w_out,
             "ffn2_w_gate": ffn2_w_gate, "ffn2_w_up": ffn2_w_up, "ffn2_w_down": ffn2_w_down}
    m_big = {"ffn1_w_gate": m_ffn1_w_gate, "ffn1_w_up": m_ffn1_w_up, "ffn1_w_down": m_ffn1_w_down, "w_in": m_w_in,
             "w_out": m_w_out, "ffn2_w_gate": m_ffn2_w_gate, "ffn2_w_up": m_ffn2_w_up, "ffn2_w_down": m_ffn2_w_down}
    v_big = {"ffn1_w_gate": v_ffn1_w_gate, "ffn1_w_up": v_ffn1_w_up, "ffn1_w_down": v_ffn1_w_down, "w_in": v_w_in,
             "w_out": v_w_out, "ffn2_w_gate": v_ffn2_w_gate, "ffn2_w_up": v_ffn2_w_up, "ffn2_w_down": v_ffn2_w_down}
    b_grad, b_delta, b_newm, b_newv = {}, {}, {}, {}

    def big_adamw(k, side=None):
        wv, mv, vv = view(k, w_big[k]), view(k, m_big[k]), view(k, v_big[k])
        wide_rows = wv.shape[0] % 64 == 0
        (g_, d_, m_, v_), got = _adamw(wv, gfull[k], mv, vv, "adamw_" + k, 8 if wide_rows else 4, 1 if wide_rows else 2,
                                       side)
        b_grad[k], b_delta[k], b_newm[k], b_newv[k] = unview(k, g_), unview(k, d_), unview(k, m_), unview(k, v_)
        return got

    p_top = _small_all_reduce(
        pack_top([dg_ffn1_pre, dg_ffn1_post, dg_mix_pre, dg_mix_post, dg_ffn2_pre, dg_ffn2_post], dh0[0:N_META],
                 loss=loss_part[0, 0]), "small_grad_all_reduce")
    p_g, p_delta, p_newm, p_newv = _adamw_small(p_w, p_top, p_rest4, p_m, p_v)
    loss = p_g[R_LOSS, 0]
    for k in names:
        big_adamw(k)

    def unpack(buf):
        out = {}
        for i, k in enumerate(["ffn1_pre_g", "ffn1_post_g", "mix_pre_g", "mix_post_g", "ffn2_pre_g", "ffn2_post_g"]):
            out[k] = buf[R_GAIN + i:R_GAIN + i + 1]
        out["meta_tokens"] = lax.dynamic_slice(buf[R_META:R_META + N_META], (zero, chip * dq), (N_META, dq))
        lru = buf[R_LRU:R_LRU + 16, 0:dl]
        out["lru_conv_w"] = lax.dynamic_slice(lru[0:4], (zero, chip * dlq), (4, dlq))[None]
        out["lru_conv_b"] = lru[4:5]
        out["lru_b_a"] = lru[5:6]
        out["lru_b_x"] = lru[6:7]
        out["lru_lambda"] = lru[7:8]
        out["lru_out_g"] = lru[8:9]
        sc = buf[R_SC:R_SC + 8, 0:dl]
        out["sconv_w"] = lax.dynamic_slice(sc[0:3], (zero, chip * dlq), (3, dlq))[None]
        out["sconv_out_g"] = sc[3:4]
        out["lru_w_a"] = _unpair_blocks(buf[R_WA:R_WX].reshape(N_HEADS // 2, LANE, LANE))[None]
        out["lru_w_x"] = _unpair_blocks(buf[R_WX:R_END].reshape(N_HEADS // 2, LANE, LANE))[None]
        return out

    s_grad, s_delta, s_newm, s_newv = unpack(p_g), unpack(p_delta), unpack(p_newm), unpack(p_newv)

    order = ["meta_tokens", "ffn1_pre_g", "ffn1_w_gate", "ffn1_w_up", "ffn1_w_down", "ffn1_post_g", "mix_pre_g", "w_in",
             "lru_conv_w", "lru_conv_b", "lru_w_a", "lru_b_a", "lru_w_x", "lru_b_x", "lru_lambda", "sconv_w", "lru_out_g",
             "sconv_out_g", "w_out", "mix_post_g", "ffn2_pre_g", "ffn2_w_gate", "ffn2_w_up", "ffn2_w_down", "ffn2_post_g"]

    def pick(small, bigd):
        return [bigd[k] if k in bigd else small[k] for k in order]

    return (loss, grad_x, *pick(s_grad, b_grad), *pick(s_delta, b_delta), *pick(s_newm, b_newm), *pick(s_newv, b_newv))
```

```python
import functools
import math

import jax
import jax.numpy as jnp
from jax import lax
from jax.experimental import pallas as pl
from jax.experimental.pallas import tpu as pltpu

F32 = jnp.float32
BF = jnp.bfloat16
MESH = pl.DeviceIdType.MESH

EPS = 1e-6
N_META = 16
N_HEADS = 16
HEAD = 64
LRU_C = 8.0
LANE = 128
MXU_COLS = 256
N_CHIP = 4
ROW_ALIGN = 384
MM_TILES = 8
MM_TILES_BIG = 4
EW_TILES = 12
MIX_CHUNKS = 24
WGRAD_TILE_X = 256
WGRAD_TILE_Y = 512
VMEM_LIMIT = 56 << 20

ADAM_LR = 0.001
ADAM_B1 = 0.9
ADAM_B2 = 0.999
ADAM_EPS = 1e-08
ADAM_WD = 0.01
ADAM_STEP = 10


def _round_up(a, b):
    return (a + b - 1) // b * b


def _params(sem=None):
    if sem is None:
        return pltpu.CompilerParams(vmem_limit_bytes=VMEM_LIMIT)
    return pltpu.CompilerParams(dimension_semantics=sem, vmem_limit_bytes=VMEM_LIMIT)


def _sigmoid(x):
    return 0.5 * jnp.tanh(0.5 * x) + 0.5


def _dot(a, b):
    return jnp.dot(a, b, preferred_element_type=F32)


def _dot_nt(a, b):
    return lax.dot_general(a, b, (((1,), (1,)), ((), ())), preferred_element_type=F32)


def _dot_tn(a, b):
    return lax.dot_general(a, b, (((0,), (0,)), ((), ())), preferred_element_type=F32)


def _rms(x, g):
    r = lax.rsqrt(jnp.mean(x * x, axis=-1, keepdims=True) + EPS)
    return x * r * g


def _rms_bwd(x, g, dy):
    r = lax.rsqrt(jnp.mean(x * x, axis=-1, keepdims=True) + EPS)
    xh = x * r
    q = dy * g
    dx = r * (q - xh * jnp.mean(q * xh, axis=-1, keepdims=True))
    return dx, dy * xh


class _Side:
    def __init__(self, ins, outs, alias, sems, start, finish):
        self.ins, self.outs, self.alias, self.sems, self.start, self.finish = ins, outs, alias, sems, start, finish


def _merge_sides(sides):
    sides = [s for s in sides if s is not None]
    if len(sides) <= 1:
        return sides[0] if sides else None
    ins, outs, sems, alias, spans = [], [], [], {}, []
    for s in sides:
        for i, o in s.alias.items():
            alias[len(ins) + i] = len(outs) + o
        spans.append((len(ins), len(ins) + len(s.ins), len(outs), len(outs) + len(s.outs), len(sems),
                      len(sems) + len(s.sems)))
        ins += list(s.ins)
        outs += list(s.outs)
        sems += list(s.sems)

    def run(which):
        def go(in_refs, out_refs, sem_refs):
            for s, (a, b, c, d, e, f) in zip(sides, spans):
                getattr(s, which)(in_refs[a:b], out_refs[c:d], sem_refs[e:f])
        return go

    return _Side(ins, outs, alias, sems, run("start"), run("finish"))


def _grid_call(body, name, grid, in_specs, out_specs, out_shape, args, side=None):
    sem = ("arbitrary",) * len(grid)
    if side is None:
        res = pl.pallas_call(body, name=name, grid=grid, in_specs=in_specs, out_specs=out_specs, out_shape=out_shape,
                             compiler_params=_params(sem))(*args)
        return res, []
    nin, nout, sin, sout = len(in_specs), len(out_specs), len(side.ins), len(side.outs)

    def full(*refs):
        base_in, side_in = refs[:nin], refs[nin:nin + sin]
        base_out = refs[nin + sin:nin + sin + nout]
        side_out = refs[nin + sin + nout:nin + sin + nout + sout]
        sems = refs[nin + sin + nout + sout:]
        first = pl.program_id(0) == 0
        last = pl.program_id(0) == grid[0] - 1
        for ax in range(1, len(grid)):
            first = first & (pl.program_id(ax) == 0)
            last = last & (pl.program_id(ax) == grid[ax] - 1)

        @pl.when(first)
        def _():
            side.start(side_in, side_out, sems)

        body(*base_in, *base_out)

        @pl.when(last)
        def _():
            side.finish(side_in, side_out, sems)

    any_spec = pl.BlockSpec(memory_space=pl.ANY)
    res = pl.pallas_call(
        full, name=name, grid=grid, in_specs=list(in_specs) + [any_spec] * sin,
        out_specs=list(out_specs) + [any_spec] * sout, out_shape=list(out_shape) + list(side.outs),
        scratch_shapes=list(side.sems), input_output_aliases={nin + i: nout + o for i, o in side.alias.items()},
        compiler_params=_params(sem))(*args, *side.ins)
    return res[:nout], res[nout:]


def _ffn_up(n, wg, wu, name, side=None, tiles=MM_TILES):
    tp, d = n.shape
    fp = wg.shape[1]
    tm = tp // tiles

    def body(n_ref, wg_ref, wu_ref, a_ref, b_ref, s_ref):
        nn = n_ref[...]
        for c0 in range(0, fp, MXU_COLS):
            cs = slice(c0, min(c0 + MXU_COLS, fp))
            a = _dot_nt(nn, wg_ref[cs, :])
            b = _dot_nt(nn, wu_ref[cs, :])
            a_ref[:, cs] = a.astype(BF)
            b_ref[:, cs] = b.astype(BF)
            s_ref[:, cs] = (a * _sigmoid(a) * b).astype(BF)

    out = jax.ShapeDtypeStruct((N_CHIP, tp, fp), BF)
    wspec = pl.BlockSpec((None, fp, d), lambda k, i: (k, 0, 0))
    ospec = pl.BlockSpec((None, tm, fp), lambda k, i: (k, i, 0))
    return _grid_call(body, name, (N_CHIP, tiles), [pl.BlockSpec((tm, d), lambda k, i: (i, 0)), wspec, wspec],
                      [ospec, ospec, ospec], [out, out, out], (n, wg, wu), side)


def _ffn_up_head(n, wg, wu, name, side):
    tp, d = n.shape
    fp = wg.shape[1]
    tiles = MM_TILES
    tm = tp // tiles
    gat = _gather_side([wg, wu], relative=True)
    sin, sout = len(side.ins), len(side.outs)
    order = (0,) + REL_SLOT

    def body(*refs):
        n_ref = refs[0]
        si = refs[3:3 + sin]
        a_ref, b_ref, s_ref = refs[3 + sin:6 + sin]
        go = refs[6 + sin:8 + sin]
        so = refs[8 + sin:8 + sin + sout]
        wbg, wbu, wsem = refs[8 + sin + sout:11 + sin + sout]
        gsems = refs[11 + sin + sout:15 + sin + sout]
        ssems = refs[15 + sin + sout:]
        k, i = pl.program_id(0), pl.program_id(1)
        cur = k % 2

        def to_vmem(slot, buf):
            return [pltpu.make_async_copy(go[0].at[slot], wbg.at[buf], wsem.at[buf, 0]),
                    pltpu.make_async_copy(go[1].at[slot], wbu.at[buf], wsem.at[buf, 1])]

        @pl.when((k == 0) & (i == 0))
        def _():
            gat.send(go, gsems)
            side.start(si, so, ssems)
            for cp in to_vmem(0, 0):
                cp.start()
            for cp in to_vmem(0, 0):
                cp.wait()

        for j in range(3):
            @pl.when((k == j) & (i == tiles // 2))
            def _():
                gat.arrived(j, go, gsems)

            @pl.when((k == j) & (i == tiles - 2))
            def _():
                gat.forwarded(j, go, gsems)
                for cp in to_vmem(order[j + 1], (j + 1) % 2):
                    cp.start()

            @pl.when((k == j + 1) & (i == 0))
            def _():
                for cp in to_vmem(order[j + 1], (j + 1) % 2):
                    cp.wait()

        nn = n_ref[...]
        for c0 in range(0, fp, MXU_COLS):
            cs = pl.ds(c0, min(MXU_COLS, fp - c0))
            a = _dot_nt(nn, wbg[cur, cs, :])
            b = _dot_nt(nn, wbu[cur, cs, :])
            a_ref[:, cs] = a.astype(BF)
            b_ref[:, cs] = b.astype(BF)
            s_ref[:, cs] = (a * _sigmoid(a) * b).astype(BF)

        @pl.when((k == N_CHIP - 1) & (i == tiles - 1))
        def _():
            gat.drain(go, gsems)
            side.finish(si, so, ssems)

    out = jax.ShapeDtypeStruct((N_CHIP, tp, fp), BF)
    any_spec = pl.BlockSpec(memory_space=pl.ANY)
    slot_of = lambda k: (k % 2) * 2 + k // 2
    ospec = pl.BlockSpec((None, tm, fp), lambda k, i: (slot_of(k), i, 0))
    wbuf = pltpu.VMEM((2, fp, d), BF)
    res = pl.pallas_call(
        body, name=name, grid=(N_CHIP, tiles),
        in_specs=[pl.BlockSpec((tm, d), lambda k, i: (i, 0))] + [any_spec] * (2 + sin),
        out_specs=[ospec, ospec, ospec] + [any_spec] * (2 + sout),
        out_shape=[out, out, out] + list(gat.outs) + list(side.outs),
        scratch_shapes=[wbuf, wbuf, pltpu.SemaphoreType.DMA((2, 2))] + list(gat.sems) + list(side.sems),
        input_output_aliases={1: 3, 2: 4, **{3 + a: 5 + b for a, b in side.alias.items()}},
        compiler_params=_params(("arbitrary", "arbitrary")))(n, wg, wu, *side.ins)
    return res[:3], res[3:5], res[5:]


def _ffn_bwd_act(df, wd, a, b, name, side=None, tiles=MM_TILES):
    tp, d = df.shape
    fp = wd.shape[1]
    tm = tp // tiles

    def body(df_ref, wd_ref, a_ref, b_ref, da_ref, db_ref):
        dfv = df_ref[...]
        for c0 in range(0, fp, MXU_COLS):
            cs = slice(c0, min(c0 + MXU_COLS, fp))
            ds = _dot_nt(dfv, wd_ref[cs, :])
            av = a_ref[:, cs].astype(F32)
            bv = b_ref[:, cs].astype(F32)
            sg = _sigmoid(av)
            da_ref[:, cs] = (ds * bv * sg * (1.0 + av * (1.0 - sg))).astype(BF)
            db_ref[:, cs] = (ds * av * sg).astype(BF)

    out = jax.ShapeDtypeStruct((N_CHIP, tp, fp), BF)
    aspec = pl.BlockSpec((None, tm, fp), lambda k, i: (k, i, 0))
    return _grid_call(
        body, name, (N_CHIP, tiles),
        [pl.BlockSpec((tm, d), lambda k, i: (i, 0)), pl.BlockSpec((None, fp, d), lambda k, i: (k, 0, 0)), aspec, aspec],
        [aspec, aspec], [out, out], (df, wd, a, b), side)


def _col_matmul(lhs, w, name, trans_b, out_dtype, side=None, tiles=MM_TILES_BIG):
    tp, kd = lhs.shape
    nk = w.shape[0]
    nc = w.shape[1] if trans_b else w.shape[2]
    tm = tp // tiles

    def body(l_ref, w_ref, o_ref):
        if trans_b:
            o_ref[...] = _dot_nt(l_ref[...], w_ref[...]).astype(out_dtype)
        else:
            o_ref[...] = _dot(l_ref[...], w_ref[...]).astype(out_dtype)

    res, extra = _grid_call(
        body, name, (nk, tiles),
        [pl.BlockSpec((tm, kd), lambda k, i: (i, 0)),
         pl.BlockSpec((None,) + tuple(w.shape[1:]), lambda k, i: (k, 0, 0), pipeline_mode=pl.Buffered(1))],
        [pl.BlockSpec((tm, nc), lambda k, i: (i, k))], [jax.ShapeDtypeStruct((tp, nk * nc), out_dtype)], (lhs, w), side)
    return res[0], extra


def _row_matmul(pairs, name, trans_b, d_out, side=None, tiles=MM_TILES_BIG):
    l0 = pairs[0][0]
    tp = l0.shape[1] if l0.ndim == 3 else l0.shape[0]
    nk = pairs[0][1].shape[0]
    tm = tp // tiles
    npair = len(pairs)

    def body(*refs):
        o_ref = refs[2 * npair]
        k = pl.program_id(1)
        part = None
        for q in range(npair):
            l = refs[2 * q][...]
            w = refs[2 * q + 1][...]
            t = _dot_nt(l, w) if trans_b else _dot(l, w)
            part = t if part is None else part + t

        @pl.when(k == 0)
        def _():
            o_ref[...] = part

        @pl.when(k > 0)
        def _():
            o_ref[...] += part

    in_specs, args = [], []
    for lhs, w in pairs:
        if lhs.ndim == 3:
            in_specs.append(pl.BlockSpec((None, tm, lhs.shape[2]), lambda i, k: (k, i, 0)))
        else:
            in_specs.append(pl.BlockSpec((tm, lhs.shape[1] // nk), lambda i, k: (i, k)))
        in_specs.append(pl.BlockSpec((None,) + tuple(w.shape[1:]), lambda i, k: (k, 0, 0)))
        args += [lhs, w]
    res, extra = _grid_call(body, name, (tiles, nk), in_specs, [pl.BlockSpec((tm, d_out), lambda i, k: (i, 0))],
                            [jax.ShapeDtypeStruct((tp, d_out), F32)], args, side)
    return res[0], extra


def _wgrad_call(x, y, name, x_width=None, y_width=None, tile_x=None, tile_y=None, side=None):
    tp = x.shape[1] if x.ndim == 3 else x.shape[0]

    def spec(a, width, tile):
        cols = a.shape[2] if a.ndim == 3 else (a.shape[1] if width is None else width)
        tc = cols if tile is None else tile
        per = cols // tc
        if a.ndim == 3:
            return pl.BlockSpec((None, tp, tc), lambda k, t: (k, 0, t if tile else 0)), cols, per
        if width is None:
            return pl.BlockSpec((tp, tc), lambda k, t: (0, t if tile else 0)), cols, per
        return pl.BlockSpec((tp, tc), lambda k, t: (0, k * per + (t if tile else 0))), cols, per

    xs, p, nx = spec(x, x_width, tile_x)
    ys, q, ny = spec(y, y_width, tile_y)
    nt = nx * ny
    if tile_x:
        ospec = pl.BlockSpec((None, tile_x, q), lambda k, t: (k, t, 0))
    else:
        ospec = pl.BlockSpec((None, p, tile_y), lambda k, t: (k, 0, t))

    def body(x_ref, y_ref, o_ref):
        o_ref[...] = _dot_tn(x_ref[...], y_ref[...]).astype(BF)

    res, extra = _grid_call(body, name, (N_CHIP, nt), [xs, ys], [ospec], [jax.ShapeDtypeStruct((N_CHIP, p, q), BF)],
                            (x, y), side)
    return res[0], extra


def _row_call(body, name, tp, d, row_ins, vec_ins, row_out_dtypes, n_acc, side=None):
    te = tp // EW_TILES
    rspec = pl.BlockSpec((te, d), lambda i: (i, 0))
    vspec = pl.BlockSpec((1, d), lambda i: (0, 0))
    res, extra = _grid_call(
        body, name, (EW_TILES,), [rspec] * len(row_ins) + [vspec] * len(vec_ins),
        [rspec] * len(row_out_dtypes) + [vspec] * n_acc,
        [jax.ShapeDtypeStruct((tp, d), dt) for dt in row_out_dtypes] + [jax.ShapeDtypeStruct((1, d), F32)] * n_acc,
        (*row_ins, *vec_ins), side)
    return res if side is None else (res, extra)


def _norm0(h, g):
    tp, d = h.shape

    def body(h_ref, g_ref, n_ref):
        n_ref[...] = _rms(h_ref[...], g_ref[...]).astype(BF)

    return _row_call(body, "norm0", tp, d, [h], [g], [BF], 0)[0]


def _post_fwd(f, h, g_post, g_next, scale, name):
    tp, d = h.shape

    def body(f_ref, h_ref, gp_ref, gn_ref, hn_ref, n_ref):
        hn = h_ref[...] + scale * _rms(f_ref[...], gp_ref[...])
        hn_ref[...] = hn
        n_ref[...] = _rms(hn, gn_ref[...]).astype(BF)

    return _row_call(body, name, tp, d, [f, h], [g_post, g_next], [F32, BF], 0)


def _loss_bwd(f, h, tgt, g_post, t_real):
    tp, d = h.shape
    te = tp // EW_TILES

    def body(f_ref, h_ref, t_ref, gp_ref, dh_ref, df_ref, dg_ref, loss_ref):
        i = pl.program_id(0)

        @pl.when(i == 0)
        def _():
            dg_ref[...] = jnp.zeros_like(dg_ref)
            loss_ref[...] = jnp.zeros_like(loss_ref)

        f = f_ref[...]
        gp = gp_ref[...]
        h3 = h_ref[...] + 0.5 * _rms(f, gp)
        rows = i * te + lax.broadcasted_iota(jnp.int32, (te, 1), 0)
        real = (rows >= N_META) & (rows < t_real)
        e = jnp.where(real, h3 - t_ref[...], 0.0)
        loss_ref[...] += 0.5 * jnp.sum(jnp.sum(e * e, axis=1, keepdims=True), axis=0, keepdims=True) / d
        dh = e / d
        dh_ref[...] = dh
        dfv, dgr = _rms_bwd(f, gp, 0.5 * dh)
        df_ref[...] = dfv.astype(BF)
        dg_ref[...] += jnp.sum(dgr, axis=0, keepdims=True)

    rspec = pl.BlockSpec((te, d), lambda i: (i, 0))
    vspec = pl.BlockSpec((1, d), lambda i: (0, 0))
    return pl.pallas_call(
        body, name="loss_bwd", grid=(EW_TILES,),
        in_specs=[rspec, rspec, rspec, vspec],
        out_specs=[rspec, rspec, vspec, pl.BlockSpec((1, 1), lambda i: (0, 0))],
        out_shape=[jax.ShapeDtypeStruct((tp, d), F32), jax.ShapeDtypeStruct((tp, d), BF),
                   jax.ShapeDtypeStruct((1, d), F32), jax.ShapeDtypeStruct((1, 1), F32)],
        compiler_params=_params(("arbitrary",)),
    )(f, h, tgt, g_post)


def _pre_bwd(dn, h, dh_out, g_pre, name, chain=None, side=None):
    tp, d = h.shape

    def body(*refs):
        if chain is None:
            dn_ref, h_ref, dho_ref, g_ref, dh_ref, dg_ref = refs
        else:
            dn_ref, h_ref, dho_ref, p_ref, g_ref, gp_ref, dh_ref, dp_ref, dg_ref, dgp_ref = refs
        i = pl.program_id(0)

        @pl.when(i == 0)
        def _():
            dg_ref[...] = jnp.zeros_like(dg_ref)
            if chain is not None:
                dgp_ref[...] = jnp.zeros_like(dgp_ref)

        dx, dgr = _rms_bwd(h_ref[...], g_ref[...], dn_ref[...])
        dh = dho_ref[...] + dx
        dh_ref[...] = dh
        dg_ref[...] += jnp.sum(dgr, axis=0, keepdims=True)
        if chain is not None:
            dp, dgpr = _rms_bwd(p_ref[...], gp_ref[...], chain[2] * dh)
            dp_ref[...] = dp.astype(BF)
            dgp_ref[...] += jnp.sum(dgpr, axis=0, keepdims=True)

    if chain is None:
        return _row_call(body, name, tp, d, [dn, h, dh_out], [g_pre], [F32], 1, side)
    return _row_call(body, name, tp, d, [dn, h, dh_out, chain[0]], [g_pre, chain[1]], [F32, BF], 2, side)


def _gelu(y):
    c = math.sqrt(2.0 / math.pi)
    return 0.5 * y * (1.0 + jnp.tanh(c * (y + 0.044715 * y * y * y)))


def _gelu_grad(y):
    c = math.sqrt(2.0 / math.pi)
    t = jnp.tanh(c * (y + 0.044715 * y * y * y))
    return 0.5 * (1.0 + t) + 0.5 * y * (1.0 - t * t) * c * (1.0 + 3.0 * 0.044715 * y * y)


def _neg_expm1(x):
    p = 1.0 + x * (1.0 / 9.0)
    for n in (8.0, 7.0, 6.0, 5.0, 4.0, 3.0, 2.0):
        p = 1.0 + x * (1.0 / n) * p
    return -jnp.where(x > -0.35, x * p, jnp.exp(x) - 1.0)


def _softplus(x):
    e = jnp.exp(-jnp.abs(x))
    w = 1.0 + e
    l1p = jnp.where(w == 1.0, e, jnp.log(w) * (e / jnp.where(w == 1.0, 1.0, w - 1.0)))
    return jnp.maximum(x, 0.0) + l1p


def _group_mean(v, gm):
    hi = v.astype(BF)
    lo = (v - hi.astype(F32)).astype(BF)
    return _dot(hi, gm) + _dot(lo, gm)


def _shift_dn(win, s, r):
    if s == 0:
        return win[8:8 + r]
    return pltpu.roll(win, s, 0)[8:8 + r]


def _shift_up(win, s, r):
    if s == 0:
        return win[0:r]
    return pltpu.roll(win, r + 8 - s, 0)[0:r]


def _window_dn(ref, t0, r, first):
    if first:
        return jnp.concatenate([jnp.zeros((8, ref.shape[1]), F32), ref[0:r, :]], axis=0)
    return ref[pl.ds(t0 - 8, r + 8), :]


def _tile_scan(a, u, reverse):
    r = a.shape[0]
    rid = lax.broadcasted_iota(jnp.int32, a.shape, 0) & 7
    for dlt in (1, 2, 4):
        sh = (r - dlt) if reverse else dlt
        a_s = pltpu.roll(a, sh, 0)
        u_s = pltpu.roll(u, sh, 0)
        keep = (rid + dlt <= 7) if reverse else (rid >= dlt)
        u = jnp.where(keep, u + a * u_s, u)
        a = jnp.where(keep, a * a_s, a)
    return a, u


def _lru_gates(xc, wa, ba, wx, bx, sp):
    xb = xc.astype(BF)
    ga = _sigmoid(_dot(xb, wa) + ba)
    gx = _sigmoid(_dot(xb, wx) + bx)
    la = -LRU_C * ga * sp
    return ga, gx, la


def _conv4(win, w4, cb, r):
    return (cb + w4[3:4] * _shift_dn(win, 0, r) + w4[2:3] * _shift_dn(win, 1, r)
            + w4[1:2] * _shift_dn(win, 2, r) + w4[0:1] * _shift_dn(win, 3, r))


def _lru_fwd(z, w4, cb, wa2, ba, wx2, bx, lam, g_out, gm, side=None):
    tp = z.shape[0]
    dl = cb.shape[1]
    nb = dl // LANE
    r = tp // MIX_CHUNKS
    c = LANE

    def body(y_ref, x_ref, w4_ref, cb_ref, wa_ref, ba_ref, wx_ref, bx_ref, lam_ref, go_ref, gm_ref, m_ref, hs_ref):
        w4v = w4_ref[...]
        cbv = cb_ref[...]
        wa = wa_ref[...]
        wx = wx_ref[...]
        bav = ba_ref[...]
        bxv = bx_ref[...]
        gov = go_ref[...]
        gmv = gm_ref[...]
        sp = _softplus(-lam_ref[...])

        def chunk(t0, hprev, first):
            win = _window_dn(x_ref, t0, r, first)
            xc = _conv4(win, w4v, cbv, r)
            ga, gx, la = _lru_gates(xc, wa, bav, wx, bxv, sp)
            a = jnp.exp(la)
            u = jnp.sqrt(_neg_expm1(2.0 * la)) * gx * xc
            ac, uc = _tile_scan(a, u, False)
            for j in range(r // 8):
                hj = uc[8 * j:8 * j + 8] + ac[8 * j:8 * j + 8] * hprev
                hs_ref[pl.ds(t0 + 8 * j, 8), :] = hj
                hprev = jnp.broadcast_to(hj[7:8], (8, c))
            h = hs_ref[pl.ds(t0, r), :]
            lo = h * _gelu(y_ref[pl.ds(t0, r), :])
            rs = lax.rsqrt(_group_mean(lo * lo, gmv) + EPS)
            m_ref[pl.ds(t0, r), :] = (lo * rs * gov).astype(BF)
            return hprev

        hp = chunk(0, jnp.zeros((8, c), F32), True)

        def loop(ci, hp):
            return chunk(pl.multiple_of(ci * r, 16), hp, False)

        lax.fori_loop(1, MIX_CHUNKS, loop, hp)

    col = lambda off: pl.BlockSpec((tp, c), lambda j: (0, off + j))
    vec = pl.BlockSpec((1, c), lambda j: (0, j))
    return _grid_call(
        body, "lru_fwd", (nb,),
        [col(0), col(nb), pl.BlockSpec((8, c), lambda j: (0, j)), vec, pl.BlockSpec((None, c, c), lambda j: (j, 0, 0)),
         vec, pl.BlockSpec((None, c, c), lambda j: (j, 0, 0)), vec, vec, vec, pl.BlockSpec((c, c), lambda j: (0, 0))],
        [col(0), col(0)], [jax.ShapeDtypeStruct((tp, dl), BF), jax.ShapeDtypeStruct((tp, dl), F32)],
        (z, z, w4, cb, wa2, ba, wx2, bx, lam, g_out, gm), side)


def _lru_bwd(z, hs, dmix, w4, cb, wa2, ba, wx2, bx, lam, g_out, gm):
    tp = z.shape[0]
    dl = cb.shape[1]
    nb = dl // LANE
    r = tp // MIX_CHUNKS
    c = LANE

    def body(y_ref, x_ref, hs_ref, dm_ref, w4_ref, cb_ref, wa_ref, ba_ref, wx_ref, bx_ref, lam_ref, go_ref, gm_ref,
             dy_ref, dx_ref, small_ref, dwa_ref, dwx_ref, xc_buf, ga_buf, gx_buf, a_buf, dh_buf, dxc_buf):
        w4v = w4_ref[...]
        cbv = cb_ref[...]
        wa = wa_ref[...]
        wx = wx_ref[...]
        bav = ba_ref[...]
        bxv = bx_ref[...]
        gov = go_ref[...]
        gmv = gm_ref[...]
        lamv = lam_ref[...]
        sp = _softplus(-lamv)
        small_ref[...] = jnp.zeros_like(small_ref)
        dwa_ref[...] = jnp.zeros_like(dwa_ref)
        dwx_ref[...] = jnp.zeros_like(dwx_ref)
        a_buf[pl.ds(tp, 8), :] = jnp.zeros((8, c), F32)
        dxc_buf[pl.ds(tp, 8), :] = jnp.zeros((8, c), F32)

        def fwd_chunk(t0, first):
            win = _window_dn(x_ref, t0, r, first)
            xc = _conv4(win, w4v, cbv, r)
            ga, gx, la = _lru_gates(xc, wa, bav, wx, bxv, sp)
            xc_buf[pl.ds(t0, r), :] = xc
            ga_buf[pl.ds(t0, r), :] = ga
            gx_buf[pl.ds(t0, r), :] = gx
            a_buf[pl.ds(t0, r), :] = jnp.exp(la)
            h = hs_ref[pl.ds(t0, r), :]
            yv = y_ref[pl.ds(t0, r), :]
            ge = _gelu(yv)
            lo = h * ge
            rs = lax.rsqrt(_group_mean(lo * lo, gmv) + EPS)
            xh = lo * rs
            dm = dm_ref[pl.ds(t0, r), :]
            q = dm * gov
            dlo = rs * (q - xh * _group_mean(q * xh, gmv))
            small_ref[8:9, :] += jnp.sum(dm * xh, axis=0, keepdims=True)
            dh_buf[pl.ds(t0, r), :] = dlo * ge
            dy_ref[pl.ds(t0, r), :] = (dlo * h * _gelu_grad(yv)).astype(BF)

        fwd_chunk(0, True)

        def floop(ci, carry):
            fwd_chunk(pl.multiple_of(ci * r, 16), False)
            return carry

        lax.fori_loop(1, MIX_CHUNKS, floop, 0)

        def bwd_chunk(t0, vnext, first):
            ap = _shift_up(a_buf[pl.ds(t0, r + 8), :], 1, r)
            ac, uc = _tile_scan(ap, dh_buf[pl.ds(t0, r), :], True)
            for j in reversed(range(r // 8)):
                vj = uc[8 * j:8 * j + 8] + ac[8 * j:8 * j + 8] * vnext
                dh_buf[pl.ds(t0 + 8 * j, 8), :] = vj
                vnext = jnp.broadcast_to(vj[0:1], (8, c))
            v = dh_buf[pl.ds(t0, r), :]
            hprev = _shift_dn(_window_dn(hs_ref, t0, r, first), 1, r)
            xc = xc_buf[pl.ds(t0, r), :]
            ga = ga_buf[pl.ds(t0, r), :]
            gx = gx_buf[pl.ds(t0, r), :]
            a = a_buf[pl.ds(t0, r), :]
            em = _neg_expm1(-2.0 * LRU_C * ga * sp)
            mult = jnp.sqrt(em)
            dla = v * hprev * a - (v * gx * xc) * ((1.0 - em) / mult)
            dgx = v * mult * xc
            dxc = v * mult * gx
            dga = dla * (-LRU_C) * sp
            small_ref[7:8, :] += jnp.sum(dla * (-LRU_C) * ga, axis=0, keepdims=True)
            dpa = dga * ga * (1.0 - ga)
            dpx = dgx * gx * (1.0 - gx)
            small_ref[5:6, :] += jnp.sum(dpa, axis=0, keepdims=True)
            small_ref[6:7, :] += jnp.sum(dpx, axis=0, keepdims=True)
            dpab = dpa.astype(BF)
            dpxb = dpx.astype(BF)
            xb = xc.astype(BF)
            dxc = dxc + _dot_nt(dpab, wa) + _dot_nt(dpxb, wx)
            dwa_ref[...] += _dot_tn(xb, dpab)
            dwx_ref[...] += _dot_tn(xb, dpxb)
            dxc_buf[pl.ds(t0, r), :] = dxc
            small_ref[4:5, :] += jnp.sum(dxc, axis=0, keepdims=True)
            dwin = dxc_buf[pl.ds(t0, r + 8), :]
            dx_ref[pl.ds(t0, r), :] = (w4v[3:4] * dxc + w4v[2:3] * _shift_up(dwin, 1, r)
                                       + w4v[1:2] * _shift_up(dwin, 2, r) + w4v[0:1] * _shift_up(dwin, 3, r)).astype(BF)
            xwin = _window_dn(x_ref, t0, r, first)
            for k in range(4):
                small_ref[k:k + 1, :] += jnp.sum(dxc * _shift_dn(xwin, 3 - k, r), axis=0, keepdims=True)
            return vnext

        def bloop(it, vnext):
            ci = MIX_CHUNKS - 1 - it
            return bwd_chunk(pl.multiple_of(ci * r, 16), vnext, False)

        vn = lax.fori_loop(0, MIX_CHUNKS - 1, bloop, jnp.zeros((8, c), F32))
        bwd_chunk(0, vn, True)
        small_ref[7:8, :] = small_ref[7:8, :] * (-_sigmoid(-lamv))

    col = lambda off: pl.BlockSpec((tp, c), lambda j: (0, off + j))
    vec = pl.BlockSpec((1, c), lambda j: (0, j))
    mat = pl.BlockSpec((None, c, c), lambda j: (j, 0, 0))
    buf = pltpu.VMEM((tp, c), F32)
    bufp = pltpu.VMEM((tp + 8, c), F32)
    return pl.pallas_call(
        body, name="lru_bwd", grid=(nb,),
        in_specs=[col(0), col(nb), col(0), col(0), pl.BlockSpec((8, c), lambda j: (0, j)), vec, mat, vec, mat, vec,
                  vec, vec, pl.BlockSpec((c, c), lambda j: (0, 0))],
        out_specs=[col(0), col(0), pl.BlockSpec((16, c), lambda j: (0, j)), mat, mat],
        out_shape=[jax.ShapeDtypeStruct((tp, dl), BF), jax.ShapeDtypeStruct((tp, dl), BF),
                   jax.ShapeDtypeStruct((16, dl), F32), jax.ShapeDtypeStruct((nb, c, c), F32),
                   jax.ShapeDtypeStruct((nb, c, c), F32)],
        scratch_shapes=[buf, buf, buf, bufp, buf, bufp],
        compiler_params=_params(("arbitrary",)),
    )(z, z, hs, dmix, w4, cb, wa2, ba, wx2, bx, lam, g_out, gm)


def _sc_conv(cvwin, w3, r):
    return w3[2:3] * _shift_dn(cvwin, 0, r) + w3[1:2] * _shift_dn(cvwin, 1, r) + w3[0:1] * _shift_dn(cvwin, 2, r)


def _sc_fwd(z, w3, g_out, gm, dl):
    tp = z.shape[0]
    nb = dl // LANE
    r = tp // MIX_CHUNKS
    c = LANE

    def body(b_ref, c_ref, v_ref, w3_ref, go_ref, gm_ref, m_ref):
        w3v = w3_ref[...]
        gov = go_ref[...]
        gmv = gm_ref[...]

        def chunk(t0, first):
            cvwin = _window_dn(c_ref, t0, r, first) * _window_dn(v_ref, t0, r, first)
            so = b_ref[pl.ds(t0, r), :] * _sc_conv(cvwin, w3v, r)
            rs = lax.rsqrt(_group_mean(so * so, gmv) + EPS)
            m_ref[pl.ds(t0, r), :] = (so * rs * gov).astype(BF)

        chunk(0, True)

        def loop(ci, carry):
            chunk(pl.multiple_of(ci * r, 16), False)
            return carry

        lax.fori_loop(1, MIX_CHUNKS, loop, 0)

    col = lambda off: pl.BlockSpec((tp, c), lambda j: (0, off + j))
    return pl.pallas_call(
        body, name="sconv_fwd", grid=(nb,),
        in_specs=[col(2 * nb), col(3 * nb), col(4 * nb), pl.BlockSpec((8, c), lambda j: (0, j)),
                  pl.BlockSpec((1, c), lambda j: (0, j)), pl.BlockSpec((c, c), lambda j: (0, 0))],
        out_specs=col(0), out_shape=jax.ShapeDtypeStruct((tp, dl), BF),
        compiler_params=_params(("arbitrary",)),
    )(z, z, z, w3, g_out, gm)


def _sc_bwd(z, dmix, w3, g_out, gm, dl):
    tp = z.shape[0]
    nb = dl // LANE
    r = tp // MIX_CHUNKS
    c = LANE

    def body(b_ref, c_ref, v_ref, dm_ref, w3_ref, go_ref, gm_ref, db_ref, dc_ref, dv_ref, small_ref, dsc_buf):
        w3v = w3_ref[...]
        gov = go_ref[...]
        gmv = gm_ref[...]
        small_ref[...] = jnp.zeros_like(small_ref)
        dsc_buf[pl.ds(tp, 8), :] = jnp.zeros((8, c), F32)

        def chunk1(t0, first):
            cvwin = _window_dn(c_ref, t0, r, first) * _window_dn(v_ref, t0, r, first)
            sc = _sc_conv(cvwin, w3v, r)
            bv = b_ref[pl.ds(t0, r), :]
            so = bv * sc
            rs = lax.rsqrt(_group_mean(so * so, gmv) + EPS)
            xh = so * rs
            dm = dm_ref[pl.ds(t0, r), :]
            q = dm * gov
            dso = rs * (q - xh * _group_mean(q * xh, gmv))
            small_ref[3:4, :] += jnp.sum(dm * xh, axis=0, keepdims=True)
            db_ref[pl.ds(t0, r), :] = (dso * sc).astype(BF)
            dsc = dso * bv
            dsc_buf[pl.ds(t0, r), :] = dsc
            for k in range(3):
                small_ref[k:k + 1, :] += jnp.sum(dsc * _shift_dn(cvwin, 2 - k, r), axis=0, keepdims=True)

        chunk1(0, True)

        def loop1(ci, carry):
            chunk1(pl.multiple_of(ci * r, 16), False)
            return carry

        lax.fori_loop(1, MIX_CHUNKS, loop1, 0)

        def loop2(ci, carry):
            t0 = pl.multiple_of(ci * r, 16)
            dwin = dsc_buf[pl.ds(t0, r + 8), :]
            dcv = w3v[2:3] * _shift_up(dwin, 0, r) + w3v[1:2] * _shift_up(dwin, 1, r) + w3v[0:1] * _shift_up(dwin, 2, r)
            dc_ref[pl.ds(t0, r), :] = (dcv * v_ref[pl.ds(t0, r), :]).astype(BF)
            dv_ref[pl.ds(t0, r), :] = (dcv * c_ref[pl.ds(t0, r), :]).astype(BF)
            return carry

        lax.fori_loop(0, MIX_CHUNKS, loop2, 0)

    col = lambda off: pl.BlockSpec((tp, c), lambda j: (0, off + j))
    out = jax.ShapeDtypeStruct((tp, dl), BF)
    return pl.pallas_call(
        body, name="sconv_bwd", grid=(nb,),
        in_specs=[col(2 * nb), col(3 * nb), col(4 * nb), col(nb), pl.BlockSpec((8, c), lambda j: (0, j)),
                  pl.BlockSpec((1, c), lambda j: (0, j)), pl.BlockSpec((c, c), lambda j: (0, 0))],
        out_specs=[col(0), col(0), col(0), pl.BlockSpec((8, c), lambda j: (0, j))],
        out_shape=[out, out, out, jax.ShapeDtypeStruct((8, dl), F32)],
        scratch_shapes=[pltpu.VMEM((tp + 8, c), F32)],
        compiler_params=_params(("arbitrary",)),
    )(z, z, z, dmix, w3, g_out, gm)


def _cast_pad(w, rows_p, cols_p, chip, name):
    r, c = w.shape

    def body(chip_ref, w_ref, o_ref):
        if (rows_p, cols_p) != (r, c):
            o_ref[...] = jnp.zeros_like(o_ref)
        o_ref[0:r, 0:c] = w_ref[...].astype(BF)

    return pl.pallas_call(
        body, name=name, out_shape=jax.ShapeDtypeStruct((N_CHIP, rows_p, cols_p), BF),
        grid_spec=pltpu.PrefetchScalarGridSpec(
            num_scalar_prefetch=1, grid=(1,),
            in_specs=[pl.BlockSpec((r, c), lambda i, chip: (0, 0))],
            out_specs=pl.BlockSpec((None, rows_p, cols_p), lambda i, chip: (chip[0], 0, 0))),
        compiler_params=_params(("arbitrary",)),
    )(chip, w)


def _adamw_math(w, g, m, v):
    m2 = ADAM_B1 * m + (1.0 - ADAM_B1) * g
    v2 = ADAM_B2 * v + (1.0 - ADAM_B2) * (g * g)
    m_hat = m2 / (1.0 - ADAM_B1 ** ADAM_STEP)
    v_hat = v2 / (1.0 - ADAM_B2 ** ADAM_STEP)
    delta = -ADAM_LR * (m_hat / (jnp.sqrt(v_hat) + ADAM_EPS) + ADAM_WD * w)
    return delta, m2, v2


def _adamw(w, g, m, v, name, row_tiles, col_tiles, side=None):
    r, c = w.shape
    tr = r // row_tiles
    tc = c // col_tiles
    gc = g.shape[1] if col_tiles == 1 else tc

    def body(w_ref, g_ref, m_ref, v_ref, go_ref, d_ref, mo_ref, vo_ref):
        gv = g_ref[...][:, 0:tc]
        delta, m2, v2 = _adamw_math(w_ref[...], gv, m_ref[...], v_ref[...])
        go_ref[...] = gv
        d_ref[...] = delta
        mo_ref[...] = m2
        vo_ref[...] = v2

    spec = pl.BlockSpec((tr, tc), lambda i, j: (i, j))
    out = jax.ShapeDtypeStruct((r, c), F32)
    return _grid_call(body, name, (row_tiles, col_tiles), [spec, pl.BlockSpec((tr, gc), lambda i, j: (i, j)), spec, spec],
                      [spec] * 4, [out] * 4, (w, g, m, v), side)


def _adamw_small(w, g_top, g4, m, v):
    def body(w_ref, gt_ref, g_ref, m_ref, v_ref, go_ref, d_ref, mo_ref, vo_ref):
        g = jnp.concatenate([gt_ref[...], (g_ref[0] + g_ref[1]) + (g_ref[2] + g_ref[3])], axis=0)
        delta, m2, v2 = _adamw_math(w_ref[...], g, m_ref[...], v_ref[...])
        go_ref[...] = g
        d_ref[...] = delta
        mo_ref[...] = m2
        vo_ref[...] = v2

    out = jax.ShapeDtypeStruct(w.shape, F32)
    spec = pl.BlockSpec(w.shape, lambda: (0, 0))
    return pl.pallas_call(
        body, name="adamw_small",
        in_specs=[spec, pl.BlockSpec(g_top.shape, lambda: (0, 0)), pl.BlockSpec(g4.shape, lambda: (0, 0, 0)), spec, spec],
        out_specs=[spec] * 4, out_shape=[out] * 4, compiler_params=_params())(w, g_top, g4, m, v)


def _place():
    x, y, c = lax.axis_index("x"), lax.axis_index("y"), lax.axis_index("c")
    chips = [(1 - x, y), (x, 1 - y), (1 - x, 1 - y)]
    return x, y, c, chips


ANY = pl.BlockSpec(memory_space=pl.ANY)


REL_SLOT = (2, 1, 3)


def _gather_side(bufs, relative=False):
    n = len(bufs)

    def copies(outs, sems):
        s_ici, r_ici, s_d2d, r_d2d = sems
        x, y, c, chips = _place()
        me = 2 * x + y

        def rows(w, slot, core):
            half = bufs[w].shape[1] // 2
            return outs[w].at[slot, pl.ds(core * half, half)]

        def theirs(j):
            return REL_SLOT[j] if relative else 2 * chips[j][0] + chips[j][1]

        def ici_send(w, j):
            px, py = chips[j]
            return pltpu.make_async_remote_copy(
                src_ref=rows(w, 0 if relative else me, c), dst_ref=rows(w, REL_SLOT[j] if relative else me, c),
                send_sem=s_ici.at[w, j], recv_sem=r_ici.at[w, j], device_id=(px, py, c), device_id_type=MESH)

        def ici_recv(w, j):
            px, py = chips[j]
            return pltpu.make_async_remote_copy(
                src_ref=rows(w, theirs(j), c), dst_ref=rows(w, theirs(j), c),
                send_sem=s_ici.at[w, j], recv_sem=r_ici.at[w, j], device_id=(px, py, c), device_id_type=MESH)

        def d2d(w, j, core):
            return pltpu.make_async_remote_copy(
                src_ref=rows(w, theirs(j), core), dst_ref=rows(w, theirs(j), core),
                send_sem=s_d2d.at[w, j], recv_sem=r_d2d.at[w, j], device_id=(x, y, 1 - c), device_id_type=MESH)

        return c, ici_send, ici_recv, d2d

    def send(outs, sems):
        c, ici_send, ici_recv, d2d = copies(outs, sems)
        for j in range(3):
            for w in range(n):
                ici_send(w, j).start()

    def arrived(j, outs, sems):
        c, ici_send, ici_recv, d2d = copies(outs, sems)
        for w in range(n):
            ici_recv(w, j).wait_recv()
            d2d(w, j, c).start()

    def forwarded(j, outs, sems):
        c, ici_send, ici_recv, d2d = copies(outs, sems)
        for w in range(n):
            d2d(w, j, 1 - c).wait_recv()

    def drain(outs, sems):
        c, ici_send, ici_recv, d2d = copies(outs, sems)
        for j in range(3):
            for w in range(n):
                ici_send(w, j).wait_send()
                d2d(w, j, c).wait_send()

    def start(ins, outs, sems):
        send(outs, sems)

    def finish(ins, outs, sems):
        for j in range(3):
            arrived(j, outs, sems)
        for j in range(3):
            forwarded(j, outs, sems)
        drain(outs, sems)

    dma = pltpu.SemaphoreType.DMA((n, 3))
    side = _Side(list(bufs), [jax.ShapeDtypeStruct(b.shape, b.dtype) for b in bufs], {w: w for w in range(n)},
                 [dma, dma, dma, dma], start, finish)
    side.send, side.arrived, side.forwarded, side.drain = send, arrived, forwarded, drain
    return side


def _run_side(side, name):
    sin, sout = len(side.ins), len(side.outs)

    def body(*refs):
        ins, outs, sems = refs[:sin], refs[sin:sin + sout], refs[sin + sout:]
        side.start(ins, outs, sems)
        side.finish(ins, outs, sems)

    return pl.pallas_call(
        body, name=name, out_shape=list(side.outs), in_specs=[ANY] * sin, out_specs=[ANY] * sout,
        scratch_shapes=list(side.sems), input_output_aliases=dict(side.alias))(*side.ins)


def _pair_exchange_side(grads):
    n = len(grads)

    def copies(ins, outs, sems):
        ssem, rsem = sems
        x, y, c, _ = _place()
        cps = []
        for w in range(n):
            half = grads[w].shape[1] // 2
            cps.append(pltpu.make_async_remote_copy(
                src_ref=ins[w].at[:, pl.ds((1 - c) * half, half)], dst_ref=outs[w],
                send_sem=ssem.at[w], recv_sem=rsem.at[w], device_id=(x, y, 1 - c), device_id_type=MESH))
        return cps

    def start(ins, outs, sems):
        for cp in copies(ins, outs, sems):
            cp.start()

    def finish(ins, outs, sems):
        for cp in copies(ins, outs, sems):
            cp.wait()

    dma = pltpu.SemaphoreType.DMA((n,))
    return _Side(list(grads), [jax.ShapeDtypeStruct((N_CHIP, g.shape[1] // 2, g.shape[2]), BF) for g in grads], {},
                 [dma, dma], start, finish)


def _sibling_copy_side(buf):
    def copy(ins, outs, sems):
        x, y, c, _ = _place()
        return pltpu.make_async_remote_copy(src_ref=ins[0], dst_ref=outs[0], send_sem=sems[0], recv_sem=sems[1],
                                            device_id=(x, y, 1 - c), device_id_type=MESH)

    return _Side([buf], [jax.ShapeDtypeStruct(buf.shape, buf.dtype)], {}, [pltpu.SemaphoreType.DMA, pltpu.SemaphoreType.DMA],
                 lambda i, o, s: copy(i, o, s).start(), lambda i, o, s: copy(i, o, s).wait())


def _slot_exchange_side(buf4):
    def copies(outs, sems, sending):
        ssem, rsem = sems
        x, y, c, chips = _place()
        me = 2 * x + y
        return [pltpu.make_async_remote_copy(
            src_ref=outs[0].at[me if sending else 2 * px + py], dst_ref=outs[0].at[me if sending else 2 * px + py],
            send_sem=ssem.at[j], recv_sem=rsem.at[j], device_id=(px, py, c), device_id_type=MESH)
            for j, (px, py) in enumerate(chips)]

    def start(ins, outs, sems):
        for cp in copies(outs, sems, True):
            cp.start()

    def finish(ins, outs, sems):
        for cp in copies(outs, sems, False):
            cp.wait_recv()
        for cp in copies(outs, sems, True):
            cp.wait_send()

    dma = pltpu.SemaphoreType.DMA((3,))
    return _Side([buf4], [jax.ShapeDtypeStruct(buf4.shape, buf4.dtype)], {0: 0}, [dma, dma], start, finish)


def _pair_sum(g, sib, core, name):
    _, r, cdim = g.shape
    half = r // 2

    def body(core_ref, g_ref, s_ref, o_ref):
        o_ref[...] = (g_ref[...].astype(F32) + s_ref[...].astype(F32)).astype(BF)

    return pl.pallas_call(
        body, name=name,
        grid_spec=pltpu.PrefetchScalarGridSpec(
            num_scalar_prefetch=1, grid=(N_CHIP,),
            in_specs=[pl.BlockSpec((None, half, cdim), lambda k, core: (k, core[0], 0)),
                      pl.BlockSpec((None, half, cdim), lambda k, core: (k, 0, 0))],
            out_specs=pl.BlockSpec((None, half, cdim), lambda k, core: (k, 0, 0))),
        out_shape=jax.ShapeDtypeStruct((N_CHIP, half, cdim), BF),
        compiler_params=_params(("arbitrary",)),
    )(core, g, sib)


def _chip_exchange_side(psums, relative=False):
    n = len(psums)

    def copies(ins, outs, sems):
        ssem, rsem = sems
        x, y, c, chips = _place()
        return [pltpu.make_async_remote_copy(
            src_ref=ins[w].at[REL_SLOT[j] if relative else 2 * px + py], dst_ref=outs[w].at[j],
            send_sem=ssem.at[w, j], recv_sem=rsem.at[w, j], device_id=(px, py, c), device_id_type=MESH)
            for w in range(n) for j, (px, py) in enumerate(chips)]

    def start(ins, outs, sems):
        for cp in copies(ins, outs, sems):
            cp.start()

    def finish(ins, outs, sems):
        for cp in copies(ins, outs, sems):
            cp.wait()

    dma = pltpu.SemaphoreType.DMA((n, 3))
    return _Side(list(psums), [jax.ShapeDtypeStruct((3,) + p.shape[1:], BF) for p in psums], {}, [dma, dma],
                 start, finish)


def _final_sum(g, sib, recv, sel, name):
    _, r, cdim = g.shape
    half = r // 2
    nt = 4
    th = half // nt

    def body(sel_ref, g_ref, s_ref, r_ref, o_ref):
        acc = g_ref[...].astype(F32) + s_ref[...].astype(F32)
        for j in range(3):
            acc = acc + r_ref[j].astype(F32)
        o_ref[...] = acc

    return pl.pallas_call(
        body, name=name,
        grid_spec=pltpu.PrefetchScalarGridSpec(
            num_scalar_prefetch=1, grid=(nt,),
            in_specs=[pl.BlockSpec((None, th, cdim), lambda i, sel: (sel[0], sel[1] * nt + i, 0)),
                      pl.BlockSpec((None, th, cdim), lambda i, sel: (sel[0], i, 0)),
                      pl.BlockSpec((3, th, cdim), lambda i, sel: (0, i, 0))],
            out_specs=pl.BlockSpec((th, cdim), lambda i, sel: (sel[1] * nt + i, 0))),
        out_shape=jax.ShapeDtypeStruct((r, cdim), F32),
        compiler_params=_params(("arbitrary",)),
    )(sel, g, sib, recv)


def _join_side(bufs):
    n = len(bufs)

    def copies(outs, sems, core_of):
        ssem, rsem = sems
        x, y, c, _ = _place()
        cps = []
        for w in range(n):
            half = bufs[w].shape[0] // 2
            rows = outs[w].at[pl.ds(core_of(c) * half, half)]
            cps.append(pltpu.make_async_remote_copy(
                src_ref=rows, dst_ref=rows, send_sem=ssem.at[w], recv_sem=rsem.at[w],
                device_id=(x, y, 1 - c), device_id_type=MESH))
        return cps

    def start(ins, outs, sems):
        for cp in copies(outs, sems, lambda c: c):
            cp.start()

    def finish(ins, outs, sems):
        for cp in copies(outs, sems, lambda c: 1 - c):
            cp.wait_recv()
        for cp in copies(outs, sems, lambda c: c):
            cp.wait_send()

    dma = pltpu.SemaphoreType.DMA((n,))
    return _Side(list(bufs), [jax.ShapeDtypeStruct(b.shape, F32) for b in bufs], {w: w for w in range(n)}, [dma, dma],
                 start, finish)


def _small_pair_sum(buf, sib, chip):
    rows, d = buf.shape

    def body(chip_ref, a_ref, b_ref, o_ref):
        o_ref[...] = a_ref[...] + b_ref[...]

    return pl.pallas_call(
        body, name="small_pair_sum", out_shape=jax.ShapeDtypeStruct((N_CHIP, rows, d), F32),
        grid_spec=pltpu.PrefetchScalarGridSpec(
            num_scalar_prefetch=1, grid=(1,),
            in_specs=[pl.BlockSpec((rows, d), lambda i, chip: (0, 0))] * 2,
            out_specs=pl.BlockSpec((None, rows, d), lambda i, chip: (chip[0], 0, 0))),
        compiler_params=_params(("arbitrary",)),
    )(chip, buf, sib)


def _small_all_reduce(buf, name):
    rows, d = buf.shape

    def body(in_ref, out_ref, sib, all4, ssem, rsem, psem, qsem):
        x, y, c, chips = _place()
        me = 2 * x + y
        to_sib = pltpu.make_async_remote_copy(src_ref=in_ref, dst_ref=sib, send_sem=ssem, recv_sem=rsem,
                                              device_id=(x, y, 1 - c), device_id_type=MESH)
        to_sib.start()
        to_sib.wait()
        all4[me] = in_ref[...] + sib[...]
        cps = [pltpu.make_async_remote_copy(src_ref=all4.at[me], dst_ref=all4.at[me], send_sem=psem.at[j],
                                            recv_sem=qsem.at[j], device_id=(px, py, c), device_id_type=MESH)
               for j, (px, py) in enumerate(chips)]
        for cp in cps:
            cp.start()
        for j, (px, py) in enumerate(chips):
            chip = 2 * px + py
            pltpu.make_async_remote_copy(src_ref=all4.at[chip], dst_ref=all4.at[chip], send_sem=psem.at[j],
                                         recv_sem=qsem.at[j], device_id=(px, py, c), device_id_type=MESH).wait_recv()
        for cp in cps:
            cp.wait_send()
        out_ref[...] = (all4[0] + all4[1]) + (all4[2] + all4[3])

    vm = pl.BlockSpec(memory_space=pltpu.VMEM)
    return pl.pallas_call(
        body, name=name, out_shape=jax.ShapeDtypeStruct((rows, d), F32),
        in_specs=[vm], out_specs=vm,
        scratch_shapes=[pltpu.VMEM((rows, d), F32), pltpu.VMEM((N_CHIP, rows, d), F32),
                        pltpu.SemaphoreType.DMA, pltpu.SemaphoreType.DMA,
                        pltpu.SemaphoreType.DMA((3,)), pltpu.SemaphoreType.DMA((3,))],
        compiler_params=_params(),
    )(buf)


def _pair_blocks(w):
    w4 = w.reshape(N_HEADS // 2, 2, HEAD, HEAD)
    eye = jnp.eye(2, dtype=w.dtype)
    return jnp.einsum("pirc,ij->pirjc", w4, eye).reshape(N_HEADS // 2, LANE, LANE)


def _unpair_blocks(w2):
    w5 = w2.reshape(N_HEADS // 2, 2, HEAD, 2, HEAD)
    return jnp.stack([w5[:, 0, :, 0, :], w5[:, 1, :, 1, :]], axis=1).reshape(N_HEADS, HEAD, HEAD)


def kernel(x, meta_tokens, ffn1_pre_g, ffn1_w_gate, ffn1_w_up, ffn1_w_down, ffn1_post_g, mix_pre_g, w_in, lru_conv_w, lru_conv_b, lru_w_a, lru_b_a, lru_w_x, lru_b_x, lru_lambda, sconv_w, lru_out_g, sconv_out_g, w_out, mix_post_g, ffn2_pre_g, ffn2_w_gate, ffn2_w_up, ffn2_w_down, ffn2_post_g, loss_target, m_meta_tokens, m_ffn1_pre_g, m_ffn1_w_gate, m_ffn1_w_up, m_ffn1_w_down, m_ffn1_post_g, m_mix_pre_g, m_w_in, m_lru_conv_w, m_lru_conv_b, m_lru_w_a, m_lru_b_a, m_lru_w_x, m_lru_b_x, m_lru_lambda, m_sconv_w, m_lru_out_g, m_sconv_out_g, m_w_out, m_mix_post_g, m_ffn2_pre_g, m_ffn2_w_gate, m_ffn2_w_up, m_ffn2_w_down, m_ffn2_post_g, v_meta_tokens, v_ffn1_pre_g, v_ffn1_w_gate, v_ffn1_w_up, v_ffn1_w_down, v_ffn1_post_g, v_mix_pre_g, v_w_in, v_lru_conv_w, v_lru_conv_b, v_lru_w_a, v_lru_b_a, v_lru_w_x, v_lru_b_x, v_lru_lambda, v_sconv_w, v_lru_out_g, v_sconv_out_g, v_w_out, v_mix_post_g, v_ffn2_pre_g, v_ffn2_w_gate, v_ffn2_w_up, v_ffn2_w_down, v_ffn2_post_g):
    seq, d = x.shape[1], x.shape[2]
    t_real = N_META + seq
    tp = _round_up(t_real, ROW_ALIGN)
    f4 = ffn1_w_gate.shape[2]
    f4p = _round_up(f4, LANE)
    dl = lru_conv_b.shape[1]
    cin = w_in.shape[2]
    xi, yi, ci = lax.axis_index("x"), lax.axis_index("y"), lax.axis_index("c")
    chip = 2 * xi + yi
    zero = jnp.zeros((), jnp.int32)

    transposed = ("ffn1_w_gate", "ffn1_w_up", "ffn2_w_gate", "ffn2_w_up")

    def view(k, a):
        return a[0].T if k in transposed else a[0]

    def unview(k, a):
        return (a.T if k in transposed else a)[None]

    big = {
        "ffn1_w_gate": (view("ffn1_w_gate", ffn1_w_gate), f4p, d), "ffn1_w_up": (view("ffn1_w_up", ffn1_w_up), f4p, d),
        "ffn1_w_down": (ffn1_w_down[0], f4p, d), "w_in": (w_in[0], d, cin), "w_out": (w_out[0], w_out.shape[1], d),
        "ffn2_w_gate": (view("ffn2_w_gate", ffn2_w_gate), f4p, d), "ffn2_w_up": (view("ffn2_w_up", ffn2_w_up), f4p, d),
        "ffn2_w_down": (ffn2_w_down[0], f4p, d),
    }
    names = list(big)
    chip1 = jnp.reshape(chip, (1,)).astype(jnp.int32)
    relative = {k: k.startswith("ffn") for k in names}
    slot0 = jnp.zeros((1,), jnp.int32)
    shard = {k: _cast_pad(big[k][0], big[k][1], big[k][2], slot0 if relative[k] else chip1, "cast_" + k) for k in names}
    full = {}

    def gather(*keys):
        return _merge_sides([_gather_side([shard[k]], relative[k]) for k in keys])

    gm = jnp.kron(jnp.eye(2, dtype=F32), jnp.full((HEAD, HEAD), 1.0 / HEAD, F32)).astype(BF)
    wa2 = _pair_blocks(lru_w_a[0])
    wx2 = _pair_blocks(lru_w_x[0])

    dlq = dl // N_CHIP
    dq = d // N_CHIP
    R_GAIN, R_LOSS, R_META, R_LRU, R_SC, R_WA = 0, 6, 8, 24, 40, 48
    n_wrows = (N_HEADS // 2) * LANE * LANE // d
    R_WX = R_WA + n_wrows
    R_END = R_WX + n_wrows

    def pack_top(gains, meta, loss=None):
        lossrow = jnp.zeros((2, d), F32)
        if loss is not None:
            lossrow = lossrow.at[0, 0].set(loss)
        return jnp.concatenate([jnp.concatenate(gains, axis=0), lossrow, meta], axis=0)

    def pack_rest(lru16, sc8, wa_, wx_):
        return jnp.concatenate([jnp.concatenate([lru16, jnp.zeros((16, d - dl), F32)], axis=1),
                                jnp.concatenate([sc8, jnp.zeros((8, d - dl), F32)], axis=1),
                                wa_.reshape(n_wrows, d), wx_.reshape(n_wrows, d)], axis=0)

    def pack(gains, meta, lru16, sc8, wa_, wx_):
        return jnp.concatenate([pack_top(gains, meta), pack_rest(lru16, sc8, wa_, wx_)], axis=0)

    def place_cols(blk, width, total):
        return lax.dynamic_update_slice(jnp.zeros((blk.shape[0], total), F32), blk, (zero, chip * width))

    def pack_params(meta_, g1pre, g1post, gmpre, gmpost, g2pre, g2post, cw, cbias, wa_, ba_, wx_, bx_, lam_, sw, lgo, sgo):
        lru16 = jnp.concatenate([place_cols(cw[0], dlq, dl), cbias, ba_, bx_, lam_, lgo, jnp.zeros((7, dl), F32)], axis=0)
        sc8 = jnp.concatenate([place_cols(sw[0], dlq, dl), sgo, jnp.zeros((4, dl), F32)], axis=0)
        return pack([g1pre, g1post, gmpre, gmpost, g2pre, g2post], place_cols(meta_, dq, d), lru16, sc8,
                    _pair_blocks(wa_[0]), _pair_blocks(wx_[0]))

    p_w = pack_params(meta_tokens, ffn1_pre_g, ffn1_post_g, mix_pre_g, mix_post_g, ffn2_pre_g, ffn2_post_g, lru_conv_w,
                      lru_conv_b, lru_w_a, lru_b_a, lru_w_x, lru_b_x, lru_lambda, sconv_w, lru_out_g, sconv_out_g)
    p_m = pack_params(m_meta_tokens, m_ffn1_pre_g, m_ffn1_post_g, m_mix_pre_g, m_mix_post_g, m_ffn2_pre_g, m_ffn2_post_g,
                      m_lru_conv_w, m_lru_conv_b, m_lru_w_a, m_lru_b_a, m_lru_w_x, m_lru_b_x, m_lru_lambda, m_sconv_w,
                      m_lru_out_g, m_sconv_out_g)
    p_v = pack_params(v_meta_tokens, v_ffn1_pre_g, v_ffn1_post_g, v_mix_pre_g, v_mix_post_g, v_ffn2_pre_g, v_ffn2_post_g,
                      v_lru_conv_w, v_lru_conv_b, v_lru_w_a, v_lru_b_a, v_lru_w_x, v_lru_b_x, v_lru_lambda, v_sconv_w,
                      v_lru_out_g, v_sconv_out_g)

    gathered = _small_all_reduce(jnp.where(ci == 0, p_w, 0.0)[R_META:R_WA], "small_weight_gather")
    meta_full = gathered[0:N_META]
    w4_full = gathered[R_LRU - R_META:R_LRU - R_META + 4, 0:dl]
    w3_full = gathered[R_SC - R_META:R_SC - R_META + 3, 0:dl]
    w4p = jnp.concatenate([w4_full, jnp.zeros((4, dl), F32)], axis=0)
    w3p = jnp.concatenate([w3_full, jnp.zeros((5, dl), F32)], axis=0)

    h0 = jnp.concatenate([meta_full, x[0], jnp.zeros((tp - t_real, d), F32)], axis=0)
    tgt = jnp.concatenate([jnp.zeros((N_META, d), F32), loss_target[0], jnp.zeros((tp - t_real, d), F32)], axis=0)

    n1 = _norm0(h0, ffn1_pre_g)
    (a1, b1, s1), (full["ffn1_w_gate"], full["ffn1_w_up"]), got = _ffn_up_head(
        n1, shard["ffn1_w_gate"], shard["ffn1_w_up"], "ffn1_up", gather("ffn1_w_down", "w_in"))
    full["ffn1_w_down"], full["w_in"] = got
    f1, got = _row_matmul([(s1, full["ffn1_w_down"])], "ffn1_down", False, d, gather("w_out"))
    full["w_out"] = got[0]
    h1, u = _post_fwd(f1, h0, ffn1_post_g, mix_pre_g, 0.5, "ffn1_post")
    z, got = _col_matmul(u, full["w_in"], "in_proj", False, F32, gather("ffn2_w_gate"))
    full["ffn2_w_gate"] = got[0]
    (m_lru, hs), got = _lru_fwd(z, w4p, lru_conv_b, wa2.astype(BF), lru_b_a, wx2.astype(BF), lru_b_x, lru_lambda,
                                lru_out_g, gm, gather("ffn2_w_up"))
    full["ffn2_w_up"] = got[0]
    m_sc = _sc_fwd(z, w3p, sconv_out_g, gm, dl)
    mixed = jnp.concatenate([m_lru, m_sc], axis=1)
    p, _ = _row_matmul([(mixed, full["w_out"])], "out_proj", False, d)
    h2, n2 = _post_fwd(p, h1, mix_post_g, ffn2_pre_g, 1.0, "mix_post")
    (a2, b2, s2), got = _ffn_up(n2, full["ffn2_w_gate"], full["ffn2_w_up"], "ffn2_up", gather("ffn2_w_down"))
    full["ffn2_w_down"] = got[0]
    f2, _ = _row_matmul([(s2, full["ffn2_w_down"])], "ffn2_down", False, d)
    dh3, df2, dg_ffn2_post, loss_part = _loss_bwd(f2, h2, tgt, ffn2_post_g, t_real)

    core = jnp.reshape(ci, (1,)).astype(jnp.int32)
    sel_of = {False: jnp.stack([chip, ci]).astype(jnp.int32), True: jnp.stack([0 * chip, ci]).astype(jnp.int32)}
    red = {}

    def pair_side(k):
        return _pair_exchange_side([red[k][0]])

    def chip_side(k):
        return _chip_exchange_side([_pair_sum(red[k][0], red[k][1], core, "pair_sum_" + k)], relative[k])

    def final_sum(k):
        return _final_sum(*red[k], sel_of[relative[k]], "final_sum_" + k)

    (da2, db2), _ = _ffn_bwd_act(df2, full["ffn2_w_down"], a2, b2, "ffn2_bwd_act")
    g, _ = _wgrad_call(s2, df2, "ffn2_down_wgrad", tile_y=WGRAD_TILE_Y)
    red["ffn2_w_down"] = [g, None, None]
    g, got = _wgrad_call(da2, n2, "ffn2_gate_wgrad", tile_y=WGRAD_TILE_Y, side=pair_side("ffn2_w_down"))
    red["ffn2_w_down"][1] = got[0]
    red["ffn2_w_gate"] = [g, None, None]
    g, got = _wgrad_call(db2, n2, "ffn2_up_wgrad", tile_y=WGRAD_TILE_Y,
                         side=_merge_sides([pair_side("ffn2_w_gate"), chip_side("ffn2_w_down")]))
    red["ffn2_w_gate"][1], red["ffn2_w_down"][2] = got
    red["ffn2_w_up"] = [g, None, None]
    red["ffn2_w_up"][1] = _run_side(pair_side("ffn2_w_up"), "pair_exchange_ffn2_w_up")[0]
    dn2, got = _row_matmul([(da2, full["ffn2_w_gate"]), (db2, full["ffn2_w_up"])], "ffn2_bwd_up", False, d,
                           _merge_sides([chip_side("ffn2_w_gate"), chip_side("ffn2_w_up")]), tiles=MM_TILES)
    red["ffn2_w_gate"][2], red["ffn2_w_up"][2] = got
    dh2, dp, dg_ffn2_pre, dg_mix_post = _pre_bwd(dn2, h2, dh3, ffn2_pre_g, "ffn2_pre_bwd", (p, mix_post_g, 1.0))
    dmixed, _ = _col_matmul(dp, full["w_out"], "out_proj_bwd", True, F32)
    g, _ = _wgrad_call(mixed, dp, "w_out_wgrad", x_width=mixed.shape[1] // N_CHIP, tile_y=WGRAD_TILE_Y)
    red["w_out"] = [g, None, None]
    dzy, dzx, lru_small, dwa2, dwx2 = _lru_bwd(z, hs, dmixed, w4p, lru_conv_b, wa2.astype(BF), lru_b_a, wx2.astype(BF),
                                               lru_b_x, lru_lambda, lru_out_g, gm)
    dzb, dzc, dzv, sc_small = _sc_bwd(z, dmixed, w3p, sconv_out_g, gm, dl)
    dz = jnp.concatenate([dzy, dzx, dzb, dzc, dzv], axis=1)
    p_rest = pack_rest(lru_small, sc_small, dwa2, dwx2)
    g, got = _wgrad_call(u, dz, "w_in_wgrad", y_width=cin, tile_x=WGRAD_TILE_X,
                         side=_merge_sides([pair_side("w_out"), _sibling_copy_side(p_rest)]))
    red["w_out"][1] = got[0]
    p_rest4 = _small_pair_sum(p_rest, got[1], chip1)
    red["w_in"] = [g, None, None]
    du, got = _row_matmul([(dz, full["w_in"])], "in_proj_bwd", True, d,
                          _merge_sides([pair_side("w_in"), chip_side("w_out")]))
    red["w_in"][1], red["w_out"][2] = got
    dh1, df1, dg_mix_pre, dg_ffn1_post = _pre_bwd(du, h1, dh2, mix_pre_g, "mix_pre_bwd", (f1, ffn1_post_g, 0.5))
    (da1, db1), got = _ffn_bwd_act(df1, full["ffn1_w_down"], a1, b1, "ffn1_bwd_act",
                                   _merge_sides([chip_side("w_in"), _slot_exchange_side(p_rest4)]))
    red["w_in"][2], p_rest4 = got
    early = ["ffn2_w_down", "ffn2_w_gate", "ffn2_w_up", "w_out", "w_in"]
    late = ["ffn1_w_down", "ffn1_w_gate", "ffn1_w_up"]
    g, got = _wgrad_call(s1, df1, "ffn1_down_wgrad", tile_y=WGRAD_TILE_Y,
                         side=_join_side([final_sum(k) for k in early]))
    gfull = dict(zip(early, got))
    red["ffn1_w_down"] = [g, None, None]
    g, got = _wgrad_call(da1, n1, "ffn1_gate_wgrad", tile_y=WGRAD_TILE_Y, side=pair_side("ffn1_w_down"))
    red["ffn1_w_down"][1] = got[0]
    red["ffn1_w_gate"] = [g, None, None]
    g, got = _wgrad_call(db1, n1, "ffn1_up_wgrad", tile_y=WGRAD_TILE_Y,
                         side=_merge_sides([pair_side("ffn1_w_gate"), chip_side("ffn1_w_down")]))
    red["ffn1_w_gate"][1], red["ffn1_w_down"][2] = got
    red["ffn1_w_up"] = [g, None, None]
    red["ffn1_w_up"][1] = _run_side(pair_side("ffn1_w_up"), "pair_exchange_ffn1_w_up")[0]
    dn1, got = _row_matmul([(da1, full["ffn1_w_gate"]), (db1, full["ffn1_w_up"])], "ffn1_bwd_up", False, d,
                           _merge_sides([chip_side("ffn1_w_gate"), chip_side("ffn1_w_up")]), tiles=MM_TILES)
    red["ffn1_w_gate"][2], red["ffn1_w_up"][2] = got
    (dh0, dg_ffn1_pre), got = _pre_bwd(dn1, h0, dh1, ffn1_pre_g, "ffn1_pre_bwd",
                                       side=_join_side([final_sum(k) for k in late]))
    gfull.update(zip(late, got))

    grad_x = dh0[N_META:t_real][None]

    w_big = {"ffn1_w_gate": ffn1_w_gate, "ffn1_w_up": ffn1_w_up, "ffn1_w_down": ffn1_w_down, "w_in": w_in, "w_out": w_out,
             "ffn2_w_gate": ffn2_w_gate, "ffn2_w_up": ffn2_w_up, "ffn2_w_down": ffn2_w_down}
    m_big = {"ffn1_w_gate": m_ffn1_w_gate, "ffn1_w_up": m_ffn1_w_up, "ffn1_w_down": m_ffn1_w_down, "w_in": m_w_in,
             "w_out": m_w_out, "ffn2_w_gate": m_ffn2_w_gate, "ffn2_w_up": m_ffn2_w_up, "ffn2_w_down": m_ffn2_w_down}
    v_big = {"ffn1_w_gate": v_ffn1_w_gate, "ffn1_w_up": v_ffn1_w_up, "ffn1_w_down": v_ffn1_w_down, "w_in": v_w_in,
             "w_out": v_w_out, "ffn2_w_gate": v_ffn2_w_gate, "ffn2_w_up": v_ffn2_w_up, "ffn2_w_down": v_ffn2_w_down}
    b_grad, b_delta, b_newm, b_newv = {}, {}, {}, {}

    def big_adamw(k, side=None):
        wv, mv, vv = view(k, w_big[k]), view(k, m_big[k]), view(k, v_big[k])
        wide_rows = wv.shape[0] % 64 == 0
        (g_, d_, m_, v_), got = _adamw(wv, gfull[k], mv, vv, "adamw_" + k, 8 if wide_rows else 4, 1 if wide_rows else 2,
                                       side)
        b_grad[k], b_delta[k], b_newm[k], b_newv[k] = unview(k, g_), unview(k, d_), unview(k, m_), unview(k, v_)
        return got

    p_top = _small_all_reduce(
        pack_top([dg_ffn1_pre, dg_ffn1_post, dg_mix_pre, dg_mix_post, dg_ffn2_pre, dg_ffn2_post], dh0[0:N_META],
                 loss=loss_part[0, 0]), "small_grad_all_reduce")
    p_g, p_delta, p_newm, p_newv = _adamw_small(p_w, p_top, p_rest4, p_m, p_v)
    loss = p_g[R_LOSS, 0]
    for k in names:
        big_adamw(k)

    def unpack(buf):
        out = {}
        for i, k in enumerate(["ffn1_pre_g", "ffn1_post_g", "mix_pre_g", "mix_post_g", "ffn2_pre_g", "ffn2_post_g"]):
            out[k] = buf[R_GAIN + i:R_GAIN + i + 1]
        out["meta_tokens"] = lax.dynamic_slice(buf[R_META:R_META + N_META], (zero, chip * dq), (N_META, dq))
        lru = buf[R_LRU:R_LRU + 16, 0:dl]
        out["lru_conv_w"] = lax.dynamic_slice(lru[0:4], (zero, chip * dlq), (4, dlq))[None]
        out["lru_conv_b"] = lru[4:5]
        out["lru_b_a"] = lru[5:6]
        out["lru_b_x"] = lru[6:7]
        out["lru_lambda"] = lru[7:8]
        out["lru_out_g"] = lru[8:9]
        sc = buf[R_SC:R_SC + 8, 0:dl]
        out["sconv_w"] = lax.dynamic_slice(sc[0:3], (zero, chip * dlq), (3, dlq))[None]
        out["sconv_out_g"] = sc[3:4]
        out["lru_w_a"] = _unpair_blocks(buf[R_WA:R_WX].reshape(N_HEADS // 2, LANE, LANE))[None]
        out["lru_w_x"] = _unpair_blocks(buf[R_WX:R_END].reshape(N_HEADS // 2, LANE, LANE))[None]
        return out

    s_grad, s_delta, s_newm, s_newv = unpack(p_g), unpack(p_delta), unpack(p_newm), unpack(p_newv)

    order = ["meta_tokens", "ffn1_pre_g", "ffn1_w_gate", "ffn1_w_up", "ffn1_w_down", "ffn1_post_g", "mix_pre_g", "w_in",
             "lru_conv_w", "lru_conv_b", "lru_w_a", "lru_b_a", "lru_w_x", "lru_b_x", "lru_lambda", "sconv_w", "lru_out_g",
             "sconv_out_g", "w_out", "mix_post_g", "ffn2_pre_g", "ffn2_w_gate", "ffn2_w_up", "ffn2_w_down", "ffn2_post_g"]

    def pick(small, bigd):
        return [bigd[k] if k in bigd else small[k] for k in order]

    return (loss, grad_x, *pick(s_grad, b_grad), *pick(s_delta, b_delta), *pick(s_newm, b_newm), *pick(s_newv, b_newv))
```

```python
import functools
import math

import jax
import jax.numpy as jnp
from jax import lax
from jax.experimental import pallas as pl
from jax.experimental.pallas import tpu as pltpu

F32 = jnp.float32
BF = jnp.bfloat16
MESH = pl.DeviceIdType.MESH

EPS = 1e-6
N_META = 16
N_HEADS = 16
HEAD = 64
LRU_C = 8.0
LANE = 128
MXU_COLS = 256
N_CHIP = 4
ROW_ALIGN = 384
MM_TILES = 8
MM_TILES_BIG = 4
EW_TILES = 12
MIX_CHUNKS = 24
WGRAD_TILE_X = 256
WGRAD_TILE_Y = 512
VMEM_LIMIT = 56 << 20

ADAM_LR = 0.001
ADAM_B1 = 0.9
ADAM_B2 = 0.999
ADAM_EPS = 1e-08
ADAM_WD = 0.01
ADAM_STEP = 10


def _round_up(a, b):
    return (a + b - 1) // b * b


def _params(sem=None):
    if sem is None:
        return pltpu.CompilerParams(vmem_limit_bytes=VMEM_LIMIT)
    return pltpu.CompilerParams(dimension_semantics=sem, vmem_limit_bytes=VMEM_LIMIT)


def _sigmoid(x):
    return 0.5 * jnp.tanh(0.5 * x) + 0.5


def _dot(a, b):
    return jnp.dot(a, b, preferred_element_type=F32)


def _dot_nt(a, b):
    return lax.dot_general(a, b, (((1,), (1,)), ((), ())), preferred_element_type=F32)


def _dot_tn(a, b):
    return lax.dot_general(a, b, (((0,), (0,)), ((), ())), preferred_element_type=F32)


def _rms(x, g):
    r = lax.rsqrt(jnp.mean(x * x, axis=-1, keepdims=True) + EPS)
    return x * r * g


def _rms_bwd(x, g, dy):
    r = lax.rsqrt(jnp.mean(x * x, axis=-1, keepdims=True) + EPS)
    xh = x * r
    q = dy * g
    dx = r * (q - xh * jnp.mean(q * xh, axis=-1, keepdims=True))
    return dx, dy * xh


class _Side:
    def __init__(self, ins, outs, alias, sems, start, finish):
        self.ins, self.outs, self.alias, self.sems, self.start, self.finish = ins, outs, alias, sems, start, finish


def _merge_sides(sides):
    sides = [s for s in sides if s is not None]
    if len(sides) <= 1:
        return sides[0] if sides else None
    ins, outs, sems, alias, spans = [], [], [], {}, []
    for s in sides:
        for i, o in s.alias.items():
            alias[len(ins) + i] = len(outs) + o
        spans.append((len(ins), len(ins) + len(s.ins), len(outs), len(outs) + len(s.outs), len(sems),
                      len(sems) + len(s.sems)))
        ins += list(s.ins)
        outs += list(s.outs)
        sems += list(s.sems)

    def run(which):
        def go(in_refs, out_refs, sem_refs):
            for s, (a, b, c, d, e, f) in zip(sides, spans):
                getattr(s, which)(in_refs[a:b], out_refs[c:d], sem_refs[e:f])
        return go

    return _Side(ins, outs, alias, sems, run("start"), run("finish"))


def _grid_call(body, name, grid, in_specs, out_specs, out_shape, args, side=None, scratch=()):
    sem = ("arbitrary",) * len(grid)
    if side is None:
        res = pl.pallas_call(body, name=name, grid=grid, in_specs=in_specs, out_specs=out_specs, out_shape=out_shape,
                             scratch_shapes=list(scratch), compiler_params=_params(sem))(*args)
        return res, []
    nin, nout, sin, sout = len(in_specs), len(out_specs), len(side.ins), len(side.outs)
    nscr = len(scratch)

    def full(*refs):
        base_in, side_in = refs[:nin], refs[nin:nin + sin]
        base_out = refs[nin + sin:nin + sin + nout]
        side_out = refs[nin + sin + nout:nin + sin + nout + sout]
        base_scr = refs[nin + sin + nout + sout:nin + sin + nout + sout + nscr]
        sems = refs[nin + sin + nout + sout + nscr:]
        first = pl.program_id(0) == 0
        last = pl.program_id(0) == grid[0] - 1
        for ax in range(1, len(grid)):
            first = first & (pl.program_id(ax) == 0)
            last = last & (pl.program_id(ax) == grid[ax] - 1)

        @pl.when(first)
        def _():
            side.start(side_in, side_out, sems)

        body(*base_in, *base_out, *base_scr)

        @pl.when(last)
        def _():
            side.finish(side_in, side_out, sems)

    any_spec = pl.BlockSpec(memory_space=pl.ANY)
    res = pl.pallas_call(
        full, name=name, grid=grid, in_specs=list(in_specs) + [any_spec] * sin,
        out_specs=list(out_specs) + [any_spec] * sout, out_shape=list(out_shape) + list(side.outs),
        scratch_shapes=list(scratch) + list(side.sems),
        input_output_aliases={nin + i: nout + o for i, o in side.alias.items()},
        compiler_params=_params(sem))(*args, *side.ins)
    return res[:nout], res[nout:]


def _ffn_up(n, wg, wu, name, side=None, tiles=MM_TILES):
    tp, d = n.shape
    fp = wg.shape[1]
    tm = tp // tiles

    def body(n_ref, wg_ref, wu_ref, a_ref, b_ref, s_ref):
        nn = n_ref[...]
        for c0 in range(0, fp, MXU_COLS):
            cs = slice(c0, min(c0 + MXU_COLS, fp))
            a = _dot_nt(nn, wg_ref[cs, :])
            b = _dot_nt(nn, wu_ref[cs, :])
            a_ref[:, cs] = a.astype(BF)
            b_ref[:, cs] = b.astype(BF)
            s_ref[:, cs] = (a * _sigmoid(a) * b).astype(BF)

    out = jax.ShapeDtypeStruct((N_CHIP, tp, fp), BF)
    wspec = pl.BlockSpec((None, fp, d), lambda k, i: (k, 0, 0))
    ospec = pl.BlockSpec((None, tm, fp), lambda k, i: (k, i, 0))
    return _grid_call(body, name, (N_CHIP, tiles), [pl.BlockSpec((tm, d), lambda k, i: (i, 0)), wspec, wspec],
                      [ospec, ospec, ospec], [out, out, out], (n, wg, wu), side)


def _ffn_up_head(n, wg, wu, name, side):
    tp, d = n.shape
    fp = wg.shape[1]
    tiles = MM_TILES
    tm = tp // tiles
    gat = _gather_side([wg, wu], relative=True)
    sin, sout = len(side.ins), len(side.outs)
    order = (0,) + REL_SLOT

    def body(*refs):
        n_ref = refs[0]
        si = refs[3:3 + sin]
        a_ref, b_ref, s_ref = refs[3 + sin:6 + sin]
        go = refs[6 + sin:8 + sin]
        so = refs[8 + sin:8 + sin + sout]
        wbg, wbu, wsem = refs[8 + sin + sout:11 + sin + sout]
        gsems = refs[11 + sin + sout:15 + sin + sout]
        ssems = refs[15 + sin + sout:]
        k, i = pl.program_id(0), pl.program_id(1)
        cur = k % 2

        def to_vmem(slot, buf):
            return [pltpu.make_async_copy(go[0].at[slot], wbg.at[buf], wsem.at[buf, 0]),
                    pltpu.make_async_copy(go[1].at[slot], wbu.at[buf], wsem.at[buf, 1])]

        @pl.when((k == 0) & (i == 0))
        def _():
            gat.send(go, gsems)
            side.start(si, so, ssems)
            for cp in to_vmem(0, 0):
                cp.start()
            for cp in to_vmem(0, 0):
                cp.wait()

        for j in range(3):
            @pl.when((k == j) & (i == tiles // 2))
            def _():
                gat.arrived(j, go, gsems)

            @pl.when((k == j) & (i == tiles - 2))
            def _():
                gat.forwarded(j, go, gsems)
                for cp in to_vmem(order[j + 1], (j + 1) % 2):
                    cp.start()

            @pl.when((k == j + 1) & (i == 0))
            def _():
                for cp in to_vmem(order[j + 1], (j + 1) % 2):
                    cp.wait()

        nn = n_ref[...]
        for c0 in range(0, fp, MXU_COLS):
            cs = pl.ds(c0, min(MXU_COLS, fp - c0))
            a = _dot_nt(nn, wbg[cur, cs, :])
            b = _dot_nt(nn, wbu[cur, cs, :])
            a_ref[:, cs] = a.astype(BF)
            b_ref[:, cs] = b.astype(BF)
            s_ref[:, cs] = (a * _sigmoid(a) * b).astype(BF)

        @pl.when((k == N_CHIP - 1) & (i == tiles - 1))
        def _():
            gat.drain(go, gsems)
            side.finish(si, so, ssems)

    out = jax.ShapeDtypeStruct((N_CHIP, tp, fp), BF)
    any_spec = pl.BlockSpec(memory_space=pl.ANY)
    slot_of = lambda k: (k % 2) * 2 + k // 2
    ospec = pl.BlockSpec((None, tm, fp), lambda k, i: (slot_of(k), i, 0))
    wbuf = pltpu.VMEM((2, fp, d), BF)
    res = pl.pallas_call(
        body, name=name, grid=(N_CHIP, tiles),
        in_specs=[pl.BlockSpec((tm, d), lambda k, i: (i, 0))] + [any_spec] * (2 + sin),
        out_specs=[ospec, ospec, ospec] + [any_spec] * (2 + sout),
        out_shape=[out, out, out] + list(gat.outs) + list(side.outs),
        scratch_shapes=[wbuf, wbuf, pltpu.SemaphoreType.DMA((2, 2))] + list(gat.sems) + list(side.sems),
        input_output_aliases={1: 3, 2: 4, **{3 + a: 5 + b for a, b in side.alias.items()}},
        compiler_params=_params(("arbitrary", "arbitrary")))(n, wg, wu, *side.ins)
    return res[:3], res[3:5], res[5:]


def _ffn_bwd_act(df, wd, a, b, name, side=None, tiles=MM_TILES):
    tp, d = df.shape
    fp = wd.shape[1]
    tm = tp // tiles

    def body(df_ref, wd_ref, a_ref, b_ref, da_ref, db_ref):
        dfv = df_ref[...]
        for c0 in range(0, fp, MXU_COLS):
            cs = slice(c0, min(c0 + MXU_COLS, fp))
            ds = _dot_nt(dfv, wd_ref[cs, :])
            av = a_ref[:, cs].astype(F32)
            bv = b_ref[:, cs].astype(F32)
            sg = _sigmoid(av)
            da_ref[:, cs] = (ds * bv * sg * (1.0 + av * (1.0 - sg))).astype(BF)
            db_ref[:, cs] = (ds * av * sg).astype(BF)

    out = jax.ShapeDtypeStruct((N_CHIP, tp, fp), BF)
    aspec = pl.BlockSpec((None, tm, fp), lambda k, i: (k, i, 0))
    return _grid_call(
        body, name, (N_CHIP, tiles),
        [pl.BlockSpec((tm, d), lambda k, i: (i, 0)), pl.BlockSpec((None, fp, d), lambda k, i: (k, 0, 0)), aspec, aspec],
        [aspec, aspec], [out, out], (df, wd, a, b), side)


def _col_matmul(lhs, w, name, trans_b, out_dtype, side=None, tiles=MM_TILES_BIG):
    tp, kd = lhs.shape
    nk = w.shape[0]
    nc = w.shape[1] if trans_b else w.shape[2]
    tm = tp // tiles

    def body(l_ref, w_ref, o_ref):
        if trans_b:
            o_ref[...] = _dot_nt(l_ref[...], w_ref[...]).astype(out_dtype)
        else:
            o_ref[...] = _dot(l_ref[...], w_ref[...]).astype(out_dtype)

    res, extra = _grid_call(
        body, name, (nk, tiles),
        [pl.BlockSpec((tm, kd), lambda k, i: (i, 0)),
         pl.BlockSpec((None,) + tuple(w.shape[1:]), lambda k, i: (k, 0, 0), pipeline_mode=pl.Buffered(1))],
        [pl.BlockSpec((tm, nc), lambda k, i: (i, k))], [jax.ShapeDtypeStruct((tp, nk * nc), out_dtype)], (lhs, w), side)
    return res[0], extra


def _row_matmul(pairs, name, trans_b, d_out, side=None, tiles=MM_TILES_BIG):
    l0 = pairs[0][0]
    tp = l0.shape[1] if l0.ndim == 3 else l0.shape[0]
    nk = pairs[0][1].shape[0]
    tm = tp // tiles
    npair = len(pairs)

    def body(*refs):
        o_ref = refs[2 * npair]
        k = pl.program_id(1)
        part = None
        for q in range(npair):
            l = refs[2 * q][...]
            w = refs[2 * q + 1][...]
            t = _dot_nt(l, w) if trans_b else _dot(l, w)
            part = t if part is None else part + t

        @pl.when(k == 0)
        def _():
            o_ref[...] = part

        @pl.when(k > 0)
        def _():
            o_ref[...] += part

    in_specs, args = [], []
    for lhs, w in pairs:
        if lhs.ndim == 3:
            in_specs.append(pl.BlockSpec((None, tm, lhs.shape[2]), lambda i, k: (k, i, 0)))
        else:
            in_specs.append(pl.BlockSpec((tm, lhs.shape[1] // nk), lambda i, k: (i, k)))
        in_specs.append(pl.BlockSpec((None,) + tuple(w.shape[1:]), lambda i, k: (k, 0, 0)))
        args += [lhs, w]
    res, extra = _grid_call(body, name, (tiles, nk), in_specs, [pl.BlockSpec((tm, d_out), lambda i, k: (i, 0))],
                            [jax.ShapeDtypeStruct((tp, d_out), F32)], args, side)
    return res[0], extra


def _wgrad_call(x, y, name, x_width=None, y_width=None, tile_x=None, tile_y=None, side=None):
    tp = x.shape[1] if x.ndim == 3 else x.shape[0]

    def spec(a, width, tile):
        cols = a.shape[2] if a.ndim == 3 else (a.shape[1] if width is None else width)
        tc = cols if tile is None else tile
        per = cols // tc
        if a.ndim == 3:
            return pl.BlockSpec((None, tp, tc), lambda k, t: (k, 0, t if tile else 0)), cols, per
        if width is None:
            return pl.BlockSpec((tp, tc), lambda k, t: (0, t if tile else 0)), cols, per
        return pl.BlockSpec((tp, tc), lambda k, t: (0, k * per + (t if tile else 0))), cols, per

    xs, p, nx = spec(x, x_width, tile_x)
    ys, q, ny = spec(y, y_width, tile_y)
    nt = nx * ny
    if tile_x:
        ospec = pl.BlockSpec((None, tile_x, q), lambda k, t: (k, t, 0))
    else:
        ospec = pl.BlockSpec((None, p, tile_y), lambda k, t: (k, 0, t))

    def body(x_ref, y_ref, o_ref):
        o_ref[...] = _dot_tn(x_ref[...], y_ref[...]).astype(BF)

    res, extra = _grid_call(body, name, (N_CHIP, nt), [xs, ys], [ospec], [jax.ShapeDtypeStruct((N_CHIP, p, q), BF)],
                            (x, y), side)
    return res[0], extra


def _row_call(body, name, tp, d, row_ins, vec_ins, row_out_dtypes, n_acc, side=None):
    te = tp // EW_TILES
    rspec = pl.BlockSpec((te, d), lambda i: (i, 0))
    vspec = pl.BlockSpec((1, d), lambda i: (0, 0))
    res, extra = _grid_call(
        body, name, (EW_TILES,), [rspec] * len(row_ins) + [vspec] * len(vec_ins),
        [rspec] * len(row_out_dtypes) + [vspec] * n_acc,
        [jax.ShapeDtypeStruct((tp, d), dt) for dt in row_out_dtypes] + [jax.ShapeDtypeStruct((1, d), F32)] * n_acc,
        (*row_ins, *vec_ins), side)
    return res if side is None else (res, extra)


def _norm0(h, g):
    tp, d = h.shape

    def body(h_ref, g_ref, n_ref):
        n_ref[...] = _rms(h_ref[...], g_ref[...]).astype(BF)

    return _row_call(body, "norm0", tp, d, [h], [g], [BF], 0)[0]


def _post_fwd(f, h, g_post, g_next, scale, name):
    tp, d = h.shape

    def body(f_ref, h_ref, gp_ref, gn_ref, hn_ref, n_ref):
        hn = h_ref[...] + scale * _rms(f_ref[...], gp_ref[...])
        hn_ref[...] = hn
        n_ref[...] = _rms(hn, gn_ref[...]).astype(BF)

    return _row_call(body, name, tp, d, [f, h], [g_post, g_next], [F32, BF], 0)


def _loss_bwd(f, h, tgt, g_post, t_real):
    tp, d = h.shape
    te = tp // EW_TILES

    def body(f_ref, h_ref, t_ref, gp_ref, dh_ref, df_ref, dg_ref, loss_ref):
        i = pl.program_id(0)

        @pl.when(i == 0)
        def _():
            dg_ref[...] = jnp.zeros_like(dg_ref)
            loss_ref[...] = jnp.zeros_like(loss_ref)

        f = f_ref[...]
        gp = gp_ref[...]
        h3 = h_ref[...] + 0.5 * _rms(f, gp)
        rows = i * te + lax.broadcasted_iota(jnp.int32, (te, 1), 0)
        real = (rows >= N_META) & (rows < t_real)
        e = jnp.where(real, h3 - t_ref[...], 0.0)
        loss_ref[...] += 0.5 * jnp.sum(jnp.sum(e * e, axis=1, keepdims=True), axis=0, keepdims=True) / d
        dh = e / d
        dh_ref[...] = dh
        dfv, dgr = _rms_bwd(f, gp, 0.5 * dh)
        df_ref[...] = dfv.astype(BF)
        dg_ref[...] += jnp.sum(dgr, axis=0, keepdims=True)

    rspec = pl.BlockSpec((te, d), lambda i: (i, 0))
    vspec = pl.BlockSpec((1, d), lambda i: (0, 0))
    return pl.pallas_call(
        body, name="loss_bwd", grid=(EW_TILES,),
        in_specs=[rspec, rspec, rspec, vspec],
        out_specs=[rspec, rspec, vspec, pl.BlockSpec((1, 1), lambda i: (0, 0))],
        out_shape=[jax.ShapeDtypeStruct((tp, d), F32), jax.ShapeDtypeStruct((tp, d), BF),
                   jax.ShapeDtypeStruct((1, d), F32), jax.ShapeDtypeStruct((1, 1), F32)],
        compiler_params=_params(("arbitrary",)),
    )(f, h, tgt, g_post)


def _pre_bwd(dn, h, dh_out, g_pre, name, chain=None, side=None):
    tp, d = h.shape

    def body(*refs):
        if chain is None:
            dn_ref, h_ref, dho_ref, g_ref, dh_ref, dg_ref = refs
        else:
            dn_ref, h_ref, dho_ref, p_ref, g_ref, gp_ref, dh_ref, dp_ref, dg_ref, dgp_ref = refs
        i = pl.program_id(0)

        @pl.when(i == 0)
        def _():
            dg_ref[...] = jnp.zeros_like(dg_ref)
            if chain is not None:
                dgp_ref[...] = jnp.zeros_like(dgp_ref)

        dx, dgr = _rms_bwd(h_ref[...], g_ref[...], dn_ref[...])
        dh = dho_ref[...] + dx
        dh_ref[...] = dh
        dg_ref[...] += jnp.sum(dgr, axis=0, keepdims=True)
        if chain is not None:
            dp, dgpr = _rms_bwd(p_ref[...], gp_ref[...], chain[2] * dh)
            dp_ref[...] = dp.astype(BF)
            dgp_ref[...] += jnp.sum(dgpr, axis=0, keepdims=True)

    if chain is None:
        return _row_call(body, name, tp, d, [dn, h, dh_out], [g_pre], [F32], 1, side)
    return _row_call(body, name, tp, d, [dn, h, dh_out, chain[0]], [g_pre, chain[1]], [F32, BF], 2, side)


def _gelu(y):
    c = math.sqrt(2.0 / math.pi)
    return 0.5 * y * (1.0 + jnp.tanh(c * (y + 0.044715 * y * y * y)))


def _gelu_grad(y):
    c = math.sqrt(2.0 / math.pi)
    t = jnp.tanh(c * (y + 0.044715 * y * y * y))
    return 0.5 * (1.0 + t) + 0.5 * y * (1.0 - t * t) * c * (1.0 + 3.0 * 0.044715 * y * y)


def _neg_expm1(x):
    p = 1.0 + x * (1.0 / 9.0)
    for n in (8.0, 7.0, 6.0, 5.0, 4.0, 3.0, 2.0):
        p = 1.0 + x * (1.0 / n) * p
    return -jnp.where(x > -0.35, x * p, jnp.exp(x) - 1.0)


def _softplus(x):
    e = jnp.exp(-jnp.abs(x))
    w = 1.0 + e
    l1p = jnp.where(w == 1.0, e, jnp.log(w) * (e / jnp.where(w == 1.0, 1.0, w - 1.0)))
    return jnp.maximum(x, 0.0) + l1p


def _group_mean(v, gm):
    hi = v.astype(BF)
    lo = (v - hi.astype(F32)).astype(BF)
    return _dot(hi, gm) + _dot(lo, gm)


def _shift_dn(win, s, r):
    if s == 0:
        return win[8:8 + r]
    return pltpu.roll(win, s, 0)[8:8 + r]


def _shift_up(win, s, r):
    if s == 0:
        return win[0:r]
    return pltpu.roll(win, r + 8 - s, 0)[0:r]


def _window_dn(ref, t0, r, first):
    if first:
        return jnp.concatenate([jnp.zeros((8, ref.shape[1]), F32), ref[0:r, :]], axis=0)
    return ref[pl.ds(t0 - 8, r + 8), :]


def _tile_scan(a, u, reverse):
    r = a.shape[0]
    rid = lax.broadcasted_iota(jnp.int32, a.shape, 0) & 7
    for dlt in (1, 2, 4):
        sh = (r - dlt) if reverse else dlt
        a_s = pltpu.roll(a, sh, 0)
        u_s = pltpu.roll(u, sh, 0)
        keep = (rid + dlt <= 7) if reverse else (rid >= dlt)
        u = jnp.where(keep, u + a * u_s, u)
        a = jnp.where(keep, a * a_s, a)
    return a, u


def _lru_gates(xc, wa, ba, wx, bx, sp):
    xb = xc.astype(BF)
    ga = _sigmoid(_dot(xb, wa) + ba)
    gx = _sigmoid(_dot(xb, wx) + bx)
    la = -LRU_C * ga * sp
    return ga, gx, la


def _conv4(win, w4, cb, r):
    return (cb + w4[3:4] * _shift_dn(win, 0, r) + w4[2:3] * _shift_dn(win, 1, r)
            + w4[1:2] * _shift_dn(win, 2, r) + w4[0:1] * _shift_dn(win, 3, r))


def _lru_fwd(z, w4, cb, wa2, ba, wx2, bx, lam, g_out, gm, side=None):
    tp = z.shape[0]
    dl = cb.shape[1]
    nb = dl // LANE
    r = tp // MIX_CHUNKS
    c = LANE

    def body(y_ref, x_ref, w4_ref, cb_ref, wa_ref, ba_ref, wx_ref, bx_ref, lam_ref, go_ref, gm_ref, m_ref, hs_ref):
        w4v = w4_ref[...]
        cbv = cb_ref[...]
        wa = wa_ref[...]
        wx = wx_ref[...]
        bav = ba_ref[...]
        bxv = bx_ref[...]
        gov = go_ref[...]
        gmv = gm_ref[...]
        sp = _softplus(-lam_ref[...])

        def chunk(t0, hprev, first):
            win = _window_dn(x_ref, t0, r, first)
            xc = _conv4(win, w4v, cbv, r)
            ga, gx, la = _lru_gates(xc, wa, bav, wx, bxv, sp)
            a = jnp.exp(la)
            u = jnp.sqrt(_neg_expm1(2.0 * la)) * gx * xc
            ac, uc = _tile_scan(a, u, False)
            for j in range(r // 8):
                hj = uc[8 * j:8 * j + 8] + ac[8 * j:8 * j + 8] * hprev
                hs_ref[pl.ds(t0 + 8 * j, 8), :] = hj
                hprev = jnp.broadcast_to(hj[7:8], (8, c))
            h = hs_ref[pl.ds(t0, r), :]
            lo = h * _gelu(y_ref[pl.ds(t0, r), :])
            rs = lax.rsqrt(_group_mean(lo * lo, gmv) + EPS)
            m_ref[pl.ds(t0, r), :] = (lo * rs * gov).astype(BF)
            return hprev

        hp = chunk(0, jnp.zeros((8, c), F32), True)

        def loop(ci, hp):
            return chunk(pl.multiple_of(ci * r, 16), hp, False)

        lax.fori_loop(1, MIX_CHUNKS, loop, hp)

    col = lambda off: pl.BlockSpec((tp, c), lambda j: (0, off + j))
    vec = pl.BlockSpec((1, c), lambda j: (0, j))
    return _grid_call(
        body, "lru_fwd", (nb,),
        [col(0), col(nb), pl.BlockSpec((8, c), lambda j: (0, j)), vec, pl.BlockSpec((None, c, c), lambda j: (j, 0, 0)),
         vec, pl.BlockSpec((None, c, c), lambda j: (j, 0, 0)), vec, vec, vec, pl.BlockSpec((c, c), lambda j: (0, 0))],
        [col(0), col(0)], [jax.ShapeDtypeStruct((tp, dl), BF), jax.ShapeDtypeStruct((tp, dl), F32)],
        (z, z, w4, cb, wa2, ba, wx2, bx, lam, g_out, gm), side)


def _lru_bwd(z, hs, dmix, w4, cb, wa2, ba, wx2, bx, lam, g_out, gm, side=None):
    tp = z.shape[0]
    dl = cb.shape[1]
    nb = dl // LANE
    r = tp // MIX_CHUNKS
    c = LANE

    def body(y_ref, x_ref, hs_ref, dm_ref, w4_ref, cb_ref, wa_ref, ba_ref, wx_ref, bx_ref, lam_ref, go_ref, gm_ref,
             dy_ref, dx_ref, small_ref, dwa_ref, dwx_ref, xc_buf, ga_buf, gx_buf, a_buf, dh_buf, dxc_buf):
        w4v = w4_ref[...]
        cbv = cb_ref[...]
        wa = wa_ref[...]
        wx = wx_ref[...]
        bav = ba_ref[...]
        bxv = bx_ref[...]
        gov = go_ref[...]
        gmv = gm_ref[...]
        lamv = lam_ref[...]
        sp = _softplus(-lamv)
        small_ref[...] = jnp.zeros_like(small_ref)
        dwa_ref[...] = jnp.zeros_like(dwa_ref)
        dwx_ref[...] = jnp.zeros_like(dwx_ref)
        a_buf[pl.ds(tp, 8), :] = jnp.zeros((8, c), F32)
        dxc_buf[pl.ds(tp, 8), :] = jnp.zeros((8, c), F32)

        def fwd_chunk(t0, first):
            win = _window_dn(x_ref, t0, r, first)
            xc = _conv4(win, w4v, cbv, r)
            ga, gx, la = _lru_gates(xc, wa, bav, wx, bxv, sp)
            xc_buf[pl.ds(t0, r), :] = xc
            ga_buf[pl.ds(t0, r), :] = ga
            gx_buf[pl.ds(t0, r), :] = gx
            a_buf[pl.ds(t0, r), :] = jnp.exp(la)
            h = hs_ref[pl.ds(t0, r), :]
            yv = y_ref[pl.ds(t0, r), :]
            ge = _gelu(yv)
            lo = h * ge
            rs = lax.rsqrt(_group_mean(lo * lo, gmv) + EPS)
            xh = lo * rs
            dm = dm_ref[pl.ds(t0, r), :]
            q = dm * gov
            dlo = rs * (q - xh * _group_mean(q * xh, gmv))
            small_ref[8:9, :] += jnp.sum(dm * xh, axis=0, keepdims=True)
            dh_buf[pl.ds(t0, r), :] = dlo * ge
            dy_ref[pl.ds(t0, r), :] = (dlo * h * _gelu_grad(yv)).astype(BF)

        fwd_chunk(0, True)

        def floop(ci, carry):
            fwd_chunk(pl.multiple_of(ci * r, 16), False)
            return carry

        lax.fori_loop(1, MIX_CHUNKS, floop, 0)

        def bwd_chunk(t0, vnext, first):
            ap = _shift_up(a_buf[pl.ds(t0, r + 8), :], 1, r)
            ac, uc = _tile_scan(ap, dh_buf[pl.ds(t0, r), :], True)
            for j in reversed(range(r // 8)):
                vj = uc[8 * j:8 * j + 8] + ac[8 * j:8 * j + 8] * vnext
                dh_buf[pl.ds(t0 + 8 * j, 8), :] = vj
                vnext = jnp.broadcast_to(vj[0:1], (8, c))
            v = dh_buf[pl.ds(t0, r), :]
            hprev = _shift_dn(_window_dn(hs_ref, t0, r, first), 1, r)
            xc = xc_buf[pl.ds(t0, r), :]
            ga = ga_buf[pl.ds(t0, r), :]
            gx = gx_buf[pl.ds(t0, r), :]
            a = a_buf[pl.ds(t0, r), :]
            em = _neg_expm1(-2.0 * LRU_C * ga * sp)
            mult = jnp.sqrt(em)
            dla = v * hprev * a - (v * gx * xc) * ((1.0 - em) / mult)
            dgx = v * mult * xc
            dxc = v * mult * gx
            dga = dla * (-LRU_C) * sp
            small_ref[7:8, :] += jnp.sum(dla * (-LRU_C) * ga, axis=0, keepdims=True)
            dpa = dga * ga * (1.0 - ga)
            dpx = dgx * gx * (1.0 - gx)
            small_ref[5:6, :] += jnp.sum(dpa, axis=0, keepdims=True)
            small_ref[6:7, :] += jnp.sum(dpx, axis=0, keepdims=True)
            dpab = dpa.astype(BF)
            dpxb = dpx.astype(BF)
            xb = xc.astype(BF)
            dxc = dxc + _dot_nt(dpab, wa) + _dot_nt(dpxb, wx)
            dwa_ref[...] += _dot_tn(xb, dpab)
            dwx_ref[...] += _dot_tn(xb, dpxb)
            dxc_buf[pl.ds(t0, r), :] = dxc
            small_ref[4:5, :] += jnp.sum(dxc, axis=0, keepdims=True)
            dwin = dxc_buf[pl.ds(t0, r + 8), :]
            dx_ref[pl.ds(t0, r), :] = (w4v[3:4] * dxc + w4v[2:3] * _shift_up(dwin, 1, r)
                                       + w4v[1:2] * _shift_up(dwin, 2, r) + w4v[0:1] * _shift_up(dwin, 3, r)).astype(BF)
            xwin = _window_dn(x_ref, t0, r, first)
            for k in range(4):
                small_ref[k:k + 1, :] += jnp.sum(dxc * _shift_dn(xwin, 3 - k, r), axis=0, keepdims=True)
            return vnext

        def bloop(it, vnext):
            ci = MIX_CHUNKS - 1 - it
            return bwd_chunk(pl.multiple_of(ci * r, 16), vnext, False)

        vn = lax.fori_loop(0, MIX_CHUNKS - 1, bloop, jnp.zeros((8, c), F32))
        bwd_chunk(0, vn, True)
        small_ref[7:8, :] = small_ref[7:8, :] * (-_sigmoid(-lamv))

    col = lambda off: pl.BlockSpec((tp, c), lambda j: (0, off + j))
    vec = pl.BlockSpec((1, c), lambda j: (0, j))
    mat = pl.BlockSpec((None, c, c), lambda j: (j, 0, 0))
    buf = pltpu.VMEM((tp, c), F32)
    bufp = pltpu.VMEM((tp + 8, c), F32)
    return _grid_call(
        body, "lru_bwd", (nb,),
        [col(0), col(nb), col(0), col(0), pl.BlockSpec((8, c), lambda j: (0, j)), vec, mat, vec, mat, vec, vec, vec,
         pl.BlockSpec((c, c), lambda j: (0, 0))],
        [col(0), col(0), pl.BlockSpec((16, c), lambda j: (0, j)), mat, mat],
        [jax.ShapeDtypeStruct((tp, dl), BF), jax.ShapeDtypeStruct((tp, dl), BF), jax.ShapeDtypeStruct((16, dl), F32),
         jax.ShapeDtypeStruct((nb, c, c), F32), jax.ShapeDtypeStruct((nb, c, c), F32)],
        (z, z, hs, dmix, w4, cb, wa2, ba, wx2, bx, lam, g_out, gm), side, [buf, buf, buf, bufp, buf, bufp])


def _sc_conv(cvwin, w3, r):
    return w3[2:3] * _shift_dn(cvwin, 0, r) + w3[1:2] * _shift_dn(cvwin, 1, r) + w3[0:1] * _shift_dn(cvwin, 2, r)


def _sc_fwd(z, w3, g_out, gm, dl, side=None):
    tp = z.shape[0]
    nb = dl // LANE
    r = tp // MIX_CHUNKS
    c = LANE

    def body(b_ref, c_ref, v_ref, w3_ref, go_ref, gm_ref, m_ref):
        w3v = w3_ref[...]
        gov = go_ref[...]
        gmv = gm_ref[...]

        def chunk(t0, first):
            cvwin = _window_dn(c_ref, t0, r, first) * _window_dn(v_ref, t0, r, first)
            so = b_ref[pl.ds(t0, r), :] * _sc_conv(cvwin, w3v, r)
            rs = lax.rsqrt(_group_mean(so * so, gmv) + EPS)
            m_ref[pl.ds(t0, r), :] = (so * rs * gov).astype(BF)

        chunk(0, True)

        def loop(ci, carry):
            chunk(pl.multiple_of(ci * r, 16), False)
            return carry

        lax.fori_loop(1, MIX_CHUNKS, loop, 0)

    col = lambda off: pl.BlockSpec((tp, c), lambda j: (0, off + j))
    res, extra = _grid_call(
        body, "sconv_fwd", (nb,),
        [col(2 * nb), col(3 * nb), col(4 * nb), pl.BlockSpec((8, c), lambda j: (0, j)),
         pl.BlockSpec((1, c), lambda j: (0, j)), pl.BlockSpec((c, c), lambda j: (0, 0))],
        [col(0)], [jax.ShapeDtypeStruct((tp, dl), BF)], (z, z, z, w3, g_out, gm), side)
    return res[0], extra


def _sc_bwd(z, dmix, w3, g_out, gm, dl, side=None):
    tp = z.shape[0]
    nb = dl // LANE
    r = tp // MIX_CHUNKS
    c = LANE

    def body(b_ref, c_ref, v_ref, dm_ref, w3_ref, go_ref, gm_ref, db_ref, dc_ref, dv_ref, small_ref, dsc_buf):
        w3v = w3_ref[...]
        gov = go_ref[...]
        gmv = gm_ref[...]
        small_ref[...] = jnp.zeros_like(small_ref)
        dsc_buf[pl.ds(tp, 8), :] = jnp.zeros((8, c), F32)

        def chunk1(t0, first):
            cvwin = _window_dn(c_ref, t0, r, first) * _window_dn(v_ref, t0, r, first)
            sc = _sc_conv(cvwin, w3v, r)
            bv = b_ref[pl.ds(t0, r), :]
            so = bv * sc
            rs = lax.rsqrt(_group_mean(so * so, gmv) + EPS)
            xh = so * rs
            dm = dm_ref[pl.ds(t0, r), :]
            q = dm * gov
            dso = rs * (q - xh * _group_mean(q * xh, gmv))
            small_ref[3:4, :] += jnp.sum(dm * xh, axis=0, keepdims=True)
            db_ref[pl.ds(t0, r), :] = (dso * sc).astype(BF)
            dsc = dso * bv
            dsc_buf[pl.ds(t0, r), :] = dsc
            for k in range(3):
                small_ref[k:k + 1, :] += jnp.sum(dsc * _shift_dn(cvwin, 2 - k, r), axis=0, keepdims=True)

        chunk1(0, True)

        def loop1(ci, carry):
            chunk1(pl.multiple_of(ci * r, 16), False)
            return carry

        lax.fori_loop(1, MIX_CHUNKS, loop1, 0)

        def loop2(ci, carry):
            t0 = pl.multiple_of(ci * r, 16)
            dwin = dsc_buf[pl.ds(t0, r + 8), :]
            dcv = w3v[2:3] * _shift_up(dwin, 0, r) + w3v[1:2] * _shift_up(dwin, 1, r) + w3v[0:1] * _shift_up(dwin, 2, r)
            dc_ref[pl.ds(t0, r), :] = (dcv * v_ref[pl.ds(t0, r), :]).astype(BF)
            dv_ref[pl.ds(t0, r), :] = (dcv * c_ref[pl.ds(t0, r), :]).astype(BF)
            return carry

        lax.fori_loop(0, MIX_CHUNKS, loop2, 0)

    col = lambda off: pl.BlockSpec((tp, c), lambda j: (0, off + j))
    out = jax.ShapeDtypeStruct((tp, dl), BF)
    return _grid_call(
        body, "sconv_bwd", (nb,),
        [col(2 * nb), col(3 * nb), col(4 * nb), col(nb), pl.BlockSpec((8, c), lambda j: (0, j)),
         pl.BlockSpec((1, c), lambda j: (0, j)), pl.BlockSpec((c, c), lambda j: (0, 0))],
        [col(0), col(0), col(0), pl.BlockSpec((8, c), lambda j: (0, j))],
        [out, out, out, jax.ShapeDtypeStruct((8, dl), F32)], (z, z, z, dmix, w3, g_out, gm), side,
        [pltpu.VMEM((tp + 8, c), F32)])


def _cast_pad(w, rows_p, cols_p, chip, name):
    r, c = w.shape

    def body(chip_ref, w_ref, o_ref):
        if (rows_p, cols_p) != (r, c):
            o_ref[...] = jnp.zeros_like(o_ref)
        o_ref[0:r, 0:c] = w_ref[...].astype(BF)

    return pl.pallas_call(
        body, name=name, out_shape=jax.ShapeDtypeStruct((N_CHIP, rows_p, cols_p), BF),
        grid_spec=pltpu.PrefetchScalarGridSpec(
            num_scalar_prefetch=1, grid=(1,),
            in_specs=[pl.BlockSpec((r, c), lambda i, chip: (0, 0))],
            out_specs=pl.BlockSpec((None, rows_p, cols_p), lambda i, chip: (chip[0], 0, 0))),
        compiler_params=_params(("arbitrary",)),
    )(chip, w)


def _adamw_math(w, g, m, v):
    m2 = ADAM_B1 * m + (1.0 - ADAM_B1) * g
    v2 = ADAM_B2 * v + (1.0 - ADAM_B2) * (g * g)
    m_hat = m2 / (1.0 - ADAM_B1 ** ADAM_STEP)
    v_hat = v2 / (1.0 - ADAM_B2 ** ADAM_STEP)
    delta = -ADAM_LR * (m_hat / (jnp.sqrt(v_hat) + ADAM_EPS) + ADAM_WD * w)
    return delta, m2, v2


def _adamw(w, g, m, v, name, row_tiles, col_tiles, side=None):
    r, c = w.shape
    tr = r // row_tiles
    tc = c // col_tiles
    gc = g.shape[1] if col_tiles == 1 else tc

    def body(w_ref, g_ref, m_ref, v_ref, go_ref, d_ref, mo_ref, vo_ref):
        gv = g_ref[...][:, 0:tc]
        delta, m2, v2 = _adamw_math(w_ref[...], gv, m_ref[...], v_ref[...])
        go_ref[...] = gv
        d_ref[...] = delta
        mo_ref[...] = m2
        vo_ref[...] = v2

    spec = pl.BlockSpec((tr, tc), lambda i, j: (i, j))
    out = jax.ShapeDtypeStruct((r, c), F32)
    return _grid_call(body, name, (row_tiles, col_tiles), [spec, pl.BlockSpec((tr, gc), lambda i, j: (i, j)), spec, spec],
                      [spec] * 4, [out] * 4, (w, g, m, v), side)


def _adamw_small(w, g_top, g4, m, v):
    def body(w_ref, gt_ref, g_ref, m_ref, v_ref, go_ref, d_ref, mo_ref, vo_ref):
        g = jnp.concatenate([gt_ref[...], (g_ref[0] + g_ref[1]) + (g_ref[2] + g_ref[3])], axis=0)
        delta, m2, v2 = _adamw_math(w_ref[...], g, m_ref[...], v_ref[...])
        go_ref[...] = g
        d_ref[...] = delta
        mo_ref[...] = m2
        vo_ref[...] = v2

    out = jax.ShapeDtypeStruct(w.shape, F32)
    spec = pl.BlockSpec(w.shape, lambda: (0, 0))
    return pl.pallas_call(
        body, name="adamw_small",
        in_specs=[spec, pl.BlockSpec(g_top.shape, lambda: (0, 0)), pl.BlockSpec(g4.shape, lambda: (0, 0, 0)), spec, spec],
        out_specs=[spec] * 4, out_shape=[out] * 4, compiler_params=_params())(w, g_top, g4, m, v)


def _place():
    x, y, c = lax.axis_index("x"), lax.axis_index("y"), lax.axis_index("c")
    chips = [(1 - x, y), (x, 1 - y), (1 - x, 1 - y)]
    return x, y, c, chips


ANY = pl.BlockSpec(memory_space=pl.ANY)


REL_SLOT = (2, 1, 3)


def _gather_side(bufs, relative=False):
    n = len(bufs)

    def copies(outs, sems):
        s_ici, r_ici, s_d2d, r_d2d = sems
        x, y, c, chips = _place()
        me = 2 * x + y

        def rows(w, slot, core):
            half = bufs[w].shape[1] // 2
            return outs[w].at[slot, pl.ds(core * half, half)]

        def theirs(j):
            return REL_SLOT[j] if relative else 2 * chips[j][0] + chips[j][1]

        def ici_send(w, j):
            px, py = chips[j]
            return pltpu.make_async_remote_copy(
                src_ref=rows(w, 0 if relative else me, c), dst_ref=rows(w, REL_SLOT[j] if relative else me, c),
                send_sem=s_ici.at[w, j], recv_sem=r_ici.at[w, j], device_id=(px, py, c), device_id_type=MESH)

        def ici_recv(w, j):
            px, py = chips[j]
            return pltpu.make_async_remote_copy(
                src_ref=rows(w, theirs(j), c), dst_ref=rows(w, theirs(j), c),
                send_sem=s_ici.at[w, j], recv_sem=r_ici.at[w, j], device_id=(px, py, c), device_id_type=MESH)

        def d2d(w, j, core):
            return pltpu.make_async_remote_copy(
                src_ref=rows(w, theirs(j), core), dst_ref=rows(w, theirs(j), core),
                send_sem=s_d2d.at[w, j], recv_sem=r_d2d.at[w, j], device_id=(x, y, 1 - c), device_id_type=MESH)

        return c, ici_send, ici_recv, d2d

    def send(outs, sems):
        c, ici_send, ici_recv, d2d = copies(outs, sems)
        for j in range(3):
            for w in range(n):
                ici_send(w, j).start()

    def arrived(j, outs, sems):
        c, ici_send, ici_recv, d2d = copies(outs, sems)
        for w in range(n):
            ici_recv(w, j).wait_recv()
            d2d(w, j, c).start()

    def forwarded(j, outs, sems):
        c, ici_send, ici_recv, d2d = copies(outs, sems)
        for w in range(n):
            d2d(w, j, 1 - c).wait_recv()

    def drain(outs, sems):
        c, ici_send, ici_recv, d2d = copies(outs, sems)
        for j in range(3):
            for w in range(n):
                ici_send(w, j).wait_send()
                d2d(w, j, c).wait_send()

    def start(ins, outs, sems):
        send(outs, sems)

    def finish(ins, outs, sems):
        for j in range(3):
            arrived(j, outs, sems)
        for j in range(3):
            forwarded(j, outs, sems)
        drain(outs, sems)

    dma = pltpu.SemaphoreType.DMA((n, 3))
    side = _Side(list(bufs), [jax.ShapeDtypeStruct(b.shape, b.dtype) for b in bufs], {w: w for w in range(n)},
                 [dma, dma, dma, dma], start, finish)
    side.send, side.arrived, side.forwarded, side.drain = send, arrived, forwarded, drain
    return side


def _run_side(side, name):
    sin, sout = len(side.ins), len(side.outs)

    def body(*refs):
        ins, outs, sems = refs[:sin], refs[sin:sin + sout], refs[sin + sout:]
        side.start(ins, outs, sems)
        side.finish(ins, outs, sems)

    return pl.pallas_call(
        body, name=name, out_shape=list(side.outs), in_specs=[ANY] * sin, out_specs=[ANY] * sout,
        scratch_shapes=list(side.sems), input_output_aliases=dict(side.alias))(*side.ins)


def _pair_exchange_side(grads):
    n = len(grads)

    def copies(ins, outs, sems):
        ssem, rsem = sems
        x, y, c, _ = _place()
        cps = []
        for w in range(n):
            half = grads[w].shape[1] // 2
            cps.append(pltpu.make_async_remote_copy(
                src_ref=ins[w].at[:, pl.ds((1 - c) * half, half)], dst_ref=outs[w],
                send_sem=ssem.at[w], recv_sem=rsem.at[w], device_id=(x, y, 1 - c), device_id_type=MESH))
        return cps

    def start(ins, outs, sems):
        for cp in copies(ins, outs, sems):
            cp.start()

    def finish(ins, outs, sems):
        for cp in copies(ins, outs, sems):
            cp.wait()

    dma = pltpu.SemaphoreType.DMA((n,))
    return _Side(list(grads), [jax.ShapeDtypeStruct((N_CHIP, g.shape[1] // 2, g.shape[2]), BF) for g in grads], {},
                 [dma, dma], start, finish)


def _sibling_copy_side(buf):
    def copy(ins, outs, sems):
        x, y, c, _ = _place()
        return pltpu.make_async_remote_copy(src_ref=ins[0], dst_ref=outs[0], send_sem=sems[0], recv_sem=sems[1],
                                            device_id=(x, y, 1 - c), device_id_type=MESH)

    return _Side([buf], [jax.ShapeDtypeStruct(buf.shape, buf.dtype)], {}, [pltpu.SemaphoreType.DMA, pltpu.SemaphoreType.DMA],
                 lambda i, o, s: copy(i, o, s).start(), lambda i, o, s: copy(i, o, s).wait())


def _slot_exchange_side(buf4):
    def copies(outs, sems, sending):
        ssem, rsem = sems
        x, y, c, chips = _place()
        me = 2 * x + y
        return [pltpu.make_async_remote_copy(
            src_ref=outs[0].at[me if sending else 2 * px + py], dst_ref=outs[0].at[me if sending else 2 * px + py],
            send_sem=ssem.at[j], recv_sem=rsem.at[j], device_id=(px, py, c), device_id_type=MESH)
            for j, (px, py) in enumerate(chips)]

    def start(ins, outs, sems):
        for cp in copies(outs, sems, True):
            cp.start()

    def finish(ins, outs, sems):
        for cp in copies(outs, sems, False):
            cp.wait_recv()
        for cp in copies(outs, sems, True):
            cp.wait_send()

    dma = pltpu.SemaphoreType.DMA((3,))
    return _Side([buf4], [jax.ShapeDtypeStruct(buf4.shape, buf4.dtype)], {0: 0}, [dma, dma], start, finish)


def _pair_sum(g, sib, core, name):
    _, r, cdim = g.shape
    half = r // 2

    def body(core_ref, g_ref, s_ref, o_ref):
        o_ref[...] = (g_ref[...].astype(F32) + s_ref[...].astype(F32)).astype(BF)

    return pl.pallas_call(
        body, name=name,
        grid_spec=pltpu.PrefetchScalarGridSpec(
            num_scalar_prefetch=1, grid=(N_CHIP,),
            in_specs=[pl.BlockSpec((None, half, cdim), lambda k, core: (k, core[0], 0)),
                      pl.BlockSpec((None, half, cdim), lambda k, core: (k, 0, 0))],
            out_specs=pl.BlockSpec((None, half, cdim), lambda k, core: (k, 0, 0))),
        out_shape=jax.ShapeDtypeStruct((N_CHIP, half, cdim), BF),
        compiler_params=_params(("arbitrary",)),
    )(core, g, sib)


def _chip_exchange_side(psums, relative=False):
    n = len(psums)

    def copies(ins, outs, sems):
        ssem, rsem = sems
        x, y, c, chips = _place()
        return [pltpu.make_async_remote_copy(
            src_ref=ins[w].at[REL_SLOT[j] if relative else 2 * px + py], dst_ref=outs[w].at[j],
            send_sem=ssem.at[w, j], recv_sem=rsem.at[w, j], device_id=(px, py, c), device_id_type=MESH)
            for w in range(n) for j, (px, py) in enumerate(chips)]

    def start(ins, outs, sems):
        for cp in copies(ins, outs, sems):
            cp.start()

    def finish(ins, outs, sems):
        for cp in copies(ins, outs, sems):
            cp.wait()

    dma = pltpu.SemaphoreType.DMA((n, 3))
    return _Side(list(psums), [jax.ShapeDtypeStruct((3,) + p.shape[1:], BF) for p in psums], {}, [dma, dma],
                 start, finish)


def _final_sum(g, sib, recv, sel, name):
    _, r, cdim = g.shape
    half = r // 2
    nt = 4
    th = half // nt

    def body(sel_ref, g_ref, s_ref, r_ref, o_ref):
        acc = g_ref[...].astype(F32) + s_ref[...].astype(F32)
        for j in range(3):
            acc = acc + r_ref[j].astype(F32)
        o_ref[...] = acc

    return pl.pallas_call(
        body, name=name,
        grid_spec=pltpu.PrefetchScalarGridSpec(
            num_scalar_prefetch=1, grid=(nt,),
            in_specs=[pl.BlockSpec((None, th, cdim), lambda i, sel: (sel[0], sel[1] * nt + i, 0)),
                      pl.BlockSpec((None, th, cdim), lambda i, sel: (sel[0], i, 0)),
                      pl.BlockSpec((3, th, cdim), lambda i, sel: (0, i, 0))],
            out_specs=pl.BlockSpec((th, cdim), lambda i, sel: (sel[1] * nt + i, 0))),
        out_shape=jax.ShapeDtypeStruct((r, cdim), F32),
        compiler_params=_params(("arbitrary",)),
    )(sel, g, sib, recv)


def _join_side(bufs):
    n = len(bufs)

    def copies(outs, sems, core_of):
        ssem, rsem = sems
        x, y, c, _ = _place()
        cps = []
        for w in range(n):
            half = bufs[w].shape[0] // 2
            rows = outs[w].at[pl.ds(core_of(c) * half, half)]
            cps.append(pltpu.make_async_remote_copy(
                src_ref=rows, dst_ref=rows, send_sem=ssem.at[w], recv_sem=rsem.at[w],
                device_id=(x, y, 1 - c), device_id_type=MESH))
        return cps

    def start(ins, outs, sems):
        for cp in copies(outs, sems, lambda c: c):
            cp.start()

    def finish(ins, outs, sems):
        for cp in copies(outs, sems, lambda c: 1 - c):
            cp.wait_recv()
        for cp in copies(outs, sems, lambda c: c):
            cp.wait_send()

    dma = pltpu.SemaphoreType.DMA((n,))
    return _Side(list(bufs), [jax.ShapeDtypeStruct(b.shape, F32) for b in bufs], {w: w for w in range(n)}, [dma, dma],
                 start, finish)


def _small_pair_sum(buf, sib, chip):
    rows, d = buf.shape

    def body(chip_ref, a_ref, b_ref, o_ref):
        o_ref[...] = a_ref[...] + b_ref[...]

    return pl.pallas_call(
        body, name="small_pair_sum", out_shape=jax.ShapeDtypeStruct((N_CHIP, rows, d), F32),
        grid_spec=pltpu.PrefetchScalarGridSpec(
            num_scalar_prefetch=1, grid=(1,),
            in_specs=[pl.BlockSpec((rows, d), lambda i, chip: (0, 0))] * 2,
            out_specs=pl.BlockSpec((None, rows, d), lambda i, chip: (chip[0], 0, 0))),
        compiler_params=_params(("arbitrary",)),
    )(chip, buf, sib)


def _small_all_reduce(buf, name):
    rows, d = buf.shape

    def body(in_ref, out_ref, sib, all4, ssem, rsem, psem, qsem):
        x, y, c, chips = _place()
        me = 2 * x + y
        to_sib = pltpu.make_async_remote_copy(src_ref=in_ref, dst_ref=sib, send_sem=ssem, recv_sem=rsem,
                                              device_id=(x, y, 1 - c), device_id_type=MESH)
        to_sib.start()
        to_sib.wait()
        all4[me] = in_ref[...] + sib[...]
        cps = [pltpu.make_async_remote_copy(src_ref=all4.at[me], dst_ref=all4.at[me], send_sem=psem.at[j],
                                            recv_sem=qsem.at[j], device_id=(px, py, c), device_id_type=MESH)
               for j, (px, py) in enumerate(chips)]
        for cp in cps:
            cp.start()
        for j, (px, py) in enumerate(chips):
            chip = 2 * px + py
            pltpu.make_async_remote_copy(src_ref=all4.at[chip], dst_ref=all4.at[chip], send_sem=psem.at[j],
                                         recv_sem=qsem.at[j], device_id=(px, py, c), device_id_type=MESH).wait_recv()
        for cp in cps:
            cp.wait_send()
        out_ref[...] = (all4[0] + all4[1]) + (all4[2] + all4[3])

    vm = pl.BlockSpec(memory_space=pltpu.VMEM)
    return pl.pallas_call(
        body, name=name, out_shape=jax.ShapeDtypeStruct((rows, d), F32),
        in_specs=[vm], out_specs=vm,
        scratch_shapes=[pltpu.VMEM((rows, d), F32), pltpu.VMEM((N_CHIP, rows, d), F32),
                        pltpu.SemaphoreType.DMA, pltpu.SemaphoreType.DMA,
                        pltpu.SemaphoreType.DMA((3,)), pltpu.SemaphoreType.DMA((3,))],
        compiler_params=_params(),
    )(buf)


def _pair_blocks(w):
    w4 = w.reshape(N_HEADS // 2, 2, HEAD, HEAD)
    eye = jnp.eye(2, dtype=w.dtype)
    return jnp.einsum("pirc,ij->pirjc", w4, eye).reshape(N_HEADS // 2, LANE, LANE)


def _unpair_blocks(w2):
    w5 = w2.reshape(N_HEADS // 2, 2, HEAD, 2, HEAD)
    return jnp.stack([w5[:, 0, :, 0, :], w5[:, 1, :, 1, :]], axis=1).reshape(N_HEADS, HEAD, HEAD)


def kernel(x, meta_tokens, ffn1_pre_g, ffn1_w_gate, ffn1_w_up, ffn1_w_down, ffn1_post_g, mix_pre_g, w_in, lru_conv_w, lru_conv_b, lru_w_a, lru_b_a, lru_w_x, lru_b_x, lru_lambda, sconv_w, lru_out_g, sconv_out_g, w_out, mix_post_g, ffn2_pre_g, ffn2_w_gate, ffn2_w_up, ffn2_w_down, ffn2_post_g, loss_target, m_meta_tokens, m_ffn1_pre_g, m_ffn1_w_gate, m_ffn1_w_up, m_ffn1_w_down, m_ffn1_post_g, m_mix_pre_g, m_w_in, m_lru_conv_w, m_lru_conv_b, m_lru_w_a, m_lru_b_a, m_lru_w_x, m_lru_b_x, m_lru_lambda, m_sconv_w, m_lru_out_g, m_sconv_out_g, m_w_out, m_mix_post_g, m_ffn2_pre_g, m_ffn2_w_gate, m_ffn2_w_up, m_ffn2_w_down, m_ffn2_post_g, v_meta_tokens, v_ffn1_pre_g, v_ffn1_w_gate, v_ffn1_w_up, v_ffn1_w_down, v_ffn1_post_g, v_mix_pre_g, v_w_in, v_lru_conv_w, v_lru_conv_b, v_lru_w_a, v_lru_b_a, v_lru_w_x, v_lru_b_x, v_lru_lambda, v_sconv_w, v_lru_out_g, v_sconv_out_g, v_w_out, v_mix_post_g, v_ffn2_pre_g, v_ffn2_w_gate, v_ffn2_w_up, v_ffn2_w_down, v_ffn2_post_g):
    seq, d = x.shape[1], x.shape[2]
    t_real = N_META + seq
    tp = _round_up(t_real, ROW_ALIGN)
    f4 = ffn1_w_gate.shape[2]
    f4p = _round_up(f4, LANE)
    dl = lru_conv_b.shape[1]
    cin = w_in.shape[2]
    xi, yi, ci = lax.axis_index("x"), lax.axis_index("y"), lax.axis_index("c")
    chip = 2 * xi + yi
    zero = jnp.zeros((), jnp.int32)

    transposed = ("ffn1_w_gate", "ffn1_w_up", "ffn2_w_gate", "ffn2_w_up")

    def view(k, a):
        return a[0].T if k in transposed else a[0]

    def unview(k, a):
        return (a.T if k in transposed else a)[None]

    big = {
        "ffn1_w_gate": (view("ffn1_w_gate", ffn1_w_gate), f4p, d), "ffn1_w_up": (view("ffn1_w_up", ffn1_w_up), f4p, d),
        "ffn1_w_down": (ffn1_w_down[0], f4p, d), "w_in": (w_in[0], d, cin), "w_out": (w_out[0], w_out.shape[1], d),
        "ffn2_w_gate": (view("ffn2_w_gate", ffn2_w_gate), f4p, d), "ffn2_w_up": (view("ffn2_w_up", ffn2_w_up), f4p, d),
        "ffn2_w_down": (ffn2_w_down[0], f4p, d),
    }
    names = list(big)
    chip1 = jnp.reshape(chip, (1,)).astype(jnp.int32)
    relative = {k: k.startswith("ffn") for k in names}
    slot0 = jnp.zeros((1,), jnp.int32)
    shard = {k: _cast_pad(big[k][0], big[k][1], big[k][2], slot0 if relative[k] else chip1, "cast_" + k) for k in names}
    full = {}

    def gather(*keys):
        return _merge_sides([_gather_side([shard[k]], relative[k]) for k in keys])

    gm = jnp.kron(jnp.eye(2, dtype=F32), jnp.full((HEAD, HEAD), 1.0 / HEAD, F32)).astype(BF)
    wa2 = _pair_blocks(lru_w_a[0])
    wx2 = _pair_blocks(lru_w_x[0])

    dlq = dl // N_CHIP
    dq = d // N_CHIP
    R_GAIN, R_LOSS, R_META, R_LRU, R_SC, R_WA = 0, 6, 8, 24, 40, 48
    n_wrows = (N_HEADS // 2) * LANE * LANE // d
    R_WX = R_WA + n_wrows
    R_END = R_WX + n_wrows

    def pack_top(gains, meta, loss=None):
        lossrow = jnp.zeros((2, d), F32)
        if loss is not None:
            lossrow = lossrow.at[0, 0].set(loss)
        return jnp.concatenate([jnp.concatenate(gains, axis=0), lossrow, meta], axis=0)

    def pack_rest(lru16, sc8, wa_, wx_):
        return jnp.concatenate([jnp.concatenate([lru16, jnp.zeros((16, d - dl), F32)], axis=1),
                                jnp.concatenate([sc8, jnp.zeros((8, d - dl), F32)], axis=1),
                                wa_.reshape(n_wrows, d), wx_.reshape(n_wrows, d)], axis=0)

    def pack(gains, meta, lru16, sc8, wa_, wx_):
        return jnp.concatenate([pack_top(gains, meta), pack_rest(lru16, sc8, wa_, wx_)], axis=0)

    def place_cols(blk, width, total):
        return lax.dynamic_update_slice(jnp.zeros((blk.shape[0], total), F32), blk, (zero, chip * width))

    def pack_params(meta_, g1pre, g1post, gmpre, gmpost, g2pre, g2post, cw, cbias, wa_, ba_, wx_, bx_, lam_, sw, lgo, sgo):
        lru16 = jnp.concatenate([place_cols(cw[0], dlq, dl), cbias, ba_, bx_, lam_, lgo, jnp.zeros((7, dl), F32)], axis=0)
        sc8 = jnp.concatenate([place_cols(sw[0], dlq, dl), sgo, jnp.zeros((4, dl), F32)], axis=0)
        return pack([g1pre, g1post, gmpre, gmpost, g2pre, g2post], place_cols(meta_, dq, d), lru16, sc8,
                    _pair_blocks(wa_[0]), _pair_blocks(wx_[0]))

    p_w = pack_params(meta_tokens, ffn1_pre_g, ffn1_post_g, mix_pre_g, mix_post_g, ffn2_pre_g, ffn2_post_g, lru_conv_w,
                      lru_conv_b, lru_w_a, lru_b_a, lru_w_x, lru_b_x, lru_lambda, sconv_w, lru_out_g, sconv_out_g)
    p_m = pack_params(m_meta_tokens, m_ffn1_pre_g, m_ffn1_post_g, m_mix_pre_g, m_mix_post_g, m_ffn2_pre_g, m_ffn2_post_g,
                      m_lru_conv_w, m_lru_conv_b, m_lru_w_a, m_lru_b_a, m_lru_w_x, m_lru_b_x, m_lru_lambda, m_sconv_w,
                      m_lru_out_g, m_sconv_out_g)
    p_v = pack_params(v_meta_tokens, v_ffn1_pre_g, v_ffn1_post_g, v_mix_pre_g, v_mix_post_g, v_ffn2_pre_g, v_ffn2_post_g,
                      v_lru_conv_w, v_lru_conv_b, v_lru_w_a, v_lru_b_a, v_lru_w_x, v_lru_b_x, v_lru_lambda, v_sconv_w,
                      v_lru_out_g, v_sconv_out_g)

    gathered = _small_all_reduce(jnp.where(ci == 0, p_w, 0.0)[R_META:R_WA], "small_weight_gather")
    meta_full = gathered[0:N_META]
    w4_full = gathered[R_LRU - R_META:R_LRU - R_META + 4, 0:dl]
    w3_full = gathered[R_SC - R_META:R_SC - R_META + 3, 0:dl]
    w4p = jnp.concatenate([w4_full, jnp.zeros((4, dl), F32)], axis=0)
    w3p = jnp.concatenate([w3_full, jnp.zeros((5, dl), F32)], axis=0)

    h0 = jnp.concatenate([meta_full, x[0], jnp.zeros((tp - t_real, d), F32)], axis=0)
    tgt = jnp.concatenate([jnp.zeros((N_META, d), F32), loss_target[0], jnp.zeros((tp - t_real, d), F32)], axis=0)

    n1 = _norm0(h0, ffn1_pre_g)
    (a1, b1, s1), (full["ffn1_w_gate"], full["ffn1_w_up"]), got = _ffn_up_head(
        n1, shard["ffn1_w_gate"], shard["ffn1_w_up"], "ffn1_up", gather("ffn1_w_down"))
    full["ffn1_w_down"] = got[0]
    f1, got = _row_matmul([(s1, full["ffn1_w_down"])], "ffn1_down", False, d, gather("w_in"))
    full["w_in"] = got[0]
    h1, u = _post_fwd(f1, h0, ffn1_post_g, mix_pre_g, 0.5, "ffn1_post")
    z, got = _col_matmul(u, full["w_in"], "in_proj", False, F32, gather("ffn2_w_gate"))
    full["ffn2_w_gate"] = got[0]
    (m_lru, hs), got = _lru_fwd(z, w4p, lru_conv_b, wa2.astype(BF), lru_b_a, wx2.astype(BF), lru_b_x, lru_lambda,
                                lru_out_g, gm, gather("ffn2_w_up"))
    full["ffn2_w_up"] = got[0]
    m_sc, got = _sc_fwd(z, w3p, sconv_out_g, gm, dl, gather("w_out"))
    full["w_out"] = got[0]
    mixed = jnp.concatenate([m_lru, m_sc], axis=1)
    p, _ = _row_matmul([(mixed, full["w_out"])], "out_proj", False, d)
    h2, n2 = _post_fwd(p, h1, mix_post_g, ffn2_pre_g, 1.0, "mix_post")
    (a2, b2, s2), got = _ffn_up(n2, full["ffn2_w_gate"], full["ffn2_w_up"], "ffn2_up", gather("ffn2_w_down"))
    full["ffn2_w_down"] = got[0]
    f2, _ = _row_matmul([(s2, full["ffn2_w_down"])], "ffn2_down", False, d)
    dh3, df2, dg_ffn2_post, loss_part = _loss_bwd(f2, h2, tgt, ffn2_post_g, t_real)

    core = jnp.reshape(ci, (1,)).astype(jnp.int32)
    sel_of = {False: jnp.stack([chip, ci]).astype(jnp.int32), True: jnp.stack([0 * chip, ci]).astype(jnp.int32)}
    red = {}

    def pair_side(k):
        return _pair_exchange_side([red[k][0]])

    def chip_side(k):
        return _chip_exchange_side([_pair_sum(red[k][0], red[k][1], core, "pair_sum_" + k)], relative[k])

    def final_sum(k):
        return _final_sum(*red[k], sel_of[relative[k]], "final_sum_" + k)

    (da2, db2), _ = _ffn_bwd_act(df2, full["ffn2_w_down"], a2, b2, "ffn2_bwd_act")
    g, _ = _wgrad_call(s2, df2, "ffn2_down_wgrad", tile_y=WGRAD_TILE_Y)
    red["ffn2_w_down"] = [g, None, None]
    g, got = _wgrad_call(da2, n2, "ffn2_gate_wgrad", tile_y=WGRAD_TILE_Y, side=pair_side("ffn2_w_down"))
    red["ffn2_w_down"][1] = got[0]
    red["ffn2_w_gate"] = [g, None, None]
    g, got = _wgrad_call(db2, n2, "ffn2_up_wgrad", tile_y=WGRAD_TILE_Y,
                         side=_merge_sides([pair_side("ffn2_w_gate"), chip_side("ffn2_w_down")]))
    red["ffn2_w_gate"][1], red["ffn2_w_down"][2] = got
    red["ffn2_w_up"] = [g, None, None]
    dn2, got = _row_matmul([(da2, full["ffn2_w_gate"]), (db2, full["ffn2_w_up"])], "ffn2_bwd_up", False, d,
                           _merge_sides([pair_side("ffn2_w_up"), chip_side("ffn2_w_gate")]), tiles=MM_TILES)
    red["ffn2_w_up"][1], red["ffn2_w_gate"][2] = got
    dh2, dp, dg_ffn2_pre, dg_mix_post = _pre_bwd(dn2, h2, dh3, ffn2_pre_g, "ffn2_pre_bwd", (p, mix_post_g, 1.0))
    dmixed, _ = _col_matmul(dp, full["w_out"], "out_proj_bwd", True, F32)
    g, _ = _wgrad_call(mixed, dp, "w_out_wgrad", x_width=mixed.shape[1] // N_CHIP, tile_y=WGRAD_TILE_Y)
    red["w_out"] = [g, None, None]
    (dzy, dzx, lru_small, dwa2, dwx2), got = _lru_bwd(
        z, hs, dmixed, w4p, lru_conv_b, wa2.astype(BF), lru_b_a, wx2.astype(BF), lru_b_x, lru_lambda, lru_out_g, gm,
        _merge_sides([pair_side("w_out"), chip_side("ffn2_w_up")]))
    red["w_out"][1], red["ffn2_w_up"][2] = got
    (dzb, dzc, dzv, sc_small), got = _sc_bwd(z, dmixed, w3p, sconv_out_g, gm, dl, chip_side("w_out"))
    red["w_out"][2] = got[0]
    dz = jnp.concatenate([dzy, dzx, dzb, dzc, dzv], axis=1)
    p_rest = pack_rest(lru_small, sc_small, dwa2, dwx2)
    g, got = _wgrad_call(u, dz, "w_in_wgrad", y_width=cin, tile_x=WGRAD_TILE_X, side=_sibling_copy_side(p_rest))
    p_rest4 = _small_pair_sum(p_rest, got[0], chip1)
    red["w_in"] = [g, None, None]
    du, got = _row_matmul([(dz, full["w_in"])], "in_proj_bwd", True, d,
                          _merge_sides([pair_side("w_in"), _slot_exchange_side(p_rest4)]))
    red["w_in"][1], p_rest4 = got
    dh1, df1, dg_mix_pre, dg_ffn1_post = _pre_bwd(du, h1, dh2, mix_pre_g, "mix_pre_bwd", (f1, ffn1_post_g, 0.5))
    (da1, db1), got = _ffn_bwd_act(df1, full["ffn1_w_down"], a1, b1, "ffn1_bwd_act", chip_side("w_in"))
    red["w_in"][2] = got[0]
    early = ["ffn2_w_down", "ffn2_w_gate", "ffn2_w_up", "w_out", "w_in"]
    late = ["ffn1_w_down", "ffn1_w_gate", "ffn1_w_up"]
    g, got = _wgrad_call(s1, df1, "ffn1_down_wgrad", tile_y=WGRAD_TILE_Y,
                         side=_join_side([final_sum(k) for k in early]))
    gfull = dict(zip(early, got))
    red["ffn1_w_down"] = [g, None, None]
    g, got = _wgrad_call(da1, n1, "ffn1_gate_wgrad", tile_y=WGRAD_TILE_Y, side=pair_side("ffn1_w_down"))
    red["ffn1_w_down"][1] = got[0]
    red["ffn1_w_gate"] = [g, None, None]
    g, got = _wgrad_call(db1, n1, "ffn1_up_wgrad", tile_y=WGRAD_TILE_Y,
                         side=_merge_sides([pair_side("ffn1_w_gate"), chip_side("ffn1_w_down")]))
    red["ffn1_w_gate"][1], red["ffn1_w_down"][2] = got
    red["ffn1_w_up"] = [g, None, None]
    red["ffn1_w_up"][1] = _run_side(pair_side("ffn1_w_up"), "pair_exchange_ffn1_w_up")[0]
    dn1, got = _row_matmul([(da1, full["ffn1_w_gate"]), (db1, full["ffn1_w_up"])], "ffn1_bwd_up", False, d,
                           _merge_sides([chip_side("ffn1_w_gate"), chip_side("ffn1_w_up")]), tiles=MM_TILES)
    red["ffn1_w_gate"][2], red["ffn1_w_up"][2] = got
    (dh0, dg_ffn1_pre), got = _pre_bwd(dn1, h0, dh1, ffn1_pre_g, "ffn1_pre_bwd",
                                       side=_join_side([final_sum(k) for k in late]))
    gfull.update(zip(late, got))

    grad_x = dh0[N_META:t_real][None]

    w_big = {"ffn1_w_gate": ffn1_w_gate, "ffn1_w_up": ffn1_w_up, "ffn1_w_down": ffn1_w_down, "w_in": w_in, "w_out": w_out,
             "ffn2_w_gate": ffn2_w_gate, "ffn2_w_up": ffn2_w_up, "ffn2_w_down": ffn2_w_down}
    m_big = {"ffn1_w_gate": m_ffn1_w_gate, "ffn1_w_up": m_ffn1_w_up, "ffn1_w_down": m_ffn1_w_down, "w_in": m_w_in,
             "w_out": m_w_out, "ffn2_w_gate": m_ffn2_w_gate, "ffn2_w_up": m_ffn2_w_up, "ffn2_w_down": m_ffn2_w_down}
    v_big = {"ffn1_w_gate": v_ffn1_w_gate, "ffn1_w_up": v_ffn1_w_up, "ffn1_w_down": v_ffn1_w_down, "w_in": v_w_in,
             "w_out": v_w_out, "ffn2_w_gate": v_ffn2_w_gate, "ffn2_w_up": v_ffn2_w_up, "ffn2_w_down": v_ffn2_w_down}
    b_grad, b_delta, b_newm, b_newv = {}, {}, {}, {}

    def big_adamw(k, side=None):
        wv, mv, vv = view(k, w_big[k]), view(k, m_big[k]), view(k, v_big[k])
        wide_rows = wv.shape[0] % 64 == 0
        (g_, d_, m_, v_), got = _adamw(wv, gfull[k], mv, vv, "adamw_" + k, 8 if wide_rows else 4, 1 if wide_rows else 2,
                                       side)
        b_grad[k], b_delta[k], b_newm[k], b_newv[k] = unview(k, g_), unview(k, d_), unview(k, m_), unview(k, v_)
        return got

    p_top = _small_all_reduce(
        pack_top([dg_ffn1_pre, dg_ffn1_post, dg_mix_pre, dg_mix_post, dg_ffn2_pre, dg_ffn2_post], dh0[0:N_META],
                 loss=loss_part[0, 0]), "small_grad_all_reduce")
    p_g, p_delta, p_newm, p_newv = _adamw_small(p_w, p_top, p_rest4, p_m, p_v)
    loss = p_g[R_LOSS, 0]
    for k in names:
        big_adamw(k)

    def unpack(buf):
        out = {}
        for i, k in enumerate(["ffn1_pre_g", "ffn1_post_g", "mix_pre_g", "mix_post_g", "ffn2_pre_g", "ffn2_post_g"]):
            out[k] = buf[R_GAIN + i:R_GAIN + i + 1]
        out["meta_tokens"] = lax.dynamic_slice(buf[R_META:R_META + N_META], (zero, chip * dq), (N_META, dq))
        lru = buf[R_LRU:R_LRU + 16, 0:dl]
        out["lru_conv_w"] = lax.dynamic_slice(lru[0:4], (zero, chip * dlq), (4, dlq))[None]
        out["lru_conv_b"] = lru[4:5]
        out["lru_b_a"] = lru[5:6]
        out["lru_b_x"] = lru[6:7]
        out["lru_lambda"] = lru[7:8]
        out["lru_out_g"] = lru[8:9]
        sc = buf[R_SC:R_SC + 8, 0:dl]
        out["sconv_w"] = lax.dynamic_slice(sc[0:3], (zero, chip * dlq), (3, dlq))[None]
        out["sconv_out_g"] = sc[3:4]
        out["lru_w_a"] = _unpair_blocks(buf[R_WA:R_WX].reshape(N_HEADS // 2, LANE, LANE))[None]
        out["lru_w_x"] = _unpair_blocks(buf[R_WX:R_END].reshape(N_HEADS // 2, LANE, LANE))[None]
        return out

    s_grad, s_delta, s_newm, s_newv = unpack(p_g), unpack(p_delta), unpack(p_newm), unpack(p_newv)

    order = ["meta_tokens", "ffn1_pre_g", "ffn1_w_gate", "ffn1_w_up", "ffn1_w_down", "ffn1_post_g", "mix_pre_g", "w_in",
             "lru_conv_w", "lru_conv_b", "lru_w_a", "lru_b_a", "lru_w_x", "lru_b_x", "lru_lambda", "sconv_w", "lru_out_g",
             "sconv_out_g", "w_out", "mix_post_g", "ffn2_pre_g", "ffn2_w_gate", "ffn2_w_up", "ffn2_w_down", "ffn2_post_g"]

    def pick(small, bigd):
        return [bigd[k] if k in bigd else small[k] for k in order]

    return (loss, grad_x, *pick(s_grad, b_grad), *pick(s_delta, b_delta), *pick(s_newm, b_newm), *pick(s_newv, b_newv))
```

```python
import functools
import math

import jax
import jax.numpy as jnp
from jax import lax
from jax.experimental import pallas as pl
from jax.experimental.pallas import tpu as pltpu

F32 = jnp.float32
BF = jnp.bfloat16
MESH = pl.DeviceIdType.MESH

EPS = 1e-6
N_META = 16
N_HEADS = 16
HEAD = 64
LRU_C = 8.0
LANE = 128
MXU_COLS = 256
N_CHIP = 4
ROW_ALIGN = 384
MM_TILES = 8
MM_TILES_BIG = 4
EW_TILES = 12
MIX_CHUNKS = 24
WGRAD_TILE_X = 256
WGRAD_TILE_Y = 512
VMEM_LIMIT = 56 << 20

ADAM_LR = 0.001
ADAM_B1 = 0.9
ADAM_B2 = 0.999
ADAM_EPS = 1e-08
ADAM_WD = 0.01
ADAM_STEP = 10


def _round_up(a, b):
    return (a + b - 1) // b * b


def _params(sem=None):
    if sem is None:
        return pltpu.CompilerParams(vmem_limit_bytes=VMEM_LIMIT)
    return pltpu.CompilerParams(dimension_semantics=sem, vmem_limit_bytes=VMEM_LIMIT)


def _sigmoid(x):
    return 0.5 * jnp.tanh(0.5 * x) + 0.5


def _dot(a, b):
    return jnp.dot(a, b, preferred_element_type=F32)


def _dot_nt(a, b):
    return lax.dot_general(a, b, (((1,), (1,)), ((), ())), preferred_element_type=F32)


def _dot_tn(a, b):
    return lax.dot_general(a, b, (((0,), (0,)), ((), ())), preferred_element_type=F32)


def _rms(x, g):
    r = lax.rsqrt(jnp.mean(x * x, axis=-1, keepdims=True) + EPS)
    return x * r * g


def _rms_bwd(x, g, dy):
    r = lax.rsqrt(jnp.mean(x * x, axis=-1, keepdims=True) + EPS)
    xh = x * r
    q = dy * g
    dx = r * (q - xh * jnp.mean(q * xh, axis=-1, keepdims=True))
    return dx, dy * xh


class _Side:
    def __init__(self, ins, outs, alias, sems, start, finish):
        self.ins, self.outs, self.alias, self.sems, self.start, self.finish = ins, outs, alias, sems, start, finish


def _merge_sides(sides):
    sides = [s for s in sides if s is not None]
    if len(sides) <= 1:
        return sides[0] if sides else None
    ins, outs, sems, alias, spans = [], [], [], {}, []
    for s in sides:
        for i, o in s.alias.items():
            alias[len(ins) + i] = len(outs) + o
        spans.append((len(ins), len(ins) + len(s.ins), len(outs), len(outs) + len(s.outs), len(sems),
                      len(sems) + len(s.sems)))
        ins += list(s.ins)
        outs += list(s.outs)
        sems += list(s.sems)

    def run(which):
        def go(in_refs, out_refs, sem_refs):
            for s, (a, b, c, d, e, f) in zip(sides, spans):
                getattr(s, which)(in_refs[a:b], out_refs[c:d], sem_refs[e:f])
        return go

    return _Side(ins, outs, alias, sems, run("start"), run("finish"))


def _grid_call(body, name, grid, in_specs, out_specs, out_shape, args, side=None, scratch=()):
    sem = ("arbitrary",) * len(grid)
    if side is None:
        res = pl.pallas_call(body, name=name, grid=grid, in_specs=in_specs, out_specs=out_specs, out_shape=out_shape,
                             scratch_shapes=list(scratch), compiler_params=_params(sem))(*args)
        return res, []
    nin, nout, sin, sout = len(in_specs), len(out_specs), len(side.ins), len(side.outs)
    nscr = len(scratch)

    def full(*refs):
        base_in, side_in = refs[:nin], refs[nin:nin + sin]
        base_out = refs[nin + sin:nin + sin + nout]
        side_out = refs[nin + sin + nout:nin + sin + nout + sout]
        base_scr = refs[nin + sin + nout + sout:nin + sin + nout + sout + nscr]
        sems = refs[nin + sin + nout + sout + nscr:]
        first = pl.program_id(0) == 0
        last = pl.program_id(0) == grid[0] - 1
        for ax in range(1, len(grid)):
            first = first & (pl.program_id(ax) == 0)
            last = last & (pl.program_id(ax) == grid[ax] - 1)

        @pl.when(first)
        def _():
            side.start(side_in, side_out, sems)

        body(*base_in, *base_out, *base_scr)

        @pl.when(last)
        def _():
            side.finish(side_in, side_out, sems)

    any_spec = pl.BlockSpec(memory_space=pl.ANY)
    res = pl.pallas_call(
        full, name=name, grid=grid, in_specs=list(in_specs) + [any_spec] * sin,
        out_specs=list(out_specs) + [any_spec] * sout, out_shape=list(out_shape) + list(side.outs),
        scratch_shapes=list(scratch) + list(side.sems),
        input_output_aliases={nin + i: nout + o for i, o in side.alias.items()},
        compiler_params=_params(sem))(*args, *side.ins)
    return res[:nout], res[nout:]


def _ffn_up(n, wg, wu, name, side=None, tiles=MM_TILES):
    tp, d = n.shape
    fp = wg.shape[1]
    tm = tp // tiles

    def body(n_ref, wg_ref, wu_ref, a_ref, b_ref, s_ref):
        nn = n_ref[...]
        for c0 in range(0, fp, MXU_COLS):
            cs = slice(c0, min(c0 + MXU_COLS, fp))
            a = _dot_nt(nn, wg_ref[cs, :])
            b = _dot_nt(nn, wu_ref[cs, :])
            a_ref[:, cs] = a.astype(BF)
            b_ref[:, cs] = b.astype(BF)
            s_ref[:, cs] = (a * _sigmoid(a) * b).astype(BF)

    out = jax.ShapeDtypeStruct((N_CHIP, tp, fp), BF)
    wspec = pl.BlockSpec((None, fp, d), lambda k, i: (k, 0, 0))
    ospec = pl.BlockSpec((None, tm, fp), lambda k, i: (k, i, 0))
    return _grid_call(body, name, (N_CHIP, tiles), [pl.BlockSpec((tm, d), lambda k, i: (i, 0)), wspec, wspec],
                      [ospec, ospec, ospec], [out, out, out], (n, wg, wu), side)


def _ffn_up_head(n, wg, wu, name, side):
    tp, d = n.shape
    fp = wg.shape[1]
    tiles = MM_TILES
    tm = tp // tiles
    gat = _gather_side([wg, wu], relative=True, two_path=True)
    sin, sout, ngs = len(side.ins), len(side.outs), len(gat.sems)
    order = (0,) + REL_SLOT
    staged = hasattr(side, "middle")

    def body(*refs):
        n_ref = refs[0]
        si = refs[3:3 + sin]
        a_ref, b_ref, s_ref = refs[3 + sin:6 + sin]
        go = refs[6 + sin:8 + sin]
        so = refs[8 + sin:8 + sin + sout]
        wbg, wbu, wsem = refs[8 + sin + sout:11 + sin + sout]
        gsems = refs[11 + sin + sout:11 + sin + sout + ngs]
        ssems = refs[11 + sin + sout + ngs:]
        k, i = pl.program_id(0), pl.program_id(1)
        cur = k % 2

        def to_vmem(slot, buf):
            return [pltpu.make_async_copy(go[0].at[slot], wbg.at[buf], wsem.at[buf, 0]),
                    pltpu.make_async_copy(go[1].at[slot], wbu.at[buf], wsem.at[buf, 1])]

        @pl.when((k == 0) & (i == 0))
        def _():
            gat.send(go, gsems)
            if not staged:
                side.start(si, so, ssems)
            for cp in to_vmem(0, 0):
                cp.start()
            for cp in to_vmem(0, 0):
                cp.wait()

        for j in range(3):
            @pl.when((k == j) & (i == tiles // 2))
            def _():
                gat.arrived(j, go, gsems)
                if staged and j == 1:
                    side.start(si, so, ssems)

            @pl.when((k == j) & (i == tiles - 2))
            def _():
                gat.forwarded(j, go, gsems)
                for cp in to_vmem(order[j + 1], (j + 1) % 2):
                    cp.start()

            @pl.when((k == j + 1) & (i == 0))
            def _():
                for cp in to_vmem(order[j + 1], (j + 1) % 2):
                    cp.wait()

        nn = n_ref[...]
        for c0 in range(0, fp, MXU_COLS):
            cs = pl.ds(c0, min(MXU_COLS, fp - c0))
            a = _dot_nt(nn, wbg[cur, cs, :])
            b = _dot_nt(nn, wbu[cur, cs, :])
            a_ref[:, cs] = a.astype(BF)
            b_ref[:, cs] = b.astype(BF)
            s_ref[:, cs] = (a * _sigmoid(a) * b).astype(BF)

        if staged:
            @pl.when((k == N_CHIP - 1) & (i == tiles // 4))
            def _():
                side.middle(si, so, ssems)

        @pl.when((k == N_CHIP - 1) & (i == tiles - 1))
        def _():
            gat.drain(go, gsems)
            if staged:
                side.rest(si, so, ssems)
            else:
                side.finish(si, so, ssems)

    out = jax.ShapeDtypeStruct((N_CHIP, tp, fp), BF)
    any_spec = pl.BlockSpec(memory_space=pl.ANY)
    slot_of = lambda k: (k % 2) * 2 + k // 2
    ospec = pl.BlockSpec((None, tm, fp), lambda k, i: (slot_of(k), i, 0))
    wbuf = pltpu.VMEM((2, fp, d), BF)
    res = pl.pallas_call(
        body, name=name, grid=(N_CHIP, tiles),
        in_specs=[pl.BlockSpec((tm, d), lambda k, i: (i, 0))] + [any_spec] * (2 + sin),
        out_specs=[ospec, ospec, ospec] + [any_spec] * (2 + sout),
        out_shape=[out, out, out] + list(gat.outs) + list(side.outs),
        scratch_shapes=[wbuf, wbuf, pltpu.SemaphoreType.DMA((2, 2))] + list(gat.sems) + list(side.sems),
        input_output_aliases={1: 3, 2: 4, **{3 + a: 5 + b for a, b in side.alias.items()}},
        compiler_params=_params(("arbitrary", "arbitrary")))(n, wg, wu, *side.ins)
    return res[:3], res[3:5], res[5:]


def _ffn_bwd_act(df, wd, a, b, name, side=None, tiles=MM_TILES):
    tp, d = df.shape
    fp = wd.shape[1]
    tm = tp // tiles

    def body(df_ref, wd_ref, a_ref, b_ref, da_ref, db_ref):
        dfv = df_ref[...]
        for c0 in range(0, fp, MXU_COLS):
            cs = slice(c0, min(c0 + MXU_COLS, fp))
            ds = _dot_nt(dfv, wd_ref[cs, :])
            av = a_ref[:, cs].astype(F32)
            bv = b_ref[:, cs].astype(F32)
            sg = _sigmoid(av)
            da_ref[:, cs] = (ds * bv * sg * (1.0 + av * (1.0 - sg))).astype(BF)
            db_ref[:, cs] = (ds * av * sg).astype(BF)

    out = jax.ShapeDtypeStruct((N_CHIP, tp, fp), BF)
    aspec = pl.BlockSpec((None, tm, fp), lambda k, i: (k, i, 0))
    return _grid_call(
        body, name, (N_CHIP, tiles),
        [pl.BlockSpec((tm, d), lambda k, i: (i, 0)), pl.BlockSpec((None, fp, d), lambda k, i: (k, 0, 0)), aspec, aspec],
        [aspec, aspec], [out, out], (df, wd, a, b), side)


def _col_matmul(lhs, w, name, trans_b, out_dtype, side=None, tiles=MM_TILES_BIG):
    tp, kd = lhs.shape
    nk = w.shape[0]
    nc = w.shape[1] if trans_b else w.shape[2]
    tm = tp // tiles

    def body(l_ref, w_ref, o_ref):
        if trans_b:
            o_ref[...] = _dot_nt(l_ref[...], w_ref[...]).astype(out_dtype)
        else:
            o_ref[...] = _dot(l_ref[...], w_ref[...]).astype(out_dtype)

    res, extra = _grid_call(
        body, name, (nk, tiles),
        [pl.BlockSpec((tm, kd), lambda k, i: (i, 0)),
         pl.BlockSpec((None,) + tuple(w.shape[1:]), lambda k, i: (k, 0, 0), pipeline_mode=pl.Buffered(1))],
        [pl.BlockSpec((tm, nc), lambda k, i: (i, k))], [jax.ShapeDtypeStruct((tp, nk * nc), out_dtype)], (lhs, w), side)
    return res[0], extra


def _row_matmul(pairs, name, trans_b, d_out, side=None, tiles=MM_TILES_BIG):
    l0 = pairs[0][0]
    tp = l0.shape[1] if l0.ndim == 3 else l0.shape[0]
    nk = pairs[0][1].shape[0]
    tm = tp // tiles
    npair = len(pairs)

    def body(*refs):
        o_ref = refs[2 * npair]
        k = pl.program_id(1)
        part = None
        for q in range(npair):
            l = refs[2 * q][...]
            w = refs[2 * q + 1][...]
            t = _dot_nt(l, w) if trans_b else _dot(l, w)
            part = t if part is None else part + t

        @pl.when(k == 0)
        def _():
            o_ref[...] = part

        @pl.when(k > 0)
        def _():
            o_ref[...] += part

    in_specs, args = [], []
    for lhs, w in pairs:
        if lhs.ndim == 3:
            in_specs.append(pl.BlockSpec((None, tm, lhs.shape[2]), lambda i, k: (k, i, 0)))
        else:
            in_specs.append(pl.BlockSpec((tm, lhs.shape[1] // nk), lambda i, k: (i, k)))
        in_specs.append(pl.BlockSpec((None,) + tuple(w.shape[1:]), lambda i, k: (k, 0, 0)))
        args += [lhs, w]
    res, extra = _grid_call(body, name, (tiles, nk), in_specs, [pl.BlockSpec((tm, d_out), lambda i, k: (i, 0))],
                            [jax.ShapeDtypeStruct((tp, d_out), F32)], args, side)
    return res[0], extra


def _wgrad_call(x, y, name, x_width=None, y_width=None, tile_x=None, tile_y=None, side=None):
    tp = x.shape[1] if x.ndim == 3 else x.shape[0]

    def spec(a, width, tile):
        cols = a.shape[2] if a.ndim == 3 else (a.shape[1] if width is None else width)
        tc = cols if tile is None else tile
        per = cols // tc
        if a.ndim == 3:
            return pl.BlockSpec((None, tp, tc), lambda k, t: (k, 0, t if tile else 0)), cols, per
        if width is None:
            return pl.BlockSpec((tp, tc), lambda k, t: (0, t if tile else 0)), cols, per
        return pl.BlockSpec((tp, tc), lambda k, t: (0, k * per + (t if tile else 0))), cols, per

    xs, p, nx = spec(x, x_width, tile_x)
    ys, q, ny = spec(y, y_width, tile_y)
    nt = nx * ny
    if tile_x:
        ospec = pl.BlockSpec((None, tile_x, q), lambda k, t: (k, t, 0))
    else:
        ospec = pl.BlockSpec((None, p, tile_y), lambda k, t: (k, 0, t))

    def body(x_ref, y_ref, o_ref):
        o_ref[...] = _dot_tn(x_ref[...], y_ref[...]).astype(BF)

    res, extra = _grid_call(body, name, (N_CHIP, nt), [xs, ys], [ospec], [jax.ShapeDtypeStruct((N_CHIP, p, q), BF)],
                            (x, y), side)
    return res[0], extra


def _row_call(body, name, tp, d, row_ins, vec_ins, row_out_dtypes, n_acc, side=None):
    te = tp // EW_TILES
    rspec = pl.BlockSpec((te, d), lambda i: (i, 0))
    vspec = pl.BlockSpec((1, d), lambda i: (0, 0))
    res, extra = _grid_call(
        body, name, (EW_TILES,), [rspec] * len(row_ins) + [vspec] * len(vec_ins),
        [rspec] * len(row_out_dtypes) + [vspec] * n_acc,
        [jax.ShapeDtypeStruct((tp, d), dt) for dt in row_out_dtypes] + [jax.ShapeDtypeStruct((1, d), F32)] * n_acc,
        (*row_ins, *vec_ins), side)
    return res if side is None else (res, extra)


def _norm0(h, g):
    tp, d = h.shape

    def body(h_ref, g_ref, n_ref):
        n_ref[...] = _rms(h_ref[...], g_ref[...]).astype(BF)

    return _row_call(body, "norm0", tp, d, [h], [g], [BF], 0)[0]


def _post_fwd(f, h, g_post, g_next, scale, name):
    tp, d = h.shape

    def body(f_ref, h_ref, gp_ref, gn_ref, hn_ref, n_ref):
        hn = h_ref[...] + scale * _rms(f_ref[...], gp_ref[...])
        hn_ref[...] = hn
        n_ref[...] = _rms(hn, gn_ref[...]).astype(BF)

    return _row_call(body, name, tp, d, [f, h], [g_post, g_next], [F32, BF], 0)


def _loss_bwd(f, h, tgt, g_post, t_real):
    tp, d = h.shape
    te = tp // EW_TILES

    def body(f_ref, h_ref, t_ref, gp_ref, dh_ref, df_ref, dg_ref, loss_ref):
        i = pl.program_id(0)

        @pl.when(i == 0)
        def _():
            dg_ref[...] = jnp.zeros_like(dg_ref)
            loss_ref[...] = jnp.zeros_like(loss_ref)

        f = f_ref[...]
        gp = gp_ref[...]
        h3 = h_ref[...] + 0.5 * _rms(f, gp)
        rows = i * te + lax.broadcasted_iota(jnp.int32, (te, 1), 0)
        real = (rows >= N_META) & (rows < t_real)
        e = jnp.where(real, h3 - t_ref[...], 0.0)
        loss_ref[...] += 0.5 * jnp.sum(jnp.sum(e * e, axis=1, keepdims=True), axis=0, keepdims=True) / d
        dh = e / d
        dh_ref[...] = dh
        dfv, dgr = _rms_bwd(f, gp, 0.5 * dh)
        df_ref[...] = dfv.astype(BF)
        dg_ref[...] += jnp.sum(dgr, axis=0, keepdims=True)

    rspec = pl.BlockSpec((te, d), lambda i: (i, 0))
    vspec = pl.BlockSpec((1, d), lambda i: (0, 0))
    return pl.pallas_call(
        body, name="loss_bwd", grid=(EW_TILES,),
        in_specs=[rspec, rspec, rspec, vspec],
        out_specs=[rspec, rspec, vspec, pl.BlockSpec((1, 1), lambda i: (0, 0))],
        out_shape=[jax.ShapeDtypeStruct((tp, d), F32), jax.ShapeDtypeStruct((tp, d), BF),
                   jax.ShapeDtypeStruct((1, d), F32), jax.ShapeDtypeStruct((1, 1), F32)],
        compiler_params=_params(("arbitrary",)),
    )(f, h, tgt, g_post)


def _pre_bwd(dn, h, dh_out, g_pre, name, chain=None, side=None):
    tp, d = h.shape

    def body(*refs):
        if chain is None:
            dn_ref, h_ref, dho_ref, g_ref, dh_ref, dg_ref = refs
        else:
            dn_ref, h_ref, dho_ref, p_ref, g_ref, gp_ref, dh_ref, dp_ref, dg_ref, dgp_ref = refs
        i = pl.program_id(0)

        @pl.when(i == 0)
        def _():
            dg_ref[...] = jnp.zeros_like(dg_ref)
            if chain is not None:
                dgp_ref[...] = jnp.zeros_like(dgp_ref)

        dx, dgr = _rms_bwd(h_ref[...], g_ref[...], dn_ref[...])
        dh = dho_ref[...] + dx
        dh_ref[...] = dh
        dg_ref[...] += jnp.sum(dgr, axis=0, keepdims=True)
        if chain is not None:
            dp, dgpr = _rms_bwd(p_ref[...], gp_ref[...], chain[2] * dh)
            dp_ref[...] = dp.astype(BF)
            dgp_ref[...] += jnp.sum(dgpr, axis=0, keepdims=True)

    if chain is None:
        return _row_call(body, name, tp, d, [dn, h, dh_out], [g_pre], [F32], 1, side)
    return _row_call(body, name, tp, d, [dn, h, dh_out, chain[0]], [g_pre, chain[1]], [F32, BF], 2, side)


def _gelu(y):
    c = math.sqrt(2.0 / math.pi)
    return 0.5 * y * (1.0 + jnp.tanh(c * (y + 0.044715 * y * y * y)))


def _gelu_grad(y):
    c = math.sqrt(2.0 / math.pi)
    t = jnp.tanh(c * (y + 0.044715 * y * y * y))
    return 0.5 * (1.0 + t) + 0.5 * y * (1.0 - t * t) * c * (1.0 + 3.0 * 0.044715 * y * y)


def _neg_expm1(x):
    p = 1.0 + x * (1.0 / 9.0)
    for n in (8.0, 7.0, 6.0, 5.0, 4.0, 3.0, 2.0):
        p = 1.0 + x * (1.0 / n) * p
    return -jnp.where(x > -0.35, x * p, jnp.exp(x) - 1.0)


def _softplus(x):
    e = jnp.exp(-jnp.abs(x))
    w = 1.0 + e
    l1p = jnp.where(w == 1.0, e, jnp.log(w) * (e / jnp.where(w == 1.0, 1.0, w - 1.0)))
    return jnp.maximum(x, 0.0) + l1p


def _group_mean(v, gm):
    hi = v.astype(BF)
    lo = (v - hi.astype(F32)).astype(BF)
    return _dot(hi, gm) + _dot(lo, gm)


def _shift_dn(win, s, r):
    if s == 0:
        return win[8:8 + r]
    return pltpu.roll(win, s, 0)[8:8 + r]


def _shift_up(win, s, r):
    if s == 0:
        return win[0:r]
    return pltpu.roll(win, r + 8 - s, 0)[0:r]


def _window_dn(ref, t0, r, first):
    if first:
        return jnp.concatenate([jnp.zeros((8, ref.shape[1]), F32), ref[0:r, :]], axis=0)
    return ref[pl.ds(t0 - 8, r + 8), :]


def _tile_scan(a, u, reverse):
    r = a.shape[0]
    rid = lax.broadcasted_iota(jnp.int32, a.shape, 0) & 7
    for dlt in (1, 2, 4):
        sh = (r - dlt) if reverse else dlt
        a_s = pltpu.roll(a, sh, 0)
        u_s = pltpu.roll(u, sh, 0)
        keep = (rid + dlt <= 7) if reverse else (rid >= dlt)
        u = jnp.where(keep, u + a * u_s, u)
        a = jnp.where(keep, a * a_s, a)
    return a, u


def _lru_gates(xc, wa, ba, wx, bx, sp):
    xb = xc.astype(BF)
    ga = _sigmoid(_dot(xb, wa) + ba)
    gx = _sigmoid(_dot(xb, wx) + bx)
    la = -LRU_C * ga * sp
    return ga, gx, la


def _conv4(win, w4, cb, r):
    return (cb + w4[3:4] * _shift_dn(win, 0, r) + w4[2:3] * _shift_dn(win, 1, r)
            + w4[1:2] * _shift_dn(win, 2, r) + w4[0:1] * _shift_dn(win, 3, r))


def _lru_fwd(z, w4, cb, wa2, ba, wx2, bx, lam, g_out, gm, side=None):
    tp = z.shape[0]
    dl = cb.shape[1]
    nb = dl // LANE
    r = tp // MIX_CHUNKS
    c = LANE

    def body(y_ref, x_ref, w4_ref, cb_ref, wa_ref, ba_ref, wx_ref, bx_ref, lam_ref, go_ref, gm_ref, m_ref, hs_ref):
        w4v = w4_ref[...]
        cbv = cb_ref[...]
        wa = wa_ref[...]
        wx = wx_ref[...]
        bav = ba_ref[...]
        bxv = bx_ref[...]
        gov = go_ref[...]
        gmv = gm_ref[...]
        sp = _softplus(-lam_ref[...])

        def chunk(t0, hprev, first):
            win = _window_dn(x_ref, t0, r, first)
            xc = _conv4(win, w4v, cbv, r)
            ga, gx, la = _lru_gates(xc, wa, bav, wx, bxv, sp)
            a = jnp.exp(la)
            u = jnp.sqrt(_neg_expm1(2.0 * la)) * gx * xc
            ac, uc = _tile_scan(a, u, False)
            for j in range(r // 8):
                hj = uc[8 * j:8 * j + 8] + ac[8 * j:8 * j + 8] * hprev
                hs_ref[pl.ds(t0 + 8 * j, 8), :] = hj
                hprev = jnp.broadcast_to(hj[7:8], (8, c))
            h = hs_ref[pl.ds(t0, r), :]
            lo = h * _gelu(y_ref[pl.ds(t0, r), :])
            rs = lax.rsqrt(_group_mean(lo * lo, gmv) + EPS)
            m_ref[pl.ds(t0, r), :] = (lo * rs * gov).astype(BF)
            return hprev

        hp = chunk(0, jnp.zeros((8, c), F32), True)

        def loop(ci, hp):
            return chunk(pl.multiple_of(ci * r, 16), hp, False)

        lax.fori_loop(1, MIX_CHUNKS, loop, hp)

    col = lambda off: pl.BlockSpec((tp, c), lambda j: (0, off + j))
    vec = pl.BlockSpec((1, c), lambda j: (0, j))
    return _grid_call(
        body, "lru_fwd", (nb,),
        [col(0), col(nb), pl.BlockSpec((8, c), lambda j: (0, j)), vec, pl.BlockSpec((None, c, c), lambda j: (j, 0, 0)),
         vec, pl.BlockSpec((None, c, c), lambda j: (j, 0, 0)), vec, vec, vec, pl.BlockSpec((c, c), lambda j: (0, 0))],
        [col(0), col(0)], [jax.ShapeDtypeStruct((tp, dl), BF), jax.ShapeDtypeStruct((tp, dl), F32)],
        (z, z, w4, cb, wa2, ba, wx2, bx, lam, g_out, gm), side)


def _lru_bwd(z, hs, dmix, w4, cb, wa2, ba, wx2, bx, lam, g_out, gm, side=None):
    tp = z.shape[0]
    dl = cb.shape[1]
    nb = dl // LANE
    r = tp // MIX_CHUNKS
    c = LANE

    def body(y_ref, x_ref, hs_ref, dm_ref, w4_ref, cb_ref, wa_ref, ba_ref, wx_ref, bx_ref, lam_ref, go_ref, gm_ref,
             dy_ref, dx_ref, small_ref, dwa_ref, dwx_ref, xc_buf, ga_buf, gx_buf, a_buf, dh_buf, dxc_buf):
        w4v = w4_ref[...]
        cbv = cb_ref[...]
        wa = wa_ref[...]
        wx = wx_ref[...]
        bav = ba_ref[...]
        bxv = bx_ref[...]
        gov = go_ref[...]
        gmv = gm_ref[...]
        lamv = lam_ref[...]
        sp = _softplus(-lamv)
        small_ref[...] = jnp.zeros_like(small_ref)
        dwa_ref[...] = jnp.zeros_like(dwa_ref)
        dwx_ref[...] = jnp.zeros_like(dwx_ref)
        a_buf[pl.ds(tp, 8), :] = jnp.zeros((8, c), F32)
        dxc_buf[pl.ds(tp, 8), :] = jnp.zeros((8, c), F32)

        def fwd_chunk(t0, first):
            win = _window_dn(x_ref, t0, r, first)
            xc = _conv4(win, w4v, cbv, r)
            ga, gx, la = _lru_gates(xc, wa, bav, wx, bxv, sp)
            xc_buf[pl.ds(t0, r), :] = xc
            ga_buf[pl.ds(t0, r), :] = ga
            gx_buf[pl.ds(t0, r), :] = gx
            a_buf[pl.ds(t0, r), :] = jnp.exp(la)
            h = hs_ref[pl.ds(t0, r), :]
            yv = y_ref[pl.ds(t0, r), :]
            ge = _gelu(yv)
            lo = h * ge
            rs = lax.rsqrt(_group_mean(lo * lo, gmv) + EPS)
            xh = lo * rs
            dm = dm_ref[pl.ds(t0, r), :]
            q = dm * gov
            dlo = rs * (q - xh * _group_mean(q * xh, gmv))
            small_ref[8:9, :] += jnp.sum(dm * xh, axis=0, keepdims=True)
            dh_buf[pl.ds(t0, r), :] = dlo * ge
            dy_ref[pl.ds(t0, r), :] = (dlo * h * _gelu_grad(yv)).astype(BF)

        fwd_chunk(0, True)

        def floop(ci, carry):
            fwd_chunk(pl.multiple_of(ci * r, 16), False)
            return carry

        lax.fori_loop(1, MIX_CHUNKS, floop, 0)

        def bwd_chunk(t0, vnext, first):
            ap = _shift_up(a_buf[pl.ds(t0, r + 8), :], 1, r)
            ac, uc = _tile_scan(ap, dh_buf[pl.ds(t0, r), :], True)
            for j in reversed(range(r // 8)):
                vj = uc[8 * j:8 * j + 8] + ac[8 * j:8 * j + 8] * vnext
                dh_buf[pl.ds(t0 + 8 * j, 8), :] = vj
                vnext = jnp.broadcast_to(vj[0:1], (8, c))
            v = dh_buf[pl.ds(t0, r), :]
            hprev = _shift_dn(_window_dn(hs_ref, t0, r, first), 1, r)
            xc = xc_buf[pl.ds(t0, r), :]
            ga = ga_buf[pl.ds(t0, r), :]
            gx = gx_buf[pl.ds(t0, r), :]
            a = a_buf[pl.ds(t0, r), :]
            em = _neg_expm1(-2.0 * LRU_C * ga * sp)
            mult = jnp.sqrt(em)
            dla = v * hprev * a - (v * gx * xc) * ((1.0 - em) / mult)
            dgx = v * mult * xc
            dxc = v * mult * gx
            dga = dla * (-LRU_C) * sp
            small_ref[7:8, :] += jnp.sum(dla * (-LRU_C) * ga, axis=0, keepdims=True)
            dpa = dga * ga * (1.0 - ga)
            dpx = dgx * gx * (1.0 - gx)
            small_ref[5:6, :] += jnp.sum(dpa, axis=0, keepdims=True)
            small_ref[6:7, :] += jnp.sum(dpx, axis=0, keepdims=True)
            dpab = dpa.astype(BF)
            dpxb = dpx.astype(BF)
            xb = xc.astype(BF)
            dxc = dxc + _dot_nt(dpab, wa) + _dot_nt(dpxb, wx)
            dwa_ref[...] += _dot_tn(xb, dpab)
            dwx_ref[...] += _dot_tn(xb, dpxb)
            dxc_buf[pl.ds(t0, r), :] = dxc
            small_ref[4:5, :] += jnp.sum(dxc, axis=0, keepdims=True)
            dwin = dxc_buf[pl.ds(t0, r + 8), :]
            dx_ref[pl.ds(t0, r), :] = (w4v[3:4] * dxc + w4v[2:3] * _shift_up(dwin, 1, r)
                                       + w4v[1:2] * _shift_up(dwin, 2, r) + w4v[0:1] * _shift_up(dwin, 3, r)).astype(BF)
            xwin = _window_dn(x_ref, t0, r, first)
            for k in range(4):
                small_ref[k:k + 1, :] += jnp.sum(dxc * _shift_dn(xwin, 3 - k, r), axis=0, keepdims=True)
            return vnext

        def bloop(it, vnext):
            ci = MIX_CHUNKS - 1 - it
            return bwd_chunk(pl.multiple_of(ci * r, 16), vnext, False)

        vn = lax.fori_loop(0, MIX_CHUNKS - 1, bloop, jnp.zeros((8, c), F32))
        bwd_chunk(0, vn, True)
        small_ref[7:8, :] = small_ref[7:8, :] * (-_sigmoid(-lamv))

    col = lambda off: pl.BlockSpec((tp, c), lambda j: (0, off + j))
    vec = pl.BlockSpec((1, c), lambda j: (0, j))
    mat = pl.BlockSpec((None, c, c), lambda j: (j, 0, 0))
    buf = pltpu.VMEM((tp, c), F32)
    bufp = pltpu.VMEM((tp + 8, c), F32)
    return _grid_call(
        body, "lru_bwd", (nb,),
        [col(0), col(nb), col(0), col(0), pl.BlockSpec((8, c), lambda j: (0, j)), vec, mat, vec, mat, vec, vec, vec,
         pl.BlockSpec((c, c), lambda j: (0, 0))],
        [col(0), col(0), pl.BlockSpec((16, c), lambda j: (0, j)), mat, mat],
        [jax.ShapeDtypeStruct((tp, dl), BF), jax.ShapeDtypeStruct((tp, dl), BF), jax.ShapeDtypeStruct((16, dl), F32),
         jax.ShapeDtypeStruct((nb, c, c), F32), jax.ShapeDtypeStruct((nb, c, c), F32)],
        (z, z, hs, dmix, w4, cb, wa2, ba, wx2, bx, lam, g_out, gm), side, [buf, buf, buf, bufp, buf, bufp])


def _sc_conv(cvwin, w3, r):
    return w3[2:3] * _shift_dn(cvwin, 0, r) + w3[1:2] * _shift_dn(cvwin, 1, r) + w3[0:1] * _shift_dn(cvwin, 2, r)


def _sc_fwd(z, w3, g_out, gm, dl, side=None):
    tp = z.shape[0]
    nb = dl // LANE
    r = tp // MIX_CHUNKS
    c = LANE

    def body(b_ref, c_ref, v_ref, w3_ref, go_ref, gm_ref, m_ref):
        w3v = w3_ref[...]
        gov = go_ref[...]
        gmv = gm_ref[...]

        def chunk(t0, first):
            cvwin = _window_dn(c_ref, t0, r, first) * _window_dn(v_ref, t0, r, first)
            so = b_ref[pl.ds(t0, r), :] * _sc_conv(cvwin, w3v, r)
            rs = lax.rsqrt(_group_mean(so * so, gmv) + EPS)
            m_ref[pl.ds(t0, r), :] = (so * rs * gov).astype(BF)

        chunk(0, True)

        def loop(ci, carry):
            chunk(pl.multiple_of(ci * r, 16), False)
            return carry

        lax.fori_loop(1, MIX_CHUNKS, loop, 0)

    col = lambda off: pl.BlockSpec((tp, c), lambda j: (0, off + j))
    res, extra = _grid_call(
        body, "sconv_fwd", (nb,),
        [col(2 * nb), col(3 * nb), col(4 * nb), pl.BlockSpec((8, c), lambda j: (0, j)),
         pl.BlockSpec((1, c), lambda j: (0, j)), pl.BlockSpec((c, c), lambda j: (0, 0))],
        [col(0)], [jax.ShapeDtypeStruct((tp, dl), BF)], (z, z, z, w3, g_out, gm), side)
    return res[0], extra


def _sc_bwd(z, dmix, w3, g_out, gm, dl, side=None):
    tp = z.shape[0]
    nb = dl // LANE
    r = tp // MIX_CHUNKS
    c = LANE

    def body(b_ref, c_ref, v_ref, dm_ref, w3_ref, go_ref, gm_ref, db_ref, dc_ref, dv_ref, small_ref, dsc_buf):
        w3v = w3_ref[...]
        gov = go_ref[...]
        gmv = gm_ref[...]
        small_ref[...] = jnp.zeros_like(small_ref)
        dsc_buf[pl.ds(tp, 8), :] = jnp.zeros((8, c), F32)

        def chunk1(t0, first):
            cvwin = _window_dn(c_ref, t0, r, first) * _window_dn(v_ref, t0, r, first)
            sc = _sc_conv(cvwin, w3v, r)
            bv = b_ref[pl.ds(t0, r), :]
            so = bv * sc
            rs = lax.rsqrt(_group_mean(so * so, gmv) + EPS)
            xh = so * rs
            dm = dm_ref[pl.ds(t0, r), :]
            q = dm * gov
            dso = rs * (q - xh * _group_mean(q * xh, gmv))
            small_ref[3:4, :] += jnp.sum(dm * xh, axis=0, keepdims=True)
            db_ref[pl.ds(t0, r), :] = (dso * sc).astype(BF)
            dsc = dso * bv
            dsc_buf[pl.ds(t0, r), :] = dsc
            for k in range(3):
                small_ref[k:k + 1, :] += jnp.sum(dsc * _shift_dn(cvwin, 2 - k, r), axis=0, keepdims=True)

        chunk1(0, True)

        def loop1(ci, carry):
            chunk1(pl.multiple_of(ci * r, 16), False)
            return carry

        lax.fori_loop(1, MIX_CHUNKS, loop1, 0)

        def loop2(ci, carry):
            t0 = pl.multiple_of(ci * r, 16)
            dwin = dsc_buf[pl.ds(t0, r + 8), :]
            dcv = w3v[2:3] * _shift_up(dwin, 0, r) + w3v[1:2] * _shift_up(dwin, 1, r) + w3v[0:1] * _shift_up(dwin, 2, r)
            dc_ref[pl.ds(t0, r), :] = (dcv * v_ref[pl.ds(t0, r), :]).astype(BF)
            dv_ref[pl.ds(t0, r), :] = (dcv * c_ref[pl.ds(t0, r), :]).astype(BF)
            return carry

        lax.fori_loop(0, MIX_CHUNKS, loop2, 0)

    col = lambda off: pl.BlockSpec((tp, c), lambda j: (0, off + j))
    out = jax.ShapeDtypeStruct((tp, dl), BF)
    return _grid_call(
        body, "sconv_bwd", (nb,),
        [col(2 * nb), col(3 * nb), col(4 * nb), col(nb), pl.BlockSpec((8, c), lambda j: (0, j)),
         pl.BlockSpec((1, c), lambda j: (0, j)), pl.BlockSpec((c, c), lambda j: (0, 0))],
        [col(0), col(0), col(0), pl.BlockSpec((8, c), lambda j: (0, j))],
        [out, out, out, jax.ShapeDtypeStruct((8, dl), F32)], (z, z, z, dmix, w3, g_out, gm), side,
        [pltpu.VMEM((tp + 8, c), F32)])


def _cast_pad(w, rows_p, cols_p, chip, name):
    r, c = w.shape

    def body(chip_ref, w_ref, o_ref):
        if (rows_p, cols_p) != (r, c):
            o_ref[...] = jnp.zeros_like(o_ref)
        o_ref[0:r, 0:c] = w_ref[...].astype(BF)

    return pl.pallas_call(
        body, name=name, out_shape=jax.ShapeDtypeStruct((N_CHIP, rows_p, cols_p), BF),
        grid_spec=pltpu.PrefetchScalarGridSpec(
            num_scalar_prefetch=1, grid=(1,),
            in_specs=[pl.BlockSpec((r, c), lambda i, chip: (0, 0))],
            out_specs=pl.BlockSpec((None, rows_p, cols_p), lambda i, chip: (chip[0], 0, 0))),
        compiler_params=_params(("arbitrary",)),
    )(chip, w)


def _adamw_math(w, g, m, v):
    m2 = ADAM_B1 * m + (1.0 - ADAM_B1) * g
    v2 = ADAM_B2 * v + (1.0 - ADAM_B2) * (g * g)
    m_hat = m2 / (1.0 - ADAM_B1 ** ADAM_STEP)
    v_hat = v2 / (1.0 - ADAM_B2 ** ADAM_STEP)
    delta = -ADAM_LR * (m_hat / (jnp.sqrt(v_hat) + ADAM_EPS) + ADAM_WD * w)
    return delta, m2, v2


def _adamw(w, g, m, v, name, row_tiles, col_tiles, side=None):
    r, c = w.shape
    tr = r // row_tiles
    tc = c // col_tiles
    gc = g.shape[1] if col_tiles == 1 else tc

    def body(w_ref, g_ref, m_ref, v_ref, go_ref, d_ref, mo_ref, vo_ref):
        gv = g_ref[...][:, 0:tc]
        delta, m2, v2 = _adamw_math(w_ref[...], gv, m_ref[...], v_ref[...])
        go_ref[...] = gv
        d_ref[...] = delta
        mo_ref[...] = m2
        vo_ref[...] = v2

    spec = pl.BlockSpec((tr, tc), lambda i, j: (i, j))
    out = jax.ShapeDtypeStruct((r, c), F32)
    return _grid_call(body, name, (row_tiles, col_tiles), [spec, pl.BlockSpec((tr, gc), lambda i, j: (i, j)), spec, spec],
                      [spec] * 4, [out] * 4, (w, g, m, v), side)


def _adamw_small(w, g_top, g4, m, v):
    def body(w_ref, gt_ref, g_ref, m_ref, v_ref, go_ref, d_ref, mo_ref, vo_ref):
        g = jnp.concatenate([gt_ref[...], (g_ref[0] + g_ref[1]) + (g_ref[2] + g_ref[3])], axis=0)
        delta, m2, v2 = _adamw_math(w_ref[...], g, m_ref[...], v_ref[...])
        go_ref[...] = g
        d_ref[...] = delta
        mo_ref[...] = m2
        vo_ref[...] = v2

    out = jax.ShapeDtypeStruct(w.shape, F32)
    spec = pl.BlockSpec(w.shape, lambda: (0, 0))
    return pl.pallas_call(
        body, name="adamw_small",
        in_specs=[spec, pl.BlockSpec(g_top.shape, lambda: (0, 0)), pl.BlockSpec(g4.shape, lambda: (0, 0, 0)), spec, spec],
        out_specs=[spec] * 4, out_shape=[out] * 4, compiler_params=_params())(w, g_top, g4, m, v)


def _place():
    x, y, c = lax.axis_index("x"), lax.axis_index("y"), lax.axis_index("c")
    chips = [(1 - x, y), (x, 1 - y), (1 - x, 1 - y)]
    return x, y, c, chips


ANY = pl.BlockSpec(memory_space=pl.ANY)


REL_SLOT = (2, 1, 3)


def _gather_side(bufs, relative=False, two_path=False):
    n = len(bufs)
    direct = (0, 1) if two_path else (0, 1, 2)

    def copies(outs, sems):
        s_ici, r_ici, s_d2d, r_d2d = sems[:4]
        x, y, c, chips = _place()
        me = 2 * x + y

        def rows(w, slot, core, part=None):
            half = bufs[w].shape[1] // 2
            if part is None:
                return outs[w].at[slot, pl.ds(core * half, half)]
            return outs[w].at[slot, pl.ds(core * half + part * (half // 2), half // 2)]

        def theirs(j):
            return REL_SLOT[j] if relative else 2 * chips[j][0] + chips[j][1]

        def ici_send(w, j):
            px, py = chips[j]
            return pltpu.make_async_remote_copy(
                src_ref=rows(w, 0 if relative else me, c), dst_ref=rows(w, REL_SLOT[j] if relative else me, c),
                send_sem=s_ici.at[w, j], recv_sem=r_ici.at[w, j], device_id=(px, py, c), device_id_type=MESH)

        def ici_recv(w, j):
            px, py = chips[j]
            return pltpu.make_async_remote_copy(
                src_ref=rows(w, theirs(j), c), dst_ref=rows(w, theirs(j), c),
                send_sem=s_ici.at[w, j], recv_sem=r_ici.at[w, j], device_id=(px, py, c), device_id_type=MESH)

        def hop_send(w, p):
            px, py = chips[1 - p]
            return pltpu.make_async_remote_copy(
                src_ref=rows(w, theirs(p), c, p), dst_ref=rows(w, REL_SLOT[2] if relative else theirs(p), c, p),
                send_sem=sems[4].at[w, p], recv_sem=sems[5].at[w, p], device_id=(px, py, c), device_id_type=MESH)

        def hop_recv(w, p):
            px, py = chips[1 - p]
            return pltpu.make_async_remote_copy(
                src_ref=rows(w, theirs(2), c, p), dst_ref=rows(w, theirs(2), c, p),
                send_sem=sems[4].at[w, p], recv_sem=sems[5].at[w, p], device_id=(px, py, c), device_id_type=MESH)

        def d2d(w, j, core):
            return pltpu.make_async_remote_copy(
                src_ref=rows(w, theirs(j), core), dst_ref=rows(w, theirs(j), core),
                send_sem=s_d2d.at[w, j], recv_sem=r_d2d.at[w, j], device_id=(x, y, 1 - c), device_id_type=MESH)

        return c, ici_send, ici_recv, hop_send, hop_recv, d2d

    def send(outs, sems):
        c, ici_send, ici_recv, hop_send, hop_recv, d2d = copies(outs, sems)
        for j in direct:
            for w in range(n):
                ici_send(w, j).start()

    def arrived(j, outs, sems):
        c, ici_send, ici_recv, hop_send, hop_recv, d2d = copies(outs, sems)
        for w in range(n):
            if j in direct:
                ici_recv(w, j).wait_recv()
                if two_path:
                    hop_send(w, j).start()
            else:
                hop_recv(w, 0).wait_recv()
                hop_recv(w, 1).wait_recv()
            d2d(w, j, c).start()

    def forwarded(j, outs, sems):
        c, ici_send, ici_recv, hop_send, hop_recv, d2d = copies(outs, sems)
        for w in range(n):
            d2d(w, j, 1 - c).wait_recv()

    def drain(outs, sems):
        c, ici_send, ici_recv, hop_send, hop_recv, d2d = copies(outs, sems)
        for w in range(n):
            for j in direct:
                ici_send(w, j).wait_send()
                if two_path:
                    hop_send(w, j).wait_send()
            for j in range(3):
                d2d(w, j, c).wait_send()

    def start(ins, outs, sems):
        send(outs, sems)

    def middle(ins, outs, sems):
        arrived(0, outs, sems)
        arrived(1, outs, sems)

    def rest(ins, outs, sems):
        arrived(2, outs, sems)
        for j in range(3):
            forwarded(j, outs, sems)
        drain(outs, sems)

    def finish(ins, outs, sems):
        middle(ins, outs, sems)
        rest(ins, outs, sems)

    dma = pltpu.SemaphoreType.DMA((n, 3))
    hop = [pltpu.SemaphoreType.DMA((n, 2))] * 2 if two_path else []
    side = _Side(list(bufs), [jax.ShapeDtypeStruct(b.shape, b.dtype) for b in bufs], {w: w for w in range(n)},
                 [dma, dma, dma, dma] + hop, start, finish)
    side.send, side.arrived, side.forwarded, side.drain = send, arrived, forwarded, drain
    side.middle, side.rest = middle, rest
    return side


def _run_side(side, name):
    sin, sout = len(side.ins), len(side.outs)

    def body(*refs):
        ins, outs, sems = refs[:sin], refs[sin:sin + sout], refs[sin + sout:]
        side.start(ins, outs, sems)
        side.finish(ins, outs, sems)

    return pl.pallas_call(
        body, name=name, out_shape=list(side.outs), in_specs=[ANY] * sin, out_specs=[ANY] * sout,
        scratch_shapes=list(side.sems), input_output_aliases=dict(side.alias))(*side.ins)


def _pair_exchange_side(grads):
    n = len(grads)

    def copies(ins, outs, sems):
        ssem, rsem = sems
        x, y, c, _ = _place()
        cps = []
        for w in range(n):
            half = grads[w].shape[1] // 2
            cps.append(pltpu.make_async_remote_copy(
                src_ref=ins[w].at[:, pl.ds((1 - c) * half, half)], dst_ref=outs[w],
                send_sem=ssem.at[w], recv_sem=rsem.at[w], device_id=(x, y, 1 - c), device_id_type=MESH))
        return cps

    def start(ins, outs, sems):
        for cp in copies(ins, outs, sems):
            cp.start()

    def finish(ins, outs, sems):
        for cp in copies(ins, outs, sems):
            cp.wait()

    dma = pltpu.SemaphoreType.DMA((n,))
    return _Side(list(grads), [jax.ShapeDtypeStruct((N_CHIP, g.shape[1] // 2, g.shape[2]), BF) for g in grads], {},
                 [dma, dma], start, finish)


def _sibling_copy_side(buf):
    def copy(ins, outs, sems):
        x, y, c, _ = _place()
        return pltpu.make_async_remote_copy(src_ref=ins[0], dst_ref=outs[0], send_sem=sems[0], recv_sem=sems[1],
                                            device_id=(x, y, 1 - c), device_id_type=MESH)

    return _Side([buf], [jax.ShapeDtypeStruct(buf.shape, buf.dtype)], {}, [pltpu.SemaphoreType.DMA, pltpu.SemaphoreType.DMA],
                 lambda i, o, s: copy(i, o, s).start(), lambda i, o, s: copy(i, o, s).wait())


def _slot_exchange_side(buf4):
    def copies(outs, sems, sending):
        ssem, rsem = sems
        x, y, c, chips = _place()
        me = 2 * x + y
        return [pltpu.make_async_remote_copy(
            src_ref=outs[0].at[me if sending else 2 * px + py], dst_ref=outs[0].at[me if sending else 2 * px + py],
            send_sem=ssem.at[j], recv_sem=rsem.at[j], device_id=(px, py, c), device_id_type=MESH)
            for j, (px, py) in enumerate(chips)]

    def start(ins, outs, sems):
        for cp in copies(outs, sems, True):
            cp.start()

    def finish(ins, outs, sems):
        for cp in copies(outs, sems, False):
            cp.wait_recv()
        for cp in copies(outs, sems, True):
            cp.wait_send()

    dma = pltpu.SemaphoreType.DMA((3,))
    return _Side([buf4], [jax.ShapeDtypeStruct(buf4.shape, buf4.dtype)], {0: 0}, [dma, dma], start, finish)


def _pair_sum(g, sib, core, name):
    _, r, cdim = g.shape
    half = r // 2

    def body(core_ref, g_ref, s_ref, o_ref):
        o_ref[...] = (g_ref[...].astype(F32) + s_ref[...].astype(F32)).astype(BF)

    return pl.pallas_call(
        body, name=name,
        grid_spec=pltpu.PrefetchScalarGridSpec(
            num_scalar_prefetch=1, grid=(N_CHIP,),
            in_specs=[pl.BlockSpec((None, half, cdim), lambda k, core: (k, core[0], 0)),
                      pl.BlockSpec((None, half, cdim), lambda k, core: (k, 0, 0))],
            out_specs=pl.BlockSpec((None, half, cdim), lambda k, core: (k, 0, 0))),
        out_shape=jax.ShapeDtypeStruct((N_CHIP, half, cdim), BF),
        compiler_params=_params(("arbitrary",)),
    )(core, g, sib)


def _chip_exchange_side(psums, relative=False):
    n = len(psums)

    def copies(ins, outs, sems):
        ssem, rsem = sems
        x, y, c, chips = _place()
        return [pltpu.make_async_remote_copy(
            src_ref=ins[w].at[REL_SLOT[j] if relative else 2 * px + py], dst_ref=outs[w].at[j],
            send_sem=ssem.at[w, j], recv_sem=rsem.at[w, j], device_id=(px, py, c), device_id_type=MESH)
            for w in range(n) for j, (px, py) in enumerate(chips)]

    def start(ins, outs, sems):
        for cp in copies(ins, outs, sems):
            cp.start()

    def finish(ins, outs, sems):
        for cp in copies(ins, outs, sems):
            cp.wait()

    dma = pltpu.SemaphoreType.DMA((n, 3))
    return _Side(list(psums), [jax.ShapeDtypeStruct((3,) + p.shape[1:], BF) for p in psums], {}, [dma, dma],
                 start, finish)


def _final_sum(g, sib, recv, sel, name):
    _, r, cdim = g.shape
    half = r // 2
    nt = 4
    th = half // nt

    def body(sel_ref, g_ref, s_ref, r_ref, o_ref):
        acc = g_ref[...].astype(F32) + s_ref[...].astype(F32)
        for j in range(3):
            acc = acc + r_ref[j].astype(F32)
        o_ref[...] = acc

    return pl.pallas_call(
        body, name=name,
        grid_spec=pltpu.PrefetchScalarGridSpec(
            num_scalar_prefetch=1, grid=(nt,),
            in_specs=[pl.BlockSpec((None, th, cdim), lambda i, sel: (sel[0], sel[1] * nt + i, 0)),
                      pl.BlockSpec((None, th, cdim), lambda i, sel: (sel[0], i, 0)),
                      pl.BlockSpec((3, th, cdim), lambda i, sel: (0, i, 0))],
            out_specs=pl.BlockSpec((th, cdim), lambda i, sel: (sel[1] * nt + i, 0))),
        out_shape=jax.ShapeDtypeStruct((r, cdim), F32),
        compiler_params=_params(("arbitrary",)),
    )(sel, g, sib, recv)


def _join_side(bufs):
    n = len(bufs)

    def copies(outs, sems, core_of):
        ssem, rsem = sems
        x, y, c, _ = _place()
        cps = []
        for w in range(n):
            half = bufs[w].shape[0] // 2
            rows = outs[w].at[pl.ds(core_of(c) * half, half)]
            cps.append(pltpu.make_async_remote_copy(
                src_ref=rows, dst_ref=rows, send_sem=ssem.at[w], recv_sem=rsem.at[w],
                device_id=(x, y, 1 - c), device_id_type=MESH))
        return cps

    def start(ins, outs, sems):
        for cp in copies(outs, sems, lambda c: c):
            cp.start()

    def finish(ins, outs, sems):
        for cp in copies(outs, sems, lambda c: 1 - c):
            cp.wait_recv()
        for cp in copies(outs, sems, lambda c: c):
            cp.wait_send()

    dma = pltpu.SemaphoreType.DMA((n,))
    return _Side(list(bufs), [jax.ShapeDtypeStruct(b.shape, F32) for b in bufs], {w: w for w in range(n)}, [dma, dma],
                 start, finish)


def _small_pair_sum(buf, sib, chip):
    rows, d = buf.shape

    def body(chip_ref, a_ref, b_ref, o_ref):
        o_ref[...] = a_ref[...] + b_ref[...]

    return pl.pallas_call(
        body, name="small_pair_sum", out_shape=jax.ShapeDtypeStruct((N_CHIP, rows, d), F32),
        grid_spec=pltpu.PrefetchScalarGridSpec(
            num_scalar_prefetch=1, grid=(1,),
            in_specs=[pl.BlockSpec((rows, d), lambda i, chip: (0, 0))] * 2,
            out_specs=pl.BlockSpec((None, rows, d), lambda i, chip: (chip[0], 0, 0))),
        compiler_params=_params(("arbitrary",)),
    )(chip, buf, sib)


def _small_all_reduce(buf, name):
    rows, d = buf.shape

    def body(in_ref, out_ref, sib, all4, ssem, rsem, psem, qsem):
        x, y, c, chips = _place()
        me = 2 * x + y
        to_sib = pltpu.make_async_remote_copy(src_ref=in_ref, dst_ref=sib, send_sem=ssem, recv_sem=rsem,
                                              device_id=(x, y, 1 - c), device_id_type=MESH)
        to_sib.start()
        to_sib.wait()
        all4[me] = in_ref[...] + sib[...]
        cps = [pltpu.make_async_remote_copy(src_ref=all4.at[me], dst_ref=all4.at[me], send_sem=psem.at[j],
                                            recv_sem=qsem.at[j], device_id=(px, py, c), device_id_type=MESH)
               for j, (px, py) in enumerate(chips)]
        for cp in cps:
            cp.start()
        for j, (px, py) in enumerate(chips):
            chip = 2 * px + py
            pltpu.make_async_remote_copy(src_ref=all4.at[chip], dst_ref=all4.at[chip], send_sem=psem.at[j],
                                         recv_sem=qsem.at[j], device_id=(px, py, c), device_id_type=MESH).wait_recv()
        for cp in cps:
            cp.wait_send()
        out_ref[...] = (all4[0] + all4[1]) + (all4[2] + all4[3])

    vm = pl.BlockSpec(memory_space=pltpu.VMEM)
    return pl.pallas_call(
        body, name=name, out_shape=jax.ShapeDtypeStruct((rows, d), F32),
        in_specs=[vm], out_specs=vm,
        scratch_shapes=[pltpu.VMEM((rows, d), F32), pltpu.VMEM((N_CHIP, rows, d), F32),
                        pltpu.SemaphoreType.DMA, pltpu.SemaphoreType.DMA,
                        pltpu.SemaphoreType.DMA((3,)), pltpu.SemaphoreType.DMA((3,))],
        compiler_params=_params(),
    )(buf)


def _pair_blocks(w):
    w4 = w.reshape(N_HEADS // 2, 2, HEAD, HEAD)
    eye = jnp.eye(2, dtype=w.dtype)
    return jnp.einsum("pirc,ij->pirjc", w4, eye).reshape(N_HEADS // 2, LANE, LANE)


def _unpair_blocks(w2):
    w5 = w2.reshape(N_HEADS // 2, 2, HEAD, 2, HEAD)
    return jnp.stack([w5[:, 0, :, 0, :], w5[:, 1, :, 1, :]], axis=1).reshape(N_HEADS, HEAD, HEAD)


def kernel(x, meta_tokens, ffn1_pre_g, ffn1_w_gate, ffn1_w_up, ffn1_w_down, ffn1_post_g, mix_pre_g, w_in, lru_conv_w, lru_conv_b, lru_w_a, lru_b_a, lru_w_x, lru_b_x, lru_lambda, sconv_w, lru_out_g, sconv_out_g, w_out, mix_post_g, ffn2_pre_g, ffn2_w_gate, ffn2_w_up, ffn2_w_down, ffn2_post_g, loss_target, m_meta_tokens, m_ffn1_pre_g, m_ffn1_w_gate, m_ffn1_w_up, m_ffn1_w_down, m_ffn1_post_g, m_mix_pre_g, m_w_in, m_lru_conv_w, m_lru_conv_b, m_lru_w_a, m_lru_b_a, m_lru_w_x, m_lru_b_x, m_lru_lambda, m_sconv_w, m_lru_out_g, m_sconv_out_g, m_w_out, m_mix_post_g, m_ffn2_pre_g, m_ffn2_w_gate, m_ffn2_w_up, m_ffn2_w_down, m_ffn2_post_g, v_meta_tokens, v_ffn1_pre_g, v_ffn1_w_gate, v_ffn1_w_up, v_ffn1_w_down, v_ffn1_post_g, v_mix_pre_g, v_w_in, v_lru_conv_w, v_lru_conv_b, v_lru_w_a, v_lru_b_a, v_lru_w_x, v_lru_b_x, v_lru_lambda, v_sconv_w, v_lru_out_g, v_sconv_out_g, v_w_out, v_mix_post_g, v_ffn2_pre_g, v_ffn2_w_gate, v_ffn2_w_up, v_ffn2_w_down, v_ffn2_post_g):
    seq, d = x.shape[1], x.shape[2]
    t_real = N_META + seq
    tp = _round_up(t_real, ROW_ALIGN)
    f4 = ffn1_w_gate.shape[2]
    f4p = _round_up(f4, LANE)
    dl = lru_conv_b.shape[1]
    cin = w_in.shape[2]
    xi, yi, ci = lax.axis_index("x"), lax.axis_index("y"), lax.axis_index("c")
    chip = 2 * xi + yi
    zero = jnp.zeros((), jnp.int32)

    transposed = ("ffn1_w_gate", "ffn1_w_up", "ffn2_w_gate", "ffn2_w_up")

    def view(k, a):
        return a[0].T if k in transposed else a[0]

    def unview(k, a):
        return (a.T if k in transposed else a)[None]

    big = {
        "ffn1_w_gate": (view("ffn1_w_gate", ffn1_w_gate), f4p, d), "ffn1_w_up": (view("ffn1_w_up", ffn1_w_up), f4p, d),
        "ffn1_w_down": (ffn1_w_down[0], f4p, d), "w_in": (w_in[0], d, cin), "w_out": (w_out[0], w_out.shape[1], d),
        "ffn2_w_gate": (view("ffn2_w_gate", ffn2_w_gate), f4p, d), "ffn2_w_up": (view("ffn2_w_up", ffn2_w_up), f4p, d),
        "ffn2_w_down": (ffn2_w_down[0], f4p, d),
    }
    names = list(big)
    chip1 = jnp.reshape(chip, (1,)).astype(jnp.int32)
    relative = {k: k.startswith("ffn") for k in names}
    slot0 = jnp.zeros((1,), jnp.int32)
    shard = {k: _cast_pad(big[k][0], big[k][1], big[k][2], slot0 if relative[k] else chip1, "cast_" + k) for k in names}
    full = {}

    def gather(*keys):
        return _merge_sides([_gather_side([shard[k]], relative[k]) for k in keys])

    gm = jnp.kron(jnp.eye(2, dtype=F32), jnp.full((HEAD, HEAD), 1.0 / HEAD, F32)).astype(BF)
    wa2 = _pair_blocks(lru_w_a[0])
    wx2 = _pair_blocks(lru_w_x[0])

    dlq = dl // N_CHIP
    dq = d // N_CHIP
    R_GAIN, R_LOSS, R_META, R_LRU, R_SC, R_WA = 0, 6, 8, 24, 40, 48
    n_wrows = (N_HEADS // 2) * LANE * LANE // d
    R_WX = R_WA + n_wrows
    R_END = R_WX + n_wrows

    def pack_top(gains, meta, loss=None):
        lossrow = jnp.zeros((2, d), F32)
        if loss is not None:
            lossrow = lossrow.at[0, 0].set(loss)
        return jnp.concatenate([jnp.concatenate(gains, axis=0), lossrow, meta], axis=0)

    def pack_rest(lru16, sc8, wa_, wx_):
        return jnp.concatenate([jnp.concatenate([lru16, jnp.zeros((16, d - dl), F32)], axis=1),
                                jnp.concatenate([sc8, jnp.zeros((8, d - dl), F32)], axis=1),
                                wa_.reshape(n_wrows, d), wx_.reshape(n_wrows, d)], axis=0)

    def pack(gains, meta, lru16, sc8, wa_, wx_):
        return jnp.concatenate([pack_top(gains, meta), pack_rest(lru16, sc8, wa_, wx_)], axis=0)

    def place_cols(blk, width, total):
        return lax.dynamic_update_slice(jnp.zeros((blk.shape[0], total), F32), blk, (zero, chip * width))

    def pack_params(meta_, g1pre, g1post, gmpre, gmpost, g2pre, g2post, cw, cbias, wa_, ba_, wx_, bx_, lam_, sw, lgo, sgo):
        lru16 = jnp.concatenate([place_cols(cw[0], dlq, dl), cbias, ba_, bx_, lam_, lgo, jnp.zeros((7, dl), F32)], axis=0)
        sc8 = jnp.concatenate([place_cols(sw[0], dlq, dl), sgo, jnp.zeros((4, dl), F32)], axis=0)
        return pack([g1pre, g1post, gmpre, gmpost, g2pre, g2post], place_cols(meta_, dq, d), lru16, sc8,
                    _pair_blocks(wa_[0]), _pair_blocks(wx_[0]))

    p_w = pack_params(meta_tokens, ffn1_pre_g, ffn1_post_g, mix_pre_g, mix_post_g, ffn2_pre_g, ffn2_post_g, lru_conv_w,
                      lru_conv_b, lru_w_a, lru_b_a, lru_w_x, lru_b_x, lru_lambda, sconv_w, lru_out_g, sconv_out_g)
    p_m = pack_params(m_meta_tokens, m_ffn1_pre_g, m_ffn1_post_g, m_mix_pre_g, m_mix_post_g, m_ffn2_pre_g, m_ffn2_post_g,
                      m_lru_conv_w, m_lru_conv_b, m_lru_w_a, m_lru_b_a, m_lru_w_x, m_lru_b_x, m_lru_lambda, m_sconv_w,
                      m_lru_out_g, m_sconv_out_g)
    p_v = pack_params(v_meta_tokens, v_ffn1_pre_g, v_ffn1_post_g, v_mix_pre_g, v_mix_post_g, v_ffn2_pre_g, v_ffn2_post_g,
                      v_lru_conv_w, v_lru_conv_b, v_lru_w_a, v_lru_b_a, v_lru_w_x, v_lru_b_x, v_lru_lambda, v_sconv_w,
                      v_lru_out_g, v_sconv_out_g)

    gathered = _small_all_reduce(jnp.where(ci == 0, p_w, 0.0)[R_META:R_WA], "small_weight_gather")
    meta_full = gathered[0:N_META]
    w4_full = gathered[R_LRU - R_META:R_LRU - R_META + 4, 0:dl]
    w3_full = gathered[R_SC - R_META:R_SC - R_META + 3, 0:dl]
    w4p = jnp.concatenate([w4_full, jnp.zeros((4, dl), F32)], axis=0)
    w3p = jnp.concatenate([w3_full, jnp.zeros((5, dl), F32)], axis=0)

    h0 = jnp.concatenate([meta_full, x[0], jnp.zeros((tp - t_real, d), F32)], axis=0)
    tgt = jnp.concatenate([jnp.zeros((N_META, d), F32), loss_target[0], jnp.zeros((tp - t_real, d), F32)], axis=0)

    n1 = _norm0(h0, ffn1_pre_g)
    (a1, b1, s1), (full["ffn1_w_gate"], full["ffn1_w_up"]), got = _ffn_up_head(
        n1, shard["ffn1_w_gate"], shard["ffn1_w_up"], "ffn1_up",
        _gather_side([shard["ffn1_w_down"]], relative=True, two_path=True))
    full["ffn1_w_down"] = got[0]
    f1, got = _row_matmul([(s1, full["ffn1_w_down"])], "ffn1_down", False, d, gather("w_in"))
    full["w_in"] = got[0]
    h1, u = _post_fwd(f1, h0, ffn1_post_g, mix_pre_g, 0.5, "ffn1_post")
    z, got = _col_matmul(u, full["w_in"], "in_proj", False, F32, gather("ffn2_w_gate"))
    full["ffn2_w_gate"] = got[0]
    (m_lru, hs), got = _lru_fwd(z, w4p, lru_conv_b, wa2.astype(BF), lru_b_a, wx2.astype(BF), lru_b_x, lru_lambda,
                                lru_out_g, gm, gather("ffn2_w_up"))
    full["ffn2_w_up"] = got[0]
    m_sc, got = _sc_fwd(z, w3p, sconv_out_g, gm, dl, gather("w_out"))
    full["w_out"] = got[0]
    mixed = jnp.concatenate([m_lru, m_sc], axis=1)
    p, _ = _row_matmul([(mixed, full["w_out"])], "out_proj", False, d)
    h2, n2 = _post_fwd(p, h1, mix_post_g, ffn2_pre_g, 1.0, "mix_post")
    (a2, b2, s2), got = _ffn_up(n2, full["ffn2_w_gate"], full["ffn2_w_up"], "ffn2_up", gather("ffn2_w_down"))
    full["ffn2_w_down"] = got[0]
    f2, _ = _row_matmul([(s2, full["ffn2_w_down"])], "ffn2_down", False, d)
    dh3, df2, dg_ffn2_post, loss_part = _loss_bwd(f2, h2, tgt, ffn2_post_g, t_real)

    core = jnp.reshape(ci, (1,)).astype(jnp.int32)
    sel_of = {False: jnp.stack([chip, ci]).astype(jnp.int32), True: jnp.stack([0 * chip, ci]).astype(jnp.int32)}
    red = {}

    def pair_side(k):
        return _pair_exchange_side([red[k][0]])

    def chip_side(k):
        return _chip_exchange_side([_pair_sum(red[k][0], red[k][1], core, "pair_sum_" + k)], relative[k])

    def final_sum(k):
        return _final_sum(*red[k], sel_of[relative[k]], "final_sum_" + k)

    (da2, db2), _ = _ffn_bwd_act(df2, full["ffn2_w_down"], a2, b2, "ffn2_bwd_act")
    g, _ = _wgrad_call(s2, df2, "ffn2_down_wgrad", tile_y=WGRAD_TILE_Y)
    red["ffn2_w_down"] = [g, None, None]
    g, got = _wgrad_call(da2, n2, "ffn2_gate_wgrad", tile_y=WGRAD_TILE_Y, side=pair_side("ffn2_w_down"))
    red["ffn2_w_down"][1] = got[0]
    red["ffn2_w_gate"] = [g, None, None]
    g, got = _wgrad_call(db2, n2, "ffn2_up_wgrad", tile_y=WGRAD_TILE_Y,
                         side=_merge_sides([pair_side("ffn2_w_gate"), chip_side("ffn2_w_down")]))
    red["ffn2_w_gate"][1], red["ffn2_w_down"][2] = got
    red["ffn2_w_up"] = [g, None, None]
    dn2, got = _row_matmul([(da2, full["ffn2_w_gate"]), (db2, full["ffn2_w_up"])], "ffn2_bwd_up", False, d,
                           _merge_sides([pair_side("ffn2_w_up"), chip_side("ffn2_w_gate")]), tiles=MM_TILES)
    red["ffn2_w_up"][1], red["ffn2_w_gate"][2] = got
    dh2, dp, dg_ffn2_pre, dg_mix_post = _pre_bwd(dn2, h2, dh3, ffn2_pre_g, "ffn2_pre_bwd", (p, mix_post_g, 1.0))
    dmixed, _ = _col_matmul(dp, full["w_out"], "out_proj_bwd", True, F32)
    g, _ = _wgrad_call(mixed, dp, "w_out_wgrad", x_width=mixed.shape[1] // N_CHIP, tile_y=WGRAD_TILE_Y)
    red["w_out"] = [g, None, None]
    (dzy, dzx, lru_small, dwa2, dwx2), got = _lru_bwd(
        z, hs, dmixed, w4p, lru_conv_b, wa2.astype(BF), lru_b_a, wx2.astype(BF), lru_b_x, lru_lambda, lru_out_g, gm,
        _merge_sides([pair_side("w_out"), chip_side("ffn2_w_up")]))
    red["w_out"][1], red["ffn2_w_up"][2] = got
    (dzb, dzc, dzv, sc_small), got = _sc_bwd(z, dmixed, w3p, sconv_out_g, gm, dl, chip_side("w_out"))
    red["w_out"][2] = got[0]
    dz = jnp.concatenate([dzy, dzx, dzb, dzc, dzv], axis=1)
    p_rest = pack_rest(lru_small, sc_small, dwa2, dwx2)
    g, got = _wgrad_call(u, dz, "w_in_wgrad", y_width=cin, tile_x=WGRAD_TILE_X, side=_sibling_copy_side(p_rest))
    p_rest4 = _small_pair_sum(p_rest, got[0], chip1)
    red["w_in"] = [g, None, None]
    du, got = _row_matmul([(dz, full["w_in"])], "in_proj_bwd", True, d,
                          _merge_sides([pair_side("w_in"), _slot_exchange_side(p_rest4)]))
    red["w_in"][1], p_rest4 = got
    dh1, df1, dg_mix_pre, dg_ffn1_post = _pre_bwd(du, h1, dh2, mix_pre_g, "mix_pre_bwd", (f1, ffn1_post_g, 0.5))
    (da1, db1), got = _ffn_bwd_act(df1, full["ffn1_w_down"], a1, b1, "ffn1_bwd_act", chip_side("w_in"))
    red["w_in"][2] = got[0]
    early = ["ffn2_w_down", "ffn2_w_gate", "ffn2_w_up", "w_out", "w_in"]
    late = ["ffn1_w_down", "ffn1_w_gate", "ffn1_w_up"]
    g, got = _wgrad_call(s1, df1, "ffn1_down_wgrad", tile_y=WGRAD_TILE_Y,
                         side=_join_side([final_sum(k) for k in early]))
    gfull = dict(zip(early, got))
    red["ffn1_w_down"] = [g, None, None]
    g, got = _wgrad_call(da1, n1, "ffn1_gate_wgrad", tile_y=WGRAD_TILE_Y, side=pair_side("ffn1_w_down"))
    red["ffn1_w_down"][1] = got[0]
    red["ffn1_w_gate"] = [g, None, None]
    g, got = _wgrad_call(db1, n1, "ffn1_up_wgrad", tile_y=WGRAD_TILE_Y,
                         side=_merge_sides([pair_side("ffn1_w_gate"), chip_side("ffn1_w_down")]))
    red["ffn1_w_gate"][1], red["ffn1_w_down"][2] = got
    red["ffn1_w_up"] = [g, None, None]
    red["ffn1_w_up"][1] = _run_side(pair_side("ffn1_w_up"), "pair_exchange_ffn1_w_up")[0]
    dn1, got = _row_matmul([(da1, full["ffn1_w_gate"]), (db1, full["ffn1_w_up"])], "ffn1_bwd_up", False, d,
                           _merge_sides([chip_side("ffn1_w_gate"), chip_side("ffn1_w_up")]), tiles=MM_TILES)
    red["ffn1_w_gate"][2], red["ffn1_w_up"][2] = got
    (dh0, dg_ffn1_pre), got = _pre_bwd(dn1, h0, dh1, ffn1_pre_g, "ffn1_pre_bwd",
                                       side=_join_side([final_sum(k) for k in late]))
    gfull.update(zip(late, got))

    grad_x = dh0[N_META:t_real][None]

    w_big = {"ffn1_w_gate": ffn1_w_gate, "ffn1_w_up": ffn1_w_up, "ffn1_w_down": ffn1_w_down, "w_in": w_in, "w_out": w_out,
             "ffn2_w_gate": ffn2_w_gate, "ffn2_w_up": ffn2_w_up, "ffn2_w_down": ffn2_w_down}
    m_big = {"ffn1_w_gate": m_ffn1_w_gate, "ffn1_w_up": m_ffn1_w_up, "ffn1_w_down": m_ffn1_w_down, "w_in": m_w_in,
             "w_out": m_w_out, "ffn2_w_gate": m_ffn2_w_gate, "ffn2_w_up": m_ffn2_w_up, "ffn2_w_down": m_ffn2_w_down}
    v_big = {"ffn1_w_gate": v_ffn1_w_gate, "ffn1_w_up": v_ffn1_w_up, "ffn1_w_down": v_ffn1_w_down, "w_in": v_w_in,
             "w_out": v_w_out, "ffn2_w_gate": v_ffn2_w_gate, "ffn2_w_up": v_ffn2_w_up, "ffn2_w_down": v_ffn2_w_down}
    b_grad, b_delta, b_newm, b_newv = {}, {}, {}, {}

    def big_adamw(k, side=None):
        wv, mv, vv = view(k, w_big[k]), view(k, m_big[k]), view(k, v_big[k])
        wide_rows = wv.shape[0] % 64 == 0
        (g_, d_, m_, v_), got = _adamw(wv, gfull[k], mv, vv, "adamw_" + k, 8 if wide_rows else 4, 1 if wide_rows else 2,
                                       side)
        b_grad[k], b_delta[k], b_newm[k], b_newv[k] = unview(k, g_), unview(k, d_), unview(k, m_), unview(k, v_)
        return got

    p_top = _small_all_reduce(
        pack_top([dg_ffn1_pre, dg_ffn1_post, dg_mix_pre, dg_mix_post, dg_ffn2_pre, dg_ffn2_post], dh0[0:N_META],
                 loss=loss_part[0, 0]), "small_grad_all_reduce")
    p_g, p_delta, p_newm, p_newv = _adamw_small(p_w, p_top, p_rest4, p_m, p_v)
    loss = p_g[R_LOSS, 0]
    for k in names:
        big_adamw(k)

    def unpack(buf):
        out = {}
        for i, k in enumerate(["ffn1_pre_g", "ffn1_post_g", "mix_pre_g", "mix_post_g", "ffn2_pre_g", "ffn2_post_g"]):
            out[k] = buf[R_GAIN + i:R_GAIN + i + 1]
        out["meta_tokens"] = lax.dynamic_slice(buf[R_META:R_META + N_META], (zero, chip * dq), (N_META, dq))
        lru = buf[R_LRU:R_LRU + 16, 0:dl]
        out["lru_conv_w"] = lax.dynamic_slice(lru[0:4], (zero, chip * dlq), (4, dlq))[None]
        out["lru_conv_b"] = lru[4:5]
        out["lru_b_a"] = lru[5:6]
        out["lru_b_x"] = lru[6:7]
        out["lru_lambda"] = lru[7:8]
        out["lru_out_g"] = lru[8:9]
        sc = buf[R_SC:R_SC + 8, 0:dl]
        out["sconv_w"] = lax.dynamic_slice(sc[0:3], (zero, chip * dlq), (3, dlq))[None]
        out["sconv_out_g"] = sc[3:4]
        out["lru_w_a"] = _unpair_blocks(buf[R_WA:R_WX].reshape(N_HEADS // 2, LANE, LANE))[None]
        out["lru_w_x"] = _unpair_blocks(buf[R_WX:R_END].reshape(N_HEADS // 2, LANE, LANE))[None]
        return out

    s_grad, s_delta, s_newm, s_newv = unpack(p_g), unpack(p_delta), unpack(p_newm), unpack(p_newv)

    order = ["meta_tokens", "ffn1_pre_g", "ffn1_w_gate", "ffn1_w_up", "ffn1_w_down", "ffn1_post_g", "mix_pre_g", "w_in",
             "lru_conv_w", "lru_conv_b", "lru_w_a", "lru_b_a", "lru_w_x", "lru_b_x", "lru_lambda", "sconv_w", "lru_out_g",
             "sconv_out_g", "w_out", "mix_post_g", "ffn2_pre_g", "ffn2_w_gate", "ffn2_w_up", "ffn2_w_down", "ffn2_post_g"]

    def pick(small, bigd):
        return [bigd[k] if k in bigd else small[k] for k in order]

    return (loss, grad_x, *pick(s_grad, b_grad), *pick(s_delta, b_delta), *pick(s_newm, b_newm), *pick(s_newv, b_newv))
```

```python
import functools
import math

import jax
import jax.numpy as jnp
from jax import lax
from jax.experimental import pallas as pl
from jax.experimental.pallas import tpu as pltpu

F32 = jnp.float32
BF = jnp.bfloat16
MESH = pl.DeviceIdType.MESH

EPS = 1e-6
N_META = 16
N_HEADS = 16
HEAD = 64
LRU_C = 8.0
LANE = 128
MXU_COLS = 256
N_CHIP = 4
ROW_ALIGN = 384
MM_TILES = 8
MM_TILES_BIG = 4
EW_TILES = 12
MIX_CHUNKS = 24
WGRAD_TILE_X = 256
WGRAD_TILE_Y = 512
WIDE_TN = 512
VMEM_LIMIT = 56 << 20

ADAM_LR = 0.001
ADAM_B1 = 0.9
ADAM_B2 = 0.999
ADAM_EPS = 1e-08
ADAM_WD = 0.01
ADAM_STEP = 10


def _round_up(a, b):
    return (a + b - 1) // b * b


def _params(sem=None):
    if sem is None:
        return pltpu.CompilerParams(vmem_limit_bytes=VMEM_LIMIT)
    return pltpu.CompilerParams(dimension_semantics=sem, vmem_limit_bytes=VMEM_LIMIT)


def _sigmoid(x):
    return 0.5 * jnp.tanh(0.5 * x) + 0.5


def _dot(a, b):
    return jnp.dot(a, b, preferred_element_type=F32)


def _dot_nt(a, b):
    return lax.dot_general(a, b, (((1,), (1,)), ((), ())), preferred_element_type=F32)


def _dot_tn(a, b):
    return lax.dot_general(a, b, (((0,), (0,)), ((), ())), preferred_element_type=F32)


def _rms(x, g):
    r = lax.rsqrt(jnp.mean(x * x, axis=-1, keepdims=True) + EPS)
    return x * r * g


def _rms_bwd(x, g, dy):
    r = lax.rsqrt(jnp.mean(x * x, axis=-1, keepdims=True) + EPS)
    xh = x * r
    q = dy * g
    dx = r * (q - xh * jnp.mean(q * xh, axis=-1, keepdims=True))
    return dx, dy * xh


class _Side:
    def __init__(self, ins, outs, alias, sems, start, finish):
        self.ins, self.outs, self.alias, self.sems, self.start, self.finish = ins, outs, alias, sems, start, finish


def _merge_sides(sides):
    sides = [s for s in sides if s is not None]
    if len(sides) <= 1:
        return sides[0] if sides else None
    ins, outs, sems, alias, spans = [], [], [], {}, []
    for s in sides:
        for i, o in s.alias.items():
            alias[len(ins) + i] = len(outs) + o
        spans.append((len(ins), len(ins) + len(s.ins), len(outs), len(outs) + len(s.outs), len(sems),
                      len(sems) + len(s.sems)))
        ins += list(s.ins)
        outs += list(s.outs)
        sems += list(s.sems)

    def run(which):
        def go(in_refs, out_refs, sem_refs):
            for s, (a, b, c, d, e, f) in zip(sides, spans):
                getattr(s, which)(in_refs[a:b], out_refs[c:d], sem_refs[e:f])
        return go

    return _Side(ins, outs, alias, sems, run("start"), run("finish"))


def _grid_call(body, name, grid, in_specs, out_specs, out_shape, args, side=None, scratch=()):
    sem = ("arbitrary",) * len(grid)
    if side is None:
        res = pl.pallas_call(body, name=name, grid=grid, in_specs=in_specs, out_specs=out_specs, out_shape=out_shape,
                             scratch_shapes=list(scratch), compiler_params=_params(sem))(*args)
        return res, []
    nin, nout, sin, sout = len(in_specs), len(out_specs), len(side.ins), len(side.outs)
    nscr = len(scratch)
    staged = hasattr(side, "middle") and math.prod(grid) >= 4
    lin, mid = (math.prod(grid) * 5) // 8, []
    for extent in reversed(grid):
        mid.insert(0, lin % extent)
        lin //= extent

    def full(*refs):
        base_in, side_in = refs[:nin], refs[nin:nin + sin]
        base_out = refs[nin + sin:nin + sin + nout]
        side_out = refs[nin + sin + nout:nin + sin + nout + sout]
        base_scr = refs[nin + sin + nout + sout:nin + sin + nout + sout + nscr]
        sems = refs[nin + sin + nout + sout + nscr:]
        first = pl.program_id(0) == 0
        last = pl.program_id(0) == grid[0] - 1
        for ax in range(1, len(grid)):
            first = first & (pl.program_id(ax) == 0)
            last = last & (pl.program_id(ax) == grid[ax] - 1)

        @pl.when(first)
        def _():
            side.start(side_in, side_out, sems)

        if staged:
            at_mid = pl.program_id(0) == mid[0]
            for ax in range(1, len(grid)):
                at_mid = at_mid & (pl.program_id(ax) == mid[ax])

            @pl.when(at_mid)
            def _():
                side.middle(side_in, side_out, sems)

        body(*base_in, *base_out, *base_scr)

        @pl.when(last)
        def _():
            (side.rest if staged else side.finish)(side_in, side_out, sems)

    any_spec = pl.BlockSpec(memory_space=pl.ANY)
    res = pl.pallas_call(
        full, name=name, grid=grid, in_specs=list(in_specs) + [any_spec] * sin,
        out_specs=list(out_specs) + [any_spec] * sout, out_shape=list(out_shape) + list(side.outs),
        scratch_shapes=list(scratch) + list(side.sems),
        input_output_aliases={nin + i: nout + o for i, o in side.alias.items()},
        compiler_params=_params(sem))(*args, *side.ins)
    return res[:nout], res[nout:]


def _ffn_up(n, wg, wu, name, side=None, tiles=MM_TILES):
    tp, d = n.shape
    fp = wg.shape[1]
    tm = tp // tiles

    def body(n_ref, wg_ref, wu_ref, a_ref, b_ref, s_ref):
        nn = n_ref[...]
        for c0 in range(0, fp, MXU_COLS):
            cs = slice(c0, min(c0 + MXU_COLS, fp))
            a = _dot_nt(nn, wg_ref[cs, :])
            b = _dot_nt(nn, wu_ref[cs, :])
            a_ref[:, cs] = a.astype(BF)
            b_ref[:, cs] = b.astype(BF)
            s_ref[:, cs] = (a * _sigmoid(a) * b).astype(BF)

    out = jax.ShapeDtypeStruct((tp, N_CHIP * fp), BF)
    wspec = pl.BlockSpec((None, fp, d), lambda k, i: (k, 0, 0))
    ospec = pl.BlockSpec((tm, fp), lambda k, i: (i, k))
    return _grid_call(body, name, (N_CHIP, tiles), [pl.BlockSpec((tm, d), lambda k, i: (i, 0)), wspec, wspec],
                      [ospec, ospec, ospec], [out, out, out], (n, wg, wu), side)


def _ffn_up_head(n, wg, wu, name, side):
    tp, d = n.shape
    fp = wg.shape[1]
    tiles = MM_TILES
    tm = tp // tiles
    gat = _gather_side([wg, wu], relative=True, two_path=True)
    sin, sout, ngs = len(side.ins), len(side.outs), len(gat.sems)
    order = (0,) + REL_SLOT
    staged = hasattr(side, "middle")

    def body(*refs):
        n_ref = refs[0]
        si = refs[3:3 + sin]
        a_ref, b_ref, s_ref = refs[3 + sin:6 + sin]
        go = refs[6 + sin:8 + sin]
        so = refs[8 + sin:8 + sin + sout]
        wbg, wbu, wsem = refs[8 + sin + sout:11 + sin + sout]
        gsems = refs[11 + sin + sout:11 + sin + sout + ngs]
        ssems = refs[11 + sin + sout + ngs:]
        k, i = pl.program_id(0), pl.program_id(1)
        cur = k % 2

        def to_vmem(slot, buf):
            return [pltpu.make_async_copy(go[0].at[slot], wbg.at[buf], wsem.at[buf, 0]),
                    pltpu.make_async_copy(go[1].at[slot], wbu.at[buf], wsem.at[buf, 1])]

        @pl.when((k == 0) & (i == 0))
        def _():
            gat.send(go, gsems)
            if not staged:
                side.start(si, so, ssems)
            for cp in to_vmem(0, 0):
                cp.start()
            for cp in to_vmem(0, 0):
                cp.wait()

        for j in range(3):
            @pl.when((k == j) & (i == tiles // 2))
            def _():
                gat.arrived(j, go, gsems)
                if staged and j == 1:
                    side.start(si, so, ssems)

            @pl.when((k == j) & (i == tiles - 2))
            def _():
                gat.forwarded(j, go, gsems)
                for cp in to_vmem(order[j + 1], (j + 1) % 2):
                    cp.start()

            @pl.when((k == j + 1) & (i == 0))
            def _():
                for cp in to_vmem(order[j + 1], (j + 1) % 2):
                    cp.wait()

        nn = n_ref[...]
        for c0 in range(0, fp, MXU_COLS):
            cs = pl.ds(c0, min(MXU_COLS, fp - c0))
            a = _dot_nt(nn, wbg[cur, cs, :])
            b = _dot_nt(nn, wbu[cur, cs, :])
            a_ref[:, cs] = a.astype(BF)
            b_ref[:, cs] = b.astype(BF)
            s_ref[:, cs] = (a * _sigmoid(a) * b).astype(BF)

        if staged:
            @pl.when((k == N_CHIP - 1) & (i == tiles // 4))
            def _():
                side.middle(si, so, ssems)

        @pl.when((k == N_CHIP - 1) & (i == tiles - 1))
        def _():
            gat.drain(go, gsems)
            if staged:
                side.rest(si, so, ssems)
            else:
                side.finish(si, so, ssems)

    out = jax.ShapeDtypeStruct((tp, N_CHIP * fp), BF)
    any_spec = pl.BlockSpec(memory_space=pl.ANY)
    slot_of = lambda k: (k % 2) * 2 + k // 2
    ospec = pl.BlockSpec((tm, fp), lambda k, i: (i, slot_of(k)))
    wbuf = pltpu.VMEM((2, fp, d), BF)
    res = pl.pallas_call(
        body, name=name, grid=(N_CHIP, tiles),
        in_specs=[pl.BlockSpec((tm, d), lambda k, i: (i, 0))] + [any_spec] * (2 + sin),
        out_specs=[ospec, ospec, ospec] + [any_spec] * (2 + sout),
        out_shape=[out, out, out] + list(gat.outs) + list(side.outs),
        scratch_shapes=[wbuf, wbuf, pltpu.SemaphoreType.DMA((2, 2))] + list(gat.sems) + list(side.sems),
        input_output_aliases={1: 3, 2: 4, **{3 + a: 5 + b for a, b in side.alias.items()}},
        compiler_params=_params(("arbitrary", "arbitrary")))(n, wg, wu, *side.ins)
    return res[:3], res[3:5], res[5:]


def _ffn_bwd_act(df, wd, a, b, name, side=None, tiles=MM_TILES):
    tp, d = df.shape
    fp = wd.shape[1]
    tm = tp // tiles

    def body(df_ref, wd_ref, a_ref, b_ref, da_ref, db_ref):
        dfv = df_ref[...]
        for c0 in range(0, fp, MXU_COLS):
            cs = slice(c0, min(c0 + MXU_COLS, fp))
            ds = _dot_nt(dfv, wd_ref[cs, :])
            av = a_ref[:, cs].astype(F32)
            bv = b_ref[:, cs].astype(F32)
            sg = _sigmoid(av)
            da_ref[:, cs] = (ds * bv * sg * (1.0 + av * (1.0 - sg))).astype(BF)
            db_ref[:, cs] = (ds * av * sg).astype(BF)

    out = jax.ShapeDtypeStruct((tp, N_CHIP * fp), BF)
    aspec = pl.BlockSpec((tm, fp), lambda k, i: (i, k))
    return _grid_call(
        body, name, (N_CHIP, tiles),
        [pl.BlockSpec((tm, d), lambda k, i: (i, 0)), pl.BlockSpec((None, fp, d), lambda k, i: (k, 0, 0)), aspec, aspec],
        [aspec, aspec], [out, out], (df, wd, a, b), side)


def _col_matmul(lhs, w, name, trans_b, out_dtype, side=None, tiles=MM_TILES_BIG):
    tp, kd = lhs.shape
    nk = w.shape[0]
    nc = w.shape[1] if trans_b else w.shape[2]
    tm = tp // tiles

    def body(l_ref, w_ref, o_ref):
        if trans_b:
            o_ref[...] = _dot_nt(l_ref[...], w_ref[...]).astype(out_dtype)
        else:
            o_ref[...] = _dot(l_ref[...], w_ref[...]).astype(out_dtype)

    res, extra = _grid_call(
        body, name, (nk, tiles),
        [pl.BlockSpec((tm, kd), lambda k, i: (i, 0)),
         pl.BlockSpec((None,) + tuple(w.shape[1:]), lambda k, i: (k, 0, 0), pipeline_mode=pl.Buffered(1))],
        [pl.BlockSpec((tm, nc), lambda k, i: (i, k))], [jax.ShapeDtypeStruct((tp, nk * nc), out_dtype)], (lhs, w), side)
    return res[0], extra


def _row_matmul(pairs, name, trans_b, d_out, side=None, tiles=MM_TILES_BIG):
    l0 = pairs[0][0]
    tp = l0.shape[1] if l0.ndim == 3 else l0.shape[0]
    nk = pairs[0][1].shape[0]
    tm = tp // tiles
    npair = len(pairs)

    def body(*refs):
        o_ref = refs[2 * npair]
        k = pl.program_id(1)
        part = None
        for q in range(npair):
            l = refs[2 * q][...]
            w = refs[2 * q + 1][...]
            t = _dot_nt(l, w) if trans_b else _dot(l, w)
            part = t if part is None else part + t

        @pl.when(k == 0)
        def _():
            o_ref[...] = part

        @pl.when(k > 0)
        def _():
            o_ref[...] += part

    in_specs, args = [], []
    for lhs, w in pairs:
        if lhs.ndim == 3:
            in_specs.append(pl.BlockSpec((None, tm, lhs.shape[2]), lambda i, k: (k, i, 0)))
        else:
            in_specs.append(pl.BlockSpec((tm, lhs.shape[1] // nk), lambda i, k: (i, k)))
        in_specs.append(pl.BlockSpec((None,) + tuple(w.shape[1:]), lambda i, k: (k, 0, 0)))
        args += [lhs, w]
    res, extra = _grid_call(body, name, (tiles, nk), in_specs, [pl.BlockSpec((tm, d_out), lambda i, k: (i, 0))],
                            [jax.ShapeDtypeStruct((tp, d_out), F32)], args, side)
    return res[0], extra


def _wide_matmul(pairs, name, tn, side=None):
    tp = pairs[0][0].shape[0]
    d_out = pairs[0][1].shape[2]
    tm = tp // MM_TILES
    npair = len(pairs)

    def body(*refs):
        acc = None
        for q in range(npair):
            t = _dot(refs[2 * q][...], refs[2 * q + 1][...])
            acc = t if acc is None else acc + t
        refs[2 * npair][...] = acc

    in_specs, args = [], []
    for lhs, w in pairs:
        kdim = lhs.shape[1]
        in_specs += [pl.BlockSpec((tm, kdim), lambda n, i: (i, 0)), pl.BlockSpec((kdim, tn), lambda n, i: (0, n))]
        args += [lhs, w.reshape(kdim, d_out)]
    res, extra = _grid_call(body, name, (d_out // tn, MM_TILES), in_specs, [pl.BlockSpec((tm, tn), lambda n, i: (i, n))],
                            [jax.ShapeDtypeStruct((tp, d_out), F32)], args, side)
    return res[0], extra


def _wgrad_call(x, y, name, x_width=None, y_width=None, tile_x=None, tile_y=None, side=None):
    tp = x.shape[1] if x.ndim == 3 else x.shape[0]

    def spec(a, width, tile):
        cols = a.shape[2] if a.ndim == 3 else (a.shape[1] if width is None else width)
        tc = cols if tile is None else tile
        per = cols // tc
        if a.ndim == 3:
            return pl.BlockSpec((None, tp, tc), lambda k, t: (k, 0, t if tile else 0)), cols, per
        if width is None:
            return pl.BlockSpec((tp, tc), lambda k, t: (0, t if tile else 0)), cols, per
        return pl.BlockSpec((tp, tc), lambda k, t: (0, k * per + (t if tile else 0))), cols, per

    xs, p, nx = spec(x, x_width, tile_x)
    ys, q, ny = spec(y, y_width, tile_y)
    nt = nx * ny
    if tile_x:
        ospec = pl.BlockSpec((None, tile_x, q), lambda k, t: (k, t, 0))
    else:
        ospec = pl.BlockSpec((None, p, tile_y), lambda k, t: (k, 0, t))

    def body(x_ref, y_ref, o_ref):
        o_ref[...] = _dot_tn(x_ref[...], y_ref[...]).astype(BF)

    res, extra = _grid_call(body, name, (N_CHIP, nt), [xs, ys], [ospec], [jax.ShapeDtypeStruct((N_CHIP, p, q), BF)],
                            (x, y), side)
    return res[0], extra


def _row_call(body, name, tp, d, row_ins, vec_ins, row_out_dtypes, n_acc, side=None):
    te = tp // EW_TILES
    rspec = pl.BlockSpec((te, d), lambda i: (i, 0))
    vspec = pl.BlockSpec((1, d), lambda i: (0, 0))
    res, extra = _grid_call(
        body, name, (EW_TILES,), [rspec] * len(row_ins) + [vspec] * len(vec_ins),
        [rspec] * len(row_out_dtypes) + [vspec] * n_acc,
        [jax.ShapeDtypeStruct((tp, d), dt) for dt in row_out_dtypes] + [jax.ShapeDtypeStruct((1, d), F32)] * n_acc,
        (*row_ins, *vec_ins), side)
    return res if side is None else (res, extra)


def _norm0(h, g):
    tp, d = h.shape

    def body(h_ref, g_ref, n_ref):
        n_ref[...] = _rms(h_ref[...], g_ref[...]).astype(BF)

    return _row_call(body, "norm0", tp, d, [h], [g], [BF], 0)[0]


def _post_fwd(f, h, g_post, g_next, scale, name):
    tp, d = h.shape

    def body(f_ref, h_ref, gp_ref, gn_ref, hn_ref, n_ref):
        hn = h_ref[...] + scale * _rms(f_ref[...], gp_ref[...])
        hn_ref[...] = hn
        n_ref[...] = _rms(hn, gn_ref[...]).astype(BF)

    return _row_call(body, name, tp, d, [f, h], [g_post, g_next], [F32, BF], 0)


def _loss_bwd(f, h, tgt, g_post, t_real):
    tp, d = h.shape
    te = tp // EW_TILES

    def body(f_ref, h_ref, t_ref, gp_ref, dh_ref, df_ref, dg_ref, loss_ref):
        i = pl.program_id(0)

        @pl.when(i == 0)
        def _():
            dg_ref[...] = jnp.zeros_like(dg_ref)
            loss_ref[...] = jnp.zeros_like(loss_ref)

        f = f_ref[...]
        gp = gp_ref[...]
        h3 = h_ref[...] + 0.5 * _rms(f, gp)
        rows = i * te + lax.broadcasted_iota(jnp.int32, (te, 1), 0)
        real = (rows >= N_META) & (rows < t_real)
        e = jnp.where(real, h3 - t_ref[...], 0.0)
        loss_ref[...] += 0.5 * jnp.sum(jnp.sum(e * e, axis=1, keepdims=True), axis=0, keepdims=True) / d
        dh = e / d
        dh_ref[...] = dh
        dfv, dgr = _rms_bwd(f, gp, 0.5 * dh)
        df_ref[...] = dfv.astype(BF)
        dg_ref[...] += jnp.sum(dgr, axis=0, keepdims=True)

    rspec = pl.BlockSpec((te, d), lambda i: (i, 0))
    vspec = pl.BlockSpec((1, d), lambda i: (0, 0))
    return pl.pallas_call(
        body, name="loss_bwd", grid=(EW_TILES,),
        in_specs=[rspec, rspec, rspec, vspec],
        out_specs=[rspec, rspec, vspec, pl.BlockSpec((1, 1), lambda i: (0, 0))],
        out_shape=[jax.ShapeDtypeStruct((tp, d), F32), jax.ShapeDtypeStruct((tp, d), BF),
                   jax.ShapeDtypeStruct((1, d), F32), jax.ShapeDtypeStruct((1, 1), F32)],
        compiler_params=_params(("arbitrary",)),
    )(f, h, tgt, g_post)


def _pre_bwd(dn, h, dh_out, g_pre, name, chain=None, side=None):
    tp, d = h.shape

    def body(*refs):
        if chain is None:
            dn_ref, h_ref, dho_ref, g_ref, dh_ref, dg_ref = refs
        else:
            dn_ref, h_ref, dho_ref, p_ref, g_ref, gp_ref, dh_ref, dp_ref, dg_ref, dgp_ref = refs
        i = pl.program_id(0)

        @pl.when(i == 0)
        def _():
            dg_ref[...] = jnp.zeros_like(dg_ref)
            if chain is not None:
                dgp_ref[...] = jnp.zeros_like(dgp_ref)

        dx, dgr = _rms_bwd(h_ref[...], g_ref[...], dn_ref[...])
        dh = dho_ref[...] + dx
        dh_ref[...] = dh
        dg_ref[...] += jnp.sum(dgr, axis=0, keepdims=True)
        if chain is not None:
            dp, dgpr = _rms_bwd(p_ref[...], gp_ref[...], chain[2] * dh)
            dp_ref[...] = dp.astype(BF)
            dgp_ref[...] += jnp.sum(dgpr, axis=0, keepdims=True)

    if chain is None:
        return _row_call(body, name, tp, d, [dn, h, dh_out], [g_pre], [F32], 1, side)
    return _row_call(body, name, tp, d, [dn, h, dh_out, chain[0]], [g_pre, chain[1]], [F32, BF], 2, side)


def _gelu(y):
    c = math.sqrt(2.0 / math.pi)
    return 0.5 * y * (1.0 + jnp.tanh(c * (y + 0.044715 * y * y * y)))


def _gelu_grad(y):
    c = math.sqrt(2.0 / math.pi)
    t = jnp.tanh(c * (y + 0.044715 * y * y * y))
    return 0.5 * (1.0 + t) + 0.5 * y * (1.0 - t * t) * c * (1.0 + 3.0 * 0.044715 * y * y)


def _neg_expm1(x):
    p = 1.0 + x * (1.0 / 9.0)
    for n in (8.0, 7.0, 6.0, 5.0, 4.0, 3.0, 2.0):
        p = 1.0 + x * (1.0 / n) * p
    return -jnp.where(x > -0.35, x * p, jnp.exp(x) - 1.0)


def _softplus(x):
    e = jnp.exp(-jnp.abs(x))
    w = 1.0 + e
    l1p = jnp.where(w == 1.0, e, jnp.log(w) * (e / jnp.where(w == 1.0, 1.0, w - 1.0)))
    return jnp.maximum(x, 0.0) + l1p


def _group_mean(v, gm):
    hi = v.astype(BF)
    lo = (v - hi.astype(F32)).astype(BF)
    return _dot(hi, gm) + _dot(lo, gm)


def _shift_dn(win, s, r):
    if s == 0:
        return win[8:8 + r]
    return pltpu.roll(win, s, 0)[8:8 + r]


def _shift_up(win, s, r):
    if s == 0:
        return win[0:r]
    return pltpu.roll(win, r + 8 - s, 0)[0:r]


def _window_dn(ref, t0, r, first):
    if first:
        return jnp.concatenate([jnp.zeros((8, ref.shape[1]), F32), ref[0:r, :]], axis=0)
    return ref[pl.ds(t0 - 8, r + 8), :]


def _tile_scan(a, u, reverse):
    r = a.shape[0]
    rid = lax.broadcasted_iota(jnp.int32, a.shape, 0) & 7
    for dlt in (1, 2, 4):
        sh = (r - dlt) if reverse else dlt
        a_s = pltpu.roll(a, sh, 0)
        u_s = pltpu.roll(u, sh, 0)
        keep = (rid + dlt <= 7) if reverse else (rid >= dlt)
        u = jnp.where(keep, u + a * u_s, u)
        a = jnp.where(keep, a * a_s, a)
    return a, u


def _lru_gates(xc, wa, ba, wx, bx, sp):
    xb = xc.astype(BF)
    ga = _sigmoid(_dot(xb, wa) + ba)
    gx = _sigmoid(_dot(xb, wx) + bx)
    la = -LRU_C * ga * sp
    return ga, gx, la


def _conv4(win, w4, cb, r):
    return (cb + w4[3:4] * _shift_dn(win, 0, r) + w4[2:3] * _shift_dn(win, 1, r)
            + w4[1:2] * _shift_dn(win, 2, r) + w4[0:1] * _shift_dn(win, 3, r))


def _lru_fwd(z, w4, cb, wa2, ba, wx2, bx, lam, g_out, gm, side=None):
    tp = z.shape[0]
    dl = cb.shape[1]
    nb = dl // LANE
    r = tp // MIX_CHUNKS
    c = LANE

    def body(y_ref, x_ref, w4_ref, cb_ref, wa_ref, ba_ref, wx_ref, bx_ref, lam_ref, go_ref, gm_ref, m_ref, hs_ref):
        w4v = w4_ref[...]
        cbv = cb_ref[...]
        wa = wa_ref[...]
        wx = wx_ref[...]
        bav = ba_ref[...]
        bxv = bx_ref[...]
        gov = go_ref[...]
        gmv = gm_ref[...]
        sp = _softplus(-lam_ref[...])

        def chunk(t0, hprev, first):
            win = _window_dn(x_ref, t0, r, first)
            xc = _conv4(win, w4v, cbv, r)
            ga, gx, la = _lru_gates(xc, wa, bav, wx, bxv, sp)
            a = jnp.exp(la)
            u = jnp.sqrt(_neg_expm1(2.0 * la)) * gx * xc
            ac, uc = _tile_scan(a, u, False)
            for j in range(r // 8):
                hj = uc[8 * j:8 * j + 8] + ac[8 * j:8 * j + 8] * hprev
                hs_ref[pl.ds(t0 + 8 * j, 8), :] = hj
                hprev = jnp.broadcast_to(hj[7:8], (8, c))
            h = hs_ref[pl.ds(t0, r), :]
            lo = h * _gelu(y_ref[pl.ds(t0, r), :])
            rs = lax.rsqrt(_group_mean(lo * lo, gmv) + EPS)
            m_ref[pl.ds(t0, r), :] = (lo * rs * gov).astype(BF)
            return hprev

        hp = chunk(0, jnp.zeros((8, c), F32), True)

        def loop(ci, hp):
            return chunk(pl.multiple_of(ci * r, 16), hp, False)

        lax.fori_loop(1, MIX_CHUNKS, loop, hp)

    col = lambda off: pl.BlockSpec((tp, c), lambda j: (0, off + j))
    vec = pl.BlockSpec((1, c), lambda j: (0, j))
    return _grid_call(
        body, "lru_fwd", (nb,),
        [col(0), col(nb), pl.BlockSpec((8, c), lambda j: (0, j)), vec, pl.BlockSpec((None, c, c), lambda j: (j, 0, 0)),
         vec, pl.BlockSpec((None, c, c), lambda j: (j, 0, 0)), vec, vec, vec, pl.BlockSpec((c, c), lambda j: (0, 0))],
        [col(0), col(0)], [jax.ShapeDtypeStruct((tp, dl), BF), jax.ShapeDtypeStruct((tp, dl), F32)],
        (z, z, w4, cb, wa2, ba, wx2, bx, lam, g_out, gm), side)


def _lru_bwd(z, hs, dmix, w4, cb, wa2, ba, wx2, bx, lam, g_out, gm, side=None):
    tp = z.shape[0]
    dl = cb.shape[1]
    nb = dl // LANE
    r = tp // MIX_CHUNKS
    c = LANE

    def body(y_ref, x_ref, hs_ref, dm_ref, w4_ref, cb_ref, wa_ref, ba_ref, wx_ref, bx_ref, lam_ref, go_ref, gm_ref,
             dy_ref, dx_ref, small_ref, dwa_ref, dwx_ref, xc_buf, ga_buf, gx_buf, a_buf, dh_buf, dxc_buf):
        w4v = w4_ref[...]
        cbv = cb_ref[...]
        wa = wa_ref[...]
        wx = wx_ref[...]
        bav = ba_ref[...]
        bxv = bx_ref[...]
        gov = go_ref[...]
        gmv = gm_ref[...]
        lamv = lam_ref[...]
        sp = _softplus(-lamv)
        small_ref[...] = jnp.zeros_like(small_ref)
        dwa_ref[...] = jnp.zeros_like(dwa_ref)
        dwx_ref[...] = jnp.zeros_like(dwx_ref)
        a_buf[pl.ds(tp, 8), :] = jnp.zeros((8, c), F32)
        dxc_buf[pl.ds(tp, 8), :] = jnp.zeros((8, c), F32)

        def fwd_chunk(t0, first):
            win = _window_dn(x_ref, t0, r, first)
            xc = _conv4(win, w4v, cbv, r)
            ga, gx, la = _lru_gates(xc, wa, bav, wx, bxv, sp)
            xc_buf[pl.ds(t0, r), :] = xc
            ga_buf[pl.ds(t0, r), :] = ga
            gx_buf[pl.ds(t0, r), :] = gx
            a_buf[pl.ds(t0, r), :] = jnp.exp(la)
            h = hs_ref[pl.ds(t0, r), :]
            yv = y_ref[pl.ds(t0, r), :]
            ge = _gelu(yv)
            lo = h * ge
            rs = lax.rsqrt(_group_mean(lo * lo, gmv) + EPS)
            xh = lo * rs
            dm = dm_ref[pl.ds(t0, r), :]
            q = dm * gov
            dlo = rs * (q - xh * _group_mean(q * xh, gmv))
            small_ref[8:9, :] += jnp.sum(dm * xh, axis=0, keepdims=True)
            dh_buf[pl.ds(t0, r), :] = dlo * ge
            dy_ref[pl.ds(t0, r), :] = (dlo * h * _gelu_grad(yv)).astype(BF)

        fwd_chunk(0, True)

        def floop(ci, carry):
            fwd_chunk(pl.multiple_of(ci * r, 16), False)
            return carry

        lax.fori_loop(1, MIX_CHUNKS, floop, 0)

        def bwd_chunk(t0, vnext, first):
            ap = _shift_up(a_buf[pl.ds(t0, r + 8), :], 1, r)
            ac, uc = _tile_scan(ap, dh_buf[pl.ds(t0, r), :], True)
            for j in reversed(range(r // 8)):
                vj = uc[8 * j:8 * j + 8] + ac[8 * j:8 * j + 8] * vnext
                dh_buf[pl.ds(t0 + 8 * j, 8), :] = vj
                vnext = jnp.broadcast_to(vj[0:1], (8, c))
            v = dh_buf[pl.ds(t0, r), :]
            hprev = _shift_dn(_window_dn(hs_ref, t0, r, first), 1, r)
            xc = xc_buf[pl.ds(t0, r), :]
            ga = ga_buf[pl.ds(t0, r), :]
            gx = gx_buf[pl.ds(t0, r), :]
            a = a_buf[pl.ds(t0, r), :]
            em = _neg_expm1(-2.0 * LRU_C * ga * sp)
            mult = jnp.sqrt(em)
            dla = v * hprev * a - (v * gx * xc) * ((1.0 - em) / mult)
            dgx = v * mult * xc
            dxc = v * mult * gx
            dga = dla * (-LRU_C) * sp
            small_ref[7:8, :] += jnp.sum(dla * (-LRU_C) * ga, axis=0, keepdims=True)
            dpa = dga * ga * (1.0 - ga)
            dpx = dgx * gx * (1.0 - gx)
            small_ref[5:6, :] += jnp.sum(dpa, axis=0, keepdims=True)
            small_ref[6:7, :] += jnp.sum(dpx, axis=0, keepdims=True)
            dpab = dpa.astype(BF)
            dpxb = dpx.astype(BF)
            xb = xc.astype(BF)
            dxc = dxc + _dot_nt(dpab, wa) + _dot_nt(dpxb, wx)
            dwa_ref[...] += _dot_tn(xb, dpab)
            dwx_ref[...] += _dot_tn(xb, dpxb)
            dxc_buf[pl.ds(t0, r), :] = dxc
            small_ref[4:5, :] += jnp.sum(dxc, axis=0, keepdims=True)
            dwin = dxc_buf[pl.ds(t0, r + 8), :]
            dx_ref[pl.ds(t0, r), :] = (w4v[3:4] * dxc + w4v[2:3] * _shift_up(dwin, 1, r)
                                       + w4v[1:2] * _shift_up(dwin, 2, r) + w4v[0:1] * _shift_up(dwin, 3, r)).astype(BF)
            xwin = _window_dn(x_ref, t0, r, first)
            for k in range(4):
                small_ref[k:k + 1, :] += jnp.sum(dxc * _shift_dn(xwin, 3 - k, r), axis=0, keepdims=True)
            return vnext

        def bloop(it, vnext):
            ci = MIX_CHUNKS - 1 - it
            return bwd_chunk(pl.multiple_of(ci * r, 16), vnext, False)

        vn = lax.fori_loop(0, MIX_CHUNKS - 1, bloop, jnp.zeros((8, c), F32))
        bwd_chunk(0, vn, True)
        small_ref[7:8, :] = small_ref[7:8, :] * (-_sigmoid(-lamv))

    col = lambda off: pl.BlockSpec((tp, c), lambda j: (0, off + j))
    vec = pl.BlockSpec((1, c), lambda j: (0, j))
    mat = pl.BlockSpec((None, c, c), lambda j: (j, 0, 0))
    buf = pltpu.VMEM((tp, c), F32)
    bufp = pltpu.VMEM((tp + 8, c), F32)
    return _grid_call(
        body, "lru_bwd", (nb,),
        [col(0), col(nb), col(0), col(0), pl.BlockSpec((8, c), lambda j: (0, j)), vec, mat, vec, mat, vec, vec, vec,
         pl.BlockSpec((c, c), lambda j: (0, 0))],
        [col(0), col(0), pl.BlockSpec((16, c), lambda j: (0, j)), mat, mat],
        [jax.ShapeDtypeStruct((tp, dl), BF), jax.ShapeDtypeStruct((tp, dl), BF), jax.ShapeDtypeStruct((16, dl), F32),
         jax.ShapeDtypeStruct((nb, c, c), F32), jax.ShapeDtypeStruct((nb, c, c), F32)],
        (z, z, hs, dmix, w4, cb, wa2, ba, wx2, bx, lam, g_out, gm), side, [buf, buf, buf, bufp, buf, bufp])


def _sc_conv(cvwin, w3, r):
    return w3[2:3] * _shift_dn(cvwin, 0, r) + w3[1:2] * _shift_dn(cvwin, 1, r) + w3[0:1] * _shift_dn(cvwin, 2, r)


def _sc_fwd(z, w3, g_out, gm, dl, side=None):
    tp = z.shape[0]
    nb = dl // LANE
    r = tp // MIX_CHUNKS
    c = LANE

    def body(b_ref, c_ref, v_ref, w3_ref, go_ref, gm_ref, m_ref):
        w3v = w3_ref[...]
        gov = go_ref[...]
        gmv = gm_ref[...]

        def chunk(t0, first):
            cvwin = _window_dn(c_ref, t0, r, first) * _window_dn(v_ref, t0, r, first)
            so = b_ref[pl.ds(t0, r), :] * _sc_conv(cvwin, w3v, r)
            rs = lax.rsqrt(_group_mean(so * so, gmv) + EPS)
            m_ref[pl.ds(t0, r), :] = (so * rs * gov).astype(BF)

        chunk(0, True)

        def loop(ci, carry):
            chunk(pl.multiple_of(ci * r, 16), False)
            return carry

        lax.fori_loop(1, MIX_CHUNKS, loop, 0)

    col = lambda off: pl.BlockSpec((tp, c), lambda j: (0, off + j))
    res, extra = _grid_call(
        body, "sconv_fwd", (nb,),
        [col(2 * nb), col(3 * nb), col(4 * nb), pl.BlockSpec((8, c), lambda j: (0, j)),
         pl.BlockSpec((1, c), lambda j: (0, j)), pl.BlockSpec((c, c), lambda j: (0, 0))],
        [col(0)], [jax.ShapeDtypeStruct((tp, dl), BF)], (z, z, z, w3, g_out, gm), side)
    return res[0], extra


def _sc_bwd(z, dmix, w3, g_out, gm, dl, side=None):
    tp = z.shape[0]
    nb = dl // LANE
    r = tp // MIX_CHUNKS
    c = LANE

    def body(b_ref, c_ref, v_ref, dm_ref, w3_ref, go_ref, gm_ref, db_ref, dc_ref, dv_ref, small_ref, dsc_buf):
        w3v = w3_ref[...]
        gov = go_ref[...]
        gmv = gm_ref[...]
        small_ref[...] = jnp.zeros_like(small_ref)
        dsc_buf[pl.ds(tp, 8), :] = jnp.zeros((8, c), F32)

        def chunk1(t0, first):
            cvwin = _window_dn(c_ref, t0, r, first) * _window_dn(v_ref, t0, r, first)
            sc = _sc_conv(cvwin, w3v, r)
            bv = b_ref[pl.ds(t0, r), :]
            so = bv * sc
            rs = lax.rsqrt(_group_mean(so * so, gmv) + EPS)
            xh = so * rs
            dm = dm_ref[pl.ds(t0, r), :]
            q = dm * gov
            dso = rs * (q - xh * _group_mean(q * xh, gmv))
            small_ref[3:4, :] += jnp.sum(dm * xh, axis=0, keepdims=True)
            db_ref[pl.ds(t0, r), :] = (dso * sc).astype(BF)
            dsc = dso * bv
            dsc_buf[pl.ds(t0, r), :] = dsc
            for k in range(3):
                small_ref[k:k + 1, :] += jnp.sum(dsc * _shift_dn(cvwin, 2 - k, r), axis=0, keepdims=True)

        chunk1(0, True)

        def loop1(ci, carry):
            chunk1(pl.multiple_of(ci * r, 16), False)
            return carry

        lax.fori_loop(1, MIX_CHUNKS, loop1, 0)

        def loop2(ci, carry):
            t0 = pl.multiple_of(ci * r, 16)
            dwin = dsc_buf[pl.ds(t0, r + 8), :]
            dcv = w3v[2:3] * _shift_up(dwin, 0, r) + w3v[1:2] * _shift_up(dwin, 1, r) + w3v[0:1] * _shift_up(dwin, 2, r)
            dc_ref[pl.ds(t0, r), :] = (dcv * v_ref[pl.ds(t0, r), :]).astype(BF)
            dv_ref[pl.ds(t0, r), :] = (dcv * c_ref[pl.ds(t0, r), :]).astype(BF)
            return carry

        lax.fori_loop(0, MIX_CHUNKS, loop2, 0)

    col = lambda off: pl.BlockSpec((tp, c), lambda j: (0, off + j))
    out = jax.ShapeDtypeStruct((tp, dl), BF)
    return _grid_call(
        body, "sconv_bwd", (nb,),
        [col(2 * nb), col(3 * nb), col(4 * nb), col(nb), pl.BlockSpec((8, c), lambda j: (0, j)),
         pl.BlockSpec((1, c), lambda j: (0, j)), pl.BlockSpec((c, c), lambda j: (0, 0))],
        [col(0), col(0), col(0), pl.BlockSpec((8, c), lambda j: (0, j))],
        [out, out, out, jax.ShapeDtypeStruct((8, dl), F32)], (z, z, z, dmix, w3, g_out, gm), side,
        [pltpu.VMEM((tp + 8, c), F32)])


def _cast_pad(w, rows_p, cols_p, chip, name):
    r, c = w.shape

    def body(chip_ref, w_ref, o_ref):
        if (rows_p, cols_p) != (r, c):
            o_ref[...] = jnp.zeros_like(o_ref)
        o_ref[0:r, 0:c] = w_ref[...].astype(BF)

    return pl.pallas_call(
        body, name=name, out_shape=jax.ShapeDtypeStruct((N_CHIP, rows_p, cols_p), BF),
        grid_spec=pltpu.PrefetchScalarGridSpec(
            num_scalar_prefetch=1, grid=(1,),
            in_specs=[pl.BlockSpec((r, c), lambda i, chip: (0, 0))],
            out_specs=pl.BlockSpec((None, rows_p, cols_p), lambda i, chip: (chip[0], 0, 0))),
        compiler_params=_params(("arbitrary",)),
    )(chip, w)


def _adamw_math(w, g, m, v):
    m2 = ADAM_B1 * m + (1.0 - ADAM_B1) * g
    v2 = ADAM_B2 * v + (1.0 - ADAM_B2) * (g * g)
    m_hat = m2 / (1.0 - ADAM_B1 ** ADAM_STEP)
    v_hat = v2 / (1.0 - ADAM_B2 ** ADAM_STEP)
    delta = -ADAM_LR * (m_hat / (jnp.sqrt(v_hat) + ADAM_EPS) + ADAM_WD * w)
    return delta, m2, v2


def _adamw(w, g, m, v, name, row_tiles, col_tiles, side=None):
    r, c = w.shape
    tr = r // row_tiles
    tc = c // col_tiles
    gc = g.shape[1] if col_tiles == 1 else tc

    def body(w_ref, g_ref, m_ref, v_ref, go_ref, d_ref, mo_ref, vo_ref):
        gv = g_ref[...][:, 0:tc]
        delta, m2, v2 = _adamw_math(w_ref[...], gv, m_ref[...], v_ref[...])
        go_ref[...] = gv
        d_ref[...] = delta
        mo_ref[...] = m2
        vo_ref[...] = v2

    spec = pl.BlockSpec((tr, tc), lambda i, j: (i, j))
    out = jax.ShapeDtypeStruct((r, c), F32)
    return _grid_call(body, name, (row_tiles, col_tiles), [spec, pl.BlockSpec((tr, gc), lambda i, j: (i, j)), spec, spec],
                      [spec] * 4, [out] * 4, (w, g, m, v), side)


def _adamw_small(w, g_top, g4, m, v):
    def body(w_ref, gt_ref, g_ref, m_ref, v_ref, go_ref, d_ref, mo_ref, vo_ref):
        g = jnp.concatenate([gt_ref[...], (g_ref[0] + g_ref[1]) + (g_ref[2] + g_ref[3])], axis=0)
        delta, m2, v2 = _adamw_math(w_ref[...], g, m_ref[...], v_ref[...])
        go_ref[...] = g
        d_ref[...] = delta
        mo_ref[...] = m2
        vo_ref[...] = v2

    out = jax.ShapeDtypeStruct(w.shape, F32)
    spec = pl.BlockSpec(w.shape, lambda: (0, 0))
    return pl.pallas_call(
        body, name="adamw_small",
        in_specs=[spec, pl.BlockSpec(g_top.shape, lambda: (0, 0)), pl.BlockSpec(g4.shape, lambda: (0, 0, 0)), spec, spec],
        out_specs=[spec] * 4, out_shape=[out] * 4, compiler_params=_params())(w, g_top, g4, m, v)


def _place():
    x, y, c = lax.axis_index("x"), lax.axis_index("y"), lax.axis_index("c")
    chips = [(1 - x, y), (x, 1 - y), (1 - x, 1 - y)]
    return x, y, c, chips


ANY = pl.BlockSpec(memory_space=pl.ANY)


REL_SLOT = (2, 1, 3)


def _gather_side(bufs, relative=False, two_path=False):
    n = len(bufs)
    direct = (0, 1) if two_path else (0, 1, 2)

    def copies(outs, sems):
        s_ici, r_ici, s_d2d, r_d2d = sems[:4]
        x, y, c, chips = _place()
        me = 2 * x + y

        def rows(w, slot, core, part=None):
            half = bufs[w].shape[1] // 2
            if part is None:
                return outs[w].at[slot, pl.ds(core * half, half)]
            return outs[w].at[slot, pl.ds(core * half + part * (half // 2), half // 2)]

        def theirs(j):
            return REL_SLOT[j] if relative else 2 * chips[j][0] + chips[j][1]

        def ici_send(w, j):
            px, py = chips[j]
            return pltpu.make_async_remote_copy(
                src_ref=rows(w, 0 if relative else me, c), dst_ref=rows(w, REL_SLOT[j] if relative else me, c),
                send_sem=s_ici.at[w, j], recv_sem=r_ici.at[w, j], device_id=(px, py, c), device_id_type=MESH)

        def ici_recv(w, j):
            px, py = chips[j]
            return pltpu.make_async_remote_copy(
                src_ref=rows(w, theirs(j), c), dst_ref=rows(w, theirs(j), c),
                send_sem=s_ici.at[w, j], recv_sem=r_ici.at[w, j], device_id=(px, py, c), device_id_type=MESH)

        def hop_send(w, p):
            px, py = chips[1 - p]
            return pltpu.make_async_remote_copy(
                src_ref=rows(w, theirs(p), c, p), dst_ref=rows(w, REL_SLOT[2] if relative else theirs(p), c, p),
                send_sem=sems[4].at[w, p], recv_sem=sems[5].at[w, p], device_id=(px, py, c), device_id_type=MESH)

        def hop_recv(w, p):
            px, py = chips[1 - p]
            return pltpu.make_async_remote_copy(
                src_ref=rows(w, theirs(2), c, p), dst_ref=rows(w, theirs(2), c, p),
                send_sem=sems[4].at[w, p], recv_sem=sems[5].at[w, p], device_id=(px, py, c), device_id_type=MESH)

        def d2d(w, j, core):
            return pltpu.make_async_remote_copy(
                src_ref=rows(w, theirs(j), core), dst_ref=rows(w, theirs(j), core),
                send_sem=s_d2d.at[w, j], recv_sem=r_d2d.at[w, j], device_id=(x, y, 1 - c), device_id_type=MESH)

        return c, ici_send, ici_recv, hop_send, hop_recv, d2d

    def send(outs, sems):
        c, ici_send, ici_recv, hop_send, hop_recv, d2d = copies(outs, sems)
        for j in direct:
            for w in range(n):
                ici_send(w, j).start()

    def arrived(j, outs, sems):
        c, ici_send, ici_recv, hop_send, hop_recv, d2d = copies(outs, sems)
        for w in range(n):
            if j in direct:
                ici_recv(w, j).wait_recv()
                if two_path:
                    hop_send(w, j).start()
            else:
                hop_recv(w, 0).wait_recv()
                hop_recv(w, 1).wait_recv()
            d2d(w, j, c).start()

    def forwarded(j, outs, sems):
        c, ici_send, ici_recv, hop_send, hop_recv, d2d = copies(outs, sems)
        for w in range(n):
            d2d(w, j, 1 - c).wait_recv()

    def drain(outs, sems):
        c, ici_send, ici_recv, hop_send, hop_recv, d2d = copies(outs, sems)
        for w in range(n):
            for j in direct:
                ici_send(w, j).wait_send()
                if two_path:
                    hop_send(w, j).wait_send()
            for j in range(3):
                d2d(w, j, c).wait_send()

    def start(ins, outs, sems):
        send(outs, sems)

    def middle(ins, outs, sems):
        arrived(0, outs, sems)
        arrived(1, outs, sems)

    def rest(ins, outs, sems):
        arrived(2, outs, sems)
        for j in range(3):
            forwarded(j, outs, sems)
        drain(outs, sems)

    def finish(ins, outs, sems):
        middle(ins, outs, sems)
        rest(ins, outs, sems)

    dma = pltpu.SemaphoreType.DMA((n, 3))
    hop = [pltpu.SemaphoreType.DMA((n, 2))] * 2 if two_path else []
    side = _Side(list(bufs), [jax.ShapeDtypeStruct(b.shape, b.dtype) for b in bufs], {w: w for w in range(n)},
                 [dma, dma, dma, dma] + hop, start, finish)
    side.send, side.arrived, side.forwarded, side.drain = send, arrived, forwarded, drain
    side.middle, side.rest = middle, rest
    return side


def _run_side(side, name):
    sin, sout = len(side.ins), len(side.outs)

    def body(*refs):
        ins, outs, sems = refs[:sin], refs[sin:sin + sout], refs[sin + sout:]
        side.start(ins, outs, sems)
        side.finish(ins, outs, sems)

    return pl.pallas_call(
        body, name=name, out_shape=list(side.outs), in_specs=[ANY] * sin, out_specs=[ANY] * sout,
        scratch_shapes=list(side.sems), input_output_aliases=dict(side.alias))(*side.ins)


def _pair_exchange_side(grads):
    n = len(grads)

    def copies(ins, outs, sems):
        ssem, rsem = sems
        x, y, c, _ = _place()
        cps = []
        for w in range(n):
            half = grads[w].shape[1] // 2
            cps.append(pltpu.make_async_remote_copy(
                src_ref=ins[w].at[:, pl.ds((1 - c) * half, half)], dst_ref=outs[w],
                send_sem=ssem.at[w], recv_sem=rsem.at[w], device_id=(x, y, 1 - c), device_id_type=MESH))
        return cps

    def start(ins, outs, sems):
        for cp in copies(ins, outs, sems):
            cp.start()

    def finish(ins, outs, sems):
        for cp in copies(ins, outs, sems):
            cp.wait()

    dma = pltpu.SemaphoreType.DMA((n,))
    return _Side(list(grads), [jax.ShapeDtypeStruct((N_CHIP, g.shape[1] // 2, g.shape[2]), BF) for g in grads], {},
                 [dma, dma], start, finish)


def _sibling_copy_side(buf):
    def copy(ins, outs, sems):
        x, y, c, _ = _place()
        return pltpu.make_async_remote_copy(src_ref=ins[0], dst_ref=outs[0], send_sem=sems[0], recv_sem=sems[1],
                                            device_id=(x, y, 1 - c), device_id_type=MESH)

    return _Side([buf], [jax.ShapeDtypeStruct(buf.shape, buf.dtype)], {}, [pltpu.SemaphoreType.DMA, pltpu.SemaphoreType.DMA],
                 lambda i, o, s: copy(i, o, s).start(), lambda i, o, s: copy(i, o, s).wait())


def _slot_exchange_side(buf4):
    def copies(outs, sems, sending):
        ssem, rsem = sems
        x, y, c, chips = _place()
        me = 2 * x + y
        return [pltpu.make_async_remote_copy(
            src_ref=outs[0].at[me if sending else 2 * px + py], dst_ref=outs[0].at[me if sending else 2 * px + py],
            send_sem=ssem.at[j], recv_sem=rsem.at[j], device_id=(px, py, c), device_id_type=MESH)
            for j, (px, py) in enumerate(chips)]

    def start(ins, outs, sems):
        for cp in copies(outs, sems, True):
            cp.start()

    def finish(ins, outs, sems):
        for cp in copies(outs, sems, False):
            cp.wait_recv()
        for cp in copies(outs, sems, True):
            cp.wait_send()

    dma = pltpu.SemaphoreType.DMA((3,))
    return _Side([buf4], [jax.ShapeDtypeStruct(buf4.shape, buf4.dtype)], {0: 0}, [dma, dma], start, finish)


def _pair_sum(g, sib, core, name):
    _, r, cdim = g.shape
    half = r // 2

    def body(core_ref, g_ref, s_ref, o_ref):
        o_ref[...] = (g_ref[...].astype(F32) + s_ref[...].astype(F32)).astype(BF)

    return pl.pallas_call(
        body, name=name,
        grid_spec=pltpu.PrefetchScalarGridSpec(
            num_scalar_prefetch=1, grid=(N_CHIP,),
            in_specs=[pl.BlockSpec((None, half, cdim), lambda k, core: (k, core[0], 0)),
                      pl.BlockSpec((None, half, cdim), lambda k, core: (k, 0, 0))],
            out_specs=pl.BlockSpec((None, half, cdim), lambda k, core: (k, 0, 0))),
        out_shape=jax.ShapeDtypeStruct((N_CHIP, half, cdim), BF),
        compiler_params=_params(("arbitrary",)),
    )(core, g, sib)


def _chip_exchange_side(psums, relative=False):
    n = len(psums)

    def copies(ins, outs, sems):
        ssem, rsem = sems
        x, y, c, chips = _place()
        return [pltpu.make_async_remote_copy(
            src_ref=ins[w].at[REL_SLOT[j] if relative else 2 * px + py], dst_ref=outs[w].at[j],
            send_sem=ssem.at[w, j], recv_sem=rsem.at[w, j], device_id=(px, py, c), device_id_type=MESH)
            for w in range(n) for j, (px, py) in enumerate(chips)]

    def start(ins, outs, sems):
        for cp in copies(ins, outs, sems):
            cp.start()

    def finish(ins, outs, sems):
        for cp in copies(ins, outs, sems):
            cp.wait()

    dma = pltpu.SemaphoreType.DMA((n, 3))
    return _Side(list(psums), [jax.ShapeDtypeStruct((3,) + p.shape[1:], BF) for p in psums], {}, [dma, dma],
                 start, finish)


def _final_sum(g, sib, recv, sel, name):
    _, r, cdim = g.shape
    half = r // 2
    nt = 4
    th = half // nt

    def body(sel_ref, g_ref, s_ref, r_ref, o_ref):
        acc = g_ref[...].astype(F32) + s_ref[...].astype(F32)
        for j in range(3):
            acc = acc + r_ref[j].astype(F32)
        o_ref[...] = acc

    return pl.pallas_call(
        body, name=name,
        grid_spec=pltpu.PrefetchScalarGridSpec(
            num_scalar_prefetch=1, grid=(nt,),
            in_specs=[pl.BlockSpec((None, th, cdim), lambda i, sel: (sel[0], sel[1] * nt + i, 0)),
                      pl.BlockSpec((None, th, cdim), lambda i, sel: (sel[0], i, 0)),
                      pl.BlockSpec((3, th, cdim), lambda i, sel: (0, i, 0))],
            out_specs=pl.BlockSpec((th, cdim), lambda i, sel: (sel[1] * nt + i, 0))),
        out_shape=jax.ShapeDtypeStruct((r, cdim), F32),
        compiler_params=_params(("arbitrary",)),
    )(sel, g, sib, recv)


def _join_side(bufs):
    n = len(bufs)

    def copies(outs, sems, core_of):
        ssem, rsem = sems
        x, y, c, _ = _place()
        cps = []
        for w in range(n):
            half = bufs[w].shape[0] // 2
            rows = outs[w].at[pl.ds(core_of(c) * half, half)]
            cps.append(pltpu.make_async_remote_copy(
                src_ref=rows, dst_ref=rows, send_sem=ssem.at[w], recv_sem=rsem.at[w],
                device_id=(x, y, 1 - c), device_id_type=MESH))
        return cps

    def start(ins, outs, sems):
        for cp in copies(outs, sems, lambda c: c):
            cp.start()

    def finish(ins, outs, sems):
        for cp in copies(outs, sems, lambda c: 1 - c):
            cp.wait_recv()
        for cp in copies(outs, sems, lambda c: c):
            cp.wait_send()

    dma = pltpu.SemaphoreType.DMA((n,))
    return _Side(list(bufs), [jax.ShapeDtypeStruct(b.shape, F32) for b in bufs], {w: w for w in range(n)}, [dma, dma],
                 start, finish)


def _small_pair_sum(buf, sib, chip):
    rows, d = buf.shape

    def body(chip_ref, a_ref, b_ref, o_ref):
        o_ref[...] = a_ref[...] + b_ref[...]

    return pl.pallas_call(
        body, name="small_pair_sum", out_shape=jax.ShapeDtypeStruct((N_CHIP, rows, d), F32),
        grid_spec=pltpu.PrefetchScalarGridSpec(
            num_scalar_prefetch=1, grid=(1,),
            in_specs=[pl.BlockSpec((rows, d), lambda i, chip: (0, 0))] * 2,
            out_specs=pl.BlockSpec((None, rows, d), lambda i, chip: (chip[0], 0, 0))),
        compiler_params=_params(("arbitrary",)),
    )(chip, buf, sib)


def _small_all_reduce(buf, name):
    rows, d = buf.shape

    def body(in_ref, out_ref, sib, all4, ssem, rsem, psem, qsem):
        x, y, c, chips = _place()
        me = 2 * x + y
        to_sib = pltpu.make_async_remote_copy(src_ref=in_ref, dst_ref=sib, send_sem=ssem, recv_sem=rsem,
                                              device_id=(x, y, 1 - c), device_id_type=MESH)
        to_sib.start()
        to_sib.wait()
        all4[me] = in_ref[...] + sib[...]
        cps = [pltpu.make_async_remote_copy(src_ref=all4.at[me], dst_ref=all4.at[me], send_sem=psem.at[j],
                                            recv_sem=qsem.at[j], device_id=(px, py, c), device_id_type=MESH)
               for j, (px, py) in enumerate(chips)]
        for cp in cps:
            cp.start()
        for j, (px, py) in enumerate(chips):
            chip = 2 * px + py
            pltpu.make_async_remote_copy(src_ref=all4.at[chip], dst_ref=all4.at[chip], send_sem=psem.at[j],
                                         recv_sem=qsem.at[j], device_id=(px, py, c), device_id_type=MESH).wait_recv()
        for cp in cps:
            cp.wait_send()
        out_ref[...] = (all4[0] + all4[1]) + (all4[2] + all4[3])

    vm = pl.BlockSpec(memory_space=pltpu.VMEM)
    return pl.pallas_call(
        body, name=name, out_shape=jax.ShapeDtypeStruct((rows, d), F32),
        in_specs=[vm], out_specs=vm,
        scratch_shapes=[pltpu.VMEM((rows, d), F32), pltpu.VMEM((N_CHIP, rows, d), F32),
                        pltpu.SemaphoreType.DMA, pltpu.SemaphoreType.DMA,
                        pltpu.SemaphoreType.DMA((3,)), pltpu.SemaphoreType.DMA((3,))],
        compiler_params=_params(),
    )(buf)


def _pair_blocks(w):
    w4 = w.reshape(N_HEADS // 2, 2, HEAD, HEAD)
    eye = jnp.eye(2, dtype=w.dtype)
    return jnp.einsum("pirc,ij->pirjc", w4, eye).reshape(N_HEADS // 2, LANE, LANE)


def _unpair_blocks(w2):
    w5 = w2.reshape(N_HEADS // 2, 2, HEAD, 2, HEAD)
    return jnp.stack([w5[:, 0, :, 0, :], w5[:, 1, :, 1, :]], axis=1).reshape(N_HEADS, HEAD, HEAD)


def kernel(x, meta_tokens, ffn1_pre_g, ffn1_w_gate, ffn1_w_up, ffn1_w_down, ffn1_post_g, mix_pre_g, w_in, lru_conv_w, lru_conv_b, lru_w_a, lru_b_a, lru_w_x, lru_b_x, lru_lambda, sconv_w, lru_out_g, sconv_out_g, w_out, mix_post_g, ffn2_pre_g, ffn2_w_gate, ffn2_w_up, ffn2_w_down, ffn2_post_g, loss_target, m_meta_tokens, m_ffn1_pre_g, m_ffn1_w_gate, m_ffn1_w_up, m_ffn1_w_down, m_ffn1_post_g, m_mix_pre_g, m_w_in, m_lru_conv_w, m_lru_conv_b, m_lru_w_a, m_lru_b_a, m_lru_w_x, m_lru_b_x, m_lru_lambda, m_sconv_w, m_lru_out_g, m_sconv_out_g, m_w_out, m_mix_post_g, m_ffn2_pre_g, m_ffn2_w_gate, m_ffn2_w_up, m_ffn2_w_down, m_ffn2_post_g, v_meta_tokens, v_ffn1_pre_g, v_ffn1_w_gate, v_ffn1_w_up, v_ffn1_w_down, v_ffn1_post_g, v_mix_pre_g, v_w_in, v_lru_conv_w, v_lru_conv_b, v_lru_w_a, v_lru_b_a, v_lru_w_x, v_lru_b_x, v_lru_lambda, v_sconv_w, v_lru_out_g, v_sconv_out_g, v_w_out, v_mix_post_g, v_ffn2_pre_g, v_ffn2_w_gate, v_ffn2_w_up, v_ffn2_w_down, v_ffn2_post_g):
    seq, d = x.shape[1], x.shape[2]
    t_real = N_META + seq
    tp = _round_up(t_real, ROW_ALIGN)
    f4 = ffn1_w_gate.shape[2]
    f4p = _round_up(f4, LANE)
    dl = lru_conv_b.shape[1]
    cin = w_in.shape[2]
    xi, yi, ci = lax.axis_index("x"), lax.axis_index("y"), lax.axis_index("c")
    chip = 2 * xi + yi
    zero = jnp.zeros((), jnp.int32)

    transposed = ("ffn1_w_gate", "ffn1_w_up", "ffn2_w_gate", "ffn2_w_up")

    def view(k, a):
        return a[0].T if k in transposed else a[0]

    def unview(k, a):
        return (a.T if k in transposed else a)[None]

    big = {
        "ffn1_w_gate": (view("ffn1_w_gate", ffn1_w_gate), f4p, d), "ffn1_w_up": (view("ffn1_w_up", ffn1_w_up), f4p, d),
        "ffn1_w_down": (ffn1_w_down[0], f4p, d), "w_in": (w_in[0], d, cin), "w_out": (w_out[0], w_out.shape[1], d),
        "ffn2_w_gate": (view("ffn2_w_gate", ffn2_w_gate), f4p, d), "ffn2_w_up": (view("ffn2_w_up", ffn2_w_up), f4p, d),
        "ffn2_w_down": (ffn2_w_down[0], f4p, d),
    }
    names = list(big)
    chip1 = jnp.reshape(chip, (1,)).astype(jnp.int32)
    relative = {k: k.startswith("ffn") for k in names}
    slot0 = jnp.zeros((1,), jnp.int32)
    shard = {k: _cast_pad(big[k][0], big[k][1], big[k][2], slot0 if relative[k] else chip1, "cast_" + k) for k in names}
    full = {}

    def gather(*keys):
        return _merge_sides([_gather_side([shard[k]], relative[k], two_path=True) for k in keys])

    gm = jnp.kron(jnp.eye(2, dtype=F32), jnp.full((HEAD, HEAD), 1.0 / HEAD, F32)).astype(BF)
    wa2 = _pair_blocks(lru_w_a[0])
    wx2 = _pair_blocks(lru_w_x[0])

    dlq = dl // N_CHIP
    dq = d // N_CHIP
    R_GAIN, R_LOSS, R_META, R_LRU, R_SC, R_WA = 0, 6, 8, 24, 40, 48
    n_wrows = (N_HEADS // 2) * LANE * LANE // d
    R_WX = R_WA + n_wrows
    R_END = R_WX + n_wrows

    def pack_top(gains, meta, loss=None):
        lossrow = jnp.zeros((2, d), F32)
        if loss is not None:
            lossrow = lossrow.at[0, 0].set(loss)
        return jnp.concatenate([jnp.concatenate(gains, axis=0), lossrow, meta], axis=0)

    def pack_rest(lru16, sc8, wa_, wx_):
        return jnp.concatenate([jnp.concatenate([lru16, jnp.zeros((16, d - dl), F32)], axis=1),
                                jnp.concatenate([sc8, jnp.zeros((8, d - dl), F32)], axis=1),
                                wa_.reshape(n_wrows, d), wx_.reshape(n_wrows, d)], axis=0)

    def pack(gains, meta, lru16, sc8, wa_, wx_):
        return jnp.concatenate([pack_top(gains, meta), pack_rest(lru16, sc8, wa_, wx_)], axis=0)

    def place_cols(blk, width, total):
        return lax.dynamic_update_slice(jnp.zeros((blk.shape[0], total), F32), blk, (zero, chip * width))

    def pack_params(meta_, g1pre, g1post, gmpre, gmpost, g2pre, g2post, cw, cbias, wa_, ba_, wx_, bx_, lam_, sw, lgo, sgo):
        lru16 = jnp.concatenate([place_cols(cw[0], dlq, dl), cbias, ba_, bx_, lam_, lgo, jnp.zeros((7, dl), F32)], axis=0)
        sc8 = jnp.concatenate([place_cols(sw[0], dlq, dl), sgo, jnp.zeros((4, dl), F32)], axis=0)
        return pack([g1pre, g1post, gmpre, gmpost, g2pre, g2post], place_cols(meta_, dq, d), lru16, sc8,
                    _pair_blocks(wa_[0]), _pair_blocks(wx_[0]))

    p_w = pack_params(meta_tokens, ffn1_pre_g, ffn1_post_g, mix_pre_g, mix_post_g, ffn2_pre_g, ffn2_post_g, lru_conv_w,
                      lru_conv_b, lru_w_a, lru_b_a, lru_w_x, lru_b_x, lru_lambda, sconv_w, lru_out_g, sconv_out_g)
    p_m = pack_params(m_meta_tokens, m_ffn1_pre_g, m_ffn1_post_g, m_mix_pre_g, m_mix_post_g, m_ffn2_pre_g, m_ffn2_post_g,
                      m_lru_conv_w, m_lru_conv_b, m_lru_w_a, m_lru_b_a, m_lru_w_x, m_lru_b_x, m_lru_lambda, m_sconv_w,
                      m_lru_out_g, m_sconv_out_g)
    p_v = pack_params(v_meta_tokens, v_ffn1_pre_g, v_ffn1_post_g, v_mix_pre_g, v_mix_post_g, v_ffn2_pre_g, v_ffn2_post_g,
                      v_lru_conv_w, v_lru_conv_b, v_lru_w_a, v_lru_b_a, v_lru_w_x, v_lru_b_x, v_lru_lambda, v_sconv_w,
                      v_lru_out_g, v_sconv_out_g)

    gathered = _small_all_reduce(jnp.where(ci == 0, p_w, 0.0)[R_META:R_WA], "small_weight_gather")
    meta_full = gathered[0:N_META]
    w4_full = gathered[R_LRU - R_META:R_LRU - R_META + 4, 0:dl]
    w3_full = gathered[R_SC - R_META:R_SC - R_META + 3, 0:dl]
    w4p = jnp.concatenate([w4_full, jnp.zeros((4, dl), F32)], axis=0)
    w3p = jnp.concatenate([w3_full, jnp.zeros((5, dl), F32)], axis=0)

    h0 = jnp.concatenate([meta_full, x[0], jnp.zeros((tp - t_real, d), F32)], axis=0)
    tgt = jnp.concatenate([jnp.zeros((N_META, d), F32), loss_target[0], jnp.zeros((tp - t_real, d), F32)], axis=0)

    n1 = _norm0(h0, ffn1_pre_g)
    (a1, b1, s1), (full["ffn1_w_gate"], full["ffn1_w_up"]), got = _ffn_up_head(
        n1, shard["ffn1_w_gate"], shard["ffn1_w_up"], "ffn1_up",
        _gather_side([shard["ffn1_w_down"]], relative=True, two_path=True))
    full["ffn1_w_down"] = got[0]
    f1, got = _wide_matmul([(s1, full["ffn1_w_down"])], "ffn1_down", WIDE_TN, gather("w_in"))
    full["w_in"] = got[0]
    h1, u = _post_fwd(f1, h0, ffn1_post_g, mix_pre_g, 0.5, "ffn1_post")
    z, got = _col_matmul(u, full["w_in"], "in_proj", False, F32, gather("ffn2_w_gate"))
    full["ffn2_w_gate"] = got[0]
    (m_lru, hs), got = _lru_fwd(z, w4p, lru_conv_b, wa2.astype(BF), lru_b_a, wx2.astype(BF), lru_b_x, lru_lambda,
                                lru_out_g, gm, gather("ffn2_w_up"))
    full["ffn2_w_up"] = got[0]
    m_sc, got = _sc_fwd(z, w3p, sconv_out_g, gm, dl, gather("w_out"))
    full["w_out"] = got[0]
    mixed = jnp.concatenate([m_lru, m_sc], axis=1)
    p, _ = _row_matmul([(mixed, full["w_out"])], "out_proj", False, d)
    h2, n2 = _post_fwd(p, h1, mix_post_g, ffn2_pre_g, 1.0, "mix_post")
    (a2, b2, s2), got = _ffn_up(n2, full["ffn2_w_gate"], full["ffn2_w_up"], "ffn2_up", gather("ffn2_w_down"))
    full["ffn2_w_down"] = got[0]
    f2, _ = _wide_matmul([(s2, full["ffn2_w_down"])], "ffn2_down", WIDE_TN)
    dh3, df2, dg_ffn2_post, loss_part = _loss_bwd(f2, h2, tgt, ffn2_post_g, t_real)

    core = jnp.reshape(ci, (1,)).astype(jnp.int32)
    sel_of = {False: jnp.stack([chip, ci]).astype(jnp.int32), True: jnp.stack([0 * chip, ci]).astype(jnp.int32)}
    red = {}

    def pair_side(k):
        return _pair_exchange_side([red[k][0]])

    def chip_side(k):
        return _chip_exchange_side([_pair_sum(red[k][0], red[k][1], core, "pair_sum_" + k)], relative[k])

    def final_sum(k):
        return _final_sum(*red[k], sel_of[relative[k]], "final_sum_" + k)

    (da2, db2), _ = _ffn_bwd_act(df2, full["ffn2_w_down"], a2, b2, "ffn2_bwd_act")
    g, _ = _wgrad_call(s2, df2, "ffn2_down_wgrad", x_width=f4p, tile_y=WGRAD_TILE_Y)
    red["ffn2_w_down"] = [g, None, None]
    g, got = _wgrad_call(da2, n2, "ffn2_gate_wgrad", x_width=f4p, tile_y=WGRAD_TILE_Y, side=pair_side("ffn2_w_down"))
    red["ffn2_w_down"][1] = got[0]
    red["ffn2_w_gate"] = [g, None, None]
    g, got = _wgrad_call(db2, n2, "ffn2_up_wgrad", x_width=f4p, tile_y=WGRAD_TILE_Y,
                         side=_merge_sides([pair_side("ffn2_w_gate"), chip_side("ffn2_w_down")]))
    red["ffn2_w_gate"][1], red["ffn2_w_down"][2] = got
    red["ffn2_w_up"] = [g, None, None]
    dn2, got = _wide_matmul([(da2, full["ffn2_w_gate"]), (db2, full["ffn2_w_up"])], "ffn2_bwd_up", WIDE_TN // 2,
                            _merge_sides([pair_side("ffn2_w_up"), chip_side("ffn2_w_gate")]))
    red["ffn2_w_up"][1], red["ffn2_w_gate"][2] = got
    dh2, dp, dg_ffn2_pre, dg_mix_post = _pre_bwd(dn2, h2, dh3, ffn2_pre_g, "ffn2_pre_bwd", (p, mix_post_g, 1.0))
    dmixed, _ = _col_matmul(dp, full["w_out"], "out_proj_bwd", True, F32)
    g, _ = _wgrad_call(mixed, dp, "w_out_wgrad", x_width=mixed.shape[1] // N_CHIP, tile_y=WGRAD_TILE_Y)
    red["w_out"] = [g, None, None]
    (dzy, dzx, lru_small, dwa2, dwx2), got = _lru_bwd(
        z, hs, dmixed, w4p, lru_conv_b, wa2.astype(BF), lru_b_a, wx2.astype(BF), lru_b_x, lru_lambda, lru_out_g, gm,
        _merge_sides([pair_side("w_out"), chip_side("ffn2_w_up")]))
    red["w_out"][1], red["ffn2_w_up"][2] = got
    (dzb, dzc, dzv, sc_small), got = _sc_bwd(z, dmixed, w3p, sconv_out_g, gm, dl, chip_side("w_out"))
    red["w_out"][2] = got[0]
    dz = jnp.concatenate([dzy, dzx, dzb, dzc, dzv], axis=1)
    p_rest = pack_rest(lru_small, sc_small, dwa2, dwx2)
    g, got = _wgrad_call(u, dz, "w_in_wgrad", y_width=cin, tile_x=WGRAD_TILE_X, side=_sibling_copy_side(p_rest))
    p_rest4 = _small_pair_sum(p_rest, got[0], chip1)
    red["w_in"] = [g, None, None]
    du, got = _row_matmul([(dz, full["w_in"])], "in_proj_bwd", True, d,
                          _merge_sides([pair_side("w_in"), _slot_exchange_side(p_rest4)]))
    red["w_in"][1], p_rest4 = got
    dh1, df1, dg_mix_pre, dg_ffn1_post = _pre_bwd(du, h1, dh2, mix_pre_g, "mix_pre_bwd", (f1, ffn1_post_g, 0.5))
    (da1, db1), got = _ffn_bwd_act(df1, full["ffn1_w_down"], a1, b1, "ffn1_bwd_act", chip_side("w_in"))
    red["w_in"][2] = got[0]
    early = ["ffn2_w_down", "ffn2_w_gate", "ffn2_w_up", "w_out", "w_in"]
    late = ["ffn1_w_down", "ffn1_w_gate", "ffn1_w_up"]
    g, got = _wgrad_call(s1, df1, "ffn1_down_wgrad", x_width=f4p, tile_y=WGRAD_TILE_Y,
                         side=_join_side([final_sum(k) for k in early]))
    gfull = dict(zip(early, got))
    red["ffn1_w_down"] = [g, None, None]
    g, got = _wgrad_call(da1, n1, "ffn1_gate_wgrad", x_width=f4p, tile_y=WGRAD_TILE_Y, side=pair_side("ffn1_w_down"))
    red["ffn1_w_down"][1] = got[0]
    red["ffn1_w_gate"] = [g, None, None]
    g, got = _wgrad_call(db1, n1, "ffn1_up_wgrad", x_width=f4p, tile_y=WGRAD_TILE_Y,
                         side=_merge_sides([pair_side("ffn1_w_gate"), chip_side("ffn1_w_down")]))
    red["ffn1_w_gate"][1], red["ffn1_w_down"][2] = got
    red["ffn1_w_up"] = [g, None, None]
    red["ffn1_w_up"][1] = _run_side(pair_side("ffn1_w_up"), "pair_exchange_ffn1_w_up")[0]
    dn1, got = _wide_matmul([(da1, full["ffn1_w_gate"]), (db1, full["ffn1_w_up"])], "ffn1_bwd_up", WIDE_TN // 2,
                            _merge_sides([chip_side("ffn1_w_gate"), chip_side("ffn1_w_up")]))
    red["ffn1_w_gate"][2], red["ffn1_w_up"][2] = got
    (dh0, dg_ffn1_pre), got = _pre_bwd(dn1, h0, dh1, ffn1_pre_g, "ffn1_pre_bwd",
                                       side=_join_side([final_sum(k) for k in late]))
    gfull.update(zip(late, got))

    grad_x = dh0[N_META:t_real][None]

    w_big = {"ffn1_w_gate": ffn1_w_gate, "ffn1_w_up": ffn1_w_up, "ffn1_w_down": ffn1_w_down, "w_in": w_in, "w_out": w_out,
             "ffn2_w_gate": ffn2_w_gate, "ffn2_w_up": ffn2_w_up, "ffn2_w_down": ffn2_w_down}
    m_big = {"ffn1_w_gate": m_ffn1_w_gate, "ffn1_w_up": m_ffn1_w_up, "ffn1_w_down": m_ffn1_w_down, "w_in": m_w_in,
             "w_out": m_w_out, "ffn2_w_gate": m_ffn2_w_gate, "ffn2_w_up": m_ffn2_w_up, "ffn2_w_down": m_ffn2_w_down}
    v_big = {"ffn1_w_gate": v_ffn1_w_gate, "ffn1_w_up": v_ffn1_w_up, "ffn1_w_down": v_ffn1_w_down, "w_in": v_w_in,
             "w_out": v_w_out, "ffn2_w_gate": v_ffn2_w_gate, "ffn2_w_up": v_ffn2_w_up, "ffn2_w_down": v_ffn2_w_down}
    b_grad, b_delta, b_newm, b_newv = {}, {}, {}, {}

    def big_adamw(k, side=None):
        wv, mv, vv = view(k, w_big[k]), view(k, m_big[k]), view(k, v_big[k])
        wide_rows = wv.shape[0] % 64 == 0
        (g_, d_, m_, v_), got = _adamw(wv, gfull[k], mv, vv, "adamw_" + k, 8 if wide_rows else 4, 1 if wide_rows else 2,
                                       side)
        b_grad[k], b_delta[k], b_newm[k], b_newv[k] = unview(k, g_), unview(k, d_), unview(k, m_), unview(k, v_)
        return got

    p_top = _small_all_reduce(
        pack_top([dg_ffn1_pre, dg_ffn1_post, dg_mix_pre, dg_mix_post, dg_ffn2_pre, dg_ffn2_post], dh0[0:N_META],
                 loss=loss_part[0, 0]), "small_grad_all_reduce")
    p_g, p_delta, p_newm, p_newv = _adamw_small(p_w, p_top, p_rest4, p_m, p_v)
    loss = p_g[R_LOSS, 0]
    for k in names:
        big_adamw(k)

    def unpack(buf):
        out = {}
        for i, k in enumerate(["ffn1_pre_g", "ffn1_post_g", "mix_pre_g", "mix_post_g", "ffn2_pre_g", "ffn2_post_g"]):
            out[k] = buf[R_GAIN + i:R_GAIN + i + 1]
        out["meta_tokens"] = lax.dynamic_slice(buf[R_META:R_META + N_META], (zero, chip * dq), (N_META, dq))
        lru = buf[R_LRU:R_LRU + 16, 0:dl]
        out["lru_conv_w"] = lax.dynamic_slice(lru[0:4], (zero, chip * dlq), (4, dlq))[None]
        out["lru_conv_b"] = lru[4:5]
        out["lru_b_a"] = lru[5:6]
        out["lru_b_x"] = lru[6:7]
        out["lru_lambda"] = lru[7:8]
        out["lru_out_g"] = lru[8:9]
        sc = buf[R_SC:R_SC + 8, 0:dl]
        out["sconv_w"] = lax.dynamic_slice(sc[0:3], (zero, chip * dlq), (3, dlq))[None]
        out["sconv_out_g"] = sc[3:4]
        out["lru_w_a"] = _unpair_blocks(buf[R_WA:R_WX].reshape(N_HEADS // 2, LANE, LANE))[None]
        out["lru_w_x"] = _unpair_blocks(buf[R_WX:R_END].reshape(N_HEADS // 2, LANE, LANE))[None]
        return out

    s_grad, s_delta, s_newm, s_newv = unpack(p_g), unpack(p_delta), unpack(p_newm), unpack(p_newv)

    order = ["meta_tokens", "ffn1_pre_g", "ffn1_w_gate", "ffn1_w_up", "ffn1_w_down", "ffn1_post_g", "mix_pre_g", "w_in",
             "lru_conv_w", "lru_conv_b", "lru_w_a", "lru_b_a", "lru_w_x", "lru_b_x", "lru_lambda", "sconv_w", "lru_out_g",
             "sconv_out_g", "w_out", "mix_post_g", "ffn2_pre_g", "ffn2_w_gate", "ffn2_w_up", "ffn2_w_down", "ffn2_post_g"]

    def pick(small, bigd):
        return [bigd[k] if k in bigd else small[k] for k in order]

    return (loss, grad_x, *pick(s_grad, b_grad), *pick(s_delta, b_delta), *pick(s_newm, b_newm), *pick(s_newv, b_newv))
```

```python
import functools
import math

import jax
import jax.numpy as jnp
from jax import lax
from jax.experimental import pallas as pl
from jax.experimental.pallas import tpu as pltpu

F32 = jnp.float32
BF = jnp.bfloat16
MESH = pl.DeviceIdType.MESH

EPS = 1e-6
N_META = 16
N_HEADS = 16
HEAD = 64
LRU_C = 8.0
LANE = 128
MXU_COLS = 256
N_CHIP = 4
ROW_ALIGN = 384
MM_TILES = 8
MM_TILES_BIG = 4
EW_TILES = 12
MIX_CHUNKS = 24
WGRAD_TILE_X = 256
WGRAD_TILE_Y = 512
WIDE_TN = 512
VMEM_LIMIT = 56 << 20

ADAM_LR = 0.001
ADAM_B1 = 0.9
ADAM_B2 = 0.999
ADAM_EPS = 1e-08
ADAM_WD = 0.01
ADAM_STEP = 10


def _round_up(a, b):
    return (a + b - 1) // b * b


def _params(sem=None):
    if sem is None:
        return pltpu.CompilerParams(vmem_limit_bytes=VMEM_LIMIT)
    return pltpu.CompilerParams(dimension_semantics=sem, vmem_limit_bytes=VMEM_LIMIT)


def _sigmoid(x):
    return 0.5 * jnp.tanh(0.5 * x) + 0.5


def _dot(a, b):
    return jnp.dot(a, b, preferred_element_type=F32)


def _dot_nt(a, b):
    return lax.dot_general(a, b, (((1,), (1,)), ((), ())), preferred_element_type=F32)


def _dot_tn(a, b):
    return lax.dot_general(a, b, (((0,), (0,)), ((), ())), preferred_element_type=F32)


def _rms(x, g):
    r = lax.rsqrt(jnp.mean(x * x, axis=-1, keepdims=True) + EPS)
    return x * r * g


def _rms_bwd(x, g, dy):
    r = lax.rsqrt(jnp.mean(x * x, axis=-1, keepdims=True) + EPS)
    xh = x * r
    q = dy * g
    dx = r * (q - xh * jnp.mean(q * xh, axis=-1, keepdims=True))
    return dx, dy * xh


class _Side:
    def __init__(self, ins, outs, alias, sems, start, finish):
        self.ins, self.outs, self.alias, self.sems, self.start, self.finish = ins, outs, alias, sems, start, finish


def _merge_sides(sides):
    sides = [s for s in sides if s is not None]
    if len(sides) <= 1:
        return sides[0] if sides else None
    ins, outs, sems, alias, spans = [], [], [], {}, []
    for s in sides:
        for i, o in s.alias.items():
            alias[len(ins) + i] = len(outs) + o
        spans.append((len(ins), len(ins) + len(s.ins), len(outs), len(outs) + len(s.outs), len(sems),
                      len(sems) + len(s.sems)))
        ins += list(s.ins)
        outs += list(s.outs)
        sems += list(s.sems)

    def run(which):
        def go(in_refs, out_refs, sem_refs):
            for s, (a, b, c, d, e, f) in zip(sides, spans):
                getattr(s, which)(in_refs[a:b], out_refs[c:d], sem_refs[e:f])
        return go

    return _Side(ins, outs, alias, sems, run("start"), run("finish"))


def _grid_call(body, name, grid, in_specs, out_specs, out_shape, args, side=None, scratch=()):
    sem = ("arbitrary",) * len(grid)
    if side is None:
        res = pl.pallas_call(body, name=name, grid=grid, in_specs=in_specs, out_specs=out_specs, out_shape=out_shape,
                             scratch_shapes=list(scratch), compiler_params=_params(sem))(*args)
        return res, []
    nin, nout, sin, sout = len(in_specs), len(out_specs), len(side.ins), len(side.outs)
    nscr = len(scratch)
    staged = hasattr(side, "middle") and math.prod(grid) >= 4
    lin, mid = (math.prod(grid) * 5) // 8, []
    for extent in reversed(grid):
        mid.insert(0, lin % extent)
        lin //= extent

    def full(*refs):
        base_in, side_in = refs[:nin], refs[nin:nin + sin]
        base_out = refs[nin + sin:nin + sin + nout]
        side_out = refs[nin + sin + nout:nin + sin + nout + sout]
        base_scr = refs[nin + sin + nout + sout:nin + sin + nout + sout + nscr]
        sems = refs[nin + sin + nout + sout + nscr:]
        first = pl.program_id(0) == 0
        last = pl.program_id(0) == grid[0] - 1
        for ax in range(1, len(grid)):
            first = first & (pl.program_id(ax) == 0)
            last = last & (pl.program_id(ax) == grid[ax] - 1)

        @pl.when(first)
        def _():
            side.start(side_in, side_out, sems)

        if staged:
            at_mid = pl.program_id(0) == mid[0]
            for ax in range(1, len(grid)):
                at_mid = at_mid & (pl.program_id(ax) == mid[ax])

            @pl.when(at_mid)
            def _():
                side.middle(side_in, side_out, sems)

        body(*base_in, *base_out, *base_scr)

        @pl.when(last)
        def _():
            (side.rest if staged else side.finish)(side_in, side_out, sems)

    any_spec = pl.BlockSpec(memory_space=pl.ANY)
    res = pl.pallas_call(
        full, name=name, grid=grid, in_specs=list(in_specs) + [any_spec] * sin,
        out_specs=list(out_specs) + [any_spec] * sout, out_shape=list(out_shape) + list(side.outs),
        scratch_shapes=list(scratch) + list(side.sems),
        input_output_aliases={nin + i: nout + o for i, o in side.alias.items()},
        compiler_params=_params(sem))(*args, *side.ins)
    return res[:nout], res[nout:]


def _ffn_up(n, wg, wu, name, side=None, tiles=MM_TILES):
    tp, d = n.shape
    fp = wg.shape[1]
    tm = tp // tiles

    def body(n_ref, wg_ref, wu_ref, a_ref, b_ref, s_ref):
        nn = n_ref[...]
        for c0 in range(0, fp, MXU_COLS):
            cs = slice(c0, min(c0 + MXU_COLS, fp))
            a = _dot_nt(nn, wg_ref[cs, :])
            b = _dot_nt(nn, wu_ref[cs, :])
            a_ref[:, cs] = a.astype(BF)
            b_ref[:, cs] = b.astype(BF)
            s_ref[:, cs] = (a * _sigmoid(a) * b).astype(BF)

    out = jax.ShapeDtypeStruct((tp, N_CHIP * fp), BF)
    wspec = pl.BlockSpec((None, fp, d), lambda k, i: (k, 0, 0))
    ospec = pl.BlockSpec((tm, fp), lambda k, i: (i, k))
    return _grid_call(body, name, (N_CHIP, tiles), [pl.BlockSpec((tm, d), lambda k, i: (i, 0)), wspec, wspec],
                      [ospec, ospec, ospec], [out, out, out], (n, wg, wu), side)


def _ffn_up_head(n, wg, wu, name, side):
    tp, d = n.shape
    fp = wg.shape[1]
    tiles = MM_TILES
    tm = tp // tiles
    gat = _gather_side([wg, wu], relative=True, two_path=True)
    sin, sout, ngs = len(side.ins), len(side.outs), len(gat.sems)
    order = (0,) + REL_SLOT
    staged = hasattr(side, "middle")

    def body(*refs):
        n_ref = refs[0]
        si = refs[3:3 + sin]
        a_ref, b_ref, s_ref = refs[3 + sin:6 + sin]
        go = refs[6 + sin:8 + sin]
        so = refs[8 + sin:8 + sin + sout]
        wbg, wbu, wsem = refs[8 + sin + sout:11 + sin + sout]
        gsems = refs[11 + sin + sout:11 + sin + sout + ngs]
        ssems = refs[11 + sin + sout + ngs:]
        k, i = pl.program_id(0), pl.program_id(1)
        cur = k % 2

        def to_vmem(slot, buf):
            return [pltpu.make_async_copy(go[0].at[slot], wbg.at[buf], wsem.at[buf, 0]),
                    pltpu.make_async_copy(go[1].at[slot], wbu.at[buf], wsem.at[buf, 1])]

        @pl.when((k == 0) & (i == 0))
        def _():
            gat.send(go, gsems)
            if not staged:
                side.start(si, so, ssems)
            for cp in to_vmem(0, 0):
                cp.start()
            for cp in to_vmem(0, 0):
                cp.wait()

        for j in range(3):
            @pl.when((k == j) & (i == tiles // 2))
            def _():
                gat.arrived(j, go, gsems)
                if staged and j == 1:
                    side.start(si, so, ssems)

            @pl.when((k == j) & (i == tiles - 2))
            def _():
                gat.forwarded(j, go, gsems)
                for cp in to_vmem(order[j + 1], (j + 1) % 2):
                    cp.start()

            @pl.when((k == j + 1) & (i == 0))
            def _():
                for cp in to_vmem(order[j + 1], (j + 1) % 2):
                    cp.wait()

        nn = n_ref[...]
        for c0 in range(0, fp, MXU_COLS):
            cs = pl.ds(c0, min(MXU_COLS, fp - c0))
            a = _dot_nt(nn, wbg[cur, cs, :])
            b = _dot_nt(nn, wbu[cur, cs, :])
            a_ref[:, cs] = a.astype(BF)
            b_ref[:, cs] = b.astype(BF)
            s_ref[:, cs] = (a * _sigmoid(a) * b).astype(BF)

        if staged:
            @pl.when((k == N_CHIP - 1) & (i == tiles // 4))
            def _():
                side.middle(si, so, ssems)

        @pl.when((k == N_CHIP - 1) & (i == tiles - 1))
        def _():
            gat.drain(go, gsems)
            if staged:
                side.rest(si, so, ssems)
            else:
                side.finish(si, so, ssems)

    out = jax.ShapeDtypeStruct((tp, N_CHIP * fp), BF)
    any_spec = pl.BlockSpec(memory_space=pl.ANY)
    slot_of = lambda k: (k % 2) * 2 + k // 2
    ospec = pl.BlockSpec((tm, fp), lambda k, i: (i, slot_of(k)))
    wbuf = pltpu.VMEM((2, fp, d), BF)
    res = pl.pallas_call(
        body, name=name, grid=(N_CHIP, tiles),
        in_specs=[pl.BlockSpec((tm, d), lambda k, i: (i, 0))] + [any_spec] * (2 + sin),
        out_specs=[ospec, ospec, ospec] + [any_spec] * (2 + sout),
        out_shape=[out, out, out] + list(gat.outs) + list(side.outs),
        scratch_shapes=[wbuf, wbuf, pltpu.SemaphoreType.DMA((2, 2))] + list(gat.sems) + list(side.sems),
        input_output_aliases={1: 3, 2: 4, **{3 + a: 5 + b for a, b in side.alias.items()}},
        compiler_params=_params(("arbitrary", "arbitrary")))(n, wg, wu, *side.ins)
    return res[:3], res[3:5], res[5:]


def _ffn_bwd_act(df, wd, a, b, name, side=None, tiles=MM_TILES):
    tp, d = df.shape
    fp = wd.shape[1]
    tm = tp // tiles

    def body(df_ref, wd_ref, a_ref, b_ref, da_ref, db_ref):
        dfv = df_ref[...]
        for c0 in range(0, fp, MXU_COLS):
            cs = slice(c0, min(c0 + MXU_COLS, fp))
            ds = _dot_nt(dfv, wd_ref[cs, :])
            av = a_ref[:, cs].astype(F32)
            bv = b_ref[:, cs].astype(F32)
            sg = _sigmoid(av)
            da_ref[:, cs] = (ds * bv * sg * (1.0 + av * (1.0 - sg))).astype(BF)
            db_ref[:, cs] = (ds * av * sg).astype(BF)

    out = jax.ShapeDtypeStruct((tp, N_CHIP * fp), BF)
    aspec = pl.BlockSpec((tm, fp), lambda k, i: (i, k))
    return _grid_call(
        body, name, (N_CHIP, tiles),
        [pl.BlockSpec((tm, d), lambda k, i: (i, 0)), pl.BlockSpec((None, fp, d), lambda k, i: (k, 0, 0)), aspec, aspec],
        [aspec, aspec], [out, out], (df, wd, a, b), side)


def _col_matmul(lhs, w, name, trans_b, out_dtype, side=None, tiles=MM_TILES_BIG):
    tp, kd = lhs.shape
    nk = w.shape[0]
    nc = w.shape[1] if trans_b else w.shape[2]
    tm = tp // tiles

    def body(l_ref, w_ref, o_ref):
        if trans_b:
            o_ref[...] = _dot_nt(l_ref[...], w_ref[...]).astype(out_dtype)
        else:
            o_ref[...] = _dot(l_ref[...], w_ref[...]).astype(out_dtype)

    res, extra = _grid_call(
        body, name, (nk, tiles),
        [pl.BlockSpec((tm, kd), lambda k, i: (i, 0)),
         pl.BlockSpec((None,) + tuple(w.shape[1:]), lambda k, i: (k, 0, 0), pipeline_mode=pl.Buffered(1))],
        [pl.BlockSpec((tm, nc), lambda k, i: (i, k))], [jax.ShapeDtypeStruct((tp, nk * nc), out_dtype)], (lhs, w), side)
    return res[0], extra


def _row_matmul(pairs, name, trans_b, d_out, side=None, tiles=MM_TILES_BIG):
    l0 = pairs[0][0]
    tp = l0.shape[1] if l0.ndim == 3 else l0.shape[0]
    nk = pairs[0][1].shape[0]
    tm = tp // tiles
    npair = len(pairs)

    def body(*refs):
        o_ref = refs[2 * npair]
        k = pl.program_id(1)
        part = None
        for q in range(npair):
            l = refs[2 * q][...]
            w = refs[2 * q + 1][...]
            t = _dot_nt(l, w) if trans_b else _dot(l, w)
            part = t if part is None else part + t

        @pl.when(k == 0)
        def _():
            o_ref[...] = part

        @pl.when(k > 0)
        def _():
            o_ref[...] += part

    in_specs, args = [], []
    for lhs, w in pairs:
        if lhs.ndim == 3:
            in_specs.append(pl.BlockSpec((None, tm, lhs.shape[2]), lambda i, k: (k, i, 0)))
        else:
            in_specs.append(pl.BlockSpec((tm, lhs.shape[1] // nk), lambda i, k: (i, k)))
        in_specs.append(pl.BlockSpec((None,) + tuple(w.shape[1:]), lambda i, k: (k, 0, 0)))
        args += [lhs, w]
    res, extra = _grid_call(body, name, (tiles, nk), in_specs, [pl.BlockSpec((tm, d_out), lambda i, k: (i, 0))],
                            [jax.ShapeDtypeStruct((tp, d_out), F32)], args, side)
    return res[0], extra


def _wide_matmul(pairs, name, tn, side=None):
    tp = pairs[0][0].shape[0]
    d_out = pairs[0][1].shape[2]
    tm = tp // MM_TILES
    npair = len(pairs)

    def body(*refs):
        acc = None
        for q in range(npair):
            t = _dot(refs[2 * q][...], refs[2 * q + 1][...])
            acc = t if acc is None else acc + t
        refs[2 * npair][...] = acc

    in_specs, args = [], []
    for lhs, w in pairs:
        kdim = lhs.shape[1]
        in_specs += [pl.BlockSpec((tm, kdim), lambda n, i: (i, 0)), pl.BlockSpec((kdim, tn), lambda n, i: (0, n))]
        args += [lhs, w.reshape(kdim, d_out)]
    res, extra = _grid_call(body, name, (d_out // tn, MM_TILES), in_specs, [pl.BlockSpec((tm, tn), lambda n, i: (i, n))],
                            [jax.ShapeDtypeStruct((tp, d_out), F32)], args, side)
    return res[0], extra


def _wgrad_call(x, y, name, x_width=None, y_width=None, tile_x=None, tile_y=None, side=None):
    tp = x.shape[1] if x.ndim == 3 else x.shape[0]

    def spec(a, width, tile):
        cols = a.shape[2] if a.ndim == 3 else (a.shape[1] if width is None else width)
        tc = cols if tile is None else tile
        per = cols // tc
        if a.ndim == 3:
            return pl.BlockSpec((None, tp, tc), lambda k, t: (k, 0, t if tile else 0)), cols, per
        if width is None:
            return pl.BlockSpec((tp, tc), lambda k, t: (0, t if tile else 0)), cols, per
        return pl.BlockSpec((tp, tc), lambda k, t: (0, k * per + (t if tile else 0))), cols, per

    xs, p, nx = spec(x, x_width, tile_x)
    ys, q, ny = spec(y, y_width, tile_y)
    nt = nx * ny
    if tile_x:
        ospec = pl.BlockSpec((None, tile_x, q), lambda k, t: (k, t, 0))
    else:
        ospec = pl.BlockSpec((None, p, tile_y), lambda k, t: (k, 0, t))

    def body(x_ref, y_ref, o_ref):
        o_ref[...] = _dot_tn(x_ref[...], y_ref[...]).astype(BF)

    res, extra = _grid_call(body, name, (N_CHIP, nt), [xs, ys], [ospec], [jax.ShapeDtypeStruct((N_CHIP, p, q), BF)],
                            (x, y), side)
    return res[0], extra


def _row_call(body, name, tp, d, row_ins, vec_ins, row_out_dtypes, n_acc, side=None):
    te = tp // EW_TILES
    rspec = pl.BlockSpec((te, d), lambda i: (i, 0))
    vspec = pl.BlockSpec((1, d), lambda i: (0, 0))
    res, extra = _grid_call(
        body, name, (EW_TILES,), [rspec] * len(row_ins) + [vspec] * len(vec_ins),
        [rspec] * len(row_out_dtypes) + [vspec] * n_acc,
        [jax.ShapeDtypeStruct((tp, d), dt) for dt in row_out_dtypes] + [jax.ShapeDtypeStruct((1, d), F32)] * n_acc,
        (*row_ins, *vec_ins), side)
    return res if side is None else (res, extra)


def _norm0(h, g):
    tp, d = h.shape

    def body(h_ref, g_ref, n_ref):
        n_ref[...] = _rms(h_ref[...], g_ref[...]).astype(BF)

    return _row_call(body, "norm0", tp, d, [h], [g], [BF], 0)[0]


def _post_fwd(f, h, g_post, g_next, scale, name):
    tp, d = h.shape

    def body(f_ref, h_ref, gp_ref, gn_ref, hn_ref, n_ref):
        hn = h_ref[...] + scale * _rms(f_ref[...], gp_ref[...])
        hn_ref[...] = hn
        n_ref[...] = _rms(hn, gn_ref[...]).astype(BF)

    return _row_call(body, name, tp, d, [f, h], [g_post, g_next], [F32, BF], 0)


def _loss_bwd(f, h, tgt, g_post, t_real):
    tp, d = h.shape
    te = tp // EW_TILES

    def body(f_ref, h_ref, t_ref, gp_ref, dh_ref, df_ref, dg_ref, loss_ref):
        i = pl.program_id(0)

        @pl.when(i == 0)
        def _():
            dg_ref[...] = jnp.zeros_like(dg_ref)
            loss_ref[...] = jnp.zeros_like(loss_ref)

        f = f_ref[...]
        gp = gp_ref[...]
        h3 = h_ref[...] + 0.5 * _rms(f, gp)
        rows = i * te + lax.broadcasted_iota(jnp.int32, (te, 1), 0)
        real = (rows >= N_META) & (rows < t_real)
        e = jnp.where(real, h3 - t_ref[...], 0.0)
        loss_ref[...] += 0.5 * jnp.sum(jnp.sum(e * e, axis=1, keepdims=True), axis=0, keepdims=True) / d
        dh = e / d
        dh_ref[...] = dh
        dfv, dgr = _rms_bwd(f, gp, 0.5 * dh)
        df_ref[...] = dfv.astype(BF)
        dg_ref[...] += jnp.sum(dgr, axis=0, keepdims=True)

    rspec = pl.BlockSpec((te, d), lambda i: (i, 0))
    vspec = pl.BlockSpec((1, d), lambda i: (0, 0))
    return pl.pallas_call(
        body, name="loss_bwd", grid=(EW_TILES,),
        in_specs=[rspec, rspec, rspec, vspec],
        out_specs=[rspec, rspec, vspec, pl.BlockSpec((1, 1), lambda i: (0, 0))],
        out_shape=[jax.ShapeDtypeStruct((tp, d), F32), jax.ShapeDtypeStruct((tp, d), BF),
                   jax.ShapeDtypeStruct((1, d), F32), jax.ShapeDtypeStruct((1, 1), F32)],
        compiler_params=_params(("arbitrary",)),
    )(f, h, tgt, g_post)


def _pre_bwd(dn, h, dh_out, g_pre, name, chain=None, side=None):
    tp, d = h.shape

    def body(*refs):
        if chain is None:
            dn_ref, h_ref, dho_ref, g_ref, dh_ref, dg_ref = refs
        else:
            dn_ref, h_ref, dho_ref, p_ref, g_ref, gp_ref, dh_ref, dp_ref, dg_ref, dgp_ref = refs
        i = pl.program_id(0)

        @pl.when(i == 0)
        def _():
            dg_ref[...] = jnp.zeros_like(dg_ref)
            if chain is not None:
                dgp_ref[...] = jnp.zeros_like(dgp_ref)

        dx, dgr = _rms_bwd(h_ref[...], g_ref[...], dn_ref[...])
        dh = dho_ref[...] + dx
        dh_ref[...] = dh
        dg_ref[...] += jnp.sum(dgr, axis=0, keepdims=True)
        if chain is not None:
            dp, dgpr = _rms_bwd(p_ref[...], gp_ref[...], chain[2] * dh)
            dp_ref[...] = dp.astype(BF)
            dgp_ref[...] += jnp.sum(dgpr, axis=0, keepdims=True)

    if chain is None:
        return _row_call(body, name, tp, d, [dn, h, dh_out], [g_pre], [F32], 1, side)
    return _row_call(body, name, tp, d, [dn, h, dh_out, chain[0]], [g_pre, chain[1]], [F32, BF], 2, side)


def _gelu(y):
    c = math.sqrt(2.0 / math.pi)
    return 0.5 * y * (1.0 + jnp.tanh(c * (y + 0.044715 * y * y * y)))


def _gelu_and_grad(y):
    c = math.sqrt(2.0 / math.pi)
    y2 = y * y
    t = jnp.tanh(c * y * (1.0 + 0.044715 * y2))
    half = 0.5 * (1.0 + t)
    return y * half, half + 0.5 * y * (1.0 - t * t) * c * (1.0 + 3.0 * 0.044715 * y2)


def _neg_expm1(x):
    p = 1.0 + x * (1.0 / 9.0)
    for n in (8.0, 7.0, 6.0, 5.0, 4.0, 3.0, 2.0):
        p = 1.0 + x * (1.0 / n) * p
    return -jnp.where(x > -0.35, x * p, jnp.exp(x) - 1.0)


def _softplus(x):
    e = jnp.exp(-jnp.abs(x))
    w = 1.0 + e
    l1p = jnp.where(w == 1.0, e, jnp.log(w) * (e / jnp.where(w == 1.0, 1.0, w - 1.0)))
    return jnp.maximum(x, 0.0) + l1p


def _group_mean(v, gm):
    hi = v.astype(BF)
    lo = (v - hi.astype(F32)).astype(BF)
    return _dot(hi, gm) + _dot(lo, gm)


def _shift_dn(win, s, r):
    if s == 0:
        return win[8:8 + r]
    return pltpu.roll(win, s, 0)[8:8 + r]


def _shift_up(win, s, r):
    if s == 0:
        return win[0:r]
    return pltpu.roll(win, r + 8 - s, 0)[0:r]


def _window_dn(ref, t0, r, first):
    if first:
        return jnp.concatenate([jnp.zeros((8, ref.shape[1]), F32), ref[0:r, :]], axis=0)
    return ref[pl.ds(t0 - 8, r + 8), :]


def _tile_scan(a, u, reverse):
    r = a.shape[0]
    rid = lax.broadcasted_iota(jnp.int32, a.shape, 0) & 7
    for dlt in (1, 2, 4):
        sh = (r - dlt) if reverse else dlt
        a_s = pltpu.roll(a, sh, 0)
        u_s = pltpu.roll(u, sh, 0)
        keep = (rid + dlt <= 7) if reverse else (rid >= dlt)
        u = jnp.where(keep, u + a * u_s, u)
        a = jnp.where(keep, a * a_s, a)
    return a, u


def _lru_gates(xc, wa, ba, wx, bx, sp):
    xb = xc.astype(BF)
    ga = _sigmoid(_dot(xb, wa) + ba)
    gx = _sigmoid(_dot(xb, wx) + bx)
    la = -LRU_C * ga * sp
    return ga, gx, la


def _conv4(win, w4, cb, r):
    return (cb + w4[3:4] * _shift_dn(win, 0, r) + w4[2:3] * _shift_dn(win, 1, r)
            + w4[1:2] * _shift_dn(win, 2, r) + w4[0:1] * _shift_dn(win, 3, r))


def _lru_fwd(z, w4, cb, wa2, ba, wx2, bx, lam, g_out, gm, side=None):
    tp = z.shape[0]
    dl = cb.shape[1]
    nb = dl // LANE
    r = tp // MIX_CHUNKS
    c = LANE

    def body(y_ref, x_ref, w4_ref, cb_ref, wa_ref, ba_ref, wx_ref, bx_ref, lam_ref, go_ref, gm_ref, m_ref, hs_ref):
        w4v = w4_ref[...]
        cbv = cb_ref[...]
        wa = wa_ref[...]
        wx = wx_ref[...]
        bav = ba_ref[...]
        bxv = bx_ref[...]
        gov = go_ref[...]
        gmv = gm_ref[...]
        sp = _softplus(-lam_ref[...])

        def chunk(t0, hprev, first):
            win = _window_dn(x_ref, t0, r, first)
            xc = _conv4(win, w4v, cbv, r)
            ga, gx, la = _lru_gates(xc, wa, bav, wx, bxv, sp)
            a = jnp.exp(la)
            u = jnp.sqrt(_neg_expm1(2.0 * la)) * gx * xc
            ac, uc = _tile_scan(a, u, False)
            for j in range(r // 8):
                hj = uc[8 * j:8 * j + 8] + ac[8 * j:8 * j + 8] * hprev
                hs_ref[pl.ds(t0 + 8 * j, 8), :] = hj
                hprev = jnp.broadcast_to(hj[7:8], (8, c))
            h = hs_ref[pl.ds(t0, r), :]
            lo = h * _gelu(y_ref[pl.ds(t0, r), :])
            rs = lax.rsqrt(_group_mean(lo * lo, gmv) + EPS)
            m_ref[pl.ds(t0, r), :] = (lo * rs * gov).astype(BF)
            return hprev

        hp = chunk(0, jnp.zeros((8, c), F32), True)

        def loop(ci, hp):
            return chunk(pl.multiple_of(ci * r, 16), hp, False)

        lax.fori_loop(1, MIX_CHUNKS, loop, hp)

    col = lambda off: pl.BlockSpec((tp, c), lambda j: (0, off + j))
    vec = pl.BlockSpec((1, c), lambda j: (0, j))
    return _grid_call(
        body, "lru_fwd", (nb,),
        [col(0), col(nb), pl.BlockSpec((8, c), lambda j: (0, j)), vec, pl.BlockSpec((None, c, c), lambda j: (j, 0, 0)),
         vec, pl.BlockSpec((None, c, c), lambda j: (j, 0, 0)), vec, vec, vec, pl.BlockSpec((c, c), lambda j: (0, 0))],
        [col(0), col(0)], [jax.ShapeDtypeStruct((tp, dl), BF), jax.ShapeDtypeStruct((tp, dl), F32)],
        (z, z, w4, cb, wa2, ba, wx2, bx, lam, g_out, gm), side)


def _lru_bwd(z, hs, dmix, w4, cb, wa2, ba, wx2, bx, lam, g_out, gm, side=None):
    tp = z.shape[0]
    dl = cb.shape[1]
    nb = dl // LANE
    r = tp // MIX_CHUNKS
    c = LANE

    def body(y_ref, x_ref, hs_ref, dm_ref, w4_ref, cb_ref, wa_ref, ba_ref, wx_ref, bx_ref, lam_ref, go_ref, gm_ref,
             dy_ref, dx_ref, small_ref, dwa_ref, dwx_ref, xc_buf, ga_buf, gx_buf, a_buf, dh_buf, dxc_buf):
        w4v = w4_ref[...]
        cbv = cb_ref[...]
        wa = wa_ref[...]
        wx = wx_ref[...]
        bav = ba_ref[...]
        bxv = bx_ref[...]
        gov = go_ref[...]
        gmv = gm_ref[...]
        lamv = lam_ref[...]
        sp = _softplus(-lamv)
        small_ref[...] = jnp.zeros_like(small_ref)
        dwa_ref[...] = jnp.zeros_like(dwa_ref)
        dwx_ref[...] = jnp.zeros_like(dwx_ref)
        a_buf[pl.ds(tp, 8), :] = jnp.zeros((8, c), F32)
        dxc_buf[pl.ds(tp, 8), :] = jnp.zeros((8, c), F32)

        def fwd_chunk(t0, first):
            win = _window_dn(x_ref, t0, r, first)
            xc = _conv4(win, w4v, cbv, r)
            ga, gx, la = _lru_gates(xc, wa, bav, wx, bxv, sp)
            xc_buf[pl.ds(t0, r), :] = xc
            ga_buf[pl.ds(t0, r), :] = ga
            gx_buf[pl.ds(t0, r), :] = gx
            a_buf[pl.ds(t0, r), :] = jnp.exp(la)
            h = hs_ref[pl.ds(t0, r), :]
            yv = y_ref[pl.ds(t0, r), :]
            ge, dge = _gelu_and_grad(yv)
            lo = h * ge
            rs = lax.rsqrt(_group_mean(lo * lo, gmv) + EPS)
            xh = lo * rs
            dm = dm_ref[pl.ds(t0, r), :]
            q = dm * gov
            dlo = rs * (q - xh * _group_mean(q * xh, gmv))
            small_ref[8:9, :] += jnp.sum(dm * xh, axis=0, keepdims=True)
            dh_buf[pl.ds(t0, r), :] = dlo * ge
            dy_ref[pl.ds(t0, r), :] = (dlo * h * dge).astype(BF)

        fwd_chunk(0, True)

        def floop(ci, carry):
            fwd_chunk(pl.multiple_of(ci * r, 16), False)
            return carry

        lax.fori_loop(1, MIX_CHUNKS, floop, 0)

        def bwd_chunk(t0, vnext, first):
            ap = _shift_up(a_buf[pl.ds(t0, r + 8), :], 1, r)
            ac, uc = _tile_scan(ap, dh_buf[pl.ds(t0, r), :], True)
            for j in reversed(range(r // 8)):
                vj = uc[8 * j:8 * j + 8] + ac[8 * j:8 * j + 8] * vnext
                dh_buf[pl.ds(t0 + 8 * j, 8), :] = vj
                vnext = jnp.broadcast_to(vj[0:1], (8, c))
            v = dh_buf[pl.ds(t0, r), :]
            hprev = _shift_dn(_window_dn(hs_ref, t0, r, first), 1, r)
            xc = xc_buf[pl.ds(t0, r), :]
            ga = ga_buf[pl.ds(t0, r), :]
            gx = gx_buf[pl.ds(t0, r), :]
            a = a_buf[pl.ds(t0, r), :]
            em = _neg_expm1(-2.0 * LRU_C * ga * sp)
            mult = jnp.sqrt(em)
            dla = v * hprev * a - (v * gx * xc) * ((1.0 - em) / mult)
            dgx = v * mult * xc
            dxc = v * mult * gx
            dga = dla * (-LRU_C) * sp
            small_ref[7:8, :] += jnp.sum(dla * (-LRU_C) * ga, axis=0, keepdims=True)
            dpa = dga * ga * (1.0 - ga)
            dpx = dgx * gx * (1.0 - gx)
            small_ref[5:6, :] += jnp.sum(dpa, axis=0, keepdims=True)
            small_ref[6:7, :] += jnp.sum(dpx, axis=0, keepdims=True)
            dpab = dpa.astype(BF)
            dpxb = dpx.astype(BF)
            xb = xc.astype(BF)
            dxc = dxc + _dot_nt(dpab, wa) + _dot_nt(dpxb, wx)
            dwa_ref[...] += _dot_tn(xb, dpab)
            dwx_ref[...] += _dot_tn(xb, dpxb)
            dxc_buf[pl.ds(t0, r), :] = dxc
            small_ref[4:5, :] += jnp.sum(dxc, axis=0, keepdims=True)
            dwin = dxc_buf[pl.ds(t0, r + 8), :]
            dx_ref[pl.ds(t0, r), :] = (w4v[3:4] * dxc + w4v[2:3] * _shift_up(dwin, 1, r)
                                       + w4v[1:2] * _shift_up(dwin, 2, r) + w4v[0:1] * _shift_up(dwin, 3, r)).astype(BF)
            xwin = _window_dn(x_ref, t0, r, first)
            for k in range(4):
                small_ref[k:k + 1, :] += jnp.sum(dxc * _shift_dn(xwin, 3 - k, r), axis=0, keepdims=True)
            return vnext

        def bloop(it, vnext):
            ci = MIX_CHUNKS - 1 - it
            return bwd_chunk(pl.multiple_of(ci * r, 16), vnext, False)

        vn = lax.fori_loop(0, MIX_CHUNKS - 1, bloop, jnp.zeros((8, c), F32))
        bwd_chunk(0, vn, True)
        small_ref[7:8, :] = small_ref[7:8, :] * (-_sigmoid(-lamv))

    col = lambda off: pl.BlockSpec((tp, c), lambda j: (0, off + j))
    vec = pl.BlockSpec((1, c), lambda j: (0, j))
    mat = pl.BlockSpec((None, c, c), lambda j: (j, 0, 0))
    buf = pltpu.VMEM((tp, c), F32)
    bufp = pltpu.VMEM((tp + 8, c), F32)
    return _grid_call(
        body, "lru_bwd", (nb,),
        [col(0), col(nb), col(0), col(0), pl.BlockSpec((8, c), lambda j: (0, j)), vec, mat, vec, mat, vec, vec, vec,
         pl.BlockSpec((c, c), lambda j: (0, 0))],
        [col(0), col(0), pl.BlockSpec((16, c), lambda j: (0, j)), mat, mat],
        [jax.ShapeDtypeStruct((tp, dl), BF), jax.ShapeDtypeStruct((tp, dl), BF), jax.ShapeDtypeStruct((16, dl), F32),
         jax.ShapeDtypeStruct((nb, c, c), F32), jax.ShapeDtypeStruct((nb, c, c), F32)],
        (z, z, hs, dmix, w4, cb, wa2, ba, wx2, bx, lam, g_out, gm), side, [buf, buf, buf, bufp, buf, bufp])


def _sc_conv(cvwin, w3, r):
    return w3[2:3] * _shift_dn(cvwin, 0, r) + w3[1:2] * _shift_dn(cvwin, 1, r) + w3[0:1] * _shift_dn(cvwin, 2, r)


def _sc_fwd(z, w3, g_out, gm, dl, side=None):
    tp = z.shape[0]
    nb = dl // LANE
    r = tp // MIX_CHUNKS
    c = LANE

    def body(b_ref, c_ref, v_ref, w3_ref, go_ref, gm_ref, m_ref):
        w3v = w3_ref[...]
        gov = go_ref[...]
        gmv = gm_ref[...]

        def chunk(t0, first):
            cvwin = _window_dn(c_ref, t0, r, first) * _window_dn(v_ref, t0, r, first)
            so = b_ref[pl.ds(t0, r), :] * _sc_conv(cvwin, w3v, r)
            rs = lax.rsqrt(_group_mean(so * so, gmv) + EPS)
            m_ref[pl.ds(t0, r), :] = (so * rs * gov).astype(BF)

        chunk(0, True)

        def loop(ci, carry):
            chunk(pl.multiple_of(ci * r, 16), False)
            return carry

        lax.fori_loop(1, MIX_CHUNKS, loop, 0)

    col = lambda off: pl.BlockSpec((tp, c), lambda j: (0, off + j))
    res, extra = _grid_call(
        body, "sconv_fwd", (nb,),
        [col(2 * nb), col(3 * nb), col(4 * nb), pl.BlockSpec((8, c), lambda j: (0, j)),
         pl.BlockSpec((1, c), lambda j: (0, j)), pl.BlockSpec((c, c), lambda j: (0, 0))],
        [col(0)], [jax.ShapeDtypeStruct((tp, dl), BF)], (z, z, z, w3, g_out, gm), side)
    return res[0], extra


def _sc_bwd(z, dmix, w3, g_out, gm, dl, side=None):
    tp = z.shape[0]
    nb = dl // LANE
    r = tp // MIX_CHUNKS
    c = LANE

    def body(b_ref, c_ref, v_ref, dm_ref, w3_ref, go_ref, gm_ref, db_ref, dc_ref, dv_ref, small_ref, dsc_buf):
        w3v = w3_ref[...]
        gov = go_ref[...]
        gmv = gm_ref[...]
        small_ref[...] = jnp.zeros_like(small_ref)
        dsc_buf[pl.ds(tp, 8), :] = jnp.zeros((8, c), F32)

        def chunk1(t0, first):
            cvwin = _window_dn(c_ref, t0, r, first) * _window_dn(v_ref, t0, r, first)
            sc = _sc_conv(cvwin, w3v, r)
            bv = b_ref[pl.ds(t0, r), :]
            so = bv * sc
            rs = lax.rsqrt(_group_mean(so * so, gmv) + EPS)
            xh = so * rs
            dm = dm_ref[pl.ds(t0, r), :]
            q = dm * gov
            dso = rs * (q - xh * _group_mean(q * xh, gmv))
            small_ref[3:4, :] += jnp.sum(dm * xh, axis=0, keepdims=True)
            db_ref[pl.ds(t0, r), :] = (dso * sc).astype(BF)
            dsc = dso * bv
            dsc_buf[pl.ds(t0, r), :] = dsc
            for k in range(3):
                small_ref[k:k + 1, :] += jnp.sum(dsc * _shift_dn(cvwin, 2 - k, r), axis=0, keepdims=True)

        chunk1(0, True)

        def loop1(ci, carry):
            chunk1(pl.multiple_of(ci * r, 16), False)
            return carry

        lax.fori_loop(1, MIX_CHUNKS, loop1, 0)

        def loop2(ci, carry):
            t0 = pl.multiple_of(ci * r, 16)
            dwin = dsc_buf[pl.ds(t0, r + 8), :]
            dcv = w3v[2:3] * _shift_up(dwin, 0, r) + w3v[1:2] * _shift_up(dwin, 1, r) + w3v[0:1] * _shift_up(dwin, 2, r)
            dc_ref[pl.ds(t0, r), :] = (dcv * v_ref[pl.ds(t0, r), :]).astype(BF)
            dv_ref[pl.ds(t0, r), :] = (dcv * c_ref[pl.ds(t0, r), :]).astype(BF)
            return carry

        lax.fori_loop(0, MIX_CHUNKS, loop2, 0)

    col = lambda off: pl.BlockSpec((tp, c), lambda j: (0, off + j))
    out = jax.ShapeDtypeStruct((tp, dl), BF)
    return _grid_call(
        body, "sconv_bwd", (nb,),
        [col(2 * nb), col(3 * nb), col(4 * nb), col(nb), pl.BlockSpec((8, c), lambda j: (0, j)),
         pl.BlockSpec((1, c), lambda j: (0, j)), pl.BlockSpec((c, c), lambda j: (0, 0))],
        [col(0), col(0), col(0), pl.BlockSpec((8, c), lambda j: (0, j))],
        [out, out, out, jax.ShapeDtypeStruct((8, dl), F32)], (z, z, z, dmix, w3, g_out, gm), side,
        [pltpu.VMEM((tp + 8, c), F32)])


def _cast_pad(w, rows_p, cols_p, chip, name):
    r, c = w.shape

    def body(chip_ref, w_ref, o_ref):
        if (rows_p, cols_p) != (r, c):
            o_ref[...] = jnp.zeros_like(o_ref)
        o_ref[0:r, 0:c] = w_ref[...].astype(BF)

    return pl.pallas_call(
        body, name=name, out_shape=jax.ShapeDtypeStruct((N_CHIP, rows_p, cols_p), BF),
        grid_spec=pltpu.PrefetchScalarGridSpec(
            num_scalar_prefetch=1, grid=(1,),
            in_specs=[pl.BlockSpec((r, c), lambda i, chip: (0, 0))],
            out_specs=pl.BlockSpec((None, rows_p, cols_p), lambda i, chip: (chip[0], 0, 0))),
        compiler_params=_params(("arbitrary",)),
    )(chip, w)


def _adamw_math(w, g, m, v):
    m2 = ADAM_B1 * m + (1.0 - ADAM_B1) * g
    v2 = ADAM_B2 * v + (1.0 - ADAM_B2) * (g * g)
    m_hat = m2 / (1.0 - ADAM_B1 ** ADAM_STEP)
    v_hat = v2 / (1.0 - ADAM_B2 ** ADAM_STEP)
    delta = -ADAM_LR * (m_hat / (jnp.sqrt(v_hat) + ADAM_EPS) + ADAM_WD * w)
    return delta, m2, v2


def _adamw(w, g, m, v, name, row_tiles, col_tiles, side=None):
    r, c = w.shape
    tr = r // row_tiles
    tc = c // col_tiles
    gc = g.shape[1] if col_tiles == 1 else tc

    def body(w_ref, g_ref, m_ref, v_ref, go_ref, d_ref, mo_ref, vo_ref):
        gv = g_ref[...][:, 0:tc]
        delta, m2, v2 = _adamw_math(w_ref[...], gv, m_ref[...], v_ref[...])
        go_ref[...] = gv
        d_ref[...] = delta
        mo_ref[...] = m2
        vo_ref[...] = v2

    spec = pl.BlockSpec((tr, tc), lambda i, j: (i, j))
    out = jax.ShapeDtypeStruct((r, c), F32)
    return _grid_call(body, name, (row_tiles, col_tiles), [spec, pl.BlockSpec((tr, gc), lambda i, j: (i, j)), spec, spec],
                      [spec] * 4, [out] * 4, (w, g, m, v), side)


def _adamw_small(w, g_top, g4, m, v):
    def body(w_ref, gt_ref, g_ref, m_ref, v_ref, go_ref, d_ref, mo_ref, vo_ref):
        g = jnp.concatenate([gt_ref[...], (g_ref[0] + g_ref[1]) + (g_ref[2] + g_ref[3])], axis=0)
        delta, m2, v2 = _adamw_math(w_ref[...], g, m_ref[...], v_ref[...])
        go_ref[...] = g
        d_ref[...] = delta
        mo_ref[...] = m2
        vo_ref[...] = v2

    out = jax.ShapeDtypeStruct(w.shape, F32)
    spec = pl.BlockSpec(w.shape, lambda: (0, 0))
    return pl.pallas_call(
        body, name="adamw_small",
        in_specs=[spec, pl.BlockSpec(g_top.shape, lambda: (0, 0)), pl.BlockSpec(g4.shape, lambda: (0, 0, 0)), spec, spec],
        out_specs=[spec] * 4, out_shape=[out] * 4, compiler_params=_params())(w, g_top, g4, m, v)


def _place():
    x, y, c = lax.axis_index("x"), lax.axis_index("y"), lax.axis_index("c")
    chips = [(1 - x, y), (x, 1 - y), (1 - x, 1 - y)]
    return x, y, c, chips


ANY = pl.BlockSpec(memory_space=pl.ANY)


REL_SLOT = (2, 1, 3)


def _gather_side(bufs, relative=False, two_path=False):
    n = len(bufs)
    direct = (0, 1) if two_path else (0, 1, 2)

    def copies(outs, sems):
        s_ici, r_ici, s_d2d, r_d2d = sems[:4]
        x, y, c, chips = _place()
        me = 2 * x + y

        def rows(w, slot, core, part=None):
            half = bufs[w].shape[1] // 2
            if part is None:
                return outs[w].at[slot, pl.ds(core * half, half)]
            return outs[w].at[slot, pl.ds(core * half + part * (half // 2), half // 2)]

        def theirs(j):
            return REL_SLOT[j] if relative else 2 * chips[j][0] + chips[j][1]

        def ici_send(w, j):
            px, py = chips[j]
            return pltpu.make_async_remote_copy(
                src_ref=rows(w, 0 if relative else me, c), dst_ref=rows(w, REL_SLOT[j] if relative else me, c),
                send_sem=s_ici.at[w, j], recv_sem=r_ici.at[w, j], device_id=(px, py, c), device_id_type=MESH)

        def ici_recv(w, j):
            px, py = chips[j]
            return pltpu.make_async_remote_copy(
                src_ref=rows(w, theirs(j), c), dst_ref=rows(w, theirs(j), c),
                send_sem=s_ici.at[w, j], recv_sem=r_ici.at[w, j], device_id=(px, py, c), device_id_type=MESH)

        def hop_send(w, p):
            px, py = chips[1 - p]
            return pltpu.make_async_remote_copy(
                src_ref=rows(w, theirs(p), c, p), dst_ref=rows(w, REL_SLOT[2] if relative else theirs(p), c, p),
                send_sem=sems[4].at[w, p], recv_sem=sems[5].at[w, p], device_id=(px, py, c), device_id_type=MESH)

        def hop_recv(w, p):
            px, py = chips[1 - p]
            return pltpu.make_async_remote_copy(
                src_ref=rows(w, theirs(2), c, p), dst_ref=rows(w, theirs(2), c, p),
                send_sem=sems[4].at[w, p], recv_sem=sems[5].at[w, p], device_id=(px, py, c), device_id_type=MESH)

        def d2d(w, j, core):
            return pltpu.make_async_remote_copy(
                src_ref=rows(w, theirs(j), core), dst_ref=rows(w, theirs(j), core),
                send_sem=s_d2d.at[w, j], recv_sem=r_d2d.at[w, j], device_id=(x, y, 1 - c), device_id_type=MESH)

        return c, ici_send, ici_recv, hop_send, hop_recv, d2d

    def send(outs, sems):
        c, ici_send, ici_recv, hop_send, hop_recv, d2d = copies(outs, sems)
        for j in direct:
            for w in range(n):
                ici_send(w, j).start()

    def arrived(j, outs, sems):
        c, ici_send, ici_recv, hop_send, hop_recv, d2d = copies(outs, sems)
        for w in range(n):
            if j in direct:
                ici_recv(w, j).wait_recv()
                if two_path:
                    hop_send(w, j).start()
            else:
                hop_recv(w, 0).wait_recv()
                hop_recv(w, 1).wait_recv()
            d2d(w, j, c).start()

    def forwarded(j, outs, sems):
        c, ici_send, ici_recv, hop_send, hop_recv, d2d = copies(outs, sems)
        for w in range(n):
            d2d(w, j, 1 - c).wait_recv()

    def drain(outs, sems):
        c, ici_send, ici_recv, hop_send, hop_recv, d2d = copies(outs, sems)
        for w in range(n):
            for j in direct:
                ici_send(w, j).wait_send()
                if two_path:
                    hop_send(w, j).wait_send()
            for j in range(3):
                d2d(w, j, c).wait_send()

    def start(ins, outs, sems):
        send(outs, sems)

    def middle(ins, outs, sems):
        arrived(0, outs, sems)
        arrived(1, outs, sems)

    def rest(ins, outs, sems):
        arrived(2, outs, sems)
        for j in range(3):
            forwarded(j, outs, sems)
        drain(outs, sems)

    def finish(ins, outs, sems):
        middle(ins, outs, sems)
        rest(ins, outs, sems)

    dma = pltpu.SemaphoreType.DMA((n, 3))
    hop = [pltpu.SemaphoreType.DMA((n, 2))] * 2 if two_path else []
    side = _Side(list(bufs), [jax.ShapeDtypeStruct(b.shape, b.dtype) for b in bufs], {w: w for w in range(n)},
                 [dma, dma, dma, dma] + hop, start, finish)
    side.send, side.arrived, side.forwarded, side.drain = send, arrived, forwarded, drain
    side.middle, side.rest = middle, rest
    return side


def _run_side(side, name):
    sin, sout = len(side.ins), len(side.outs)

    def body(*refs):
        ins, outs, sems = refs[:sin], refs[sin:sin + sout], refs[sin + sout:]
        side.start(ins, outs, sems)
        side.finish(ins, outs, sems)

    return pl.pallas_call(
        body, name=name, out_shape=list(side.outs), in_specs=[ANY] * sin, out_specs=[ANY] * sout,
        scratch_shapes=list(side.sems), input_output_aliases=dict(side.alias))(*side.ins)


def _pair_exchange_side(grads):
    n = len(grads)

    def copies(ins, outs, sems):
        ssem, rsem = sems
        x, y, c, _ = _place()
        cps = []
        for w in range(n):
            half = grads[w].shape[1] // 2
            cps.append(pltpu.make_async_remote_copy(
                src_ref=ins[w].at[:, pl.ds((1 - c) * half, half)], dst_ref=outs[w],
                send_sem=ssem.at[w], recv_sem=rsem.at[w], device_id=(x, y, 1 - c), device_id_type=MESH))
        return cps

    def start(ins, outs, sems):
        for cp in copies(ins, outs, sems):
            cp.start()

    def finish(ins, outs, sems):
        for cp in copies(ins, outs, sems):
            cp.wait()

    dma = pltpu.SemaphoreType.DMA((n,))
    return _Side(list(grads), [jax.ShapeDtypeStruct((N_CHIP, g.shape[1] // 2, g.shape[2]), BF) for g in grads], {},
                 [dma, dma], start, finish)


def _sibling_copy_side(buf):
    def copy(ins, outs, sems):
        x, y, c, _ = _place()
        return pltpu.make_async_remote_copy(src_ref=ins[0], dst_ref=outs[0], send_sem=sems[0], recv_sem=sems[1],
                                            device_id=(x, y, 1 - c), device_id_type=MESH)

    return _Side([buf], [jax.ShapeDtypeStruct(buf.shape, buf.dtype)], {}, [pltpu.SemaphoreType.DMA, pltpu.SemaphoreType.DMA],
                 lambda i, o, s: copy(i, o, s).start(), lambda i, o, s: copy(i, o, s).wait())


def _slot_exchange_side(buf4):
    def copies(outs, sems, sending):
        ssem, rsem = sems
        x, y, c, chips = _place()
        me = 2 * x + y
        return [pltpu.make_async_remote_copy(
            src_ref=outs[0].at[me if sending else 2 * px + py], dst_ref=outs[0].at[me if sending else 2 * px + py],
            send_sem=ssem.at[j], recv_sem=rsem.at[j], device_id=(px, py, c), device_id_type=MESH)
            for j, (px, py) in enumerate(chips)]

    def start(ins, outs, sems):
        for cp in copies(outs, sems, True):
            cp.start()

    def finish(ins, outs, sems):
        for cp in copies(outs, sems, False):
            cp.wait_recv()
        for cp in copies(outs, sems, True):
            cp.wait_send()

    dma = pltpu.SemaphoreType.DMA((3,))
    return _Side([buf4], [jax.ShapeDtypeStruct(buf4.shape, buf4.dtype)], {0: 0}, [dma, dma], start, finish)


def _pair_sum(g, sib, core, name):
    _, r, cdim = g.shape
    half = r // 2

    def body(core_ref, g_ref, s_ref, o_ref):
        o_ref[...] = (g_ref[...].astype(F32) + s_ref[...].astype(F32)).astype(BF)

    return pl.pallas_call(
        body, name=name,
        grid_spec=pltpu.PrefetchScalarGridSpec(
            num_scalar_prefetch=1, grid=(N_CHIP,),
            in_specs=[pl.BlockSpec((None, half, cdim), lambda k, core: (k, core[0], 0)),
                      pl.BlockSpec((None, half, cdim), lambda k, core: (k, 0, 0))],
            out_specs=pl.BlockSpec((None, half, cdim), lambda k, core: (k, 0, 0))),
        out_shape=jax.ShapeDtypeStruct((N_CHIP, half, cdim), BF),
        compiler_params=_params(("arbitrary",)),
    )(core, g, sib)


def _chip_exchange_side(psums, relative=False):
    n = len(psums)

    def copies(ins, outs, sems):
        ssem, rsem = sems
        x, y, c, chips = _place()
        return [pltpu.make_async_remote_copy(
            src_ref=ins[w].at[REL_SLOT[j] if relative else 2 * px + py], dst_ref=outs[w].at[j],
            send_sem=ssem.at[w, j], recv_sem=rsem.at[w, j], device_id=(px, py, c), device_id_type=MESH)
            for w in range(n) for j, (px, py) in enumerate(chips)]

    def start(ins, outs, sems):
        for cp in copies(ins, outs, sems):
            cp.start()

    def finish(ins, outs, sems):
        for cp in copies(ins, outs, sems):
            cp.wait()

    dma = pltpu.SemaphoreType.DMA((n, 3))
    return _Side(list(psums), [jax.ShapeDtypeStruct((3,) + p.shape[1:], BF) for p in psums], {}, [dma, dma],
                 start, finish)


def _final_sum(g, sib, recv, sel, name):
    _, r, cdim = g.shape
    half = r // 2
    nt = 4
    th = half // nt

    def body(sel_ref, g_ref, s_ref, r_ref, o_ref):
        acc = g_ref[...].astype(F32) + s_ref[...].astype(F32)
        for j in range(3):
            acc = acc + r_ref[j].astype(F32)
        o_ref[...] = acc

    return pl.pallas_call(
        body, name=name,
        grid_spec=pltpu.PrefetchScalarGridSpec(
            num_scalar_prefetch=1, grid=(nt,),
            in_specs=[pl.BlockSpec((None, th, cdim), lambda i, sel: (sel[0], sel[1] * nt + i, 0)),
                      pl.BlockSpec((None, th, cdim), lambda i, sel: (sel[0], i, 0)),
                      pl.BlockSpec((3, th, cdim), lambda i, sel: (0, i, 0))],
            out_specs=pl.BlockSpec((th, cdim), lambda i, sel: (sel[1] * nt + i, 0))),
        out_shape=jax.ShapeDtypeStruct((r, cdim), F32),
        compiler_params=_params(("arbitrary",)),
    )(sel, g, sib, recv)


def _join_side(bufs):
    n = len(bufs)

    def copies(outs, sems, core_of):
        ssem, rsem = sems
        x, y, c, _ = _place()
        cps = []
        for w in range(n):
            half = bufs[w].shape[0] // 2
            rows = outs[w].at[pl.ds(core_of(c) * half, half)]
            cps.append(pltpu.make_async_remote_copy(
                src_ref=rows, dst_ref=rows, send_sem=ssem.at[w], recv_sem=rsem.at[w],
                device_id=(x, y, 1 - c), device_id_type=MESH))
        return cps

    def start(ins, outs, sems):
        for cp in copies(outs, sems, lambda c: c):
            cp.start()

    def finish(ins, outs, sems):
        for cp in copies(outs, sems, lambda c: 1 - c):
            cp.wait_recv()
        for cp in copies(outs, sems, lambda c: c):
            cp.wait_send()

    dma = pltpu.SemaphoreType.DMA((n,))
    return _Side(list(bufs), [jax.ShapeDtypeStruct(b.shape, F32) for b in bufs], {w: w for w in range(n)}, [dma, dma],
                 start, finish)


def _small_pair_sum(buf, sib, chip):
    rows, d = buf.shape

    def body(chip_ref, a_ref, b_ref, o_ref):
        o_ref[...] = a_ref[...] + b_ref[...]

    return pl.pallas_call(
        body, name="small_pair_sum", out_shape=jax.ShapeDtypeStruct((N_CHIP, rows, d), F32),
        grid_spec=pltpu.PrefetchScalarGridSpec(
            num_scalar_prefetch=1, grid=(1,),
            in_specs=[pl.BlockSpec((rows, d), lambda i, chip: (0, 0))] * 2,
            out_specs=pl.BlockSpec((None, rows, d), lambda i, chip: (chip[0], 0, 0))),
        compiler_params=_params(("arbitrary",)),
    )(chip, buf, sib)


def _small_all_reduce(buf, name):
    rows, d = buf.shape

    def body(in_ref, out_ref, sib, all4, ssem, rsem, psem, qsem):
        x, y, c, chips = _place()
        me = 2 * x + y
        to_sib = pltpu.make_async_remote_copy(src_ref=in_ref, dst_ref=sib, send_sem=ssem, recv_sem=rsem,
                                              device_id=(x, y, 1 - c), device_id_type=MESH)
        to_sib.start()
        to_sib.wait()
        all4[me] = in_ref[...] + sib[...]
        cps = [pltpu.make_async_remote_copy(src_ref=all4.at[me], dst_ref=all4.at[me], send_sem=psem.at[j],
                                            recv_sem=qsem.at[j], device_id=(px, py, c), device_id_type=MESH)
               for j, (px, py) in enumerate(chips)]
        for cp in cps:
            cp.start()
        for j, (px, py) in enumerate(chips):
            chip = 2 * px + py
            pltpu.make_async_remote_copy(src_ref=all4.at[chip], dst_ref=all4.at[chip], send_sem=psem.at[j],
                                         recv_sem=qsem.at[j], device_id=(px, py, c), device_id_type=MESH).wait_recv()
        for cp in cps:
            cp.wait_send()
        out_ref[...] = (all4[0] + all4[1]) + (all4[2] + all4[3])

    vm = pl.BlockSpec(memory_space=pltpu.VMEM)
    return pl.pallas_call(
        body, name=name, out_shape=jax.ShapeDtypeStruct((rows, d), F32),
        in_specs=[vm], out_specs=vm,
        scratch_shapes=[pltpu.VMEM((rows, d), F32), pltpu.VMEM((N_CHIP, rows, d), F32),
                        pltpu.SemaphoreType.DMA, pltpu.SemaphoreType.DMA,
                        pltpu.SemaphoreType.DMA((3,)), pltpu.SemaphoreType.DMA((3,))],
        compiler_params=_params(),
    )(buf)


def _pair_blocks(w):
    w4 = w.reshape(N_HEADS // 2, 2, HEAD, HEAD)
    eye = jnp.eye(2, dtype=w.dtype)
    return jnp.einsum("pirc,ij->pirjc", w4, eye).reshape(N_HEADS // 2, LANE, LANE)


def _unpair_blocks(w2):
    w5 = w2.reshape(N_HEADS // 2, 2, HEAD, 2, HEAD)
    return jnp.stack([w5[:, 0, :, 0, :], w5[:, 1, :, 1, :]], axis=1).reshape(N_HEADS, HEAD, HEAD)


def kernel(x, meta_tokens, ffn1_pre_g, ffn1_w_gate, ffn1_w_up, ffn1_w_down, ffn1_post_g, mix_pre_g, w_in, lru_conv_w, lru_conv_b, lru_w_a, lru_b_a, lru_w_x, lru_b_x, lru_lambda, sconv_w, lru_out_g, sconv_out_g, w_out, mix_post_g, ffn2_pre_g, ffn2_w_gate, ffn2_w_up, ffn2_w_down, ffn2_post_g, loss_target, m_meta_tokens, m_ffn1_pre_g, m_ffn1_w_gate, m_ffn1_w_up, m_ffn1_w_down, m_ffn1_post_g, m_mix_pre_g, m_w_in, m_lru_conv_w, m_lru_conv_b, m_lru_w_a, m_lru_b_a, m_lru_w_x, m_lru_b_x, m_lru_lambda, m_sconv_w, m_lru_out_g, m_sconv_out_g, m_w_out, m_mix_post_g, m_ffn2_pre_g, m_ffn2_w_gate, m_ffn2_w_up, m_ffn2_w_down, m_ffn2_post_g, v_meta_tokens, v_ffn1_pre_g, v_ffn1_w_gate, v_ffn1_w_up, v_ffn1_w_down, v_ffn1_post_g, v_mix_pre_g, v_w_in, v_lru_conv_w, v_lru_conv_b, v_lru_w_a, v_lru_b_a, v_lru_w_x, v_lru_b_x, v_lru_lambda, v_sconv_w, v_lru_out_g, v_sconv_out_g, v_w_out, v_mix_post_g, v_ffn2_pre_g, v_ffn2_w_gate, v_ffn2_w_up, v_ffn2_w_down, v_ffn2_post_g):
    seq, d = x.shape[1], x.shape[2]
    t_real = N_META + seq
    tp = _round_up(t_real, ROW_ALIGN)
    f4 = ffn1_w_gate.shape[2]
    f4p = _round_up(f4, LANE)
    dl = lru_conv_b.shape[1]
    cin = w_in.shape[2]
    xi, yi, ci = lax.axis_index("x"), lax.axis_index("y"), lax.axis_index("c")
    chip = 2 * xi + yi
    zero = jnp.zeros((), jnp.int32)

    transposed = ("ffn1_w_gate", "ffn1_w_up", "ffn2_w_gate", "ffn2_w_up")

    def view(k, a):
        return a[0].T if k in transposed else a[0]

    def unview(k, a):
        return (a.T if k in transposed else a)[None]

    big = {
        "ffn1_w_gate": (view("ffn1_w_gate", ffn1_w_gate), f4p, d), "ffn1_w_up": (view("ffn1_w_up", ffn1_w_up), f4p, d),
        "ffn1_w_down": (ffn1_w_down[0], f4p, d), "w_in": (w_in[0], d, cin), "w_out": (w_out[0], w_out.shape[1], d),
        "ffn2_w_gate": (view("ffn2_w_gate", ffn2_w_gate), f4p, d), "ffn2_w_up": (view("ffn2_w_up", ffn2_w_up), f4p, d),
        "ffn2_w_down": (ffn2_w_down[0], f4p, d),
    }
    names = list(big)
    chip1 = jnp.reshape(chip, (1,)).astype(jnp.int32)
    relative = {k: k.startswith("ffn") for k in names}
    slot0 = jnp.zeros((1,), jnp.int32)
    shard = {k: _cast_pad(big[k][0], big[k][1], big[k][2], slot0 if relative[k] else chip1, "cast_" + k) for k in names}
    full = {}

    def gather(*keys):
        return _merge_sides([_gather_side([shard[k]], relative[k], two_path=True) for k in keys])

    gm = jnp.kron(jnp.eye(2, dtype=F32), jnp.full((HEAD, HEAD), 1.0 / HEAD, F32)).astype(BF)
    wa2 = _pair_blocks(lru_w_a[0])
    wx2 = _pair_blocks(lru_w_x[0])

    dlq = dl // N_CHIP
    dq = d // N_CHIP
    R_GAIN, R_LOSS, R_META, R_LRU, R_SC, R_WA = 0, 6, 8, 24, 40, 48
    n_wrows = (N_HEADS // 2) * LANE * LANE // d
    R_WX = R_WA + n_wrows
    R_END = R_WX + n_wrows

    def pack_top(gains, meta, loss=None):
        lossrow = jnp.zeros((2, d), F32)
        if loss is not None:
            lossrow = lossrow.at[0, 0].set(loss)
        return jnp.concatenate([jnp.concatenate(gains, axis=0), lossrow, meta], axis=0)

    def pack_rest(lru16, sc8, wa_, wx_):
        return jnp.concatenate([jnp.concatenate([lru16, jnp.zeros((16, d - dl), F32)], axis=1),
                                jnp.concatenate([sc8, jnp.zeros((8, d - dl), F32)], axis=1),
                                wa_.reshape(n_wrows, d), wx_.reshape(n_wrows, d)], axis=0)

    def pack(gains, meta, lru16, sc8, wa_, wx_):
        return jnp.concatenate([pack_top(gains, meta), pack_rest(lru16, sc8, wa_, wx_)], axis=0)

    def place_cols(blk, width, total):
        return lax.dynamic_update_slice(jnp.zeros((blk.shape[0], total), F32), blk, (zero, chip * width))

    def pack_params(meta_, g1pre, g1post, gmpre, gmpost, g2pre, g2post, cw, cbias, wa_, ba_, wx_, bx_, lam_, sw, lgo, sgo):
        lru16 = jnp.concatenate([place_cols(cw[0], dlq, dl), cbias, ba_, bx_, lam_, lgo, jnp.zeros((7, dl), F32)], axis=0)
        sc8 = jnp.concatenate([place_cols(sw[0], dlq, dl), sgo, jnp.zeros((4, dl), F32)], axis=0)
        return pack([g1pre, g1post, gmpre, gmpost, g2pre, g2post], place_cols(meta_, dq, d), lru16, sc8,
                    _pair_blocks(wa_[0]), _pair_blocks(wx_[0]))

    p_w = pack_params(meta_tokens, ffn1_pre_g, ffn1_post_g, mix_pre_g, mix_post_g, ffn2_pre_g, ffn2_post_g, lru_conv_w,
                      lru_conv_b, lru_w_a, lru_b_a, lru_w_x, lru_b_x, lru_lambda, sconv_w, lru_out_g, sconv_out_g)
    p_m = pack_params(m_meta_tokens, m_ffn1_pre_g, m_ffn1_post_g, m_mix_pre_g, m_mix_post_g, m_ffn2_pre_g, m_ffn2_post_g,
                      m_lru_conv_w, m_lru_conv_b, m_lru_w_a, m_lru_b_a, m_lru_w_x, m_lru_b_x, m_lru_lambda, m_sconv_w,
                      m_lru_out_g, m_sconv_out_g)
    p_v = pack_params(v_meta_tokens, v_ffn1_pre_g, v_ffn1_post_g, v_mix_pre_g, v_mix_post_g, v_ffn2_pre_g, v_ffn2_post_g,
                      v_lru_conv_w, v_lru_conv_b, v_lru_w_a, v_lru_b_a, v_lru_w_x, v_lru_b_x, v_lru_lambda, v_sconv_w,
                      v_lru_out_g, v_sconv_out_g)

    gathered = _small_all_reduce(jnp.where(ci == 0, p_w, 0.0)[R_META:R_WA], "small_weight_gather")
    meta_full = gathered[0:N_META]
    w4_full = gathered[R_LRU - R_META:R_LRU - R_META + 4, 0:dl]
    w3_full = gathered[R_SC - R_META:R_SC - R_META + 3, 0:dl]
    w4p = jnp.concatenate([w4_full, jnp.zeros((4, dl), F32)], axis=0)
    w3p = jnp.concatenate([w3_full, jnp.zeros((5, dl), F32)], axis=0)

    h0 = jnp.concatenate([meta_full, x[0], jnp.zeros((tp - t_real, d), F32)], axis=0)
    tgt = jnp.concatenate([jnp.zeros((N_META, d), F32), loss_target[0], jnp.zeros((tp - t_real, d), F32)], axis=0)

    n1 = _norm0(h0, ffn1_pre_g)
    (a1, b1, s1), (full["ffn1_w_gate"], full["ffn1_w_up"]), got = _ffn_up_head(
        n1, shard["ffn1_w_gate"], shard["ffn1_w_up"], "ffn1_up",
        _gather_side([shard["ffn1_w_down"]], relative=True, two_path=True))
    full["ffn1_w_down"] = got[0]
    f1, got = _wide_matmul([(s1, full["ffn1_w_down"])], "ffn1_down", WIDE_TN, gather("w_in"))
    full["w_in"] = got[0]
    h1, u = _post_fwd(f1, h0, ffn1_post_g, mix_pre_g, 0.5, "ffn1_post")
    z, got = _col_matmul(u, full["w_in"], "in_proj", False, F32, gather("ffn2_w_gate"))
    full["ffn2_w_gate"] = got[0]
    (m_lru, hs), got = _lru_fwd(z, w4p, lru_conv_b, wa2.astype(BF), lru_b_a, wx2.astype(BF), lru_b_x, lru_lambda,
                                lru_out_g, gm, gather("ffn2_w_up"))
    full["ffn2_w_up"] = got[0]
    m_sc, got = _sc_fwd(z, w3p, sconv_out_g, gm, dl, gather("w_out"))
    full["w_out"] = got[0]
    mixed = jnp.concatenate([m_lru, m_sc], axis=1)
    p, _ = _row_matmul([(mixed, full["w_out"])], "out_proj", False, d)
    h2, n2 = _post_fwd(p, h1, mix_post_g, ffn2_pre_g, 1.0, "mix_post")
    (a2, b2, s2), got = _ffn_up(n2, full["ffn2_w_gate"], full["ffn2_w_up"], "ffn2_up", gather("ffn2_w_down"))
    full["ffn2_w_down"] = got[0]
    f2, _ = _wide_matmul([(s2, full["ffn2_w_down"])], "ffn2_down", WIDE_TN)
    dh3, df2, dg_ffn2_post, loss_part = _loss_bwd(f2, h2, tgt, ffn2_post_g, t_real)

    core = jnp.reshape(ci, (1,)).astype(jnp.int32)
    sel_of = {False: jnp.stack([chip, ci]).astype(jnp.int32), True: jnp.stack([0 * chip, ci]).astype(jnp.int32)}
    red = {}

    def pair_side(k):
        return _pair_exchange_side([red[k][0]])

    def chip_side(k):
        return _chip_exchange_side([_pair_sum(red[k][0], red[k][1], core, "pair_sum_" + k)], relative[k])

    def final_sum(k):
        return _final_sum(*red[k], sel_of[relative[k]], "final_sum_" + k)

    (da2, db2), _ = _ffn_bwd_act(df2, full["ffn2_w_down"], a2, b2, "ffn2_bwd_act")
    g, _ = _wgrad_call(s2, df2, "ffn2_down_wgrad", x_width=f4p, tile_y=WGRAD_TILE_Y)
    red["ffn2_w_down"] = [g, None, None]
    g, got = _wgrad_call(da2, n2, "ffn2_gate_wgrad", x_width=f4p, tile_y=WGRAD_TILE_Y, side=pair_side("ffn2_w_down"))
    red["ffn2_w_down"][1] = got[0]
    red["ffn2_w_gate"] = [g, None, None]
    g, got = _wgrad_call(db2, n2, "ffn2_up_wgrad", x_width=f4p, tile_y=WGRAD_TILE_Y,
                         side=_merge_sides([pair_side("ffn2_w_gate"), chip_side("ffn2_w_down")]))
    red["ffn2_w_gate"][1], red["ffn2_w_down"][2] = got
    red["ffn2_w_up"] = [g, None, None]
    dn2, got = _row_matmul([(da2, full["ffn2_w_gate"]), (db2, full["ffn2_w_up"])], "ffn2_bwd_up", False, d,
                           _merge_sides([pair_side("ffn2_w_up"), chip_side("ffn2_w_gate")]), tiles=MM_TILES)
    red["ffn2_w_up"][1], red["ffn2_w_gate"][2] = got
    dh2, dp, dg_ffn2_pre, dg_mix_post = _pre_bwd(dn2, h2, dh3, ffn2_pre_g, "ffn2_pre_bwd", (p, mix_post_g, 1.0))
    dmixed, _ = _col_matmul(dp, full["w_out"], "out_proj_bwd", True, F32)
    g, _ = _wgrad_call(mixed, dp, "w_out_wgrad", x_width=mixed.shape[1] // N_CHIP, tile_y=WGRAD_TILE_Y)
    red["w_out"] = [g, None, None]
    (dzy, dzx, lru_small, dwa2, dwx2), got = _lru_bwd(
        z, hs, dmixed, w4p, lru_conv_b, wa2.astype(BF), lru_b_a, wx2.astype(BF), lru_b_x, lru_lambda, lru_out_g, gm,
        _merge_sides([pair_side("w_out"), chip_side("ffn2_w_up")]))
    red["w_out"][1], red["ffn2_w_up"][2] = got
    (dzb, dzc, dzv, sc_small), got = _sc_bwd(z, dmixed, w3p, sconv_out_g, gm, dl, chip_side("w_out"))
    red["w_out"][2] = got[0]
    dz = jnp.concatenate([dzy, dzx, dzb, dzc, dzv], axis=1)
    p_rest = pack_rest(lru_small, sc_small, dwa2, dwx2)
    g, got = _wgrad_call(u, dz, "w_in_wgrad", y_width=cin, tile_x=WGRAD_TILE_X, side=_sibling_copy_side(p_rest))
    p_rest4 = _small_pair_sum(p_rest, got[0], chip1)
    red["w_in"] = [g, None, None]
    du, got = _row_matmul([(dz, full["w_in"])], "in_proj_bwd", True, d,
                          _merge_sides([pair_side("w_in"), _slot_exchange_side(p_rest4)]))
    red["w_in"][1], p_rest4 = got
    dh1, df1, dg_mix_pre, dg_ffn1_post = _pre_bwd(du, h1, dh2, mix_pre_g, "mix_pre_bwd", (f1, ffn1_post_g, 0.5))
    (da1, db1), got = _ffn_bwd_act(df1, full["ffn1_w_down"], a1, b1, "ffn1_bwd_act", chip_side("w_in"))
    red["w_in"][2] = got[0]
    early = ["ffn2_w_down", "ffn2_w_gate", "ffn2_w_up", "w_out", "w_in"]
    late = ["ffn1_w_down", "ffn1_w_gate", "ffn1_w_up"]
    g, got = _wgrad_call(s1, df1, "ffn1_down_wgrad", x_width=f4p, tile_y=WGRAD_TILE_Y,
                         side=_join_side([final_sum(k) for k in early]))
    gfull = dict(zip(early, got))
    red["ffn1_w_down"] = [g, None, None]
    g, got = _wgrad_call(da1, n1, "ffn1_gate_wgrad", x_width=f4p, tile_y=WGRAD_TILE_Y, side=pair_side("ffn1_w_down"))
    red["ffn1_w_down"][1] = got[0]
    red["ffn1_w_gate"] = [g, None, None]
    g, got = _wgrad_call(db1, n1, "ffn1_up_wgrad", x_width=f4p, tile_y=WGRAD_TILE_Y,
                         side=_merge_sides([pair_side("ffn1_w_gate"), chip_side("ffn1_w_down")]))
    red["ffn1_w_gate"][1], red["ffn1_w_down"][2] = got
    red["ffn1_w_up"] = [g, None, None]
    red["ffn1_w_up"][1] = _run_side(pair_side("ffn1_w_up"), "pair_exchange_ffn1_w_up")[0]
    dn1, got = _row_matmul([(da1, full["ffn1_w_gate"]), (db1, full["ffn1_w_up"])], "ffn1_bwd_up", False, d,
                           _merge_sides([chip_side("ffn1_w_gate"), chip_side("ffn1_w_up")]), tiles=MM_TILES)
    red["ffn1_w_gate"][2], red["ffn1_w_up"][2] = got
    (dh0, dg_ffn1_pre), got = _pre_bwd(dn1, h0, dh1, ffn1_pre_g, "ffn1_pre_bwd",
                                       side=_join_side([final_sum(k) for k in late]))
    gfull.update(zip(late, got))

    grad_x = dh0[N_META:t_real][None]

    w_big = {"ffn1_w_gate": ffn1_w_gate, "ffn1_w_up": ffn1_w_up, "ffn1_w_down": ffn1_w_down, "w_in": w_in, "w_out": w_out,
             "ffn2_w_gate": ffn2_w_gate, "ffn2_w_up": ffn2_w_up, "ffn2_w_down": ffn2_w_down}
    m_big = {"ffn1_w_gate": m_ffn1_w_gate, "ffn1_w_up": m_ffn1_w_up, "ffn1_w_down": m_ffn1_w_down, "w_in": m_w_in,
             "w_out": m_w_out, "ffn2_w_gate": m_ffn2_w_gate, "ffn2_w_up": m_ffn2_w_up, "ffn2_w_down": m_ffn2_w_down}
    v_big = {"ffn1_w_gate": v_ffn1_w_gate, "ffn1_w_up": v_ffn1_w_up, "ffn1_w_down": v_ffn1_w_down, "w_in": v_w_in,
             "w_out": v_w_out, "ffn2_w_gate": v_ffn2_w_gate, "ffn2_w_up": v_ffn2_w_up, "ffn2_w_down": v_ffn2_w_down}
    b_grad, b_delta, b_newm, b_newv = {}, {}, {}, {}

    def big_adamw(k, side=None):
        wv, mv, vv = view(k, w_big[k]), view(k, m_big[k]), view(k, v_big[k])
        wide_rows = wv.shape[0] % 64 == 0
        (g_, d_, m_, v_), got = _adamw(wv, gfull[k], mv, vv, "adamw_" + k, 8 if wide_rows else 4, 1 if wide_rows else 2,
                                       side)
        b_grad[k], b_delta[k], b_newm[k], b_newv[k] = unview(k, g_), unview(k, d_), unview(k, m_), unview(k, v_)
        return got

    p_top = _small_all_reduce(
        pack_top([dg_ffn1_pre, dg_ffn1_post, dg_mix_pre, dg_mix_post, dg_ffn2_pre, dg_ffn2_post], dh0[0:N_META],
                 loss=loss_part[0, 0]), "small_grad_all_reduce")
    p_g, p_delta, p_newm, p_newv = _adamw_small(p_w, p_top, p_rest4, p_m, p_v)
    loss = p_g[R_LOSS, 0]
    for k in names:
        big_adamw(k)

    def unpack(buf):
        out = {}
        for i, k in enumerate(["ffn1_pre_g", "ffn1_post_g", "mix_pre_g", "mix_post_g", "ffn2_pre_g", "ffn2_post_g"]):
            out[k] = buf[R_GAIN + i:R_GAIN + i + 1]
        out["meta_tokens"] = lax.dynamic_slice(buf[R_META:R_META + N_META], (zero, chip * dq), (N_META, dq))
        lru = buf[R_LRU:R_LRU + 16, 0:dl]
        out["lru_conv_w"] = lax.dynamic_slice(lru[0:4], (zero, chip * dlq), (4, dlq))[None]
        out["lru_conv_b"] = lru[4:5]
        out["lru_b_a"] = lru[5:6]
        out["lru_b_x"] = lru[6:7]
        out["lru_lambda"] = lru[7:8]
        out["lru_out_g"] = lru[8:9]
        sc = buf[R_SC:R_SC + 8, 0:dl]
        out["sconv_w"] = lax.dynamic_slice(sc[0:3], (zero, chip * dlq), (3, dlq))[None]
        out["sconv_out_g"] = sc[3:4]
        out["lru_w_a"] = _unpair_blocks(buf[R_WA:R_WX].reshape(N_HEADS // 2, LANE, LANE))[None]
        out["lru_w_x"] = _unpair_blocks(buf[R_WX:R_END].reshape(N_HEADS // 2, LANE, LANE))[None]
        return out

    s_grad, s_delta, s_newm, s_newv = unpack(p_g), unpack(p_delta), unpack(p_newm), unpack(p_newv)

    order = ["meta_tokens", "ffn1_pre_g", "ffn1_w_gate", "ffn1_w_up", "ffn1_w_down", "ffn1_post_g", "mix_pre_g", "w_in",
             "lru_conv_w", "lru_conv_b", "lru_w_a", "lru_b_a", "lru_w_x", "lru_b_x", "lru_lambda", "sconv_w", "lru_out_g",
             "sconv_out_g", "w_out", "mix_post_g", "ffn2_pre_g", "ffn2_w_gate", "ffn2_w_up", "ffn2_w_down", "ffn2_post_g"]

    def pick(small, bigd):
        return [bigd[k] if k in bigd else small[k] for k in order]

    return (loss, grad_x, *pick(s_grad, b_grad), *pick(s_delta, b_delta), *pick(s_newm, b_newm), *pick(s_newv, b_newv))
```

```python
import functools
import math

import jax
import jax.numpy as jnp
from jax import lax
from jax.experimental import pallas as pl
from jax.experimental.pallas import tpu as pltpu

F32 = jnp.float32
BF = jnp.bfloat16
MESH = pl.DeviceIdType.MESH

EPS = 1e-6
N_META = 16
N_HEADS = 16
HEAD = 64
LRU_C = 8.0
LANE = 128
MXU_COLS = 256
N_CHIP = 4
ROW_ALIGN = 384
MM_TILES = 8
MM_TILES_BIG = 4
SWIGLU_TILES = 6
EW_TILES = 12
MIX_CHUNKS = 24
WGRAD_TILE_X = 256
WGRAD_TILE_Y = 512
WIDE_TN = 512
VMEM_LIMIT = 56 << 20

ADAM_LR = 0.001
ADAM_B1 = 0.9
ADAM_B2 = 0.999
ADAM_EPS = 1e-08
ADAM_WD = 0.01
ADAM_STEP = 10


def _round_up(a, b):
    return (a + b - 1) // b * b


def _params(sem=None):
    if sem is None:
        return pltpu.CompilerParams(vmem_limit_bytes=VMEM_LIMIT)
    return pltpu.CompilerParams(dimension_semantics=sem, vmem_limit_bytes=VMEM_LIMIT)


def _sigmoid(x):
    return 0.5 * jnp.tanh(0.5 * x) + 0.5


def _dot(a, b):
    return jnp.dot(a, b, preferred_element_type=F32)


def _dot_nt(a, b):
    return lax.dot_general(a, b, (((1,), (1,)), ((), ())), preferred_element_type=F32)


def _dot_tn(a, b):
    return lax.dot_general(a, b, (((0,), (0,)), ((), ())), preferred_element_type=F32)


def _rms(x, g):
    r = lax.rsqrt(jnp.mean(x * x, axis=-1, keepdims=True) + EPS)
    return x * r * g


def _rms_bwd(x, g, dy):
    r = lax.rsqrt(jnp.mean(x * x, axis=-1, keepdims=True) + EPS)
    xh = x * r
    q = dy * g
    dx = r * (q - xh * jnp.mean(q * xh, axis=-1, keepdims=True))
    return dx, dy * xh


class _Side:
    def __init__(self, ins, outs, alias, sems, start, finish):
        self.ins, self.outs, self.alias, self.sems, self.start, self.finish = ins, outs, alias, sems, start, finish


def _merge_sides(sides):
    sides = [s for s in sides if s is not None]
    if len(sides) <= 1:
        return sides[0] if sides else None
    ins, outs, sems, alias, spans = [], [], [], {}, []
    for s in sides:
        for i, o in s.alias.items():
            alias[len(ins) + i] = len(outs) + o
        spans.append((len(ins), len(ins) + len(s.ins), len(outs), len(outs) + len(s.outs), len(sems),
                      len(sems) + len(s.sems)))
        ins += list(s.ins)
        outs += list(s.outs)
        sems += list(s.sems)

    def run(which):
        def go(in_refs, out_refs, sem_refs):
            for s, (a, b, c, d, e, f) in zip(sides, spans):
                getattr(s, which)(in_refs[a:b], out_refs[c:d], sem_refs[e:f])
        return go

    return _Side(ins, outs, alias, sems, run("start"), run("finish"))


def _grid_call(body, name, grid, in_specs, out_specs, out_shape, args, side=None, scratch=()):
    sem = ("arbitrary",) * len(grid)
    if side is None:
        res = pl.pallas_call(body, name=name, grid=grid, in_specs=in_specs, out_specs=out_specs, out_shape=out_shape,
                             scratch_shapes=list(scratch), compiler_params=_params(sem))(*args)
        return res, []
    nin, nout, sin, sout = len(in_specs), len(out_specs), len(side.ins), len(side.outs)
    nscr = len(scratch)
    staged = hasattr(side, "middle") and math.prod(grid) >= 4
    lin, mid = (math.prod(grid) * 5) // 8, []
    for extent in reversed(grid):
        mid.insert(0, lin % extent)
        lin //= extent

    def full(*refs):
        base_in, side_in = refs[:nin], refs[nin:nin + sin]
        base_out = refs[nin + sin:nin + sin + nout]
        side_out = refs[nin + sin + nout:nin + sin + nout + sout]
        base_scr = refs[nin + sin + nout + sout:nin + sin + nout + sout + nscr]
        sems = refs[nin + sin + nout + sout + nscr:]
        first = pl.program_id(0) == 0
        last = pl.program_id(0) == grid[0] - 1
        for ax in range(1, len(grid)):
            first = first & (pl.program_id(ax) == 0)
            last = last & (pl.program_id(ax) == grid[ax] - 1)

        @pl.when(first)
        def _():
            side.start(side_in, side_out, sems)

        if staged:
            at_mid = pl.program_id(0) == mid[0]
            for ax in range(1, len(grid)):
                at_mid = at_mid & (pl.program_id(ax) == mid[ax])

            @pl.when(at_mid)
            def _():
                side.middle(side_in, side_out, sems)

        body(*base_in, *base_out, *base_scr)

        @pl.when(last)
        def _():
            (side.rest if staged else side.finish)(side_in, side_out, sems)

    any_spec = pl.BlockSpec(memory_space=pl.ANY)
    res = pl.pallas_call(
        full, name=name, grid=grid, in_specs=list(in_specs) + [any_spec] * sin,
        out_specs=list(out_specs) + [any_spec] * sout, out_shape=list(out_shape) + list(side.outs),
        scratch_shapes=list(scratch) + list(side.sems),
        input_output_aliases={nin + i: nout + o for i, o in side.alias.items()},
        compiler_params=_params(sem))(*args, *side.ins)
    return res[:nout], res[nout:]


def _ffn_up(n, wg, wu, name, side=None, tiles=SWIGLU_TILES):
    tp, d = n.shape
    fp = wg.shape[1]
    tm = tp // tiles

    def body(n_ref, wg_ref, wu_ref, a_ref, b_ref, s_ref):
        nn = n_ref[...]
        for c0 in range(0, fp, MXU_COLS):
            cs = slice(c0, min(c0 + MXU_COLS, fp))
            a = _dot_nt(nn, wg_ref[cs, :])
            b = _dot_nt(nn, wu_ref[cs, :])
            a_ref[:, cs] = a.astype(BF)
            b_ref[:, cs] = b.astype(BF)
            s_ref[:, cs] = (a * _sigmoid(a) * b).astype(BF)

    out = jax.ShapeDtypeStruct((tp, N_CHIP * fp), BF)
    wspec = pl.BlockSpec((None, fp, d), lambda k, i: (k, 0, 0))
    ospec = pl.BlockSpec((tm, fp), lambda k, i: (i, k))
    return _grid_call(body, name, (N_CHIP, tiles), [pl.BlockSpec((tm, d), lambda k, i: (i, 0)), wspec, wspec],
                      [ospec, ospec, ospec], [out, out, out], (n, wg, wu), side)


def _ffn_up_head(n, wg, wu, name, side):
    tp, d = n.shape
    fp = wg.shape[1]
    tiles = SWIGLU_TILES
    tm = tp // tiles
    gat = _gather_side([wg, wu], relative=True, two_path=True)
    sin, sout, ngs = len(side.ins), len(side.outs), len(gat.sems)
    order = (0,) + REL_SLOT
    staged = hasattr(side, "middle")

    def body(*refs):
        n_ref = refs[0]
        si = refs[3:3 + sin]
        a_ref, b_ref, s_ref = refs[3 + sin:6 + sin]
        go = refs[6 + sin:8 + sin]
        so = refs[8 + sin:8 + sin + sout]
        wbg, wbu, wsem = refs[8 + sin + sout:11 + sin + sout]
        gsems = refs[11 + sin + sout:11 + sin + sout + ngs]
        ssems = refs[11 + sin + sout + ngs:]
        k, i = pl.program_id(0), pl.program_id(1)
        cur = k % 2

        def to_vmem(slot, buf):
            return [pltpu.make_async_copy(go[0].at[slot], wbg.at[buf], wsem.at[buf, 0]),
                    pltpu.make_async_copy(go[1].at[slot], wbu.at[buf], wsem.at[buf, 1])]

        @pl.when((k == 0) & (i == 0))
        def _():
            gat.send(go, gsems)
            if not staged:
                side.start(si, so, ssems)
            for cp in to_vmem(0, 0):
                cp.start()
            for cp in to_vmem(0, 0):
                cp.wait()

        for j in range(3):
            @pl.when((k == j) & (i == tiles // 2))
            def _():
                gat.arrived(j, go, gsems)
                if staged and j == 1:
                    side.start(si, so, ssems)

            @pl.when((k == j) & (i == tiles - 2))
            def _():
                gat.forwarded(j, go, gsems)
                for cp in to_vmem(order[j + 1], (j + 1) % 2):
                    cp.start()

            @pl.when((k == j + 1) & (i == 0))
            def _():
                for cp in to_vmem(order[j + 1], (j + 1) % 2):
                    cp.wait()

        nn = n_ref[...]
        for c0 in range(0, fp, MXU_COLS):
            cs = pl.ds(c0, min(MXU_COLS, fp - c0))
            a = _dot_nt(nn, wbg[cur, cs, :])
            b = _dot_nt(nn, wbu[cur, cs, :])
            a_ref[:, cs] = a.astype(BF)
            b_ref[:, cs] = b.astype(BF)
            s_ref[:, cs] = (a * _sigmoid(a) * b).astype(BF)

        if staged:
            @pl.when((k == N_CHIP - 1) & (i == tiles // 4))
            def _():
                side.middle(si, so, ssems)

        @pl.when((k == N_CHIP - 1) & (i == tiles - 1))
        def _():
            gat.drain(go, gsems)
            if staged:
                side.rest(si, so, ssems)
            else:
                side.finish(si, so, ssems)

    out = jax.ShapeDtypeStruct((tp, N_CHIP * fp), BF)
    any_spec = pl.BlockSpec(memory_space=pl.ANY)
    slot_of = lambda k: (k % 2) * 2 + k // 2
    ospec = pl.BlockSpec((tm, fp), lambda k, i: (i, slot_of(k)))
    wbuf = pltpu.VMEM((2, fp, d), BF)
    res = pl.pallas_call(
        body, name=name, grid=(N_CHIP, tiles),
        in_specs=[pl.BlockSpec((tm, d), lambda k, i: (i, 0))] + [any_spec] * (2 + sin),
        out_specs=[ospec, ospec, ospec] + [any_spec] * (2 + sout),
        out_shape=[out, out, out] + list(gat.outs) + list(side.outs),
        scratch_shapes=[wbuf, wbuf, pltpu.SemaphoreType.DMA((2, 2))] + list(gat.sems) + list(side.sems),
        input_output_aliases={1: 3, 2: 4, **{3 + a: 5 + b for a, b in side.alias.items()}},
        compiler_params=_params(("arbitrary", "arbitrary")))(n, wg, wu, *side.ins)
    return res[:3], res[3:5], res[5:]


def _ffn_bwd_act(df, wd, a, b, name, side=None, tiles=SWIGLU_TILES):
    tp, d = df.shape
    fp = wd.shape[1]
    tm = tp // tiles

    def body(df_ref, wd_ref, a_ref, b_ref, da_ref, db_ref):
        dfv = df_ref[...]
        for c0 in range(0, fp, MXU_COLS):
            cs = slice(c0, min(c0 + MXU_COLS, fp))
            ds = _dot_nt(dfv, wd_ref[cs, :])
            av = a_ref[:, cs].astype(F32)
            bv = b_ref[:, cs].astype(F32)
            sg = _sigmoid(av)
            da_ref[:, cs] = (ds * bv * sg * (1.0 + av * (1.0 - sg))).astype(BF)
            db_ref[:, cs] = (ds * av * sg).astype(BF)

    out = jax.ShapeDtypeStruct((tp, N_CHIP * fp), BF)
    aspec = pl.BlockSpec((tm, fp), lambda k, i: (i, k))
    return _grid_call(
        body, name, (N_CHIP, tiles),
        [pl.BlockSpec((tm, d), lambda k, i: (i, 0)), pl.BlockSpec((None, fp, d), lambda k, i: (k, 0, 0)), aspec, aspec],
        [aspec, aspec], [out, out], (df, wd, a, b), side)


def _col_matmul(lhs, w, name, trans_b, out_dtype, side=None, tiles=MM_TILES_BIG):
    tp, kd = lhs.shape
    nk = w.shape[0]
    nc = w.shape[1] if trans_b else w.shape[2]
    tm = tp // tiles

    def body(l_ref, w_ref, o_ref):
        if trans_b:
            o_ref[...] = _dot_nt(l_ref[...], w_ref[...]).astype(out_dtype)
        else:
            o_ref[...] = _dot(l_ref[...], w_ref[...]).astype(out_dtype)

    res, extra = _grid_call(
        body, name, (nk, tiles),
        [pl.BlockSpec((tm, kd), lambda k, i: (i, 0)),
         pl.BlockSpec((None,) + tuple(w.shape[1:]), lambda k, i: (k, 0, 0), pipeline_mode=pl.Buffered(1))],
        [pl.BlockSpec((tm, nc), lambda k, i: (i, k))], [jax.ShapeDtypeStruct((tp, nk * nc), out_dtype)], (lhs, w), side)
    return res[0], extra


def _row_matmul(pairs, name, trans_b, d_out, side=None, tiles=MM_TILES_BIG):
    l0 = pairs[0][0]
    tp = l0.shape[1] if l0.ndim == 3 else l0.shape[0]
    nk = pairs[0][1].shape[0]
    tm = tp // tiles
    npair = len(pairs)

    def body(*refs):
        o_ref = refs[2 * npair]
        k = pl.program_id(1)
        part = None
        for q in range(npair):
            l = refs[2 * q][...]
            w = refs[2 * q + 1][...]
            t = _dot_nt(l, w) if trans_b else _dot(l, w)
            part = t if part is None else part + t

        @pl.when(k == 0)
        def _():
            o_ref[...] = part

        @pl.when(k > 0)
        def _():
            o_ref[...] += part

    in_specs, args = [], []
    for lhs, w in pairs:
        if lhs.ndim == 3:
            in_specs.append(pl.BlockSpec((None, tm, lhs.shape[2]), lambda i, k: (k, i, 0)))
        else:
            in_specs.append(pl.BlockSpec((tm, lhs.shape[1] // nk), lambda i, k: (i, k)))
        in_specs.append(pl.BlockSpec((None,) + tuple(w.shape[1:]), lambda i, k: (k, 0, 0)))
        args += [lhs, w]
    res, extra = _grid_call(body, name, (tiles, nk), in_specs, [pl.BlockSpec((tm, d_out), lambda i, k: (i, 0))],
                            [jax.ShapeDtypeStruct((tp, d_out), F32)], args, side)
    return res[0], extra


def _wide_matmul(pairs, name, tn, side=None):
    tp = pairs[0][0].shape[0]
    d_out = pairs[0][1].shape[2]
    tm = tp // MM_TILES
    npair = len(pairs)

    def body(*refs):
        acc = None
        for q in range(npair):
            t = _dot(refs[2 * q][...], refs[2 * q + 1][...])
            acc = t if acc is None else acc + t
        refs[2 * npair][...] = acc

    in_specs, args = [], []
    for lhs, w in pairs:
        kdim = lhs.shape[1]
        in_specs += [pl.BlockSpec((tm, kdim), lambda n, i: (i, 0)), pl.BlockSpec((kdim, tn), lambda n, i: (0, n))]
        args += [lhs, w.reshape(kdim, d_out)]
    res, extra = _grid_call(body, name, (d_out // tn, MM_TILES), in_specs, [pl.BlockSpec((tm, tn), lambda n, i: (i, n))],
                            [jax.ShapeDtypeStruct((tp, d_out), F32)], args, side)
    return res[0], extra


def _wgrad_call(x, y, name, x_width=None, y_width=None, tile_x=None, tile_y=None, side=None):
    tp = x.shape[1] if x.ndim == 3 else x.shape[0]

    def spec(a, width, tile):
        cols = a.shape[2] if a.ndim == 3 else (a.shape[1] if width is None else width)
        tc = cols if tile is None else tile
        per = cols // tc
        if a.ndim == 3:
            return pl.BlockSpec((None, tp, tc), lambda k, t: (k, 0, t if tile else 0)), cols, per
        if width is None:
            return pl.BlockSpec((tp, tc), lambda k, t: (0, t if tile else 0)), cols, per
        return pl.BlockSpec((tp, tc), lambda k, t: (0, k * per + (t if tile else 0))), cols, per

    xs, p, nx = spec(x, x_width, tile_x)
    ys, q, ny = spec(y, y_width, tile_y)
    nt = nx * ny
    if tile_x:
        ospec = pl.BlockSpec((None, tile_x, q), lambda k, t: (k, t, 0))
    else:
        ospec = pl.BlockSpec((None, p, tile_y), lambda k, t: (k, 0, t))

    def body(x_ref, y_ref, o_ref):
        o_ref[...] = _dot_tn(x_ref[...], y_ref[...]).astype(BF)

    res, extra = _grid_call(body, name, (N_CHIP, nt), [xs, ys], [ospec], [jax.ShapeDtypeStruct((N_CHIP, p, q), BF)],
                            (x, y), side)
    return res[0], extra


def _row_call(body, name, tp, d, row_ins, vec_ins, row_out_dtypes, n_acc, side=None):
    te = tp // EW_TILES
    rspec = pl.BlockSpec((te, d), lambda i: (i, 0))
    vspec = pl.BlockSpec((1, d), lambda i: (0, 0))
    res, extra = _grid_call(
        body, name, (EW_TILES,), [rspec] * len(row_ins) + [vspec] * len(vec_ins),
        [rspec] * len(row_out_dtypes) + [vspec] * n_acc,
        [jax.ShapeDtypeStruct((tp, d), dt) for dt in row_out_dtypes] + [jax.ShapeDtypeStruct((1, d), F32)] * n_acc,
        (*row_ins, *vec_ins), side)
    return res if side is None else (res, extra)


def _norm0(h, g):
    tp, d = h.shape

    def body(h_ref, g_ref, n_ref):
        n_ref[...] = _rms(h_ref[...], g_ref[...]).astype(BF)

    return _row_call(body, "norm0", tp, d, [h], [g], [BF], 0)[0]


def _post_fwd(f, h, g_post, g_next, scale, name):
    tp, d = h.shape

    def body(f_ref, h_ref, gp_ref, gn_ref, hn_ref, n_ref):
        hn = h_ref[...] + scale * _rms(f_ref[...], gp_ref[...])
        hn_ref[...] = hn
        n_ref[...] = _rms(hn, gn_ref[...]).astype(BF)

    return _row_call(body, name, tp, d, [f, h], [g_post, g_next], [F32, BF], 0)


def _loss_bwd(f, h, tgt, g_post, t_real):
    tp, d = h.shape
    te = tp // EW_TILES

    def body(f_ref, h_ref, t_ref, gp_ref, dh_ref, df_ref, dg_ref, loss_ref):
        i = pl.program_id(0)

        @pl.when(i == 0)
        def _():
            dg_ref[...] = jnp.zeros_like(dg_ref)
            loss_ref[...] = jnp.zeros_like(loss_ref)

        f = f_ref[...]
        gp = gp_ref[...]
        h3 = h_ref[...] + 0.5 * _rms(f, gp)
        rows = i * te + lax.broadcasted_iota(jnp.int32, (te, 1), 0)
        real = (rows >= N_META) & (rows < t_real)
        e = jnp.where(real, h3 - t_ref[...], 0.0)
        loss_ref[...] += 0.5 * jnp.sum(jnp.sum(e * e, axis=1, keepdims=True), axis=0, keepdims=True) / d
        dh = e / d
        dh_ref[...] = dh
        dfv, dgr = _rms_bwd(f, gp, 0.5 * dh)
        df_ref[...] = dfv.astype(BF)
        dg_ref[...] += jnp.sum(dgr, axis=0, keepdims=True)

    rspec = pl.BlockSpec((te, d), lambda i: (i, 0))
    vspec = pl.BlockSpec((1, d), lambda i: (0, 0))
    return pl.pallas_call(
        body, name="loss_bwd", grid=(EW_TILES,),
        in_specs=[rspec, rspec, rspec, vspec],
        out_specs=[rspec, rspec, vspec, pl.BlockSpec((1, 1), lambda i: (0, 0))],
        out_shape=[jax.ShapeDtypeStruct((tp, d), F32), jax.ShapeDtypeStruct((tp, d), BF),
                   jax.ShapeDtypeStruct((1, d), F32), jax.ShapeDtypeStruct((1, 1), F32)],
        compiler_params=_params(("arbitrary",)),
    )(f, h, tgt, g_post)


def _pre_bwd(dn, h, dh_out, g_pre, name, chain=None, side=None):
    tp, d = h.shape

    def body(*refs):
        if chain is None:
            dn_ref, h_ref, dho_ref, g_ref, dh_ref, dg_ref = refs
        else:
            dn_ref, h_ref, dho_ref, p_ref, g_ref, gp_ref, dh_ref, dp_ref, dg_ref, dgp_ref = refs
        i = pl.program_id(0)

        @pl.when(i == 0)
        def _():
            dg_ref[...] = jnp.zeros_like(dg_ref)
            if chain is not None:
                dgp_ref[...] = jnp.zeros_like(dgp_ref)

        dx, dgr = _rms_bwd(h_ref[...], g_ref[...], dn_ref[...])
        dh = dho_ref[...] + dx
        dh_ref[...] = dh
        dg_ref[...] += jnp.sum(dgr, axis=0, keepdims=True)
        if chain is not None:
            dp, dgpr = _rms_bwd(p_ref[...], gp_ref[...], chain[2] * dh)
            dp_ref[...] = dp.astype(BF)
            dgp_ref[...] += jnp.sum(dgpr, axis=0, keepdims=True)

    if chain is None:
        return _row_call(body, name, tp, d, [dn, h, dh_out], [g_pre], [F32], 1, side)
    return _row_call(body, name, tp, d, [dn, h, dh_out, chain[0]], [g_pre, chain[1]], [F32, BF], 2, side)


def _gelu(y):
    c = math.sqrt(2.0 / math.pi)
    return 0.5 * y * (1.0 + jnp.tanh(c * (y + 0.044715 * y * y * y)))


def _gelu_and_grad(y):
    c = math.sqrt(2.0 / math.pi)
    y2 = y * y
    t = jnp.tanh(c * y * (1.0 + 0.044715 * y2))
    half = 0.5 * (1.0 + t)
    return y * half, half + 0.5 * y * (1.0 - t * t) * c * (1.0 + 3.0 * 0.044715 * y2)


def _neg_expm1(x):
    p = 1.0 + x * (1.0 / 9.0)
    for n in (8.0, 7.0, 6.0, 5.0, 4.0, 3.0, 2.0):
        p = 1.0 + x * (1.0 / n) * p
    return -jnp.where(x > -0.35, x * p, jnp.exp(x) - 1.0)


def _softplus(x):
    e = jnp.exp(-jnp.abs(x))
    w = 1.0 + e
    l1p = jnp.where(w == 1.0, e, jnp.log(w) * (e / jnp.where(w == 1.0, 1.0, w - 1.0)))
    return jnp.maximum(x, 0.0) + l1p


def _group_mean(v, gm):
    hi = v.astype(BF)
    lo = (v - hi.astype(F32)).astype(BF)
    return _dot(hi, gm) + _dot(lo, gm)


def _shift_dn(win, s, r):
    if s == 0:
        return win[8:8 + r]
    return pltpu.roll(win, s, 0)[8:8 + r]


def _shift_up(win, s, r):
    if s == 0:
        return win[0:r]
    return pltpu.roll(win, r + 8 - s, 0)[0:r]


def _window_dn(ref, t0, r, first):
    if first:
        return jnp.concatenate([jnp.zeros((8, ref.shape[1]), F32), ref[0:r, :]], axis=0)
    return ref[pl.ds(t0 - 8, r + 8), :]


def _tile_scan(a, u, reverse):
    r = a.shape[0]
    rid = lax.broadcasted_iota(jnp.int32, a.shape, 0) & 7
    for dlt in (1, 2, 4):
        sh = (r - dlt) if reverse else dlt
        a_s = pltpu.roll(a, sh, 0)
        u_s = pltpu.roll(u, sh, 0)
        keep = (rid + dlt <= 7) if reverse else (rid >= dlt)
        u = jnp.where(keep, u + a * u_s, u)
        a = jnp.where(keep, a * a_s, a)
    return a, u


def _lru_gates(xc, wa, ba, wx, bx, sp):
    xb = xc.astype(BF)
    ga = _sigmoid(_dot(xb, wa) + ba)
    gx = _sigmoid(_dot(xb, wx) + bx)
    la = -LRU_C * ga * sp
    return ga, gx, la


def _conv4(win, w4, cb, r):
    return (cb + w4[3:4] * _shift_dn(win, 0, r) + w4[2:3] * _shift_dn(win, 1, r)
            + w4[1:2] * _shift_dn(win, 2, r) + w4[0:1] * _shift_dn(win, 3, r))


def _lru_fwd(z, w4, cb, wa2, ba, wx2, bx, lam, g_out, gm, side=None):
    tp = z.shape[0]
    dl = cb.shape[1]
    nb = dl // LANE
    r = tp // MIX_CHUNKS
    c = LANE

    def body(y_ref, x_ref, w4_ref, cb_ref, wa_ref, ba_ref, wx_ref, bx_ref, lam_ref, go_ref, gm_ref, m_ref, hs_ref):
        w4v = w4_ref[...]
        cbv = cb_ref[...]
        wa = wa_ref[...]
        wx = wx_ref[...]
        bav = ba_ref[...]
        bxv = bx_ref[...]
        gov = go_ref[...]
        gmv = gm_ref[...]
        sp = _softplus(-lam_ref[...])

        def chunk(t0, hprev, first):
            win = _window_dn(x_ref, t0, r, first)
            xc = _conv4(win, w4v, cbv, r)
            ga, gx, la = _lru_gates(xc, wa, bav, wx, bxv, sp)
            a = jnp.exp(la)
            u = jnp.sqrt(_neg_expm1(2.0 * la)) * gx * xc
            ac, uc = _tile_scan(a, u, False)
            for j in range(r // 8):
                hj = uc[8 * j:8 * j + 8] + ac[8 * j:8 * j + 8] * hprev
                hs_ref[pl.ds(t0 + 8 * j, 8), :] = hj
                hprev = jnp.broadcast_to(hj[7:8], (8, c))
            h = hs_ref[pl.ds(t0, r), :]
            lo = h * _gelu(y_ref[pl.ds(t0, r), :])
            rs = lax.rsqrt(_group_mean(lo * lo, gmv) + EPS)
            m_ref[pl.ds(t0, r), :] = (lo * rs * gov).astype(BF)
            return hprev

        hp = chunk(0, jnp.zeros((8, c), F32), True)

        def loop(ci, hp):
            return chunk(pl.multiple_of(ci * r, 16), hp, False)

        lax.fori_loop(1, MIX_CHUNKS, loop, hp)

    col = lambda off: pl.BlockSpec((tp, c), lambda j: (0, off + j))
    vec = pl.BlockSpec((1, c), lambda j: (0, j))
    return _grid_call(
        body, "lru_fwd", (nb,),
        [col(0), col(nb), pl.BlockSpec((8, c), lambda j: (0, j)), vec, pl.BlockSpec((None, c, c), lambda j: (j, 0, 0)),
         vec, pl.BlockSpec((None, c, c), lambda j: (j, 0, 0)), vec, vec, vec, pl.BlockSpec((c, c), lambda j: (0, 0))],
        [col(0), col(0)], [jax.ShapeDtypeStruct((tp, dl), BF), jax.ShapeDtypeStruct((tp, dl), F32)],
        (z, z, w4, cb, wa2, ba, wx2, bx, lam, g_out, gm), side)


def _lru_bwd(z, hs, dmix, w4, cb, wa2, ba, wx2, bx, lam, g_out, gm, side=None):
    tp = z.shape[0]
    dl = cb.shape[1]
    nb = dl // LANE
    r = tp // MIX_CHUNKS
    c = LANE

    def body(y_ref, x_ref, hs_ref, dm_ref, w4_ref, cb_ref, wa_ref, ba_ref, wx_ref, bx_ref, lam_ref, go_ref, gm_ref,
             dy_ref, dx_ref, small_ref, dwa_ref, dwx_ref, xc_buf, ga_buf, gx_buf, a_buf, dh_buf, dxc_buf):
        w4v = w4_ref[...]
        cbv = cb_ref[...]
        wa = wa_ref[...]
        wx = wx_ref[...]
        bav = ba_ref[...]
        bxv = bx_ref[...]
        gov = go_ref[...]
        gmv = gm_ref[...]
        lamv = lam_ref[...]
        sp = _softplus(-lamv)
        small_ref[...] = jnp.zeros_like(small_ref)
        dwa_ref[...] = jnp.zeros_like(dwa_ref)
        dwx_ref[...] = jnp.zeros_like(dwx_ref)
        a_buf[pl.ds(tp, 8), :] = jnp.zeros((8, c), F32)
        dxc_buf[pl.ds(tp, 8), :] = jnp.zeros((8, c), F32)

        def fwd_chunk(t0, first):
            win = _window_dn(x_ref, t0, r, first)
            xc = _conv4(win, w4v, cbv, r)
            ga, gx, la = _lru_gates(xc, wa, bav, wx, bxv, sp)
            xc_buf[pl.ds(t0, r), :] = xc
            ga_buf[pl.ds(t0, r), :] = ga
            gx_buf[pl.ds(t0, r), :] = gx
            a_buf[pl.ds(t0, r), :] = jnp.exp(la)
            h = hs_ref[pl.ds(t0, r), :]
            yv = y_ref[pl.ds(t0, r), :]
            ge, dge = _gelu_and_grad(yv)
            lo = h * ge
            rs = lax.rsqrt(_group_mean(lo * lo, gmv) + EPS)
            xh = lo * rs
            dm = dm_ref[pl.ds(t0, r), :]
            q = dm * gov
            dlo = rs * (q - xh * _group_mean(q * xh, gmv))
            small_ref[8:9, :] += jnp.sum(dm * xh, axis=0, keepdims=True)
            dh_buf[pl.ds(t0, r), :] = dlo * ge
            dy_ref[pl.ds(t0, r), :] = (dlo * h * dge).astype(BF)

        fwd_chunk(0, True)

        def floop(ci, carry):
            fwd_chunk(pl.multiple_of(ci * r, 16), False)
            return carry

        lax.fori_loop(1, MIX_CHUNKS, floop, 0)

        def bwd_chunk(t0, vnext, first):
            ap = _shift_up(a_buf[pl.ds(t0, r + 8), :], 1, r)
            ac, uc = _tile_scan(ap, dh_buf[pl.ds(t0, r), :], True)
            for j in reversed(range(r // 8)):
                vj = uc[8 * j:8 * j + 8] + ac[8 * j:8 * j + 8] * vnext
                dh_buf[pl.ds(t0 + 8 * j, 8), :] = vj
                vnext = jnp.broadcast_to(vj[0:1], (8, c))
            v = dh_buf[pl.ds(t0, r), :]
            hprev = _shift_dn(_window_dn(hs_ref, t0, r, first), 1, r)
            xc = xc_buf[pl.ds(t0, r), :]
            ga = ga_buf[pl.ds(t0, r), :]
            gx = gx_buf[pl.ds(t0, r), :]
            a = a_buf[pl.ds(t0, r), :]
            em = _neg_expm1(-2.0 * LRU_C * ga * sp)
            mult = jnp.sqrt(em)
            dla = v * hprev * a - (v * gx * xc) * ((1.0 - em) / mult)
            dgx = v * mult * xc
            dxc = v * mult * gx
            dga = dla * (-LRU_C) * sp
            small_ref[7:8, :] += jnp.sum(dla * (-LRU_C) * ga, axis=0, keepdims=True)
            dpa = dga * ga * (1.0 - ga)
            dpx = dgx * gx * (1.0 - gx)
            small_ref[5:6, :] += jnp.sum(dpa, axis=0, keepdims=True)
            small_ref[6:7, :] += jnp.sum(dpx, axis=0, keepdims=True)
            dpab = dpa.astype(BF)
            dpxb = dpx.astype(BF)
            xb = xc.astype(BF)
            dxc = dxc + _dot_nt(dpab, wa) + _dot_nt(dpxb, wx)
            dwa_ref[...] += _dot_tn(xb, dpab)
            dwx_ref[...] += _dot_tn(xb, dpxb)
            dxc_buf[pl.ds(t0, r), :] = dxc
            small_ref[4:5, :] += jnp.sum(dxc, axis=0, keepdims=True)
            dwin = dxc_buf[pl.ds(t0, r + 8), :]
            dx_ref[pl.ds(t0, r), :] = (w4v[3:4] * dxc + w4v[2:3] * _shift_up(dwin, 1, r)
                                       + w4v[1:2] * _shift_up(dwin, 2, r) + w4v[0:1] * _shift_up(dwin, 3, r)).astype(BF)
            xwin = _window_dn(x_ref, t0, r, first)
            for k in range(4):
                small_ref[k:k + 1, :] += jnp.sum(dxc * _shift_dn(xwin, 3 - k, r), axis=0, keepdims=True)
            return vnext

        def bloop(it, vnext):
            ci = MIX_CHUNKS - 1 - it
            return bwd_chunk(pl.multiple_of(ci * r, 16), vnext, False)

        vn = lax.fori_loop(0, MIX_CHUNKS - 1, bloop, jnp.zeros((8, c), F32))
        bwd_chunk(0, vn, True)
        small_ref[7:8, :] = small_ref[7:8, :] * (-_sigmoid(-lamv))

    col = lambda off: pl.BlockSpec((tp, c), lambda j: (0, off + j))
    vec = pl.BlockSpec((1, c), lambda j: (0, j))
    mat = pl.BlockSpec((None, c, c), lambda j: (j, 0, 0))
    buf = pltpu.VMEM((tp, c), F32)
    bufp = pltpu.VMEM((tp + 8, c), F32)
    return _grid_call(
        body, "lru_bwd", (nb,),
        [col(0), col(nb), col(0), col(0), pl.BlockSpec((8, c), lambda j: (0, j)), vec, mat, vec, mat, vec, vec, vec,
         pl.BlockSpec((c, c), lambda j: (0, 0))],
        [col(0), col(0), pl.BlockSpec((16, c), lambda j: (0, j)), mat, mat],
        [jax.ShapeDtypeStruct((tp, dl), BF), jax.ShapeDtypeStruct((tp, dl), BF), jax.ShapeDtypeStruct((16, dl), F32),
         jax.ShapeDtypeStruct((nb, c, c), F32), jax.ShapeDtypeStruct((nb, c, c), F32)],
        (z, z, hs, dmix, w4, cb, wa2, ba, wx2, bx, lam, g_out, gm), side, [buf, buf, buf, bufp, buf, bufp])


def _sc_conv(cvwin, w3, r):
    return w3[2:3] * _shift_dn(cvwin, 0, r) + w3[1:2] * _shift_dn(cvwin, 1, r) + w3[0:1] * _shift_dn(cvwin, 2, r)


def _sc_fwd(z, w3, g_out, gm, dl, side=None):
    tp = z.shape[0]
    nb = dl // LANE
    r = tp // MIX_CHUNKS
    c = LANE

    def body(b_ref, c_ref, v_ref, w3_ref, go_ref, gm_ref, m_ref):
        w3v = w3_ref[...]
        gov = go_ref[...]
        gmv = gm_ref[...]

        def chunk(t0, first):
            cvwin = _window_dn(c_ref, t0, r, first) * _window_dn(v_ref, t0, r, first)
            so = b_ref[pl.ds(t0, r), :] * _sc_conv(cvwin, w3v, r)
            rs = lax.rsqrt(_group_mean(so * so, gmv) + EPS)
            m_ref[pl.ds(t0, r), :] = (so * rs * gov).astype(BF)

        chunk(0, True)

        def loop(ci, carry):
            chunk(pl.multiple_of(ci * r, 16), False)
            return carry

        lax.fori_loop(1, MIX_CHUNKS, loop, 0)

    col = lambda off: pl.BlockSpec((tp, c), lambda j: (0, off + j))
    res, extra = _grid_call(
        body, "sconv_fwd", (nb,),
        [col(2 * nb), col(3 * nb), col(4 * nb), pl.BlockSpec((8, c), lambda j: (0, j)),
         pl.BlockSpec((1, c), lambda j: (0, j)), pl.BlockSpec((c, c), lambda j: (0, 0))],
        [col(0)], [jax.ShapeDtypeStruct((tp, dl), BF)], (z, z, z, w3, g_out, gm), side)
    return res[0], extra


def _sc_bwd(z, dmix, w3, g_out, gm, dl, side=None):
    tp = z.shape[0]
    nb = dl // LANE
    r = tp // MIX_CHUNKS
    c = LANE

    def body(b_ref, c_ref, v_ref, dm_ref, w3_ref, go_ref, gm_ref, db_ref, dc_ref, dv_ref, small_ref, dsc_buf):
        w3v = w3_ref[...]
        gov = go_ref[...]
        gmv = gm_ref[...]
        small_ref[...] = jnp.zeros_like(small_ref)
        dsc_buf[pl.ds(tp, 8), :] = jnp.zeros((8, c), F32)

        def chunk1(t0, first):
            cvwin = _window_dn(c_ref, t0, r, first) * _window_dn(v_ref, t0, r, first)
            sc = _sc_conv(cvwin, w3v, r)
            bv = b_ref[pl.ds(t0, r), :]
            so = bv * sc
            rs = lax.rsqrt(_group_mean(so * so, gmv) + EPS)
            xh = so * rs
            dm = dm_ref[pl.ds(t0, r), :]
            q = dm * gov
            dso = rs * (q - xh * _group_mean(q * xh, gmv))
            small_ref[3:4, :] += jnp.sum(dm * xh, axis=0, keepdims=True)
            db_ref[pl.ds(t0, r), :] = (dso * sc).astype(BF)
            dsc = dso * bv
            dsc_buf[pl.ds(t0, r), :] = dsc
            for k in range(3):
                small_ref[k:k + 1, :] += jnp.sum(dsc * _shift_dn(cvwin, 2 - k, r), axis=0, keepdims=True)

        chunk1(0, True)

        def loop1(ci, carry):
            chunk1(pl.multiple_of(ci * r, 16), False)
            return carry

        lax.fori_loop(1, MIX_CHUNKS, loop1, 0)

        def loop2(ci, carry):
            t0 = pl.multiple_of(ci * r, 16)
            dwin = dsc_buf[pl.ds(t0, r + 8), :]
            dcv = w3v[2:3] * _shift_up(dwin, 0, r) + w3v[1:2] * _shift_up(dwin, 1, r) + w3v[0:1] * _shift_up(dwin, 2, r)
            dc_ref[pl.ds(t0, r), :] = (dcv * v_ref[pl.ds(t0, r), :]).astype(BF)
            dv_ref[pl.ds(t0, r), :] = (dcv * c_ref[pl.ds(t0, r), :]).astype(BF)
            return carry

        lax.fori_loop(0, MIX_CHUNKS, loop2, 0)

    col = lambda off: pl.BlockSpec((tp, c), lambda j: (0, off + j))
    out = jax.ShapeDtypeStruct((tp, dl), BF)
    return _grid_call(
        body, "sconv_bwd", (nb,),
        [col(2 * nb), col(3 * nb), col(4 * nb), col(nb), pl.BlockSpec((8, c), lambda j: (0, j)),
         pl.BlockSpec((1, c), lambda j: (0, j)), pl.BlockSpec((c, c), lambda j: (0, 0))],
        [col(0), col(0), col(0), pl.BlockSpec((8, c), lambda j: (0, j))],
        [out, out, out, jax.ShapeDtypeStruct((8, dl), F32)], (z, z, z, dmix, w3, g_out, gm), side,
        [pltpu.VMEM((tp + 8, c), F32)])


def _cast_pad(w, rows_p, cols_p, chip, name):
    r, c = w.shape

    def body(chip_ref, w_ref, o_ref):
        if (rows_p, cols_p) != (r, c):
            o_ref[...] = jnp.zeros_like(o_ref)
        o_ref[0:r, 0:c] = w_ref[...].astype(BF)

    return pl.pallas_call(
        body, name=name, out_shape=jax.ShapeDtypeStruct((N_CHIP, rows_p, cols_p), BF),
        grid_spec=pltpu.PrefetchScalarGridSpec(
            num_scalar_prefetch=1, grid=(1,),
            in_specs=[pl.BlockSpec((r, c), lambda i, chip: (0, 0))],
            out_specs=pl.BlockSpec((None, rows_p, cols_p), lambda i, chip: (chip[0], 0, 0))),
        compiler_params=_params(("arbitrary",)),
    )(chip, w)


def _adamw_math(w, g, m, v):
    m2 = ADAM_B1 * m + (1.0 - ADAM_B1) * g
    v2 = ADAM_B2 * v + (1.0 - ADAM_B2) * (g * g)
    m_hat = m2 / (1.0 - ADAM_B1 ** ADAM_STEP)
    v_hat = v2 / (1.0 - ADAM_B2 ** ADAM_STEP)
    delta = -ADAM_LR * (m_hat / (jnp.sqrt(v_hat) + ADAM_EPS) + ADAM_WD * w)
    return delta, m2, v2


def _adamw(w, g, m, v, name, row_tiles, col_tiles, side=None):
    r, c = w.shape
    tr = r // row_tiles
    tc = c // col_tiles
    gc = g.shape[1] if col_tiles == 1 else tc

    def body(w_ref, g_ref, m_ref, v_ref, go_ref, d_ref, mo_ref, vo_ref):
        gv = g_ref[...][:, 0:tc]
        delta, m2, v2 = _adamw_math(w_ref[...], gv, m_ref[...], v_ref[...])
        go_ref[...] = gv
        d_ref[...] = delta
        mo_ref[...] = m2
        vo_ref[...] = v2

    spec = pl.BlockSpec((tr, tc), lambda i, j: (i, j))
    out = jax.ShapeDtypeStruct((r, c), F32)
    return _grid_call(body, name, (row_tiles, col_tiles), [spec, pl.BlockSpec((tr, gc), lambda i, j: (i, j)), spec, spec],
                      [spec] * 4, [out] * 4, (w, g, m, v), side)


def _adamw_small(w, g_top, g4, m, v):
    def body(w_ref, gt_ref, g_ref, m_ref, v_ref, go_ref, d_ref, mo_ref, vo_ref):
        g = jnp.concatenate([gt_ref[...], (g_ref[0] + g_ref[1]) + (g_ref[2] + g_ref[3])], axis=0)
        delta, m2, v2 = _adamw_math(w_ref[...], g, m_ref[...], v_ref[...])
        go_ref[...] = g
        d_ref[...] = delta
        mo_ref[...] = m2
        vo_ref[...] = v2

    out = jax.ShapeDtypeStruct(w.shape, F32)
    spec = pl.BlockSpec(w.shape, lambda: (0, 0))
    return pl.pallas_call(
        body, name="adamw_small",
        in_specs=[spec, pl.BlockSpec(g_top.shape, lambda: (0, 0)), pl.BlockSpec(g4.shape, lambda: (0, 0, 0)), spec, spec],
        out_specs=[spec] * 4, out_shape=[out] * 4, compiler_params=_params())(w, g_top, g4, m, v)


def _place():
    x, y, c = lax.axis_index("x"), lax.axis_index("y"), lax.axis_index("c")
    chips = [(1 - x, y), (x, 1 - y), (1 - x, 1 - y)]
    return x, y, c, chips


ANY = pl.BlockSpec(memory_space=pl.ANY)


REL_SLOT = (2, 1, 3)


def _gather_side(bufs, relative=False, two_path=False):
    n = len(bufs)
    direct = (0, 1) if two_path else (0, 1, 2)

    def copies(outs, sems):
        s_ici, r_ici, s_d2d, r_d2d = sems[:4]
        x, y, c, chips = _place()
        me = 2 * x + y

        def rows(w, slot, core, part=None):
            half = bufs[w].shape[1] // 2
            if part is None:
                return outs[w].at[slot, pl.ds(core * half, half)]
            return outs[w].at[slot, pl.ds(core * half + part * (half // 2), half // 2)]

        def theirs(j):
            return REL_SLOT[j] if relative else 2 * chips[j][0] + chips[j][1]

        def ici_send(w, j):
            px, py = chips[j]
            return pltpu.make_async_remote_copy(
                src_ref=rows(w, 0 if relative else me, c), dst_ref=rows(w, REL_SLOT[j] if relative else me, c),
                send_sem=s_ici.at[w, j], recv_sem=r_ici.at[w, j], device_id=(px, py, c), device_id_type=MESH)

        def ici_recv(w, j):
            px, py = chips[j]
            return pltpu.make_async_remote_copy(
                src_ref=rows(w, theirs(j), c), dst_ref=rows(w, theirs(j), c),
                send_sem=s_ici.at[w, j], recv_sem=r_ici.at[w, j], device_id=(px, py, c), device_id_type=MESH)

        def hop_send(w, p):
            px, py = chips[1 - p]
            return pltpu.make_async_remote_copy(
                src_ref=rows(w, theirs(p), c, p), dst_ref=rows(w, REL_SLOT[2] if relative else theirs(p), c, p),
                send_sem=sems[4].at[w, p], recv_sem=sems[5].at[w, p], device_id=(px, py, c), device_id_type=MESH)

        def hop_recv(w, p):
            px, py = chips[1 - p]
            return pltpu.make_async_remote_copy(
                src_ref=rows(w, theirs(2), c, p), dst_ref=rows(w, theirs(2), c, p),
                send_sem=sems[4].at[w, p], recv_sem=sems[5].at[w, p], device_id=(px, py, c), device_id_type=MESH)

        def d2d(w, j, core):
            return pltpu.make_async_remote_copy(
                src_ref=rows(w, theirs(j), core), dst_ref=rows(w, theirs(j), core),
                send_sem=s_d2d.at[w, j], recv_sem=r_d2d.at[w, j], device_id=(x, y, 1 - c), device_id_type=MESH)

        return c, ici_send, ici_recv, hop_send, hop_recv, d2d

    def send(outs, sems):
        c, ici_send, ici_recv, hop_send, hop_recv, d2d = copies(outs, sems)
        for j in direct:
            for w in range(n):
                ici_send(w, j).start()

    def arrived(j, outs, sems):
        c, ici_send, ici_recv, hop_send, hop_recv, d2d = copies(outs, sems)
        for w in range(n):
            if j in direct:
                ici_recv(w, j).wait_recv()
                if two_path:
                    hop_send(w, j).start()
            else:
                hop_recv(w, 0).wait_recv()
                hop_recv(w, 1).wait_recv()
            d2d(w, j, c).start()

    def forwarded(j, outs, sems):
        c, ici_send, ici_recv, hop_send, hop_recv, d2d = copies(outs, sems)
        for w in range(n):
            d2d(w, j, 1 - c).wait_recv()

    def drain(outs, sems):
        c, ici_send, ici_recv, hop_send, hop_recv, d2d = copies(outs, sems)
        for w in range(n):
            for j in direct:
                ici_send(w, j).wait_send()
                if two_path:
                    hop_send(w, j).wait_send()
            for j in range(3):
                d2d(w, j, c).wait_send()

    def start(ins, outs, sems):
        send(outs, sems)

    def middle(ins, outs, sems):
        arrived(0, outs, sems)
        arrived(1, outs, sems)

    def rest(ins, outs, sems):
        arrived(2, outs, sems)
        for j in range(3):
            forwarded(j, outs, sems)
        drain(outs, sems)

    def finish(ins, outs, sems):
        middle(ins, outs, sems)
        rest(ins, outs, sems)

    dma = pltpu.SemaphoreType.DMA((n, 3))
    hop = [pltpu.SemaphoreType.DMA((n, 2))] * 2 if two_path else []
    side = _Side(list(bufs), [jax.ShapeDtypeStruct(b.shape, b.dtype) for b in bufs], {w: w for w in range(n)},
                 [dma, dma, dma, dma] + hop, start, finish)
    side.send, side.arrived, side.forwarded, side.drain = send, arrived, forwarded, drain
    side.middle, side.rest = middle, rest
    return side


def _run_side(side, name):
    sin, sout = len(side.ins), len(side.outs)

    def body(*refs):
        ins, outs, sems = refs[:sin], refs[sin:sin + sout], refs[sin + sout:]
        side.start(ins, outs, sems)
        side.finish(ins, outs, sems)

    return pl.pallas_call(
        body, name=name, out_shape=list(side.outs), in_specs=[ANY] * sin, out_specs=[ANY] * sout,
        scratch_shapes=list(side.sems), input_output_aliases=dict(side.alias))(*side.ins)


def _pair_exchange_side(grads):
    n = len(grads)

    def copies(ins, outs, sems):
        ssem, rsem = sems
        x, y, c, _ = _place()
        cps = []
        for w in range(n):
            half = grads[w].shape[1] // 2
            cps.append(pltpu.make_async_remote_copy(
                src_ref=ins[w].at[:, pl.ds((1 - c) * half, half)], dst_ref=outs[w],
                send_sem=ssem.at[w], recv_sem=rsem.at[w], device_id=(x, y, 1 - c), device_id_type=MESH))
        return cps

    def start(ins, outs, sems):
        for cp in copies(ins, outs, sems):
            cp.start()

    def finish(ins, outs, sems):
        for cp in copies(ins, outs, sems):
            cp.wait()

    dma = pltpu.SemaphoreType.DMA((n,))
    return _Side(list(grads), [jax.ShapeDtypeStruct((N_CHIP, g.shape[1] // 2, g.shape[2]), BF) for g in grads], {},
                 [dma, dma], start, finish)


def _sibling_copy_side(buf):
    def copy(ins, outs, sems):
        x, y, c, _ = _place()
        return pltpu.make_async_remote_copy(src_ref=ins[0], dst_ref=outs[0], send_sem=sems[0], recv_sem=sems[1],
                                            device_id=(x, y, 1 - c), device_id_type=MESH)

    return _Side([buf], [jax.ShapeDtypeStruct(buf.shape, buf.dtype)], {}, [pltpu.SemaphoreType.DMA, pltpu.SemaphoreType.DMA],
                 lambda i, o, s: copy(i, o, s).start(), lambda i, o, s: copy(i, o, s).wait())


def _slot_exchange_side(buf4):
    def copies(outs, sems, sending):
        ssem, rsem = sems
        x, y, c, chips = _place()
        me = 2 * x + y
        return [pltpu.make_async_remote_copy(
            src_ref=outs[0].at[me if sending else 2 * px + py], dst_ref=outs[0].at[me if sending else 2 * px + py],
            send_sem=ssem.at[j], recv_sem=rsem.at[j], device_id=(px, py, c), device_id_type=MESH)
            for j, (px, py) in enumerate(chips)]

    def start(ins, outs, sems):
        for cp in copies(outs, sems, True):
            cp.start()

    def finish(ins, outs, sems):
        for cp in copies(outs, sems, False):
            cp.wait_recv()
        for cp in copies(outs, sems, True):
            cp.wait_send()

    dma = pltpu.SemaphoreType.DMA((3,))
    return _Side([buf4], [jax.ShapeDtypeStruct(buf4.shape, buf4.dtype)], {0: 0}, [dma, dma], start, finish)


def _pair_sum(g, sib, core, name):
    _, r, cdim = g.shape
    half = r // 2

    def body(core_ref, g_ref, s_ref, o_ref):
        o_ref[...] = (g_ref[...].astype(F32) + s_ref[...].astype(F32)).astype(BF)

    return pl.pallas_call(
        body, name=name,
        grid_spec=pltpu.PrefetchScalarGridSpec(
            num_scalar_prefetch=1, grid=(N_CHIP,),
            in_specs=[pl.BlockSpec((None, half, cdim), lambda k, core: (k, core[0], 0)),
                      pl.BlockSpec((None, half, cdim), lambda k, core: (k, 0, 0))],
            out_specs=pl.BlockSpec((None, half, cdim), lambda k, core: (k, 0, 0))),
        out_shape=jax.ShapeDtypeStruct((N_CHIP, half, cdim), BF),
        compiler_params=_params(("arbitrary",)),
    )(core, g, sib)


def _chip_exchange_side(psums, relative=False):
    n = len(psums)

    def copies(ins, outs, sems):
        ssem, rsem = sems
        x, y, c, chips = _place()
        return [pltpu.make_async_remote_copy(
            src_ref=ins[w].at[REL_SLOT[j] if relative else 2 * px + py], dst_ref=outs[w].at[j],
            send_sem=ssem.at[w, j], recv_sem=rsem.at[w, j], device_id=(px, py, c), device_id_type=MESH)
            for w in range(n) for j, (px, py) in enumerate(chips)]

    def start(ins, outs, sems):
        for cp in copies(ins, outs, sems):
            cp.start()

    def finish(ins, outs, sems):
        for cp in copies(ins, outs, sems):
            cp.wait()

    dma = pltpu.SemaphoreType.DMA((n, 3))
    return _Side(list(psums), [jax.ShapeDtypeStruct((3,) + p.shape[1:], BF) for p in psums], {}, [dma, dma],
                 start, finish)


def _final_sum(g, sib, recv, sel, name):
    _, r, cdim = g.shape
    half = r // 2
    nt = 4
    th = half // nt

    def body(sel_ref, g_ref, s_ref, r_ref, o_ref):
        acc = g_ref[...].astype(F32) + s_ref[...].astype(F32)
        for j in range(3):
            acc = acc + r_ref[j].astype(F32)
        o_ref[...] = acc

    return pl.pallas_call(
        body, name=name,
        grid_spec=pltpu.PrefetchScalarGridSpec(
            num_scalar_prefetch=1, grid=(nt,),
            in_specs=[pl.BlockSpec((None, th, cdim), lambda i, sel: (sel[0], sel[1] * nt + i, 0)),
                      pl.BlockSpec((None, th, cdim), lambda i, sel: (sel[0], i, 0)),
                      pl.BlockSpec((3, th, cdim), lambda i, sel: (0, i, 0))],
            out_specs=pl.BlockSpec((th, cdim), lambda i, sel: (sel[1] * nt + i, 0))),
        out_shape=jax.ShapeDtypeStruct((r, cdim), F32),
        compiler_params=_params(("arbitrary",)),
    )(sel, g, sib, recv)


def _join_side(bufs):
    n = len(bufs)

    def copies(outs, sems, core_of):
        ssem, rsem = sems
        x, y, c, _ = _place()
        cps = []
        for w in range(n):
            half = bufs[w].shape[0] // 2
            rows = outs[w].at[pl.ds(core_of(c) * half, half)]
            cps.append(pltpu.make_async_remote_copy(
                src_ref=rows, dst_ref=rows, send_sem=ssem.at[w], recv_sem=rsem.at[w],
                device_id=(x, y, 1 - c), device_id_type=MESH))
        return cps

    def start(ins, outs, sems):
        for cp in copies(outs, sems, lambda c: c):
            cp.start()

    def finish(ins, outs, sems):
        for cp in copies(outs, sems, lambda c: 1 - c):
            cp.wait_recv()
        for cp in copies(outs, sems, lambda c: c):
            cp.wait_send()

    dma = pltpu.SemaphoreType.DMA((n,))
    return _Side(list(bufs), [jax.ShapeDtypeStruct(b.shape, F32) for b in bufs], {w: w for w in range(n)}, [dma, dma],
                 start, finish)


def _small_pair_sum(buf, sib, chip):
    rows, d = buf.shape

    def body(chip_ref, a_ref, b_ref, o_ref):
        o_ref[...] = a_ref[...] + b_ref[...]

    return pl.pallas_call(
        body, name="small_pair_sum", out_shape=jax.ShapeDtypeStruct((N_CHIP, rows, d), F32),
        grid_spec=pltpu.PrefetchScalarGridSpec(
            num_scalar_prefetch=1, grid=(1,),
            in_specs=[pl.BlockSpec((rows, d), lambda i, chip: (0, 0))] * 2,
            out_specs=pl.BlockSpec((None, rows, d), lambda i, chip: (chip[0], 0, 0))),
        compiler_params=_params(("arbitrary",)),
    )(chip, buf, sib)


def _small_all_reduce(buf, name):
    rows, d = buf.shape

    def body(in_ref, out_ref, sib, all4, ssem, rsem, psem, qsem):
        x, y, c, chips = _place()
        me = 2 * x + y
        to_sib = pltpu.make_async_remote_copy(src_ref=in_ref, dst_ref=sib, send_sem=ssem, recv_sem=rsem,
                                              device_id=(x, y, 1 - c), device_id_type=MESH)
        to_sib.start()
        to_sib.wait()
        all4[me] = in_ref[...] + sib[...]
        cps = [pltpu.make_async_remote_copy(src_ref=all4.at[me], dst_ref=all4.at[me], send_sem=psem.at[j],
                                            recv_sem=qsem.at[j], device_id=(px, py, c), device_id_type=MESH)
               for j, (px, py) in enumerate(chips)]
        for cp in cps:
            cp.start()
        for j, (px, py) in enumerate(chips):
            chip = 2 * px + py
            pltpu.make_async_remote_copy(src_ref=all4.at[chip], dst_ref=all4.at[chip], send_sem=psem.at[j],
                                         recv_sem=qsem.at[j], device_id=(px, py, c), device_id_type=MESH).wait_recv()
        for cp in cps:
            cp.wait_send()
        out_ref[...] = (all4[0] + all4[1]) + (all4[2] + all4[3])

    vm = pl.BlockSpec(memory_space=pltpu.VMEM)
    return pl.pallas_call(
        body, name=name, out_shape=jax.ShapeDtypeStruct((rows, d), F32),
        in_specs=[vm], out_specs=vm,
        scratch_shapes=[pltpu.VMEM((rows, d), F32), pltpu.VMEM((N_CHIP, rows, d), F32),
                        pltpu.SemaphoreType.DMA, pltpu.SemaphoreType.DMA,
                        pltpu.SemaphoreType.DMA((3,)), pltpu.SemaphoreType.DMA((3,))],
        compiler_params=_params(),
    )(buf)


def _pair_blocks(w):
    w4 = w.reshape(N_HEADS // 2, 2, HEAD, HEAD)
    eye = jnp.eye(2, dtype=w.dtype)
    return jnp.einsum("pirc,ij->pirjc", w4, eye).reshape(N_HEADS // 2, LANE, LANE)


def _unpair_blocks(w2):
    w5 = w2.reshape(N_HEADS // 2, 2, HEAD, 2, HEAD)
    return jnp.stack([w5[:, 0, :, 0, :], w5[:, 1, :, 1, :]], axis=1).reshape(N_HEADS, HEAD, HEAD)


def kernel(x, meta_tokens, ffn1_pre_g, ffn1_w_gate, ffn1_w_up, ffn1_w_down, ffn1_post_g, mix_pre_g, w_in, lru_conv_w, lru_conv_b, lru_w_a, lru_b_a, lru_w_x, lru_b_x, lru_lambda, sconv_w, lru_out_g, sconv_out_g, w_out, mix_post_g, ffn2_pre_g, ffn2_w_gate, ffn2_w_up, ffn2_w_down, ffn2_post_g, loss_target, m_meta_tokens, m_ffn1_pre_g, m_ffn1_w_gate, m_ffn1_w_up, m_ffn1_w_down, m_ffn1_post_g, m_mix_pre_g, m_w_in, m_lru_conv_w, m_lru_conv_b, m_lru_w_a, m_lru_b_a, m_lru_w_x, m_lru_b_x, m_lru_lambda, m_sconv_w, m_lru_out_g, m_sconv_out_g, m_w_out, m_mix_post_g, m_ffn2_pre_g, m_ffn2_w_gate, m_ffn2_w_up, m_ffn2_w_down, m_ffn2_post_g, v_meta_tokens, v_ffn1_pre_g, v_ffn1_w_gate, v_ffn1_w_up, v_ffn1_w_down, v_ffn1_post_g, v_mix_pre_g, v_w_in, v_lru_conv_w, v_lru_conv_b, v_lru_w_a, v_lru_b_a, v_lru_w_x, v_lru_b_x, v_lru_lambda, v_sconv_w, v_lru_out_g, v_sconv_out_g, v_w_out, v_mix_post_g, v_ffn2_pre_g, v_ffn2_w_gate, v_ffn2_w_up, v_ffn2_w_down, v_ffn2_post_g):
    seq, d = x.shape[1], x.shape[2]
    t_real = N_META + seq
    tp = _round_up(t_real, ROW_ALIGN)
    f4 = ffn1_w_gate.shape[2]
    f4p = _round_up(f4, LANE)
    dl = lru_conv_b.shape[1]
    cin = w_in.shape[2]
    xi, yi, ci = lax.axis_index("x"), lax.axis_index("y"), lax.axis_index("c")
    chip = 2 * xi + yi
    zero = jnp.zeros((), jnp.int32)

    transposed = ("ffn1_w_gate", "ffn1_w_up", "ffn2_w_gate", "ffn2_w_up")

    def view(k, a):
        return a[0].T if k in transposed else a[0]

    def unview(k, a):
        return (a.T if k in transposed else a)[None]

    big = {
        "ffn1_w_gate": (view("ffn1_w_gate", ffn1_w_gate), f4p, d), "ffn1_w_up": (view("ffn1_w_up", ffn1_w_up), f4p, d),
        "ffn1_w_down": (ffn1_w_down[0], f4p, d), "w_in": (w_in[0], d, cin), "w_out": (w_out[0], w_out.shape[1], d),
        "ffn2_w_gate": (view("ffn2_w_gate", ffn2_w_gate), f4p, d), "ffn2_w_up": (view("ffn2_w_up", ffn2_w_up), f4p, d),
        "ffn2_w_down": (ffn2_w_down[0], f4p, d),
    }
    names = list(big)
    chip1 = jnp.reshape(chip, (1,)).astype(jnp.int32)
    relative = {k: k.startswith("ffn") for k in names}
    slot0 = jnp.zeros((1,), jnp.int32)
    shard = {k: _cast_pad(big[k][0], big[k][1], big[k][2], slot0 if relative[k] else chip1, "cast_" + k) for k in names}
    full = {}

    def gather(*keys):
        return _merge_sides([_gather_side([shard[k]], relative[k], two_path=True) for k in keys])

    gm = jnp.kron(jnp.eye(2, dtype=F32), jnp.full((HEAD, HEAD), 1.0 / HEAD, F32)).astype(BF)
    wa2 = _pair_blocks(lru_w_a[0])
    wx2 = _pair_blocks(lru_w_x[0])

    dlq = dl // N_CHIP
    dq = d // N_CHIP
    R_GAIN, R_LOSS, R_META, R_LRU, R_SC, R_WA = 0, 6, 8, 24, 40, 48
    n_wrows = (N_HEADS // 2) * LANE * LANE // d
    R_WX = R_WA + n_wrows
    R_END = R_WX + n_wrows

    def pack_top(gains, meta, loss=None):
        lossrow = jnp.zeros((2, d), F32)
        if loss is not None:
            lossrow = lossrow.at[0, 0].set(loss)
        return jnp.concatenate([jnp.concatenate(gains, axis=0), lossrow, meta], axis=0)

    def pack_rest(lru16, sc8, wa_, wx_):
        return jnp.concatenate([jnp.concatenate([lru16, jnp.zeros((16, d - dl), F32)], axis=1),
                                jnp.concatenate([sc8, jnp.zeros((8, d - dl), F32)], axis=1),
                                wa_.reshape(n_wrows, d), wx_.reshape(n_wrows, d)], axis=0)

    def pack(gains, meta, lru16, sc8, wa_, wx_):
        return jnp.concatenate([pack_top(gains, meta), pack_rest(lru16, sc8, wa_, wx_)], axis=0)

    def place_cols(blk, width, total):
        return lax.dynamic_update_slice(jnp.zeros((blk.shape[0], total), F32), blk, (zero, chip * width))

    def pack_params(meta_, g1pre, g1post, gmpre, gmpost, g2pre, g2post, cw, cbias, wa_, ba_, wx_, bx_, lam_, sw, lgo, sgo):
        lru16 = jnp.concatenate([place_cols(cw[0], dlq, dl), cbias, ba_, bx_, lam_, lgo, jnp.zeros((7, dl), F32)], axis=0)
        sc8 = jnp.concatenate([place_cols(sw[0], dlq, dl), sgo, jnp.zeros((4, dl), F32)], axis=0)
        return pack([g1pre, g1post, gmpre, gmpost, g2pre, g2post], place_cols(meta_, dq, d), lru16, sc8,
                    _pair_blocks(wa_[0]), _pair_blocks(wx_[0]))

    p_w = pack_params(meta_tokens, ffn1_pre_g, ffn1_post_g, mix_pre_g, mix_post_g, ffn2_pre_g, ffn2_post_g, lru_conv_w,
                      lru_conv_b, lru_w_a, lru_b_a, lru_w_x, lru_b_x, lru_lambda, sconv_w, lru_out_g, sconv_out_g)
    p_m = pack_params(m_meta_tokens, m_ffn1_pre_g, m_ffn1_post_g, m_mix_pre_g, m_mix_post_g, m_ffn2_pre_g, m_ffn2_post_g,
                      m_lru_conv_w, m_lru_conv_b, m_lru_w_a, m_lru_b_a, m_lru_w_x, m_lru_b_x, m_lru_lambda, m_sconv_w,
                      m_lru_out_g, m_sconv_out_g)
    p_v = pack_params(v_meta_tokens, v_ffn1_pre_g, v_ffn1_post_g, v_mix_pre_g, v_mix_post_g, v_ffn2_pre_g, v_ffn2_post_g,
                      v_lru_conv_w, v_lru_conv_b, v_lru_w_a, v_lru_b_a, v_lru_w_x, v_lru_b_x, v_lru_lambda, v_sconv_w,
                      v_lru_out_g, v_sconv_out_g)

    gathered = _small_all_reduce(jnp.where(ci == 0, p_w, 0.0)[R_META:R_WA], "small_weight_gather")
    meta_full = gathered[0:N_META]
    w4_full = gathered[R_LRU - R_META:R_LRU - R_META + 4, 0:dl]
    w3_full = gathered[R_SC - R_META:R_SC - R_META + 3, 0:dl]
    w4p = jnp.concatenate([w4_full, jnp.zeros((4, dl), F32)], axis=0)
    w3p = jnp.concatenate([w3_full, jnp.zeros((5, dl), F32)], axis=0)

    h0 = jnp.concatenate([meta_full, x[0], jnp.zeros((tp - t_real, d), F32)], axis=0)
    tgt = jnp.concatenate([jnp.zeros((N_META, d), F32), loss_target[0], jnp.zeros((tp - t_real, d), F32)], axis=0)

    n1 = _norm0(h0, ffn1_pre_g)
    (a1, b1, s1), (full["ffn1_w_gate"], full["ffn1_w_up"]), got = _ffn_up_head(
        n1, shard["ffn1_w_gate"], shard["ffn1_w_up"], "ffn1_up",
        _gather_side([shard["ffn1_w_down"]], relative=True, two_path=True))
    full["ffn1_w_down"] = got[0]
    f1, got = _wide_matmul([(s1, full["ffn1_w_down"])], "ffn1_down", WIDE_TN, gather("w_in"))
    full["w_in"] = got[0]
    h1, u = _post_fwd(f1, h0, ffn1_post_g, mix_pre_g, 0.5, "ffn1_post")
    z, got = _col_matmul(u, full["w_in"], "in_proj", False, F32, gather("ffn2_w_gate"))
    full["ffn2_w_gate"] = got[0]
    (m_lru, hs), got = _lru_fwd(z, w4p, lru_conv_b, wa2.astype(BF), lru_b_a, wx2.astype(BF), lru_b_x, lru_lambda,
                                lru_out_g, gm, gather("ffn2_w_up"))
    full["ffn2_w_up"] = got[0]
    m_sc, got = _sc_fwd(z, w3p, sconv_out_g, gm, dl, gather("w_out"))
    full["w_out"] = got[0]
    mixed = jnp.concatenate([m_lru, m_sc], axis=1)
    p, _ = _row_matmul([(mixed, full["w_out"])], "out_proj", False, d)
    h2, n2 = _post_fwd(p, h1, mix_post_g, ffn2_pre_g, 1.0, "mix_post")
    (a2, b2, s2), got = _ffn_up(n2, full["ffn2_w_gate"], full["ffn2_w_up"], "ffn2_up", gather("ffn2_w_down"))
    full["ffn2_w_down"] = got[0]
    f2, _ = _wide_matmul([(s2, full["ffn2_w_down"])], "ffn2_down", WIDE_TN)
    dh3, df2, dg_ffn2_post, loss_part = _loss_bwd(f2, h2, tgt, ffn2_post_g, t_real)

    core = jnp.reshape(ci, (1,)).astype(jnp.int32)
    sel_of = {False: jnp.stack([chip, ci]).astype(jnp.int32), True: jnp.stack([0 * chip, ci]).astype(jnp.int32)}
    red = {}

    def pair_side(k):
        return _pair_exchange_side([red[k][0]])

    def chip_side(k):
        return _chip_exchange_side([_pair_sum(red[k][0], red[k][1], core, "pair_sum_" + k)], relative[k])

    def final_sum(k):
        return _final_sum(*red[k], sel_of[relative[k]], "final_sum_" + k)

    (da2, db2), _ = _ffn_bwd_act(df2, full["ffn2_w_down"], a2, b2, "ffn2_bwd_act")
    g, _ = _wgrad_call(s2, df2, "ffn2_down_wgrad", x_width=f4p, tile_y=WGRAD_TILE_Y)
    red["ffn2_w_down"] = [g, None, None]
    g, got = _wgrad_call(da2, n2, "ffn2_gate_wgrad", x_width=f4p, tile_y=WGRAD_TILE_Y, side=pair_side("ffn2_w_down"))
    red["ffn2_w_down"][1] = got[0]
    red["ffn2_w_gate"] = [g, None, None]
    g, got = _wgrad_call(db2, n2, "ffn2_up_wgrad", x_width=f4p, tile_y=WGRAD_TILE_Y,
                         side=_merge_sides([pair_side("ffn2_w_gate"), chip_side("ffn2_w_down")]))
    red["ffn2_w_gate"][1], red["ffn2_w_down"][2] = got
    red["ffn2_w_up"] = [g, None, None]
    dn2, got = _row_matmul([(da2, full["ffn2_w_gate"]), (db2, full["ffn2_w_up"])], "ffn2_bwd_up", False, d,
                           _merge_sides([pair_side("ffn2_w_up"), chip_side("ffn2_w_gate")]), tiles=MM_TILES)
    red["ffn2_w_up"][1], red["ffn2_w_gate"][2] = got
    dh2, dp, dg_ffn2_pre, dg_mix_post = _pre_bwd(dn2, h2, dh3, ffn2_pre_g, "ffn2_pre_bwd", (p, mix_post_g, 1.0))
    dmixed, _ = _col_matmul(dp, full["w_out"], "out_proj_bwd", True, F32)
    g, _ = _wgrad_call(mixed, dp, "w_out_wgrad", x_width=mixed.shape[1] // N_CHIP, tile_y=WGRAD_TILE_Y)
    red["w_out"] = [g, None, None]
    (dzy, dzx, lru_small, dwa2, dwx2), got = _lru_bwd(
        z, hs, dmixed, w4p, lru_conv_b, wa2.astype(BF), lru_b_a, wx2.astype(BF), lru_b_x, lru_lambda, lru_out_g, gm,
        _merge_sides([pair_side("w_out"), chip_side("ffn2_w_up")]))
    red["w_out"][1], red["ffn2_w_up"][2] = got
    (dzb, dzc, dzv, sc_small), got = _sc_bwd(z, dmixed, w3p, sconv_out_g, gm, dl, chip_side("w_out"))
    red["w_out"][2] = got[0]
    dz = jnp.concatenate([dzy, dzx, dzb, dzc, dzv], axis=1)
    p_rest = pack_rest(lru_small, sc_small, dwa2, dwx2)
    g, got = _wgrad_call(u, dz, "w_in_wgrad", y_width=cin, tile_x=WGRAD_TILE_X, side=_sibling_copy_side(p_rest))
    p_rest4 = _small_pair_sum(p_rest, got[0], chip1)
    red["w_in"] = [g, None, None]
    du, got = _row_matmul([(dz, full["w_in"])], "in_proj_bwd", True, d,
                          _merge_sides([pair_side("w_in"), _slot_exchange_side(p_rest4)]))
    red["w_in"][1], p_rest4 = got
    dh1, df1, dg_mix_pre, dg_ffn1_post = _pre_bwd(du, h1, dh2, mix_pre_g, "mix_pre_bwd", (f1, ffn1_post_g, 0.5))
    (da1, db1), got = _ffn_bwd_act(df1, full["ffn1_w_down"], a1, b1, "ffn1_bwd_act", chip_side("w_in"))
    red["w_in"][2] = got[0]
    early = ["ffn2_w_down", "ffn2_w_gate", "ffn2_w_up", "w_out", "w_in"]
    late = ["ffn1_w_down", "ffn1_w_gate", "ffn1_w_up"]
    g, got = _wgrad_call(s1, df1, "ffn1_down_wgrad", x_width=f4p, tile_y=WGRAD_TILE_Y,
                         side=_join_side([final_sum(k) for k in early]))
    gfull = dict(zip(early, got))
    red["ffn1_w_down"] = [g, None, None]
    g, got = _wgrad_call(da1, n1, "ffn1_gate_wgrad", x_width=f4p, tile_y=WGRAD_TILE_Y, side=pair_side("ffn1_w_down"))
    red["ffn1_w_down"][1] = got[0]
    red["ffn1_w_gate"] = [g, None, None]
    g, got = _wgrad_call(db1, n1, "ffn1_up_wgrad", x_width=f4p, tile_y=WGRAD_TILE_Y,
                         side=_merge_sides([pair_side("ffn1_w_gate"), chip_side("ffn1_w_down")]))
    red["ffn1_w_gate"][1], red["ffn1_w_down"][2] = got
    red["ffn1_w_up"] = [g, None, None]
    red["ffn1_w_up"][1] = _run_side(pair_side("ffn1_w_up"), "pair_exchange_ffn1_w_up")[0]
    dn1, got = _row_matmul([(da1, full["ffn1_w_gate"]), (db1, full["ffn1_w_up"])], "ffn1_bwd_up", False, d,
                           _merge_sides([chip_side("ffn1_w_gate"), chip_side("ffn1_w_up")]), tiles=MM_TILES)
    red["ffn1_w_gate"][2], red["ffn1_w_up"][2] = got
    (dh0, dg_ffn1_pre), got = _pre_bwd(dn1, h0, dh1, ffn1_pre_g, "ffn1_pre_bwd",
                                       side=_join_side([final_sum(k) for k in late]))
    gfull.update(zip(late, got))

    grad_x = dh0[N_META:t_real][None]

    w_big = {"ffn1_w_gate": ffn1_w_gate, "ffn1_w_up": ffn1_w_up, "ffn1_w_down": ffn1_w_down, "w_in": w_in, "w_out": w_out,
             "ffn2_w_gate": ffn2_w_gate, "ffn2_w_up": ffn2_w_up, "ffn2_w_down": ffn2_w_down}
    m_big = {"ffn1_w_gate": m_ffn1_w_gate, "ffn1_w_up": m_ffn1_w_up, "ffn1_w_down": m_ffn1_w_down, "w_in": m_w_in,
             "w_out": m_w_out, "ffn2_w_gate": m_ffn2_w_gate, "ffn2_w_up": m_ffn2_w_up, "ffn2_w_down": m_ffn2_w_down}
    v_big = {"ffn1_w_gate": v_ffn1_w_gate, "ffn1_w_up": v_ffn1_w_up, "ffn1_w_down": v_ffn1_w_down, "w_in": v_w_in,
             "w_out": v_w_out, "ffn2_w_gate": v_ffn2_w_gate, "ffn2_w_up": v_ffn2_w_up, "ffn2_w_down": v_ffn2_w_down}
    b_grad, b_delta, b_newm, b_newv = {}, {}, {}, {}

    def big_adamw(k, side=None):
        wv, mv, vv = view(k, w_big[k]), view(k, m_big[k]), view(k, v_big[k])
        wide_rows = wv.shape[0] % 64 == 0
        (g_, d_, m_, v_), got = _adamw(wv, gfull[k], mv, vv, "adamw_" + k, 8 if wide_rows else 4, 1 if wide_rows else 2,
                                       side)
        b_grad[k], b_delta[k], b_newm[k], b_newv[k] = unview(k, g_), unview(k, d_), unview(k, m_), unview(k, v_)
        return got

    p_top = _small_all_reduce(
        pack_top([dg_ffn1_pre, dg_ffn1_post, dg_mix_pre, dg_mix_post, dg_ffn2_pre, dg_ffn2_post], dh0[0:N_META],
                 loss=loss_part[0, 0]), "small_grad_all_reduce")
    p_g, p_delta, p_newm, p_newv = _adamw_small(p_w, p_top, p_rest4, p_m, p_v)
    loss = p_g[R_LOSS, 0]
    for k in names:
        big_adamw(k)

    def unpack(buf):
        out = {}
        for i, k in enumerate(["ffn1_pre_g", "ffn1_post_g", "mix_pre_g", "mix_post_g", "ffn2_pre_g", "ffn2_post_g"]):
            out[k] = buf[R_GAIN + i:R_GAIN + i + 1]
        out["meta_tokens"] = lax.dynamic_slice(buf[R_META:R_META + N_META], (zero, chip * dq), (N_META, dq))
        lru = buf[R_LRU:R_LRU + 16, 0:dl]
        out["lru_conv_w"] = lax.dynamic_slice(lru[0:4], (zero, chip * dlq), (4, dlq))[None]
        out["lru_conv_b"] = lru[4:5]
        out["lru_b_a"] = lru[5:6]
        out["lru_b_x"] = lru[6:7]
        out["lru_lambda"] = lru[7:8]
        out["lru_out_g"] = lru[8:9]
        sc = buf[R_SC:R_SC + 8, 0:dl]
        out["sconv_w"] = lax.dynamic_slice(sc[0:3], (zero, chip * dlq), (3, dlq))[None]
        out["sconv_out_g"] = sc[3:4]
        out["lru_w_a"] = _unpair_blocks(buf[R_WA:R_WX].reshape(N_HEADS // 2, LANE, LANE))[None]
        out["lru_w_x"] = _unpair_blocks(buf[R_WX:R_END].reshape(N_HEADS // 2, LANE, LANE))[None]
        return out

    s_grad, s_delta, s_newm, s_newv = unpack(p_g), unpack(p_delta), unpack(p_newm), unpack(p_newv)

    order = ["meta_tokens", "ffn1_pre_g", "ffn1_w_gate", "ffn1_w_up", "ffn1_w_down", "ffn1_post_g", "mix_pre_g", "w_in",
             "lru_conv_w", "lru_conv_b", "lru_w_a", "lru_b_a", "lru_w_x", "lru_b_x", "lru_lambda", "sconv_w", "lru_out_g",
             "sconv_out_g", "w_out", "mix_post_g", "ffn2_pre_g", "ffn2_w_gate", "ffn2_w_up", "ffn2_w_down", "ffn2_post_g"]

    def pick(small, bigd):
        return [bigd[k] if k in bigd else small[k] for k in order]

    return (loss, grad_x, *pick(s_grad, b_grad), *pick(s_delta, b_delta), *pick(s_newm, b_newm), *pick(s_newv, b_newv))
```

```python
import functools
import math

import jax
import jax.numpy as jnp
from jax import lax
from jax.experimental import pallas as pl
from jax.experimental.pallas import tpu as pltpu

F32 = jnp.float32
BF = jnp.bfloat16
MESH = pl.DeviceIdType.MESH

EPS = 1e-6
N_META = 16
N_HEADS = 16
HEAD = 64
LRU_C = 8.0
LANE = 128
MXU_COLS = 256
N_CHIP = 4
ROW_ALIGN = 384
MM_TILES = 8
MM_TILES_BIG = 4
SWIGLU_TILES = 6
EW_TILES = 12
MIX_CHUNKS = 24
WGRAD_TILE_X = 256
WGRAD_TILE_Y = 512
WIDE_TN = 512
CAST_COLS = 256
VMEM_LIMIT = 56 << 20

ADAM_LR = 0.001
ADAM_B1 = 0.9
ADAM_B2 = 0.999
ADAM_EPS = 1e-08
ADAM_WD = 0.01
ADAM_STEP = 10


def _round_up(a, b):
    return (a + b - 1) // b * b


def _params(sem=None):
    if sem is None:
        return pltpu.CompilerParams(vmem_limit_bytes=VMEM_LIMIT)
    return pltpu.CompilerParams(dimension_semantics=sem, vmem_limit_bytes=VMEM_LIMIT)


def _sigmoid(x):
    return 0.5 * jnp.tanh(0.5 * x) + 0.5


def _dot(a, b):
    return jnp.dot(a, b, preferred_element_type=F32)


def _dot_nt(a, b):
    return lax.dot_general(a, b, (((1,), (1,)), ((), ())), preferred_element_type=F32)


def _dot_tn(a, b):
    return lax.dot_general(a, b, (((0,), (0,)), ((), ())), preferred_element_type=F32)


def _rms(x, g):
    r = lax.rsqrt(jnp.mean(x * x, axis=-1, keepdims=True) + EPS)
    return x * r * g


def _rms_bwd(x, g, dy):
    r = lax.rsqrt(jnp.mean(x * x, axis=-1, keepdims=True) + EPS)
    xh = x * r
    q = dy * g
    dx = r * (q - xh * jnp.mean(q * xh, axis=-1, keepdims=True))
    return dx, dy * xh


class _Side:
    def __init__(self, ins, outs, alias, sems, start, finish):
        self.ins, self.outs, self.alias, self.sems, self.start, self.finish = ins, outs, alias, sems, start, finish


def _merge_sides(sides):
    sides = [s for s in sides if s is not None]
    if len(sides) <= 1:
        return sides[0] if sides else None
    ins, outs, sems, alias, spans = [], [], [], {}, []
    for s in sides:
        for i, o in s.alias.items():
            alias[len(ins) + i] = len(outs) + o
        spans.append((len(ins), len(ins) + len(s.ins), len(outs), len(outs) + len(s.outs), len(sems),
                      len(sems) + len(s.sems)))
        ins += list(s.ins)
        outs += list(s.outs)
        sems += list(s.sems)

    def run(which):
        def go(in_refs, out_refs, sem_refs):
            for s, (a, b, c, d, e, f) in zip(sides, spans):
                getattr(s, which)(in_refs[a:b], out_refs[c:d], sem_refs[e:f])
        return go

    return _Side(ins, outs, alias, sems, run("start"), run("finish"))


def _grid_call(body, name, grid, in_specs, out_specs, out_shape, args, side=None, scratch=()):
    sem = ("arbitrary",) * len(grid)
    if side is None:
        res = pl.pallas_call(body, name=name, grid=grid, in_specs=in_specs, out_specs=out_specs, out_shape=out_shape,
                             scratch_shapes=list(scratch), compiler_params=_params(sem))(*args)
        return res, []
    nin, nout, sin, sout = len(in_specs), len(out_specs), len(side.ins), len(side.outs)
    nscr = len(scratch)
    staged = hasattr(side, "middle") and math.prod(grid) >= 4
    lin, mid = (math.prod(grid) * 5) // 8, []
    for extent in reversed(grid):
        mid.insert(0, lin % extent)
        lin //= extent

    def full(*refs):
        base_in, side_in = refs[:nin], refs[nin:nin + sin]
        base_out = refs[nin + sin:nin + sin + nout]
        side_out = refs[nin + sin + nout:nin + sin + nout + sout]
        base_scr = refs[nin + sin + nout + sout:nin + sin + nout + sout + nscr]
        sems = refs[nin + sin + nout + sout + nscr:]
        first = pl.program_id(0) == 0
        last = pl.program_id(0) == grid[0] - 1
        for ax in range(1, len(grid)):
            first = first & (pl.program_id(ax) == 0)
            last = last & (pl.program_id(ax) == grid[ax] - 1)

        @pl.when(first)
        def _():
            side.start(side_in, side_out, sems)

        if staged:
            at_mid = pl.program_id(0) == mid[0]
            for ax in range(1, len(grid)):
                at_mid = at_mid & (pl.program_id(ax) == mid[ax])

            @pl.when(at_mid)
            def _():
                side.middle(side_in, side_out, sems)

        body(*base_in, *base_out, *base_scr)

        @pl.when(last)
        def _():
            (side.rest if staged else side.finish)(side_in, side_out, sems)

    any_spec = pl.BlockSpec(memory_space=pl.ANY)
    res = pl.pallas_call(
        full, name=name, grid=grid, in_specs=list(in_specs) + [any_spec] * sin,
        out_specs=list(out_specs) + [any_spec] * sout, out_shape=list(out_shape) + list(side.outs),
        scratch_shapes=list(scratch) + list(side.sems),
        input_output_aliases={nin + i: nout + o for i, o in side.alias.items()},
        compiler_params=_params(sem))(*args, *side.ins)
    return res[:nout], res[nout:]


def _ffn_up(n, wg, wu, name, side=None, tiles=SWIGLU_TILES):
    tp, d = n.shape
    fp = wg.shape[1]
    tm = tp // tiles

    def body(n_ref, wg_ref, wu_ref, a_ref, b_ref, s_ref):
        nn = n_ref[...]
        for c0 in range(0, fp, MXU_COLS):
            cs = slice(c0, min(c0 + MXU_COLS, fp))
            a = _dot_nt(nn, wg_ref[cs, :])
            b = _dot_nt(nn, wu_ref[cs, :])
            a_ref[:, cs] = a.astype(BF)
            b_ref[:, cs] = b.astype(BF)
            s_ref[:, cs] = (a * _sigmoid(a) * b).astype(BF)

    out = jax.ShapeDtypeStruct((tp, N_CHIP * fp), BF)
    wspec = pl.BlockSpec((None, fp, d), lambda k, i: (k, 0, 0))
    ospec = pl.BlockSpec((tm, fp), lambda k, i: (i, k))
    return _grid_call(body, name, (N_CHIP, tiles), [pl.BlockSpec((tm, d), lambda k, i: (i, 0)), wspec, wspec],
                      [ospec, ospec, ospec], [out, out, out], (n, wg, wu), side)


def _ffn_up_head(n, wg, wu, name, side):
    tp, d = n.shape
    fp = wg.shape[1]
    tiles = SWIGLU_TILES
    tm = tp // tiles
    gat = _gather_side([wg, wu], relative=True, two_path=True)
    sin, sout, ngs = len(side.ins), len(side.outs), len(gat.sems)
    order = (0,) + REL_SLOT
    staged = hasattr(side, "middle")

    def body(*refs):
        n_ref = refs[0]
        si = refs[3:3 + sin]
        a_ref, b_ref, s_ref = refs[3 + sin:6 + sin]
        go = refs[6 + sin:8 + sin]
        so = refs[8 + sin:8 + sin + sout]
        wbg, wbu, wsem = refs[8 + sin + sout:11 + sin + sout]
        gsems = refs[11 + sin + sout:11 + sin + sout + ngs]
        ssems = refs[11 + sin + sout + ngs:]
        k, i = pl.program_id(0), pl.program_id(1)
        cur = k % 2

        def to_vmem(slot, buf):
            return [pltpu.make_async_copy(go[0].at[slot], wbg.at[buf], wsem.at[buf, 0]),
                    pltpu.make_async_copy(go[1].at[slot], wbu.at[buf], wsem.at[buf, 1])]

        @pl.when((k == 0) & (i == 0))
        def _():
            gat.send(go, gsems)
            if not staged:
                side.start(si, so, ssems)
            for cp in to_vmem(0, 0):
                cp.start()
            for cp in to_vmem(0, 0):
                cp.wait()

        for j in range(3):
            @pl.when((k == j) & (i == tiles // 2))
            def _():
                gat.arrived(j, go, gsems)
                if staged and j == 1:
                    side.start(si, so, ssems)

            @pl.when((k == j) & (i == tiles - 2))
            def _():
                gat.forwarded(j, go, gsems)
                for cp in to_vmem(order[j + 1], (j + 1) % 2):
                    cp.start()

            @pl.when((k == j + 1) & (i == 0))
            def _():
                for cp in to_vmem(order[j + 1], (j + 1) % 2):
                    cp.wait()

        nn = n_ref[...]
        for c0 in range(0, fp, MXU_COLS):
            cs = pl.ds(c0, min(MXU_COLS, fp - c0))
            a = _dot_nt(nn, wbg[cur, cs, :])
            b = _dot_nt(nn, wbu[cur, cs, :])
            a_ref[:, cs] = a.astype(BF)
            b_ref[:, cs] = b.astype(BF)
            s_ref[:, cs] = (a * _sigmoid(a) * b).astype(BF)

        if staged:
            @pl.when((k == N_CHIP - 1) & (i == tiles // 4))
            def _():
                side.middle(si, so, ssems)

        @pl.when((k == N_CHIP - 1) & (i == tiles - 1))
        def _():
            gat.drain(go, gsems)
            if staged:
                side.rest(si, so, ssems)
            else:
                side.finish(si, so, ssems)

    out = jax.ShapeDtypeStruct((tp, N_CHIP * fp), BF)
    any_spec = pl.BlockSpec(memory_space=pl.ANY)
    slot_of = lambda k: (k % 2) * 2 + k // 2
    ospec = pl.BlockSpec((tm, fp), lambda k, i: (i, slot_of(k)))
    wbuf = pltpu.VMEM((2, fp, d), BF)
    res = pl.pallas_call(
        body, name=name, grid=(N_CHIP, tiles),
        in_specs=[pl.BlockSpec((tm, d), lambda k, i: (i, 0))] + [any_spec] * (2 + sin),
        out_specs=[ospec, ospec, ospec] + [any_spec] * (2 + sout),
        out_shape=[out, out, out] + list(gat.outs) + list(side.outs),
        scratch_shapes=[wbuf, wbuf, pltpu.SemaphoreType.DMA((2, 2))] + list(gat.sems) + list(side.sems),
        input_output_aliases={1: 3, 2: 4, **{3 + a: 5 + b for a, b in side.alias.items()}},
        compiler_params=_params(("arbitrary", "arbitrary")))(n, wg, wu, *side.ins)
    return res[:3], res[3:5], res[5:]


def _ffn_bwd_act(df, wd, a, b, name, side=None, tiles=SWIGLU_TILES):
    tp, d = df.shape
    fp = wd.shape[1]
    tm = tp // tiles

    def body(df_ref, wd_ref, a_ref, b_ref, da_ref, db_ref):
        dfv = df_ref[...]
        for c0 in range(0, fp, MXU_COLS):
            cs = slice(c0, min(c0 + MXU_COLS, fp))
            ds = _dot_nt(dfv, wd_ref[cs, :])
            av = a_ref[:, cs].astype(F32)
            bv = b_ref[:, cs].astype(F32)
            sg = _sigmoid(av)
            da_ref[:, cs] = (ds * bv * sg * (1.0 + av * (1.0 - sg))).astype(BF)
            db_ref[:, cs] = (ds * av * sg).astype(BF)

    out = jax.ShapeDtypeStruct((tp, N_CHIP * fp), BF)
    aspec = pl.BlockSpec((tm, fp), lambda k, i: (i, k))
    return _grid_call(
        body, name, (N_CHIP, tiles),
        [pl.BlockSpec((tm, d), lambda k, i: (i, 0)), pl.BlockSpec((None, fp, d), lambda k, i: (k, 0, 0)), aspec, aspec],
        [aspec, aspec], [out, out], (df, wd, a, b), side)


def _col_matmul(lhs, w, name, trans_b, out_dtype, side=None, tiles=MM_TILES_BIG):
    tp, kd = lhs.shape
    nk = w.shape[0]
    nc = w.shape[1] if trans_b else w.shape[2]
    tm = tp // tiles

    def body(l_ref, w_ref, o_ref):
        if trans_b:
            o_ref[...] = _dot_nt(l_ref[...], w_ref[...]).astype(out_dtype)
        else:
            o_ref[...] = _dot(l_ref[...], w_ref[...]).astype(out_dtype)

    res, extra = _grid_call(
        body, name, (nk, tiles),
        [pl.BlockSpec((tm, kd), lambda k, i: (i, 0)),
         pl.BlockSpec((None,) + tuple(w.shape[1:]), lambda k, i: (k, 0, 0), pipeline_mode=pl.Buffered(1))],
        [pl.BlockSpec((tm, nc), lambda k, i: (i, k))], [jax.ShapeDtypeStruct((tp, nk * nc), out_dtype)], (lhs, w), side)
    return res[0], extra


def _row_matmul(pairs, name, trans_b, d_out, side=None, tiles=MM_TILES_BIG):
    l0 = pairs[0][0]
    tp = l0.shape[1] if l0.ndim == 3 else l0.shape[0]
    nk = pairs[0][1].shape[0]
    tm = tp // tiles
    npair = len(pairs)

    def body(*refs):
        o_ref = refs[2 * npair]
        k = pl.program_id(1)
        part = None
        for q in range(npair):
            l = refs[2 * q][...]
            w = refs[2 * q + 1][...]
            t = _dot_nt(l, w) if trans_b else _dot(l, w)
            part = t if part is None else part + t

        @pl.when(k == 0)
        def _():
            o_ref[...] = part

        @pl.when(k > 0)
        def _():
            o_ref[...] += part

    in_specs, args = [], []
    for lhs, w in pairs:
        if lhs.ndim == 3:
            in_specs.append(pl.BlockSpec((None, tm, lhs.shape[2]), lambda i, k: (k, i, 0)))
        else:
            in_specs.append(pl.BlockSpec((tm, lhs.shape[1] // nk), lambda i, k: (i, k)))
        in_specs.append(pl.BlockSpec((None,) + tuple(w.shape[1:]), lambda i, k: (k, 0, 0)))
        args += [lhs, w]
    res, extra = _grid_call(body, name, (tiles, nk), in_specs, [pl.BlockSpec((tm, d_out), lambda i, k: (i, 0))],
                            [jax.ShapeDtypeStruct((tp, d_out), F32)], args, side)
    return res[0], extra


def _wide_matmul(pairs, name, tn, side=None):
    tp = pairs[0][0].shape[0]
    d_out = pairs[0][1].shape[2]
    tm = tp // MM_TILES
    npair = len(pairs)

    def body(*refs):
        acc = None
        for q in range(npair):
            t = _dot(refs[2 * q][...], refs[2 * q + 1][...])
            acc = t if acc is None else acc + t
        refs[2 * npair][...] = acc

    in_specs, args = [], []
    for lhs, w in pairs:
        kdim = lhs.shape[1]
        in_specs += [pl.BlockSpec((tm, kdim), lambda n, i: (i, 0)), pl.BlockSpec((kdim, tn), lambda n, i: (0, n))]
        args += [lhs, w.reshape(kdim, d_out)]
    res, extra = _grid_call(body, name, (d_out // tn, MM_TILES), in_specs, [pl.BlockSpec((tm, tn), lambda n, i: (i, n))],
                            [jax.ShapeDtypeStruct((tp, d_out), F32)], args, side)
    return res[0], extra


def _wgrad_call(x, y, name, x_width=None, y_width=None, tile_x=None, tile_y=None, side=None):
    tp = x.shape[1] if x.ndim == 3 else x.shape[0]

    def spec(a, width, tile):
        cols = a.shape[2] if a.ndim == 3 else (a.shape[1] if width is None else width)
        tc = cols if tile is None else tile
        per = cols // tc
        if a.ndim == 3:
            return pl.BlockSpec((None, tp, tc), lambda k, t: (k, 0, t if tile else 0)), cols, per
        if width is None:
            return pl.BlockSpec((tp, tc), lambda k, t: (0, t if tile else 0)), cols, per
        return pl.BlockSpec((tp, tc), lambda k, t: (0, k * per + (t if tile else 0))), cols, per

    xs, p, nx = spec(x, x_width, tile_x)
    ys, q, ny = spec(y, y_width, tile_y)
    nt = nx * ny
    if tile_x:
        ospec = pl.BlockSpec((None, tile_x, q), lambda k, t: (k, t, 0))
    else:
        ospec = pl.BlockSpec((None, p, tile_y), lambda k, t: (k, 0, t))

    def body(x_ref, y_ref, o_ref):
        o_ref[...] = _dot_tn(x_ref[...], y_ref[...]).astype(BF)

    res, extra = _grid_call(body, name, (N_CHIP, nt), [xs, ys], [ospec], [jax.ShapeDtypeStruct((N_CHIP, p, q), BF)],
                            (x, y), side)
    return res[0], extra


def _row_call(body, name, tp, d, row_ins, vec_ins, row_out_dtypes, n_acc, side=None):
    te = tp // EW_TILES
    rspec = pl.BlockSpec((te, d), lambda i: (i, 0))
    vspec = pl.BlockSpec((1, d), lambda i: (0, 0))
    res, extra = _grid_call(
        body, name, (EW_TILES,), [rspec] * len(row_ins) + [vspec] * len(vec_ins),
        [rspec] * len(row_out_dtypes) + [vspec] * n_acc,
        [jax.ShapeDtypeStruct((tp, d), dt) for dt in row_out_dtypes] + [jax.ShapeDtypeStruct((1, d), F32)] * n_acc,
        (*row_ins, *vec_ins), side)
    return res if side is None else (res, extra)


def _norm0(h, g):
    tp, d = h.shape

    def body(h_ref, g_ref, n_ref):
        n_ref[...] = _rms(h_ref[...], g_ref[...]).astype(BF)

    return _row_call(body, "norm0", tp, d, [h], [g], [BF], 0)[0]


def _post_fwd(f, h, g_post, g_next, scale, name):
    tp, d = h.shape

    def body(f_ref, h_ref, gp_ref, gn_ref, hn_ref, n_ref):
        hn = h_ref[...] + scale * _rms(f_ref[...], gp_ref[...])
        hn_ref[...] = hn
        n_ref[...] = _rms(hn, gn_ref[...]).astype(BF)

    return _row_call(body, name, tp, d, [f, h], [g_post, g_next], [F32, BF], 0)


def _loss_bwd(f, h, tgt, g_post, t_real):
    tp, d = h.shape
    te = tp // EW_TILES

    def body(f_ref, h_ref, t_ref, gp_ref, dh_ref, df_ref, dg_ref, loss_ref):
        i = pl.program_id(0)

        @pl.when(i == 0)
        def _():
            dg_ref[...] = jnp.zeros_like(dg_ref)
            loss_ref[...] = jnp.zeros_like(loss_ref)

        f = f_ref[...]
        gp = gp_ref[...]
        h3 = h_ref[...] + 0.5 * _rms(f, gp)
        rows = i * te + lax.broadcasted_iota(jnp.int32, (te, 1), 0)
        real = (rows >= N_META) & (rows < t_real)
        e = jnp.where(real, h3 - t_ref[...], 0.0)
        loss_ref[...] += 0.5 * jnp.sum(jnp.sum(e * e, axis=1, keepdims=True), axis=0, keepdims=True) / d
        dh = e / d
        dh_ref[...] = dh
        dfv, dgr = _rms_bwd(f, gp, 0.5 * dh)
        df_ref[...] = dfv.astype(BF)
        dg_ref[...] += jnp.sum(dgr, axis=0, keepdims=True)

    rspec = pl.BlockSpec((te, d), lambda i: (i, 0))
    vspec = pl.BlockSpec((1, d), lambda i: (0, 0))
    return pl.pallas_call(
        body, name="loss_bwd", grid=(EW_TILES,),
        in_specs=[rspec, rspec, rspec, vspec],
        out_specs=[rspec, rspec, vspec, pl.BlockSpec((1, 1), lambda i: (0, 0))],
        out_shape=[jax.ShapeDtypeStruct((tp, d), F32), jax.ShapeDtypeStruct((tp, d), BF),
                   jax.ShapeDtypeStruct((1, d), F32), jax.ShapeDtypeStruct((1, 1), F32)],
        compiler_params=_params(("arbitrary",)),
    )(f, h, tgt, g_post)


def _pre_bwd(dn, h, dh_out, g_pre, name, chain=None, side=None):
    tp, d = h.shape

    def body(*refs):
        if chain is None:
            dn_ref, h_ref, dho_ref, g_ref, dh_ref, dg_ref = refs
        else:
            dn_ref, h_ref, dho_ref, p_ref, g_ref, gp_ref, dh_ref, dp_ref, dg_ref, dgp_ref = refs
        i = pl.program_id(0)

        @pl.when(i == 0)
        def _():
            dg_ref[...] = jnp.zeros_like(dg_ref)
            if chain is not None:
                dgp_ref[...] = jnp.zeros_like(dgp_ref)

        dx, dgr = _rms_bwd(h_ref[...], g_ref[...], dn_ref[...])
        dh = dho_ref[...] + dx
        dh_ref[...] = dh
        dg_ref[...] += jnp.sum(dgr, axis=0, keepdims=True)
        if chain is not None:
            dp, dgpr = _rms_bwd(p_ref[...], gp_ref[...], chain[2] * dh)
            dp_ref[...] = dp.astype(BF)
            dgp_ref[...] += jnp.sum(dgpr, axis=0, keepdims=True)

    if chain is None:
        return _row_call(body, name, tp, d, [dn, h, dh_out], [g_pre], [F32], 1, side)
    return _row_call(body, name, tp, d, [dn, h, dh_out, chain[0]], [g_pre, chain[1]], [F32, BF], 2, side)


def _gelu(y):
    c = math.sqrt(2.0 / math.pi)
    return 0.5 * y * (1.0 + jnp.tanh(c * (y + 0.044715 * y * y * y)))


def _gelu_and_grad(y):
    c = math.sqrt(2.0 / math.pi)
    y2 = y * y
    t = jnp.tanh(c * y * (1.0 + 0.044715 * y2))
    half = 0.5 * (1.0 + t)
    return y * half, half + 0.5 * y * (1.0 - t * t) * c * (1.0 + 3.0 * 0.044715 * y2)


def _neg_expm1(x):
    p = 1.0 + x * (1.0 / 9.0)
    for n in (8.0, 7.0, 6.0, 5.0, 4.0, 3.0, 2.0):
        p = 1.0 + x * (1.0 / n) * p
    return -jnp.where(x > -0.35, x * p, jnp.exp(x) - 1.0)


def _softplus(x):
    e = jnp.exp(-jnp.abs(x))
    w = 1.0 + e
    l1p = jnp.where(w == 1.0, e, jnp.log(w) * (e / jnp.where(w == 1.0, 1.0, w - 1.0)))
    return jnp.maximum(x, 0.0) + l1p


def _group_mean(v, gm):
    hi = v.astype(BF)
    lo = (v - hi.astype(F32)).astype(BF)
    return _dot(hi, gm) + _dot(lo, gm)


def _shift_dn(win, s, r):
    if s == 0:
        return win[8:8 + r]
    return pltpu.roll(win, s, 0)[8:8 + r]


def _shift_up(win, s, r):
    if s == 0:
        return win[0:r]
    return pltpu.roll(win, r + 8 - s, 0)[0:r]


def _window_dn(ref, t0, r, first):
    if first:
        return jnp.concatenate([jnp.zeros((8, ref.shape[1]), F32), ref[0:r, :]], axis=0)
    return ref[pl.ds(t0 - 8, r + 8), :]


def _tile_scan(a, u, reverse):
    r = a.shape[0]
    rid = lax.broadcasted_iota(jnp.int32, a.shape, 0) & 7
    for dlt in (1, 2, 4):
        sh = (r - dlt) if reverse else dlt
        a_s = pltpu.roll(a, sh, 0)
        u_s = pltpu.roll(u, sh, 0)
        keep = (rid + dlt <= 7) if reverse else (rid >= dlt)
        u = jnp.where(keep, u + a * u_s, u)
        a = jnp.where(keep, a * a_s, a)
    return a, u


def _lru_gates(xc, wa, ba, wx, bx, sp):
    xb = xc.astype(BF)
    ga = _sigmoid(_dot(xb, wa) + ba)
    gx = _sigmoid(_dot(xb, wx) + bx)
    la = -LRU_C * ga * sp
    return ga, gx, la


def _conv4(win, w4, cb, r):
    return (cb + w4[3:4] * _shift_dn(win, 0, r) + w4[2:3] * _shift_dn(win, 1, r)
            + w4[1:2] * _shift_dn(win, 2, r) + w4[0:1] * _shift_dn(win, 3, r))


def _lru_fwd(z, w4, cb, wa2, ba, wx2, bx, lam, g_out, gm, side=None):
    tp = z.shape[0]
    dl = cb.shape[1]
    nb = dl // LANE
    r = tp // MIX_CHUNKS
    c = LANE

    def body(y_ref, x_ref, w4_ref, cb_ref, wa_ref, ba_ref, wx_ref, bx_ref, lam_ref, go_ref, gm_ref, m_ref, hs_ref):
        w4v = w4_ref[...]
        cbv = cb_ref[...]
        wa = wa_ref[...]
        wx = wx_ref[...]
        bav = ba_ref[...]
        bxv = bx_ref[...]
        gov = go_ref[...]
        gmv = gm_ref[...]
        sp = _softplus(-lam_ref[...])

        def chunk(t0, hprev, first):
            win = _window_dn(x_ref, t0, r, first)
            xc = _conv4(win, w4v, cbv, r)
            ga, gx, la = _lru_gates(xc, wa, bav, wx, bxv, sp)
            a = jnp.exp(la)
            u = jnp.sqrt(_neg_expm1(2.0 * la)) * gx * xc
            ac, uc = _tile_scan(a, u, False)
            for j in range(r // 8):
                hj = uc[8 * j:8 * j + 8] + ac[8 * j:8 * j + 8] * hprev
                hs_ref[pl.ds(t0 + 8 * j, 8), :] = hj
                hprev = jnp.broadcast_to(hj[7:8], (8, c))
            h = hs_ref[pl.ds(t0, r), :]
            lo = h * _gelu(y_ref[pl.ds(t0, r), :])
            rs = lax.rsqrt(_group_mean(lo * lo, gmv) + EPS)
            m_ref[pl.ds(t0, r), :] = (lo * rs * gov).astype(BF)
            return hprev

        hp = chunk(0, jnp.zeros((8, c), F32), True)

        def loop(ci, hp):
            return chunk(pl.multiple_of(ci * r, 16), hp, False)

        lax.fori_loop(1, MIX_CHUNKS, loop, hp)

    col = lambda off: pl.BlockSpec((tp, c), lambda j: (0, off + j))
    vec = pl.BlockSpec((1, c), lambda j: (0, j))
    return _grid_call(
        body, "lru_fwd", (nb,),
        [col(0), col(nb), pl.BlockSpec((8, c), lambda j: (0, j)), vec, pl.BlockSpec((None, c, c), lambda j: (j, 0, 0)),
         vec, pl.BlockSpec((None, c, c), lambda j: (j, 0, 0)), vec, vec, vec, pl.BlockSpec((c, c), lambda j: (0, 0))],
        [col(0), col(0)], [jax.ShapeDtypeStruct((tp, dl), BF), jax.ShapeDtypeStruct((tp, dl), F32)],
        (z, z, w4, cb, wa2, ba, wx2, bx, lam, g_out, gm), side)


def _lru_bwd(z, hs, dmix, w4, cb, wa2, ba, wx2, bx, lam, g_out, gm, side=None):
    tp = z.shape[0]
    dl = cb.shape[1]
    nb = dl // LANE
    r = tp // MIX_CHUNKS
    c = LANE

    def body(y_ref, x_ref, hs_ref, dm_ref, w4_ref, cb_ref, wa_ref, ba_ref, wx_ref, bx_ref, lam_ref, go_ref, gm_ref,
             dy_ref, dx_ref, small_ref, dwa_ref, dwx_ref, xc_buf, ga_buf, gx_buf, a_buf, dh_buf, dxc_buf):
        w4v = w4_ref[...]
        cbv = cb_ref[...]
        wa = wa_ref[...]
        wx = wx_ref[...]
        bav = ba_ref[...]
        bxv = bx_ref[...]
        gov = go_ref[...]
        gmv = gm_ref[...]
        lamv = lam_ref[...]
        sp = _softplus(-lamv)
        small_ref[...] = jnp.zeros_like(small_ref)
        dwa_ref[...] = jnp.zeros_like(dwa_ref)
        dwx_ref[...] = jnp.zeros_like(dwx_ref)
        a_buf[pl.ds(tp, 8), :] = jnp.zeros((8, c), F32)
        dxc_buf[pl.ds(tp, 8), :] = jnp.zeros((8, c), F32)

        def fwd_chunk(t0, first):
            win = _window_dn(x_ref, t0, r, first)
            xc = _conv4(win, w4v, cbv, r)
            ga, gx, la = _lru_gates(xc, wa, bav, wx, bxv, sp)
            xc_buf[pl.ds(t0, r), :] = xc
            ga_buf[pl.ds(t0, r), :] = ga
            gx_buf[pl.ds(t0, r), :] = gx
            a_buf[pl.ds(t0, r), :] = jnp.exp(la)
            h = hs_ref[pl.ds(t0, r), :]
            yv = y_ref[pl.ds(t0, r), :]
            ge, dge = _gelu_and_grad(yv)
            lo = h * ge
            rs = lax.rsqrt(_group_mean(lo * lo, gmv) + EPS)
            xh = lo * rs
            dm = dm_ref[pl.ds(t0, r), :]
            q = dm * gov
            dlo = rs * (q - xh * _group_mean(q * xh, gmv))
            small_ref[8:9, :] += jnp.sum(dm * xh, axis=0, keepdims=True)
            dh_buf[pl.ds(t0, r), :] = dlo * ge
            dy_ref[pl.ds(t0, r), :] = (dlo * h * dge).astype(BF)

        fwd_chunk(0, True)

        def floop(ci, carry):
            fwd_chunk(pl.multiple_of(ci * r, 16), False)
            return carry

        lax.fori_loop(1, MIX_CHUNKS, floop, 0)

        def bwd_chunk(t0, vnext, first):
            ap = _shift_up(a_buf[pl.ds(t0, r + 8), :], 1, r)
            ac, uc = _tile_scan(ap, dh_buf[pl.ds(t0, r), :], True)
            for j in reversed(range(r // 8)):
                vj = uc[8 * j:8 * j + 8] + ac[8 * j:8 * j + 8] * vnext
                dh_buf[pl.ds(t0 + 8 * j, 8), :] = vj
                vnext = jnp.broadcast_to(vj[0:1], (8, c))
            v = dh_buf[pl.ds(t0, r), :]
            hprev = _shift_dn(_window_dn(hs_ref, t0, r, first), 1, r)
            xc = xc_buf[pl.ds(t0, r), :]
            ga = ga_buf[pl.ds(t0, r), :]
            gx = gx_buf[pl.ds(t0, r), :]
            a = a_buf[pl.ds(t0, r), :]
            em = _neg_expm1(-2.0 * LRU_C * ga * sp)
            mult = jnp.sqrt(em)
            dla = v * hprev * a - (v * gx * xc) * ((1.0 - em) / mult)
            dgx = v * mult * xc
            dxc = v * mult * gx
            dga = dla * (-LRU_C) * sp
            small_ref[7:8, :] += jnp.sum(dla * (-LRU_C) * ga, axis=0, keepdims=True)
            dpa = dga * ga * (1.0 - ga)
            dpx = dgx * gx * (1.0 - gx)
            small_ref[5:6, :] += jnp.sum(dpa, axis=0, keepdims=True)
            small_ref[6:7, :] += jnp.sum(dpx, axis=0, keepdims=True)
            dpab = dpa.astype(BF)
            dpxb = dpx.astype(BF)
            xb = xc.astype(BF)
            dxc = dxc + _dot_nt(dpab, wa) + _dot_nt(dpxb, wx)
            dwa_ref[...] += _dot_tn(xb, dpab)
            dwx_ref[...] += _dot_tn(xb, dpxb)
            dxc_buf[pl.ds(t0, r), :] = dxc
            small_ref[4:5, :] += jnp.sum(dxc, axis=0, keepdims=True)
            dwin = dxc_buf[pl.ds(t0, r + 8), :]
            dx_ref[pl.ds(t0, r), :] = (w4v[3:4] * dxc + w4v[2:3] * _shift_up(dwin, 1, r)
                                       + w4v[1:2] * _shift_up(dwin, 2, r) + w4v[0:1] * _shift_up(dwin, 3, r)).astype(BF)
            xwin = _window_dn(x_ref, t0, r, first)
            for k in range(4):
                small_ref[k:k + 1, :] += jnp.sum(dxc * _shift_dn(xwin, 3 - k, r), axis=0, keepdims=True)
            return vnext

        def bloop(it, vnext):
            ci = MIX_CHUNKS - 1 - it
            return bwd_chunk(pl.multiple_of(ci * r, 16), vnext, False)

        vn = lax.fori_loop(0, MIX_CHUNKS - 1, bloop, jnp.zeros((8, c), F32))
        bwd_chunk(0, vn, True)
        small_ref[7:8, :] = small_ref[7:8, :] * (-_sigmoid(-lamv))

    col = lambda off: pl.BlockSpec((tp, c), lambda j: (0, off + j))
    vec = pl.BlockSpec((1, c), lambda j: (0, j))
    mat = pl.BlockSpec((None, c, c), lambda j: (j, 0, 0))
    buf = pltpu.VMEM((tp, c), F32)
    bufp = pltpu.VMEM((tp + 8, c), F32)
    return _grid_call(
        body, "lru_bwd", (nb,),
        [col(0), col(nb), col(0), col(0), pl.BlockSpec((8, c), lambda j: (0, j)), vec, mat, vec, mat, vec, vec, vec,
         pl.BlockSpec((c, c), lambda j: (0, 0))],
        [col(0), col(0), pl.BlockSpec((16, c), lambda j: (0, j)), mat, mat],
        [jax.ShapeDtypeStruct((tp, dl), BF), jax.ShapeDtypeStruct((tp, dl), BF), jax.ShapeDtypeStruct((16, dl), F32),
         jax.ShapeDtypeStruct((nb, c, c), F32), jax.ShapeDtypeStruct((nb, c, c), F32)],
        (z, z, hs, dmix, w4, cb, wa2, ba, wx2, bx, lam, g_out, gm), side, [buf, buf, buf, bufp, buf, bufp])


def _sc_conv(cvwin, w3, r):
    return w3[2:3] * _shift_dn(cvwin, 0, r) + w3[1:2] * _shift_dn(cvwin, 1, r) + w3[0:1] * _shift_dn(cvwin, 2, r)


def _sc_fwd(z, w3, g_out, gm, dl, side=None):
    tp = z.shape[0]
    nb = dl // LANE
    r = tp // MIX_CHUNKS
    c = LANE

    def body(b_ref, c_ref, v_ref, w3_ref, go_ref, gm_ref, m_ref):
        w3v = w3_ref[...]
        gov = go_ref[...]
        gmv = gm_ref[...]

        def chunk(t0, first):
            cvwin = _window_dn(c_ref, t0, r, first) * _window_dn(v_ref, t0, r, first)
            so = b_ref[pl.ds(t0, r), :] * _sc_conv(cvwin, w3v, r)
            rs = lax.rsqrt(_group_mean(so * so, gmv) + EPS)
            m_ref[pl.ds(t0, r), :] = (so * rs * gov).astype(BF)

        chunk(0, True)

        def loop(ci, carry):
            chunk(pl.multiple_of(ci * r, 16), False)
            return carry

        lax.fori_loop(1, MIX_CHUNKS, loop, 0)

    col = lambda off: pl.BlockSpec((tp, c), lambda j: (0, off + j))
    res, extra = _grid_call(
        body, "sconv_fwd", (nb,),
        [col(2 * nb), col(3 * nb), col(4 * nb), pl.BlockSpec((8, c), lambda j: (0, j)),
         pl.BlockSpec((1, c), lambda j: (0, j)), pl.BlockSpec((c, c), lambda j: (0, 0))],
        [col(0)], [jax.ShapeDtypeStruct((tp, dl), BF)], (z, z, z, w3, g_out, gm), side)
    return res[0], extra


def _sc_bwd(z, dmix, w3, g_out, gm, dl, side=None):
    tp = z.shape[0]
    nb = dl // LANE
    r = tp // MIX_CHUNKS
    c = LANE

    def body(b_ref, c_ref, v_ref, dm_ref, w3_ref, go_ref, gm_ref, db_ref, dc_ref, dv_ref, small_ref, dsc_buf):
        w3v = w3_ref[...]
        gov = go_ref[...]
        gmv = gm_ref[...]
        small_ref[...] = jnp.zeros_like(small_ref)
        dsc_buf[pl.ds(tp, 8), :] = jnp.zeros((8, c), F32)

        def chunk1(t0, first):
            cvwin = _window_dn(c_ref, t0, r, first) * _window_dn(v_ref, t0, r, first)
            sc = _sc_conv(cvwin, w3v, r)
            bv = b_ref[pl.ds(t0, r), :]
            so = bv * sc
            rs = lax.rsqrt(_group_mean(so * so, gmv) + EPS)
            xh = so * rs
            dm = dm_ref[pl.ds(t0, r), :]
            q = dm * gov
            dso = rs * (q - xh * _group_mean(q * xh, gmv))
            small_ref[3:4, :] += jnp.sum(dm * xh, axis=0, keepdims=True)
            db_ref[pl.ds(t0, r), :] = (dso * sc).astype(BF)
            dsc = dso * bv
            dsc_buf[pl.ds(t0, r), :] = dsc
            for k in range(3):
                small_ref[k:k + 1, :] += jnp.sum(dsc * _shift_dn(cvwin, 2 - k, r), axis=0, keepdims=True)

        chunk1(0, True)

        def loop1(ci, carry):
            chunk1(pl.multiple_of(ci * r, 16), False)
            return carry

        lax.fori_loop(1, MIX_CHUNKS, loop1, 0)

        def loop2(ci, carry):
            t0 = pl.multiple_of(ci * r, 16)
            dwin = dsc_buf[pl.ds(t0, r + 8), :]
            dcv = w3v[2:3] * _shift_up(dwin, 0, r) + w3v[1:2] * _shift_up(dwin, 1, r) + w3v[0:1] * _shift_up(dwin, 2, r)
            dc_ref[pl.ds(t0, r), :] = (dcv * v_ref[pl.ds(t0, r), :]).astype(BF)
            dv_ref[pl.ds(t0, r), :] = (dcv * c_ref[pl.ds(t0, r), :]).astype(BF)
            return carry

        lax.fori_loop(0, MIX_CHUNKS, loop2, 0)

    col = lambda off: pl.BlockSpec((tp, c), lambda j: (0, off + j))
    out = jax.ShapeDtypeStruct((tp, dl), BF)
    return _grid_call(
        body, "sconv_bwd", (nb,),
        [col(2 * nb), col(3 * nb), col(4 * nb), col(nb), pl.BlockSpec((8, c), lambda j: (0, j)),
         pl.BlockSpec((1, c), lambda j: (0, j)), pl.BlockSpec((c, c), lambda j: (0, 0))],
        [col(0), col(0), col(0), pl.BlockSpec((8, c), lambda j: (0, j))],
        [out, out, out, jax.ShapeDtypeStruct((8, dl), F32)], (z, z, z, dmix, w3, g_out, gm), side,
        [pltpu.VMEM((tp + 8, c), F32)])


def _cast_pad(w, rows_p, cols_p, chip, name):
    r, c = w.shape
    assert cols_p == c and c % CAST_COLS == 0, (w.shape, rows_p, cols_p)

    def body(chip_ref, w_ref, o_ref):
        if rows_p != r:
            o_ref[...] = jnp.zeros_like(o_ref)
        o_ref[0:r, :] = w_ref[...].astype(BF)

    return pl.pallas_call(
        body, name=name, out_shape=jax.ShapeDtypeStruct((N_CHIP, rows_p, cols_p), BF),
        grid_spec=pltpu.PrefetchScalarGridSpec(
            num_scalar_prefetch=1, grid=(c // CAST_COLS,),
            in_specs=[pl.BlockSpec((r, CAST_COLS), lambda j, chip: (0, j))],
            out_specs=pl.BlockSpec((None, rows_p, CAST_COLS), lambda j, chip: (chip[0], 0, j))),
        compiler_params=_params(("arbitrary",)),
    )(chip, w)


def _adamw_math(w, g, m, v):
    m2 = ADAM_B1 * m + (1.0 - ADAM_B1) * g
    v2 = ADAM_B2 * v + (1.0 - ADAM_B2) * (g * g)
    m_hat = m2 / (1.0 - ADAM_B1 ** ADAM_STEP)
    v_hat = v2 / (1.0 - ADAM_B2 ** ADAM_STEP)
    delta = -ADAM_LR * (m_hat / (jnp.sqrt(v_hat) + ADAM_EPS) + ADAM_WD * w)
    return delta, m2, v2


def _adamw(w, g, m, v, name, row_tiles, col_tiles, side=None):
    r, c = w.shape
    tr = r // row_tiles
    tc = c // col_tiles
    gc = g.shape[1] if col_tiles == 1 else tc

    def body(w_ref, g_ref, m_ref, v_ref, go_ref, d_ref, mo_ref, vo_ref):
        gv = g_ref[...][:, 0:tc]
        delta, m2, v2 = _adamw_math(w_ref[...], gv, m_ref[...], v_ref[...])
        go_ref[...] = gv
        d_ref[...] = delta
        mo_ref[...] = m2
        vo_ref[...] = v2

    spec = pl.BlockSpec((tr, tc), lambda i, j: (i, j))
    out = jax.ShapeDtypeStruct((r, c), F32)
    return _grid_call(body, name, (row_tiles, col_tiles), [spec, pl.BlockSpec((tr, gc), lambda i, j: (i, j)), spec, spec],
                      [spec] * 4, [out] * 4, (w, g, m, v), side)


def _adamw_small(w, g_top, g4, m, v):
    def body(w_ref, gt_ref, g_ref, m_ref, v_ref, go_ref, d_ref, mo_ref, vo_ref):
        g = jnp.concatenate([gt_ref[...], (g_ref[0] + g_ref[1]) + (g_ref[2] + g_ref[3])], axis=0)
        delta, m2, v2 = _adamw_math(w_ref[...], g, m_ref[...], v_ref[...])
        go_ref[...] = g
        d_ref[...] = delta
        mo_ref[...] = m2
        vo_ref[...] = v2

    out = jax.ShapeDtypeStruct(w.shape, F32)
    spec = pl.BlockSpec(w.shape, lambda: (0, 0))
    return pl.pallas_call(
        body, name="adamw_small",
        in_specs=[spec, pl.BlockSpec(g_top.shape, lambda: (0, 0)), pl.BlockSpec(g4.shape, lambda: (0, 0, 0)), spec, spec],
        out_specs=[spec] * 4, out_shape=[out] * 4, compiler_params=_params())(w, g_top, g4, m, v)


def _place():
    x, y, c = lax.axis_index("x"), lax.axis_index("y"), lax.axis_index("c")
    chips = [(1 - x, y), (x, 1 - y), (1 - x, 1 - y)]
    return x, y, c, chips


ANY = pl.BlockSpec(memory_space=pl.ANY)


REL_SLOT = (2, 1, 3)


def _gather_side(bufs, relative=False, two_path=False):
    n = len(bufs)
    direct = (0, 1) if two_path else (0, 1, 2)

    def copies(outs, sems):
        s_ici, r_ici, s_d2d, r_d2d = sems[:4]
        x, y, c, chips = _place()
        me = 2 * x + y

        def rows(w, slot, core, part=None):
            half = bufs[w].shape[1] // 2
            if part is None:
                return outs[w].at[slot, pl.ds(core * half, half)]
            return outs[w].at[slot, pl.ds(core * half + part * (half // 2), half // 2)]

        def theirs(j):
            return REL_SLOT[j] if relative else 2 * chips[j][0] + chips[j][1]

        def ici_send(w, j):
            px, py = chips[j]
            return pltpu.make_async_remote_copy(
                src_ref=rows(w, 0 if relative else me, c), dst_ref=rows(w, REL_SLOT[j] if relative else me, c),
                send_sem=s_ici.at[w, j], recv_sem=r_ici.at[w, j], device_id=(px, py, c), device_id_type=MESH)

        def ici_recv(w, j):
            px, py = chips[j]
            return pltpu.make_async_remote_copy(
                src_ref=rows(w, theirs(j), c), dst_ref=rows(w, theirs(j), c),
                send_sem=s_ici.at[w, j], recv_sem=r_ici.at[w, j], device_id=(px, py, c), device_id_type=MESH)

        def hop_send(w, p):
            px, py = chips[1 - p]
            return pltpu.make_async_remote_copy(
                src_ref=rows(w, theirs(p), c, p), dst_ref=rows(w, REL_SLOT[2] if relative else theirs(p), c, p),
                send_sem=sems[4].at[w, p], recv_sem=sems[5].at[w, p], device_id=(px, py, c), device_id_type=MESH)

        def hop_recv(w, p):
            px, py = chips[1 - p]
            return pltpu.make_async_remote_copy(
                src_ref=rows(w, theirs(2), c, p), dst_ref=rows(w, theirs(2), c, p),
                send_sem=sems[4].at[w, p], recv_sem=sems[5].at[w, p], device_id=(px, py, c), device_id_type=MESH)

        def d2d(w, j, core):
            return pltpu.make_async_remote_copy(
                src_ref=rows(w, theirs(j), core), dst_ref=rows(w, theirs(j), core),
                send_sem=s_d2d.at[w, j], recv_sem=r_d2d.at[w, j], device_id=(x, y, 1 - c), device_id_type=MESH)

        return c, ici_send, ici_recv, hop_send, hop_recv, d2d

    def send(outs, sems):
        c, ici_send, ici_recv, hop_send, hop_recv, d2d = copies(outs, sems)
        for j in direct:
            for w in range(n):
                ici_send(w, j).start()

    def arrived(j, outs, sems):
        c, ici_send, ici_recv, hop_send, hop_recv, d2d = copies(outs, sems)
        for w in range(n):
            if j in direct:
                ici_recv(w, j).wait_recv()
                if two_path:
                    hop_send(w, j).start()
            else:
                hop_recv(w, 0).wait_recv()
                hop_recv(w, 1).wait_recv()
            d2d(w, j, c).start()

    def forwarded(j, outs, sems):
        c, ici_send, ici_recv, hop_send, hop_recv, d2d = copies(outs, sems)
        for w in range(n):
            d2d(w, j, 1 - c).wait_recv()

    def drain(outs, sems):
        c, ici_send, ici_recv, hop_send, hop_recv, d2d = copies(outs, sems)
        for w in range(n):
            for j in direct:
                ici_send(w, j).wait_send()
                if two_path:
                    hop_send(w, j).wait_send()
            for j in range(3):
                d2d(w, j, c).wait_send()

    def start(ins, outs, sems):
        send(outs, sems)

    def middle(ins, outs, sems):
        arrived(0, outs, sems)
        arrived(1, outs, sems)

    def rest(ins, outs, sems):
        arrived(2, outs, sems)
        for j in range(3):
            forwarded(j, outs, sems)
        drain(outs, sems)

    def finish(ins, outs, sems):
        middle(ins, outs, sems)
        rest(ins, outs, sems)

    dma = pltpu.SemaphoreType.DMA((n, 3))
    hop = [pltpu.SemaphoreType.DMA((n, 2))] * 2 if two_path else []
    side = _Side(list(bufs), [jax.ShapeDtypeStruct(b.shape, b.dtype) for b in bufs], {w: w for w in range(n)},
                 [dma, dma, dma, dma] + hop, start, finish)
    side.send, side.arrived, side.forwarded, side.drain = send, arrived, forwarded, drain
    side.middle, side.rest = middle, rest
    return side


def _run_side(side, name):
    sin, sout = len(side.ins), len(side.outs)

    def body(*refs):
        ins, outs, sems = refs[:sin], refs[sin:sin + sout], refs[sin + sout:]
        side.start(ins, outs, sems)
        side.finish(ins, outs, sems)

    return pl.pallas_call(
        body, name=name, out_shape=list(side.outs), in_specs=[ANY] * sin, out_specs=[ANY] * sout,
        scratch_shapes=list(side.sems), input_output_aliases=dict(side.alias))(*side.ins)


def _pair_exchange_side(grads):
    n = len(grads)

    def copies(ins, outs, sems):
        ssem, rsem = sems
        x, y, c, _ = _place()
        cps = []
        for w in range(n):
            half = grads[w].shape[1] // 2
            cps.append(pltpu.make_async_remote_copy(
                src_ref=ins[w].at[:, pl.ds((1 - c) * half, half)], dst_ref=outs[w],
                send_sem=ssem.at[w], recv_sem=rsem.at[w], device_id=(x, y, 1 - c), device_id_type=MESH))
        return cps

    def start(ins, outs, sems):
        for cp in copies(ins, outs, sems):
            cp.start()

    def finish(ins, outs, sems):
        for cp in copies(ins, outs, sems):
            cp.wait()

    dma = pltpu.SemaphoreType.DMA((n,))
    return _Side(list(grads), [jax.ShapeDtypeStruct((N_CHIP, g.shape[1] // 2, g.shape[2]), BF) for g in grads], {},
                 [dma, dma], start, finish)


def _sibling_copy_side(buf):
    def copy(ins, outs, sems):
        x, y, c, _ = _place()
        return pltpu.make_async_remote_copy(src_ref=ins[0], dst_ref=outs[0], send_sem=sems[0], recv_sem=sems[1],
                                            device_id=(x, y, 1 - c), device_id_type=MESH)

    return _Side([buf], [jax.ShapeDtypeStruct(buf.shape, buf.dtype)], {}, [pltpu.SemaphoreType.DMA, pltpu.SemaphoreType.DMA],
                 lambda i, o, s: copy(i, o, s).start(), lambda i, o, s: copy(i, o, s).wait())


def _slot_exchange_side(buf4):
    def copies(outs, sems, sending):
        ssem, rsem = sems
        x, y, c, chips = _place()
        me = 2 * x + y
        return [pltpu.make_async_remote_copy(
            src_ref=outs[0].at[me if sending else 2 * px + py], dst_ref=outs[0].at[me if sending else 2 * px + py],
            send_sem=ssem.at[j], recv_sem=rsem.at[j], device_id=(px, py, c), device_id_type=MESH)
            for j, (px, py) in enumerate(chips)]

    def start(ins, outs, sems):
        for cp in copies(outs, sems, True):
            cp.start()

    def finish(ins, outs, sems):
        for cp in copies(outs, sems, False):
            cp.wait_recv()
        for cp in copies(outs, sems, True):
            cp.wait_send()

    dma = pltpu.SemaphoreType.DMA((3,))
    return _Side([buf4], [jax.ShapeDtypeStruct(buf4.shape, buf4.dtype)], {0: 0}, [dma, dma], start, finish)


def _pair_sum(g, sib, core, name):
    _, r, cdim = g.shape
    half = r // 2

    def body(core_ref, g_ref, s_ref, o_ref):
        o_ref[...] = (g_ref[...].astype(F32) + s_ref[...].astype(F32)).astype(BF)

    return pl.pallas_call(
        body, name=name,
        grid_spec=pltpu.PrefetchScalarGridSpec(
            num_scalar_prefetch=1, grid=(N_CHIP,),
            in_specs=[pl.BlockSpec((None, half, cdim), lambda k, core: (k, core[0], 0)),
                      pl.BlockSpec((None, half, cdim), lambda k, core: (k, 0, 0))],
            out_specs=pl.BlockSpec((None, half, cdim), lambda k, core: (k, 0, 0))),
        out_shape=jax.ShapeDtypeStruct((N_CHIP, half, cdim), BF),
        compiler_params=_params(("arbitrary",)),
    )(core, g, sib)


def _chip_exchange_side(psums, relative=False):
    n = len(psums)

    def copies(ins, outs, sems):
        ssem, rsem = sems
        x, y, c, chips = _place()
        return [pltpu.make_async_remote_copy(
            src_ref=ins[w].at[REL_SLOT[j] if relative else 2 * px + py], dst_ref=outs[w].at[j],
            send_sem=ssem.at[w, j], recv_sem=rsem.at[w, j], device_id=(px, py, c), device_id_type=MESH)
            for w in range(n) for j, (px, py) in enumerate(chips)]

    def start(ins, outs, sems):
        for cp in copies(ins, outs, sems):
            cp.start()

    def finish(ins, outs, sems):
        for cp in copies(ins, outs, sems):
            cp.wait()

    dma = pltpu.SemaphoreType.DMA((n, 3))
    return _Side(list(psums), [jax.ShapeDtypeStruct((3,) + p.shape[1:], BF) for p in psums], {}, [dma, dma],
                 start, finish)


def _final_sum(g, sib, recv, sel, name):
    _, r, cdim = g.shape
    half = r // 2
    nt = 4
    th = half // nt

    def body(sel_ref, g_ref, s_ref, r_ref, o_ref):
        acc = g_ref[...].astype(F32) + s_ref[...].astype(F32)
        for j in range(3):
            acc = acc + r_ref[j].astype(F32)
        o_ref[...] = acc

    return pl.pallas_call(
        body, name=name,
        grid_spec=pltpu.PrefetchScalarGridSpec(
            num_scalar_prefetch=1, grid=(nt,),
            in_specs=[pl.BlockSpec((None, th, cdim), lambda i, sel: (sel[0], sel[1] * nt + i, 0)),
                      pl.BlockSpec((None, th, cdim), lambda i, sel: (sel[0], i, 0)),
                      pl.BlockSpec((3, th, cdim), lambda i, sel: (0, i, 0))],
            out_specs=pl.BlockSpec((th, cdim), lambda i, sel: (sel[1] * nt + i, 0))),
        out_shape=jax.ShapeDtypeStruct((r, cdim), F32),
        compiler_params=_params(("arbitrary",)),
    )(sel, g, sib, recv)


def _join_side(bufs):
    n = len(bufs)

    def copies(outs, sems, core_of):
        ssem, rsem = sems
        x, y, c, _ = _place()
        cps = []
        for w in range(n):
            half = bufs[w].shape[0] // 2
            rows = outs[w].at[pl.ds(core_of(c) * half, half)]
            cps.append(pltpu.make_async_remote_copy(
                src_ref=rows, dst_ref=rows, send_sem=ssem.at[w], recv_sem=rsem.at[w],
                device_id=(x, y, 1 - c), device_id_type=MESH))
        return cps

    def start(ins, outs, sems):
        for cp in copies(outs, sems, lambda c: c):
            cp.start()

    def finish(ins, outs, sems):
        for cp in copies(outs, sems, lambda c: 1 - c):
            cp.wait_recv()
        for cp in copies(outs, sems, lambda c: c):
            cp.wait_send()

    dma = pltpu.SemaphoreType.DMA((n,))
    return _Side(list(bufs), [jax.ShapeDtypeStruct(b.shape, F32) for b in bufs], {w: w for w in range(n)}, [dma, dma],
                 start, finish)


def _small_pair_sum(buf, sib, chip):
    rows, d = buf.shape

    def body(chip_ref, a_ref, b_ref, o_ref):
        o_ref[...] = a_ref[...] + b_ref[...]

    return pl.pallas_call(
        body, name="small_pair_sum", out_shape=jax.ShapeDtypeStruct((N_CHIP, rows, d), F32),
        grid_spec=pltpu.PrefetchScalarGridSpec(
            num_scalar_prefetch=1, grid=(1,),
            in_specs=[pl.BlockSpec((rows, d), lambda i, chip: (0, 0))] * 2,
            out_specs=pl.BlockSpec((None, rows, d), lambda i, chip: (chip[0], 0, 0))),
        compiler_params=_params(("arbitrary",)),
    )(chip, buf, sib)


def _small_all_reduce(buf, name):
    rows, d = buf.shape

    def body(in_ref, out_ref, sib, all4, ssem, rsem, psem, qsem):
        x, y, c, chips = _place()
        me = 2 * x + y
        to_sib = pltpu.make_async_remote_copy(src_ref=in_ref, dst_ref=sib, send_sem=ssem, recv_sem=rsem,
                                              device_id=(x, y, 1 - c), device_id_type=MESH)
        to_sib.start()
        to_sib.wait()
        all4[me] = in_ref[...] + sib[...]
        cps = [pltpu.make_async_remote_copy(src_ref=all4.at[me], dst_ref=all4.at[me], send_sem=psem.at[j],
                                            recv_sem=qsem.at[j], device_id=(px, py, c), device_id_type=MESH)
               for j, (px, py) in enumerate(chips)]
        for cp in cps:
            cp.start()
        for j, (px, py) in enumerate(chips):
            chip = 2 * px + py
            pltpu.make_async_remote_copy(src_ref=all4.at[chip], dst_ref=all4.at[chip], send_sem=psem.at[j],
                                         recv_sem=qsem.at[j], device_id=(px, py, c), device_id_type=MESH).wait_recv()
        for cp in cps:
            cp.wait_send()
        out_ref[...] = (all4[0] + all4[1]) + (all4[2] + all4[3])

    vm = pl.BlockSpec(memory_space=pltpu.VMEM)
    return pl.pallas_call(
        body, name=name, out_shape=jax.ShapeDtypeStruct((rows, d), F32),
        in_specs=[vm], out_specs=vm,
        scratch_shapes=[pltpu.VMEM((rows, d), F32), pltpu.VMEM((N_CHIP, rows, d), F32),
                        pltpu.SemaphoreType.DMA, pltpu.SemaphoreType.DMA,
                        pltpu.SemaphoreType.DMA((3,)), pltpu.SemaphoreType.DMA((3,))],
        compiler_params=_params(),
    )(buf)


def _pair_blocks(w):
    w4 = w.reshape(N_HEADS // 2, 2, HEAD, HEAD)
    eye = jnp.eye(2, dtype=w.dtype)
    return jnp.einsum("pirc,ij->pirjc", w4, eye).reshape(N_HEADS // 2, LANE, LANE)


def _unpair_blocks(w2):
    w5 = w2.reshape(N_HEADS // 2, 2, HEAD, 2, HEAD)
    return jnp.stack([w5[:, 0, :, 0, :], w5[:, 1, :, 1, :]], axis=1).reshape(N_HEADS, HEAD, HEAD)


def kernel(x, meta_tokens, ffn1_pre_g, ffn1_w_gate, ffn1_w_up, ffn1_w_down, ffn1_post_g, mix_pre_g, w_in, lru_conv_w, lru_conv_b, lru_w_a, lru_b_a, lru_w_x, lru_b_x, lru_lambda, sconv_w, lru_out_g, sconv_out_g, w_out, mix_post_g, ffn2_pre_g, ffn2_w_gate, ffn2_w_up, ffn2_w_down, ffn2_post_g, loss_target, m_meta_tokens, m_ffn1_pre_g, m_ffn1_w_gate, m_ffn1_w_up, m_ffn1_w_down, m_ffn1_post_g, m_mix_pre_g, m_w_in, m_lru_conv_w, m_lru_conv_b, m_lru_w_a, m_lru_b_a, m_lru_w_x, m_lru_b_x, m_lru_lambda, m_sconv_w, m_lru_out_g, m_sconv_out_g, m_w_out, m_mix_post_g, m_ffn2_pre_g, m_ffn2_w_gate, m_ffn2_w_up, m_ffn2_w_down, m_ffn2_post_g, v_meta_tokens, v_ffn1_pre_g, v_ffn1_w_gate, v_ffn1_w_up, v_ffn1_w_down, v_ffn1_post_g, v_mix_pre_g, v_w_in, v_lru_conv_w, v_lru_conv_b, v_lru_w_a, v_lru_b_a, v_lru_w_x, v_lru_b_x, v_lru_lambda, v_sconv_w, v_lru_out_g, v_sconv_out_g, v_w_out, v_mix_post_g, v_ffn2_pre_g, v_ffn2_w_gate, v_ffn2_w_up, v_ffn2_w_down, v_ffn2_post_g):
    seq, d = x.shape[1], x.shape[2]
    t_real = N_META + seq
    tp = _round_up(t_real, ROW_ALIGN)
    f4 = ffn1_w_gate.shape[2]
    f4p = _round_up(f4, LANE)
    dl = lru_conv_b.shape[1]
    cin = w_in.shape[2]
    xi, yi, ci = lax.axis_index("x"), lax.axis_index("y"), lax.axis_index("c")
    chip = 2 * xi + yi
    zero = jnp.zeros((), jnp.int32)

    transposed = ("ffn1_w_gate", "ffn1_w_up", "ffn2_w_gate", "ffn2_w_up")

    def view(k, a):
        return a[0].T if k in transposed else a[0]

    def unview(k, a):
        return (a.T if k in transposed else a)[None]

    big = {
        "ffn1_w_gate": (view("ffn1_w_gate", ffn1_w_gate), f4p, d), "ffn1_w_up": (view("ffn1_w_up", ffn1_w_up), f4p, d),
        "ffn1_w_down": (ffn1_w_down[0], f4p, d), "w_in": (w_in[0], d, cin), "w_out": (w_out[0], w_out.shape[1], d),
        "ffn2_w_gate": (view("ffn2_w_gate", ffn2_w_gate), f4p, d), "ffn2_w_up": (view("ffn2_w_up", ffn2_w_up), f4p, d),
        "ffn2_w_down": (ffn2_w_down[0], f4p, d),
    }
    names = list(big)
    chip1 = jnp.reshape(chip, (1,)).astype(jnp.int32)
    relative = {k: k.startswith("ffn") for k in names}
    slot0 = jnp.zeros((1,), jnp.int32)
    shard = {k: _cast_pad(big[k][0], big[k][1], big[k][2], slot0 if relative[k] else chip1, "cast_" + k) for k in names}
    full = {}

    def gather(*keys):
        return _merge_sides([_gather_side([shard[k]], relative[k], two_path=True) for k in keys])

    gm = jnp.kron(jnp.eye(2, dtype=F32), jnp.full((HEAD, HEAD), 1.0 / HEAD, F32)).astype(BF)
    wa2 = _pair_blocks(lru_w_a[0])
    wx2 = _pair_blocks(lru_w_x[0])

    dlq = dl // N_CHIP
    dq = d // N_CHIP
    R_GAIN, R_LOSS, R_META, R_LRU, R_SC, R_WA = 0, 6, 8, 24, 40, 48
    n_wrows = (N_HEADS // 2) * LANE * LANE // d
    R_WX = R_WA + n_wrows
    R_END = R_WX + n_wrows

    def pack_top(gains, meta, loss=None):
        lossrow = jnp.zeros((2, d), F32)
        if loss is not None:
            lossrow = lossrow.at[0, 0].set(loss)
        return jnp.concatenate([jnp.concatenate(gains, axis=0), lossrow, meta], axis=0)

    def pack_rest(lru16, sc8, wa_, wx_):
        return jnp.concatenate([jnp.concatenate([lru16, jnp.zeros((16, d - dl), F32)], axis=1),
                                jnp.concatenate([sc8, jnp.zeros((8, d - dl), F32)], axis=1),
                                wa_.reshape(n_wrows, d), wx_.reshape(n_wrows, d)], axis=0)

    def pack(gains, meta, lru16, sc8, wa_, wx_):
        return jnp.concatenate([pack_top(gains, meta), pack_rest(lru16, sc8, wa_, wx_)], axis=0)

    def place_cols(blk, width, total):
        return lax.dynamic_update_slice(jnp.zeros((blk.shape[0], total), F32), blk, (zero, chip * width))

    def pack_params(meta_, g1pre, g1post, gmpre, gmpost, g2pre, g2post, cw, cbias, wa_, ba_, wx_, bx_, lam_, sw, lgo, sgo):
        lru16 = jnp.concatenate([place_cols(cw[0], dlq, dl), cbias, ba_, bx_, lam_, lgo, jnp.zeros((7, dl), F32)], axis=0)
        sc8 = jnp.concatenate([place_cols(sw[0], dlq, dl), sgo, jnp.zeros((4, dl), F32)], axis=0)
        return pack([g1pre, g1post, gmpre, gmpost, g2pre, g2post], place_cols(meta_, dq, d), lru16, sc8,
                    _pair_blocks(wa_[0]), _pair_blocks(wx_[0]))

    p_w = pack_params(meta_tokens, ffn1_pre_g, ffn1_post_g, mix_pre_g, mix_post_g, ffn2_pre_g, ffn2_post_g, lru_conv_w,
                      lru_conv_b, lru_w_a, lru_b_a, lru_w_x, lru_b_x, lru_lambda, sconv_w, lru_out_g, sconv_out_g)
    p_m = pack_params(m_meta_tokens, m_ffn1_pre_g, m_ffn1_post_g, m_mix_pre_g, m_mix_post_g, m_ffn2_pre_g, m_ffn2_post_g,
                      m_lru_conv_w, m_lru_conv_b, m_lru_w_a, m_lru_b_a, m_lru_w_x, m_lru_b_x, m_lru_lambda, m_sconv_w,
                      m_lru_out_g, m_sconv_out_g)
    p_v = pack_params(v_meta_tokens, v_ffn1_pre_g, v_ffn1_post_g, v_mix_pre_g, v_mix_post_g, v_ffn2_pre_g, v_ffn2_post_g,
                      v_lru_conv_w, v_lru_conv_b, v_lru_w_a, v_lru_b_a, v_lru_w_x, v_lru_b_x, v_lru_lambda, v_sconv_w,
                      v_lru_out_g, v_sconv_out_g)

    gathered = _small_all_reduce(jnp.where(ci == 0, p_w, 0.0)[R_META:R_WA], "small_weight_gather")
    meta_full = gathered[0:N_META]
    w4_full = gathered[R_LRU - R_META:R_LRU - R_META + 4, 0:dl]
    w3_full = gathered[R_SC - R_META:R_SC - R_META + 3, 0:dl]
    w4p = jnp.concatenate([w4_full, jnp.zeros((4, dl), F32)], axis=0)
    w3p = jnp.concatenate([w3_full, jnp.zeros((5, dl), F32)], axis=0)

    h0 = jnp.concatenate([meta_full, x[0], jnp.zeros((tp - t_real, d), F32)], axis=0)
    tgt = jnp.concatenate([jnp.zeros((N_META, d), F32), loss_target[0], jnp.zeros((tp - t_real, d), F32)], axis=0)

    n1 = _norm0(h0, ffn1_pre_g)
    (a1, b1, s1), (full["ffn1_w_gate"], full["ffn1_w_up"]), got = _ffn_up_head(
        n1, shard["ffn1_w_gate"], shard["ffn1_w_up"], "ffn1_up",
        _gather_side([shard["ffn1_w_down"]], relative=True, two_path=True))
    full["ffn1_w_down"] = got[0]
    f1, got = _wide_matmul([(s1, full["ffn1_w_down"])], "ffn1_down", WIDE_TN, gather("w_in"))
    full["w_in"] = got[0]
    h1, u = _post_fwd(f1, h0, ffn1_post_g, mix_pre_g, 0.5, "ffn1_post")
    z, got = _col_matmul(u, full["w_in"], "in_proj", False, F32, gather("ffn2_w_gate"))
    full["ffn2_w_gate"] = got[0]
    (m_lru, hs), got = _lru_fwd(z, w4p, lru_conv_b, wa2.astype(BF), lru_b_a, wx2.astype(BF), lru_b_x, lru_lambda,
                                lru_out_g, gm, gather("ffn2_w_up"))
    full["ffn2_w_up"] = got[0]
    m_sc, got = _sc_fwd(z, w3p, sconv_out_g, gm, dl, gather("w_out"))
    full["w_out"] = got[0]
    mixed = jnp.concatenate([m_lru, m_sc], axis=1)
    p, _ = _row_matmul([(mixed, full["w_out"])], "out_proj", False, d)
    h2, n2 = _post_fwd(p, h1, mix_post_g, ffn2_pre_g, 1.0, "mix_post")
    (a2, b2, s2), got = _ffn_up(n2, full["ffn2_w_gate"], full["ffn2_w_up"], "ffn2_up", gather("ffn2_w_down"))
    full["ffn2_w_down"] = got[0]
    f2, _ = _wide_matmul([(s2, full["ffn2_w_down"])], "ffn2_down", WIDE_TN)
    dh3, df2, dg_ffn2_post, loss_part = _loss_bwd(f2, h2, tgt, ffn2_post_g, t_real)

    core = jnp.reshape(ci, (1,)).astype(jnp.int32)
    sel_of = {False: jnp.stack([chip, ci]).astype(jnp.int32), True: jnp.stack([0 * chip, ci]).astype(jnp.int32)}
    red = {}

    def pair_side(k):
        return _pair_exchange_side([red[k][0]])

    def chip_side(k):
        return _chip_exchange_side([_pair_sum(red[k][0], red[k][1], core, "pair_sum_" + k)], relative[k])

    def final_sum(k):
        return _final_sum(*red[k], sel_of[relative[k]], "final_sum_" + k)

    (da2, db2), _ = _ffn_bwd_act(df2, full["ffn2_w_down"], a2, b2, "ffn2_bwd_act")
    g, _ = _wgrad_call(s2, df2, "ffn2_down_wgrad", x_width=f4p, tile_y=WGRAD_TILE_Y)
    red["ffn2_w_down"] = [g, None, None]
    g, got = _wgrad_call(da2, n2, "ffn2_gate_wgrad", x_width=f4p, tile_y=WGRAD_TILE_Y, side=pair_side("ffn2_w_down"))
    red["ffn2_w_down"][1] = got[0]
    red["ffn2_w_gate"] = [g, None, None]
    g, got = _wgrad_call(db2, n2, "ffn2_up_wgrad", x_width=f4p, tile_y=WGRAD_TILE_Y,
                         side=_merge_sides([pair_side("ffn2_w_gate"), chip_side("ffn2_w_down")]))
    red["ffn2_w_gate"][1], red["ffn2_w_down"][2] = got
    red["ffn2_w_up"] = [g, None, None]
    dn2, got = _row_matmul([(da2, full["ffn2_w_gate"]), (db2, full["ffn2_w_up"])], "ffn2_bwd_up", False, d,
                           _merge_sides([pair_side("ffn2_w_up"), chip_side("ffn2_w_gate")]), tiles=MM_TILES)
    red["ffn2_w_up"][1], red["ffn2_w_gate"][2] = got
    dh2, dp, dg_ffn2_pre, dg_mix_post = _pre_bwd(dn2, h2, dh3, ffn2_pre_g, "ffn2_pre_bwd", (p, mix_post_g, 1.0))
    dmixed, _ = _col_matmul(dp, full["w_out"], "out_proj_bwd", True, F32)
    g, _ = _wgrad_call(mixed, dp, "w_out_wgrad", x_width=mixed.shape[1] // N_CHIP, tile_y=WGRAD_TILE_Y)
    red["w_out"] = [g, None, None]
    (dzy, dzx, lru_small, dwa2, dwx2), got = _lru_bwd(
        z, hs, dmixed, w4p, lru_conv_b, wa2.astype(BF), lru_b_a, wx2.astype(BF), lru_b_x, lru_lambda, lru_out_g, gm,
        _merge_sides([pair_side("w_out"), chip_side("ffn2_w_up")]))
    red["w_out"][1], red["ffn2_w_up"][2] = got
    (dzb, dzc, dzv, sc_small), got = _sc_bwd(z, dmixed, w3p, sconv_out_g, gm, dl, chip_side("w_out"))
    red["w_out"][2] = got[0]
    dz = jnp.concatenate([dzy, dzx, dzb, dzc, dzv], axis=1)
    p_rest = pack_rest(lru_small, sc_small, dwa2, dwx2)
    g, got = _wgrad_call(u, dz, "w_in_wgrad", y_width=cin, tile_x=WGRAD_TILE_X, side=_sibling_copy_side(p_rest))
    p_rest4 = _small_pair_sum(p_rest, got[0], chip1)
    red["w_in"] = [g, None, None]
    du, got = _row_matmul([(dz, full["w_in"])], "in_proj_bwd", True, d,
                          _merge_sides([pair_side("w_in"), _slot_exchange_side(p_rest4)]))
    red["w_in"][1], p_rest4 = got
    dh1, df1, dg_mix_pre, dg_ffn1_post = _pre_bwd(du, h1, dh2, mix_pre_g, "mix_pre_bwd", (f1, ffn1_post_g, 0.5))
    (da1, db1), got = _ffn_bwd_act(df1, full["ffn1_w_down"], a1, b1, "ffn1_bwd_act", chip_side("w_in"))
    red["w_in"][2] = got[0]
    early = ["ffn2_w_down", "ffn2_w_gate", "ffn2_w_up", "w_out", "w_in"]
    late = ["ffn1_w_down", "ffn1_w_gate", "ffn1_w_up"]
    g, got = _wgrad_call(s1, df1, "ffn1_down_wgrad", x_width=f4p, tile_y=WGRAD_TILE_Y,
                         side=_join_side([final_sum(k) for k in early]))
    gfull = dict(zip(early, got))
    red["ffn1_w_down"] = [g, None, None]
    g, got = _wgrad_call(da1, n1, "ffn1_gate_wgrad", x_width=f4p, tile_y=WGRAD_TILE_Y, side=pair_side("ffn1_w_down"))
    red["ffn1_w_down"][1] = got[0]
    red["ffn1_w_gate"] = [g, None, None]
    g, got = _wgrad_call(db1, n1, "ffn1_up_wgrad", x_width=f4p, tile_y=WGRAD_TILE_Y,
                         side=_merge_sides([pair_side("ffn1_w_gate"), chip_side("ffn1_w_down")]))
    red["ffn1_w_gate"][1], red["ffn1_w_down"][2] = got
    red["ffn1_w_up"] = [g, None, None]
    red["ffn1_w_up"][1] = _run_side(pair_side("ffn1_w_up"), "pair_exchange_ffn1_w_up")[0]
    dn1, got = _row_matmul([(da1, full["ffn1_w_gate"]), (db1, full["ffn1_w_up"])], "ffn1_bwd_up", False, d,
                           _merge_sides([chip_side("ffn1_w_gate"), chip_side("ffn1_w_up")]), tiles=MM_TILES)
    red["ffn1_w_gate"][2], red["ffn1_w_up"][2] = got
    (dh0, dg_ffn1_pre), got = _pre_bwd(dn1, h0, dh1, ffn1_pre_g, "ffn1_pre_bwd",
                                       side=_join_side([final_sum(k) for k in late]))
    gfull.update(zip(late, got))

    grad_x = dh0[N_META:t_real][None]

    w_big = {"ffn1_w_gate": ffn1_w_gate, "ffn1_w_up": ffn1_w_up, "ffn1_w_down": ffn1_w_down, "w_in": w_in, "w_out": w_out,
             "ffn2_w_gate": ffn2_w_gate, "ffn2_w_up": ffn2_w_up, "ffn2_w_down": ffn2_w_down}
    m_big = {"ffn1_w_gate": m_ffn1_w_gate, "ffn1_w_up": m_ffn1_w_up, "ffn1_w_down": m_ffn1_w_down, "w_in": m_w_in,
             "w_out": m_w_out, "ffn2_w_gate": m_ffn2_w_gate, "ffn2_w_up": m_ffn2_w_up, "ffn2_w_down": m_ffn2_w_down}
    v_big = {"ffn1_w_gate": v_ffn1_w_gate, "ffn1_w_up": v_ffn1_w_up, "ffn1_w_down": v_ffn1_w_down, "w_in": v_w_in,
             "w_out": v_w_out, "ffn2_w_gate": v_ffn2_w_gate, "ffn2_w_up": v_ffn2_w_up, "ffn2_w_down": v_ffn2_w_down}
    b_grad, b_delta, b_newm, b_newv = {}, {}, {}, {}

    def big_adamw(k, side=None):
        wv, mv, vv = view(k, w_big[k]), view(k, m_big[k]), view(k, v_big[k])
        wide_rows = wv.shape[0] % 64 == 0
        (g_, d_, m_, v_), got = _adamw(wv, gfull[k], mv, vv, "adamw_" + k, 8 if wide_rows else 4, 1 if wide_rows else 2,
                                       side)
        b_grad[k], b_delta[k], b_newm[k], b_newv[k] = unview(k, g_), unview(k, d_), unview(k, m_), unview(k, v_)
        return got

    p_top = _small_all_reduce(
        pack_top([dg_ffn1_pre, dg_ffn1_post, dg_mix_pre, dg_mix_post, dg_ffn2_pre, dg_ffn2_post], dh0[0:N_META],
                 loss=loss_part[0, 0]), "small_grad_all_reduce")
    p_g, p_delta, p_newm, p_newv = _adamw_small(p_w, p_top, p_rest4, p_m, p_v)
    loss = p_g[R_LOSS, 0]
    for k in names:
        big_adamw(k)

    def unpack(buf):
        out = {}
        for i, k in enumerate(["ffn1_pre_g", "ffn1_post_g", "mix_pre_g", "mix_post_g", "ffn2_pre_g", "ffn2_post_g"]):
            out[k] = buf[R_GAIN + i:R_GAIN + i + 1]
        out["meta_tokens"] = lax.dynamic_slice(buf[R_META:R_META + N_META], (zero, chip * dq), (N_META, dq))
        lru = buf[R_LRU:R_LRU + 16, 0:dl]
        out["lru_conv_w"] = lax.dynamic_slice(lru[0:4], (zero, chip * dlq), (4, dlq))[None]
        out["lru_conv_b"] = lru[4:5]
        out["lru_b_a"] = lru[5:6]
        out["lru_b_x"] = lru[6:7]
        out["lru_lambda"] = lru[7:8]
        out["lru_out_g"] = lru[8:9]
        sc = buf[R_SC:R_SC + 8, 0:dl]
        out["sconv_w"] = lax.dynamic_slice(sc[0:3], (zero, chip * dlq), (3, dlq))[None]
        out["sconv_out_g"] = sc[3:4]
        out["lru_w_a"] = _unpair_blocks(buf[R_WA:R_WX].reshape(N_HEADS // 2, LANE, LANE))[None]
        out["lru_w_x"] = _unpair_blocks(buf[R_WX:R_END].reshape(N_HEADS // 2, LANE, LANE))[None]
        return out

    s_grad, s_delta, s_newm, s_newv = unpack(p_g), unpack(p_delta), unpack(p_newm), unpack(p_newv)

    order = ["meta_tokens", "ffn1_pre_g", "ffn1_w_gate", "ffn1_w_up", "ffn1_w_down", "ffn1_post_g", "mix_pre_g", "w_in",
             "lru_conv_w", "lru_conv_b", "lru_w_a", "lru_b_a", "lru_w_x", "lru_b_x", "lru_lambda", "sconv_w", "lru_out_g",
             "sconv_out_g", "w_out", "mix_post_g", "ffn2_pre_g", "ffn2_w_gate", "ffn2_w_up", "ffn2_w_down", "ffn2_post_g"]

    def pick(small, bigd):
        return [bigd[k] if k in bigd else small[k] for k in order]

    return (loss, grad_x, *pick(s_grad, b_grad), *pick(s_delta, b_delta), *pick(s_newm, b_newm), *pick(s_newv, b_newv))
```

```python
import functools
import math

import jax
import jax.numpy as jnp
from jax import lax
from jax.experimental import pallas as pl
from jax.experimental.pallas import tpu as pltpu

F32 = jnp.float32
BF = jnp.bfloat16
MESH = pl.DeviceIdType.MESH

EPS = 1e-6
N_META = 16
N_HEADS = 16
HEAD = 64
LRU_C = 8.0
LANE = 128
MXU_COLS = 256
N_CHIP = 4
ROW_ALIGN = 384
MM_TILES = 8
MM_TILES_BIG = 4
SWIGLU_TILES = 6
EW_TILES = 12
MIX_CHUNKS = 24
WGRAD_TILE_X = 256
WGRAD_TILE_Y = 512
WIDE_TN = 512
VMEM_LIMIT = 56 << 20

ADAM_LR = 0.001
ADAM_B1 = 0.9
ADAM_B2 = 0.999
ADAM_EPS = 1e-08
ADAM_WD = 0.01
ADAM_STEP = 10


def _round_up(a, b):
    return (a + b - 1) // b * b


def _params(sem=None):
    if sem is None:
        return pltpu.CompilerParams(vmem_limit_bytes=VMEM_LIMIT)
    return pltpu.CompilerParams(dimension_semantics=sem, vmem_limit_bytes=VMEM_LIMIT)


def _sigmoid(x):
    return 0.5 * jnp.tanh(0.5 * x) + 0.5


def _dot(a, b):
    return jnp.dot(a, b, preferred_element_type=F32)


def _dot_nt(a, b):
    return lax.dot_general(a, b, (((1,), (1,)), ((), ())), preferred_element_type=F32)


def _dot_tn(a, b):
    return lax.dot_general(a, b, (((0,), (0,)), ((), ())), preferred_element_type=F32)


def _rms(x, g):
    r = lax.rsqrt(jnp.mean(x * x, axis=-1, keepdims=True) + EPS)
    return x * r * g


def _rms_bwd(x, g, dy):
    r = lax.rsqrt(jnp.mean(x * x, axis=-1, keepdims=True) + EPS)
    xh = x * r
    q = dy * g
    dx = r * (q - xh * jnp.mean(q * xh, axis=-1, keepdims=True))
    return dx, dy * xh


class _Side:
    def __init__(self, ins, outs, alias, sems, start, finish):
        self.ins, self.outs, self.alias, self.sems, self.start, self.finish = ins, outs, alias, sems, start, finish


def _merge_sides(sides):
    sides = [s for s in sides if s is not None]
    if len(sides) <= 1:
        return sides[0] if sides else None
    ins, outs, sems, alias, spans = [], [], [], {}, []
    for s in sides:
        for i, o in s.alias.items():
            alias[len(ins) + i] = len(outs) + o
        spans.append((len(ins), len(ins) + len(s.ins), len(outs), len(outs) + len(s.outs), len(sems),
                      len(sems) + len(s.sems)))
        ins += list(s.ins)
        outs += list(s.outs)
        sems += list(s.sems)

    def run(which):
        def go(in_refs, out_refs, sem_refs):
            for s, (a, b, c, d, e, f) in zip(sides, spans):
                getattr(s, which)(in_refs[a:b], out_refs[c:d], sem_refs[e:f])
        return go

    return _Side(ins, outs, alias, sems, run("start"), run("finish"))


def _grid_call(body, name, grid, in_specs, out_specs, out_shape, args, side=None, scratch=()):
    sem = ("arbitrary",) * len(grid)
    if side is None:
        res = pl.pallas_call(body, name=name, grid=grid, in_specs=in_specs, out_specs=out_specs, out_shape=out_shape,
                             scratch_shapes=list(scratch), compiler_params=_params(sem))(*args)
        return res, []
    nin, nout, sin, sout = len(in_specs), len(out_specs), len(side.ins), len(side.outs)
    nscr = len(scratch)
    staged = hasattr(side, "middle") and math.prod(grid) >= 4
    lin, mid = (math.prod(grid) * 5) // 8, []
    for extent in reversed(grid):
        mid.insert(0, lin % extent)
        lin //= extent

    def full(*refs):
        base_in, side_in = refs[:nin], refs[nin:nin + sin]
        base_out = refs[nin + sin:nin + sin + nout]
        side_out = refs[nin + sin + nout:nin + sin + nout + sout]
        base_scr = refs[nin + sin + nout + sout:nin + sin + nout + sout + nscr]
        sems = refs[nin + sin + nout + sout + nscr:]
        first = pl.program_id(0) == 0
        last = pl.program_id(0) == grid[0] - 1
        for ax in range(1, len(grid)):
            first = first & (pl.program_id(ax) == 0)
            last = last & (pl.program_id(ax) == grid[ax] - 1)

        @pl.when(first)
        def _():
            side.start(side_in, side_out, sems)

        if staged:
            at_mid = pl.program_id(0) == mid[0]
            for ax in range(1, len(grid)):
                at_mid = at_mid & (pl.program_id(ax) == mid[ax])

            @pl.when(at_mid)
            def _():
                side.middle(side_in, side_out, sems)

        body(*base_in, *base_out, *base_scr)

        @pl.when(last)
        def _():
            (side.rest if staged else side.finish)(side_in, side_out, sems)

    any_spec = pl.BlockSpec(memory_space=pl.ANY)
    res = pl.pallas_call(
        full, name=name, grid=grid, in_specs=list(in_specs) + [any_spec] * sin,
        out_specs=list(out_specs) + [any_spec] * sout, out_shape=list(out_shape) + list(side.outs),
        scratch_shapes=list(scratch) + list(side.sems),
        input_output_aliases={nin + i: nout + o for i, o in side.alias.items()},
        compiler_params=_params(sem))(*args, *side.ins)
    return res[:nout], res[nout:]


def _ffn_up(n, wg, wu, name, side=None, tiles=SWIGLU_TILES):
    tp, d = n.shape
    fp = wg.shape[1]
    tm = tp // tiles

    def body(n_ref, wg_ref, wu_ref, a_ref, b_ref, s_ref):
        nn = n_ref[...]
        for c0 in range(0, fp, MXU_COLS):
            cs = slice(c0, min(c0 + MXU_COLS, fp))
            a = _dot_nt(nn, wg_ref[cs, :])
            b = _dot_nt(nn, wu_ref[cs, :])
            a_ref[:, cs] = a.astype(BF)
            b_ref[:, cs] = b.astype(BF)
            s_ref[:, cs] = (a * _sigmoid(a) * b).astype(BF)

    out = jax.ShapeDtypeStruct((tp, N_CHIP * fp), BF)
    wspec = pl.BlockSpec((None, fp, d), lambda k, i: (k, 0, 0))
    ospec = pl.BlockSpec((tm, fp), lambda k, i: (i, k))
    return _grid_call(body, name, (N_CHIP, tiles), [pl.BlockSpec((tm, d), lambda k, i: (i, 0)), wspec, wspec],
                      [ospec, ospec, ospec], [out, out, out], (n, wg, wu), side)


def _ffn_up_head(n, wg, wu, name, side):
    tp, d = n.shape
    fp = wg.shape[1]
    tiles = SWIGLU_TILES
    tm = tp // tiles
    gat = _gather_side([wg, wu], relative=True, two_path=True)
    sin, sout, ngs = len(side.ins), len(side.outs), len(gat.sems)
    order = (0,) + REL_SLOT
    staged = hasattr(side, "middle")

    def body(*refs):
        n_ref = refs[0]
        si = refs[3:3 + sin]
        a_ref, b_ref, s_ref = refs[3 + sin:6 + sin]
        go = refs[6 + sin:8 + sin]
        so = refs[8 + sin:8 + sin + sout]
        wbg, wbu, wsem = refs[8 + sin + sout:11 + sin + sout]
        gsems = refs[11 + sin + sout:11 + sin + sout + ngs]
        ssems = refs[11 + sin + sout + ngs:]
        k, i = pl.program_id(0), pl.program_id(1)
        cur = k % 2

        def to_vmem(slot, buf):
            return [pltpu.make_async_copy(go[0].at[slot], wbg.at[buf], wsem.at[buf, 0]),
                    pltpu.make_async_copy(go[1].at[slot], wbu.at[buf], wsem.at[buf, 1])]

        @pl.when((k == 0) & (i == 0))
        def _():
            gat.send(go, gsems)
            if not staged:
                side.start(si, so, ssems)
            for cp in to_vmem(0, 0):
                cp.start()
            for cp in to_vmem(0, 0):
                cp.wait()

        for j in range(3):
            @pl.when((k == j) & (i == tiles // 2))
            def _():
                gat.arrived(j, go, gsems)
                if staged and j == 1:
                    side.start(si, so, ssems)

            @pl.when((k == j) & (i == tiles - 2))
            def _():
                gat.forwarded(j, go, gsems)
                for cp in to_vmem(order[j + 1], (j + 1) % 2):
                    cp.start()

            @pl.when((k == j + 1) & (i == 0))
            def _():
                for cp in to_vmem(order[j + 1], (j + 1) % 2):
                    cp.wait()

        nn = n_ref[...]
        for c0 in range(0, fp, MXU_COLS):
            cs = pl.ds(c0, min(MXU_COLS, fp - c0))
            a = _dot_nt(nn, wbg[cur, cs, :])
            b = _dot_nt(nn, wbu[cur, cs, :])
            a_ref[:, cs] = a.astype(BF)
            b_ref[:, cs] = b.astype(BF)
            s_ref[:, cs] = (a * _sigmoid(a) * b).astype(BF)

        if staged:
            @pl.when((k == N_CHIP - 1) & (i == tiles // 4))
            def _():
                side.middle(si, so, ssems)

        @pl.when((k == N_CHIP - 1) & (i == tiles - 1))
        def _():
            gat.drain(go, gsems)
            if staged:
                side.rest(si, so, ssems)
            else:
                side.finish(si, so, ssems)

    out = jax.ShapeDtypeStruct((tp, N_CHIP * fp), BF)
    any_spec = pl.BlockSpec(memory_space=pl.ANY)
    slot_of = lambda k: (k % 2) * 2 + k // 2
    ospec = pl.BlockSpec((tm, fp), lambda k, i: (i, slot_of(k)))
    wbuf = pltpu.VMEM((2, fp, d), BF)
    res = pl.pallas_call(
        body, name=name, grid=(N_CHIP, tiles),
        in_specs=[pl.BlockSpec((tm, d), lambda k, i: (i, 0))] + [any_spec] * (2 + sin),
        out_specs=[ospec, ospec, ospec] + [any_spec] * (2 + sout),
        out_shape=[out, out, out] + list(gat.outs) + list(side.outs),
        scratch_shapes=[wbuf, wbuf, pltpu.SemaphoreType.DMA((2, 2))] + list(gat.sems) + list(side.sems),
        input_output_aliases={1: 3, 2: 4, **{3 + a: 5 + b for a, b in side.alias.items()}},
        compiler_params=_params(("arbitrary", "arbitrary")))(n, wg, wu, *side.ins)
    return res[:3], res[3:5], res[5:]


def _ffn_bwd_act(df, wd, a, b, name, side=None, tiles=SWIGLU_TILES):
    tp, d = df.shape
    fp = wd.shape[1]
    tm = tp // tiles

    def body(df_ref, wd_ref, a_ref, b_ref, da_ref, db_ref):
        dfv = df_ref[...]
        for c0 in range(0, fp, MXU_COLS):
            cs = slice(c0, min(c0 + MXU_COLS, fp))
            ds = _dot_nt(dfv, wd_ref[cs, :])
            av = a_ref[:, cs].astype(F32)
            bv = b_ref[:, cs].astype(F32)
            sg = _sigmoid(av)
            da_ref[:, cs] = (ds * bv * sg * (1.0 + av * (1.0 - sg))).astype(BF)
            db_ref[:, cs] = (ds * av * sg).astype(BF)

    out = jax.ShapeDtypeStruct((tp, N_CHIP * fp), BF)
    aspec = pl.BlockSpec((tm, fp), lambda k, i: (i, k))
    return _grid_call(
        body, name, (N_CHIP, tiles),
        [pl.BlockSpec((tm, d), lambda k, i: (i, 0)), pl.BlockSpec((None, fp, d), lambda k, i: (k, 0, 0)), aspec, aspec],
        [aspec, aspec], [out, out], (df, wd, a, b), side)


def _col_matmul(lhs, w, name, trans_b, out_dtype, side=None, tiles=MM_TILES_BIG):
    tp, kd = lhs.shape
    nk = w.shape[0]
    nc = w.shape[1] if trans_b else w.shape[2]
    tm = tp // tiles

    def body(l_ref, w_ref, o_ref):
        if trans_b:
            o_ref[...] = _dot_nt(l_ref[...], w_ref[...]).astype(out_dtype)
        else:
            o_ref[...] = _dot(l_ref[...], w_ref[...]).astype(out_dtype)

    res, extra = _grid_call(
        body, name, (nk, tiles),
        [pl.BlockSpec((tm, kd), lambda k, i: (i, 0)),
         pl.BlockSpec((None,) + tuple(w.shape[1:]), lambda k, i: (k, 0, 0), pipeline_mode=pl.Buffered(1))],
        [pl.BlockSpec((tm, nc), lambda k, i: (i, k))], [jax.ShapeDtypeStruct((tp, nk * nc), out_dtype)], (lhs, w), side)
    return res[0], extra


def _row_matmul(pairs, name, trans_b, d_out, side=None, tiles=MM_TILES_BIG):
    l0 = pairs[0][0]
    tp = l0.shape[1] if l0.ndim == 3 else l0.shape[0]
    nk = pairs[0][1].shape[0]
    tm = tp // tiles
    npair = len(pairs)

    def body(*refs):
        o_ref = refs[2 * npair]
        k = pl.program_id(1)
        part = None
        for q in range(npair):
            l = refs[2 * q][...]
            w = refs[2 * q + 1][...]
            t = _dot_nt(l, w) if trans_b else _dot(l, w)
            part = t if part is None else part + t

        @pl.when(k == 0)
        def _():
            o_ref[...] = part

        @pl.when(k > 0)
        def _():
            o_ref[...] += part

    in_specs, args = [], []
    for lhs, w in pairs:
        if lhs.ndim == 3:
            in_specs.append(pl.BlockSpec((None, tm, lhs.shape[2]), lambda i, k: (k, i, 0)))
        else:
            in_specs.append(pl.BlockSpec((tm, lhs.shape[1] // nk), lambda i, k: (i, k)))
        in_specs.append(pl.BlockSpec((None,) + tuple(w.shape[1:]), lambda i, k: (k, 0, 0)))
        args += [lhs, w]
    res, extra = _grid_call(body, name, (tiles, nk), in_specs, [pl.BlockSpec((tm, d_out), lambda i, k: (i, 0))],
                            [jax.ShapeDtypeStruct((tp, d_out), F32)], args, side)
    return res[0], extra


def _wide_matmul(pairs, name, tn, side=None):
    tp = pairs[0][0].shape[0]
    d_out = pairs[0][1].shape[2]
    tm = tp // MM_TILES_BIG
    npair = len(pairs)

    def body(*refs):
        acc = None
        for q in range(npair):
            t = _dot(refs[2 * q][...], refs[2 * q + 1][...])
            acc = t if acc is None else acc + t
        refs[2 * npair][...] = acc

    in_specs, args = [], []
    for lhs, w in pairs:
        kdim = lhs.shape[1]
        in_specs += [pl.BlockSpec((tm, kdim), lambda n, i: (i, 0)), pl.BlockSpec((kdim, tn), lambda n, i: (0, n))]
        args += [lhs, w.reshape(kdim, d_out)]
    res, extra = _grid_call(body, name, (d_out // tn, MM_TILES_BIG), in_specs, [pl.BlockSpec((tm, tn), lambda n, i: (i, n))],
                            [jax.ShapeDtypeStruct((tp, d_out), F32)], args, side)
    return res[0], extra


def _wgrad_call(x, y, name, x_width=None, y_width=None, tile_x=None, tile_y=None, side=None):
    tp = x.shape[1] if x.ndim == 3 else x.shape[0]

    def spec(a, width, tile):
        cols = a.shape[2] if a.ndim == 3 else (a.shape[1] if width is None else width)
        tc = cols if tile is None else tile
        per = cols // tc
        if a.ndim == 3:
            return pl.BlockSpec((None, tp, tc), lambda k, t: (k, 0, t if tile else 0)), cols, per
        if width is None:
            return pl.BlockSpec((tp, tc), lambda k, t: (0, t if tile else 0)), cols, per
        return pl.BlockSpec((tp, tc), lambda k, t: (0, k * per + (t if tile else 0))), cols, per

    xs, p, nx = spec(x, x_width, tile_x)
    ys, q, ny = spec(y, y_width, tile_y)
    nt = nx * ny
    if tile_x:
        ospec = pl.BlockSpec((None, tile_x, q), lambda k, t: (k, t, 0))
    else:
        ospec = pl.BlockSpec((None, p, tile_y), lambda k, t: (k, 0, t))

    def body(x_ref, y_ref, o_ref):
        o_ref[...] = _dot_tn(x_ref[...], y_ref[...]).astype(BF)

    res, extra = _grid_call(body, name, (N_CHIP, nt), [xs, ys], [ospec], [jax.ShapeDtypeStruct((N_CHIP, p, q), BF)],
                            (x, y), side)
    return res[0], extra


def _row_call(body, name, tp, d, row_ins, vec_ins, row_out_dtypes, n_acc, side=None):
    te = tp // EW_TILES
    rspec = pl.BlockSpec((te, d), lambda i: (i, 0))
    vspec = pl.BlockSpec((1, d), lambda i: (0, 0))
    res, extra = _grid_call(
        body, name, (EW_TILES,), [rspec] * len(row_ins) + [vspec] * len(vec_ins),
        [rspec] * len(row_out_dtypes) + [vspec] * n_acc,
        [jax.ShapeDtypeStruct((tp, d), dt) for dt in row_out_dtypes] + [jax.ShapeDtypeStruct((1, d), F32)] * n_acc,
        (*row_ins, *vec_ins), side)
    return res if side is None else (res, extra)


def _norm0(h, g):
    tp, d = h.shape

    def body(h_ref, g_ref, n_ref):
        n_ref[...] = _rms(h_ref[...], g_ref[...]).astype(BF)

    return _row_call(body, "norm0", tp, d, [h], [g], [BF], 0)[0]


def _post_fwd(f, h, g_post, g_next, scale, name):
    tp, d = h.shape

    def body(f_ref, h_ref, gp_ref, gn_ref, hn_ref, n_ref):
        hn = h_ref[...] + scale * _rms(f_ref[...], gp_ref[...])
        hn_ref[...] = hn
        n_ref[...] = _rms(hn, gn_ref[...]).astype(BF)

    return _row_call(body, name, tp, d, [f, h], [g_post, g_next], [F32, BF], 0)


def _loss_bwd(f, h, tgt, g_post, t_real):
    tp, d = h.shape
    te = tp // EW_TILES

    def body(f_ref, h_ref, t_ref, gp_ref, dh_ref, df_ref, dg_ref, loss_ref):
        i = pl.program_id(0)

        @pl.when(i == 0)
        def _():
            dg_ref[...] = jnp.zeros_like(dg_ref)
            loss_ref[...] = jnp.zeros_like(loss_ref)

        f = f_ref[...]
        gp = gp_ref[...]
        h3 = h_ref[...] + 0.5 * _rms(f, gp)
        rows = i * te + lax.broadcasted_iota(jnp.int32, (te, 1), 0)
        real = (rows >= N_META) & (rows < t_real)
        e = jnp.where(real, h3 - t_ref[...], 0.0)
        loss_ref[...] += 0.5 * jnp.sum(jnp.sum(e * e, axis=1, keepdims=True), axis=0, keepdims=True) / d
        dh = e / d
        dh_ref[...] = dh
        dfv, dgr = _rms_bwd(f, gp, 0.5 * dh)
        df_ref[...] = dfv.astype(BF)
        dg_ref[...] += jnp.sum(dgr, axis=0, keepdims=True)

    rspec = pl.BlockSpec((te, d), lambda i: (i, 0))
    vspec = pl.BlockSpec((1, d), lambda i: (0, 0))
    return pl.pallas_call(
        body, name="loss_bwd", grid=(EW_TILES,),
        in_specs=[rspec, rspec, rspec, vspec],
        out_specs=[rspec, rspec, vspec, pl.BlockSpec((1, 1), lambda i: (0, 0))],
        out_shape=[jax.ShapeDtypeStruct((tp, d), F32), jax.ShapeDtypeStruct((tp, d), BF),
                   jax.ShapeDtypeStruct((1, d), F32), jax.ShapeDtypeStruct((1, 1), F32)],
        compiler_params=_params(("arbitrary",)),
    )(f, h, tgt, g_post)


def _pre_bwd(dn, h, dh_out, g_pre, name, chain=None, side=None):
    tp, d = h.shape

    def body(*refs):
        if chain is None:
            dn_ref, h_ref, dho_ref, g_ref, dh_ref, dg_ref = refs
        else:
            dn_ref, h_ref, dho_ref, p_ref, g_ref, gp_ref, dh_ref, dp_ref, dg_ref, dgp_ref = refs
        i = pl.program_id(0)

        @pl.when(i == 0)
        def _():
            dg_ref[...] = jnp.zeros_like(dg_ref)
            if chain is not None:
                dgp_ref[...] = jnp.zeros_like(dgp_ref)

        dx, dgr = _rms_bwd(h_ref[...], g_ref[...], dn_ref[...])
        dh = dho_ref[...] + dx
        dh_ref[...] = dh
        dg_ref[...] += jnp.sum(dgr, axis=0, keepdims=True)
        if chain is not None:
            dp, dgpr = _rms_bwd(p_ref[...], gp_ref[...], chain[2] * dh)
            dp_ref[...] = dp.astype(BF)
            dgp_ref[...] += jnp.sum(dgpr, axis=0, keepdims=True)

    if chain is None:
        return _row_call(body, name, tp, d, [dn, h, dh_out], [g_pre], [F32], 1, side)
    return _row_call(body, name, tp, d, [dn, h, dh_out, chain[0]], [g_pre, chain[1]], [F32, BF], 2, side)


def _gelu(y):
    c = math.sqrt(2.0 / math.pi)
    return 0.5 * y * (1.0 + jnp.tanh(c * (y + 0.044715 * y * y * y)))


def _gelu_and_grad(y):
    c = math.sqrt(2.0 / math.pi)
    y2 = y * y
    t = jnp.tanh(c * y * (1.0 + 0.044715 * y2))
    half = 0.5 * (1.0 + t)
    return y * half, half + 0.5 * y * (1.0 - t * t) * c * (1.0 + 3.0 * 0.044715 * y2)


def _neg_expm1(x):
    p = 1.0 + x * (1.0 / 9.0)
    for n in (8.0, 7.0, 6.0, 5.0, 4.0, 3.0, 2.0):
        p = 1.0 + x * (1.0 / n) * p
    return -jnp.where(x > -0.35, x * p, jnp.exp(x) - 1.0)


def _softplus(x):
    e = jnp.exp(-jnp.abs(x))
    w = 1.0 + e
    l1p = jnp.where(w == 1.0, e, jnp.log(w) * (e / jnp.where(w == 1.0, 1.0, w - 1.0)))
    return jnp.maximum(x, 0.0) + l1p


def _group_mean(v, gm):
    hi = v.astype(BF)
    lo = (v - hi.astype(F32)).astype(BF)
    return _dot(hi, gm) + _dot(lo, gm)


def _shift_dn(win, s, r):
    if s == 0:
        return win[8:8 + r]
    return pltpu.roll(win, s, 0)[8:8 + r]


def _shift_up(win, s, r):
    if s == 0:
        return win[0:r]
    return pltpu.roll(win, r + 8 - s, 0)[0:r]


def _window_dn(ref, t0, r, first):
    if first:
        return jnp.concatenate([jnp.zeros((8, ref.shape[1]), F32), ref[0:r, :]], axis=0)
    return ref[pl.ds(t0 - 8, r + 8), :]


def _tile_scan(a, u, reverse):
    r = a.shape[0]
    rid = lax.broadcasted_iota(jnp.int32, a.shape, 0) & 7
    for dlt in (1, 2, 4):
        sh = (r - dlt) if reverse else dlt
        a_s = pltpu.roll(a, sh, 0)
        u_s = pltpu.roll(u, sh, 0)
        keep = (rid + dlt <= 7) if reverse else (rid >= dlt)
        u = jnp.where(keep, u + a * u_s, u)
        a = jnp.where(keep, a * a_s, a)
    return a, u


def _lru_gates(xc, wa, ba, wx, bx, sp):
    xb = xc.astype(BF)
    ga = _sigmoid(_dot(xb, wa) + ba)
    gx = _sigmoid(_dot(xb, wx) + bx)
    la = -LRU_C * ga * sp
    return ga, gx, la


def _conv4(win, w4, cb, r):
    return (cb + w4[3:4] * _shift_dn(win, 0, r) + w4[2:3] * _shift_dn(win, 1, r)
            + w4[1:2] * _shift_dn(win, 2, r) + w4[0:1] * _shift_dn(win, 3, r))


def _lru_fwd(z, w4, cb, wa2, ba, wx2, bx, lam, g_out, gm, side=None):
    tp = z.shape[0]
    dl = cb.shape[1]
    nb = dl // LANE
    r = tp // MIX_CHUNKS
    c = LANE

    def body(y_ref, x_ref, w4_ref, cb_ref, wa_ref, ba_ref, wx_ref, bx_ref, lam_ref, go_ref, gm_ref, m_ref, hs_ref):
        w4v = w4_ref[...]
        cbv = cb_ref[...]
        wa = wa_ref[...]
        wx = wx_ref[...]
        bav = ba_ref[...]
        bxv = bx_ref[...]
        gov = go_ref[...]
        gmv = gm_ref[...]
        sp = _softplus(-lam_ref[...])

        def chunk(t0, hprev, first):
            win = _window_dn(x_ref, t0, r, first)
            xc = _conv4(win, w4v, cbv, r)
            ga, gx, la = _lru_gates(xc, wa, bav, wx, bxv, sp)
            a = jnp.exp(la)
            u = jnp.sqrt(_neg_expm1(2.0 * la)) * gx * xc
            ac, uc = _tile_scan(a, u, False)
            for j in range(r // 8):
                hj = uc[8 * j:8 * j + 8] + ac[8 * j:8 * j + 8] * hprev
                hs_ref[pl.ds(t0 + 8 * j, 8), :] = hj
                hprev = jnp.broadcast_to(hj[7:8], (8, c))
            h = hs_ref[pl.ds(t0, r), :]
            lo = h * _gelu(y_ref[pl.ds(t0, r), :])
            rs = lax.rsqrt(_group_mean(lo * lo, gmv) + EPS)
            m_ref[pl.ds(t0, r), :] = (lo * rs * gov).astype(BF)
            return hprev

        hp = chunk(0, jnp.zeros((8, c), F32), True)

        def loop(ci, hp):
            return chunk(pl.multiple_of(ci * r, 16), hp, False)

        lax.fori_loop(1, MIX_CHUNKS, loop, hp)

    col = lambda off: pl.BlockSpec((tp, c), lambda j: (0, off + j))
    vec = pl.BlockSpec((1, c), lambda j: (0, j))
    return _grid_call(
        body, "lru_fwd", (nb,),
        [col(0), col(nb), pl.BlockSpec((8, c), lambda j: (0, j)), vec, pl.BlockSpec((None, c, c), lambda j: (j, 0, 0)),
         vec, pl.BlockSpec((None, c, c), lambda j: (j, 0, 0)), vec, vec, vec, pl.BlockSpec((c, c), lambda j: (0, 0))],
        [col(0), col(0)], [jax.ShapeDtypeStruct((tp, dl), BF), jax.ShapeDtypeStruct((tp, dl), F32)],
        (z, z, w4, cb, wa2, ba, wx2, bx, lam, g_out, gm), side)


def _lru_bwd(z, hs, dmix, w4, cb, wa2, ba, wx2, bx, lam, g_out, gm, side=None):
    tp = z.shape[0]
    dl = cb.shape[1]
    nb = dl // LANE
    r = tp // MIX_CHUNKS
    c = LANE

    def body(y_ref, x_ref, hs_ref, dm_ref, w4_ref, cb_ref, wa_ref, ba_ref, wx_ref, bx_ref, lam_ref, go_ref, gm_ref,
             dy_ref, dx_ref, small_ref, dwa_ref, dwx_ref, xc_buf, ga_buf, gx_buf, a_buf, dh_buf, dxc_buf):
        w4v = w4_ref[...]
        cbv = cb_ref[...]
        wa = wa_ref[...]
        wx = wx_ref[...]
        bav = ba_ref[...]
        bxv = bx_ref[...]
        gov = go_ref[...]
        gmv = gm_ref[...]
        lamv = lam_ref[...]
        sp = _softplus(-lamv)
        small_ref[...] = jnp.zeros_like(small_ref)
        dwa_ref[...] = jnp.zeros_like(dwa_ref)
        dwx_ref[...] = jnp.zeros_like(dwx_ref)
        a_buf[pl.ds(tp, 8), :] = jnp.zeros((8, c), F32)
        dxc_buf[pl.ds(tp, 8), :] = jnp.zeros((8, c), F32)

        def fwd_chunk(t0, first):
            win = _window_dn(x_ref, t0, r, first)
            xc = _conv4(win, w4v, cbv, r)
            ga, gx, la = _lru_gates(xc, wa, bav, wx, bxv, sp)
            xc_buf[pl.ds(t0, r), :] = xc
            ga_buf[pl.ds(t0, r), :] = ga
            gx_buf[pl.ds(t0, r), :] = gx
            a_buf[pl.ds(t0, r), :] = jnp.exp(la)
            h = hs_ref[pl.ds(t0, r), :]
            yv = y_ref[pl.ds(t0, r), :]
            ge, dge = _gelu_and_grad(yv)
            lo = h * ge
            rs = lax.rsqrt(_group_mean(lo * lo, gmv) + EPS)
            xh = lo * rs
            dm = dm_ref[pl.ds(t0, r), :]
            q = dm * gov
            dlo = rs * (q - xh * _group_mean(q * xh, gmv))
            small_ref[8:9, :] += jnp.sum(dm * xh, axis=0, keepdims=True)
            dh_buf[pl.ds(t0, r), :] = dlo * ge
            dy_ref[pl.ds(t0, r), :] = (dlo * h * dge).astype(BF)

        fwd_chunk(0, True)

        def floop(ci, carry):
            fwd_chunk(pl.multiple_of(ci * r, 16), False)
            return carry

        lax.fori_loop(1, MIX_CHUNKS, floop, 0)

        def bwd_chunk(t0, vnext, first):
            ap = _shift_up(a_buf[pl.ds(t0, r + 8), :], 1, r)
            ac, uc = _tile_scan(ap, dh_buf[pl.ds(t0, r), :], True)
            for j in reversed(range(r // 8)):
                vj = uc[8 * j:8 * j + 8] + ac[8 * j:8 * j + 8] * vnext
                dh_buf[pl.ds(t0 + 8 * j, 8), :] = vj
                vnext = jnp.broadcast_to(vj[0:1], (8, c))
            v = dh_buf[pl.ds(t0, r), :]
            hprev = _shift_dn(_window_dn(hs_ref, t0, r, first), 1, r)
            xc = xc_buf[pl.ds(t0, r), :]
            ga = ga_buf[pl.ds(t0, r), :]
            gx = gx_buf[pl.ds(t0, r), :]
            a = a_buf[pl.ds(t0, r), :]
            em = _neg_expm1(-2.0 * LRU_C * ga * sp)
            mult = jnp.sqrt(em)
            dla = v * hprev * a - (v * gx * xc) * ((1.0 - em) / mult)
            dgx = v * mult * xc
            dxc = v * mult * gx
            dga = dla * (-LRU_C) * sp
            small_ref[7:8, :] += jnp.sum(dla * (-LRU_C) * ga, axis=0, keepdims=True)
            dpa = dga * ga * (1.0 - ga)
            dpx = dgx * gx * (1.0 - gx)
            small_ref[5:6, :] += jnp.sum(dpa, axis=0, keepdims=True)
            small_ref[6:7, :] += jnp.sum(dpx, axis=0, keepdims=True)
            dpab = dpa.astype(BF)
            dpxb = dpx.astype(BF)
            xb = xc.astype(BF)
            dxc = dxc + _dot_nt(dpab, wa) + _dot_nt(dpxb, wx)
            dwa_ref[...] += _dot_tn(xb, dpab)
            dwx_ref[...] += _dot_tn(xb, dpxb)
            dxc_buf[pl.ds(t0, r), :] = dxc
            small_ref[4:5, :] += jnp.sum(dxc, axis=0, keepdims=True)
            dwin = dxc_buf[pl.ds(t0, r + 8), :]
            dx_ref[pl.ds(t0, r), :] = (w4v[3:4] * dxc + w4v[2:3] * _shift_up(dwin, 1, r)
                                       + w4v[1:2] * _shift_up(dwin, 2, r) + w4v[0:1] * _shift_up(dwin, 3, r)).astype(BF)
            xwin = _window_dn(x_ref, t0, r, first)
            for k in range(4):
                small_ref[k:k + 1, :] += jnp.sum(dxc * _shift_dn(xwin, 3 - k, r), axis=0, keepdims=True)
            return vnext

        def bloop(it, vnext):
            ci = MIX_CHUNKS - 1 - it
            return bwd_chunk(pl.multiple_of(ci * r, 16), vnext, False)

        vn = lax.fori_loop(0, MIX_CHUNKS - 1, bloop, jnp.zeros((8, c), F32))
        bwd_chunk(0, vn, True)
        small_ref[7:8, :] = small_ref[7:8, :] * (-_sigmoid(-lamv))

    col = lambda off: pl.BlockSpec((tp, c), lambda j: (0, off + j))
    vec = pl.BlockSpec((1, c), lambda j: (0, j))
    mat = pl.BlockSpec((None, c, c), lambda j: (j, 0, 0))
    buf = pltpu.VMEM((tp, c), F32)
    bufp = pltpu.VMEM((tp + 8, c), F32)
    return _grid_call(
        body, "lru_bwd", (nb,),
        [col(0), col(nb), col(0), col(0), pl.BlockSpec((8, c), lambda j: (0, j)), vec, mat, vec, mat, vec, vec, vec,
         pl.BlockSpec((c, c), lambda j: (0, 0))],
        [col(0), col(0), pl.BlockSpec((16, c), lambda j: (0, j)), mat, mat],
        [jax.ShapeDtypeStruct((tp, dl), BF), jax.ShapeDtypeStruct((tp, dl), BF), jax.ShapeDtypeStruct((16, dl), F32),
         jax.ShapeDtypeStruct((nb, c, c), F32), jax.ShapeDtypeStruct((nb, c, c), F32)],
        (z, z, hs, dmix, w4, cb, wa2, ba, wx2, bx, lam, g_out, gm), side, [buf, buf, buf, bufp, buf, bufp])


def _sc_conv(cvwin, w3, r):
    return w3[2:3] * _shift_dn(cvwin, 0, r) + w3[1:2] * _shift_dn(cvwin, 1, r) + w3[0:1] * _shift_dn(cvwin, 2, r)


def _sc_fwd(z, w3, g_out, gm, dl, side=None):
    tp = z.shape[0]
    nb = dl // LANE
    r = tp // MIX_CHUNKS
    c = LANE

    def body(b_ref, c_ref, v_ref, w3_ref, go_ref, gm_ref, m_ref):
        w3v = w3_ref[...]
        gov = go_ref[...]
        gmv = gm_ref[...]

        def chunk(t0, first):
            cvwin = _window_dn(c_ref, t0, r, first) * _window_dn(v_ref, t0, r, first)
            so = b_ref[pl.ds(t0, r), :] * _sc_conv(cvwin, w3v, r)
            rs = lax.rsqrt(_group_mean(so * so, gmv) + EPS)
            m_ref[pl.ds(t0, r), :] = (so * rs * gov).astype(BF)

        chunk(0, True)

        def loop(ci, carry):
            chunk(pl.multiple_of(ci * r, 16), False)
            return carry

        lax.fori_loop(1, MIX_CHUNKS, loop, 0)

    col = lambda off: pl.BlockSpec((tp, c), lambda j: (0, off + j))
    res, extra = _grid_call(
        body, "sconv_fwd", (nb,),
        [col(2 * nb), col(3 * nb), col(4 * nb), pl.BlockSpec((8, c), lambda j: (0, j)),
         pl.BlockSpec((1, c), lambda j: (0, j)), pl.BlockSpec((c, c), lambda j: (0, 0))],
        [col(0)], [jax.ShapeDtypeStruct((tp, dl), BF)], (z, z, z, w3, g_out, gm), side)
    return res[0], extra


def _sc_bwd(z, dmix, w3, g_out, gm, dl, side=None):
    tp = z.shape[0]
    nb = dl // LANE
    r = tp // MIX_CHUNKS
    c = LANE

    def body(b_ref, c_ref, v_ref, dm_ref, w3_ref, go_ref, gm_ref, db_ref, dc_ref, dv_ref, small_ref, dsc_buf):
        w3v = w3_ref[...]
        gov = go_ref[...]
        gmv = gm_ref[...]
        small_ref[...] = jnp.zeros_like(small_ref)
        dsc_buf[pl.ds(tp, 8), :] = jnp.zeros((8, c), F32)

        def chunk1(t0, first):
            cvwin = _window_dn(c_ref, t0, r, first) * _window_dn(v_ref, t0, r, first)
            sc = _sc_conv(cvwin, w3v, r)
            bv = b_ref[pl.ds(t0, r), :]
            so = bv * sc
            rs = lax.rsqrt(_group_mean(so * so, gmv) + EPS)
            xh = so * rs
            dm = dm_ref[pl.ds(t0, r), :]
            q = dm * gov
            dso = rs * (q - xh * _group_mean(q * xh, gmv))
            small_ref[3:4, :] += jnp.sum(dm * xh, axis=0, keepdims=True)
            db_ref[pl.ds(t0, r), :] = (dso * sc).astype(BF)
            dsc = dso * bv
            dsc_buf[pl.ds(t0, r), :] = dsc
            for k in range(3):
                small_ref[k:k + 1, :] += jnp.sum(dsc * _shift_dn(cvwin, 2 - k, r), axis=0, keepdims=True)

        chunk1(0, True)

        def loop1(ci, carry):
            chunk1(pl.multiple_of(ci * r, 16), False)
            return carry

        lax.fori_loop(1, MIX_CHUNKS, loop1, 0)

        def loop2(ci, carry):
            t0 = pl.multiple_of(ci * r, 16)
            dwin = dsc_buf[pl.ds(t0, r + 8), :]
            dcv = w3v[2:3] * _shift_up(dwin, 0, r) + w3v[1:2] * _shift_up(dwin, 1, r) + w3v[0:1] * _shift_up(dwin, 2, r)
            dc_ref[pl.ds(t0, r), :] = (dcv * v_ref[pl.ds(t0, r), :]).astype(BF)
            dv_ref[pl.ds(t0, r), :] = (dcv * c_ref[pl.ds(t0, r), :]).astype(BF)
            return carry

        lax.fori_loop(0, MIX_CHUNKS, loop2, 0)

    col = lambda off: pl.BlockSpec((tp, c), lambda j: (0, off + j))
    out = jax.ShapeDtypeStruct((tp, dl), BF)
    return _grid_call(
        body, "sconv_bwd", (nb,),
        [col(2 * nb), col(3 * nb), col(4 * nb), col(nb), pl.BlockSpec((8, c), lambda j: (0, j)),
         pl.BlockSpec((1, c), lambda j: (0, j)), pl.BlockSpec((c, c), lambda j: (0, 0))],
        [col(0), col(0), col(0), pl.BlockSpec((8, c), lambda j: (0, j))],
        [out, out, out, jax.ShapeDtypeStruct((8, dl), F32)], (z, z, z, dmix, w3, g_out, gm), side,
        [pltpu.VMEM((tp + 8, c), F32)])


def _cast_pad(w, rows_p, cols_p, chip, name):
    r, c = w.shape

    def body(chip_ref, w_ref, o_ref):
        if (rows_p, cols_p) != (r, c):
            o_ref[...] = jnp.zeros_like(o_ref)
        o_ref[0:r, 0:c] = w_ref[...].astype(BF)

    return pl.pallas_call(
        body, name=name, out_shape=jax.ShapeDtypeStruct((N_CHIP, rows_p, cols_p), BF),
        grid_spec=pltpu.PrefetchScalarGridSpec(
            num_scalar_prefetch=1, grid=(1,),
            in_specs=[pl.BlockSpec((r, c), lambda i, chip: (0, 0))],
            out_specs=pl.BlockSpec((None, rows_p, cols_p), lambda i, chip: (chip[0], 0, 0))),
        compiler_params=_params(("arbitrary",)),
    )(chip, w)


def _adamw_math(w, g, m, v):
    m2 = ADAM_B1 * m + (1.0 - ADAM_B1) * g
    v2 = ADAM_B2 * v + (1.0 - ADAM_B2) * (g * g)
    m_hat = m2 / (1.0 - ADAM_B1 ** ADAM_STEP)
    v_hat = v2 / (1.0 - ADAM_B2 ** ADAM_STEP)
    delta = -ADAM_LR * (m_hat / (jnp.sqrt(v_hat) + ADAM_EPS) + ADAM_WD * w)
    return delta, m2, v2


def _adamw(w, g, m, v, name, row_tiles, col_tiles, side=None):
    r, c = w.shape
    tr = r // row_tiles
    tc = c // col_tiles
    gc = g.shape[1] if col_tiles == 1 else tc

    def body(w_ref, g_ref, m_ref, v_ref, go_ref, d_ref, mo_ref, vo_ref):
        gv = g_ref[...][:, 0:tc]
        delta, m2, v2 = _adamw_math(w_ref[...], gv, m_ref[...], v_ref[...])
        go_ref[...] = gv
        d_ref[...] = delta
        mo_ref[...] = m2
        vo_ref[...] = v2

    spec = pl.BlockSpec((tr, tc), lambda i, j: (i, j))
    out = jax.ShapeDtypeStruct((r, c), F32)
    return _grid_call(body, name, (row_tiles, col_tiles), [spec, pl.BlockSpec((tr, gc), lambda i, j: (i, j)), spec, spec],
                      [spec] * 4, [out] * 4, (w, g, m, v), side)


def _adamw_small(w, g_top, g4, m, v):
    def body(w_ref, gt_ref, g_ref, m_ref, v_ref, go_ref, d_ref, mo_ref, vo_ref):
        g = jnp.concatenate([gt_ref[...], (g_ref[0] + g_ref[1]) + (g_ref[2] + g_ref[3])], axis=0)
        delta, m2, v2 = _adamw_math(w_ref[...], g, m_ref[...], v_ref[...])
        go_ref[...] = g
        d_ref[...] = delta
        mo_ref[...] = m2
        vo_ref[...] = v2

    out = jax.ShapeDtypeStruct(w.shape, F32)
    spec = pl.BlockSpec(w.shape, lambda: (0, 0))
    return pl.pallas_call(
        body, name="adamw_small",
        in_specs=[spec, pl.BlockSpec(g_top.shape, lambda: (0, 0)), pl.BlockSpec(g4.shape, lambda: (0, 0, 0)), spec, spec],
        out_specs=[spec] * 4, out_shape=[out] * 4, compiler_params=_params())(w, g_top, g4, m, v)


def _place():
    x, y, c = lax.axis_index("x"), lax.axis_index("y"), lax.axis_index("c")
    chips = [(1 - x, y), (x, 1 - y), (1 - x, 1 - y)]
    return x, y, c, chips


ANY = pl.BlockSpec(memory_space=pl.ANY)


REL_SLOT = (2, 1, 3)


def _gather_side(bufs, relative=False, two_path=False):
    n = len(bufs)
    direct = (0, 1) if two_path else (0, 1, 2)

    def copies(outs, sems):
        s_ici, r_ici, s_d2d, r_d2d = sems[:4]
        x, y, c, chips = _place()
        me = 2 * x + y

        def rows(w, slot, core, part=None):
            half = bufs[w].shape[1] // 2
            if part is None:
                return outs[w].at[slot, pl.ds(core * half, half)]
            return outs[w].at[slot, pl.ds(core * half + part * (half // 2), half // 2)]

        def theirs(j):
            return REL_SLOT[j] if relative else 2 * chips[j][0] + chips[j][1]

        def ici_send(w, j):
            px, py = chips[j]
            return pltpu.make_async_remote_copy(
                src_ref=rows(w, 0 if relative else me, c), dst_ref=rows(w, REL_SLOT[j] if relative else me, c),
                send_sem=s_ici.at[w, j], recv_sem=r_ici.at[w, j], device_id=(px, py, c), device_id_type=MESH)

        def ici_recv(w, j):
            px, py = chips[j]
            return pltpu.make_async_remote_copy(
                src_ref=rows(w, theirs(j), c), dst_ref=rows(w, theirs(j), c),
                send_sem=s_ici.at[w, j], recv_sem=r_ici.at[w, j], device_id=(px, py, c), device_id_type=MESH)

        def hop_send(w, p):
            px, py = chips[1 - p]
            return pltpu.make_async_remote_copy(
                src_ref=rows(w, theirs(p), c, p), dst_ref=rows(w, REL_SLOT[2] if relative else theirs(p), c, p),
                send_sem=sems[4].at[w, p], recv_sem=sems[5].at[w, p], device_id=(px, py, c), device_id_type=MESH)

        def hop_recv(w, p):
            px, py = chips[1 - p]
            return pltpu.make_async_remote_copy(
                src_ref=rows(w, theirs(2), c, p), dst_ref=rows(w, theirs(2), c, p),
                send_sem=sems[4].at[w, p], recv_sem=sems[5].at[w, p], device_id=(px, py, c), device_id_type=MESH)

        def d2d(w, j, core):
            return pltpu.make_async_remote_copy(
                src_ref=rows(w, theirs(j), core), dst_ref=rows(w, theirs(j), core),
                send_sem=s_d2d.at[w, j], recv_sem=r_d2d.at[w, j], device_id=(x, y, 1 - c), device_id_type=MESH)

        return c, ici_send, ici_recv, hop_send, hop_recv, d2d

    def send(outs, sems):
        c, ici_send, ici_recv, hop_send, hop_recv, d2d = copies(outs, sems)
        for j in direct:
            for w in range(n):
                ici_send(w, j).start()

    def arrived(j, outs, sems):
        c, ici_send, ici_recv, hop_send, hop_recv, d2d = copies(outs, sems)
        for w in range(n):
            if j in direct:
                ici_recv(w, j).wait_recv()
                if two_path:
                    hop_send(w, j).start()
            else:
                hop_recv(w, 0).wait_recv()
                hop_recv(w, 1).wait_recv()
            d2d(w, j, c).start()

    def forwarded(j, outs, sems):
        c, ici_send, ici_recv, hop_send, hop_recv, d2d = copies(outs, sems)
        for w in range(n):
            d2d(w, j, 1 - c).wait_recv()

    def drain(outs, sems):
        c, ici_send, ici_recv, hop_send, hop_recv, d2d = copies(outs, sems)
        for w in range(n):
            for j in direct:
                ici_send(w, j).wait_send()
                if two_path:
                    hop_send(w, j).wait_send()
            for j in range(3):
                d2d(w, j, c).wait_send()

    def start(ins, outs, sems):
        send(outs, sems)

    def middle(ins, outs, sems):
        arrived(0, outs, sems)
        arrived(1, outs, sems)

    def rest(ins, outs, sems):
        arrived(2, outs, sems)
        for j in range(3):
            forwarded(j, outs, sems)
        drain(outs, sems)

    def finish(ins, outs, sems):
        middle(ins, outs, sems)
        rest(ins, outs, sems)

    dma = pltpu.SemaphoreType.DMA((n, 3))
    hop = [pltpu.SemaphoreType.DMA((n, 2))] * 2 if two_path else []
    side = _Side(list(bufs), [jax.ShapeDtypeStruct(b.shape, b.dtype) for b in bufs], {w: w for w in range(n)},
                 [dma, dma, dma, dma] + hop, start, finish)
    side.send, side.arrived, side.forwarded, side.drain = send, arrived, forwarded, drain
    side.middle, side.rest = middle, rest
    return side


def _run_side(side, name):
    sin, sout = len(side.ins), len(side.outs)

    def body(*refs):
        ins, outs, sems = refs[:sin], refs[sin:sin + sout], refs[sin + sout:]
        side.start(ins, outs, sems)
        side.finish(ins, outs, sems)

    return pl.pallas_call(
        body, name=name, out_shape=list(side.outs), in_specs=[ANY] * sin, out_specs=[ANY] * sout,
        scratch_shapes=list(side.sems), input_output_aliases=dict(side.alias))(*side.ins)


def _pair_exchange_side(grads):
    n = len(grads)

    def copies(ins, outs, sems):
        ssem, rsem = sems
        x, y, c, _ = _place()
        cps = []
        for w in range(n):
            half = grads[w].shape[1] // 2
            cps.append(pltpu.make_async_remote_copy(
                src_ref=ins[w].at[:, pl.ds((1 - c) * half, half)], dst_ref=outs[w],
                send_sem=ssem.at[w], recv_sem=rsem.at[w], device_id=(x, y, 1 - c), device_id_type=MESH))
        return cps

    def start(ins, outs, sems):
        for cp in copies(ins, outs, sems):
            cp.start()

    def finish(ins, outs, sems):
        for cp in copies(ins, outs, sems):
            cp.wait()

    dma = pltpu.SemaphoreType.DMA((n,))
    return _Side(list(grads), [jax.ShapeDtypeStruct((N_CHIP, g.shape[1] // 2, g.shape[2]), BF) for g in grads], {},
                 [dma, dma], start, finish)


def _sibling_copy_side(buf):
    def copy(ins, outs, sems):
        x, y, c, _ = _place()
        return pltpu.make_async_remote_copy(src_ref=ins[0], dst_ref=outs[0], send_sem=sems[0], recv_sem=sems[1],
                                            device_id=(x, y, 1 - c), device_id_type=MESH)

    return _Side([buf], [jax.ShapeDtypeStruct(buf.shape, buf.dtype)], {}, [pltpu.SemaphoreType.DMA, pltpu.SemaphoreType.DMA],
                 lambda i, o, s: copy(i, o, s).start(), lambda i, o, s: copy(i, o, s).wait())


def _slot_exchange_side(buf4):
    def copies(outs, sems, sending):
        ssem, rsem = sems
        x, y, c, chips = _place()
        me = 2 * x + y
        return [pltpu.make_async_remote_copy(
            src_ref=outs[0].at[me if sending else 2 * px + py], dst_ref=outs[0].at[me if sending else 2 * px + py],
            send_sem=ssem.at[j], recv_sem=rsem.at[j], device_id=(px, py, c), device_id_type=MESH)
            for j, (px, py) in enumerate(chips)]

    def start(ins, outs, sems):
        for cp in copies(outs, sems, True):
            cp.start()

    def finish(ins, outs, sems):
        for cp in copies(outs, sems, False):
            cp.wait_recv()
        for cp in copies(outs, sems, True):
            cp.wait_send()

    dma = pltpu.SemaphoreType.DMA((3,))
    return _Side([buf4], [jax.ShapeDtypeStruct(buf4.shape, buf4.dtype)], {0: 0}, [dma, dma], start, finish)


def _pair_sum(g, sib, core, name):
    _, r, cdim = g.shape
    half = r // 2

    def body(core_ref, g_ref, s_ref, o_ref):
        o_ref[...] = (g_ref[...].astype(F32) + s_ref[...].astype(F32)).astype(BF)

    return pl.pallas_call(
        body, name=name,
        grid_spec=pltpu.PrefetchScalarGridSpec(
            num_scalar_prefetch=1, grid=(N_CHIP,),
            in_specs=[pl.BlockSpec((None, half, cdim), lambda k, core: (k, core[0], 0)),
                      pl.BlockSpec((None, half, cdim), lambda k, core: (k, 0, 0))],
            out_specs=pl.BlockSpec((None, half, cdim), lambda k, core: (k, 0, 0))),
        out_shape=jax.ShapeDtypeStruct((N_CHIP, half, cdim), BF),
        compiler_params=_params(("arbitrary",)),
    )(core, g, sib)


def _chip_exchange_side(psums, relative=False):
    n = len(psums)

    def copies(ins, outs, sems):
        ssem, rsem = sems
        x, y, c, chips = _place()
        return [pltpu.make_async_remote_copy(
            src_ref=ins[w].at[REL_SLOT[j] if relative else 2 * px + py], dst_ref=outs[w].at[j],
            send_sem=ssem.at[w, j], recv_sem=rsem.at[w, j], device_id=(px, py, c), device_id_type=MESH)
            for w in range(n) for j, (px, py) in enumerate(chips)]

    def start(ins, outs, sems):
        for cp in copies(ins, outs, sems):
            cp.start()

    def finish(ins, outs, sems):
        for cp in copies(ins, outs, sems):
            cp.wait()

    dma = pltpu.SemaphoreType.DMA((n, 3))
    return _Side(list(psums), [jax.ShapeDtypeStruct((3,) + p.shape[1:], BF) for p in psums], {}, [dma, dma],
                 start, finish)


def _final_sum(g, sib, recv, sel, name):
    _, r, cdim = g.shape
    half = r // 2
    nt = 4
    th = half // nt

    def body(sel_ref, g_ref, s_ref, r_ref, o_ref):
        acc = g_ref[...].astype(F32) + s_ref[...].astype(F32)
        for j in range(3):
            acc = acc + r_ref[j].astype(F32)
        o_ref[...] = acc

    return pl.pallas_call(
        body, name=name,
        grid_spec=pltpu.PrefetchScalarGridSpec(
            num_scalar_prefetch=1, grid=(nt,),
            in_specs=[pl.BlockSpec((None, th, cdim), lambda i, sel: (sel[0], sel[1] * nt + i, 0)),
                      pl.BlockSpec((None, th, cdim), lambda i, sel: (sel[0], i, 0)),
                      pl.BlockSpec((3, th, cdim), lambda i, sel: (0, i, 0))],
            out_specs=pl.BlockSpec((th, cdim), lambda i, sel: (sel[1] * nt + i, 0))),
        out_shape=jax.ShapeDtypeStruct((r, cdim), F32),
        compiler_params=_params(("arbitrary",)),
    )(sel, g, sib, recv)


def _join_side(bufs):
    n = len(bufs)

    def copies(outs, sems, core_of):
        ssem, rsem = sems
        x, y, c, _ = _place()
        cps = []
        for w in range(n):
            half = bufs[w].shape[0] // 2
            rows = outs[w].at[pl.ds(core_of(c) * half, half)]
            cps.append(pltpu.make_async_remote_copy(
                src_ref=rows, dst_ref=rows, send_sem=ssem.at[w], recv_sem=rsem.at[w],
                device_id=(x, y, 1 - c), device_id_type=MESH))
        return cps

    def start(ins, outs, sems):
        for cp in copies(outs, sems, lambda c: c):
            cp.start()

    def finish(ins, outs, sems):
        for cp in copies(outs, sems, lambda c: 1 - c):
            cp.wait_recv()
        for cp in copies(outs, sems, lambda c: c):
            cp.wait_send()

    dma = pltpu.SemaphoreType.DMA((n,))
    return _Side(list(bufs), [jax.ShapeDtypeStruct(b.shape, F32) for b in bufs], {w: w for w in range(n)}, [dma, dma],
                 start, finish)


def _small_pair_sum(buf, sib, chip):
    rows, d = buf.shape

    def body(chip_ref, a_ref, b_ref, o_ref):
        o_ref[...] = a_ref[...] + b_ref[...]

    return pl.pallas_call(
        body, name="small_pair_sum", out_shape=jax.ShapeDtypeStruct((N_CHIP, rows, d), F32),
        grid_spec=pltpu.PrefetchScalarGridSpec(
            num_scalar_prefetch=1, grid=(1,),
            in_specs=[pl.BlockSpec((rows, d), lambda i, chip: (0, 0))] * 2,
            out_specs=pl.BlockSpec((None, rows, d), lambda i, chip: (chip[0], 0, 0))),
        compiler_params=_params(("arbitrary",)),
    )(chip, buf, sib)


def _small_all_reduce(buf, name):
    rows, d = buf.shape

    def body(in_ref, out_ref, sib, all4, ssem, rsem, psem, qsem):
        x, y, c, chips = _place()
        me = 2 * x + y
        to_sib = pltpu.make_async_remote_copy(src_ref=in_ref, dst_ref=sib, send_sem=ssem, recv_sem=rsem,
                                              device_id=(x, y, 1 - c), device_id_type=MESH)
        to_sib.start()
        to_sib.wait()
        all4[me] = in_ref[...] + sib[...]
        cps = [pltpu.make_async_remote_copy(src_ref=all4.at[me], dst_ref=all4.at[me], send_sem=psem.at[j],
                                            recv_sem=qsem.at[j], device_id=(px, py, c), device_id_type=MESH)
               for j, (px, py) in enumerate(chips)]
        for cp in cps:
            cp.start()
        for j, (px, py) in enumerate(chips):
            chip = 2 * px + py
            pltpu.make_async_remote_copy(src_ref=all4.at[chip], dst_ref=all4.at[chip], send_sem=psem.at[j],
                                         recv_sem=qsem.at[j], device_id=(px, py, c), device_id_type=MESH).wait_recv()
        for cp in cps:
            cp.wait_send()
        out_ref[...] = (all4[0] + all4[1]) + (all4[2] + all4[3])

    vm = pl.BlockSpec(memory_space=pltpu.VMEM)
    return pl.pallas_call(
        body, name=name, out_shape=jax.ShapeDtypeStruct((rows, d), F32),
        in_specs=[vm], out_specs=vm,
        scratch_shapes=[pltpu.VMEM((rows, d), F32), pltpu.VMEM((N_CHIP, rows, d), F32),
                        pltpu.SemaphoreType.DMA, pltpu.SemaphoreType.DMA,
                        pltpu.SemaphoreType.DMA((3,)), pltpu.SemaphoreType.DMA((3,))],
        compiler_params=_params(),
    )(buf)


def _pair_blocks(w):
    w4 = w.reshape(N_HEADS // 2, 2, HEAD, HEAD)
    eye = jnp.eye(2, dtype=w.dtype)
    return jnp.einsum("pirc,ij->pirjc", w4, eye).reshape(N_HEADS // 2, LANE, LANE)


def _unpair_blocks(w2):
    w5 = w2.reshape(N_HEADS // 2, 2, HEAD, 2, HEAD)
    return jnp.stack([w5[:, 0, :, 0, :], w5[:, 1, :, 1, :]], axis=1).reshape(N_HEADS, HEAD, HEAD)


def kernel(x, meta_tokens, ffn1_pre_g, ffn1_w_gate, ffn1_w_up, ffn1_w_down, ffn1_post_g, mix_pre_g, w_in, lru_conv_w, lru_conv_b, lru_w_a, lru_b_a, lru_w_x, lru_b_x, lru_lambda, sconv_w, lru_out_g, sconv_out_g, w_out, mix_post_g, ffn2_pre_g, ffn2_w_gate, ffn2_w_up, ffn2_w_down, ffn2_post_g, loss_target, m_meta_tokens, m_ffn1_pre_g, m_ffn1_w_gate, m_ffn1_w_up, m_ffn1_w_down, m_ffn1_post_g, m_mix_pre_g, m_w_in, m_lru_conv_w, m_lru_conv_b, m_lru_w_a, m_lru_b_a, m_lru_w_x, m_lru_b_x, m_lru_lambda, m_sconv_w, m_lru_out_g, m_sconv_out_g, m_w_out, m_mix_post_g, m_ffn2_pre_g, m_ffn2_w_gate, m_ffn2_w_up, m_ffn2_w_down, m_ffn2_post_g, v_meta_tokens, v_ffn1_pre_g, v_ffn1_w_gate, v_ffn1_w_up, v_ffn1_w_down, v_ffn1_post_g, v_mix_pre_g, v_w_in, v_lru_conv_w, v_lru_conv_b, v_lru_w_a, v_lru_b_a, v_lru_w_x, v_lru_b_x, v_lru_lambda, v_sconv_w, v_lru_out_g, v_sconv_out_g, v_w_out, v_mix_post_g, v_ffn2_pre_g, v_ffn2_w_gate, v_ffn2_w_up, v_ffn2_w_down, v_ffn2_post_g):
    seq, d = x.shape[1], x.shape[2]
    t_real = N_META + seq
    tp = _round_up(t_real, ROW_ALIGN)
    f4 = ffn1_w_gate.shape[2]
    f4p = _round_up(f4, LANE)
    dl = lru_conv_b.shape[1]
    cin = w_in.shape[2]
    xi, yi, ci = lax.axis_index("x"), lax.axis_index("y"), lax.axis_index("c")
    chip = 2 * xi + yi
    zero = jnp.zeros((), jnp.int32)

    transposed = ("ffn1_w_gate", "ffn1_w_up", "ffn2_w_gate", "ffn2_w_up")

    def view(k, a):
        return a[0].T if k in transposed else a[0]

    def unview(k, a):
        return (a.T if k in transposed else a)[None]

    big = {
        "ffn1_w_gate": (view("ffn1_w_gate", ffn1_w_gate), f4p, d), "ffn1_w_up": (view("ffn1_w_up", ffn1_w_up), f4p, d),
        "ffn1_w_down": (ffn1_w_down[0], f4p, d), "w_in": (w_in[0], d, cin), "w_out": (w_out[0], w_out.shape[1], d),
        "ffn2_w_gate": (view("ffn2_w_gate", ffn2_w_gate), f4p, d), "ffn2_w_up": (view("ffn2_w_up", ffn2_w_up), f4p, d),
        "ffn2_w_down": (ffn2_w_down[0], f4p, d),
    }
    names = list(big)
    chip1 = jnp.reshape(chip, (1,)).astype(jnp.int32)
    relative = {k: k.startswith("ffn") for k in names}
    slot0 = jnp.zeros((1,), jnp.int32)
    shard = {k: _cast_pad(big[k][0], big[k][1], big[k][2], slot0 if relative[k] else chip1, "cast_" + k) for k in names}
    full = {}

    def gather(*keys):
        return _merge_sides([_gather_side([shard[k]], relative[k], two_path=True) for k in keys])

    gm = jnp.kron(jnp.eye(2, dtype=F32), jnp.full((HEAD, HEAD), 1.0 / HEAD, F32)).astype(BF)
    wa2 = _pair_blocks(lru_w_a[0])
    wx2 = _pair_blocks(lru_w_x[0])

    dlq = dl // N_CHIP
    dq = d // N_CHIP
    R_GAIN, R_LOSS, R_META, R_LRU, R_SC, R_WA = 0, 6, 8, 24, 40, 48
    n_wrows = (N_HEADS // 2) * LANE * LANE // d
    R_WX = R_WA + n_wrows
    R_END = R_WX + n_wrows

    def pack_top(gains, meta, loss=None):
        lossrow = jnp.zeros((2, d), F32)
        if loss is not None:
            lossrow = lossrow.at[0, 0].set(loss)
        return jnp.concatenate([jnp.concatenate(gains, axis=0), lossrow, meta], axis=0)

    def pack_rest(lru16, sc8, wa_, wx_):
        return jnp.concatenate([jnp.concatenate([lru16, jnp.zeros((16, d - dl), F32)], axis=1),
                                jnp.concatenate([sc8, jnp.zeros((8, d - dl), F32)], axis=1),
                                wa_.reshape(n_wrows, d), wx_.reshape(n_wrows, d)], axis=0)

    def pack(gains, meta, lru16, sc8, wa_, wx_):
        return jnp.concatenate([pack_top(gains, meta), pack_rest(lru16, sc8, wa_, wx_)], axis=0)

    def place_cols(blk, width, total):
        return lax.dynamic_update_slice(jnp.zeros((blk.shape[0], total), F32), blk, (zero, chip * width))

    def pack_params(meta_, g1pre, g1post, gmpre, gmpost, g2pre, g2post, cw, cbias, wa_, ba_, wx_, bx_, lam_, sw, lgo, sgo):
        lru16 = jnp.concatenate([place_cols(cw[0], dlq, dl), cbias, ba_, bx_, lam_, lgo, jnp.zeros((7, dl), F32)], axis=0)
        sc8 = jnp.concatenate([place_cols(sw[0], dlq, dl), sgo, jnp.zeros((4, dl), F32)], axis=0)
        return pack([g1pre, g1post, gmpre, gmpost, g2pre, g2post], place_cols(meta_, dq, d), lru16, sc8,
                    _pair_blocks(wa_[0]), _pair_blocks(wx_[0]))

    p_w = pack_params(meta_tokens, ffn1_pre_g, ffn1_post_g, mix_pre_g, mix_post_g, ffn2_pre_g, ffn2_post_g, lru_conv_w,
                      lru_conv_b, lru_w_a, lru_b_a, lru_w_x, lru_b_x, lru_lambda, sconv_w, lru_out_g, sconv_out_g)
    p_m = pack_params(m_meta_tokens, m_ffn1_pre_g, m_ffn1_post_g, m_mix_pre_g, m_mix_post_g, m_ffn2_pre_g, m_ffn2_post_g,
                      m_lru_conv_w, m_lru_conv_b, m_lru_w_a, m_lru_b_a, m_lru_w_x, m_lru_b_x, m_lru_lambda, m_sconv_w,
                      m_lru_out_g, m_sconv_out_g)
    p_v = pack_params(v_meta_tokens, v_ffn1_pre_g, v_ffn1_post_g, v_mix_pre_g, v_mix_post_g, v_ffn2_pre_g, v_ffn2_post_g,
                      v_lru_conv_w, v_lru_conv_b, v_lru_w_a, v_lru_b_a, v_lru_w_x, v_lru_b_x, v_lru_lambda, v_sconv_w,
                      v_lru_out_g, v_sconv_out_g)

    gathered = _small_all_reduce(jnp.where(ci == 0, p_w, 0.0)[R_META:R_WA], "small_weight_gather")
    meta_full = gathered[0:N_META]
    w4_full = gathered[R_LRU - R_META:R_LRU - R_META + 4, 0:dl]
    w3_full = gathered[R_SC - R_META:R_SC - R_META + 3, 0:dl]
    w4p = jnp.concatenate([w4_full, jnp.zeros((4, dl), F32)], axis=0)
    w3p = jnp.concatenate([w3_full, jnp.zeros((5, dl), F32)], axis=0)

    h0 = jnp.concatenate([meta_full, x[0], jnp.zeros((tp - t_real, d), F32)], axis=0)
    tgt = jnp.concatenate([jnp.zeros((N_META, d), F32), loss_target[0], jnp.zeros((tp - t_real, d), F32)], axis=0)

    n1 = _norm0(h0, ffn1_pre_g)
    (a1, b1, s1), (full["ffn1_w_gate"], full["ffn1_w_up"]), got = _ffn_up_head(
        n1, shard["ffn1_w_gate"], shard["ffn1_w_up"], "ffn1_up",
        _gather_side([shard["ffn1_w_down"]], relative=True, two_path=True))
    full["ffn1_w_down"] = got[0]
    f1, got = _wide_matmul([(s1, full["ffn1_w_down"])], "ffn1_down", WIDE_TN, gather("w_in"))
    full["w_in"] = got[0]
    h1, u = _post_fwd(f1, h0, ffn1_post_g, mix_pre_g, 0.5, "ffn1_post")
    z, got = _col_matmul(u, full["w_in"], "in_proj", False, F32, gather("ffn2_w_gate"))
    full["ffn2_w_gate"] = got[0]
    (m_lru, hs), got = _lru_fwd(z, w4p, lru_conv_b, wa2.astype(BF), lru_b_a, wx2.astype(BF), lru_b_x, lru_lambda,
                                lru_out_g, gm, gather("ffn2_w_up"))
    full["ffn2_w_up"] = got[0]
    m_sc, got = _sc_fwd(z, w3p, sconv_out_g, gm, dl, gather("w_out"))
    full["w_out"] = got[0]
    mixed = jnp.concatenate([m_lru, m_sc], axis=1)
    p, _ = _row_matmul([(mixed, full["w_out"])], "out_proj", False, d)
    h2, n2 = _post_fwd(p, h1, mix_post_g, ffn2_pre_g, 1.0, "mix_post")
    (a2, b2, s2), got = _ffn_up(n2, full["ffn2_w_gate"], full["ffn2_w_up"], "ffn2_up", gather("ffn2_w_down"))
    full["ffn2_w_down"] = got[0]
    f2, _ = _wide_matmul([(s2, full["ffn2_w_down"])], "ffn2_down", WIDE_TN)
    dh3, df2, dg_ffn2_post, loss_part = _loss_bwd(f2, h2, tgt, ffn2_post_g, t_real)

    core = jnp.reshape(ci, (1,)).astype(jnp.int32)
    sel_of = {False: jnp.stack([chip, ci]).astype(jnp.int32), True: jnp.stack([0 * chip, ci]).astype(jnp.int32)}
    red = {}

    def pair_side(k):
        return _pair_exchange_side([red[k][0]])

    def chip_side(k):
        return _chip_exchange_side([_pair_sum(red[k][0], red[k][1], core, "pair_sum_" + k)], relative[k])

    def final_sum(k):
        return _final_sum(*red[k], sel_of[relative[k]], "final_sum_" + k)

    (da2, db2), _ = _ffn_bwd_act(df2, full["ffn2_w_down"], a2, b2, "ffn2_bwd_act")
    g, _ = _wgrad_call(s2, df2, "ffn2_down_wgrad", x_width=f4p, tile_y=WGRAD_TILE_Y)
    red["ffn2_w_down"] = [g, None, None]
    g, got = _wgrad_call(da2, n2, "ffn2_gate_wgrad", x_width=f4p, tile_y=WGRAD_TILE_Y, side=pair_side("ffn2_w_down"))
    red["ffn2_w_down"][1] = got[0]
    red["ffn2_w_gate"] = [g, None, None]
    g, got = _wgrad_call(db2, n2, "ffn2_up_wgrad", x_width=f4p, tile_y=WGRAD_TILE_Y,
                         side=_merge_sides([pair_side("ffn2_w_gate"), chip_side("ffn2_w_down")]))
    red["ffn2_w_gate"][1], red["ffn2_w_down"][2] = got
    red["ffn2_w_up"] = [g, None, None]
    dn2, got = _row_matmul([(da2, full["ffn2_w_gate"]), (db2, full["ffn2_w_up"])], "ffn2_bwd_up", False, d,
                           _merge_sides([pair_side("ffn2_w_up"), chip_side("ffn2_w_gate")]), tiles=MM_TILES)
    red["ffn2_w_up"][1], red["ffn2_w_gate"][2] = got
    dh2, dp, dg_ffn2_pre, dg_mix_post = _pre_bwd(dn2, h2, dh3, ffn2_pre_g, "ffn2_pre_bwd", (p, mix_post_g, 1.0))
    dmixed, _ = _col_matmul(dp, full["w_out"], "out_proj_bwd", True, F32)
    g, _ = _wgrad_call(mixed, dp, "w_out_wgrad", x_width=mixed.shape[1] // N_CHIP, tile_y=WGRAD_TILE_Y)
    red["w_out"] = [g, None, None]
    (dzy, dzx, lru_small, dwa2, dwx2), got = _lru_bwd(
        z, hs, dmixed, w4p, lru_conv_b, wa2.astype(BF), lru_b_a, wx2.astype(BF), lru_b_x, lru_lambda, lru_out_g, gm,
        _merge_sides([pair_side("w_out"), chip_side("ffn2_w_up")]))
    red["w_out"][1], red["ffn2_w_up"][2] = got
    (dzb, dzc, dzv, sc_small), got = _sc_bwd(z, dmixed, w3p, sconv_out_g, gm, dl, chip_side("w_out"))
    red["w_out"][2] = got[0]
    dz = jnp.concatenate([dzy, dzx, dzb, dzc, dzv], axis=1)
    p_rest = pack_rest(lru_small, sc_small, dwa2, dwx2)
    g, got = _wgrad_call(u, dz, "w_in_wgrad", y_width=cin, tile_x=WGRAD_TILE_X, side=_sibling_copy_side(p_rest))
    p_rest4 = _small_pair_sum(p_rest, got[0], chip1)
    red["w_in"] = [g, None, None]
    du, got = _row_matmul([(dz, full["w_in"])], "in_proj_bwd", True, d,
                          _merge_sides([pair_side("w_in"), _slot_exchange_side(p_rest4)]))
    red["w_in"][1], p_rest4 = got
    dh1, df1, dg_mix_pre, dg_ffn1_post = _pre_bwd(du, h1, dh2, mix_pre_g, "mix_pre_bwd", (f1, ffn1_post_g, 0.5))
    (da1, db1), got = _ffn_bwd_act(df1, full["ffn1_w_down"], a1, b1, "ffn1_bwd_act", chip_side("w_in"))
    red["w_in"][2] = got[0]
    early = ["ffn2_w_down", "ffn2_w_gate", "ffn2_w_up", "w_out", "w_in"]
    late = ["ffn1_w_down", "ffn1_w_gate", "ffn1_w_up"]
    g, got = _wgrad_call(s1, df1, "ffn1_down_wgrad", x_width=f4p, tile_y=WGRAD_TILE_Y,
                         side=_join_side([final_sum(k) for k in early]))
    gfull = dict(zip(early, got))
    red["ffn1_w_down"] = [g, None, None]
    g, got = _wgrad_call(da1, n1, "ffn1_gate_wgrad", x_width=f4p, tile_y=WGRAD_TILE_Y, side=pair_side("ffn1_w_down"))
    red["ffn1_w_down"][1] = got[0]
    red["ffn1_w_gate"] = [g, None, None]
    g, got = _wgrad_call(db1, n1, "ffn1_up_wgrad", x_width=f4p, tile_y=WGRAD_TILE_Y,
                         side=_merge_sides([pair_side("ffn1_w_gate"), chip_side("ffn1_w_down")]))
    red["ffn1_w_gate"][1], red["ffn1_w_down"][2] = got
    red["ffn1_w_up"] = [g, None, None]
    red["ffn1_w_up"][1] = _run_side(pair_side("ffn1_w_up"), "pair_exchange_ffn1_w_up")[0]
    dn1, got = _row_matmul([(da1, full["ffn1_w_gate"]), (db1, full["ffn1_w_up"])], "ffn1_bwd_up", False, d,
                           _merge_sides([chip_side("ffn1_w_gate"), chip_side("ffn1_w_up")]), tiles=MM_TILES)
    red["ffn1_w_gate"][2], red["ffn1_w_up"][2] = got
    (dh0, dg_ffn1_pre), got = _pre_bwd(dn1, h0, dh1, ffn1_pre_g, "ffn1_pre_bwd",
                                       side=_join_side([final_sum(k) for k in late]))
    gfull.update(zip(late, got))

    grad_x = dh0[N_META:t_real][None]

    w_big = {"ffn1_w_gate": ffn1_w_gate, "ffn1_w_up": ffn1_w_up, "ffn1_w_down": ffn1_w_down, "w_in": w_in, "w_out": w_out,
             "ffn2_w_gate": ffn2_w_gate, "ffn2_w_up": ffn2_w_up, "ffn2_w_down": ffn2_w_down}
    m_big = {"ffn1_w_gate": m_ffn1_w_gate, "ffn1_w_up": m_ffn1_w_up, "ffn1_w_down": m_ffn1_w_down, "w_in": m_w_in,
             "w_out": m_w_out, "ffn2_w_gate": m_ffn2_w_gate, "ffn2_w_up": m_ffn2_w_up, "ffn2_w_down": m_ffn2_w_down}
    v_big = {"ffn1_w_gate": v_ffn1_w_gate, "ffn1_w_up": v_ffn1_w_up, "ffn1_w_down": v_ffn1_w_down, "w_in": v_w_in,
             "w_out": v_w_out, "ffn2_w_gate": v_ffn2_w_gate, "ffn2_w_up": v_ffn2_w_up, "ffn2_w_down": v_ffn2_w_down}
    b_grad, b_delta, b_newm, b_newv = {}, {}, {}, {}

    def big_adamw(k, side=None):
        wv, mv, vv = view(k, w_big[k]), view(k, m_big[k]), view(k, v_big[k])
        wide_rows = wv.shape[0] % 64 == 0
        (g_, d_, m_, v_), got = _adamw(wv, gfull[k], mv, vv, "adamw_" + k, 8 if wide_rows else 4, 1 if wide_rows else 2,
                                       side)
        b_grad[k], b_delta[k], b_newm[k], b_newv[k] = unview(k, g_), unview(k, d_), unview(k, m_), unview(k, v_)
        return got

    p_top = _small_all_reduce(
        pack_top([dg_ffn1_pre, dg_ffn1_post, dg_mix_pre, dg_mix_post, dg_ffn2_pre, dg_ffn2_post], dh0[0:N_META],
                 loss=loss_part[0, 0]), "small_grad_all_reduce")
    p_g, p_delta, p_newm, p_newv = _adamw_small(p_w, p_top, p_rest4, p_m, p_v)
    loss = p_g[R_LOSS, 0]
    for k in names:
        big_adamw(k)

    def unpack(buf):
        out = {}
        for i, k in enumerate(["ffn1_pre_g", "ffn1_post_g", "mix_pre_g", "mix_post_g", "ffn2_pre_g", "ffn2_post_g"]):
            out[k] = buf[R_GAIN + i:R_GAIN + i + 1]
        out["meta_tokens"] = lax.dynamic_slice(buf[R_META:R_META + N_META], (zero, chip * dq), (N_META, dq))
        lru = buf[R_LRU:R_LRU + 16, 0:dl]
        out["lru_conv_w"] = lax.dynamic_slice(lru[0:4], (zero, chip * dlq), (4, dlq))[None]
        out["lru_conv_b"] = lru[4:5]
        out["lru_b_a"] = lru[5:6]
        out["lru_b_x"] = lru[6:7]
        out["lru_lambda"] = lru[7:8]
        out["lru_out_g"] = lru[8:9]
        sc = buf[R_SC:R_SC + 8, 0:dl]
        out["sconv_w"] = lax.dynamic_slice(sc[0:3], (zero, chip * dlq), (3, dlq))[None]
        out["sconv_out_g"] = sc[3:4]
        out["lru_w_a"] = _unpair_blocks(buf[R_WA:R_WX].reshape(N_HEADS // 2, LANE, LANE))[None]
        out["lru_w_x"] = _unpair_blocks(buf[R_WX:R_END].reshape(N_HEADS // 2, LANE, LANE))[None]
        return out

    s_grad, s_delta, s_newm, s_newv = unpack(p_g), unpack(p_delta), unpack(p_newm), unpack(p_newv)

    order = ["meta_tokens", "ffn1_pre_g", "ffn1_w_gate", "ffn1_w_up", "ffn1_w_down", "ffn1_post_g", "mix_pre_g", "w_in",
             "lru_conv_w", "lru_conv_b", "lru_w_a", "lru_b_a", "lru_w_x", "lru_b_x", "lru_lambda", "sconv_w", "lru_out_g",
             "sconv_out_g", "w_out", "mix_post_g", "ffn2_pre_g", "ffn2_w_gate", "ffn2_w_up", "ffn2_w_down", "ffn2_post_g"]

    def pick(small, bigd):
        return [bigd[k] if k in bigd else small[k] for k in order]

    return (loss, grad_x, *pick(s_grad, b_grad), *pick(s_delta, b_delta), *pick(s_newm, b_newm), *pick(s_newv, b_newv))
```

```python
import functools
import math

import jax
import jax.numpy as jnp
from jax import lax
from jax.experimental import pallas as pl
from jax.experimental.pallas import tpu as pltpu

F32 = jnp.float32
BF = jnp.bfloat16
MESH = pl.DeviceIdType.MESH

EPS = 1e-6
N_META = 16
N_HEADS = 16
HEAD = 64
LRU_C = 8.0
LANE = 128
MXU_COLS = 256
N_CHIP = 4
ROW_ALIGN = 384
MM_TILES = 8
MM_TILES_BIG = 4
SWIGLU_TILES = 6
EW_TILES = 12
MIX_CHUNKS = 24
WGRAD_TILE_X = 256
WGRAD_TILE_Y = 512
WIDE_TN = 512
VMEM_LIMIT = 56 << 20

ADAM_LR = 0.001
ADAM_B1 = 0.9
ADAM_B2 = 0.999
ADAM_EPS = 1e-08
ADAM_WD = 0.01
ADAM_STEP = 10


def _round_up(a, b):
    return (a + b - 1) // b * b


def _params(sem=None):
    if sem is None:
        return pltpu.CompilerParams(vmem_limit_bytes=VMEM_LIMIT)
    return pltpu.CompilerParams(dimension_semantics=sem, vmem_limit_bytes=VMEM_LIMIT)


def _sigmoid(x):
    return 0.5 * jnp.tanh(0.5 * x) + 0.5


def _dot(a, b):
    return jnp.dot(a, b, preferred_element_type=F32)


def _dot_nt(a, b):
    return lax.dot_general(a, b, (((1,), (1,)), ((), ())), preferred_element_type=F32)


def _dot_tn(a, b):
    return lax.dot_general(a, b, (((0,), (0,)), ((), ())), preferred_element_type=F32)


def _rms(x, g):
    r = lax.rsqrt(jnp.mean(x * x, axis=-1, keepdims=True) + EPS)
    return x * r * g


def _rms_bwd(x, g, dy):
    r = lax.rsqrt(jnp.mean(x * x, axis=-1, keepdims=True) + EPS)
    xh = x * r
    q = dy * g
    dx = r * (q - xh * jnp.mean(q * xh, axis=-1, keepdims=True))
    return dx, dy * xh


class _Side:
    def __init__(self, ins, outs, alias, sems, start, finish):
        self.ins, self.outs, self.alias, self.sems, self.start, self.finish = ins, outs, alias, sems, start, finish


def _merge_sides(sides):
    sides = [s for s in sides if s is not None]
    if len(sides) <= 1:
        return sides[0] if sides else None
    ins, outs, sems, alias, spans = [], [], [], {}, []
    for s in sides:
        for i, o in s.alias.items():
            alias[len(ins) + i] = len(outs) + o
        spans.append((len(ins), len(ins) + len(s.ins), len(outs), len(outs) + len(s.outs), len(sems),
                      len(sems) + len(s.sems)))
        ins += list(s.ins)
        outs += list(s.outs)
        sems += list(s.sems)

    def run(which):
        def go(in_refs, out_refs, sem_refs):
            for s, (a, b, c, d, e, f) in zip(sides, spans):
                getattr(s, which)(in_refs[a:b], out_refs[c:d], sem_refs[e:f])
        return go

    return _Side(ins, outs, alias, sems, run("start"), run("finish"))


def _grid_call(body, name, grid, in_specs, out_specs, out_shape, args, side=None, scratch=()):
    sem = ("arbitrary",) * len(grid)
    if side is None:
        res = pl.pallas_call(body, name=name, grid=grid, in_specs=in_specs, out_specs=out_specs, out_shape=out_shape,
                             scratch_shapes=list(scratch), compiler_params=_params(sem))(*args)
        return res, []
    nin, nout, sin, sout = len(in_specs), len(out_specs), len(side.ins), len(side.outs)
    nscr = len(scratch)
    staged = hasattr(side, "middle") and math.prod(grid) >= 4
    lin, mid = (math.prod(grid) * 5) // 8, []
    for extent in reversed(grid):
        mid.insert(0, lin % extent)
        lin //= extent

    def full(*refs):
        base_in, side_in = refs[:nin], refs[nin:nin + sin]
        base_out = refs[nin + sin:nin + sin + nout]
        side_out = refs[nin + sin + nout:nin + sin + nout + sout]
        base_scr = refs[nin + sin + nout + sout:nin + sin + nout + sout + nscr]
        sems = refs[nin + sin + nout + sout + nscr:]
        first = pl.program_id(0) == 0
        last = pl.program_id(0) == grid[0] - 1
        for ax in range(1, len(grid)):
            first = first & (pl.program_id(ax) == 0)
            last = last & (pl.program_id(ax) == grid[ax] - 1)

        @pl.when(first)
        def _():
            side.start(side_in, side_out, sems)

        if staged:
            at_mid = pl.program_id(0) == mid[0]
            for ax in range(1, len(grid)):
                at_mid = at_mid & (pl.program_id(ax) == mid[ax])

            @pl.when(at_mid)
            def _():
                side.middle(side_in, side_out, sems)

        body(*base_in, *base_out, *base_scr)

        @pl.when(last)
        def _():
            (side.rest if staged else side.finish)(side_in, side_out, sems)

    any_spec = pl.BlockSpec(memory_space=pl.ANY)
    res = pl.pallas_call(
        full, name=name, grid=grid, in_specs=list(in_specs) + [any_spec] * sin,
        out_specs=list(out_specs) + [any_spec] * sout, out_shape=list(out_shape) + list(side.outs),
        scratch_shapes=list(scratch) + list(side.sems),
        input_output_aliases={nin + i: nout + o for i, o in side.alias.items()},
        compiler_params=_params(sem))(*args, *side.ins)
    return res[:nout], res[nout:]


def _ffn_up(n, wg, wu, name, side=None, tiles=SWIGLU_TILES):
    tp, d = n.shape
    fp = wg.shape[1]
    tm = tp // tiles

    def body(n_ref, wg_ref, wu_ref, a_ref, b_ref, s_ref):
        nn = n_ref[...]
        for c0 in range(0, fp, MXU_COLS):
            cs = slice(c0, min(c0 + MXU_COLS, fp))
            a = _dot_nt(nn, wg_ref[cs, :])
            b = _dot_nt(nn, wu_ref[cs, :])
            a_ref[:, cs] = a.astype(BF)
            b_ref[:, cs] = b.astype(BF)
            s_ref[:, cs] = (a * _sigmoid(a) * b).astype(BF)

    out = jax.ShapeDtypeStruct((tp, N_CHIP * fp), BF)
    wspec = pl.BlockSpec((None, fp, d), lambda k, i: (k, 0, 0))
    ospec = pl.BlockSpec((tm, fp), lambda k, i: (i, k))
    return _grid_call(body, name, (N_CHIP, tiles), [pl.BlockSpec((tm, d), lambda k, i: (i, 0)), wspec, wspec],
                      [ospec, ospec, ospec], [out, out, out], (n, wg, wu), side)


def _ffn_up_head(n, wg, wu, name, side):
    tp, d = n.shape
    fp = wg.shape[1]
    tiles = SWIGLU_TILES
    tm = tp // tiles
    gat = _gather_side([wg, wu], relative=True, two_path=True)
    sin, sout, ngs = len(side.ins), len(side.outs), len(gat.sems)
    order = (0,) + REL_SLOT
    staged = hasattr(side, "middle")

    def body(*refs):
        n_ref = refs[0]
        si = refs[3:3 + sin]
        a_ref, b_ref, s_ref = refs[3 + sin:6 + sin]
        go = refs[6 + sin:8 + sin]
        so = refs[8 + sin:8 + sin + sout]
        wbg, wbu, wsem = refs[8 + sin + sout:11 + sin + sout]
        gsems = refs[11 + sin + sout:11 + sin + sout + ngs]
        ssems = refs[11 + sin + sout + ngs:]
        k, i = pl.program_id(0), pl.program_id(1)
        cur = k % 2

        def to_vmem(slot, buf):
            return [pltpu.make_async_copy(go[0].at[slot], wbg.at[buf], wsem.at[buf, 0]),
                    pltpu.make_async_copy(go[1].at[slot], wbu.at[buf], wsem.at[buf, 1])]

        @pl.when((k == 0) & (i == 0))
        def _():
            gat.send(go, gsems)
            if not staged:
                side.start(si, so, ssems)
            for cp in to_vmem(0, 0):
                cp.start()
            for cp in to_vmem(0, 0):
                cp.wait()

        for j in range(3):
            @pl.when((k == j) & (i == tiles // 2))
            def _():
                gat.arrived(j, go, gsems)
                if staged and j == 1:
                    side.start(si, so, ssems)

            @pl.when((k == j) & (i == tiles - 2))
            def _():
                gat.forwarded(j, go, gsems)
                for cp in to_vmem(order[j + 1], (j + 1) % 2):
                    cp.start()

            @pl.when((k == j + 1) & (i == 0))
            def _():
                for cp in to_vmem(order[j + 1], (j + 1) % 2):
                    cp.wait()

        nn = n_ref[...]
        for c0 in range(0, fp, MXU_COLS):
            cs = pl.ds(c0, min(MXU_COLS, fp - c0))
            a = _dot_nt(nn, wbg[cur, cs, :])
            b = _dot_nt(nn, wbu[cur, cs, :])
            a_ref[:, cs] = a.astype(BF)
            b_ref[:, cs] = b.astype(BF)
            s_ref[:, cs] = (a * _sigmoid(a) * b).astype(BF)

        if staged:
            @pl.when((k == N_CHIP - 1) & (i == tiles // 4))
            def _():
                side.middle(si, so, ssems)

        @pl.when((k == N_CHIP - 1) & (i == tiles - 1))
        def _():
            gat.drain(go, gsems)
            if staged:
                side.rest(si, so, ssems)
            else:
                side.finish(si, so, ssems)

    out = jax.ShapeDtypeStruct((tp, N_CHIP * fp), BF)
    any_spec = pl.BlockSpec(memory_space=pl.ANY)
    slot_of = lambda k: (k % 2) * 2 + k // 2
    ospec = pl.BlockSpec((tm, fp), lambda k, i: (i, slot_of(k)))
    wbuf = pltpu.VMEM((2, fp, d), BF)
    res = pl.pallas_call(
        body, name=name, grid=(N_CHIP, tiles),
        in_specs=[pl.BlockSpec((tm, d), lambda k, i: (i, 0))] + [any_spec] * (2 + sin),
        out_specs=[ospec, ospec, ospec] + [any_spec] * (2 + sout),
        out_shape=[out, out, out] + list(gat.outs) + list(side.outs),
        scratch_shapes=[wbuf, wbuf, pltpu.SemaphoreType.DMA((2, 2))] + list(gat.sems) + list(side.sems),
        input_output_aliases={1: 3, 2: 4, **{3 + a: 5 + b for a, b in side.alias.items()}},
        compiler_params=_params(("arbitrary", "arbitrary")))(n, wg, wu, *side.ins)
    return res[:3], res[3:5], res[5:]


def _ffn_bwd_act(df, wd, a, b, name, side=None, tiles=SWIGLU_TILES):
    tp, d = df.shape
    fp = wd.shape[1]
    tm = tp // tiles

    def body(df_ref, wd_ref, a_ref, b_ref, da_ref, db_ref):
        dfv = df_ref[...]
        for c0 in range(0, fp, MXU_COLS):
            cs = slice(c0, min(c0 + MXU_COLS, fp))
            ds = _dot_nt(dfv, wd_ref[cs, :])
            av = a_ref[:, cs].astype(F32)
            bv = b_ref[:, cs].astype(F32)
            sg = _sigmoid(av)
            da_ref[:, cs] = (ds * bv * sg * (1.0 + av * (1.0 - sg))).astype(BF)
            db_ref[:, cs] = (ds * av * sg).astype(BF)

    out = jax.ShapeDtypeStruct((tp, N_CHIP * fp), BF)
    aspec = pl.BlockSpec((tm, fp), lambda k, i: (i, k))
    return _grid_call(
        body, name, (N_CHIP, tiles),
        [pl.BlockSpec((tm, d), lambda k, i: (i, 0)), pl.BlockSpec((None, fp, d), lambda k, i: (k, 0, 0)), aspec, aspec],
        [aspec, aspec], [out, out], (df, wd, a, b), side)


def _col_matmul(lhs, w, name, trans_b, out_dtype, side=None, tiles=MM_TILES_BIG):
    tp, kd = lhs.shape
    nk = w.shape[0]
    nc = w.shape[1] if trans_b else w.shape[2]
    tm = tp // tiles

    def body(l_ref, w_ref, o_ref):
        if trans_b:
            o_ref[...] = _dot_nt(l_ref[...], w_ref[...]).astype(out_dtype)
        else:
            o_ref[...] = _dot(l_ref[...], w_ref[...]).astype(out_dtype)

    res, extra = _grid_call(
        body, name, (nk, tiles),
        [pl.BlockSpec((tm, kd), lambda k, i: (i, 0)),
         pl.BlockSpec((None,) + tuple(w.shape[1:]), lambda k, i: (k, 0, 0), pipeline_mode=pl.Buffered(1))],
        [pl.BlockSpec((tm, nc), lambda k, i: (i, k))], [jax.ShapeDtypeStruct((tp, nk * nc), out_dtype)], (lhs, w), side)
    return res[0], extra


def _row_matmul(pairs, name, trans_b, d_out, side=None, tiles=MM_TILES_BIG):
    l0 = pairs[0][0]
    tp = l0.shape[1] if l0.ndim == 3 else l0.shape[0]
    nk = pairs[0][1].shape[0]
    tm = tp // tiles
    npair = len(pairs)

    def body(*refs):
        o_ref = refs[2 * npair]
        k = pl.program_id(1)
        part = None
        for q in range(npair):
            l = refs[2 * q][...]
            w = refs[2 * q + 1][...]
            t = _dot_nt(l, w) if trans_b else _dot(l, w)
            part = t if part is None else part + t

        @pl.when(k == 0)
        def _():
            o_ref[...] = part

        @pl.when(k > 0)
        def _():
            o_ref[...] += part

    in_specs, args = [], []
    for lhs, w in pairs:
        if lhs.ndim == 3:
            in_specs.append(pl.BlockSpec((None, tm, lhs.shape[2]), lambda i, k: (k, i, 0)))
        else:
            in_specs.append(pl.BlockSpec((tm, lhs.shape[1] // nk), lambda i, k: (i, k)))
        in_specs.append(pl.BlockSpec((None,) + tuple(w.shape[1:]), lambda i, k: (k, 0, 0)))
        args += [lhs, w]
    res, extra = _grid_call(body, name, (tiles, nk), in_specs, [pl.BlockSpec((tm, d_out), lambda i, k: (i, 0))],
                            [jax.ShapeDtypeStruct((tp, d_out), F32)], args, side)
    return res[0], extra


def _wide_matmul(pairs, name, tn, side=None):
    tp = pairs[0][0].shape[0]
    d_out = pairs[0][1].shape[2]
    tm = tp // MM_TILES_BIG
    npair = len(pairs)

    def body(*refs):
        acc = None
        for q in range(npair):
            t = _dot(refs[2 * q][...], refs[2 * q + 1][...])
            acc = t if acc is None else acc + t
        refs[2 * npair][...] = acc

    in_specs, args = [], []
    for lhs, w in pairs:
        kdim = lhs.shape[1]
        in_specs += [pl.BlockSpec((tm, kdim), lambda n, i: (i, 0)), pl.BlockSpec((kdim, tn), lambda n, i: (0, n))]
        args += [lhs, w.reshape(kdim, d_out)]
    res, extra = _grid_call(body, name, (d_out // tn, MM_TILES_BIG), in_specs, [pl.BlockSpec((tm, tn), lambda n, i: (i, n))],
                            [jax.ShapeDtypeStruct((tp, d_out), F32)], args, side)
    return res[0], extra


def _wgrad_call(x, y, name, x_width=None, y_width=None, tile_x=None, tile_y=None, side=None):
    tp = x.shape[1] if x.ndim == 3 else x.shape[0]

    def spec(a, width, tile):
        cols = a.shape[2] if a.ndim == 3 else (a.shape[1] if width is None else width)
        tc = cols if tile is None else tile
        per = cols // tc
        if a.ndim == 3:
            return pl.BlockSpec((None, tp, tc), lambda k, t: (k, 0, t if tile else 0)), cols, per
        if width is None:
            return pl.BlockSpec((tp, tc), lambda k, t: (0, t if tile else 0)), cols, per
        return pl.BlockSpec((tp, tc), lambda k, t: (0, k * per + (t if tile else 0))), cols, per

    xs, p, nx = spec(x, x_width, tile_x)
    ys, q, ny = spec(y, y_width, tile_y)
    nt = nx * ny
    if tile_x:
        ospec = pl.BlockSpec((None, tile_x, q), lambda k, t: (k, t, 0))
    else:
        ospec = pl.BlockSpec((None, p, tile_y), lambda k, t: (k, 0, t))

    def body(x_ref, y_ref, o_ref):
        o_ref[...] = _dot_tn(x_ref[...], y_ref[...]).astype(BF)

    res, extra = _grid_call(body, name, (N_CHIP, nt), [xs, ys], [ospec], [jax.ShapeDtypeStruct((N_CHIP, p, q), BF)],
                            (x, y), side)
    return res[0], extra


def _row_call(body, name, tp, d, row_ins, vec_ins, row_out_dtypes, n_acc, side=None):
    te = tp // EW_TILES
    rspec = pl.BlockSpec((te, d), lambda i: (i, 0))
    vspec = pl.BlockSpec((1, d), lambda i: (0, 0))
    res, extra = _grid_call(
        body, name, (EW_TILES,), [rspec] * len(row_ins) + [vspec] * len(vec_ins),
        [rspec] * len(row_out_dtypes) + [vspec] * n_acc,
        [jax.ShapeDtypeStruct((tp, d), dt) for dt in row_out_dtypes] + [jax.ShapeDtypeStruct((1, d), F32)] * n_acc,
        (*row_ins, *vec_ins), side)
    return res if side is None else (res, extra)


def _norm0(h, g):
    tp, d = h.shape

    def body(h_ref, g_ref, n_ref):
        n_ref[...] = _rms(h_ref[...], g_ref[...]).astype(BF)

    return _row_call(body, "norm0", tp, d, [h], [g], [BF], 0)[0]


def _post_fwd(f, h, g_post, g_next, scale, name):
    tp, d = h.shape

    def body(f_ref, h_ref, gp_ref, gn_ref, hn_ref, n_ref):
        hn = h_ref[...] + scale * _rms(f_ref[...], gp_ref[...])
        hn_ref[...] = hn
        n_ref[...] = _rms(hn, gn_ref[...]).astype(BF)

    return _row_call(body, name, tp, d, [f, h], [g_post, g_next], [F32, BF], 0)


def _loss_bwd(f, h, tgt, g_post, t_real):
    tp, d = h.shape
    te = tp // EW_TILES

    def body(f_ref, h_ref, t_ref, gp_ref, dh_ref, df_ref, dg_ref, loss_ref):
        i = pl.program_id(0)

        @pl.when(i == 0)
        def _():
            dg_ref[...] = jnp.zeros_like(dg_ref)
            loss_ref[...] = jnp.zeros_like(loss_ref)

        f = f_ref[...]
        gp = gp_ref[...]
        h3 = h_ref[...] + 0.5 * _rms(f, gp)
        rows = i * te + lax.broadcasted_iota(jnp.int32, (te, 1), 0)
        real = (rows >= N_META) & (rows < t_real)
        e = jnp.where(real, h3 - t_ref[...], 0.0)
        loss_ref[...] += 0.5 * jnp.sum(jnp.sum(e * e, axis=1, keepdims=True), axis=0, keepdims=True) / d
        dh = e / d
        dh_ref[...] = dh
        dfv, dgr = _rms_bwd(f, gp, 0.5 * dh)
        df_ref[...] = dfv.astype(BF)
        dg_ref[...] += jnp.sum(dgr, axis=0, keepdims=True)

    rspec = pl.BlockSpec((te, d), lambda i: (i, 0))
    vspec = pl.BlockSpec((1, d), lambda i: (0, 0))
    return pl.pallas_call(
        body, name="loss_bwd", grid=(EW_TILES,),
        in_specs=[rspec, rspec, rspec, vspec],
        out_specs=[rspec, rspec, vspec, pl.BlockSpec((1, 1), lambda i: (0, 0))],
        out_shape=[jax.ShapeDtypeStruct((tp, d), F32), jax.ShapeDtypeStruct((tp, d), BF),
                   jax.ShapeDtypeStruct((1, d), F32), jax.ShapeDtypeStruct((1, 1), F32)],
        compiler_params=_params(("arbitrary",)),
    )(f, h, tgt, g_post)


def _pre_bwd(dn, h, dh_out, g_pre, name, chain=None, side=None):
    tp, d = h.shape

    def body(*refs):
        if chain is None:
            dn_ref, h_ref, dho_ref, g_ref, dh_ref, dg_ref = refs
        else:
            dn_ref, h_ref, dho_ref, p_ref, g_ref, gp_ref, dh_ref, dp_ref, dg_ref, dgp_ref = refs
        i = pl.program_id(0)

        @pl.when(i == 0)
        def _():
            dg_ref[...] = jnp.zeros_like(dg_ref)
            if chain is not None:
                dgp_ref[...] = jnp.zeros_like(dgp_ref)

        dx, dgr = _rms_bwd(h_ref[...], g_ref[...], dn_ref[...])
        dh = dho_ref[...] + dx
        dh_ref[...] = dh
        dg_ref[...] += jnp.sum(dgr, axis=0, keepdims=True)
        if chain is not None:
            dp, dgpr = _rms_bwd(p_ref[...], gp_ref[...], chain[2] * dh)
            dp_ref[...] = dp.astype(BF)
            dgp_ref[...] += jnp.sum(dgpr, axis=0, keepdims=True)

    if chain is None:
        return _row_call(body, name, tp, d, [dn, h, dh_out], [g_pre], [F32], 1, side)
    return _row_call(body, name, tp, d, [dn, h, dh_out, chain[0]], [g_pre, chain[1]], [F32, BF], 2, side)


def _gelu(y):
    c = math.sqrt(2.0 / math.pi)
    return 0.5 * y * (1.0 + jnp.tanh(c * (y + 0.044715 * y * y * y)))


def _gelu_and_grad(y):
    c = math.sqrt(2.0 / math.pi)
    y2 = y * y
    t = jnp.tanh(c * y * (1.0 + 0.044715 * y2))
    half = 0.5 * (1.0 + t)
    return y * half, half + 0.5 * y * (1.0 - t * t) * c * (1.0 + 3.0 * 0.044715 * y2)


def _neg_expm1(x):
    p = 1.0 + x * (1.0 / 9.0)
    for n in (8.0, 7.0, 6.0, 5.0, 4.0, 3.0, 2.0):
        p = 1.0 + x * (1.0 / n) * p
    return -jnp.where(x > -0.35, x * p, jnp.exp(x) - 1.0)


def _softplus(x):
    e = jnp.exp(-jnp.abs(x))
    w = 1.0 + e
    l1p = jnp.where(w == 1.0, e, jnp.log(w) * (e / jnp.where(w == 1.0, 1.0, w - 1.0)))
    return jnp.maximum(x, 0.0) + l1p


def _group_mean(v, gm):
    hi = v.astype(BF)
    lo = (v - hi.astype(F32)).astype(BF)
    return _dot(hi, gm) + _dot(lo, gm)


def _shift_dn(win, s, r):
    if s == 0:
        return win[8:8 + r]
    return pltpu.roll(win, s, 0)[8:8 + r]


def _shift_up(win, s, r):
    if s == 0:
        return win[0:r]
    return pltpu.roll(win, r + 8 - s, 0)[0:r]


def _window_dn(ref, t0, r, first):
    if first:
        return jnp.concatenate([jnp.zeros((8, ref.shape[1]), F32), ref[0:r, :]], axis=0)
    return ref[pl.ds(t0 - 8, r + 8), :]


def _tile_scan(a, u, reverse):
    r = a.shape[0]
    rid = lax.broadcasted_iota(jnp.int32, a.shape, 0) & 7
    for dlt in (1, 2, 4):
        sh = (r - dlt) if reverse else dlt
        a_s = pltpu.roll(a, sh, 0)
        u_s = pltpu.roll(u, sh, 0)
        keep = (rid + dlt <= 7) if reverse else (rid >= dlt)
        u = jnp.where(keep, u + a * u_s, u)
        a = jnp.where(keep, a * a_s, a)
    return a, u


def _lru_gates(xc, wa, ba, wx, bx, sp):
    xb = xc.astype(BF)
    ga = _sigmoid(_dot(xb, wa) + ba)
    gx = _sigmoid(_dot(xb, wx) + bx)
    la = -LRU_C * ga * sp
    return ga, gx, la


def _conv4(win, w4, cb, r):
    return (cb + w4[3:4] * _shift_dn(win, 0, r) + w4[2:3] * _shift_dn(win, 1, r)
            + w4[1:2] * _shift_dn(win, 2, r) + w4[0:1] * _shift_dn(win, 3, r))


def _lru_fwd(z, w4, cb, wa2, ba, wx2, bx, lam, g_out, gm, side=None):
    tp = z.shape[0]
    dl = cb.shape[1]
    nb = dl // LANE
    r = tp // MIX_CHUNKS
    c = LANE

    def body(y_ref, x_ref, w4_ref, cb_ref, wa_ref, ba_ref, wx_ref, bx_ref, lam_ref, go_ref, gm_ref, m_ref, hs_ref):
        w4v = w4_ref[...]
        cbv = cb_ref[...]
        wa = wa_ref[...]
        wx = wx_ref[...]
        bav = ba_ref[...]
        bxv = bx_ref[...]
        gov = go_ref[...]
        gmv = gm_ref[...]
        sp = _softplus(-lam_ref[...])

        def chunk(t0, hprev, first):
            win = _window_dn(x_ref, t0, r, first)
            xc = _conv4(win, w4v, cbv, r)
            ga, gx, la = _lru_gates(xc, wa, bav, wx, bxv, sp)
            a = jnp.exp(la)
            u = jnp.sqrt(_neg_expm1(2.0 * la)) * gx * xc
            ac, uc = _tile_scan(a, u, False)
            for j in range(r // 8):
                hj = uc[8 * j:8 * j + 8] + ac[8 * j:8 * j + 8] * hprev
                hs_ref[pl.ds(t0 + 8 * j, 8), :] = hj
                hprev = jnp.broadcast_to(hj[7:8], (8, c))
            h = hs_ref[pl.ds(t0, r), :]
            lo = h * _gelu(y_ref[pl.ds(t0, r), :])
            rs = lax.rsqrt(_group_mean(lo * lo, gmv) + EPS)
            m_ref[pl.ds(t0, r), :] = (lo * rs * gov).astype(BF)
            return hprev

        hp = chunk(0, jnp.zeros((8, c), F32), True)

        def loop(ci, hp):
            return chunk(pl.multiple_of(ci * r, 16), hp, False)

        lax.fori_loop(1, MIX_CHUNKS, loop, hp)

    col = lambda off: pl.BlockSpec((tp, c), lambda j: (0, off + j))
    vec = pl.BlockSpec((1, c), lambda j: (0, j))
    return _grid_call(
        body, "lru_fwd", (nb,),
        [col(0), col(nb), pl.BlockSpec((8, c), lambda j: (0, j)), vec, pl.BlockSpec((None, c, c), lambda j: (j, 0, 0)),
         vec, pl.BlockSpec((None, c, c), lambda j: (j, 0, 0)), vec, vec, vec, pl.BlockSpec((c, c), lambda j: (0, 0))],
        [col(0), col(0)], [jax.ShapeDtypeStruct((tp, dl), BF), jax.ShapeDtypeStruct((tp, dl), F32)],
        (z, z, w4, cb, wa2, ba, wx2, bx, lam, g_out, gm), side)


def _lru_bwd(z, hs, dmix, w4, cb, wa2, ba, wx2, bx, lam, g_out, gm, side=None):
    tp = z.shape[0]
    dl = cb.shape[1]
    nb = dl // LANE
    r = tp // MIX_CHUNKS
    c = LANE

    def body(y_ref, x_ref, hs_ref, dm_ref, w4_ref, cb_ref, wa_ref, ba_ref, wx_ref, bx_ref, lam_ref, go_ref, gm_ref,
             dy_ref, dx_ref, small_ref, dwa_ref, dwx_ref, xc_buf, ga_buf, gx_buf, a_buf, dh_buf, dxc_buf):
        w4v = w4_ref[...]
        cbv = cb_ref[...]
        wa = wa_ref[...]
        wx = wx_ref[...]
        bav = ba_ref[...]
        bxv = bx_ref[...]
        gov = go_ref[...]
        gmv = gm_ref[...]
        lamv = lam_ref[...]
        sp = _softplus(-lamv)
        small_ref[...] = jnp.zeros_like(small_ref)
        dwa_ref[...] = jnp.zeros_like(dwa_ref)
        dwx_ref[...] = jnp.zeros_like(dwx_ref)
        a_buf[pl.ds(tp, 8), :] = jnp.zeros((8, c), F32)
        dxc_buf[pl.ds(tp, 8), :] = jnp.zeros((8, c), F32)

        def fwd_chunk(t0, first):
            win = _window_dn(x_ref, t0, r, first)
            xc = _conv4(win, w4v, cbv, r)
            ga, gx, la = _lru_gates(xc, wa, bav, wx, bxv, sp)
            xc_buf[pl.ds(t0, r), :] = xc
            ga_buf[pl.ds(t0, r), :] = ga
            gx_buf[pl.ds(t0, r), :] = gx
            a_buf[pl.ds(t0, r), :] = jnp.exp(la)
            h = hs_ref[pl.ds(t0, r), :]
            yv = y_ref[pl.ds(t0, r), :]
            ge, dge = _gelu_and_grad(yv)
            lo = h * ge
            rs = lax.rsqrt(_group_mean(lo * lo, gmv) + EPS)
            xh = lo * rs
            dm = dm_ref[pl.ds(t0, r), :]
            q = dm * gov
            dlo = rs * (q - xh * _group_mean(q * xh, gmv))
            small_ref[8:9, :] += jnp.sum(dm * xh, axis=0, keepdims=True)
            dh_buf[pl.ds(t0, r), :] = dlo * ge
            dy_ref[pl.ds(t0, r), :] = (dlo * h * dge).astype(BF)

        fwd_chunk(0, True)

        def floop(ci, carry):
            fwd_chunk(pl.multiple_of(ci * r, 16), False)
            return carry

        lax.fori_loop(1, MIX_CHUNKS, floop, 0)

        def bwd_chunk(t0, vnext, first):
            ap = _shift_up(a_buf[pl.ds(t0, r + 8), :], 1, r)
            ac, uc = _tile_scan(ap, dh_buf[pl.ds(t0, r), :], True)
            for j in reversed(range(r // 8)):
                vj = uc[8 * j:8 * j + 8] + ac[8 * j:8 * j + 8] * vnext
                dh_buf[pl.ds(t0 + 8 * j, 8), :] = vj
                vnext = jnp.broadcast_to(vj[0:1], (8, c))
            v = dh_buf[pl.ds(t0, r), :]
            hprev = _shift_dn(_window_dn(hs_ref, t0, r, first), 1, r)
            xc = xc_buf[pl.ds(t0, r), :]
            ga = ga_buf[pl.ds(t0, r), :]
            gx = gx_buf[pl.ds(t0, r), :]
            a = a_buf[pl.ds(t0, r), :]
            em = _neg_expm1(-2.0 * LRU_C * ga * sp)
            mult = jnp.sqrt(em)
            dla = v * hprev * a - (v * gx * xc) * ((1.0 - em) / mult)
            dgx = v * mult * xc
            dxc = v * mult * gx
            dga = dla * (-LRU_C) * sp
            small_ref[7:8, :] += jnp.sum(dla * (-LRU_C) * ga, axis=0, keepdims=True)
            dpa = dga * ga * (1.0 - ga)
            dpx = dgx * gx * (1.0 - gx)
            small_ref[5:6, :] += jnp.sum(dpa, axis=0, keepdims=True)
            small_ref[6:7, :] += jnp.sum(dpx, axis=0, keepdims=True)
            dpab = dpa.astype(BF)
            dpxb = dpx.astype(BF)
            xb = xc.astype(BF)
            dxc = dxc + _dot_nt(dpab, wa) + _dot_nt(dpxb, wx)
            dwa_ref[...] += _dot_tn(xb, dpab)
            dwx_ref[...] += _dot_tn(xb, dpxb)
            dxc_buf[pl.ds(t0, r), :] = dxc
            small_ref[4:5, :] += jnp.sum(dxc, axis=0, keepdims=True)
            dwin = dxc_buf[pl.ds(t0, r + 8), :]
            dx_ref[pl.ds(t0, r), :] = (w4v[3:4] * dxc + w4v[2:3] * _shift_up(dwin, 1, r)
                                       + w4v[1:2] * _shift_up(dwin, 2, r) + w4v[0:1] * _shift_up(dwin, 3, r)).astype(BF)
            xwin = _window_dn(x_ref, t0, r, first)
            for k in range(4):
                small_ref[k:k + 1, :] += jnp.sum(dxc * _shift_dn(xwin, 3 - k, r), axis=0, keepdims=True)
            return vnext

        def bloop(it, vnext):
            ci = MIX_CHUNKS - 1 - it
            return bwd_chunk(pl.multiple_of(ci * r, 16), vnext, False)

        vn = lax.fori_loop(0, MIX_CHUNKS - 1, bloop, jnp.zeros((8, c), F32))
        bwd_chunk(0, vn, True)
        small_ref[7:8, :] = small_ref[7:8, :] * (-_sigmoid(-lamv))

    col = lambda off: pl.BlockSpec((tp, c), lambda j: (0, off + j))
    vec = pl.BlockSpec((1, c), lambda j: (0, j))
    mat = pl.BlockSpec((None, c, c), lambda j: (j, 0, 0))
    buf = pltpu.VMEM((tp, c), F32)
    bufp = pltpu.VMEM((tp + 8, c), F32)
    return _grid_call(
        body, "lru_bwd", (nb,),
        [col(0), col(nb), col(0), col(0), pl.BlockSpec((8, c), lambda j: (0, j)), vec, mat, vec, mat, vec, vec, vec,
         pl.BlockSpec((c, c), lambda j: (0, 0))],
        [col(0), col(0), pl.BlockSpec((16, c), lambda j: (0, j)), mat, mat],
        [jax.ShapeDtypeStruct((tp, dl), BF), jax.ShapeDtypeStruct((tp, dl), BF), jax.ShapeDtypeStruct((16, dl), F32),
         jax.ShapeDtypeStruct((nb, c, c), F32), jax.ShapeDtypeStruct((nb, c, c), F32)],
        (z, z, hs, dmix, w4, cb, wa2, ba, wx2, bx, lam, g_out, gm), side, [buf, buf, buf, bufp, buf, bufp])


def _sc_conv(cvwin, w3, r):
    return w3[2:3] * _shift_dn(cvwin, 0, r) + w3[1:2] * _shift_dn(cvwin, 1, r) + w3[0:1] * _shift_dn(cvwin, 2, r)


def _sc_fwd(z, w3, g_out, gm, dl, side=None):
    tp = z.shape[0]
    nb = dl // LANE
    r = tp // MIX_CHUNKS
    c = LANE

    def body(b_ref, c_ref, v_ref, w3_ref, go_ref, gm_ref, m_ref):
        w3v = w3_ref[...]
        gov = go_ref[...]
        gmv = gm_ref[...]

        def chunk(t0, first):
            cvwin = _window_dn(c_ref, t0, r, first) * _window_dn(v_ref, t0, r, first)
            so = b_ref[pl.ds(t0, r), :] * _sc_conv(cvwin, w3v, r)
            rs = lax.rsqrt(_group_mean(so * so, gmv) + EPS)
            m_ref[pl.ds(t0, r), :] = (so * rs * gov).astype(BF)

        chunk(0, True)

        def loop(ci, carry):
            chunk(pl.multiple_of(ci * r, 16), False)
            return carry

        lax.fori_loop(1, MIX_CHUNKS, loop, 0)

    col = lambda off: pl.BlockSpec((tp, c), lambda j: (0, off + j))
    res, extra = _grid_call(
        body, "sconv_fwd", (nb,),
        [col(2 * nb), col(3 * nb), col(4 * nb), pl.BlockSpec((8, c), lambda j: (0, j)),
         pl.BlockSpec((1, c), lambda j: (0, j)), pl.BlockSpec((c, c), lambda j: (0, 0))],
        [col(0)], [jax.ShapeDtypeStruct((tp, dl), BF)], (z, z, z, w3, g_out, gm), side)
    return res[0], extra


def _sc_bwd(z, dmix, w3, g_out, gm, dl, side=None):
    tp = z.shape[0]
    nb = dl // LANE
    r = tp // MIX_CHUNKS
    c = LANE

    def body(b_ref, c_ref, v_ref, dm_ref, w3_ref, go_ref, gm_ref, db_ref, dc_ref, dv_ref, small_ref, dsc_buf):
        w3v = w3_ref[...]
        gov = go_ref[...]
        gmv = gm_ref[...]
        small_ref[...] = jnp.zeros_like(small_ref)
        dsc_buf[pl.ds(tp, 8), :] = jnp.zeros((8, c), F32)

        def chunk1(t0, first):
            cvwin = _window_dn(c_ref, t0, r, first) * _window_dn(v_ref, t0, r, first)
            sc = _sc_conv(cvwin, w3v, r)
            bv = b_ref[pl.ds(t0, r), :]
            so = bv * sc
            rs = lax.rsqrt(_group_mean(so * so, gmv) + EPS)
            xh = so * rs
            dm = dm_ref[pl.ds(t0, r), :]
            q = dm * gov
            dso = rs * (q - xh * _group_mean(q * xh, gmv))
            small_ref[3:4, :] += jnp.sum(dm * xh, axis=0, keepdims=True)
            db_ref[pl.ds(t0, r), :] = (dso * sc).astype(BF)
            dsc = dso * bv
            dsc_buf[pl.ds(t0, r), :] = dsc
            for k in range(3):
                small_ref[k:k + 1, :] += jnp.sum(dsc * _shift_dn(cvwin, 2 - k, r), axis=0, keepdims=True)

        chunk1(0, True)

        def loop1(ci, carry):
            chunk1(pl.multiple_of(ci * r, 16), False)
            return carry

        lax.fori_loop(1, MIX_CHUNKS, loop1, 0)

        def loop2(ci, carry):
            t0 = pl.multiple_of(ci * r, 16)
            dwin = dsc_buf[pl.ds(t0, r + 8), :]
            dcv = w3v[2:3] * _shift_up(dwin, 0, r) + w3v[1:2] * _shift_up(dwin, 1, r) + w3v[0:1] * _shift_up(dwin, 2, r)
            dc_ref[pl.ds(t0, r), :] = (dcv * v_ref[pl.ds(t0, r), :]).astype(BF)
            dv_ref[pl.ds(t0, r), :] = (dcv * c_ref[pl.ds(t0, r), :]).astype(BF)
            return carry

        lax.fori_loop(0, MIX_CHUNKS, loop2, 0)

    col = lambda off: pl.BlockSpec((tp, c), lambda j: (0, off + j))
    out = jax.ShapeDtypeStruct((tp, dl), BF)
    return _grid_call(
        body, "sconv_bwd", (nb,),
        [col(2 * nb), col(3 * nb), col(4 * nb), col(nb), pl.BlockSpec((8, c), lambda j: (0, j)),
         pl.BlockSpec((1, c), lambda j: (0, j)), pl.BlockSpec((c, c), lambda j: (0, 0))],
        [col(0), col(0), col(0), pl.BlockSpec((8, c), lambda j: (0, j))],
        [out, out, out, jax.ShapeDtypeStruct((8, dl), F32)], (z, z, z, dmix, w3, g_out, gm), side,
        [pltpu.VMEM((tp + 8, c), F32)])


def _cast_pad(w, rows_p, cols_p, chip, name):
    r, c = w.shape

    def body(chip_ref, w_ref, o_ref):
        if (rows_p, cols_p) != (r, c):
            o_ref[...] = jnp.zeros_like(o_ref)
        o_ref[0:r, 0:c] = w_ref[...].astype(BF)

    return pl.pallas_call(
        body, name=name, out_shape=jax.ShapeDtypeStruct((N_CHIP, rows_p, cols_p), BF),
        grid_spec=pltpu.PrefetchScalarGridSpec(
            num_scalar_prefetch=1, grid=(1,),
            in_specs=[pl.BlockSpec((r, c), lambda i, chip: (0, 0))],
            out_specs=pl.BlockSpec((None, rows_p, cols_p), lambda i, chip: (chip[0], 0, 0))),
        compiler_params=_params(("arbitrary",)),
    )(chip, w)


def _adamw_math(w, g, m, v):
    m2 = ADAM_B1 * m + (1.0 - ADAM_B1) * g
    v2 = ADAM_B2 * v + (1.0 - ADAM_B2) * (g * g)
    m_hat = m2 / (1.0 - ADAM_B1 ** ADAM_STEP)
    v_hat = v2 / (1.0 - ADAM_B2 ** ADAM_STEP)
    delta = -ADAM_LR * (m_hat / (jnp.sqrt(v_hat) + ADAM_EPS) + ADAM_WD * w)
    return delta, m2, v2


def _adamw(w, g, m, v, name, row_tiles, col_tiles, side=None):
    r, c = w.shape
    tr = r // row_tiles
    tc = c // col_tiles
    gc = g.shape[1] if col_tiles == 1 else tc

    def body(w_ref, g_ref, m_ref, v_ref, go_ref, d_ref, mo_ref, vo_ref):
        gv = g_ref[...][:, 0:tc]
        delta, m2, v2 = _adamw_math(w_ref[...], gv, m_ref[...], v_ref[...])
        go_ref[...] = gv
        d_ref[...] = delta
        mo_ref[...] = m2
        vo_ref[...] = v2

    spec = pl.BlockSpec((tr, tc), lambda i, j: (i, j))
    out = jax.ShapeDtypeStruct((r, c), F32)
    return _grid_call(body, name, (row_tiles, col_tiles), [spec, pl.BlockSpec((tr, gc), lambda i, j: (i, j)), spec, spec],
                      [spec] * 4, [out] * 4, (w, g, m, v), side)


def _adamw_small(w, g_top, g4, m, v):
    def body(w_ref, gt_ref, g_ref, m_ref, v_ref, go_ref, d_ref, mo_ref, vo_ref):
        g = jnp.concatenate([gt_ref[...], (g_ref[0] + g_ref[1]) + (g_ref[2] + g_ref[3])], axis=0)
        delta, m2, v2 = _adamw_math(w_ref[...], g, m_ref[...], v_ref[...])
        go_ref[...] = g
        d_ref[...] = delta
        mo_ref[...] = m2
        vo_ref[...] = v2

    out = jax.ShapeDtypeStruct(w.shape, F32)
    spec = pl.BlockSpec(w.shape, lambda: (0, 0))
    return pl.pallas_call(
        body, name="adamw_small",
        in_specs=[spec, pl.BlockSpec(g_top.shape, lambda: (0, 0)), pl.BlockSpec(g4.shape, lambda: (0, 0, 0)), spec, spec],
        out_specs=[spec] * 4, out_shape=[out] * 4, compiler_params=_params())(w, g_top, g4, m, v)


def _place():
    x, y, c = lax.axis_index("x"), lax.axis_index("y"), lax.axis_index("c")
    chips = [(1 - x, y), (x, 1 - y), (1 - x, 1 - y)]
    return x, y, c, chips


ANY = pl.BlockSpec(memory_space=pl.ANY)


REL_SLOT = (2, 1, 3)


def _gather_side(bufs, relative=False, two_path=False):
    n = len(bufs)
    direct = (0, 1) if two_path else (0, 1, 2)

    def copies(outs, sems):
        s_ici, r_ici, s_d2d, r_d2d = sems[:4]
        x, y, c, chips = _place()
        me = 2 * x + y

        def rows(w, slot, core, part=None):
            half = bufs[w].shape[1] // 2
            if part is None:
                return outs[w].at[slot, pl.ds(core * half, half)]
            return outs[w].at[slot, pl.ds(core * half + part * (half // 2), half // 2)]

        def theirs(j):
            return REL_SLOT[j] if relative else 2 * chips[j][0] + chips[j][1]

        def ici_send(w, j):
            px, py = chips[j]
            return pltpu.make_async_remote_copy(
                src_ref=rows(w, 0 if relative else me, c), dst_ref=rows(w, REL_SLOT[j] if relative else me, c),
                send_sem=s_ici.at[w, j], recv_sem=r_ici.at[w, j], device_id=(px, py, c), device_id_type=MESH)

        def ici_recv(w, j):
            px, py = chips[j]
            return pltpu.make_async_remote_copy(
                src_ref=rows(w, theirs(j), c), dst_ref=rows(w, theirs(j), c),
                send_sem=s_ici.at[w, j], recv_sem=r_ici.at[w, j], device_id=(px, py, c), device_id_type=MESH)

        def hop_send(w, p):
            px, py = chips[1 - p]
            return pltpu.make_async_remote_copy(
                src_ref=rows(w, theirs(p), c, p), dst_ref=rows(w, REL_SLOT[2] if relative else theirs(p), c, p),
                send_sem=sems[4].at[w, p], recv_sem=sems[5].at[w, p], device_id=(px, py, c), device_id_type=MESH)

        def hop_recv(w, p):
            px, py = chips[1 - p]
            return pltpu.make_async_remote_copy(
                src_ref=rows(w, theirs(2), c, p), dst_ref=rows(w, theirs(2), c, p),
                send_sem=sems[4].at[w, p], recv_sem=sems[5].at[w, p], device_id=(px, py, c), device_id_type=MESH)

        def d2d(w, j, core):
            return pltpu.make_async_remote_copy(
                src_ref=rows(w, theirs(j), core), dst_ref=rows(w, theirs(j), core),
                send_sem=s_d2d.at[w, j], recv_sem=r_d2d.at[w, j], device_id=(x, y, 1 - c), device_id_type=MESH)

        return c, ici_send, ici_recv, hop_send, hop_recv, d2d

    def send(outs, sems):
        c, ici_send, ici_recv, hop_send, hop_recv, d2d = copies(outs, sems)
        for j in direct:
            for w in range(n):
                ici_send(w, j).start()

    def arrived(j, outs, sems):
        c, ici_send, ici_recv, hop_send, hop_recv, d2d = copies(outs, sems)
        for w in range(n):
            if j in direct:
                ici_recv(w, j).wait_recv()
                if two_path:
                    hop_send(w, j).start()
            else:
                hop_recv(w, 0).wait_recv()
                hop_recv(w, 1).wait_recv()
            d2d(w, j, c).start()

    def forwarded(j, outs, sems):
        c, ici_send, ici_recv, hop_send, hop_recv, d2d = copies(outs, sems)
        for w in range(n):
            d2d(w, j, 1 - c).wait_recv()

    def drain(outs, sems):
        c, ici_send, ici_recv, hop_send, hop_recv, d2d = copies(outs, sems)
        for w in range(n):
            for j in direct:
                ici_send(w, j).wait_send()
                if two_path:
                    hop_send(w, j).wait_send()
            for j in range(3):
                d2d(w, j, c).wait_send()

    def start(ins, outs, sems):
        send(outs, sems)

    def middle(ins, outs, sems):
        arrived(0, outs, sems)
        arrived(1, outs, sems)

    def rest(ins, outs, sems):
        arrived(2, outs, sems)
        for j in range(3):
            forwarded(j, outs, sems)
        drain(outs, sems)

    def finish(ins, outs, sems):
        middle(ins, outs, sems)
        rest(ins, outs, sems)

    dma = pltpu.SemaphoreType.DMA((n, 3))
    hop = [pltpu.SemaphoreType.DMA((n, 2))] * 2 if two_path else []
    side = _Side(list(bufs), [jax.ShapeDtypeStruct(b.shape, b.dtype) for b in bufs], {w: w for w in range(n)},
                 [dma, dma, dma, dma] + hop, start, finish)
    side.send, side.arrived, side.forwarded, side.drain = send, arrived, forwarded, drain
    side.middle, side.rest = middle, rest
    return side


def _run_side(side, name):
    sin, sout = len(side.ins), len(side.outs)

    def body(*refs):
        ins, outs, sems = refs[:sin], refs[sin:sin + sout], refs[sin + sout:]
        side.start(ins, outs, sems)
        side.finish(ins, outs, sems)

    return pl.pallas_call(
        body, name=name, out_shape=list(side.outs), in_specs=[ANY] * sin, out_specs=[ANY] * sout,
        scratch_shapes=list(side.sems), input_output_aliases=dict(side.alias))(*side.ins)


def _pair_exchange_side(grads):
    n = len(grads)

    def copies(ins, outs, sems):
        ssem, rsem = sems
        x, y, c, _ = _place()
        cps = []
        for w in range(n):
            half = grads[w].shape[1] // 2
            cps.append(pltpu.make_async_remote_copy(
                src_ref=ins[w].at[:, pl.ds((1 - c) * half, half)], dst_ref=outs[w],
                send_sem=ssem.at[w], recv_sem=rsem.at[w], device_id=(x, y, 1 - c), device_id_type=MESH))
        return cps

    def start(ins, outs, sems):
        for cp in copies(ins, outs, sems):
            cp.start()

    def finish(ins, outs, sems):
        for cp in copies(ins, outs, sems):
            cp.wait()

    dma = pltpu.SemaphoreType.DMA((n,))
    return _Side(list(grads), [jax.ShapeDtypeStruct((N_CHIP, g.shape[1] // 2, g.shape[2]), BF) for g in grads], {},
                 [dma, dma], start, finish)


def _sibling_copy_side(buf):
    def copy(ins, outs, sems):
        x, y, c, _ = _place()
        return pltpu.make_async_remote_copy(src_ref=ins[0], dst_ref=outs[0], send_sem=sems[0], recv_sem=sems[1],
                                            device_id=(x, y, 1 - c), device_id_type=MESH)

    return _Side([buf], [jax.ShapeDtypeStruct(buf.shape, buf.dtype)], {}, [pltpu.SemaphoreType.DMA, pltpu.SemaphoreType.DMA],
                 lambda i, o, s: copy(i, o, s).start(), lambda i, o, s: copy(i, o, s).wait())


def _slot_exchange_side(buf4):
    def copies(outs, sems, sending):
        ssem, rsem = sems
        x, y, c, chips = _place()
        me = 2 * x + y
        return [pltpu.make_async_remote_copy(
            src_ref=outs[0].at[me if sending else 2 * px + py], dst_ref=outs[0].at[me if sending else 2 * px + py],
            send_sem=ssem.at[j], recv_sem=rsem.at[j], device_id=(px, py, c), device_id_type=MESH)
            for j, (px, py) in enumerate(chips)]

    def start(ins, outs, sems):
        for cp in copies(outs, sems, True):
            cp.start()

    def finish(ins, outs, sems):
        for cp in copies(outs, sems, False):
            cp.wait_recv()
        for cp in copies(outs, sems, True):
            cp.wait_send()

    dma = pltpu.SemaphoreType.DMA((3,))
    return _Side([buf4], [jax.ShapeDtypeStruct(buf4.shape, buf4.dtype)], {0: 0}, [dma, dma], start, finish)


def _pair_sum(g, sib, core, name):
    _, r, cdim = g.shape
    half = r // 2

    def body(core_ref, g_ref, s_ref, o_ref):
        o_ref[...] = (g_ref[...].astype(F32) + s_ref[...].astype(F32)).astype(BF)

    return pl.pallas_call(
        body, name=name,
        grid_spec=pltpu.PrefetchScalarGridSpec(
            num_scalar_prefetch=1, grid=(N_CHIP,),
            in_specs=[pl.BlockSpec((None, half, cdim), lambda k, core: (k, core[0], 0)),
                      pl.BlockSpec((None, half, cdim), lambda k, core: (k, 0, 0))],
            out_specs=pl.BlockSpec((None, half, cdim), lambda k, core: (k, 0, 0))),
        out_shape=jax.ShapeDtypeStruct((N_CHIP, half, cdim), BF),
        compiler_params=_params(("arbitrary",)),
    )(core, g, sib)


def _chip_exchange_side(psums, relative=False):
    n = len(psums)

    def copies(ins, outs, sems):
        ssem, rsem = sems
        x, y, c, chips = _place()
        return [pltpu.make_async_remote_copy(
            src_ref=ins[w].at[REL_SLOT[j] if relative else 2 * px + py], dst_ref=outs[w].at[j],
            send_sem=ssem.at[w, j], recv_sem=rsem.at[w, j], device_id=(px, py, c), device_id_type=MESH)
            for w in range(n) for j, (px, py) in enumerate(chips)]

    def start(ins, outs, sems):
        for cp in copies(ins, outs, sems):
            cp.start()

    def finish(ins, outs, sems):
        for cp in copies(ins, outs, sems):
            cp.wait()

    dma = pltpu.SemaphoreType.DMA((n, 3))
    return _Side(list(psums), [jax.ShapeDtypeStruct((3,) + p.shape[1:], BF) for p in psums], {}, [dma, dma],
                 start, finish)


def _final_sum(g, sib, recv, sel, name):
    _, r, cdim = g.shape
    half = r // 2
    nt = 4
    th = half // nt

    def body(sel_ref, g_ref, s_ref, r_ref, o_ref):
        acc = g_ref[...].astype(F32) + s_ref[...].astype(F32)
        for j in range(3):
            acc = acc + r_ref[j].astype(F32)
        o_ref[...] = acc

    return pl.pallas_call(
        body, name=name,
        grid_spec=pltpu.PrefetchScalarGridSpec(
            num_scalar_prefetch=1, grid=(nt,),
            in_specs=[pl.BlockSpec((None, th, cdim), lambda i, sel: (sel[0], sel[1] * nt + i, 0)),
                      pl.BlockSpec((None, th, cdim), lambda i, sel: (sel[0], i, 0)),
                      pl.BlockSpec((3, th, cdim), lambda i, sel: (0, i, 0))],
            out_specs=pl.BlockSpec((th, cdim), lambda i, sel: (sel[1] * nt + i, 0))),
        out_shape=jax.ShapeDtypeStruct((r, cdim), F32),
        compiler_params=_params(("arbitrary",)),
    )(sel, g, sib, recv)


def _join_side(bufs):
    n = len(bufs)

    def copies(outs, sems, core_of):
        ssem, rsem = sems
        x, y, c, _ = _place()
        cps = []
        for w in range(n):
            half = bufs[w].shape[0] // 2
            rows = outs[w].at[pl.ds(core_of(c) * half, half)]
            cps.append(pltpu.make_async_remote_copy(
                src_ref=rows, dst_ref=rows, send_sem=ssem.at[w], recv_sem=rsem.at[w],
                device_id=(x, y, 1 - c), device_id_type=MESH))
        return cps

    def start(ins, outs, sems):
        for cp in copies(outs, sems, lambda c: c):
            cp.start()

    def finish(ins, outs, sems):
        for cp in copies(outs, sems, lambda c: 1 - c):
            cp.wait_recv()
        for cp in copies(outs, sems, lambda c: c):
            cp.wait_send()

    dma = pltpu.SemaphoreType.DMA((n,))
    return _Side(list(bufs), [jax.ShapeDtypeStruct(b.shape, F32) for b in bufs], {w: w for w in range(n)}, [dma, dma],
                 start, finish)


def _small_pair_sum(buf, sib, chip):
    rows, d = buf.shape

    def body(chip_ref, a_ref, b_ref, o_ref):
        o_ref[...] = a_ref[...] + b_ref[...]

    return pl.pallas_call(
        body, name="small_pair_sum", out_shape=jax.ShapeDtypeStruct((N_CHIP, rows, d), F32),
        grid_spec=pltpu.PrefetchScalarGridSpec(
            num_scalar_prefetch=1, grid=(1,),
            in_specs=[pl.BlockSpec((rows, d), lambda i, chip: (0, 0))] * 2,
            out_specs=pl.BlockSpec((None, rows, d), lambda i, chip: (chip[0], 0, 0))),
        compiler_params=_params(("arbitrary",)),
    )(chip, buf, sib)


def _small_all_reduce(buf, name):
    rows, d = buf.shape

    def body(in_ref, out_ref, sib, all4, ssem, rsem, psem, qsem):
        x, y, c, chips = _place()
        me = 2 * x + y
        to_sib = pltpu.make_async_remote_copy(src_ref=in_ref, dst_ref=sib, send_sem=ssem, recv_sem=rsem,
                                              device_id=(x, y, 1 - c), device_id_type=MESH)
        to_sib.start()
        to_sib.wait()
        all4[me] = in_ref[...] + sib[...]
        cps = [pltpu.make_async_remote_copy(src_ref=all4.at[me], dst_ref=all4.at[me], send_sem=psem.at[j],
                                            recv_sem=qsem.at[j], device_id=(px, py, c), device_id_type=MESH)
               for j, (px, py) in enumerate(chips)]
        for cp in cps:
            cp.start()
        for j, (px, py) in enumerate(chips):
            chip = 2 * px + py
            pltpu.make_async_remote_copy(src_ref=all4.at[chip], dst_ref=all4.at[chip], send_sem=psem.at[j],
                                         recv_sem=qsem.at[j], device_id=(px, py, c), device_id_type=MESH).wait_recv()
        for cp in cps:
            cp.wait_send()
        out_ref[...] = (all4[0] + all4[1]) + (all4[2] + all4[3])

    vm = pl.BlockSpec(memory_space=pltpu.VMEM)
    return pl.pallas_call(
        body, name=name, out_shape=jax.ShapeDtypeStruct((rows, d), F32),
        in_specs=[vm], out_specs=vm,
        scratch_shapes=[pltpu.VMEM((rows, d), F32), pltpu.VMEM((N_CHIP, rows, d), F32),
                        pltpu.SemaphoreType.DMA, pltpu.SemaphoreType.DMA,
                        pltpu.SemaphoreType.DMA((3,)), pltpu.SemaphoreType.DMA((3,))],
        compiler_params=_params(),
    )(buf)


def _pair_blocks(w):
    w4 = w.reshape(N_HEADS // 2, 2, HEAD, HEAD)
    eye = jnp.eye(2, dtype=w.dtype)
    return jnp.einsum("pirc,ij->pirjc", w4, eye).reshape(N_HEADS // 2, LANE, LANE)


def _unpair_blocks(w2):
    w5 = w2.reshape(N_HEADS // 2, 2, HEAD, 2, HEAD)
    return jnp.stack([w5[:, 0, :, 0, :], w5[:, 1, :, 1, :]], axis=1).reshape(N_HEADS, HEAD, HEAD)


def kernel(x, meta_tokens, ffn1_pre_g, ffn1_w_gate, ffn1_w_up, ffn1_w_down, ffn1_post_g, mix_pre_g, w_in, lru_conv_w, lru_conv_b, lru_w_a, lru_b_a, lru_w_x, lru_b_x, lru_lambda, sconv_w, lru_out_g, sconv_out_g, w_out, mix_post_g, ffn2_pre_g, ffn2_w_gate, ffn2_w_up, ffn2_w_down, ffn2_post_g, loss_target, m_meta_tokens, m_ffn1_pre_g, m_ffn1_w_gate, m_ffn1_w_up, m_ffn1_w_down, m_ffn1_post_g, m_mix_pre_g, m_w_in, m_lru_conv_w, m_lru_conv_b, m_lru_w_a, m_lru_b_a, m_lru_w_x, m_lru_b_x, m_lru_lambda, m_sconv_w, m_lru_out_g, m_sconv_out_g, m_w_out, m_mix_post_g, m_ffn2_pre_g, m_ffn2_w_gate, m_ffn2_w_up, m_ffn2_w_down, m_ffn2_post_g, v_meta_tokens, v_ffn1_pre_g, v_ffn1_w_gate, v_ffn1_w_up, v_ffn1_w_down, v_ffn1_post_g, v_mix_pre_g, v_w_in, v_lru_conv_w, v_lru_conv_b, v_lru_w_a, v_lru_b_a, v_lru_w_x, v_lru_b_x, v_lru_lambda, v_sconv_w, v_lru_out_g, v_sconv_out_g, v_w_out, v_mix_post_g, v_ffn2_pre_g, v_ffn2_w_gate, v_ffn2_w_up, v_ffn2_w_down, v_ffn2_post_g):
    seq, d = x.shape[1], x.shape[2]
    t_real = N_META + seq
    tp = _round_up(t_real, ROW_ALIGN)
    f4 = ffn1_w_gate.shape[2]
    f4p = _round_up(f4, LANE)
    dl = lru_conv_b.shape[1]
    cin = w_in.shape[2]
    xi, yi, ci = lax.axis_index("x"), lax.axis_index("y"), lax.axis_index("c")
    chip = 2 * xi + yi
    zero = jnp.zeros((), jnp.int32)

    transposed = ("ffn1_w_gate", "ffn1_w_up", "ffn2_w_gate", "ffn2_w_up")

    def view(k, a):
        return a[0].T if k in transposed else a[0]

    def unview(k, a):
        return (a.T if k in transposed else a)[None]

    big = {
        "ffn1_w_gate": (view("ffn1_w_gate", ffn1_w_gate), f4p, d), "ffn1_w_up": (view("ffn1_w_up", ffn1_w_up), f4p, d),
        "ffn1_w_down": (ffn1_w_down[0], f4p, d), "w_in": (w_in[0], d, cin), "w_out": (w_out[0], w_out.shape[1], d),
        "ffn2_w_gate": (view("ffn2_w_gate", ffn2_w_gate), f4p, d), "ffn2_w_up": (view("ffn2_w_up", ffn2_w_up), f4p, d),
        "ffn2_w_down": (ffn2_w_down[0], f4p, d),
    }
    names = list(big)
    chip1 = jnp.reshape(chip, (1,)).astype(jnp.int32)
    relative = {k: k.startswith("ffn") for k in names}
    slot0 = jnp.zeros((1,), jnp.int32)
    shard = {k: _cast_pad(big[k][0], big[k][1], big[k][2], slot0 if relative[k] else chip1, "cast_" + k) for k in names}
    full = {}

    def gather(*keys):
        return _merge_sides([_gather_side([shard[k]], relative[k], two_path=True) for k in keys])

    gm = jnp.kron(jnp.eye(2, dtype=F32), jnp.full((HEAD, HEAD), 1.0 / HEAD, F32)).astype(BF)
    wa2 = _pair_blocks(lru_w_a[0])
    wx2 = _pair_blocks(lru_w_x[0])

    dlq = dl // N_CHIP
    dq = d // N_CHIP
    R_GAIN, R_LOSS, R_META, R_LRU, R_SC, R_WA = 0, 6, 8, 24, 40, 48
    n_wrows = (N_HEADS // 2) * LANE * LANE // d
    R_WX = R_WA + n_wrows
    R_END = R_WX + n_wrows

    def pack_top(gains, meta, loss=None):
        lossrow = jnp.zeros((2, d), F32)
        if loss is not None:
            lossrow = lossrow.at[0, 0].set(loss)
        return jnp.concatenate([jnp.concatenate(gains, axis=0), lossrow, meta], axis=0)

    def pack_rest(lru16, sc8, wa_, wx_):
        return jnp.concatenate([jnp.concatenate([lru16, jnp.zeros((16, d - dl), F32)], axis=1),
                                jnp.concatenate([sc8, jnp.zeros((8, d - dl), F32)], axis=1),
                                wa_.reshape(n_wrows, d), wx_.reshape(n_wrows, d)], axis=0)

    def pack(gains, meta, lru16, sc8, wa_, wx_):
        return jnp.concatenate([pack_top(gains, meta), pack_rest(lru16, sc8, wa_, wx_)], axis=0)

    def place_cols(blk, width, total):
        return lax.dynamic_update_slice(jnp.zeros((blk.shape[0], total), F32), blk, (zero, chip * width))

    def pack_params(meta_, g1pre, g1post, gmpre, gmpost, g2pre, g2post, cw, cbias, wa_, ba_, wx_, bx_, lam_, sw, lgo, sgo):
        lru16 = jnp.concatenate([place_cols(cw[0], dlq, dl), cbias, ba_, bx_, lam_, lgo, jnp.zeros((7, dl), F32)], axis=0)
        sc8 = jnp.concatenate([place_cols(sw[0], dlq, dl), sgo, jnp.zeros((4, dl), F32)], axis=0)
        return pack([g1pre, g1post, gmpre, gmpost, g2pre, g2post], place_cols(meta_, dq, d), lru16, sc8,
                    _pair_blocks(wa_[0]), _pair_blocks(wx_[0]))

    p_w = pack_params(meta_tokens, ffn1_pre_g, ffn1_post_g, mix_pre_g, mix_post_g, ffn2_pre_g, ffn2_post_g, lru_conv_w,
                      lru_conv_b, lru_w_a, lru_b_a, lru_w_x, lru_b_x, lru_lambda, sconv_w, lru_out_g, sconv_out_g)
    p_m = pack_params(m_meta_tokens, m_ffn1_pre_g, m_ffn1_post_g, m_mix_pre_g, m_mix_post_g, m_ffn2_pre_g, m_ffn2_post_g,
                      m_lru_conv_w, m_lru_conv_b, m_lru_w_a, m_lru_b_a, m_lru_w_x, m_lru_b_x, m_lru_lambda, m_sconv_w,
                      m_lru_out_g, m_sconv_out_g)
    p_v = pack_params(v_meta_tokens, v_ffn1_pre_g, v_ffn1_post_g, v_mix_pre_g, v_mix_post_g, v_ffn2_pre_g, v_ffn2_post_g,
                      v_lru_conv_w, v_lru_conv_b, v_lru_w_a, v_lru_b_a, v_lru_w_x, v_lru_b_x, v_lru_lambda, v_sconv_w,
                      v_lru_out_g, v_sconv_out_g)

    gathered = _small_all_reduce(jnp.where(ci == 0, p_w, 0.0)[R_META:R_WA], "small_weight_gather")
    meta_full = gathered[0:N_META]
    w4_full = gathered[R_LRU - R_META:R_LRU - R_META + 4, 0:dl]
    w3_full = gathered[R_SC - R_META:R_SC - R_META + 3, 0:dl]
    w4p = jnp.concatenate([w4_full, jnp.zeros((4, dl), F32)], axis=0)
    w3p = jnp.concatenate([w3_full, jnp.zeros((5, dl), F32)], axis=0)

    h0 = jnp.concatenate([meta_full, x[0], jnp.zeros((tp - t_real, d), F32)], axis=0)
    tgt = jnp.concatenate([jnp.zeros((N_META, d), F32), loss_target[0], jnp.zeros((tp - t_real, d), F32)], axis=0)

    n1 = _norm0(h0, ffn1_pre_g)
    (a1, b1, s1), (full["ffn1_w_gate"], full["ffn1_w_up"]), got = _ffn_up_head(
        n1, shard["ffn1_w_gate"], shard["ffn1_w_up"], "ffn1_up",
        _gather_side([shard["ffn1_w_down"]], relative=True, two_path=True))
    full["ffn1_w_down"] = got[0]
    f1, got = _wide_matmul([(s1, full["ffn1_w_down"])], "ffn1_down", WIDE_TN, gather("w_in"))
    full["w_in"] = got[0]
    h1, u = _post_fwd(f1, h0, ffn1_post_g, mix_pre_g, 0.5, "ffn1_post")
    z, got = _col_matmul(u, full["w_in"], "in_proj", False, F32, gather("ffn2_w_gate"))
    full["ffn2_w_gate"] = got[0]
    (m_lru, hs), got = _lru_fwd(z, w4p, lru_conv_b, wa2.astype(BF), lru_b_a, wx2.astype(BF), lru_b_x, lru_lambda,
                                lru_out_g, gm, gather("ffn2_w_up"))
    full["ffn2_w_up"] = got[0]
    m_sc, got = _sc_fwd(z, w3p, sconv_out_g, gm, dl, gather("w_out"))
    full["w_out"] = got[0]
    mixed = jnp.concatenate([m_lru, m_sc], axis=1)
    p, _ = _wide_matmul([(mixed, full["w_out"])], "out_proj", WIDE_TN)
    h2, n2 = _post_fwd(p, h1, mix_post_g, ffn2_pre_g, 1.0, "mix_post")
    (a2, b2, s2), got = _ffn_up(n2, full["ffn2_w_gate"], full["ffn2_w_up"], "ffn2_up", gather("ffn2_w_down"))
    full["ffn2_w_down"] = got[0]
    f2, _ = _wide_matmul([(s2, full["ffn2_w_down"])], "ffn2_down", WIDE_TN)
    dh3, df2, dg_ffn2_post, loss_part = _loss_bwd(f2, h2, tgt, ffn2_post_g, t_real)

    core = jnp.reshape(ci, (1,)).astype(jnp.int32)
    sel_of = {False: jnp.stack([chip, ci]).astype(jnp.int32), True: jnp.stack([0 * chip, ci]).astype(jnp.int32)}
    red = {}

    def pair_side(k):
        return _pair_exchange_side([red[k][0]])

    def chip_side(k):
        return _chip_exchange_side([_pair_sum(red[k][0], red[k][1], core, "pair_sum_" + k)], relative[k])

    def final_sum(k):
        return _final_sum(*red[k], sel_of[relative[k]], "final_sum_" + k)

    (da2, db2), _ = _ffn_bwd_act(df2, full["ffn2_w_down"], a2, b2, "ffn2_bwd_act")
    g, _ = _wgrad_call(s2, df2, "ffn2_down_wgrad", x_width=f4p, tile_y=WGRAD_TILE_Y)
    red["ffn2_w_down"] = [g, None, None]
    g, got = _wgrad_call(da2, n2, "ffn2_gate_wgrad", x_width=f4p, tile_y=WGRAD_TILE_Y, side=pair_side("ffn2_w_down"))
    red["ffn2_w_down"][1] = got[0]
    red["ffn2_w_gate"] = [g, None, None]
    g, got = _wgrad_call(db2, n2, "ffn2_up_wgrad", x_width=f4p, tile_y=WGRAD_TILE_Y,
                         side=_merge_sides([pair_side("ffn2_w_gate"), chip_side("ffn2_w_down")]))
    red["ffn2_w_gate"][1], red["ffn2_w_down"][2] = got
    red["ffn2_w_up"] = [g, None, None]
    dn2, got = _row_matmul([(da2, full["ffn2_w_gate"]), (db2, full["ffn2_w_up"])], "ffn2_bwd_up", False, d,
                           _merge_sides([pair_side("ffn2_w_up"), chip_side("ffn2_w_gate")]), tiles=MM_TILES)
    red["ffn2_w_up"][1], red["ffn2_w_gate"][2] = got
    dh2, dp, dg_ffn2_pre, dg_mix_post = _pre_bwd(dn2, h2, dh3, ffn2_pre_g, "ffn2_pre_bwd", (p, mix_post_g, 1.0))
    dmixed, _ = _col_matmul(dp, full["w_out"], "out_proj_bwd", True, F32)
    g, _ = _wgrad_call(mixed, dp, "w_out_wgrad", x_width=mixed.shape[1] // N_CHIP, tile_y=WGRAD_TILE_Y)
    red["w_out"] = [g, None, None]
    (dzy, dzx, lru_small, dwa2, dwx2), got = _lru_bwd(
        z, hs, dmixed, w4p, lru_conv_b, wa2.astype(BF), lru_b_a, wx2.astype(BF), lru_b_x, lru_lambda, lru_out_g, gm,
        _merge_sides([pair_side("w_out"), chip_side("ffn2_w_up")]))
    red["w_out"][1], red["ffn2_w_up"][2] = got
    (dzb, dzc, dzv, sc_small), got = _sc_bwd(z, dmixed, w3p, sconv_out_g, gm, dl, chip_side("w_out"))
    red["w_out"][2] = got[0]
    dz = jnp.concatenate([dzy, dzx, dzb, dzc, dzv], axis=1)
    p_rest = pack_rest(lru_small, sc_small, dwa2, dwx2)
    g, got = _wgrad_call(u, dz, "w_in_wgrad", y_width=cin, tile_x=WGRAD_TILE_X, side=_sibling_copy_side(p_rest))
    p_rest4 = _small_pair_sum(p_rest, got[0], chip1)
    red["w_in"] = [g, None, None]
    du, got = _row_matmul([(dz, full["w_in"])], "in_proj_bwd", True, d,
                          _merge_sides([pair_side("w_in"), _slot_exchange_side(p_rest4)]))
    red["w_in"][1], p_rest4 = got
    dh1, df1, dg_mix_pre, dg_ffn1_post = _pre_bwd(du, h1, dh2, mix_pre_g, "mix_pre_bwd", (f1, ffn1_post_g, 0.5))
    (da1, db1), got = _ffn_bwd_act(df1, full["ffn1_w_down"], a1, b1, "ffn1_bwd_act", chip_side("w_in"))
    red["w_in"][2] = got[0]
    early = ["ffn2_w_down", "ffn2_w_gate", "ffn2_w_up", "w_out", "w_in"]
    late = ["ffn1_w_down", "ffn1_w_gate", "ffn1_w_up"]
    g, got = _wgrad_call(s1, df1, "ffn1_down_wgrad", x_width=f4p, tile_y=WGRAD_TILE_Y,
                         side=_join_side([final_sum(k) for k in early]))
    gfull = dict(zip(early, got))
    red["ffn1_w_down"] = [g, None, None]
    g, got = _wgrad_call(da1, n1, "ffn1_gate_wgrad", x_width=f4p, tile_y=WGRAD_TILE_Y, side=pair_side("ffn1_w_down"))
    red["ffn1_w_down"][1] = got[0]
    red["ffn1_w_gate"] = [g, None, None]
    g, got = _wgrad_call(db1, n1, "ffn1_up_wgrad", x_width=f4p, tile_y=WGRAD_TILE_Y,
                         side=_merge_sides([pair_side("ffn1_w_gate"), chip_side("ffn1_w_down")]))
    red["ffn1_w_gate"][1], red["ffn1_w_down"][2] = got
    red["ffn1_w_up"] = [g, None, None]
    red["ffn1_w_up"][1] = _run_side(pair_side("ffn1_w_up"), "pair_exchange_ffn1_w_up")[0]
    dn1, got = _row_matmul([(da1, full["ffn1_w_gate"]), (db1, full["ffn1_w_up"])], "ffn1_bwd_up", False, d,
                           _merge_sides([chip_side("ffn1_w_gate"), chip_side("ffn1_w_up")]), tiles=MM_TILES)
    red["ffn1_w_gate"][2], red["ffn1_w_up"][2] = got
    (dh0, dg_ffn1_pre), got = _pre_bwd(dn1, h0, dh1, ffn1_pre_g, "ffn1_pre_bwd",
                                       side=_join_side([final_sum(k) for k in late]))
    gfull.update(zip(late, got))

    grad_x = dh0[N_META:t_real][None]

    w_big = {"ffn1_w_gate": ffn1_w_gate, "ffn1_w_up": ffn1_w_up, "ffn1_w_down": ffn1_w_down, "w_in": w_in, "w_out": w_out,
             "ffn2_w_gate": ffn2_w_gate, "ffn2_w_up": ffn2_w_up, "ffn2_w_down": ffn2_w_down}
    m_big = {"ffn1_w_gate": m_ffn1_w_gate, "ffn1_w_up": m_ffn1_w_up, "ffn1_w_down": m_ffn1_w_down, "w_in": m_w_in,
             "w_out": m_w_out, "ffn2_w_gate": m_ffn2_w_gate, "ffn2_w_up": m_ffn2_w_up, "ffn2_w_down": m_ffn2_w_down}
    v_big = {"ffn1_w_gate": v_ffn1_w_gate, "ffn1_w_up": v_ffn1_w_up, "ffn1_w_down": v_ffn1_w_down, "w_in": v_w_in,
             "w_out": v_w_out, "ffn2_w_gate": v_ffn2_w_gate, "ffn2_w_up": v_ffn2_w_up, "ffn2_w_down": v_ffn2_w_down}
    b_grad, b_delta, b_newm, b_newv = {}, {}, {}, {}

    def big_adamw(k, side=None):
        wv, mv, vv = view(k, w_big[k]), view(k, m_big[k]), view(k, v_big[k])
        wide_rows = wv.shape[0] % 64 == 0
        (g_, d_, m_, v_), got = _adamw(wv, gfull[k], mv, vv, "adamw_" + k, 8 if wide_rows else 4, 1 if wide_rows else 2,
                                       side)
        b_grad[k], b_delta[k], b_newm[k], b_newv[k] = unview(k, g_), unview(k, d_), unview(k, m_), unview(k, v_)
        return got

    p_top = _small_all_reduce(
        pack_top([dg_ffn1_pre, dg_ffn1_post, dg_mix_pre, dg_mix_post, dg_ffn2_pre, dg_ffn2_post], dh0[0:N_META],
                 loss=loss_part[0, 0]), "small_grad_all_reduce")
    p_g, p_delta, p_newm, p_newv = _adamw_small(p_w, p_top, p_rest4, p_m, p_v)
    loss = p_g[R_LOSS, 0]
    for k in names:
        big_adamw(k)

    def unpack(buf):
        out = {}
        for i, k in enumerate(["ffn1_pre_g", "ffn1_post_g", "mix_pre_g", "mix_post_g", "ffn2_pre_g", "ffn2_post_g"]):
            out[k] = buf[R_GAIN + i:R_GAIN + i + 1]
        out["meta_tokens"] = lax.dynamic_slice(buf[R_META:R_META + N_META], (zero, chip * dq), (N_META, dq))
        lru = buf[R_LRU:R_LRU + 16, 0:dl]
        out["lru_conv_w"] = lax.dynamic_slice(lru[0:4], (zero, chip * dlq), (4, dlq))[None]
        out["lru_conv_b"] = lru[4:5]
        out["lru_b_a"] = lru[5:6]
        out["lru_b_x"] = lru[6:7]
        out["lru_lambda"] = lru[7:8]
        out["lru_out_g"] = lru[8:9]
        sc = buf[R_SC:R_SC + 8, 0:dl]
        out["sconv_w"] = lax.dynamic_slice(sc[0:3], (zero, chip * dlq), (3, dlq))[None]
        out["sconv_out_g"] = sc[3:4]
        out["lru_w_a"] = _unpair_blocks(buf[R_WA:R_WX].reshape(N_HEADS // 2, LANE, LANE))[None]
        out["lru_w_x"] = _unpair_blocks(buf[R_WX:R_END].reshape(N_HEADS // 2, LANE, LANE))[None]
        return out

    s_grad, s_delta, s_newm, s_newv = unpack(p_g), unpack(p_delta), unpack(p_newm), unpack(p_newv)

    order = ["meta_tokens", "ffn1_pre_g", "ffn1_w_gate", "ffn1_w_up", "ffn1_w_down", "ffn1_post_g", "mix_pre_g", "w_in",
             "lru_conv_w", "lru_conv_b", "lru_w_a", "lru_b_a", "lru_w_x", "lru_b_x", "lru_lambda", "sconv_w", "lru_out_g",
             "sconv_out_g", "w_out", "mix_post_g", "ffn2_pre_g", "ffn2_w_gate", "ffn2_w_up", "ffn2_w_down", "ffn2_post_g"]

    def pick(small, bigd):
        return [bigd[k] if k in bigd else small[k] for k in order]

    return (loss, grad_x, *pick(s_grad, b_grad), *pick(s_delta, b_delta), *pick(s_newm, b_newm), *pick(s_newv, b_newv))
```

```python
import functools
import math

import jax
import jax.numpy as jnp
from jax import lax
from jax.experimental import pallas as pl
from jax.experimental.pallas import tpu as pltpu

F32 = jnp.float32
BF = jnp.bfloat16
MESH = pl.DeviceIdType.MESH

EPS = 1e-6
N_META = 16
N_HEADS = 16
HEAD = 64
LRU_C = 8.0
LANE = 128
MXU_COLS = 256
N_CHIP = 4
ROW_ALIGN = 384
MM_TILES = 8
MM_TILES_BIG = 4
SWIGLU_TILES = 6
EW_TILES = 12
MIX_CHUNKS = 24
WGRAD_TILE_X = 512
WGRAD_TILE_Y = 512
WIDE_TN = 512
VMEM_LIMIT = 56 << 20

ADAM_LR = 0.001
ADAM_B1 = 0.9
ADAM_B2 = 0.999
ADAM_EPS = 1e-08
ADAM_WD = 0.01
ADAM_STEP = 10


def _round_up(a, b):
    return (a + b - 1) // b * b


def _params(sem=None):
    if sem is None:
        return pltpu.CompilerParams(vmem_limit_bytes=VMEM_LIMIT)
    return pltpu.CompilerParams(dimension_semantics=sem, vmem_limit_bytes=VMEM_LIMIT)


def _sigmoid(x):
    return 0.5 * jnp.tanh(0.5 * x) + 0.5


def _dot(a, b):
    return jnp.dot(a, b, preferred_element_type=F32)


def _dot_nt(a, b):
    return lax.dot_general(a, b, (((1,), (1,)), ((), ())), preferred_element_type=F32)


def _dot_tn(a, b):
    return lax.dot_general(a, b, (((0,), (0,)), ((), ())), preferred_element_type=F32)


def _rms(x, g):
    r = lax.rsqrt(jnp.mean(x * x, axis=-1, keepdims=True) + EPS)
    return x * r * g


def _rms_bwd(x, g, dy):
    r = lax.rsqrt(jnp.mean(x * x, axis=-1, keepdims=True) + EPS)
    xh = x * r
    q = dy * g
    dx = r * (q - xh * jnp.mean(q * xh, axis=-1, keepdims=True))
    return dx, dy * xh


class _Side:
    def __init__(self, ins, outs, alias, sems, start, finish):
        self.ins, self.outs, self.alias, self.sems, self.start, self.finish = ins, outs, alias, sems, start, finish


def _merge_sides(sides):
    sides = [s for s in sides if s is not None]
    if len(sides) <= 1:
        return sides[0] if sides else None
    ins, outs, sems, alias, spans = [], [], [], {}, []
    for s in sides:
        for i, o in s.alias.items():
            alias[len(ins) + i] = len(outs) + o
        spans.append((len(ins), len(ins) + len(s.ins), len(outs), len(outs) + len(s.outs), len(sems),
                      len(sems) + len(s.sems)))
        ins += list(s.ins)
        outs += list(s.outs)
        sems += list(s.sems)

    def run(which):
        def go(in_refs, out_refs, sem_refs):
            for s, (a, b, c, d, e, f) in zip(sides, spans):
                getattr(s, which)(in_refs[a:b], out_refs[c:d], sem_refs[e:f])
        return go

    return _Side(ins, outs, alias, sems, run("start"), run("finish"))


def _grid_call(body, name, grid, in_specs, out_specs, out_shape, args, side=None, scratch=()):
    sem = ("arbitrary",) * len(grid)
    if side is None:
        res = pl.pallas_call(body, name=name, grid=grid, in_specs=in_specs, out_specs=out_specs, out_shape=out_shape,
                             scratch_shapes=list(scratch), compiler_params=_params(sem))(*args)
        return res, []
    nin, nout, sin, sout = len(in_specs), len(out_specs), len(side.ins), len(side.outs)
    nscr = len(scratch)
    staged = hasattr(side, "middle") and math.prod(grid) >= 4
    lin, mid = (math.prod(grid) * 5) // 8, []
    for extent in reversed(grid):
        mid.insert(0, lin % extent)
        lin //= extent

    def full(*refs):
        base_in, side_in = refs[:nin], refs[nin:nin + sin]
        base_out = refs[nin + sin:nin + sin + nout]
        side_out = refs[nin + sin + nout:nin + sin + nout + sout]
        base_scr = refs[nin + sin + nout + sout:nin + sin + nout + sout + nscr]
        sems = refs[nin + sin + nout + sout + nscr:]
        first = pl.program_id(0) == 0
        last = pl.program_id(0) == grid[0] - 1
        for ax in range(1, len(grid)):
            first = first & (pl.program_id(ax) == 0)
            last = last & (pl.program_id(ax) == grid[ax] - 1)

        @pl.when(first)
        def _():
            side.start(side_in, side_out, sems)

        if staged:
            at_mid = pl.program_id(0) == mid[0]
            for ax in range(1, len(grid)):
                at_mid = at_mid & (pl.program_id(ax) == mid[ax])

            @pl.when(at_mid)
            def _():
                side.middle(side_in, side_out, sems)

        body(*base_in, *base_out, *base_scr)

        @pl.when(last)
        def _():
            (side.rest if staged else side.finish)(side_in, side_out, sems)

    any_spec = pl.BlockSpec(memory_space=pl.ANY)
    res = pl.pallas_call(
        full, name=name, grid=grid, in_specs=list(in_specs) + [any_spec] * sin,
        out_specs=list(out_specs) + [any_spec] * sout, out_shape=list(out_shape) + list(side.outs),
        scratch_shapes=list(scratch) + list(side.sems),
        input_output_aliases={nin + i: nout + o for i, o in side.alias.items()},
        compiler_params=_params(sem))(*args, *side.ins)
    return res[:nout], res[nout:]


def _ffn_up(n, wg, wu, name, side=None, tiles=SWIGLU_TILES):
    tp, d = n.shape
    fp = wg.shape[1]
    tm = tp // tiles

    def body(n_ref, wg_ref, wu_ref, a_ref, b_ref, s_ref):
        nn = n_ref[...]
        for c0 in range(0, fp, MXU_COLS):
            cs = slice(c0, min(c0 + MXU_COLS, fp))
            a = _dot_nt(nn, wg_ref[cs, :])
            b = _dot_nt(nn, wu_ref[cs, :])
            a_ref[:, cs] = a.astype(BF)
            b_ref[:, cs] = b.astype(BF)
            s_ref[:, cs] = (a * _sigmoid(a) * b).astype(BF)

    out = jax.ShapeDtypeStruct((tp, N_CHIP * fp), BF)
    wspec = pl.BlockSpec((None, fp, d), lambda k, i: (k, 0, 0))
    ospec = pl.BlockSpec((tm, fp), lambda k, i: (i, k))
    return _grid_call(body, name, (N_CHIP, tiles), [pl.BlockSpec((tm, d), lambda k, i: (i, 0)), wspec, wspec],
                      [ospec, ospec, ospec], [out, out, out], (n, wg, wu), side)


def _ffn_up_head(n, wg, wu, name, side):
    tp, d = n.shape
    fp = wg.shape[1]
    tiles = SWIGLU_TILES
    tm = tp // tiles
    gat = _gather_side([wg, wu], relative=True, two_path=True)
    sin, sout, ngs = len(side.ins), len(side.outs), len(gat.sems)
    order = (0,) + REL_SLOT
    staged = hasattr(side, "middle")

    def body(*refs):
        n_ref = refs[0]
        si = refs[3:3 + sin]
        a_ref, b_ref, s_ref = refs[3 + sin:6 + sin]
        go = refs[6 + sin:8 + sin]
        so = refs[8 + sin:8 + sin + sout]
        wbg, wbu, wsem = refs[8 + sin + sout:11 + sin + sout]
        gsems = refs[11 + sin + sout:11 + sin + sout + ngs]
        ssems = refs[11 + sin + sout + ngs:]
        k, i = pl.program_id(0), pl.program_id(1)
        cur = k % 2

        def to_vmem(slot, buf):
            return [pltpu.make_async_copy(go[0].at[slot], wbg.at[buf], wsem.at[buf, 0]),
                    pltpu.make_async_copy(go[1].at[slot], wbu.at[buf], wsem.at[buf, 1])]

        @pl.when((k == 0) & (i == 0))
        def _():
            gat.send(go, gsems)
            if not staged:
                side.start(si, so, ssems)
            for cp in to_vmem(0, 0):
                cp.start()
            for cp in to_vmem(0, 0):
                cp.wait()

        for j in range(3):
            @pl.when((k == j) & (i == tiles // 2))
            def _():
                gat.arrived(j, go, gsems)
                if staged and j == 1:
                    side.start(si, so, ssems)

            @pl.when((k == j) & (i == tiles - 2))
            def _():
                gat.forwarded(j, go, gsems)
                for cp in to_vmem(order[j + 1], (j + 1) % 2):
                    cp.start()

            @pl.when((k == j + 1) & (i == 0))
            def _():
                for cp in to_vmem(order[j + 1], (j + 1) % 2):
                    cp.wait()

        nn = n_ref[...]
        for c0 in range(0, fp, MXU_COLS):
            cs = pl.ds(c0, min(MXU_COLS, fp - c0))
            a = _dot_nt(nn, wbg[cur, cs, :])
            b = _dot_nt(nn, wbu[cur, cs, :])
            a_ref[:, cs] = a.astype(BF)
            b_ref[:, cs] = b.astype(BF)
            s_ref[:, cs] = (a * _sigmoid(a) * b).astype(BF)

        if staged:
            @pl.when((k == N_CHIP - 1) & (i == tiles // 4))
            def _():
                side.middle(si, so, ssems)

        @pl.when((k == N_CHIP - 1) & (i == tiles - 1))
        def _():
            gat.drain(go, gsems)
            if staged:
                side.rest(si, so, ssems)
            else:
                side.finish(si, so, ssems)

    out = jax.ShapeDtypeStruct((tp, N_CHIP * fp), BF)
    any_spec = pl.BlockSpec(memory_space=pl.ANY)
    slot_of = lambda k: (k % 2) * 2 + k // 2
    ospec = pl.BlockSpec((tm, fp), lambda k, i: (i, slot_of(k)))
    wbuf = pltpu.VMEM((2, fp, d), BF)
    res = pl.pallas_call(
        body, name=name, grid=(N_CHIP, tiles),
        in_specs=[pl.BlockSpec((tm, d), lambda k, i: (i, 0))] + [any_spec] * (2 + sin),
        out_specs=[ospec, ospec, ospec] + [any_spec] * (2 + sout),
        out_shape=[out, out, out] + list(gat.outs) + list(side.outs),
        scratch_shapes=[wbuf, wbuf, pltpu.SemaphoreType.DMA((2, 2))] + list(gat.sems) + list(side.sems),
        input_output_aliases={1: 3, 2: 4, **{3 + a: 5 + b for a, b in side.alias.items()}},
        compiler_params=_params(("arbitrary", "arbitrary")))(n, wg, wu, *side.ins)
    return res[:3], res[3:5], res[5:]


def _ffn_bwd_act(df, wd, a, b, name, side=None, tiles=SWIGLU_TILES):
    tp, d = df.shape
    fp = wd.shape[1]
    tm = tp // tiles

    def body(df_ref, wd_ref, a_ref, b_ref, da_ref, db_ref):
        dfv = df_ref[...]
        for c0 in range(0, fp, MXU_COLS):
            cs = slice(c0, min(c0 + MXU_COLS, fp))
            ds = _dot_nt(dfv, wd_ref[cs, :])
            av = a_ref[:, cs].astype(F32)
            bv = b_ref[:, cs].astype(F32)
            sg = _sigmoid(av)
            da_ref[:, cs] = (ds * bv * sg * (1.0 + av * (1.0 - sg))).astype(BF)
            db_ref[:, cs] = (ds * av * sg).astype(BF)

    out = jax.ShapeDtypeStruct((tp, N_CHIP * fp), BF)
    aspec = pl.BlockSpec((tm, fp), lambda k, i: (i, k))
    return _grid_call(
        body, name, (N_CHIP, tiles),
        [pl.BlockSpec((tm, d), lambda k, i: (i, 0)), pl.BlockSpec((None, fp, d), lambda k, i: (k, 0, 0)), aspec, aspec],
        [aspec, aspec], [out, out], (df, wd, a, b), side)


def _col_matmul(lhs, w, name, trans_b, out_dtype, side=None, tiles=MM_TILES_BIG):
    tp, kd = lhs.shape
    nk = w.shape[0]
    nc = w.shape[1] if trans_b else w.shape[2]
    tm = tp // tiles

    def body(l_ref, w_ref, o_ref):
        if trans_b:
            o_ref[...] = _dot_nt(l_ref[...], w_ref[...]).astype(out_dtype)
        else:
            o_ref[...] = _dot(l_ref[...], w_ref[...]).astype(out_dtype)

    res, extra = _grid_call(
        body, name, (nk, tiles),
        [pl.BlockSpec((tm, kd), lambda k, i: (i, 0)),
         pl.BlockSpec((None,) + tuple(w.shape[1:]), lambda k, i: (k, 0, 0), pipeline_mode=pl.Buffered(1))],
        [pl.BlockSpec((tm, nc), lambda k, i: (i, k))], [jax.ShapeDtypeStruct((tp, nk * nc), out_dtype)], (lhs, w), side)
    return res[0], extra


def _row_matmul(pairs, name, trans_b, d_out, side=None, tiles=MM_TILES_BIG):
    l0 = pairs[0][0]
    tp = l0.shape[1] if l0.ndim == 3 else l0.shape[0]
    nk = pairs[0][1].shape[0]
    tm = tp // tiles
    npair = len(pairs)

    def body(*refs):
        o_ref = refs[2 * npair]
        k = pl.program_id(1)
        part = None
        for q in range(npair):
            l = refs[2 * q][...]
            w = refs[2 * q + 1][...]
            t = _dot_nt(l, w) if trans_b else _dot(l, w)
            part = t if part is None else part + t

        @pl.when(k == 0)
        def _():
            o_ref[...] = part

        @pl.when(k > 0)
        def _():
            o_ref[...] += part

    in_specs, args = [], []
    for lhs, w in pairs:
        if lhs.ndim == 3:
            in_specs.append(pl.BlockSpec((None, tm, lhs.shape[2]), lambda i, k: (k, i, 0)))
        else:
            in_specs.append(pl.BlockSpec((tm, lhs.shape[1] // nk), lambda i, k: (i, k)))
        in_specs.append(pl.BlockSpec((None,) + tuple(w.shape[1:]), lambda i, k: (k, 0, 0)))
        args += [lhs, w]
    res, extra = _grid_call(body, name, (tiles, nk), in_specs, [pl.BlockSpec((tm, d_out), lambda i, k: (i, 0))],
                            [jax.ShapeDtypeStruct((tp, d_out), F32)], args, side)
    return res[0], extra


def _wide_matmul(pairs, name, tn, side=None):
    tp = pairs[0][0].shape[0]
    d_out = pairs[0][1].shape[2]
    tm = tp // MM_TILES_BIG
    npair = len(pairs)

    def body(*refs):
        acc = None
        for q in range(npair):
            t = _dot(refs[2 * q][...], refs[2 * q + 1][...])
            acc = t if acc is None else acc + t
        refs[2 * npair][...] = acc

    in_specs, args = [], []
    for lhs, w in pairs:
        kdim = lhs.shape[1]
        in_specs += [pl.BlockSpec((tm, kdim), lambda n, i: (i, 0)), pl.BlockSpec((kdim, tn), lambda n, i: (0, n))]
        args += [lhs, w.reshape(kdim, d_out)]
    res, extra = _grid_call(body, name, (d_out // tn, MM_TILES_BIG), in_specs, [pl.BlockSpec((tm, tn), lambda n, i: (i, n))],
                            [jax.ShapeDtypeStruct((tp, d_out), F32)], args, side)
    return res[0], extra


def _wgrad_call(x, y, name, x_width=None, y_width=None, tile_x=None, tile_y=None, side=None):
    tp = x.shape[1] if x.ndim == 3 else x.shape[0]

    def spec(a, width, tile):
        cols = a.shape[2] if a.ndim == 3 else (a.shape[1] if width is None else width)
        tc = cols if tile is None else tile
        per = cols // tc
        if a.ndim == 3:
            return pl.BlockSpec((None, tp, tc), lambda k, t: (k, 0, t if tile else 0)), cols, per
        if width is None:
            return pl.BlockSpec((tp, tc), lambda k, t: (0, t if tile else 0)), cols, per
        return pl.BlockSpec((tp, tc), lambda k, t: (0, k * per + (t if tile else 0))), cols, per

    xs, p, nx = spec(x, x_width, tile_x)
    ys, q, ny = spec(y, y_width, tile_y)
    nt = nx * ny
    if tile_x:
        ospec = pl.BlockSpec((None, tile_x, q), lambda k, t: (k, t, 0))
    else:
        ospec = pl.BlockSpec((None, p, tile_y), lambda k, t: (k, 0, t))

    def body(x_ref, y_ref, o_ref):
        o_ref[...] = _dot_tn(x_ref[...], y_ref[...]).astype(BF)

    res, extra = _grid_call(body, name, (N_CHIP, nt), [xs, ys], [ospec], [jax.ShapeDtypeStruct((N_CHIP, p, q), BF)],
                            (x, y), side)
    return res[0], extra


def _row_call(body, name, tp, d, row_ins, vec_ins, row_out_dtypes, n_acc, side=None):
    te = tp // EW_TILES
    rspec = pl.BlockSpec((te, d), lambda i: (i, 0))
    vspec = pl.BlockSpec((1, d), lambda i: (0, 0))
    res, extra = _grid_call(
        body, name, (EW_TILES,), [rspec] * len(row_ins) + [vspec] * len(vec_ins),
        [rspec] * len(row_out_dtypes) + [vspec] * n_acc,
        [jax.ShapeDtypeStruct((tp, d), dt) for dt in row_out_dtypes] + [jax.ShapeDtypeStruct((1, d), F32)] * n_acc,
        (*row_ins, *vec_ins), side)
    return res if side is None else (res, extra)


def _norm0(h, g):
    tp, d = h.shape

    def body(h_ref, g_ref, n_ref):
        n_ref[...] = _rms(h_ref[...], g_ref[...]).astype(BF)

    return _row_call(body, "norm0", tp, d, [h], [g], [BF], 0)[0]


def _post_fwd(f, h, g_post, g_next, scale, name):
    tp, d = h.shape

    def body(f_ref, h_ref, gp_ref, gn_ref, hn_ref, n_ref):
        hn = h_ref[...] + scale * _rms(f_ref[...], gp_ref[...])
        hn_ref[...] = hn
        n_ref[...] = _rms(hn, gn_ref[...]).astype(BF)

    return _row_call(body, name, tp, d, [f, h], [g_post, g_next], [F32, BF], 0)


def _loss_bwd(f, h, tgt, g_post, t_real):
    tp, d = h.shape
    te = tp // EW_TILES

    def body(f_ref, h_ref, t_ref, gp_ref, dh_ref, df_ref, dg_ref, loss_ref):
        i = pl.program_id(0)

        @pl.when(i == 0)
        def _():
            dg_ref[...] = jnp.zeros_like(dg_ref)
            loss_ref[...] = jnp.zeros_like(loss_ref)

        f = f_ref[...]
        gp = gp_ref[...]
        h3 = h_ref[...] + 0.5 * _rms(f, gp)
        rows = i * te + lax.broadcasted_iota(jnp.int32, (te, 1), 0)
        real = (rows >= N_META) & (rows < t_real)
        e = jnp.where(real, h3 - t_ref[...], 0.0)
        loss_ref[...] += 0.5 * jnp.sum(jnp.sum(e * e, axis=1, keepdims=True), axis=0, keepdims=True) / d
        dh = e / d
        dh_ref[...] = dh
        dfv, dgr = _rms_bwd(f, gp, 0.5 * dh)
        df_ref[...] = dfv.astype(BF)
        dg_ref[...] += jnp.sum(dgr, axis=0, keepdims=True)

    rspec = pl.BlockSpec((te, d), lambda i: (i, 0))
    vspec = pl.BlockSpec((1, d), lambda i: (0, 0))
    return pl.pallas_call(
        body, name="loss_bwd", grid=(EW_TILES,),
        in_specs=[rspec, rspec, rspec, vspec],
        out_specs=[rspec, rspec, vspec, pl.BlockSpec((1, 1), lambda i: (0, 0))],
        out_shape=[jax.ShapeDtypeStruct((tp, d), F32), jax.ShapeDtypeStruct((tp, d), BF),
                   jax.ShapeDtypeStruct((1, d), F32), jax.ShapeDtypeStruct((1, 1), F32)],
        compiler_params=_params(("arbitrary",)),
    )(f, h, tgt, g_post)


def _pre_bwd(dn, h, dh_out, g_pre, name, chain=None, side=None):
    tp, d = h.shape

    def body(*refs):
        if chain is None:
            dn_ref, h_ref, dho_ref, g_ref, dh_ref, dg_ref = refs
        else:
            dn_ref, h_ref, dho_ref, p_ref, g_ref, gp_ref, dh_ref, dp_ref, dg_ref, dgp_ref = refs
        i = pl.program_id(0)

        @pl.when(i == 0)
        def _():
            dg_ref[...] = jnp.zeros_like(dg_ref)
            if chain is not None:
                dgp_ref[...] = jnp.zeros_like(dgp_ref)

        dx, dgr = _rms_bwd(h_ref[...], g_ref[...], dn_ref[...])
        dh = dho_ref[...] + dx
        dh_ref[...] = dh
        dg_ref[...] += jnp.sum(dgr, axis=0, keepdims=True)
        if chain is not None:
            dp, dgpr = _rms_bwd(p_ref[...], gp_ref[...], chain[2] * dh)
            dp_ref[...] = dp.astype(BF)
            dgp_ref[...] += jnp.sum(dgpr, axis=0, keepdims=True)

    if chain is None:
        return _row_call(body, name, tp, d, [dn, h, dh_out], [g_pre], [F32], 1, side)
    return _row_call(body, name, tp, d, [dn, h, dh_out, chain[0]], [g_pre, chain[1]], [F32, BF], 2, side)


def _gelu(y):
    c = math.sqrt(2.0 / math.pi)
    return 0.5 * y * (1.0 + jnp.tanh(c * (y + 0.044715 * y * y * y)))


def _gelu_and_grad(y):
    c = math.sqrt(2.0 / math.pi)
    y2 = y * y
    t = jnp.tanh(c * y * (1.0 + 0.044715 * y2))
    half = 0.5 * (1.0 + t)
    return y * half, half + 0.5 * y * (1.0 - t * t) * c * (1.0 + 3.0 * 0.044715 * y2)


def _neg_expm1(x):
    p = 1.0 + x * (1.0 / 9.0)
    for n in (8.0, 7.0, 6.0, 5.0, 4.0, 3.0, 2.0):
        p = 1.0 + x * (1.0 / n) * p
    return -jnp.where(x > -0.35, x * p, jnp.exp(x) - 1.0)


def _softplus(x):
    e = jnp.exp(-jnp.abs(x))
    w = 1.0 + e
    l1p = jnp.where(w == 1.0, e, jnp.log(w) * (e / jnp.where(w == 1.0, 1.0, w - 1.0)))
    return jnp.maximum(x, 0.0) + l1p


def _group_mean(v, gm):
    hi = v.astype(BF)
    lo = (v - hi.astype(F32)).astype(BF)
    return _dot(hi, gm) + _dot(lo, gm)


def _shift_dn(win, s, r):
    if s == 0:
        return win[8:8 + r]
    return pltpu.roll(win, s, 0)[8:8 + r]


def _shift_up(win, s, r):
    if s == 0:
        return win[0:r]
    return pltpu.roll(win, r + 8 - s, 0)[0:r]


def _window_dn(ref, t0, r, first):
    if first:
        return jnp.concatenate([jnp.zeros((8, ref.shape[1]), F32), ref[0:r, :]], axis=0)
    return ref[pl.ds(t0 - 8, r + 8), :]


def _tile_scan(a, u, reverse):
    r = a.shape[0]
    rid = lax.broadcasted_iota(jnp.int32, a.shape, 0) & 7
    for dlt in (1, 2, 4):
        sh = (r - dlt) if reverse else dlt
        a_s = pltpu.roll(a, sh, 0)
        u_s = pltpu.roll(u, sh, 0)
        keep = (rid + dlt <= 7) if reverse else (rid >= dlt)
        u = jnp.where(keep, u + a * u_s, u)
        a = jnp.where(keep, a * a_s, a)
    return a, u


def _lru_gates(xc, wa, ba, wx, bx, sp):
    xb = xc.astype(BF)
    ga = _sigmoid(_dot(xb, wa) + ba)
    gx = _sigmoid(_dot(xb, wx) + bx)
    la = -LRU_C * ga * sp
    return ga, gx, la


def _conv4(win, w4, cb, r):
    return (cb + w4[3:4] * _shift_dn(win, 0, r) + w4[2:3] * _shift_dn(win, 1, r)
            + w4[1:2] * _shift_dn(win, 2, r) + w4[0:1] * _shift_dn(win, 3, r))


def _lru_fwd(z, w4, cb, wa2, ba, wx2, bx, lam, g_out, gm, side=None):
    tp = z.shape[0]
    dl = cb.shape[1]
    nb = dl // LANE
    r = tp // MIX_CHUNKS
    c = LANE

    def body(y_ref, x_ref, w4_ref, cb_ref, wa_ref, ba_ref, wx_ref, bx_ref, lam_ref, go_ref, gm_ref, m_ref, hs_ref):
        w4v = w4_ref[...]
        cbv = cb_ref[...]
        wa = wa_ref[...]
        wx = wx_ref[...]
        bav = ba_ref[...]
        bxv = bx_ref[...]
        gov = go_ref[...]
        gmv = gm_ref[...]
        sp = _softplus(-lam_ref[...])

        def chunk(t0, hprev, first):
            win = _window_dn(x_ref, t0, r, first)
            xc = _conv4(win, w4v, cbv, r)
            ga, gx, la = _lru_gates(xc, wa, bav, wx, bxv, sp)
            a = jnp.exp(la)
            u = jnp.sqrt(_neg_expm1(2.0 * la)) * gx * xc
            ac, uc = _tile_scan(a, u, False)
            for j in range(r // 8):
                hj = uc[8 * j:8 * j + 8] + ac[8 * j:8 * j + 8] * hprev
                hs_ref[pl.ds(t0 + 8 * j, 8), :] = hj
                hprev = jnp.broadcast_to(hj[7:8], (8, c))
            h = hs_ref[pl.ds(t0, r), :]
            lo = h * _gelu(y_ref[pl.ds(t0, r), :])
            rs = lax.rsqrt(_group_mean(lo * lo, gmv) + EPS)
            m_ref[pl.ds(t0, r), :] = (lo * rs * gov).astype(BF)
            return hprev

        hp = chunk(0, jnp.zeros((8, c), F32), True)

        def loop(ci, hp):
            return chunk(pl.multiple_of(ci * r, 16), hp, False)

        lax.fori_loop(1, MIX_CHUNKS, loop, hp)

    col = lambda off: pl.BlockSpec((tp, c), lambda j: (0, off + j))
    vec = pl.BlockSpec((1, c), lambda j: (0, j))
    return _grid_call(
        body, "lru_fwd", (nb,),
        [col(0), col(nb), pl.BlockSpec((8, c), lambda j: (0, j)), vec, pl.BlockSpec((None, c, c), lambda j: (j, 0, 0)),
         vec, pl.BlockSpec((None, c, c), lambda j: (j, 0, 0)), vec, vec, vec, pl.BlockSpec((c, c), lambda j: (0, 0))],
        [col(0), col(0)], [jax.ShapeDtypeStruct((tp, dl), BF), jax.ShapeDtypeStruct((tp, dl), F32)],
        (z, z, w4, cb, wa2, ba, wx2, bx, lam, g_out, gm), side)


def _lru_bwd(z, hs, dmix, w4, cb, wa2, ba, wx2, bx, lam, g_out, gm, side=None):
    tp = z.shape[0]
    dl = cb.shape[1]
    nb = dl // LANE
    r = tp // MIX_CHUNKS
    c = LANE

    def body(y_ref, x_ref, hs_ref, dm_ref, w4_ref, cb_ref, wa_ref, ba_ref, wx_ref, bx_ref, lam_ref, go_ref, gm_ref,
             dy_ref, dx_ref, small_ref, dwa_ref, dwx_ref, xc_buf, ga_buf, gx_buf, a_buf, dh_buf, dxc_buf):
        w4v = w4_ref[...]
        cbv = cb_ref[...]
        wa = wa_ref[...]
        wx = wx_ref[...]
        bav = ba_ref[...]
        bxv = bx_ref[...]
        gov = go_ref[...]
        gmv = gm_ref[...]
        lamv = lam_ref[...]
        sp = _softplus(-lamv)
        small_ref[...] = jnp.zeros_like(small_ref)
        dwa_ref[...] = jnp.zeros_like(dwa_ref)
        dwx_ref[...] = jnp.zeros_like(dwx_ref)
        a_buf[pl.ds(tp, 8), :] = jnp.zeros((8, c), F32)
        dxc_buf[pl.ds(tp, 8), :] = jnp.zeros((8, c), F32)

        def fwd_chunk(t0, first):
            win = _window_dn(x_ref, t0, r, first)
            xc = _conv4(win, w4v, cbv, r)
            ga, gx, la = _lru_gates(xc, wa, bav, wx, bxv, sp)
            xc_buf[pl.ds(t0, r), :] = xc
            ga_buf[pl.ds(t0, r), :] = ga
            gx_buf[pl.ds(t0, r), :] = gx
            a_buf[pl.ds(t0, r), :] = jnp.exp(la)
            h = hs_ref[pl.ds(t0, r), :]
            yv = y_ref[pl.ds(t0, r), :]
            ge, dge = _gelu_and_grad(yv)
            lo = h * ge
            rs = lax.rsqrt(_group_mean(lo * lo, gmv) + EPS)
            xh = lo * rs
            dm = dm_ref[pl.ds(t0, r), :]
            q = dm * gov
            dlo = rs * (q - xh * _group_mean(q * xh, gmv))
            small_ref[8:9, :] += jnp.sum(dm * xh, axis=0, keepdims=True)
            dh_buf[pl.ds(t0, r), :] = dlo * ge
            dy_ref[pl.ds(t0, r), :] = (dlo * h * dge).astype(BF)

        fwd_chunk(0, True)

        def floop(ci, carry):
            fwd_chunk(pl.multiple_of(ci * r, 16), False)
            return carry

        lax.fori_loop(1, MIX_CHUNKS, floop, 0)

        def bwd_chunk(t0, vnext, first):
            ap = _shift_up(a_buf[pl.ds(t0, r + 8), :], 1, r)
            ac, uc = _tile_scan(ap, dh_buf[pl.ds(t0, r), :], True)
            for j in reversed(range(r // 8)):
                vj = uc[8 * j:8 * j + 8] + ac[8 * j:8 * j + 8] * vnext
                dh_buf[pl.ds(t0 + 8 * j, 8), :] = vj
                vnext = jnp.broadcast_to(vj[0:1], (8, c))
            v = dh_buf[pl.ds(t0, r), :]
            hprev = _shift_dn(_window_dn(hs_ref, t0, r, first), 1, r)
            xc = xc_buf[pl.ds(t0, r), :]
            ga = ga_buf[pl.ds(t0, r), :]
            gx = gx_buf[pl.ds(t0, r), :]
            a = a_buf[pl.ds(t0, r), :]
            em = _neg_expm1(-2.0 * LRU_C * ga * sp)
            mult = jnp.sqrt(em)
            dla = v * hprev * a - (v * gx * xc) * ((1.0 - em) / mult)
            dgx = v * mult * xc
            dxc = v * mult * gx
            dga = dla * (-LRU_C) * sp
            small_ref[7:8, :] += jnp.sum(dla * (-LRU_C) * ga, axis=0, keepdims=True)
            dpa = dga * ga * (1.0 - ga)
            dpx = dgx * gx * (1.0 - gx)
            small_ref[5:6, :] += jnp.sum(dpa, axis=0, keepdims=True)
            small_ref[6:7, :] += jnp.sum(dpx, axis=0, keepdims=True)
            dpab = dpa.astype(BF)
            dpxb = dpx.astype(BF)
            xb = xc.astype(BF)
            dxc = dxc + _dot_nt(dpab, wa) + _dot_nt(dpxb, wx)
            dwa_ref[...] += _dot_tn(xb, dpab)
            dwx_ref[...] += _dot_tn(xb, dpxb)
            dxc_buf[pl.ds(t0, r), :] = dxc
            small_ref[4:5, :] += jnp.sum(dxc, axis=0, keepdims=True)
            dwin = dxc_buf[pl.ds(t0, r + 8), :]
            dx_ref[pl.ds(t0, r), :] = (w4v[3:4] * dxc + w4v[2:3] * _shift_up(dwin, 1, r)
                                       + w4v[1:2] * _shift_up(dwin, 2, r) + w4v[0:1] * _shift_up(dwin, 3, r)).astype(BF)
            xwin = _window_dn(x_ref, t0, r, first)
            for k in range(4):
                small_ref[k:k + 1, :] += jnp.sum(dxc * _shift_dn(xwin, 3 - k, r), axis=0, keepdims=True)
            return vnext

        def bloop(it, vnext):
            ci = MIX_CHUNKS - 1 - it
            return bwd_chunk(pl.multiple_of(ci * r, 16), vnext, False)

        vn = lax.fori_loop(0, MIX_CHUNKS - 1, bloop, jnp.zeros((8, c), F32))
        bwd_chunk(0, vn, True)
        small_ref[7:8, :] = small_ref[7:8, :] * (-_sigmoid(-lamv))

    col = lambda off: pl.BlockSpec((tp, c), lambda j: (0, off + j))
    vec = pl.BlockSpec((1, c), lambda j: (0, j))
    mat = pl.BlockSpec((None, c, c), lambda j: (j, 0, 0))
    buf = pltpu.VMEM((tp, c), F32)
    bufp = pltpu.VMEM((tp + 8, c), F32)
    return _grid_call(
        body, "lru_bwd", (nb,),
        [col(0), col(nb), col(0), col(0), pl.BlockSpec((8, c), lambda j: (0, j)), vec, mat, vec, mat, vec, vec, vec,
         pl.BlockSpec((c, c), lambda j: (0, 0))],
        [col(0), col(0), pl.BlockSpec((16, c), lambda j: (0, j)), mat, mat],
        [jax.ShapeDtypeStruct((tp, dl), BF), jax.ShapeDtypeStruct((tp, dl), BF), jax.ShapeDtypeStruct((16, dl), F32),
         jax.ShapeDtypeStruct((nb, c, c), F32), jax.ShapeDtypeStruct((nb, c, c), F32)],
        (z, z, hs, dmix, w4, cb, wa2, ba, wx2, bx, lam, g_out, gm), side, [buf, buf, buf, bufp, buf, bufp])


def _sc_conv(cvwin, w3, r):
    return w3[2:3] * _shift_dn(cvwin, 0, r) + w3[1:2] * _shift_dn(cvwin, 1, r) + w3[0:1] * _shift_dn(cvwin, 2, r)


def _sc_fwd(z, w3, g_out, gm, dl, side=None):
    tp = z.shape[0]
    nb = dl // LANE
    r = tp // MIX_CHUNKS
    c = LANE

    def body(b_ref, c_ref, v_ref, w3_ref, go_ref, gm_ref, m_ref):
        w3v = w3_ref[...]
        gov = go_ref[...]
        gmv = gm_ref[...]

        def chunk(t0, first):
            cvwin = _window_dn(c_ref, t0, r, first) * _window_dn(v_ref, t0, r, first)
            so = b_ref[pl.ds(t0, r), :] * _sc_conv(cvwin, w3v, r)
            rs = lax.rsqrt(_group_mean(so * so, gmv) + EPS)
            m_ref[pl.ds(t0, r), :] = (so * rs * gov).astype(BF)

        chunk(0, True)

        def loop(ci, carry):
            chunk(pl.multiple_of(ci * r, 16), False)
            return carry

        lax.fori_loop(1, MIX_CHUNKS, loop, 0)

    col = lambda off: pl.BlockSpec((tp, c), lambda j: (0, off + j))
    res, extra = _grid_call(
        body, "sconv_fwd", (nb,),
        [col(2 * nb), col(3 * nb), col(4 * nb), pl.BlockSpec((8, c), lambda j: (0, j)),
         pl.BlockSpec((1, c), lambda j: (0, j)), pl.BlockSpec((c, c), lambda j: (0, 0))],
        [col(0)], [jax.ShapeDtypeStruct((tp, dl), BF)], (z, z, z, w3, g_out, gm), side)
    return res[0], extra


def _sc_bwd(z, dmix, w3, g_out, gm, dl, side=None):
    tp = z.shape[0]
    nb = dl // LANE
    r = tp // MIX_CHUNKS
    c = LANE

    def body(b_ref, c_ref, v_ref, dm_ref, w3_ref, go_ref, gm_ref, db_ref, dc_ref, dv_ref, small_ref, dsc_buf):
        w3v = w3_ref[...]
        gov = go_ref[...]
        gmv = gm_ref[...]
        small_ref[...] = jnp.zeros_like(small_ref)
        dsc_buf[pl.ds(tp, 8), :] = jnp.zeros((8, c), F32)

        def chunk1(t0, first):
            cvwin = _window_dn(c_ref, t0, r, first) * _window_dn(v_ref, t0, r, first)
            sc = _sc_conv(cvwin, w3v, r)
            bv = b_ref[pl.ds(t0, r), :]
            so = bv * sc
            rs = lax.rsqrt(_group_mean(so * so, gmv) + EPS)
            xh = so * rs
            dm = dm_ref[pl.ds(t0, r), :]
            q = dm * gov
            dso = rs * (q - xh * _group_mean(q * xh, gmv))
            small_ref[3:4, :] += jnp.sum(dm * xh, axis=0, keepdims=True)
            db_ref[pl.ds(t0, r), :] = (dso * sc).astype(BF)
            dsc = dso * bv
            dsc_buf[pl.ds(t0, r), :] = dsc
            for k in range(3):
                small_ref[k:k + 1, :] += jnp.sum(dsc * _shift_dn(cvwin, 2 - k, r), axis=0, keepdims=True)

        chunk1(0, True)

        def loop1(ci, carry):
            chunk1(pl.multiple_of(ci * r, 16), False)
            return carry

        lax.fori_loop(1, MIX_CHUNKS, loop1, 0)

        def loop2(ci, carry):
            t0 = pl.multiple_of(ci * r, 16)
            dwin = dsc_buf[pl.ds(t0, r + 8), :]
            dcv = w3v[2:3] * _shift_up(dwin, 0, r) + w3v[1:2] * _shift_up(dwin, 1, r) + w3v[0:1] * _shift_up(dwin, 2, r)
            dc_ref[pl.ds(t0, r), :] = (dcv * v_ref[pl.ds(t0, r), :]).astype(BF)
            dv_ref[pl.ds(t0, r), :] = (dcv * c_ref[pl.ds(t0, r), :]).astype(BF)
            return carry

        lax.fori_loop(0, MIX_CHUNKS, loop2, 0)

    col = lambda off: pl.BlockSpec((tp, c), lambda j: (0, off + j))
    out = jax.ShapeDtypeStruct((tp, dl), BF)
    return _grid_call(
        body, "sconv_bwd", (nb,),
        [col(2 * nb), col(3 * nb), col(4 * nb), col(nb), pl.BlockSpec((8, c), lambda j: (0, j)),
         pl.BlockSpec((1, c), lambda j: (0, j)), pl.BlockSpec((c, c), lambda j: (0, 0))],
        [col(0), col(0), col(0), pl.BlockSpec((8, c), lambda j: (0, j))],
        [out, out, out, jax.ShapeDtypeStruct((8, dl), F32)], (z, z, z, dmix, w3, g_out, gm), side,
        [pltpu.VMEM((tp + 8, c), F32)])


def _cast_pad(w, rows_p, cols_p, chip, name):
    r, c = w.shape

    def body(chip_ref, w_ref, o_ref):
        if (rows_p, cols_p) != (r, c):
            o_ref[...] = jnp.zeros_like(o_ref)
        o_ref[0:r, 0:c] = w_ref[...].astype(BF)

    return pl.pallas_call(
        body, name=name, out_shape=jax.ShapeDtypeStruct((N_CHIP, rows_p, cols_p), BF),
        grid_spec=pltpu.PrefetchScalarGridSpec(
            num_scalar_prefetch=1, grid=(1,),
            in_specs=[pl.BlockSpec((r, c), lambda i, chip: (0, 0))],
            out_specs=pl.BlockSpec((None, rows_p, cols_p), lambda i, chip: (chip[0], 0, 0))),
        compiler_params=_params(("arbitrary",)),
    )(chip, w)


def _adamw_math(w, g, m, v):
    m2 = ADAM_B1 * m + (1.0 - ADAM_B1) * g
    v2 = ADAM_B2 * v + (1.0 - ADAM_B2) * (g * g)
    m_hat = m2 / (1.0 - ADAM_B1 ** ADAM_STEP)
    v_hat = v2 / (1.0 - ADAM_B2 ** ADAM_STEP)
    delta = -ADAM_LR * (m_hat / (jnp.sqrt(v_hat) + ADAM_EPS) + ADAM_WD * w)
    return delta, m2, v2


def _adamw(w, g, m, v, name, row_tiles, col_tiles, side=None):
    r, c = w.shape
    tr = r // row_tiles
    tc = c // col_tiles
    gc = g.shape[1] if col_tiles == 1 else tc

    def body(w_ref, g_ref, m_ref, v_ref, go_ref, d_ref, mo_ref, vo_ref):
        gv = g_ref[...][:, 0:tc]
        delta, m2, v2 = _adamw_math(w_ref[...], gv, m_ref[...], v_ref[...])
        go_ref[...] = gv
        d_ref[...] = delta
        mo_ref[...] = m2
        vo_ref[...] = v2

    spec = pl.BlockSpec((tr, tc), lambda i, j: (i, j))
    out = jax.ShapeDtypeStruct((r, c), F32)
    return _grid_call(body, name, (row_tiles, col_tiles), [spec, pl.BlockSpec((tr, gc), lambda i, j: (i, j)), spec, spec],
                      [spec] * 4, [out] * 4, (w, g, m, v), side)


def _adamw_small(w, g_top, g4, m, v):
    def body(w_ref, gt_ref, g_ref, m_ref, v_ref, go_ref, d_ref, mo_ref, vo_ref):
        g = jnp.concatenate([gt_ref[...], (g_ref[0] + g_ref[1]) + (g_ref[2] + g_ref[3])], axis=0)
        delta, m2, v2 = _adamw_math(w_ref[...], g, m_ref[...], v_ref[...])
        go_ref[...] = g
        d_ref[...] = delta
        mo_ref[...] = m2
        vo_ref[...] = v2

    out = jax.ShapeDtypeStruct(w.shape, F32)
    spec = pl.BlockSpec(w.shape, lambda: (0, 0))
    return pl.pallas_call(
        body, name="adamw_small",
        in_specs=[spec, pl.BlockSpec(g_top.shape, lambda: (0, 0)), pl.BlockSpec(g4.shape, lambda: (0, 0, 0)), spec, spec],
        out_specs=[spec] * 4, out_shape=[out] * 4, compiler_params=_params())(w, g_top, g4, m, v)


def _place():
    x, y, c = lax.axis_index("x"), lax.axis_index("y"), lax.axis_index("c")
    chips = [(1 - x, y), (x, 1 - y), (1 - x, 1 - y)]
    return x, y, c, chips


ANY = pl.BlockSpec(memory_space=pl.ANY)


REL_SLOT = (2, 1, 3)


def _gather_side(bufs, relative=False, two_path=False):
    n = len(bufs)
    direct = (0, 1) if two_path else (0, 1, 2)

    def copies(outs, sems):
        s_ici, r_ici, s_d2d, r_d2d = sems[:4]
        x, y, c, chips = _place()
        me = 2 * x + y

        def rows(w, slot, core, part=None):
            half = bufs[w].shape[1] // 2
            if part is None:
                return outs[w].at[slot, pl.ds(core * half, half)]
            return outs[w].at[slot, pl.ds(core * half + part * (half // 2), half // 2)]

        def theirs(j):
            return REL_SLOT[j] if relative else 2 * chips[j][0] + chips[j][1]

        def ici_send(w, j):
            px, py = chips[j]
            return pltpu.make_async_remote_copy(
                src_ref=rows(w, 0 if relative else me, c), dst_ref=rows(w, REL_SLOT[j] if relative else me, c),
                send_sem=s_ici.at[w, j], recv_sem=r_ici.at[w, j], device_id=(px, py, c), device_id_type=MESH)

        def ici_recv(w, j):
            px, py = chips[j]
            return pltpu.make_async_remote_copy(
                src_ref=rows(w, theirs(j), c), dst_ref=rows(w, theirs(j), c),
                send_sem=s_ici.at[w, j], recv_sem=r_ici.at[w, j], device_id=(px, py, c), device_id_type=MESH)

        def hop_send(w, p):
            px, py = chips[1 - p]
            return pltpu.make_async_remote_copy(
                src_ref=rows(w, theirs(p), c, p), dst_ref=rows(w, REL_SLOT[2] if relative else theirs(p), c, p),
                send_sem=sems[4].at[w, p], recv_sem=sems[5].at[w, p], device_id=(px, py, c), device_id_type=MESH)

        def hop_recv(w, p):
            px, py = chips[1 - p]
            return pltpu.make_async_remote_copy(
                src_ref=rows(w, theirs(2), c, p), dst_ref=rows(w, theirs(2), c, p),
                send_sem=sems[4].at[w, p], recv_sem=sems[5].at[w, p], device_id=(px, py, c), device_id_type=MESH)

        def d2d(w, j, core):
            return pltpu.make_async_remote_copy(
                src_ref=rows(w, theirs(j), core), dst_ref=rows(w, theirs(j), core),
                send_sem=s_d2d.at[w, j], recv_sem=r_d2d.at[w, j], device_id=(x, y, 1 - c), device_id_type=MESH)

        return c, ici_send, ici_recv, hop_send, hop_recv, d2d

    def send(outs, sems):
        c, ici_send, ici_recv, hop_send, hop_recv, d2d = copies(outs, sems)
        for j in direct:
            for w in range(n):
                ici_send(w, j).start()

    def arrived(j, outs, sems):
        c, ici_send, ici_recv, hop_send, hop_recv, d2d = copies(outs, sems)
        for w in range(n):
            if j in direct:
                ici_recv(w, j).wait_recv()
                if two_path:
                    hop_send(w, j).start()
            else:
                hop_recv(w, 0).wait_recv()
                hop_recv(w, 1).wait_recv()
            d2d(w, j, c).start()

    def forwarded(j, outs, sems):
        c, ici_send, ici_recv, hop_send, hop_recv, d2d = copies(outs, sems)
        for w in range(n):
            d2d(w, j, 1 - c).wait_recv()

    def drain(outs, sems):
        c, ici_send, ici_recv, hop_send, hop_recv, d2d = copies(outs, sems)
        for w in range(n):
            for j in direct:
                ici_send(w, j).wait_send()
                if two_path:
                    hop_send(w, j).wait_send()
            for j in range(3):
                d2d(w, j, c).wait_send()

    def start(ins, outs, sems):
        send(outs, sems)

    def middle(ins, outs, sems):
        arrived(0, outs, sems)
        arrived(1, outs, sems)

    def rest(ins, outs, sems):
        arrived(2, outs, sems)
        for j in range(3):
            forwarded(j, outs, sems)
        drain(outs, sems)

    def finish(ins, outs, sems):
        middle(ins, outs, sems)
        rest(ins, outs, sems)

    dma = pltpu.SemaphoreType.DMA((n, 3))
    hop = [pltpu.SemaphoreType.DMA((n, 2))] * 2 if two_path else []
    side = _Side(list(bufs), [jax.ShapeDtypeStruct(b.shape, b.dtype) for b in bufs], {w: w for w in range(n)},
                 [dma, dma, dma, dma] + hop, start, finish)
    side.send, side.arrived, side.forwarded, side.drain = send, arrived, forwarded, drain
    side.middle, side.rest = middle, rest
    return side


def _run_side(side, name):
    sin, sout = len(side.ins), len(side.outs)

    def body(*refs):
        ins, outs, sems = refs[:sin], refs[sin:sin + sout], refs[sin + sout:]
        side.start(ins, outs, sems)
        side.finish(ins, outs, sems)

    return pl.pallas_call(
        body, name=name, out_shape=list(side.outs), in_specs=[ANY] * sin, out_specs=[ANY] * sout,
        scratch_shapes=list(side.sems), input_output_aliases=dict(side.alias))(*side.ins)


def _pair_exchange_side(grads):
    n = len(grads)

    def copies(ins, outs, sems):
        ssem, rsem = sems
        x, y, c, _ = _place()
        cps = []
        for w in range(n):
            half = grads[w].shape[1] // 2
            cps.append(pltpu.make_async_remote_copy(
                src_ref=ins[w].at[:, pl.ds((1 - c) * half, half)], dst_ref=outs[w],
                send_sem=ssem.at[w], recv_sem=rsem.at[w], device_id=(x, y, 1 - c), device_id_type=MESH))
        return cps

    def start(ins, outs, sems):
        for cp in copies(ins, outs, sems):
            cp.start()

    def finish(ins, outs, sems):
        for cp in copies(ins, outs, sems):
            cp.wait()

    dma = pltpu.SemaphoreType.DMA((n,))
    return _Side(list(grads), [jax.ShapeDtypeStruct((N_CHIP, g.shape[1] // 2, g.shape[2]), BF) for g in grads], {},
                 [dma, dma], start, finish)


def _sibling_copy_side(buf):
    def copy(ins, outs, sems):
        x, y, c, _ = _place()
        return pltpu.make_async_remote_copy(src_ref=ins[0], dst_ref=outs[0], send_sem=sems[0], recv_sem=sems[1],
                                            device_id=(x, y, 1 - c), device_id_type=MESH)

    return _Side([buf], [jax.ShapeDtypeStruct(buf.shape, buf.dtype)], {}, [pltpu.SemaphoreType.DMA, pltpu.SemaphoreType.DMA],
                 lambda i, o, s: copy(i, o, s).start(), lambda i, o, s: copy(i, o, s).wait())


def _slot_exchange_side(buf4):
    def copies(outs, sems, sending):
        ssem, rsem = sems
        x, y, c, chips = _place()
        me = 2 * x + y
        return [pltpu.make_async_remote_copy(
            src_ref=outs[0].at[me if sending else 2 * px + py], dst_ref=outs[0].at[me if sending else 2 * px + py],
            send_sem=ssem.at[j], recv_sem=rsem.at[j], device_id=(px, py, c), device_id_type=MESH)
            for j, (px, py) in enumerate(chips)]

    def start(ins, outs, sems):
        for cp in copies(outs, sems, True):
            cp.start()

    def finish(ins, outs, sems):
        for cp in copies(outs, sems, False):
            cp.wait_recv()
        for cp in copies(outs, sems, True):
            cp.wait_send()

    dma = pltpu.SemaphoreType.DMA((3,))
    return _Side([buf4], [jax.ShapeDtypeStruct(buf4.shape, buf4.dtype)], {0: 0}, [dma, dma], start, finish)


def _pair_sum(g, sib, core, name):
    _, r, cdim = g.shape
    half = r // 2

    def body(core_ref, g_ref, s_ref, o_ref):
        o_ref[...] = (g_ref[...].astype(F32) + s_ref[...].astype(F32)).astype(BF)

    return pl.pallas_call(
        body, name=name,
        grid_spec=pltpu.PrefetchScalarGridSpec(
            num_scalar_prefetch=1, grid=(N_CHIP,),
            in_specs=[pl.BlockSpec((None, half, cdim), lambda k, core: (k, core[0], 0)),
                      pl.BlockSpec((None, half, cdim), lambda k, core: (k, 0, 0))],
            out_specs=pl.BlockSpec((None, half, cdim), lambda k, core: (k, 0, 0))),
        out_shape=jax.ShapeDtypeStruct((N_CHIP, half, cdim), BF),
        compiler_params=_params(("arbitrary",)),
    )(core, g, sib)


def _chip_exchange_side(psums, relative=False):
    n = len(psums)

    def copies(ins, outs, sems):
        ssem, rsem = sems
        x, y, c, chips = _place()
        return [pltpu.make_async_remote_copy(
            src_ref=ins[w].at[REL_SLOT[j] if relative else 2 * px + py], dst_ref=outs[w].at[j],
            send_sem=ssem.at[w, j], recv_sem=rsem.at[w, j], device_id=(px, py, c), device_id_type=MESH)
            for w in range(n) for j, (px, py) in enumerate(chips)]

    def start(ins, outs, sems):
        for cp in copies(ins, outs, sems):
            cp.start()

    def finish(ins, outs, sems):
        for cp in copies(ins, outs, sems):
            cp.wait()

    dma = pltpu.SemaphoreType.DMA((n, 3))
    return _Side(list(psums), [jax.ShapeDtypeStruct((3,) + p.shape[1:], BF) for p in psums], {}, [dma, dma],
                 start, finish)


def _final_sum(g, sib, recv, sel, name):
    _, r, cdim = g.shape
    half = r // 2
    nt = 4
    th = half // nt

    def body(sel_ref, g_ref, s_ref, r_ref, o_ref):
        acc = g_ref[...].astype(F32) + s_ref[...].astype(F32)
        for j in range(3):
            acc = acc + r_ref[j].astype(F32)
        o_ref[...] = acc

    return pl.pallas_call(
        body, name=name,
        grid_spec=pltpu.PrefetchScalarGridSpec(
            num_scalar_prefetch=1, grid=(nt,),
            in_specs=[pl.BlockSpec((None, th, cdim), lambda i, sel: (sel[0], sel[1] * nt + i, 0)),
                      pl.BlockSpec((None, th, cdim), lambda i, sel: (sel[0], i, 0)),
                      pl.BlockSpec((3, th, cdim), lambda i, sel: (0, i, 0))],
            out_specs=pl.BlockSpec((th, cdim), lambda i, sel: (sel[1] * nt + i, 0))),
        out_shape=jax.ShapeDtypeStruct((r, cdim), F32),
        compiler_params=_params(("arbitrary",)),
    )(sel, g, sib, recv)


def _join_side(bufs):
    n = len(bufs)

    def copies(outs, sems, core_of):
        ssem, rsem = sems
        x, y, c, _ = _place()
        cps = []
        for w in range(n):
            half = bufs[w].shape[0] // 2
            rows = outs[w].at[pl.ds(core_of(c) * half, half)]
            cps.append(pltpu.make_async_remote_copy(
                src_ref=rows, dst_ref=rows, send_sem=ssem.at[w], recv_sem=rsem.at[w],
                device_id=(x, y, 1 - c), device_id_type=MESH))
        return cps

    def start(ins, outs, sems):
        for cp in copies(outs, sems, lambda c: c):
            cp.start()

    def finish(ins, outs, sems):
        for cp in copies(outs, sems, lambda c: 1 - c):
            cp.wait_recv()
        for cp in copies(outs, sems, lambda c: c):
            cp.wait_send()

    dma = pltpu.SemaphoreType.DMA((n,))
    return _Side(list(bufs), [jax.ShapeDtypeStruct(b.shape, F32) for b in bufs], {w: w for w in range(n)}, [dma, dma],
                 start, finish)


def _small_pair_sum(buf, sib, chip):
    rows, d = buf.shape

    def body(chip_ref, a_ref, b_ref, o_ref):
        o_ref[...] = a_ref[...] + b_ref[...]

    return pl.pallas_call(
        body, name="small_pair_sum", out_shape=jax.ShapeDtypeStruct((N_CHIP, rows, d), F32),
        grid_spec=pltpu.PrefetchScalarGridSpec(
            num_scalar_prefetch=1, grid=(1,),
            in_specs=[pl.BlockSpec((rows, d), lambda i, chip: (0, 0))] * 2,
            out_specs=pl.BlockSpec((None, rows, d), lambda i, chip: (chip[0], 0, 0))),
        compiler_params=_params(("arbitrary",)),
    )(chip, buf, sib)


def _small_all_reduce(buf, name):
    rows, d = buf.shape

    def body(in_ref, out_ref, sib, all4, ssem, rsem, psem, qsem):
        x, y, c, chips = _place()
        me = 2 * x + y
        to_sib = pltpu.make_async_remote_copy(src_ref=in_ref, dst_ref=sib, send_sem=ssem, recv_sem=rsem,
                                              device_id=(x, y, 1 - c), device_id_type=MESH)
        to_sib.start()
        to_sib.wait()
        all4[me] = in_ref[...] + sib[...]
        cps = [pltpu.make_async_remote_copy(src_ref=all4.at[me], dst_ref=all4.at[me], send_sem=psem.at[j],
                                            recv_sem=qsem.at[j], device_id=(px, py, c), device_id_type=MESH)
               for j, (px, py) in enumerate(chips)]
        for cp in cps:
            cp.start()
        for j, (px, py) in enumerate(chips):
            chip = 2 * px + py
            pltpu.make_async_remote_copy(src_ref=all4.at[chip], dst_ref=all4.at[chip], send_sem=psem.at[j],
                                         recv_sem=qsem.at[j], device_id=(px, py, c), device_id_type=MESH).wait_recv()
        for cp in cps:
            cp.wait_send()
        out_ref[...] = (all4[0] + all4[1]) + (all4[2] + all4[3])

    vm = pl.BlockSpec(memory_space=pltpu.VMEM)
    return pl.pallas_call(
        body, name=name, out_shape=jax.ShapeDtypeStruct((rows, d), F32),
        in_specs=[vm], out_specs=vm,
        scratch_shapes=[pltpu.VMEM((rows, d), F32), pltpu.VMEM((N_CHIP, rows, d), F32),
                        pltpu.SemaphoreType.DMA, pltpu.SemaphoreType.DMA,
                        pltpu.SemaphoreType.DMA((3,)), pltpu.SemaphoreType.DMA((3,))],
        compiler_params=_params(),
    )(buf)


def _pair_blocks(w):
    w4 = w.reshape(N_HEADS // 2, 2, HEAD, HEAD)
    eye = jnp.eye(2, dtype=w.dtype)
    return jnp.einsum("pirc,ij->pirjc", w4, eye).reshape(N_HEADS // 2, LANE, LANE)


def _unpair_blocks(w2):
    w5 = w2.reshape(N_HEADS // 2, 2, HEAD, 2, HEAD)
    return jnp.stack([w5[:, 0, :, 0, :], w5[:, 1, :, 1, :]], axis=1).reshape(N_HEADS, HEAD, HEAD)


def kernel(x, meta_tokens, ffn1_pre_g, ffn1_w_gate, ffn1_w_up, ffn1_w_down, ffn1_post_g, mix_pre_g, w_in, lru_conv_w, lru_conv_b, lru_w_a, lru_b_a, lru_w_x, lru_b_x, lru_lambda, sconv_w, lru_out_g, sconv_out_g, w_out, mix_post_g, ffn2_pre_g, ffn2_w_gate, ffn2_w_up, ffn2_w_down, ffn2_post_g, loss_target, m_meta_tokens, m_ffn1_pre_g, m_ffn1_w_gate, m_ffn1_w_up, m_ffn1_w_down, m_ffn1_post_g, m_mix_pre_g, m_w_in, m_lru_conv_w, m_lru_conv_b, m_lru_w_a, m_lru_b_a, m_lru_w_x, m_lru_b_x, m_lru_lambda, m_sconv_w, m_lru_out_g, m_sconv_out_g, m_w_out, m_mix_post_g, m_ffn2_pre_g, m_ffn2_w_gate, m_ffn2_w_up, m_ffn2_w_down, m_ffn2_post_g, v_meta_tokens, v_ffn1_pre_g, v_ffn1_w_gate, v_ffn1_w_up, v_ffn1_w_down, v_ffn1_post_g, v_mix_pre_g, v_w_in, v_lru_conv_w, v_lru_conv_b, v_lru_w_a, v_lru_b_a, v_lru_w_x, v_lru_b_x, v_lru_lambda, v_sconv_w, v_lru_out_g, v_sconv_out_g, v_w_out, v_mix_post_g, v_ffn2_pre_g, v_ffn2_w_gate, v_ffn2_w_up, v_ffn2_w_down, v_ffn2_post_g):
    seq, d = x.shape[1], x.shape[2]
    t_real = N_META + seq
    tp = _round_up(t_real, ROW_ALIGN)
    f4 = ffn1_w_gate.shape[2]
    f4p = _round_up(f4, LANE)
    dl = lru_conv_b.shape[1]
    cin = w_in.shape[2]
    xi, yi, ci = lax.axis_index("x"), lax.axis_index("y"), lax.axis_index("c")
    chip = 2 * xi + yi
    zero = jnp.zeros((), jnp.int32)

    transposed = ("ffn1_w_gate", "ffn1_w_up", "ffn2_w_gate", "ffn2_w_up")

    def view(k, a):
        return a[0].T if k in transposed else a[0]

    def unview(k, a):
        return (a.T if k in transposed else a)[None]

    big = {
        "ffn1_w_gate": (view("ffn1_w_gate", ffn1_w_gate), f4p, d), "ffn1_w_up": (view("ffn1_w_up", ffn1_w_up), f4p, d),
        "ffn1_w_down": (ffn1_w_down[0], f4p, d), "w_in": (w_in[0], d, cin), "w_out": (w_out[0], w_out.shape[1], d),
        "ffn2_w_gate": (view("ffn2_w_gate", ffn2_w_gate), f4p, d), "ffn2_w_up": (view("ffn2_w_up", ffn2_w_up), f4p, d),
        "ffn2_w_down": (ffn2_w_down[0], f4p, d),
    }
    names = list(big)
    chip1 = jnp.reshape(chip, (1,)).astype(jnp.int32)
    relative = {k: k.startswith("ffn") for k in names}
    slot0 = jnp.zeros((1,), jnp.int32)
    shard = {k: _cast_pad(big[k][0], big[k][1], big[k][2], slot0 if relative[k] else chip1, "cast_" + k) for k in names}
    full = {}

    def gather(*keys):
        return _merge_sides([_gather_side([shard[k]], relative[k], two_path=True) for k in keys])

    gm = jnp.kron(jnp.eye(2, dtype=F32), jnp.full((HEAD, HEAD), 1.0 / HEAD, F32)).astype(BF)
    wa2 = _pair_blocks(lru_w_a[0])
    wx2 = _pair_blocks(lru_w_x[0])

    dlq = dl // N_CHIP
    dq = d // N_CHIP
    R_GAIN, R_LOSS, R_META, R_LRU, R_SC, R_WA = 0, 6, 8, 24, 40, 48
    n_wrows = (N_HEADS // 2) * LANE * LANE // d
    R_WX = R_WA + n_wrows
    R_END = R_WX + n_wrows

    def pack_top(gains, meta, loss=None):
        lossrow = jnp.zeros((2, d), F32)
        if loss is not None:
            lossrow = lossrow.at[0, 0].set(loss)
        return jnp.concatenate([jnp.concatenate(gains, axis=0), lossrow, meta], axis=0)

    def pack_rest(lru16, sc8, wa_, wx_):
        return jnp.concatenate([jnp.concatenate([lru16, jnp.zeros((16, d - dl), F32)], axis=1),
                                jnp.concatenate([sc8, jnp.zeros((8, d - dl), F32)], axis=1),
                                wa_.reshape(n_wrows, d), wx_.reshape(n_wrows, d)], axis=0)

    def pack(gains, meta, lru16, sc8, wa_, wx_):
        return jnp.concatenate([pack_top(gains, meta), pack_rest(lru16, sc8, wa_, wx_)], axis=0)

    def place_cols(blk, width, total):
        return lax.dynamic_update_slice(jnp.zeros((blk.shape[0], total), F32), blk, (zero, chip * width))

    def pack_params(meta_, g1pre, g1post, gmpre, gmpost, g2pre, g2post, cw, cbias, wa_, ba_, wx_, bx_, lam_, sw, lgo, sgo):
        lru16 = jnp.concatenate([place_cols(cw[0], dlq, dl), cbias, ba_, bx_, lam_, lgo, jnp.zeros((7, dl), F32)], axis=0)
        sc8 = jnp.concatenate([place_cols(sw[0], dlq, dl), sgo, jnp.zeros((4, dl), F32)], axis=0)
        return pack([g1pre, g1post, gmpre, gmpost, g2pre, g2post], place_cols(meta_, dq, d), lru16, sc8,
                    _pair_blocks(wa_[0]), _pair_blocks(wx_[0]))

    p_w = pack_params(meta_tokens, ffn1_pre_g, ffn1_post_g, mix_pre_g, mix_post_g, ffn2_pre_g, ffn2_post_g, lru_conv_w,
                      lru_conv_b, lru_w_a, lru_b_a, lru_w_x, lru_b_x, lru_lambda, sconv_w, lru_out_g, sconv_out_g)
    p_m = pack_params(m_meta_tokens, m_ffn1_pre_g, m_ffn1_post_g, m_mix_pre_g, m_mix_post_g, m_ffn2_pre_g, m_ffn2_post_g,
                      m_lru_conv_w, m_lru_conv_b, m_lru_w_a, m_lru_b_a, m_lru_w_x, m_lru_b_x, m_lru_lambda, m_sconv_w,
                      m_lru_out_g, m_sconv_out_g)
    p_v = pack_params(v_meta_tokens, v_ffn1_pre_g, v_ffn1_post_g, v_mix_pre_g, v_mix_post_g, v_ffn2_pre_g, v_ffn2_post_g,
                      v_lru_conv_w, v_lru_conv_b, v_lru_w_a, v_lru_b_a, v_lru_w_x, v_lru_b_x, v_lru_lambda, v_sconv_w,
                      v_lru_out_g, v_sconv_out_g)

    gathered = _small_all_reduce(jnp.where(ci == 0, p_w, 0.0)[R_META:R_WA], "small_weight_gather")
    meta_full = gathered[0:N_META]
    w4_full = gathered[R_LRU - R_META:R_LRU - R_META + 4, 0:dl]
    w3_full = gathered[R_SC - R_META:R_SC - R_META + 3, 0:dl]
    w4p = jnp.concatenate([w4_full, jnp.zeros((4, dl), F32)], axis=0)
    w3p = jnp.concatenate([w3_full, jnp.zeros((5, dl), F32)], axis=0)

    h0 = jnp.concatenate([meta_full, x[0], jnp.zeros((tp - t_real, d), F32)], axis=0)
    tgt = jnp.concatenate([jnp.zeros((N_META, d), F32), loss_target[0], jnp.zeros((tp - t_real, d), F32)], axis=0)

    n1 = _norm0(h0, ffn1_pre_g)
    (a1, b1, s1), (full["ffn1_w_gate"], full["ffn1_w_up"]), got = _ffn_up_head(
        n1, shard["ffn1_w_gate"], shard["ffn1_w_up"], "ffn1_up",
        _gather_side([shard["ffn1_w_down"]], relative=True, two_path=True))
    full["ffn1_w_down"] = got[0]
    f1, got = _wide_matmul([(s1, full["ffn1_w_down"])], "ffn1_down", WIDE_TN, gather("w_in"))
    full["w_in"] = got[0]
    h1, u = _post_fwd(f1, h0, ffn1_post_g, mix_pre_g, 0.5, "ffn1_post")
    z, got = _col_matmul(u, full["w_in"], "in_proj", False, F32, gather("ffn2_w_gate"))
    full["ffn2_w_gate"] = got[0]
    (m_lru, hs), got = _lru_fwd(z, w4p, lru_conv_b, wa2.astype(BF), lru_b_a, wx2.astype(BF), lru_b_x, lru_lambda,
                                lru_out_g, gm, gather("ffn2_w_up"))
    full["ffn2_w_up"] = got[0]
    m_sc, got = _sc_fwd(z, w3p, sconv_out_g, gm, dl, gather("w_out"))
    full["w_out"] = got[0]
    mixed = jnp.concatenate([m_lru, m_sc], axis=1)
    p, _ = _wide_matmul([(mixed, full["w_out"])], "out_proj", WIDE_TN)
    h2, n2 = _post_fwd(p, h1, mix_post_g, ffn2_pre_g, 1.0, "mix_post")
    (a2, b2, s2), got = _ffn_up(n2, full["ffn2_w_gate"], full["ffn2_w_up"], "ffn2_up", gather("ffn2_w_down"))
    full["ffn2_w_down"] = got[0]
    f2, _ = _wide_matmul([(s2, full["ffn2_w_down"])], "ffn2_down", WIDE_TN)
    dh3, df2, dg_ffn2_post, loss_part = _loss_bwd(f2, h2, tgt, ffn2_post_g, t_real)

    core = jnp.reshape(ci, (1,)).astype(jnp.int32)
    sel_of = {False: jnp.stack([chip, ci]).astype(jnp.int32), True: jnp.stack([0 * chip, ci]).astype(jnp.int32)}
    red = {}

    def pair_side(k):
        return _pair_exchange_side([red[k][0]])

    def chip_side(k):
        return _chip_exchange_side([_pair_sum(red[k][0], red[k][1], core, "pair_sum_" + k)], relative[k])

    def final_sum(k):
        return _final_sum(*red[k], sel_of[relative[k]], "final_sum_" + k)

    (da2, db2), _ = _ffn_bwd_act(df2, full["ffn2_w_down"], a2, b2, "ffn2_bwd_act")
    g, _ = _wgrad_call(s2, df2, "ffn2_down_wgrad", x_width=f4p, tile_y=WGRAD_TILE_Y)
    red["ffn2_w_down"] = [g, None, None]
    g, got = _wgrad_call(da2, n2, "ffn2_gate_wgrad", x_width=f4p, tile_y=WGRAD_TILE_Y, side=pair_side("ffn2_w_down"))
    red["ffn2_w_down"][1] = got[0]
    red["ffn2_w_gate"] = [g, None, None]
    g, got = _wgrad_call(db2, n2, "ffn2_up_wgrad", x_width=f4p, tile_y=WGRAD_TILE_Y,
                         side=_merge_sides([pair_side("ffn2_w_gate"), chip_side("ffn2_w_down")]))
    red["ffn2_w_gate"][1], red["ffn2_w_down"][2] = got
    red["ffn2_w_up"] = [g, None, None]
    dn2, got = _row_matmul([(da2, full["ffn2_w_gate"]), (db2, full["ffn2_w_up"])], "ffn2_bwd_up", False, d,
                           _merge_sides([pair_side("ffn2_w_up"), chip_side("ffn2_w_gate")]), tiles=MM_TILES)
    red["ffn2_w_up"][1], red["ffn2_w_gate"][2] = got
    dh2, dp, dg_ffn2_pre, dg_mix_post = _pre_bwd(dn2, h2, dh3, ffn2_pre_g, "ffn2_pre_bwd", (p, mix_post_g, 1.0))
    dmixed, _ = _col_matmul(dp, full["w_out"], "out_proj_bwd", True, F32)
    g, _ = _wgrad_call(mixed, dp, "w_out_wgrad", x_width=mixed.shape[1] // N_CHIP, tile_y=WGRAD_TILE_Y)
    red["w_out"] = [g, None, None]
    (dzy, dzx, lru_small, dwa2, dwx2), got = _lru_bwd(
        z, hs, dmixed, w4p, lru_conv_b, wa2.astype(BF), lru_b_a, wx2.astype(BF), lru_b_x, lru_lambda, lru_out_g, gm,
        _merge_sides([pair_side("w_out"), chip_side("ffn2_w_up")]))
    red["w_out"][1], red["ffn2_w_up"][2] = got
    (dzb, dzc, dzv, sc_small), got = _sc_bwd(z, dmixed, w3p, sconv_out_g, gm, dl, chip_side("w_out"))
    red["w_out"][2] = got[0]
    dz = jnp.concatenate([dzy, dzx, dzb, dzc, dzv], axis=1)
    p_rest = pack_rest(lru_small, sc_small, dwa2, dwx2)
    g, got = _wgrad_call(u, dz, "w_in_wgrad", y_width=cin, tile_x=WGRAD_TILE_X, side=_sibling_copy_side(p_rest))
    p_rest4 = _small_pair_sum(p_rest, got[0], chip1)
    red["w_in"] = [g, None, None]
    du, got = _row_matmul([(dz, full["w_in"])], "in_proj_bwd", True, d,
                          _merge_sides([pair_side("w_in"), _slot_exchange_side(p_rest4)]))
    red["w_in"][1], p_rest4 = got
    dh1, df1, dg_mix_pre, dg_ffn1_post = _pre_bwd(du, h1, dh2, mix_pre_g, "mix_pre_bwd", (f1, ffn1_post_g, 0.5))
    (da1, db1), got = _ffn_bwd_act(df1, full["ffn1_w_down"], a1, b1, "ffn1_bwd_act", chip_side("w_in"))
    red["w_in"][2] = got[0]
    early = ["ffn2_w_down", "ffn2_w_gate", "ffn2_w_up", "w_out", "w_in"]
    late = ["ffn1_w_down", "ffn1_w_gate", "ffn1_w_up"]
    g, got = _wgrad_call(s1, df1, "ffn1_down_wgrad", x_width=f4p, tile_y=WGRAD_TILE_Y,
                         side=_join_side([final_sum(k) for k in early]))
    gfull = dict(zip(early, got))
    red["ffn1_w_down"] = [g, None, None]
    g, got = _wgrad_call(da1, n1, "ffn1_gate_wgrad", x_width=f4p, tile_y=WGRAD_TILE_Y, side=pair_side("ffn1_w_down"))
    red["ffn1_w_down"][1] = got[0]
    red["ffn1_w_gate"] = [g, None, None]
    g, got = _wgrad_call(db1, n1, "ffn1_up_wgrad", x_width=f4p, tile_y=WGRAD_TILE_Y,
                         side=_merge_sides([pair_side("ffn1_w_gate"), chip_side("ffn1_w_down")]))
    red["ffn1_w_gate"][1], red["ffn1_w_down"][2] = got
    red["ffn1_w_up"] = [g, None, None]
    red["ffn1_w_up"][1] = _run_side(pair_side("ffn1_w_up"), "pair_exchange_ffn1_w_up")[0]
    dn1, got = _row_matmul([(da1, full["ffn1_w_gate"]), (db1, full["ffn1_w_up"])], "ffn1_bwd_up", False, d,
                           _merge_sides([chip_side("ffn1_w_gate"), chip_side("ffn1_w_up")]), tiles=MM_TILES)
    red["ffn1_w_gate"][2], red["ffn1_w_up"][2] = got
    (dh0, dg_ffn1_pre), got = _pre_bwd(dn1, h0, dh1, ffn1_pre_g, "ffn1_pre_bwd",
                                       side=_join_side([final_sum(k) for k in late]))
    gfull.update(zip(late, got))

    grad_x = dh0[N_META:t_real][None]

    w_big = {"ffn1_w_gate": ffn1_w_gate, "ffn1_w_up": ffn1_w_up, "ffn1_w_down": ffn1_w_down, "w_in": w_in, "w_out": w_out,
             "ffn2_w_gate": ffn2_w_gate, "ffn2_w_up": ffn2_w_up, "ffn2_w_down": ffn2_w_down}
    m_big = {"ffn1_w_gate": m_ffn1_w_gate, "ffn1_w_up": m_ffn1_w_up, "ffn1_w_down": m_ffn1_w_down, "w_in": m_w_in,
             "w_out": m_w_out, "ffn2_w_gate": m_ffn2_w_gate, "ffn2_w_up": m_ffn2_w_up, "ffn2_w_down": m_ffn2_w_down}
    v_big = {"ffn1_w_gate": v_ffn1_w_gate, "ffn1_w_up": v_ffn1_w_up, "ffn1_w_down": v_ffn1_w_down, "w_in": v_w_in,
             "w_out": v_w_out, "ffn2_w_gate": v_ffn2_w_gate, "ffn2_w_up": v_ffn2_w_up, "ffn2_w_down": v_ffn2_w_down}
    b_grad, b_delta, b_newm, b_newv = {}, {}, {}, {}

    def big_adamw(k, side=None):
        wv, mv, vv = view(k, w_big[k]), view(k, m_big[k]), view(k, v_big[k])
        wide_rows = wv.shape[0] % 64 == 0
        (g_, d_, m_, v_), got = _adamw(wv, gfull[k], mv, vv, "adamw_" + k, 8 if wide_rows else 4, 1 if wide_rows else 2,
                                       side)
        b_grad[k], b_delta[k], b_newm[k], b_newv[k] = unview(k, g_), unview(k, d_), unview(k, m_), unview(k, v_)
        return got

    p_top = _small_all_reduce(
        pack_top([dg_ffn1_pre, dg_ffn1_post, dg_mix_pre, dg_mix_post, dg_ffn2_pre, dg_ffn2_post], dh0[0:N_META],
                 loss=loss_part[0, 0]), "small_grad_all_reduce")
    p_g, p_delta, p_newm, p_newv = _adamw_small(p_w, p_top, p_rest4, p_m, p_v)
    loss = p_g[R_LOSS, 0]
    for k in names:
        big_adamw(k)

    def unpack(buf):
        out = {}
        for i, k in enumerate(["ffn1_pre_g", "ffn1_post_g", "mix_pre_g", "mix_post_g", "ffn2_pre_g", "ffn2_post_g"]):
            out[k] = buf[R_GAIN + i:R_GAIN + i + 1]
        out["meta_tokens"] = lax.dynamic_slice(buf[R_META:R_META + N_META], (zero, chip * dq), (N_META, dq))
        lru = buf[R_LRU:R_LRU + 16, 0:dl]
        out["lru_conv_w"] = lax.dynamic_slice(lru[0:4], (zero, chip * dlq), (4, dlq))[None]
        out["lru_conv_b"] = lru[4:5]
        out["lru_b_a"] = lru[5:6]
        out["lru_b_x"] = lru[6:7]
        out["lru_lambda"] = lru[7:8]
        out["lru_out_g"] = lru[8:9]
        sc = buf[R_SC:R_SC + 8, 0:dl]
        out["sconv_w"] = lax.dynamic_slice(sc[0:3], (zero, chip * dlq), (3, dlq))[None]
        out["sconv_out_g"] = sc[3:4]
        out["lru_w_a"] = _unpair_blocks(buf[R_WA:R_WX].reshape(N_HEADS // 2, LANE, LANE))[None]
        out["lru_w_x"] = _unpair_blocks(buf[R_WX:R_END].reshape(N_HEADS // 2, LANE, LANE))[None]
        return out

    s_grad, s_delta, s_newm, s_newv = unpack(p_g), unpack(p_delta), unpack(p_newm), unpack(p_newv)

    order = ["meta_tokens", "ffn1_pre_g", "ffn1_w_gate", "ffn1_w_up", "ffn1_w_down", "ffn1_post_g", "mix_pre_g", "w_in",
             "lru_conv_w", "lru_conv_b", "lru_w_a", "lru_b_a", "lru_w_x", "lru_b_x", "lru_lambda", "sconv_w", "lru_out_g",
             "sconv_out_g", "w_out", "mix_post_g", "ffn2_pre_g", "ffn2_w_gate", "ffn2_w_up", "ffn2_w_down", "ffn2_post_g"]

    def pick(small, bigd):
        return [bigd[k] if k in bigd else small[k] for k in order]

    return (loss, grad_x, *pick(s_grad, b_grad), *pick(s_delta, b_delta), *pick(s_newm, b_newm), *pick(s_newv, b_newv))
```
